```python
import jax, jax.numpy as jnp
from jax import lax
import numpy as np

D_MODEL = 2048
BATCH = 8
SEQ = 4096
DEPTH = 2

N_MEM = 256
N_MIXERS = 2
N_A = (DEPTH + 1) // 2
N_B = DEPTH // 2
E_BRANCH = 2 * D_MODEL
E_CA = E_BRANCH // 4
E_MIX = E_BRANCH - E_CA
CA_HEADS = 4
CA_HEAD_DIM = E_CA // CA_HEADS
POOL_WINDOWS = (2, 4, 8, 16)
N_POOL_GROUPS = len(POOL_WINDOWS)
POOL_GROUP = E_MIX // N_POOL_GROUPS
HG_HEAD_DIM = 128
HG_HEADS = E_MIX // HG_HEAD_DIM
HG_CHUNK = 64
EPS = 1e-6

kernel_name = "interleaved_pool_hgrn2_memory_hybrid"


def rmsnorm(x, g):
    xf = x.astype(jnp.float32)
    y = xf * lax.rsqrt(jnp.mean(xf * xf, axis=-1, keepdims=True) + EPS)
    return (y * g.astype(jnp.float32)).astype(x.dtype)


def pool_mixer(u, w_grp, scale):
    B, S, _ = u.shape
    wmax = max(POOL_WINDOWS)
    uf = u.astype(jnp.float32)
    c = jnp.cumsum(uf, axis=1)
    c_pad = jnp.pad(c, ((0, 0), (wmax, 0), (0, 0)))
    pos = jnp.arange(S, dtype=jnp.float32)[:, None]
    pooled = []
    for j, w in enumerate(POOL_WINDOWS):
        lo, hi = j * POOL_GROUP, (j + 1) * POOL_GROUP
        win_sum = c[:, :, lo:hi] - c_pad[:, wmax - w: wmax - w + S, lo:hi]
        count = jnp.minimum(pos + 1.0, float(w))
        pooled.append(win_sum / count - uf[:, :, lo:hi])
    pooled = jnp.stack(pooled, axis=2).astype(u.dtype)
    y = jnp.einsum('bsgc,gcd->bsgd', pooled, w_grp)
    return y.reshape(B, S, E_MIX) * scale


def hgrn2_mixer(q_in, f_in, i_in, lb, norm_g):
    B, S, _ = q_in.shape
    N = S // HG_CHUNK
    f32 = jnp.float32

    def heads(t):
        return t.reshape(B, N, HG_CHUNK, HG_HEADS, HG_HEAD_DIM).transpose(0, 3, 1, 2, 4)

    q = heads(jax.nn.silu(q_in.astype(f32))) * (HG_HEAD_DIM ** -0.5)
    f = lb + (1.0 - lb) * jax.nn.sigmoid(f_in.astype(f32))
    k = heads(1.0 - f)
    b = jnp.cumsum(heads(jnp.log(f)), axis=3)
    v = heads(i_in.astype(f32))
    b_last = b[:, :, :, -1:, :]

    q_dec = q * jnp.exp(b)
    k_inv = k * jnp.exp(-b)
    k_to_end = k * jnp.exp(b_last - b)

    causal = jnp.tril(jnp.ones((HG_CHUNK, HG_CHUNK), dtype=bool))
    attn = jnp.einsum('bhnck,bhnsk->bhncs', q_dec, k_inv)
    attn = jnp.where(causal, attn, 0.0)
    o_intra = jnp.einsum('bhncs,bhnsv->bhncv', attn, v)

    def step(state, xs):
        qd, kte, vv, dec = xs
        o = jnp.einsum('bhck,bhkv->bhcv', qd, state)
        state = dec[..., None] * state + jnp.einsum('bhck,bhcv->bhkv', kte, vv)
        return state, o

    mv = lambda t: jnp.moveaxis(t, 2, 0)
    s0 = jnp.zeros((B, HG_HEADS, HG_HEAD_DIM, HG_HEAD_DIM), f32)
    _, o_inter = lax.scan(step, s0, (mv(q_dec), mv(k_to_end), mv(v), mv(jnp.exp(b_last[:, :, :, 0, :]))))
    o = o_intra + jnp.moveaxis(o_inter, 0, 2)
    o = o * lax.rsqrt(jnp.mean(o * o, axis=-1, keepdims=True) + EPS)
    o = o.transpose(0, 2, 3, 1, 4).reshape(B, S, E_MIX) * norm_g.astype(f32)
    return o.astype(q_in.dtype)


def memory_attention(q_in, mem_n, w_kv):
    B, S, _ = q_in.shape
    M = mem_n.shape[1]
    kv = jnp.einsum('bmd,de->bme', mem_n, w_kv)
    k, v = jnp.split(kv, 2, axis=-1)
    q = q_in.reshape(B, S, CA_HEADS, CA_HEAD_DIM)
    k = k.reshape(B, M, CA_HEADS, CA_HEAD_DIM)
    v = v.reshape(B, M, CA_HEADS, CA_HEAD_DIM)
    s = jnp.einsum('bshd,bmhd->bhsm', q, k).astype(jnp.float32) * (CA_HEAD_DIM ** -0.5)
    p = jax.nn.softmax(s, axis=-1).astype(v.dtype)
    o = jnp.einsum('bhsm,bmhd->bshd', p, v)
    return o.reshape(B, S, E_CA)


def _fwd_setup_inputs(seed: int = 0) -> dict:
    key = jax.random.key(seed)
    ks = jax.random.split(key, 14)
    f32 = jnp.float32
    nrm = lambda k, shape, s: jax.random.normal(k, shape, f32) * s
    d_in_pool = E_MIX + E_CA + E_BRANCH
    d_in_hg = 3 * E_MIX + E_CA + E_BRANCH
    return {
        "x": nrm(ks[0], (BATCH, SEQ, D_MODEL), 1.0),
        "mem": nrm(ks[1], (BATCH, N_MEM, D_MODEL), 1.0),
        "norm_g": 1.0 + nrm(ks[2], (DEPTH, D_MODEL), 0.02),
        "mem_norm_g": 1.0 + nrm(ks[3], (D_MODEL,), 0.02),
        "w_kv": nrm(ks[4], (DEPTH, D_MODEL, 2 * E_CA), D_MODEL ** -0.5),
        "w_out": nrm(ks[5], (DEPTH, E_BRANCH, D_MODEL), E_BRANCH ** -0.5),
        "pool_w_in": nrm(ks[6], (N_A, D_MODEL, d_in_pool), D_MODEL ** -0.5),
        "pool_w_grp": nrm(ks[7], (N_A, N_POOL_GROUPS, POOL_GROUP, POOL_GROUP), POOL_GROUP ** -0.5),
        "pool_scale": 1.0 + nrm(ks[8], (N_A, E_MIX), 0.1),
        "hgrn_w_in": nrm(ks[9], (N_B, D_MODEL, d_in_hg), D_MODEL ** -0.5),
        "hgrn_lb": nrm(ks[10], (DEPTH, E_MIX), 0.1),
        "hgrn_norm_g": 1.0 + nrm(ks[11], (N_B, E_MIX), 0.02),
        "final_g": 1.0 + nrm(ks[12], (D_MODEL,), 0.02),
    }


def _fwd_reference(x, mem, norm_g, mem_norm_g, w_kv, w_out, pool_w_in, pool_w_grp, pool_scale,
              hgrn_w_in, hgrn_lb, hgrn_norm_g, final_g):
    mem_n = rmsnorm(mem, mem_norm_g)
    sm = jax.nn.softmax(hgrn_lb.astype(jnp.float32), axis=0)
    lb_all = jnp.cumsum(sm, axis=0) - sm[0:1]
    for i in range(DEPTH):
        h = rmsnorm(x, norm_g[i])
        j = i // N_MIXERS
        if i % N_MIXERS == 0:
            proj = jnp.einsum('bsd,de->bse', h, pool_w_in[j])
            u, q_ca, gate = jnp.split(proj, [E_MIX, E_MIX + E_CA], axis=-1)
            mix = pool_mixer(u, pool_w_grp[j], pool_scale[j])
        else:
            proj = jnp.einsum('bsd,de->bse', h, hgrn_w_in[j])
            q_hg, f_hg, i_hg, q_ca, gate = jnp.split(
                proj, [E_MIX, 2 * E_MIX, 3 * E_MIX, 3 * E_MIX + E_CA], axis=-1)
            mix = hgrn2_mixer(q_hg, f_hg, i_hg, lb_all[i], hgrn_norm_g[j])
        ca = memory_attention(q_ca, mem_n, w_kv[i])
        branch = jnp.concatenate([mix, ca], axis=-1) * jax.nn.silu(gate)
        x = x + jnp.einsum('bse,ed->bsd', branch, w_out[i])
    return rmsnorm(x, final_g)


import jax as _jax
import jax.numpy as _jnp

TWIN_FORMAT = 'train_step'
FWD_PARAMS = ['x', 'mem', 'norm_g', 'mem_norm_g', 'w_kv', 'w_out', 'pool_w_in', 'pool_w_grp', 'pool_scale', 'hgrn_w_in', 'hgrn_lb', 'hgrn_norm_g', 'final_g']
TWIN_WEIGHTS = ['norm_g', 'mem_norm_g', 'w_kv', 'w_out', 'pool_w_in', 'pool_w_grp', 'pool_scale', 'hgrn_w_in', 'hgrn_lb', 'hgrn_norm_g', 'final_g']
TWIN_DIFF_INPUT = 'x'
TWIN_INPUTS = ['x', 'mem', 'norm_g', 'mem_norm_g', 'w_kv', 'w_out', 'pool_w_in', 'pool_w_grp', 'pool_scale', 'hgrn_w_in', 'hgrn_lb', 'hgrn_norm_g', 'final_g', 'loss_target', 'm_norm_g', 'm_mem_norm_g', 'm_w_kv', 'm_w_out', 'm_pool_w_in', 'm_pool_w_grp', 'm_pool_scale', 'm_hgrn_w_in', 'm_hgrn_lb', 'm_hgrn_norm_g', 'm_final_g', 'v_norm_g', 'v_mem_norm_g', 'v_w_kv', 'v_w_out', 'v_pool_w_in', 'v_pool_w_grp', 'v_pool_scale', 'v_hgrn_w_in', 'v_hgrn_lb', 'v_hgrn_norm_g', 'v_final_g']
TWIN_OUTPUTS = ['loss', 'grad_x', 'grad_norm_g', 'grad_mem_norm_g', 'grad_w_kv', 'grad_w_out', 'grad_pool_w_in', 'grad_pool_w_grp', 'grad_pool_scale', 'grad_hgrn_w_in', 'grad_hgrn_lb', 'grad_hgrn_norm_g', 'grad_final_g', 'delta_norm_g', 'delta_mem_norm_g', 'delta_w_kv', 'delta_w_out', 'delta_pool_w_in', 'delta_pool_w_grp', 'delta_pool_scale', 'delta_hgrn_w_in', 'delta_hgrn_lb', 'delta_hgrn_norm_g', 'delta_final_g', 'new_m_norm_g', 'new_m_mem_norm_g', 'new_m_w_kv', 'new_m_w_out', 'new_m_pool_w_in', 'new_m_pool_w_grp', 'new_m_pool_scale', 'new_m_hgrn_w_in', 'new_m_hgrn_lb', 'new_m_hgrn_norm_g', 'new_m_final_g', 'new_v_norm_g', 'new_v_mem_norm_g', 'new_v_w_kv', 'new_v_w_out', 'new_v_pool_w_in', 'new_v_pool_w_grp', 'new_v_pool_scale', 'new_v_hgrn_w_in', 'new_v_hgrn_lb', 'new_v_hgrn_norm_g', 'new_v_final_g']
TWIN_LEAF_KINDS = {'loss': 'loss', 'grad_x': 'grad_x', 'grad_norm_g': 'grad_w', 'grad_mem_norm_g': 'grad_w', 'grad_w_kv': 'grad_w', 'grad_w_out': 'grad_w', 'grad_pool_w_in': 'grad_w', 'grad_pool_w_grp': 'grad_w', 'grad_pool_scale': 'grad_w', 'grad_hgrn_w_in': 'grad_w', 'grad_hgrn_lb': 'grad_w', 'grad_hgrn_norm_g': 'grad_w', 'grad_final_g': 'grad_w', 'delta_norm_g': 'delta_w', 'delta_mem_norm_g': 'delta_w', 'delta_w_kv': 'delta_w', 'delta_w_out': 'delta_w', 'delta_pool_w_in': 'delta_w', 'delta_pool_w_grp': 'delta_w', 'delta_pool_scale': 'delta_w', 'delta_hgrn_w_in': 'delta_w', 'delta_hgrn_lb': 'delta_w', 'delta_hgrn_norm_g': 'delta_w', 'delta_final_g': 'delta_w', 'new_m_norm_g': 'new_m', 'new_m_mem_norm_g': 'new_m', 'new_m_w_kv': 'new_m', 'new_m_w_out': 'new_m', 'new_m_pool_w_in': 'new_m', 'new_m_pool_w_grp': 'new_m', 'new_m_pool_scale': 'new_m', 'new_m_hgrn_w_in': 'new_m', 'new_m_hgrn_lb': 'new_m', 'new_m_hgrn_norm_g': 'new_m', 'new_m_final_g': 'new_m', 'new_v_norm_g': 'new_v', 'new_v_mem_norm_g': 'new_v', 'new_v_w_kv': 'new_v', 'new_v_w_out': 'new_v', 'new_v_pool_w_in': 'new_v', 'new_v_pool_w_grp': 'new_v', 'new_v_pool_scale': 'new_v', 'new_v_hgrn_w_in': 'new_v', 'new_v_hgrn_lb': 'new_v', 'new_v_hgrn_norm_g': 'new_v', 'new_v_final_g': 'new_v'}


def _forward(args):
    return _fwd_reference(*[args[k] for k in FWD_PARAMS])


def _output_shape():
    def fwd():
        inp = _fwd_setup_inputs(0)
        return _fwd_reference(*[inp[k] for k in FWD_PARAMS])
    out = _jax.eval_shape(fwd)
    return out.shape, out.dtype

N_MICROBATCH = 1
ADAM_LR = 0.001
ADAM_B1 = 0.9
ADAM_B2 = 0.999
ADAM_EPS = 1e-08
ADAM_WD = 0.01
ADAM_STEP = 10
PER_EXAMPLE_BATCH_AXIS = {'x': 0, 'mem': 0, 'loss_target': 0}
SHARED_INPUTS = []
_WEIGHT_DTYPES = {'norm_g': _jnp.float32, 'mem_norm_g': _jnp.float32, 'w_kv': _jnp.float32, 'w_out': _jnp.float32, 'pool_w_in': _jnp.float32, 'pool_w_grp': _jnp.float32, 'pool_scale': _jnp.float32, 'hgrn_w_in': _jnp.float32, 'hgrn_lb': _jnp.float32, 'hgrn_norm_g': _jnp.float32, 'final_g': _jnp.float32}
MOMENT_SCALE = {'norm_g': 5.645133e-02, 'mem_norm_g': 5.227697e-03, 'w_kv': 3.568947e-03, 'w_out': 3.909267e-02, 'pool_w_in': 2.909107e-02, 'pool_w_grp': 3.295944e-02, 'pool_scale': 3.380864e-02, 'hgrn_w_in': 2.069346e-02, 'hgrn_lb': 2.900585e-03, 'hgrn_norm_g': 3.072261e-02, 'final_g': 1.599698e+01}


def _to_microbatches(a, axis):
    t = _jnp.moveaxis(a, axis, 0)
    t = t.reshape((N_MICROBATCH, t.shape[0] // N_MICROBATCH) + t.shape[1:])
    return _jnp.moveaxis(t, 1, axis + 1)


def setup_inputs(seed: int = 0) -> dict:
    inp = _fwd_setup_inputs(seed)
    key = _jax.random.fold_in(_jax.random.key(seed), 7919)
    shape, _ = _output_shape()
    out = dict(inp)
    out["loss_target"] = _jax.random.normal(_jax.random.fold_in(key, 0), shape, _jnp.float32)
    for i, name in enumerate(TWIN_WEIGHTS):
        w = inp[name].astype(_jnp.float32)
        if MOMENT_SCALE is None:
            s = _jnp.sqrt(_jnp.mean(_jnp.square(w)) + 1e-30)
        else:
            s = MOMENT_SCALE[name]
        km, kv = _jax.random.split(_jax.random.fold_in(key, i + 1))
        out[name] = w
        out["m_" + name] = s * _jax.random.normal(km, w.shape, _jnp.float32)
        out["v_" + name] = (s * s) * _jax.random.uniform(kv, w.shape, _jnp.float32, 0.5, 1.5)
    if N_MICROBATCH > 1:
        for name, axis in PER_EXAMPLE_BATCH_AXIS.items():
            out[name] = _to_microbatches(out[name], axis)
    return {'x': out['x'], 'mem': out['mem'], 'norm_g': out['norm_g'], 'mem_norm_g': out['mem_norm_g'], 'w_kv': out['w_kv'], 'w_out': out['w_out'], 'pool_w_in': out['pool_w_in'], 'pool_w_grp': out['pool_w_grp'], 'pool_scale': out['pool_scale'], 'hgrn_w_in': out['hgrn_w_in'], 'hgrn_lb': out['hgrn_lb'], 'hgrn_norm_g': out['hgrn_norm_g'], 'final_g': out['final_g'], 'loss_target': out['loss_target'], 'm_norm_g': out['m_norm_g'], 'm_mem_norm_g': out['m_mem_norm_g'], 'm_w_kv': out['m_w_kv'], 'm_w_out': out['m_w_out'], 'm_pool_w_in': out['m_pool_w_in'], 'm_pool_w_grp': out['m_pool_w_grp'], 'm_pool_scale': out['m_pool_scale'], 'm_hgrn_w_in': out['m_hgrn_w_in'], 'm_hgrn_lb': out['m_hgrn_lb'], 'm_hgrn_norm_g': out['m_hgrn_norm_g'], 'm_final_g': out['m_final_g'], 'v_norm_g': out['v_norm_g'], 'v_mem_norm_g': out['v_mem_norm_g'], 'v_w_kv': out['v_w_kv'], 'v_w_out': out['v_w_out'], 'v_pool_w_in': out['v_pool_w_in'], 'v_pool_w_grp': out['v_pool_w_grp'], 'v_pool_scale': out['v_pool_scale'], 'v_hgrn_w_in': out['v_hgrn_w_in'], 'v_hgrn_lb': out['v_hgrn_lb'], 'v_hgrn_norm_g': out['v_hgrn_norm_g'], 'v_final_g': out['v_final_g']}


def _loss(weights, diff, rest, loss_target):
    with _jax.named_scope("forward"):
        args = {**rest, TWIN_DIFF_INPUT: diff, **{k: w.astype(_WEIGHT_DTYPES[k]) for k, w in weights.items()}}
        y = _forward(args)
    with _jax.named_scope("loss_head"):
        err = _jnp.square(y.astype(_jnp.float32) - loss_target)
        return 0.5 * _jnp.sum(_jnp.mean(err, axis=-1)) if err.ndim else 0.5 * err


def _adamw(w, g, m, v):
    m = ADAM_B1 * m + (1.0 - ADAM_B1) * g
    v = ADAM_B2 * v + (1.0 - ADAM_B2) * _jnp.square(g)
    m_hat = m / (1.0 - ADAM_B1 ** ADAM_STEP)
    v_hat = v / (1.0 - ADAM_B2 ** ADAM_STEP)
    delta = -ADAM_LR * (m_hat / (_jnp.sqrt(v_hat) + ADAM_EPS) + ADAM_WD * w)
    return delta, m, v


def reference(x, mem, norm_g, mem_norm_g, w_kv, w_out, pool_w_in, pool_w_grp, pool_scale, hgrn_w_in, hgrn_lb, hgrn_norm_g, final_g, loss_target, m_norm_g, m_mem_norm_g, m_w_kv, m_w_out, m_pool_w_in, m_pool_w_grp, m_pool_scale, m_hgrn_w_in, m_hgrn_lb, m_hgrn_norm_g, m_final_g, v_norm_g, v_mem_norm_g, v_w_kv, v_w_out, v_pool_w_in, v_pool_w_grp, v_pool_scale, v_hgrn_w_in, v_hgrn_lb, v_hgrn_norm_g, v_final_g):
    given = dict(x=x, mem=mem, norm_g=norm_g, mem_norm_g=mem_norm_g, w_kv=w_kv, w_out=w_out, pool_w_in=pool_w_in, pool_w_grp=pool_w_grp, pool_scale=pool_scale, hgrn_w_in=hgrn_w_in, hgrn_lb=hgrn_lb, hgrn_norm_g=hgrn_norm_g, final_g=final_g, loss_target=loss_target, m_norm_g=m_norm_g, m_mem_norm_g=m_mem_norm_g, m_w_kv=m_w_kv, m_w_out=m_w_out, m_pool_w_in=m_pool_w_in, m_pool_w_grp=m_pool_w_grp, m_pool_scale=m_pool_scale, m_hgrn_w_in=m_hgrn_w_in, m_hgrn_lb=m_hgrn_lb, m_hgrn_norm_g=m_hgrn_norm_g, m_final_g=m_final_g, v_norm_g=v_norm_g, v_mem_norm_g=v_mem_norm_g, v_w_kv=v_w_kv, v_w_out=v_w_out, v_pool_w_in=v_pool_w_in, v_pool_w_grp=v_pool_w_grp, v_pool_scale=v_pool_scale, v_hgrn_w_in=v_hgrn_w_in, v_hgrn_lb=v_hgrn_lb, v_hgrn_norm_g=v_hgrn_norm_g, v_final_g=v_final_g)
    weights = {n: given[n] for n in TWIN_WEIGHTS}
    shared = {n: given[n] for n in SHARED_INPUTS}
    per_example = {n: given[n] for n in ['x', 'mem']}
    grad_fn = _jax.value_and_grad(_loss, argnums=(0, 1))

    def one_microbatch(ex, loss_target):
        ex = dict(ex)
        diff = ex.pop(TWIN_DIFF_INPUT)
        return grad_fn(weights, diff, {**shared, **ex}, loss_target)

    if N_MICROBATCH == 1:
        loss, (grad_w, grad_x) = one_microbatch(per_example, given["loss_target"])
    else:
        def body(carry, xs):
            loss_sum, grad_sum = carry
            l_k, (gw_k, gx_k) = one_microbatch(xs[0], xs[1])
            with _jax.named_scope("update"):
                return (loss_sum + l_k, _jax.tree.map(_jnp.add, grad_sum, gw_k)), gx_k

        init = (_jnp.zeros((), _jnp.float32), _jax.tree.map(_jnp.zeros_like, weights))
        (loss, grad_w), grad_x = _jax.lax.scan(body, init, (per_example, given["loss_target"]))
    with _jax.named_scope("update"):
        delta_w, new_m, new_v = {}, {}, {}
        for n in TWIN_WEIGHTS:
            delta_w[n], new_m[n], new_v[n] = _adamw(weights[n], grad_w[n], given["m_" + n], given["v_" + n])
    return (loss, grad_x, *[grad_w[n] for n in TWIN_WEIGHTS], *[delta_w[n] for n in TWIN_WEIGHTS],
            *[new_m[n] for n in TWIN_WEIGHTS], *[new_v[n] for n in TWIN_WEIGHTS])
```

```python
import functools

import jax
import jax.numpy as jnp
from jax import lax
from jax.experimental import pallas as pl
from jax.experimental.pallas import tpu as pltpu

F32, BF16 = jnp.float32, jnp.bfloat16
MM_DTYPE = BF16
COMM_DTYPE = BF16
EPS = 1e-6
POOL_WINDOWS = (2, 4, 8, 16)
POOL_HALO = 16
HG_HEAD_DIM = 128
HG_CHUNK = 64
CA_HEADS = 4
N_DEV = 8
ADAM_LR, ADAM_B1, ADAM_B2, ADAM_EPS, ADAM_WD, ADAM_STEP = 0.001, 0.9, 0.999, 1e-08, 0.01, 10
VMEM_LIMIT = 48 << 20
LANE = 128
MESH = pl.DeviceIdType.MESH
AXES = ("x", "y", "c")

NN = (((1,), (0,)), ((), ()))
NT = (((1,), (1,)), ((), ()))
TN = (((0,), (0,)), ((), ()))


def _cp(sem=None):
    return pltpu.CompilerParams(dimension_semantics=sem, vmem_limit_bytes=VMEM_LIMIT)


def _dot(a, b, dims):
    return lax.dot_general(a.astype(MM_DTYPE), b.astype(MM_DTYPE), dims, preferred_element_type=F32)


def _sig(v):
    return 1.0 / (1.0 + jnp.exp(-v))


def _tile(n, pref):
    if n <= pref:
        return n
    t = (pref // LANE) * LANE
    while n % t:
        t -= LANE
    return t


def _sds(shape, dtype):
    return jax.ShapeDtypeStruct(tuple(shape), dtype)


def _mm(name, grid, sem, ins, in_specs, out_shapes, out_specs, dims, epi=None, nk=1, acc_shape=None, aliases=None):
    n_in, n_out = len(ins), len(out_shapes)

    def body(*refs):
        a_ref, b_ref = refs[0], refs[1]
        extra = refs[2:n_in]
        outs = refs[n_in:n_in + n_out]
        p = _dot(a_ref[...], b_ref[...], dims)

        def finish(acc):
            res = epi(acc, *extra) if epi is not None else (acc,)
            for r, o_ref in zip(res, outs):
                o_ref[...] = r.astype(o_ref.dtype)

        if nk == 1:
            finish(p)
        else:
            acc_ref = refs[n_in + n_out]
            k = pl.program_id(len(grid) - 1)

            @pl.when(k == 0)
            def _():
                acc_ref[...] = p

            @pl.when(k > 0)
            def _():
                acc_ref[...] += p

            @pl.when(k == nk - 1)
            def _():
                finish(acc_ref[...])

    scratch = [pltpu.VMEM(acc_shape, F32)] if nk > 1 else []
    return pl.pallas_call(
        body, name=name, grid=grid, in_specs=in_specs, out_specs=out_specs, out_shape=out_shapes,
        scratch_shapes=scratch, compiler_params=_cp(sem), input_output_aliases=aliases or {},
    )(*ins)


def proj_fwd(name, h, wblk, tm=512):
    T, D = h.shape
    nblk, _, nb = wblk.shape
    tm = _tile(T, tm)
    return _mm(name, (nblk, T // tm), ("parallel", "parallel"), [h, wblk],
               [pl.BlockSpec((tm, D), lambda j, i: (i, 0)), pl.BlockSpec((None, D, nb), lambda j, i: (j, 0, 0))],
               [_sds((T, nblk * nb), F32)], [pl.BlockSpec((tm, nb), lambda j, i: (i, j))], NN)[0]


def proj_bwd_w(name, h, dproj, nblk, tm=512, tn=1024):
    T, D = h.shape
    N = dproj.shape[1]
    nb = N // nblk
    tm, tn = _tile(D, tm), _tile(nb, tn)
    per = nb // tn
    return _mm(name, (D // tm, N // tn), ("parallel", "parallel"), [h, dproj],
               [pl.BlockSpec((T, tm), lambda i, n: (0, i)), pl.BlockSpec((T, tn), lambda i, n: (0, n))],
               [_sds((nblk, D, nb), COMM_DTYPE)], [pl.BlockSpec((None, tm, tn), lambda i, n: (n // per, i, n % per))], TN)[0]


def proj_bwd_x(name, dproj, wblk, tm=512):
    T, N = dproj.shape
    nblk, D, nb = wblk.shape
    tm = _tile(T, tm)
    return _mm(name, (T // tm, nblk), ("parallel", "arbitrary"), [dproj, wblk],
               [pl.BlockSpec((tm, nb), lambda i, k: (i, k)), pl.BlockSpec((None, D, nb), lambda i, k: (k, 0, 0))],
               [_sds((T, D), F32)], [pl.BlockSpec((tm, D), lambda i, k: (i, 0))], NT, nk=nblk, acc_shape=(tm, D))[0]


def out_fwd(name, branch, wout, x, tm=512, tn=512):
    T, E = branch.shape
    D = wout.shape[1]
    tm, tn = _tile(T, tm), _tile(D, tn)
    return _mm(name, (T // tm, D // tn), ("parallel", "parallel"), [branch, wout, x],
               [pl.BlockSpec((tm, E), lambda i, j: (i, 0)), pl.BlockSpec((E, tn), lambda i, j: (0, j)),
                pl.BlockSpec((tm, tn), lambda i, j: (i, j))],
               [_sds((T, D), F32)], [pl.BlockSpec((tm, tn), lambda i, j: (i, j))], NN,
               epi=lambda acc, x_ref: (acc + x_ref[...],))[0]


def mm_nt(name, a, b, out_dtype, tm=512, tn=512, res=None):
    M, K = a.shape
    N = b.shape[0]
    tm, tn = _tile(M, tm), _tile(N, tn)
    ins = [a, b] + ([res] if res is not None else [])
    specs = [pl.BlockSpec((tm, K), lambda i, j: (i, 0)), pl.BlockSpec((tn, K), lambda i, j: (j, 0))]
    if res is not None:
        specs.append(pl.BlockSpec((tm, tn), lambda i, j: (i, j)))
    epi = (lambda acc, r_ref: (acc + r_ref[...],)) if res is not None else None
    return _mm(name, (M // tm, N // tn), ("parallel", "parallel"), ins, specs,
               [_sds((M, N), out_dtype)], [pl.BlockSpec((tm, tn), lambda i, j: (i, j))], NT, epi=epi)[0]


def mm_nn(name, a, b, out_dtype, tm=512, tn=512):
    M, K = a.shape
    N = b.shape[1]
    tm, tn = _tile(M, tm), _tile(N, tn)
    return _mm(name, (M // tm, N // tn), ("parallel", "parallel"), [a, b],
               [pl.BlockSpec((tm, K), lambda i, j: (i, 0)), pl.BlockSpec((K, tn), lambda i, j: (0, j))],
               [_sds((M, N), out_dtype)], [pl.BlockSpec((tm, tn), lambda i, j: (i, j))], NN)[0]


def mm_tn(name, a, b, out_dtype, tm=512, tn=512):
    K, M = a.shape
    N = b.shape[1]
    tm, tn = _tile(M, tm), _tile(N, tn)
    return _mm(name, (M // tm, N // tn), ("parallel", "parallel"), [a, b],
               [pl.BlockSpec((K, tm), lambda i, j: (0, i)), pl.BlockSpec((K, tn), lambda i, j: (0, j))],
               [_sds((M, N), out_dtype)], [pl.BlockSpec((tm, tn), lambda i, j: (i, j))], TN)[0]


def rmsnorm_fwd(name, x, g, tm=512):
    T, D = x.shape
    tm = _tile(T, tm)

    def body(x_ref, g_ref, o_ref):
        xf = x_ref[...]
        r = lax.rsqrt(jnp.mean(xf * xf, axis=-1, keepdims=True) + EPS)
        o_ref[...] = ((xf * r) * g_ref[...]).astype(o_ref.dtype)

    return pl.pallas_call(
        body, name=name, grid=(T // tm,),
        in_specs=[pl.BlockSpec((tm, D), lambda i: (i, 0)), pl.BlockSpec((1, D), lambda i: (0, 0))],
        out_specs=pl.BlockSpec((tm, D), lambda i: (i, 0)), out_shape=_sds((T, D), MM_DTYPE),
        compiler_params=_cp(("parallel",)),
    )(x, g.reshape(1, D))


def _rms_bwd_math(xf, g, dh):
    r = lax.rsqrt(jnp.mean(xf * xf, axis=-1, keepdims=True) + EPS)
    gd = dh * g
    dx = r * gd - xf * ((r * r * r) * jnp.mean(gd * xf, axis=-1, keepdims=True))
    dg = jnp.sum(dh * (xf * r), axis=0, keepdims=True)
    return dx, dg


def rmsnorm_bwd(name, x, g, dh, res=None, tm=256):
    T, D = x.shape
    tm = _tile(T, tm)
    has_res = res is not None

    def body(*refs):
        x_ref, g_ref, dh_ref = refs[:3]
        dx_ref, dxb_ref, dg_ref = refs[3 + has_res:]
        dx, dg = _rms_bwd_math(x_ref[...], g_ref[...], dh_ref[...])
        if has_res:
            dx = dx + refs[3][...]
        dx_ref[...] = dx
        dxb_ref[...] = dx.astype(dxb_ref.dtype)

        @pl.when(pl.program_id(0) == 0)
        def _():
            dg_ref[...] = jnp.zeros_like(dg_ref)

        dg_ref[...] += dg

    row = pl.BlockSpec((tm, D), lambda i: (i, 0))
    vec = pl.BlockSpec((1, D), lambda i: (0, 0))
    ins = [x, g.reshape(1, D), dh] + ([res] if has_res else [])
    return pl.pallas_call(
        body, name=name, grid=(T // tm,), in_specs=[row, vec, row] + ([row] if has_res else []),
        out_specs=[row, row, vec], out_shape=[_sds((T, D), F32), _sds((T, D), MM_DTYPE), _sds((1, D), F32)],
        compiler_params=_cp(("arbitrary",)),
    )(*ins)


def final_loss_bwd(name, x, g, target, tm=256):
    T, D = x.shape
    tm = _tile(T, tm)

    def body(x_ref, g_ref, t_ref, loss_ref, dx_ref, dxb_ref, dg_ref):
        xf, gv = x_ref[...], g_ref[...]
        r = lax.rsqrt(jnp.mean(xf * xf, axis=-1, keepdims=True) + EPS)
        err = (xf * r) * gv - t_ref[...]
        part = 0.5 * jnp.sum(jnp.mean(err * err, axis=-1, keepdims=True), axis=0, keepdims=True)
        dx, dg = _rms_bwd_math(xf, gv, err / D)
        dx_ref[...] = dx
        dxb_ref[...] = dx.astype(dxb_ref.dtype)

        @pl.when(pl.program_id(0) == 0)
        def _():
            dg_ref[...] = jnp.zeros_like(dg_ref)
            loss_ref[...] = jnp.zeros_like(loss_ref)

        dg_ref[...] += dg
        loss_ref[...] += jnp.broadcast_to(part, loss_ref.shape)

    row = pl.BlockSpec((tm, D), lambda i: (i, 0))
    vec = pl.BlockSpec((1, D), lambda i: (0, 0))
    return pl.pallas_call(
        body, name=name, grid=(T // tm,), in_specs=[row, vec, row],
        out_specs=[pl.BlockSpec((8, LANE), lambda i: (0, 0)), row, row, vec],
        out_shape=[_sds((8, LANE), F32), _sds((T, D), F32), _sds((T, D), MM_DTYPE), _sds((1, D), F32)],
        compiler_params=_cp(("arbitrary",)),
    )(x, g.reshape(1, D), target)


def _pool_tiles(T, emix):
    pg = emix // len(POOL_WINDOWS)
    tc = 256 if pg % 256 == 0 else LANE
    return pg, tc, _tile(T, 512)


def pool_fwd(name, proj, T, emix):
    pg, tc, R = _pool_tiles(T, emix)
    per_g = pg // tc

    def body(u_ref, o_ref):
        g = pl.program_id(0) // per_g
        for gi, w in enumerate(POOL_WINDOWS):
            @pl.when(g == gi)
            def _():
                for ci in range(T // R):
                    r0 = ci * R
                    cur = u_ref[r0:r0 + R, :]
                    halo = jnp.zeros((POOL_HALO, tc), F32) if ci == 0 else u_ref[r0 - POOL_HALO:r0, :]
                    s = jnp.concatenate([halo, cur], axis=0)
                    for st in range(w.bit_length() - 1):
                        s = s + pltpu.roll(s, 1 << st, axis=0)
                    t = r0 + lax.broadcasted_iota(jnp.int32, (R, 1), 0)
                    cnt = jnp.minimum(t + 1, w).astype(F32)
                    o_ref[r0:r0 + R, :] = (s[POOL_HALO:] / cnt - cur).astype(o_ref.dtype)

    return pl.pallas_call(
        body, name=name, grid=(emix // tc,), in_specs=[pl.BlockSpec((T, tc), lambda j: (0, j))],
        out_specs=pl.BlockSpec((T, tc), lambda j: (0, j)), out_shape=_sds((T, emix), MM_DTYPE),
        compiler_params=_cp(("parallel",)),
    )(proj)


def pool_bwd(name, dpooled, dproj, T, emix):
    pg, tc, R = _pool_tiles(T, emix)
    per_g = pg // tc

    def body(d_ref, _, o_ref):
        g = pl.program_id(0) // per_g
        for gi, w in enumerate(POOL_WINDOWS):
            @pl.when(g == gi)
            def _():
                n = R + POOL_HALO
                for ci in range(T // R):
                    r0 = ci * R
                    cur = d_ref[r0:r0 + R, :]
                    halo = jnp.zeros((POOL_HALO, tc), F32) if ci == T // R - 1 else d_ref[r0 + R:r0 + n, :]
                    t = r0 + lax.broadcasted_iota(jnp.int32, (n, 1), 0)
                    cnt = jnp.minimum(t + 1, w).astype(F32)
                    s = jnp.concatenate([cur, halo], axis=0) / cnt
                    for st in range(w.bit_length() - 1):
                        s = s + pltpu.roll(s, n - (1 << st), axis=0)
                    o_ref[r0:r0 + R, :] = (s[:R] - cur).astype(o_ref.dtype)

    return pl.pallas_call(
        body, name=name, grid=(emix // tc,),
        in_specs=[pl.BlockSpec((T, tc), lambda j: (0, j)), pl.BlockSpec(memory_space=pl.ANY)],
        out_specs=pl.BlockSpec((T, tc), lambda j: (0, j)), out_shape=_sds(dproj.shape, dproj.dtype),
        input_output_aliases={1: 0}, compiler_params=_cp(("parallel",)),
    )(dpooled, dproj)


def grp_fwd(name, pooled, wgrp, scale, proj, gate_off, E):
    T, emix = pooled.shape
    pg = emix // len(POOL_WINDOWS)
    tm = _tile(T, 512)
    tc = 256 if pg % 256 == 0 and gate_off % 256 == 0 else LANE
    per = pg // tc

    def epi(acc, s_ref, g_ref):
        gt = g_ref[...]
        return acc, (acc * s_ref[...]) * (gt * _sig(gt))

    col = pl.BlockSpec((tm, tc), lambda n, i: (i, n))
    return _mm(name, (emix // tc, T // tm), ("parallel", "parallel"), [pooled, wgrp, scale, proj],
               [pl.BlockSpec((tm, pg), lambda n, i: (i, n // per)), pl.BlockSpec((None, pg, tc), lambda n, i: (n // per, 0, n % per)),
                pl.BlockSpec((1, tc), lambda n, i: (0, n)), pl.BlockSpec((tm, tc), lambda n, i: (i, gate_off // tc + n))],
               [_sds((T, emix), F32), _sds((T, E), MM_DTYPE)], [col, col], NN, epi=epi)


def grp_bwd_x(name, dy, wgrp):
    T, emix = dy.shape
    pg = emix // len(POOL_WINDOWS)
    tm = _tile(T, 512)
    return _mm(name, (len(POOL_WINDOWS), T // tm), ("parallel", "parallel"), [dy, wgrp],
               [pl.BlockSpec((tm, pg), lambda g, i: (i, g)), pl.BlockSpec((None, pg, pg), lambda g, i: (g, 0, 0))],
               [_sds((T, emix), F32)], [pl.BlockSpec((tm, pg), lambda g, i: (i, g))], NT)[0]


def grp_bwd_w(name, pooled, dy):
    T, emix = dy.shape
    ng = len(POOL_WINDOWS)
    pg = emix // ng
    col = pl.BlockSpec((T, pg), lambda g: (0, g))
    return _mm(name, (ng,), ("parallel",), [pooled, dy], [col, col],
               [_sds((ng, pg, pg), COMM_DTYPE)], [pl.BlockSpec((None, pg, pg), lambda g: (g, 0, 0))], TN)[0]


def gate_bwd_pool(name, dbranch, y, scale, proj, gate_off, n_proj):
    T, emix = y.shape
    tm, tc = _tile(T, 256), _tile(emix, 512)
    assert gate_off % tc == 0

    def body(db_ref, y_ref, s_ref, g_ref, dy_ref, dg_ref, ds_ref):
        db, yv, sc, gt = db_ref[...], y_ref[...], s_ref[...], g_ref[...]
        sg = _sig(gt)
        dmix = db * (gt * sg)
        dy_ref[...] = (dmix * sc).astype(dy_ref.dtype)
        dg_ref[...] = (db * (yv * sc) * (sg * (1.0 + gt * (1.0 - sg)))).astype(dg_ref.dtype)

        @pl.when(pl.program_id(1) == 0)
        def _():
            ds_ref[...] = jnp.zeros_like(ds_ref)

        ds_ref[...] += jnp.sum(dmix * yv, axis=0, keepdims=True)

    blk = pl.BlockSpec((tm, tc), lambda j, i: (i, j))
    vec = pl.BlockSpec((1, tc), lambda j, i: (0, j))
    gate = pl.BlockSpec((tm, tc), lambda j, i: (i, gate_off // tc + j))
    return pl.pallas_call(
        body, name=name, grid=(emix // tc, T // tm), in_specs=[blk, blk, vec, gate],
        out_specs=[blk, gate, vec],
        out_shape=[_sds((T, emix), MM_DTYPE), _sds((T, n_proj), MM_DTYPE), _sds((1, emix), F32)],
        compiler_params=_cp(("parallel", "arbitrary")),
    )(dbranch, y, scale, proj)


def attn_fwd(name, proj, kv, branch, q_off, gate_off, tm=512):
    T = proj.shape[0]
    M, two_eca = kv.shape
    eca = two_eca // 2
    hd = eca // CA_HEADS
    E = branch.shape[1]
    tm = _tile(T, tm)
    scale = hd ** -0.5

    def body(q_ref, g_ref, k_ref, v_ref, _, ca_ref, br_ref):
        s = _dot(q_ref[...], k_ref[...], NT) * scale
        s = s - jnp.max(s, axis=-1, keepdims=True)
        e = jnp.exp(s)
        p = e / jnp.sum(e, axis=-1, keepdims=True)
        o = _dot(p, v_ref[...], NN)
        gt = g_ref[...]
        ca_ref[...] = o
        br_ref[...] = (o * (gt * _sig(gt))).astype(br_ref.dtype)

    return pl.pallas_call(
        body, name=name, grid=(CA_HEADS, T // tm),
        in_specs=[pl.BlockSpec((tm, hd), lambda h, i: (i, q_off // hd + h)),
                  pl.BlockSpec((tm, hd), lambda h, i: (i, (gate_off + E - eca) // hd + h)),
                  pl.BlockSpec((M, hd), lambda h, i: (0, h)), pl.BlockSpec((M, hd), lambda h, i: (0, CA_HEADS + h)),
                  pl.BlockSpec(memory_space=pl.ANY)],
        out_specs=[pl.BlockSpec((tm, hd), lambda h, i: (i, h)), pl.BlockSpec((tm, hd), lambda h, i: (i, (E - eca) // hd + h))],
        out_shape=[_sds((T, eca), F32), _sds(branch.shape, branch.dtype)],
        input_output_aliases={4: 1}, compiler_params=_cp(("parallel", "parallel")),
    )(proj, proj, kv, kv, branch)


def attn_bwd(name, proj, kv, ca, dbranch, dproj, q_off, gate_off, tm=512):
    T = proj.shape[0]
    M, two_eca = kv.shape
    eca = two_eca // 2
    hd = eca // CA_HEADS
    E = dbranch.shape[1]
    tm = _tile(T, tm)
    scale = hd ** -0.5
    q_blk, g_blk = q_off // hd, (gate_off + E - eca) // hd

    def body(q_ref, g_ref, k_ref, v_ref, ca_ref, db_ref, _, dp_ref, dk_ref, dv_ref, dgate_buf):
        first_rows, part = pl.program_id(1) == 0, pl.program_id(2)

        @pl.when(part == 0)
        def _():
            q, k, v = q_ref[...], k_ref[...], v_ref[...]
            s = _dot(q, k, NT) * scale
            s = s - jnp.max(s, axis=-1, keepdims=True)
            e = jnp.exp(s)
            p = e / jnp.sum(e, axis=-1, keepdims=True)
            gt, db = g_ref[...], db_ref[...]
            sg = _sig(gt)
            do = db * (gt * sg)
            dgate_buf[...] = (db * ca_ref[...] * (sg * (1.0 + gt * (1.0 - sg)))).astype(dgate_buf.dtype)
            dpr = _dot(do, v, NT)
            ds = p * (dpr - jnp.sum(dpr * p, axis=-1, keepdims=True)) * scale
            dp_ref[...] = _dot(ds, k, NN).astype(dp_ref.dtype)

            @pl.when(first_rows)
            def _():
                dk_ref[...] = jnp.zeros_like(dk_ref)
                dv_ref[...] = jnp.zeros_like(dv_ref)

            dk_ref[...] += _dot(ds, q, TN)
            dv_ref[...] += _dot(p, do, TN)

        @pl.when(part == 1)
        def _():
            dp_ref[...] = dgate_buf[...]

    row = lambda blk: pl.BlockSpec((tm, hd), lambda h, i, p: (i, blk + h))
    out_dp = pl.BlockSpec((tm, hd), lambda h, i, p: (i, jnp.where(p == 0, q_blk, g_blk) + h))
    outs = pl.pallas_call(
        body, name=name, grid=(CA_HEADS, T // tm, 2),
        in_specs=[row(q_blk), row(g_blk), pl.BlockSpec((M, hd), lambda h, i, p: (0, h)),
                  pl.BlockSpec((M, hd), lambda h, i, p: (0, CA_HEADS + h)), row(0), row((E - eca) // hd),
                  pl.BlockSpec(memory_space=pl.ANY)],
        out_specs=[out_dp, pl.BlockSpec((M, hd), lambda h, i, p: (0, h)), pl.BlockSpec((M, hd), lambda h, i, p: (0, h))],
        out_shape=[_sds(dproj.shape, dproj.dtype), _sds((M, eca), F32), _sds((M, eca), F32)],
        scratch_shapes=[pltpu.VMEM((tm, hd), dproj.dtype)],
        input_output_aliases={6: 0}, compiler_params=_cp(("parallel", "arbitrary", "arbitrary")),
    )(proj, proj, kv, kv, ca, dbranch, dproj)
    return outs[0], jnp.concatenate([outs[1], outs[2]], axis=1)


def _split3(v):
    hi = v.astype(BF16)
    r1 = v - hi.astype(F32)
    mid = r1.astype(BF16)
    lo = (r1 - mid.astype(F32)).astype(BF16)
    return hi, mid, lo


def _tri_dot(tri, v):
    hi, mid, lo = _split3(v)
    d = functools.partial(lax.dot_general, dimension_numbers=NN, preferred_element_type=F32)
    return d(tri, hi) + d(tri, mid) + d(tri, lo)


def _tri_masks():
    C = HG_CHUNK
    ri = lax.broadcasted_iota(jnp.int32, (C, C), 0)
    ci = lax.broadcasted_iota(jnp.int32, (C, C), 1)
    return ri >= ci, jnp.where(ri >= ci, 1.0, 0.0).astype(BF16), jnp.where(ri <= ci, 1.0, 0.0).astype(BF16)


def _lower_bound(lb2):
    l0, l1 = lb2[0:1, :], lb2[1:2, :]
    m = jnp.maximum(l0, l1)
    e0, e1 = jnp.exp(l0 - m), jnp.exp(l1 - m)
    sm0, sm1 = e0 / (e0 + e1), e1 / (e0 + e1)
    return (sm0 + sm1) - sm0, sm0, sm1


def _hg_chunk_fwd(qi, fi, lb, causal, tril):
    sig = _sig(fi)
    f = lb + (1.0 - lb) * sig
    k = 1.0 - f
    b = _tri_dot(tril, jnp.log(f))
    last = lax.broadcasted_iota(jnp.int32, b.shape, 0) == HG_CHUNK - 1
    bl = jnp.sum(jnp.where(last, b, 0.0), axis=0, keepdims=True)
    eb, enb, ete = jnp.exp(b), jnp.exp(-b), jnp.exp(bl - b)
    sq = _sig(qi)
    q = (qi * sq) * (HG_HEAD_DIM ** -0.5)
    q_dec, k_inv, k_te = q * eb, k * enb, k * ete
    att = jnp.where(causal, _dot(q_dec, k_inv, NT), 0.0)
    return dict(sig=sig, f=f, k=k, eb=eb, enb=enb, ete=ete, dcy=jnp.exp(bl), sq=sq, q_dec=q_dec, k_inv=k_inv, k_te=k_te,
                att=att, last=last)


def hgrn_fwd(name, proj, lb2, norm_g, emix, gate_off, E):
    T = proj.shape[0]
    hd, C = HG_HEAD_DIM, HG_CHUNK
    H, N = emix // hd, T // C

    def body(q_ref, f_ref, i_ref, g_ref, lb_ref, ng_ref, o_ref, st_ref, br_ref):
        causal, tril, _ = _tri_masks()
        lb, _, _ = _lower_bound(lb_ref[...])
        ng = ng_ref[...]

        def chunk(n, st):
            rows = pl.ds(pl.multiple_of(n * C, C), C)
            v = i_ref[rows, :]
            c = _hg_chunk_fwd(q_ref[rows, :], f_ref[rows, :], lb, causal, tril)
            o = _dot(c["att"], v, NN) + _dot(c["q_dec"], st, NT)
            st_ref[n] = st
            o_ref[rows, :] = o
            r = lax.rsqrt(jnp.mean(o * o, axis=-1, keepdims=True) + EPS)
            gt = g_ref[rows, :]
            br_ref[rows, :] = (((o * r) * ng) * (gt * _sig(gt))).astype(br_ref.dtype)
            return st * c["dcy"] + _dot(v, c["k_te"], TN)

        lax.fori_loop(0, N, chunk, jnp.zeros((hd, hd), F32))

    col = lambda off: pl.BlockSpec((T, hd), lambda h: (0, off // hd + h))
    return pl.pallas_call(
        body, name=name, grid=(H,),
        in_specs=[col(0), col(emix), col(2 * emix), col(gate_off), pl.BlockSpec((2, hd), lambda h: (0, h)),
                  pl.BlockSpec((1, hd), lambda h: (0, h))],
        out_specs=[col(0), pl.BlockSpec((None, N, hd, hd), lambda h: (h, 0, 0, 0)), col(0)],
        out_shape=[_sds((T, emix), F32), _sds((H, N, hd, hd), F32), _sds((T, E), MM_DTYPE)],
        compiler_params=_cp(("parallel",)),
    )(proj, proj, proj, proj, lb2, norm_g)


def hgrn_bwd(name, proj, lb2, norm_g, o_pre, states, dbranch, emix, gate_off):
    T, n_proj = proj.shape
    hd, C = HG_HEAD_DIM, HG_CHUNK
    H, N = emix // hd, T // C
    qscale = HG_HEAD_DIM ** -0.5

    def body(q_ref, f_ref, i_ref, g_ref, lb_ref, ng_ref, o_ref, st_ref, db_ref, dp_ref, dlb_ref, dng_ref, stash, dst_ref):
        part = pl.program_id(1)

        @pl.when(part == 0)
        def _():
            causal, tril, triu = _tri_masks()
            lb, sm0, sm1 = _lower_bound(lb_ref[...])
            ng = ng_ref[...]

            def chunk(j, carry):
                dlb, dng = carry
                dst = dst_ref[...]
                n = N - 1 - j
                rows = pl.ds(pl.multiple_of(n * C, C), C)
                qi, v, gt = q_ref[rows, :], i_ref[rows, :], g_ref[rows, :]
                c = _hg_chunk_fwd(qi, f_ref[rows, :], lb, causal, tril)
                st = st_ref[n]
                o, db = o_ref[rows, :], db_ref[rows, :]
                r = lax.rsqrt(jnp.mean(o * o, axis=-1, keepdims=True) + EPS)
                sg = _sig(gt)
                dmix = db * (gt * sg)
                dgate = db * ((o * r) * ng) * (sg * (1.0 + gt * (1.0 - sg)))
                dyn = dmix * ng
                do = r * dyn - o * ((r * r * r) * jnp.mean(dyn * o, axis=-1, keepdims=True))
                dng = dng + jnp.sum(dmix * (o * r), axis=0, keepdims=True)
                datt = jnp.where(causal, _dot(do, v, NT), 0.0)
                dq_dec = _dot(datt, c["k_inv"], NN) + _dot(do, st, NN)
                dk_inv = _dot(datt, c["q_dec"], TN)
                dv = _dot(c["att"], do, TN) + _dot(c["k_te"], dst, NT)
                dk_te = _dot(v, dst, NN)
                ddcy = jnp.sum(dst * st, axis=0, keepdims=True)
                dst_new = _dot(do, c["q_dec"], TN) + dst * c["dcy"]
                t_te = dk_te * c["k_te"]
                dbv = dq_dec * c["q_dec"] - dk_inv * c["k_inv"] - t_te
                dbl = jnp.sum(t_te, axis=0, keepdims=True) + ddcy * c["dcy"]
                dbv = dbv + jnp.where(c["last"], dbl, 0.0)
                dlf = _tri_dot(triu, dbv)
                dk = dk_inv * c["enb"] + dk_te * c["ete"]
                df = dlf / c["f"] - dk
                sig, sq = c["sig"], c["sq"]
                dlb = dlb + jnp.sum(df * (1.0 - sig), axis=0, keepdims=True)
                dqi = (dq_dec * c["eb"]) * qscale * (sq * (1.0 + qi * (1.0 - sq)))
                dp_ref[rows, :] = dqi.astype(dp_ref.dtype)
                stash[0, rows, :] = (df * (1.0 - lb) * (sig * (1.0 - sig))).astype(stash.dtype)
                stash[1, rows, :] = dv.astype(stash.dtype)
                stash[2, rows, :] = dgate.astype(stash.dtype)
                dst_ref[...] = dst_new
                return dlb, dng

            zero = jnp.zeros((1, hd), F32)
            dst_ref[...] = jnp.zeros_like(dst_ref)
            dlb, dng = lax.fori_loop(0, N, chunk, (zero, zero))
            t = dlb * (sm0 * sm1)
            dlb_ref[...] = jnp.concatenate([-t, t], axis=0)
            dng_ref[...] = dng

        for pi in range(3):
            @pl.when(part == pi + 1)
            def _():
                dp_ref[...] = stash[pi]

    col = lambda off: pl.BlockSpec((T, hd), lambda h, p: (0, off // hd + h))
    n_mix = emix // hd
    out_col = pl.BlockSpec((T, hd), lambda h, p: (0, jnp.where(p < 3, p * n_mix, gate_off // hd) + h))
    return pl.pallas_call(
        body, name=name, grid=(H, 4),
        in_specs=[col(0), col(emix), col(2 * emix), col(gate_off), pl.BlockSpec((2, hd), lambda h, p: (0, h)),
                  pl.BlockSpec((1, hd), lambda h, p: (0, h)), col(0),
                  pl.BlockSpec((None, N, hd, hd), lambda h, p: (h, 0, 0, 0)), col(0)],
        out_specs=[out_col, pl.BlockSpec((2, hd), lambda h, p: (0, h)), pl.BlockSpec((1, hd), lambda h, p: (0, h))],
        out_shape=[_sds((T, n_proj), MM_DTYPE), _sds((2, emix), F32), _sds((1, emix), F32)],
        scratch_shapes=[pltpu.VMEM((3, T, hd), MM_DTYPE), pltpu.VMEM((hd, hd), F32)],
        compiler_params=_cp(("parallel", "arbitrary")),
    )(proj, proj, proj, proj, lb2, norm_g, o_pre, states, dbranch)


def local_step(x, mem, target, norm_g, mem_norm_g, wkv, wout, wpin, wgrp, pool_scale, whin, hgrn_lb, hgrn_norm_g, final_g):
    T, D = x.shape
    E = wout[0].shape[0]
    eca = wkv[0].shape[1] // 2
    emix = E - eca
    n0, n1 = wpin.shape[0] * wpin.shape[2], whin.shape[0] * whin.shape[2]
    q0, g0 = emix, emix + eca
    q1, g1 = 3 * emix, 3 * emix + eca

    mem_n = rmsnorm_fwd("mem_norm", mem, mem_norm_g)
    h0 = rmsnorm_fwd("norm0", x, norm_g[0])
    proj0 = proj_fwd("proj0", h0, wpin)
    pooled = pool_fwd("pool_fwd", proj0, T, emix)
    y0, branch0 = grp_fwd("grp_fwd", pooled, wgrp, pool_scale, proj0, g0, E)
    kv0 = mm_nn("kv0", mem_n, wkv[0], MM_DTYPE)
    ca0, branch0 = attn_fwd("attn_fwd0", proj0, kv0, branch0, q0, g0)
    x1 = out_fwd("out0", branch0, wout[0], x)
    h1 = rmsnorm_fwd("norm1", x1, norm_g[1])
    proj1 = proj_fwd("proj1", h1, whin)
    o1, states, branch1 = hgrn_fwd("hgrn_fwd", proj1, hgrn_lb, hgrn_norm_g, emix, g1, E)
    kv1 = mm_nn("kv1", mem_n, wkv[1], MM_DTYPE)
    ca1, branch1 = attn_fwd("attn_fwd1", proj1, kv1, branch1, q1, g1)
    x2 = out_fwd("out1", branch1, wout[1], x1)
    loss, dx2, dx2b, g_final = final_loss_bwd("final", x2, final_g, target)
    g_wout1 = mm_tn("gwout1", branch1, dx2b, COMM_DTYPE)
    dbranch1 = mm_nt("dbranch1", dx2b, wout[1], F32)
    dproj1, g_lb, g_hnorm = hgrn_bwd("hgrn_bwd", proj1, hgrn_lb, hgrn_norm_g, o1, states, dbranch1, emix, g1)
    dproj1, dkv1 = attn_bwd("attn_bwd1", proj1, kv1, ca1, dbranch1, dproj1, q1, g1)
    g_wkv1 = mm_tn("gwkv1", mem_n, dkv1, COMM_DTYPE)
    dmem_n = mm_nt("dmem1", dkv1, wkv[1], F32)
    g_whin = proj_bwd_w("gwhin", h1, dproj1, whin.shape[0])
    dh1 = proj_bwd_x("dh1", dproj1, whin)
    dx1, dx1b, g_norm1 = rmsnorm_bwd("norm1_bwd", x1, norm_g[1], dh1, res=dx2)
    g_wout0 = mm_tn("gwout0", branch0, dx1b, COMM_DTYPE)
    dbranch0 = mm_nt("dbranch0", dx1b, wout[0], F32)
    dy0, dproj0, g_scale = gate_bwd_pool("gate_bwd0", dbranch0, y0, pool_scale, proj0, g0, n0)
    g_wgrp = grp_bwd_w("gwgrp", pooled, dy0)
    dpooled = grp_bwd_x("dpooled", dy0, wgrp)
    dproj0 = pool_bwd("pool_bwd", dpooled, dproj0, T, emix)
    dproj0, dkv0 = attn_bwd("attn_bwd0", proj0, kv0, ca0, dbranch0, dproj0, q0, g0)
    g_wkv0 = mm_tn("gwkv0", mem_n, dkv0, COMM_DTYPE)
    dmem_n = mm_nt("dmem0", dkv0, wkv[0], F32, res=dmem_n)
    g_wpin = proj_bwd_w("gwpin", h0, dproj0, wpin.shape[0])
    dh0 = proj_bwd_x("dh0", dproj0, wpin)
    grad_x, _, g_norm0 = rmsnorm_bwd("norm0_bwd", x, norm_g[0], dh0, res=dx1)
    _, _, g_mem = rmsnorm_bwd("mem_norm_bwd", mem, mem_norm_g, dmem_n)

    small = dict(norm_g=jnp.concatenate([g_norm0, g_norm1], axis=0), mem_norm_g=g_mem[0], pool_scale=g_scale,
                 hgrn_lb=g_lb, hgrn_norm_g=g_hnorm, final_g=g_final[0])
    big = dict(w_kv=[g_wkv0, g_wkv1], w_out=[g_wout0, g_wout1], pool_w_in=g_wpin, pool_w_grp=g_wgrp, hgrn_w_in=g_whin)
    return loss[0, 0], grad_x, big, small


def _position():
    return lax.axis_index("x"), lax.axis_index("y"), lax.axis_index("c")


def _slot(p):
    return 4 * p[0] + 2 * p[1] + p[2]


def all_gather_blocks(name, blocks):
    n = len(blocks)

    def body(*refs):
        ins, outs = refs[:n], refs[n:2 * n]
        send_sems, recv_sems, local_sems = refs[2 * n:]
        x, y, c = _position()
        me, sibling = (x, y, c), (x, y, 1 - c)
        chips = [(1 - x, y), (x, 1 - y), (1 - x, 1 - y)]

        def copy(t, k, block, to, src=None):
            dst = outs[t].at[_slot(block)]
            return pltpu.make_async_remote_copy(src_ref=dst if src is None else src, dst_ref=dst, send_sem=send_sems.at[t, k],
                                                recv_sem=recv_sems.at[t, k], device_id=to, device_id_type=MESH)

        mine = [pltpu.make_async_copy(ins[t], outs[t].at[_slot(me)], local_sems.at[t]) for t in range(n)]
        for cp in mine:
            cp.start()
        sends = []
        for t in range(n):
            sends.append(copy(t, 0, me, sibling, src=ins[t]))
            sends += [copy(t, 1 + j, me, (*chip, c), src=ins[t]) for j, chip in enumerate(chips)]
        for cp in sends:
            cp.start()
        for t in range(n):
            for j, chip in enumerate(chips):
                copy(t, 1 + j, (*chip, c), me).wait_recv()
                passed = copy(t, 4 + j, (*chip, c), sibling)
                passed.start()
                sends.append(passed)
        for t in range(n):
            copy(t, 0, sibling, me).wait_recv()
            for j, chip in enumerate(chips):
                copy(t, 4 + j, (*chip, 1 - c), me).wait_recv()
        for cp in sends:
            cp.wait_send()
        for cp in mine:
            cp.wait()

    any_spec = pl.BlockSpec(memory_space=pl.ANY)
    return pl.pallas_call(
        body, name=name, in_specs=[any_spec] * n, out_specs=[any_spec] * n,
        out_shape=[_sds((N_DEV,) + b.shape, b.dtype) for b in blocks],
        scratch_shapes=[pltpu.SemaphoreType.DMA((n, 7)), pltpu.SemaphoreType.DMA((n, 7)), pltpu.SemaphoreType.DMA((n,))],
        compiler_params=pltpu.CompilerParams(has_side_effects=True),
    )(*blocks)


def exchange_partials(name, groups):
    flat = [(gi, li, g) for gi, grp in enumerate(groups) for li, g in enumerate(grp)]
    n, n_out = len(flat), len(groups)

    def body(*refs):
        ins, outs = refs[:n], refs[n:n + n_out]
        send_sems, recv_sems, local_sems = refs[n + n_out:]
        x, y, c = _position()
        me = (x, y, c)
        flip = lambda v, bit: 1 - v if bit else v
        peers = [(flip(x, k & 4), flip(y, k & 2), flip(c, k & 1)) for k in range(1, N_DEV)]
        local, sends = [], []
        for t, (gi, li, _) in enumerate(flat):
            cp = pltpu.make_async_copy(ins[t].at[_slot(me)], outs[gi].at[_slot(me), li], local_sems.at[t])
            cp.start()
            local.append(cp)
            for k, peer in enumerate(peers):
                cp = pltpu.make_async_remote_copy(src_ref=ins[t].at[_slot(peer)], dst_ref=outs[gi].at[_slot(me), li],
                                                  send_sem=send_sems.at[t, k], recv_sem=recv_sems.at[t, k],
                                                  device_id=peer, device_id_type=MESH)
                cp.start()
                sends.append(cp)
        for t, (gi, li, _) in enumerate(flat):
            for k, peer in enumerate(peers):
                pltpu.make_async_remote_copy(src_ref=ins[t].at[_slot(peer)], dst_ref=outs[gi].at[_slot(peer), li],
                                             send_sem=send_sems.at[t, k], recv_sem=recv_sems.at[t, k],
                                             device_id=peer, device_id_type=MESH).wait_recv()
        for cp in sends:
            cp.wait_send()
        for cp in local:
            cp.wait()

    any_spec = pl.BlockSpec(memory_space=pl.ANY)
    return pl.pallas_call(
        body, name=name, in_specs=[any_spec] * n, out_specs=[any_spec] * n_out,
        out_shape=[_sds((N_DEV, len(grp)) + grp[0].shape[1:], grp[0].dtype) for grp in groups],
        scratch_shapes=[pltpu.SemaphoreType.DMA((n, 7)), pltpu.SemaphoreType.DMA((n, 7)), pltpu.SemaphoreType.DMA((n,))],
        compiler_params=pltpu.CompilerParams(has_side_effects=True),
    )(*[g for _, _, g in flat])


def all_reduce_small(name, vec):
    R = vec.shape[0]

    def body(v_ref, sum_ref, land_ref, send_sems, recv_sems):
        x, y, c = _position()
        me = (x, y, c)
        flip = lambda v, bit: 1 - v if bit else v
        peers = [(flip(x, k & 4), flip(y, k & 2), flip(c, k & 1)) for k in range(1, N_DEV)]
        land_ref[_slot(me)] = v_ref[...]
        sends = [pltpu.make_async_remote_copy(src_ref=v_ref, dst_ref=land_ref.at[_slot(me)], send_sem=send_sems.at[k],
                                              recv_sem=recv_sems.at[k], device_id=peer, device_id_type=MESH)
                 for k, peer in enumerate(peers)]
        for cp in sends:
            cp.start()
        for k, peer in enumerate(peers):
            pltpu.make_async_remote_copy(src_ref=v_ref, dst_ref=land_ref.at[_slot(peer)], send_sem=send_sems.at[k],
                                         recv_sem=recv_sems.at[k], device_id=peer, device_id_type=MESH).wait_recv()
        acc = land_ref[0]
        for s in range(1, N_DEV):
            acc = acc + land_ref[s]
        sum_ref[...] = acc
        for cp in sends:
            cp.wait_send()

    vm = pl.BlockSpec(memory_space=pltpu.VMEM)
    return pl.pallas_call(
        body, name=name, in_specs=[vm], out_specs=[vm, vm],
        out_shape=[_sds(vec.shape, F32), _sds((N_DEV,) + vec.shape, F32)],
        scratch_shapes=[pltpu.SemaphoreType.DMA((7,)), pltpu.SemaphoreType.DMA((7,))],
        compiler_params=pltpu.CompilerParams(has_side_effects=True),
    )(vec)[0]


def _adamw_math(g, w, m, v):
    m2 = ADAM_B1 * m + (1.0 - ADAM_B1) * g
    v2 = ADAM_B2 * v + (1.0 - ADAM_B2) * (g * g)
    m_hat = m2 / (1.0 - ADAM_B1 ** ADAM_STEP)
    v_hat = v2 / (1.0 - ADAM_B2 ** ADAM_STEP)
    return -ADAM_LR * (m_hat / (jnp.sqrt(v_hat) + ADAM_EPS) + ADAM_WD * w), m2, v2


def adamw(name, parts, w, m, v, tr=64):
    S, R, C = parts.shape
    tr = R if R <= tr else tr
    assert R % tr == 0

    def body(p_ref, w_ref, m_ref, v_ref, g_ref, d_ref, nm_ref, nv_ref):
        g = p_ref[0].astype(F32)
        for s in range(1, S):
            g = g + p_ref[s].astype(F32)
        g_ref[...] = g
        d_ref[...], nm_ref[...], nv_ref[...] = _adamw_math(g, w_ref[...], m_ref[...], v_ref[...])

    row = pl.BlockSpec((tr, C), lambda i: (i, 0))
    return pl.pallas_call(
        body, name=name, grid=(R // tr,), in_specs=[pl.BlockSpec((S, tr, C), lambda i: (0, i, 0)), row, row, row],
        out_specs=[row] * 4, out_shape=[_sds((R, C), F32)] * 4, compiler_params=_cp(("parallel",)),
    )(parts, w, m, v)


def _pack(vs):
    flat = jnp.concatenate([v.reshape(-1) for v in vs])
    return flat.reshape(-1, LANE)


def _unpack(packed, like):
    flat, out, off = packed.reshape(-1), [], 0
    for v in like:
        out.append(flat[off:off + v.size].reshape(v.shape))
        off += v.size
    return out


def kernel(x, mem, norm_g, mem_norm_g, w_kv, w_out, pool_w_in, pool_w_grp, pool_scale, hgrn_w_in, hgrn_lb, hgrn_norm_g, final_g, loss_target, m_norm_g, m_mem_norm_g, m_w_kv, m_w_out, m_pool_w_in, m_pool_w_grp, m_pool_scale, m_hgrn_w_in, m_hgrn_lb, m_hgrn_norm_g, m_final_g, v_norm_g, v_mem_norm_g, v_w_kv, v_w_out, v_pool_w_in, v_pool_w_grp, v_pool_scale, v_hgrn_w_in, v_hgrn_lb, v_hgrn_norm_g, v_final_g):
    n_layers = w_kv.shape[0]
    n_grp, pg_loc, pg = pool_w_grp.shape[1:]
    cast = lambda a: a.astype(COMM_DTYPE)
    blocks = [cast(w_kv[l]) for l in range(n_layers)] + [cast(w_out[l]) for l in range(n_layers)]
    blocks += [cast(pool_w_in[0]), cast(pool_w_grp[0]), cast(hgrn_w_in[0]), hgrn_norm_g]
    gathered = all_gather_blocks("gather_weights", blocks)
    wkv = [g.reshape(-1, g.shape[-1]) for g in gathered[:n_layers]]
    wout = [g.reshape(-1, g.shape[-1]) for g in gathered[n_layers:2 * n_layers]]
    wpin, wgrp_g, whin, hnorm_g = gathered[2 * n_layers:]
    wgrp = wgrp_g.transpose(1, 0, 2, 3).reshape(n_grp, pg, pg)
    hnorm = hnorm_g.reshape(1, -1)
    loss, grad_x, big, small = local_step(x[0], mem[0], loss_target[0], norm_g, mem_norm_g, wkv, wout, wpin, wgrp, pool_scale,
                                          whin, hgrn_lb, hnorm, final_g)
    loss = lax.psum(loss, AXES)
    slots = lambda g: g.reshape((N_DEV, g.shape[0] // N_DEV) + g.shape[1:])
    g_grp = big["pool_w_grp"].reshape(n_grp, N_DEV, pg_loc, pg).transpose(1, 0, 2, 3)
    groups = [[slots(g) for g in big["w_kv"]], [slots(g) for g in big["w_out"]], [big["pool_w_in"]], [g_grp], [big["hgrn_w_in"]]]
    landed = exchange_partials("exchange_grads", groups)
    outs = {}
    for nm, land, w, m, v in zip(("w_kv", "w_out", "pool_w_in", "pool_w_grp", "hgrn_w_in"), landed,
                                 (w_kv, w_out, pool_w_in, pool_w_grp, hgrn_w_in),
                                 (m_w_kv, m_w_out, m_pool_w_in, m_pool_w_grp, m_hgrn_w_in),
                                 (v_w_kv, v_w_out, v_pool_w_in, v_pool_w_grp, v_hgrn_w_in)):
        two_d = lambda a: a.reshape(-1, a.shape[-1])
        res = adamw("adamw_" + nm, land.reshape(N_DEV, -1, land.shape[-1]), two_d(w), two_d(m), two_d(v))
        outs[nm] = [r.reshape(w.shape) for r in res]
    names = ("norm_g", "mem_norm_g", "pool_scale", "hgrn_lb", "hgrn_norm_g", "final_g")
    like = (norm_g, mem_norm_g, pool_scale, hgrn_lb, hnorm, final_g)
    reduced = _unpack(all_reduce_small("reduce_small", _pack([small[nm].reshape(lk.shape) for nm, lk in zip(names, like)])), like)
    n_loc = hgrn_norm_g.shape[1]
    me = _slot(_position())
    reduced[4] = lax.dynamic_slice(reduced[4], (0, me * n_loc), (1, n_loc))
    ws = (norm_g, mem_norm_g, pool_scale, hgrn_lb, hgrn_norm_g, final_g)
    ms = (m_norm_g, m_mem_norm_g, m_pool_scale, m_hgrn_lb, m_hgrn_norm_g, m_final_g)
    vs = (v_norm_g, v_mem_norm_g, v_pool_scale, v_hgrn_lb, v_hgrn_norm_g, v_final_g)
    packed_w = _pack(ws)
    res = adamw("adamw_small", _pack(reduced)[None], packed_w, _pack(ms), _pack(vs), tr=packed_w.shape[0])
    for i, nm in enumerate(names):
        outs[nm] = [_unpack(r, ws)[i] for r in res]
    order = ("norm_g", "mem_norm_g", "w_kv", "w_out", "pool_w_in", "pool_w_grp", "pool_scale", "hgrn_w_in", "hgrn_lb",
             "hgrn_norm_g", "final_g")
    return (loss, grad_x[None], *[outs[nm][0] for nm in order], *[outs[nm][1] for nm in order],
            *[outs[nm][2] for nm in order], *[outs[nm][3] for nm in order])
```

```python
import functools

import jax
import jax.numpy as jnp
from jax import lax
from jax.experimental import pallas as pl
from jax.experimental.pallas import tpu as pltpu

F32, BF16 = jnp.float32, jnp.bfloat16
MM_DTYPE = BF16
COMM_DTYPE = BF16
EPS = 1e-6
POOL_WINDOWS = (2, 4, 8, 16)
POOL_HALO = 16
HG_HEAD_DIM = 128
HG_CHUNK = 64
CA_HEADS = 4
N_DEV = 8
ADAM_LR, ADAM_B1, ADAM_B2, ADAM_EPS, ADAM_WD, ADAM_STEP = 0.001, 0.9, 0.999, 1e-08, 0.01, 10
VMEM_LIMIT = 48 << 20
LANE = 128
MESH = pl.DeviceIdType.MESH
AXES = ("x", "y", "c")

NN = (((1,), (0,)), ((), ()))
NT = (((1,), (1,)), ((), ()))
TN = (((0,), (0,)), ((), ()))


def _cp(sem=None):
    return pltpu.CompilerParams(dimension_semantics=sem, vmem_limit_bytes=VMEM_LIMIT)


def _dot(a, b, dims):
    return lax.dot_general(a.astype(MM_DTYPE), b.astype(MM_DTYPE), dims, preferred_element_type=F32)


def _sig(v):
    return 1.0 / (1.0 + jnp.exp(-v))


def _tile(n, pref):
    if n <= pref:
        return n
    t = (pref // LANE) * LANE
    while n % t:
        t -= LANE
    return t


def _sds(shape, dtype):
    return jax.ShapeDtypeStruct(tuple(shape), dtype)


def _mm(name, grid, sem, ins, in_specs, out_shapes, out_specs, dims, epi=None, nk=1, acc_shape=None, aliases=None):
    n_in, n_out = len(ins), len(out_shapes)

    def body(*refs):
        a_ref, b_ref = refs[0], refs[1]
        extra = refs[2:n_in]
        outs = refs[n_in:n_in + n_out]
        p = _dot(a_ref[...], b_ref[...], dims)

        def finish(acc):
            res = epi(acc, *extra) if epi is not None else (acc,)
            for r, o_ref in zip(res, outs):
                o_ref[...] = r.astype(o_ref.dtype)

        if nk == 1:
            finish(p)
        else:
            acc_ref = refs[n_in + n_out]
            k = pl.program_id(len(grid) - 1)

            @pl.when(k == 0)
            def _():
                acc_ref[...] = p

            @pl.when(k > 0)
            def _():
                acc_ref[...] += p

            @pl.when(k == nk - 1)
            def _():
                finish(acc_ref[...])

    scratch = [pltpu.VMEM(acc_shape, F32)] if nk > 1 else []
    return pl.pallas_call(
        body, name=name, grid=grid, in_specs=in_specs, out_specs=out_specs, out_shape=out_shapes,
        scratch_shapes=scratch, compiler_params=_cp(sem), input_output_aliases=aliases or {},
    )(*ins)


def proj_fwd(name, h, wblk, tm=512):
    T, D = h.shape
    nblk, _, nb = wblk.shape
    tm = _tile(T, tm)
    return _mm(name, (nblk, T // tm), ("parallel", "parallel"), [h, wblk],
               [pl.BlockSpec((tm, D), lambda j, i: (i, 0)), pl.BlockSpec((None, D, nb), lambda j, i: (j, 0, 0))],
               [_sds((T, nblk * nb), F32)], [pl.BlockSpec((tm, nb), lambda j, i: (i, j))], NN)[0]


def proj_bwd_w(name, h, dproj, nblk, tm=512, tn=1024):
    T, D = h.shape
    N = dproj.shape[1]
    nb = N // nblk
    tm, tn = _tile(D, tm), _tile(nb, tn)
    per = nb // tn
    return _mm(name, (D // tm, N // tn), ("parallel", "parallel"), [h, dproj],
               [pl.BlockSpec((T, tm), lambda i, n: (0, i)), pl.BlockSpec((T, tn), lambda i, n: (0, n))],
               [_sds((nblk, D, nb), COMM_DTYPE)], [pl.BlockSpec((None, tm, tn), lambda i, n: (n // per, i, n % per))], TN)[0]


def proj_bwd_x(name, dproj, wblk, tm=512):
    T, N = dproj.shape
    nblk, D, nb = wblk.shape
    tm = _tile(T, tm)
    return _mm(name, (T // tm, nblk), ("parallel", "arbitrary"), [dproj, wblk],
               [pl.BlockSpec((tm, nb), lambda i, k: (i, k)), pl.BlockSpec((None, D, nb), lambda i, k: (k, 0, 0))],
               [_sds((T, D), F32)], [pl.BlockSpec((tm, D), lambda i, k: (i, 0))], NT, nk=nblk, acc_shape=(tm, D))[0]


def out_fwd(name, branch, wout, x, tm=512, tn=512):
    T, E = branch.shape
    D = wout.shape[1]
    tm, tn = _tile(T, tm), _tile(D, tn)
    return _mm(name, (T // tm, D // tn), ("parallel", "parallel"), [branch, wout, x],
               [pl.BlockSpec((tm, E), lambda i, j: (i, 0)), pl.BlockSpec((E, tn), lambda i, j: (0, j)),
                pl.BlockSpec((tm, tn), lambda i, j: (i, j))],
               [_sds((T, D), F32)], [pl.BlockSpec((tm, tn), lambda i, j: (i, j))], NN,
               epi=lambda acc, x_ref: (acc + x_ref[...],))[0]


def mm_nt(name, a, b, out_dtype, tm=512, tn=512, res=None):
    M, K = a.shape
    N = b.shape[0]
    tm, tn = _tile(M, tm), _tile(N, tn)
    ins = [a, b] + ([res] if res is not None else [])
    specs = [pl.BlockSpec((tm, K), lambda i, j: (i, 0)), pl.BlockSpec((tn, K), lambda i, j: (j, 0))]
    if res is not None:
        specs.append(pl.BlockSpec((tm, tn), lambda i, j: (i, j)))
    epi = (lambda acc, r_ref: (acc + r_ref[...],)) if res is not None else None
    return _mm(name, (M // tm, N // tn), ("parallel", "parallel"), ins, specs,
               [_sds((M, N), out_dtype)], [pl.BlockSpec((tm, tn), lambda i, j: (i, j))], NT, epi=epi)[0]


def mm_nn(name, a, b, out_dtype, tm=512, tn=512):
    M, K = a.shape
    N = b.shape[1]
    tm, tn = _tile(M, tm), _tile(N, tn)
    return _mm(name, (M // tm, N // tn), ("parallel", "parallel"), [a, b],
               [pl.BlockSpec((tm, K), lambda i, j: (i, 0)), pl.BlockSpec((K, tn), lambda i, j: (0, j))],
               [_sds((M, N), out_dtype)], [pl.BlockSpec((tm, tn), lambda i, j: (i, j))], NN)[0]


def mm_tn(name, a, b, out_dtype, tm=512, tn=512):
    K, M = a.shape
    N = b.shape[1]
    tm, tn = _tile(M, tm), _tile(N, tn)
    return _mm(name, (M // tm, N // tn), ("parallel", "parallel"), [a, b],
               [pl.BlockSpec((K, tm), lambda i, j: (0, i)), pl.BlockSpec((K, tn), lambda i, j: (0, j))],
               [_sds((M, N), out_dtype)], [pl.BlockSpec((tm, tn), lambda i, j: (i, j))], TN)[0]


def rmsnorm_fwd(name, x, g, tm=512):
    T, D = x.shape
    tm = _tile(T, tm)

    def body(x_ref, g_ref, o_ref):
        xf = x_ref[...]
        r = lax.rsqrt(jnp.mean(xf * xf, axis=-1, keepdims=True) + EPS)
        o_ref[...] = ((xf * r) * g_ref[...]).astype(o_ref.dtype)

    return pl.pallas_call(
        body, name=name, grid=(T // tm,),
        in_specs=[pl.BlockSpec((tm, D), lambda i: (i, 0)), pl.BlockSpec((1, D), lambda i: (0, 0))],
        out_specs=pl.BlockSpec((tm, D), lambda i: (i, 0)), out_shape=_sds((T, D), MM_DTYPE),
        compiler_params=_cp(("parallel",)),
    )(x, g.reshape(1, D))


def _rms_bwd_math(xf, g, dh):
    r = lax.rsqrt(jnp.mean(xf * xf, axis=-1, keepdims=True) + EPS)
    gd = dh * g
    dx = r * gd - xf * ((r * r * r) * jnp.mean(gd * xf, axis=-1, keepdims=True))
    dg = jnp.sum(dh * (xf * r), axis=0, keepdims=True)
    return dx, dg


def rmsnorm_bwd(name, x, g, dh, res=None, tm=256):
    T, D = x.shape
    tm = _tile(T, tm)
    has_res = res is not None

    def body(*refs):
        x_ref, g_ref, dh_ref = refs[:3]
        dx_ref, dxb_ref, dg_ref = refs[3 + has_res:]
        dx, dg = _rms_bwd_math(x_ref[...], g_ref[...], dh_ref[...])
        if has_res:
            dx = dx + refs[3][...]
        dx_ref[...] = dx
        dxb_ref[...] = dx.astype(dxb_ref.dtype)

        @pl.when(pl.program_id(0) == 0)
        def _():
            dg_ref[...] = jnp.zeros_like(dg_ref)

        dg_ref[...] += dg

    row = pl.BlockSpec((tm, D), lambda i: (i, 0))
    vec = pl.BlockSpec((1, D), lambda i: (0, 0))
    ins = [x, g.reshape(1, D), dh] + ([res] if has_res else [])
    return pl.pallas_call(
        body, name=name, grid=(T // tm,), in_specs=[row, vec, row] + ([row] if has_res else []),
        out_specs=[row, row, vec], out_shape=[_sds((T, D), F32), _sds((T, D), MM_DTYPE), _sds((1, D), F32)],
        compiler_params=_cp(("arbitrary",)),
    )(*ins)


def final_loss_bwd(name, x, g, target, tm=256):
    T, D = x.shape
    tm = _tile(T, tm)

    def body(x_ref, g_ref, t_ref, loss_ref, dx_ref, dxb_ref, dg_ref):
        xf, gv = x_ref[...], g_ref[...]
        r = lax.rsqrt(jnp.mean(xf * xf, axis=-1, keepdims=True) + EPS)
        err = (xf * r) * gv - t_ref[...]
        part = 0.5 * jnp.sum(jnp.mean(err * err, axis=-1, keepdims=True), axis=0, keepdims=True)
        dx, dg = _rms_bwd_math(xf, gv, err / D)
        dx_ref[...] = dx
        dxb_ref[...] = dx.astype(dxb_ref.dtype)

        @pl.when(pl.program_id(0) == 0)
        def _():
            dg_ref[...] = jnp.zeros_like(dg_ref)
            loss_ref[...] = jnp.zeros_like(loss_ref)

        dg_ref[...] += dg
        loss_ref[...] += jnp.broadcast_to(part, loss_ref.shape)

    row = pl.BlockSpec((tm, D), lambda i: (i, 0))
    vec = pl.BlockSpec((1, D), lambda i: (0, 0))
    return pl.pallas_call(
        body, name=name, grid=(T // tm,), in_specs=[row, vec, row],
        out_specs=[pl.BlockSpec((8, LANE), lambda i: (0, 0)), row, row, vec],
        out_shape=[_sds((8, LANE), F32), _sds((T, D), F32), _sds((T, D), MM_DTYPE), _sds((1, D), F32)],
        compiler_params=_cp(("arbitrary",)),
    )(x, g.reshape(1, D), target)


def _pool_tiles(T, emix):
    pg = emix // len(POOL_WINDOWS)
    tc = 256 if pg % 256 == 0 else LANE
    return pg, tc, _tile(T, 512)


def pool_fwd(name, proj, T, emix):
    pg, tc, R = _pool_tiles(T, emix)
    per_g = pg // tc

    def body(u_ref, o_ref):
        g = pl.program_id(0) // per_g
        for gi, w in enumerate(POOL_WINDOWS):
            @pl.when(g == gi)
            def _():
                for ci in range(T // R):
                    r0 = ci * R
                    cur = u_ref[r0:r0 + R, :]
                    halo = jnp.zeros((POOL_HALO, tc), F32) if ci == 0 else u_ref[r0 - POOL_HALO:r0, :]
                    s = jnp.concatenate([halo, cur], axis=0)
                    for st in range(w.bit_length() - 1):
                        s = s + pltpu.roll(s, 1 << st, axis=0)
                    t = r0 + lax.broadcasted_iota(jnp.int32, (R, 1), 0)
                    cnt = jnp.minimum(t + 1, w).astype(F32)
                    o_ref[r0:r0 + R, :] = (s[POOL_HALO:] / cnt - cur).astype(o_ref.dtype)

    return pl.pallas_call(
        body, name=name, grid=(emix // tc,), in_specs=[pl.BlockSpec((T, tc), lambda j: (0, j))],
        out_specs=pl.BlockSpec((T, tc), lambda j: (0, j)), out_shape=_sds((T, emix), MM_DTYPE),
        compiler_params=_cp(("parallel",)),
    )(proj)


def pool_bwd(name, dpooled, dproj, T, emix):
    pg, tc, R = _pool_tiles(T, emix)
    per_g = pg // tc

    def body(d_ref, _, o_ref):
        g = pl.program_id(0) // per_g
        for gi, w in enumerate(POOL_WINDOWS):
            @pl.when(g == gi)
            def _():
                n = R + POOL_HALO
                for ci in range(T // R):
                    r0 = ci * R
                    cur = d_ref[r0:r0 + R, :]
                    halo = jnp.zeros((POOL_HALO, tc), F32) if ci == T // R - 1 else d_ref[r0 + R:r0 + n, :]
                    t = r0 + lax.broadcasted_iota(jnp.int32, (n, 1), 0)
                    cnt = jnp.minimum(t + 1, w).astype(F32)
                    s = jnp.concatenate([cur, halo], axis=0) / cnt
                    for st in range(w.bit_length() - 1):
                        s = s + pltpu.roll(s, n - (1 << st), axis=0)
                    o_ref[r0:r0 + R, :] = (s[:R] - cur).astype(o_ref.dtype)

    return pl.pallas_call(
        body, name=name, grid=(emix // tc,),
        in_specs=[pl.BlockSpec((T, tc), lambda j: (0, j)), pl.BlockSpec(memory_space=pl.ANY)],
        out_specs=pl.BlockSpec((T, tc), lambda j: (0, j)), out_shape=_sds(dproj.shape, dproj.dtype),
        input_output_aliases={1: 0}, compiler_params=_cp(("parallel",)),
    )(dpooled, dproj)


def grp_fwd(name, pooled, wgrp, scale, proj, gate_off, E):
    T, emix = pooled.shape
    pg = emix // len(POOL_WINDOWS)
    tm = _tile(T, 512)
    tc = 256 if pg % 256 == 0 and gate_off % 256 == 0 else LANE
    per = pg // tc

    def epi(acc, s_ref, g_ref):
        gt = g_ref[...]
        return acc, (acc * s_ref[...]) * (gt * _sig(gt))

    col = pl.BlockSpec((tm, tc), lambda n, i: (i, n))
    return _mm(name, (emix // tc, T // tm), ("parallel", "parallel"), [pooled, wgrp, scale, proj],
               [pl.BlockSpec((tm, pg), lambda n, i: (i, n // per)), pl.BlockSpec((None, pg, tc), lambda n, i: (n // per, 0, n % per)),
                pl.BlockSpec((1, tc), lambda n, i: (0, n)), pl.BlockSpec((tm, tc), lambda n, i: (i, gate_off // tc + n))],
               [_sds((T, emix), F32), _sds((T, E), MM_DTYPE)], [col, col], NN, epi=epi)


def grp_bwd_x(name, dy, wgrp):
    T, emix = dy.shape
    pg = emix // len(POOL_WINDOWS)
    tm = _tile(T, 512)
    return _mm(name, (len(POOL_WINDOWS), T // tm), ("parallel", "parallel"), [dy, wgrp],
               [pl.BlockSpec((tm, pg), lambda g, i: (i, g)), pl.BlockSpec((None, pg, pg), lambda g, i: (g, 0, 0))],
               [_sds((T, emix), F32)], [pl.BlockSpec((tm, pg), lambda g, i: (i, g))], NT)[0]


def grp_bwd_w(name, pooled, dy):
    T, emix = dy.shape
    ng = len(POOL_WINDOWS)
    pg = emix // ng
    col = pl.BlockSpec((T, pg), lambda g: (0, g))
    return _mm(name, (ng,), ("parallel",), [pooled, dy], [col, col],
               [_sds((ng, pg, pg), COMM_DTYPE)], [pl.BlockSpec((None, pg, pg), lambda g: (g, 0, 0))], TN)[0]


def gate_bwd_pool(name, dbranch, y, scale, proj, gate_off, n_proj):
    T, emix = y.shape
    tm, tc = _tile(T, 256), _tile(emix, 512)
    assert gate_off % tc == 0

    def body(db_ref, y_ref, s_ref, g_ref, dy_ref, dg_ref, ds_ref):
        db, yv, sc, gt = db_ref[...], y_ref[...], s_ref[...], g_ref[...]
        sg = _sig(gt)
        dmix = db * (gt * sg)
        dy_ref[...] = (dmix * sc).astype(dy_ref.dtype)
        dg_ref[...] = (db * (yv * sc) * (sg * (1.0 + gt * (1.0 - sg)))).astype(dg_ref.dtype)

        @pl.when(pl.program_id(1) == 0)
        def _():
            ds_ref[...] = jnp.zeros_like(ds_ref)

        ds_ref[...] += jnp.sum(dmix * yv, axis=0, keepdims=True)

    blk = pl.BlockSpec((tm, tc), lambda j, i: (i, j))
    vec = pl.BlockSpec((1, tc), lambda j, i: (0, j))
    gate = pl.BlockSpec((tm, tc), lambda j, i: (i, gate_off // tc + j))
    return pl.pallas_call(
        body, name=name, grid=(emix // tc, T // tm), in_specs=[blk, blk, vec, gate],
        out_specs=[blk, gate, vec],
        out_shape=[_sds((T, emix), MM_DTYPE), _sds((T, n_proj), MM_DTYPE), _sds((1, emix), F32)],
        compiler_params=_cp(("parallel", "arbitrary")),
    )(dbranch, y, scale, proj)


def attn_fwd(name, proj, kv, branch, q_off, gate_off, tm=512):
    T = proj.shape[0]
    M, two_eca = kv.shape
    eca = two_eca // 2
    hd = eca // CA_HEADS
    E = branch.shape[1]
    tm = _tile(T, tm)
    scale = hd ** -0.5

    def body(q_ref, g_ref, k_ref, v_ref, _, ca_ref, br_ref):
        s = _dot(q_ref[...], k_ref[...], NT) * scale
        s = s - jnp.max(s, axis=-1, keepdims=True)
        e = jnp.exp(s)
        p = e / jnp.sum(e, axis=-1, keepdims=True)
        o = _dot(p, v_ref[...], NN)
        gt = g_ref[...]
        ca_ref[...] = o
        br_ref[...] = (o * (gt * _sig(gt))).astype(br_ref.dtype)

    return pl.pallas_call(
        body, name=name, grid=(CA_HEADS, T // tm),
        in_specs=[pl.BlockSpec((tm, hd), lambda h, i: (i, q_off // hd + h)),
                  pl.BlockSpec((tm, hd), lambda h, i: (i, (gate_off + E - eca) // hd + h)),
                  pl.BlockSpec((M, hd), lambda h, i: (0, h)), pl.BlockSpec((M, hd), lambda h, i: (0, CA_HEADS + h)),
                  pl.BlockSpec(memory_space=pl.ANY)],
        out_specs=[pl.BlockSpec((tm, hd), lambda h, i: (i, h)), pl.BlockSpec((tm, hd), lambda h, i: (i, (E - eca) // hd + h))],
        out_shape=[_sds((T, eca), F32), _sds(branch.shape, branch.dtype)],
        input_output_aliases={4: 1}, compiler_params=_cp(("parallel", "parallel")),
    )(proj, proj, kv, kv, branch)


def attn_bwd(name, proj, kv, ca, dbranch, dproj, q_off, gate_off, tm=512):
    T = proj.shape[0]
    M, two_eca = kv.shape
    eca = two_eca // 2
    hd = eca // CA_HEADS
    E = dbranch.shape[1]
    tm = _tile(T, tm)
    scale = hd ** -0.5
    q_blk, g_blk = q_off // hd, (gate_off + E - eca) // hd

    def body(q_ref, g_ref, k_ref, v_ref, ca_ref, db_ref, _, dp_ref, dk_ref, dv_ref, dgate_buf):
        first_rows, part = pl.program_id(1) == 0, pl.program_id(2)

        @pl.when(part == 0)
        def _():
            q, k, v = q_ref[...], k_ref[...], v_ref[...]
            s = _dot(q, k, NT) * scale
            s = s - jnp.max(s, axis=-1, keepdims=True)
            e = jnp.exp(s)
            p = e / jnp.sum(e, axis=-1, keepdims=True)
            gt, db = g_ref[...], db_ref[...]
            sg = _sig(gt)
            do = db * (gt * sg)
            dgate_buf[...] = (db * ca_ref[...] * (sg * (1.0 + gt * (1.0 - sg)))).astype(dgate_buf.dtype)
            dpr = _dot(do, v, NT)
            ds = p * (dpr - jnp.sum(dpr * p, axis=-1, keepdims=True)) * scale
            dp_ref[...] = _dot(ds, k, NN).astype(dp_ref.dtype)

            @pl.when(first_rows)
            def _():
                dk_ref[...] = jnp.zeros_like(dk_ref)
                dv_ref[...] = jnp.zeros_like(dv_ref)

            dk_ref[...] += _dot(ds, q, TN)
            dv_ref[...] += _dot(p, do, TN)

        @pl.when(part == 1)
        def _():
            dp_ref[...] = dgate_buf[...]

    row = lambda blk: pl.BlockSpec((tm, hd), lambda h, i, p: (i, blk + h))
    out_dp = pl.BlockSpec((tm, hd), lambda h, i, p: (i, jnp.where(p == 0, q_blk, g_blk) + h))
    outs = pl.pallas_call(
        body, name=name, grid=(CA_HEADS, T // tm, 2),
        in_specs=[row(q_blk), row(g_blk), pl.BlockSpec((M, hd), lambda h, i, p: (0, h)),
                  pl.BlockSpec((M, hd), lambda h, i, p: (0, CA_HEADS + h)), row(0), row((E - eca) // hd),
                  pl.BlockSpec(memory_space=pl.ANY)],
        out_specs=[out_dp, pl.BlockSpec((M, hd), lambda h, i, p: (0, h)), pl.BlockSpec((M, hd), lambda h, i, p: (0, h))],
        out_shape=[_sds(dproj.shape, dproj.dtype), _sds((M, eca), F32), _sds((M, eca), F32)],
        scratch_shapes=[pltpu.VMEM((tm, hd), dproj.dtype)],
        input_output_aliases={6: 0}, compiler_params=_cp(("parallel", "arbitrary", "arbitrary")),
    )(proj, proj, kv, kv, ca, dbranch, dproj)
    return outs[0], jnp.concatenate([outs[1], outs[2]], axis=1)


def _split3(v):
    hi = v.astype(BF16)
    r1 = v - hi.astype(F32)
    mid = r1.astype(BF16)
    lo = (r1 - mid.astype(F32)).astype(BF16)
    return hi, mid, lo


def _tri_dot(tri, v):
    hi, mid, lo = _split3(v)
    d = functools.partial(lax.dot_general, dimension_numbers=NN, preferred_element_type=F32)
    return d(tri, hi) + d(tri, mid) + d(tri, lo)


def _tri_masks():
    C = HG_CHUNK
    ri = lax.broadcasted_iota(jnp.int32, (C, C), 0)
    ci = lax.broadcasted_iota(jnp.int32, (C, C), 1)
    return ri >= ci, jnp.where(ri >= ci, 1.0, 0.0).astype(BF16), jnp.where(ri <= ci, 1.0, 0.0).astype(BF16)


def _lower_bound(lb2):
    l0, l1 = lb2[0:1, :], lb2[1:2, :]
    m = jnp.maximum(l0, l1)
    e0, e1 = jnp.exp(l0 - m), jnp.exp(l1 - m)
    sm0, sm1 = e0 / (e0 + e1), e1 / (e0 + e1)
    return (sm0 + sm1) - sm0, sm0, sm1


def _hg_chunk_fwd(qi, fi, lb, causal, tril):
    sig = _sig(fi)
    f = lb + (1.0 - lb) * sig
    k = 1.0 - f
    b = _tri_dot(tril, jnp.log(f))
    last = lax.broadcasted_iota(jnp.int32, b.shape, 0) == HG_CHUNK - 1
    bl = jnp.sum(jnp.where(last, b, 0.0), axis=0, keepdims=True)
    eb, enb, ete = jnp.exp(b), jnp.exp(-b), jnp.exp(bl - b)
    sq = _sig(qi)
    q = (qi * sq) * (HG_HEAD_DIM ** -0.5)
    q_dec, k_inv, k_te = q * eb, k * enb, k * ete
    att = jnp.where(causal, _dot(q_dec, k_inv, NT), 0.0)
    return dict(sig=sig, f=f, k=k, eb=eb, enb=enb, ete=ete, dcy=jnp.exp(bl), sq=sq, q_dec=q_dec, k_inv=k_inv, k_te=k_te,
                att=att, last=last)


def _hg_blocking(H, T):
    hb = 4 if H % 4 == 0 else (2 if H % 2 == 0 else 1)
    return hb, _tile(T, 1024)


def hgrn_fwd(name, proj, lb2, norm_g, emix, gate_off, E):
    T = proj.shape[0]
    hd, C = HG_HEAD_DIM, HG_CHUNK
    H, N = emix // hd, T // C
    hb, ts = _hg_blocking(H, T)
    W, nc = hb * hd, ts // C

    def body(q_ref, f_ref, i_ref, g_ref, lb_ref, ng_ref, o_ref, st_ref, br_ref, state):
        @pl.when(pl.program_id(1) == 0)
        def _():
            state[...] = jnp.zeros_like(state)

        causal, tril, _ = _tri_masks()
        lb, _, _ = _lower_bound(lb_ref[...])
        ng = ng_ref[...]

        def chunk(n, carry):
            rows = pl.ds(pl.multiple_of(n * C, C), C)
            for h in range(hb):
                cols = slice(h * hd, (h + 1) * hd)
                v = i_ref[rows, cols]
                c = _hg_chunk_fwd(q_ref[rows, cols], f_ref[rows, cols], lb[:, cols], causal, tril)
                st = state[h]
                o = _dot(c["att"], v, NN) + _dot(c["q_dec"], st, NT)
                st_ref[h, n] = st
                o_ref[rows, cols] = o
                r = lax.rsqrt(jnp.mean(o * o, axis=-1, keepdims=True) + EPS)
                gt = g_ref[rows, cols]
                br_ref[rows, cols] = (((o * r) * ng[:, cols]) * (gt * _sig(gt))).astype(br_ref.dtype)
                state[h] = st * c["dcy"] + _dot(v, c["k_te"], TN)
            return carry

        lax.fori_loop(0, nc, chunk, 0)

    col = lambda off: pl.BlockSpec((ts, W), lambda h, s: (s, off // W + h))
    vec = lambda r: pl.BlockSpec((r, W), lambda h, s: (0, h))
    return pl.pallas_call(
        body, name=name, grid=(H // hb, T // ts),
        in_specs=[col(0), col(emix), col(2 * emix), col(gate_off), vec(2), vec(1)],
        out_specs=[col(0), pl.BlockSpec((hb, nc, hd, hd), lambda h, s: (h, s, 0, 0)), col(0)],
        out_shape=[_sds((T, emix), F32), _sds((H, N, hd, hd), F32), _sds((T, E), MM_DTYPE)],
        scratch_shapes=[pltpu.VMEM((hb, hd, hd), F32)],
        compiler_params=_cp(("parallel", "arbitrary")),
    )(proj, proj, proj, proj, lb2, norm_g)


def hgrn_bwd(name, proj, lb2, norm_g, o_pre, states, dbranch, emix, gate_off):
    T, n_proj = proj.shape
    hd, C = HG_HEAD_DIM, HG_CHUNK
    H, N = emix // hd, T // C
    hb, ts = _hg_blocking(H, T)
    W, nc, ns = hb * hd, ts // C, T // ts
    qscale = HG_HEAD_DIM ** -0.5

    def body(q_ref, f_ref, i_ref, g_ref, lb_ref, ng_ref, o_ref, st_ref, db_ref, dp_ref, dlb_ref, dng_ref, stash, dst_ref, acc):
        first_block, part = pl.program_id(1) == 0, pl.program_id(2)

        @pl.when(part == 0)
        def _():
            @pl.when(first_block)
            def _():
                dst_ref[...] = jnp.zeros_like(dst_ref)
                acc[...] = jnp.zeros_like(acc)

            causal, tril, triu = _tri_masks()
            lb, sm0, sm1 = _lower_bound(lb_ref[...])
            ng = ng_ref[...]

            def chunk(j, carry):
                n = nc - 1 - j
                rows = pl.ds(pl.multiple_of(n * C, C), C)
                for h in range(hb):
                    cols = slice(h * hd, (h + 1) * hd)
                    dst = dst_ref[h]
                    qi, v, gt = q_ref[rows, cols], i_ref[rows, cols], g_ref[rows, cols]
                    c = _hg_chunk_fwd(qi, f_ref[rows, cols], lb[:, cols], causal, tril)
                    st = st_ref[h, n]
                    o, db, ngh = o_ref[rows, cols], db_ref[rows, cols], ng[:, cols]
                    r = lax.rsqrt(jnp.mean(o * o, axis=-1, keepdims=True) + EPS)
                    sg = _sig(gt)
                    dmix = db * (gt * sg)
                    dgate = db * ((o * r) * ngh) * (sg * (1.0 + gt * (1.0 - sg)))
                    dyn = dmix * ngh
                    do = r * dyn - o * ((r * r * r) * jnp.mean(dyn * o, axis=-1, keepdims=True))
                    acc[1:2, cols] += jnp.sum(dmix * (o * r), axis=0, keepdims=True)
                    datt = jnp.where(causal, _dot(do, v, NT), 0.0)
                    dq_dec = _dot(datt, c["k_inv"], NN) + _dot(do, st, NN)
                    dk_inv = _dot(datt, c["q_dec"], TN)
                    dv = _dot(c["att"], do, TN) + _dot(c["k_te"], dst, NT)
                    dk_te = _dot(v, dst, NN)
                    ddcy = jnp.sum(dst * st, axis=0, keepdims=True)
                    dst_ref[h] = _dot(do, c["q_dec"], TN) + dst * c["dcy"]
                    t_te = dk_te * c["k_te"]
                    dbv = dq_dec * c["q_dec"] - dk_inv * c["k_inv"] - t_te
                    dbl = jnp.sum(t_te, axis=0, keepdims=True) + ddcy * c["dcy"]
                    dbv = dbv + jnp.where(c["last"], dbl, 0.0)
                    dlf = _tri_dot(triu, dbv)
                    dk = dk_inv * c["enb"] + dk_te * c["ete"]
                    df = dlf / c["f"] - dk
                    sig, sq = c["sig"], c["sq"]
                    acc[0:1, cols] += jnp.sum(df * (1.0 - sig), axis=0, keepdims=True)
                    dqi = (dq_dec * c["eb"]) * qscale * (sq * (1.0 + qi * (1.0 - sq)))
                    dp_ref[rows, cols] = dqi.astype(dp_ref.dtype)
                    stash[0, rows, cols] = (df * (1.0 - lb[:, cols]) * (sig * (1.0 - sig))).astype(stash.dtype)
                    stash[1, rows, cols] = dv.astype(stash.dtype)
                    stash[2, rows, cols] = dgate.astype(stash.dtype)
                return carry

            lax.fori_loop(0, nc, chunk, 0)
            t = acc[0:1, :] * (sm0 * sm1)
            dlb_ref[...] = jnp.concatenate([-t, t], axis=0)
            dng_ref[...] = acc[1:2, :]

        for pi in range(3):
            @pl.when(part == pi + 1)
            def _():
                dp_ref[...] = stash[pi]

    col = lambda off: pl.BlockSpec((ts, W), lambda h, s, p: (ns - 1 - s, off // W + h))
    vec = lambda r: pl.BlockSpec((r, W), lambda h, s, p: (0, h))
    n_mix = emix // W
    out_col = pl.BlockSpec((ts, W), lambda h, s, p: (ns - 1 - s, jnp.where(p < 3, p * n_mix, gate_off // W) + h))
    return pl.pallas_call(
        body, name=name, grid=(H // hb, ns, 4),
        in_specs=[col(0), col(emix), col(2 * emix), col(gate_off), vec(2), vec(1), col(0),
                  pl.BlockSpec((hb, nc, hd, hd), lambda h, s, p: (h, ns - 1 - s, 0, 0)), col(0)],
        out_specs=[out_col, vec(2), vec(1)],
        out_shape=[_sds((T, n_proj), MM_DTYPE), _sds((2, emix), F32), _sds((1, emix), F32)],
        scratch_shapes=[pltpu.VMEM((3, ts, W), MM_DTYPE), pltpu.VMEM((hb, hd, hd), F32), pltpu.VMEM((8, W), F32)],
        compiler_params=_cp(("parallel", "arbitrary", "arbitrary")),
    )(proj, proj, proj, proj, lb2, norm_g, o_pre, states, dbranch)


class WholeWeights:
    def __init__(self, wkv, wout, wpin, wgrp, whin, hgrn_norm_g):
        self.w = dict(wkv=wkv, wout=wout, wpin=wpin, wgrp=wgrp, whin=whin, hnorm=hgrn_norm_g)
        self.grads = {}

    def first_weights(self):
        return self.w["wpin"], self.w["hnorm"], 0.0

    def after_proj0(self, after):
        pass

    def layer0_weights(self, after):
        return self.w["wgrp"], self.w["wkv"][0], self.w["wout"][0]

    def after_attn0(self, after):
        pass

    def layer1_weights(self, after):
        return self.w["wkv"][1], self.w["wout"][1], self.w["whin"]

    def grads_ready(self, layer, grads):
        self.grads[layer] = grads
        return 0.0

    def grads_reduce(self, layer, after):
        return 0.0


def run_step(x, mem, target, norm_g, mem_norm_g, pool_scale, hgrn_lb, final_g, comm):
    T, D = x.shape
    mem_n = rmsnorm_fwd("mem_norm", mem, mem_norm_g)
    wpin, hgrn_norm_g, zero = comm.first_weights()
    h0 = rmsnorm_fwd("norm0", x, norm_g[0] + zero)
    proj0 = proj_fwd("proj0", h0, wpin)
    comm.after_proj0(proj0)
    n0 = proj0.shape[1]
    E = n0 // 2
    eca = E // 4
    emix = E - eca
    q0, g0 = emix, emix + eca
    q1, g1 = 3 * emix, 3 * emix + eca
    pooled = pool_fwd("pool_fwd", proj0, T, emix)
    wgrp, wkv0, wout0 = comm.layer0_weights(pooled)
    y0, branch0 = grp_fwd("grp_fwd", pooled, wgrp, pool_scale, proj0, g0, E)
    kv0 = mm_nn("kv0", mem_n, wkv0, MM_DTYPE)
    ca0, branch0 = attn_fwd("attn_fwd0", proj0, kv0, branch0, q0, g0)
    comm.after_attn0(ca0)
    x1 = out_fwd("out0", branch0, wout0, x)
    h1 = rmsnorm_fwd("norm1", x1, norm_g[1])
    wkv1, wout1, whin = comm.layer1_weights(h1)
    proj1 = proj_fwd("proj1", h1, whin)
    o1, states, branch1 = hgrn_fwd("hgrn_fwd", proj1, hgrn_lb, hgrn_norm_g, emix, g1, E)
    kv1 = mm_nn("kv1", mem_n, wkv1, MM_DTYPE)
    ca1, branch1 = attn_fwd("attn_fwd1", proj1, kv1, branch1, q1, g1)
    x2 = out_fwd("out1", branch1, wout1, x1)
    loss, dx2, dx2b, g_final = final_loss_bwd("final", x2, final_g, target)
    g_wout1 = mm_tn("gwout1", branch1, dx2b, COMM_DTYPE)
    dbranch1 = mm_nt("dbranch1", dx2b, wout1, F32)
    dproj1, g_lb, g_hnorm = hgrn_bwd("hgrn_bwd", proj1, hgrn_lb, hgrn_norm_g, o1, states, dbranch1, emix, g1)
    dproj1, dkv1 = attn_bwd("attn_bwd1", proj1, kv1, ca1, dbranch1, dproj1, q1, g1)
    g_wkv1 = mm_tn("gwkv1", mem_n, dkv1, COMM_DTYPE)
    dmem_n = mm_nt("dmem1", dkv1, wkv1, F32)
    g_whin = proj_bwd_w("gwhin", h1, dproj1, whin.shape[0])
    zero = comm.grads_ready(1, dict(w_kv=g_wkv1, w_out=g_wout1, hgrn_w_in=g_whin))
    dh1 = proj_bwd_x("dh1", dproj1, whin)
    zero = zero + comm.grads_reduce(1, dh1)
    dx1, dx1b, g_norm1 = rmsnorm_bwd("norm1_bwd", x1, norm_g[1] + zero, dh1, res=dx2)
    g_wout0 = mm_tn("gwout0", branch0, dx1b, COMM_DTYPE)
    dbranch0 = mm_nt("dbranch0", dx1b, wout0, F32)
    dy0, dproj0, g_scale = gate_bwd_pool("gate_bwd0", dbranch0, y0, pool_scale, proj0, g0, n0)
    g_wgrp = grp_bwd_w("gwgrp", pooled, dy0)
    dpooled = grp_bwd_x("dpooled", dy0, wgrp)
    dproj0 = pool_bwd("pool_bwd", dpooled, dproj0, T, emix)
    dproj0, dkv0 = attn_bwd("attn_bwd0", proj0, kv0, ca0, dbranch0, dproj0, q0, g0)
    g_wkv0 = mm_tn("gwkv0", mem_n, dkv0, COMM_DTYPE)
    dmem_n = mm_nt("dmem0", dkv0, wkv0, F32, res=dmem_n)
    g_wpin = proj_bwd_w("gwpin", h0, dproj0, wpin.shape[0])
    zero = comm.grads_ready(0, dict(w_kv=g_wkv0, w_out=g_wout0, pool_w_in=g_wpin, pool_w_grp=g_wgrp))
    dh0 = proj_bwd_x("dh0", dproj0, wpin)
    zero = zero + comm.grads_reduce(0, dh0)
    grad_x, _, g_norm0 = rmsnorm_bwd("norm0_bwd", x, norm_g[0] + zero, dh0, res=dx1)
    _, _, g_mem = rmsnorm_bwd("mem_norm_bwd", mem, mem_norm_g, dmem_n)

    small = dict(norm_g=jnp.concatenate([g_norm0, g_norm1], axis=0), mem_norm_g=g_mem[0], pool_scale=g_scale,
                 hgrn_lb=g_lb, hgrn_norm_g=g_hnorm, final_g=g_final[0])
    return loss[0, 0], grad_x, small


def _position():
    return lax.axis_index("x"), lax.axis_index("y"), lax.axis_index("c")


def _slot(p):
    return 4 * p[0] + 2 * p[1] + p[2]


def all_gather_blocks(name, blocks):
    n = len(blocks)

    def body(*refs):
        ins, outs, token = refs[:n], refs[n:2 * n], refs[2 * n]
        send_sems, recv_sems, local_sems = refs[2 * n + 1:]
        token[...] = jnp.zeros_like(token)
        x, y, c = _position()
        me, sibling = (x, y, c), (x, y, 1 - c)
        chips = _other_chips(x, y)

        def copy(t, k, block, to, src=None):
            dst = outs[t].at[_slot(block)]
            return pltpu.make_async_remote_copy(src_ref=dst if src is None else src, dst_ref=dst, send_sem=send_sems.at[t, k],
                                                recv_sem=recv_sems.at[t, k], device_id=to, device_id_type=MESH)

        mine = [pltpu.make_async_copy(ins[t], outs[t].at[_slot(me)], local_sems.at[t]) for t in range(n)]
        for cp in mine:
            cp.start()
        sends = []
        for t in range(n):
            sends.append(copy(t, 0, me, sibling, src=ins[t]))
            sends += [copy(t, 1 + j, me, (*chip, c), src=ins[t]) for j, chip in enumerate(chips)]
        for cp in sends:
            cp.start()
        for t in range(n):
            for j, chip in enumerate(chips):
                copy(t, 1 + j, (*chip, c), me).wait_recv()
                passed = copy(t, 4 + j, (*chip, c), sibling)
                passed.start()
                sends.append(passed)
        for t in range(n):
            copy(t, 0, sibling, me).wait_recv()
            for j, chip in enumerate(chips):
                copy(t, 4 + j, (*chip, 1 - c), me).wait_recv()
        for cp in sends:
            cp.wait_send()
        for cp in mine:
            cp.wait()

    any_spec = pl.BlockSpec(memory_space=pl.ANY)
    return pl.pallas_call(
        body, name=name, in_specs=[any_spec] * n, out_specs=[any_spec] * n + [pl.BlockSpec(memory_space=pltpu.VMEM)],
        out_shape=[_sds((N_DEV,) + b.shape, b.dtype) for b in blocks] + [_sds((8, LANE), F32)],
        scratch_shapes=[pltpu.SemaphoreType.DMA((n, 7)), pltpu.SemaphoreType.DMA((n, 7)), pltpu.SemaphoreType.DMA((n,))],
        compiler_params=pltpu.CompilerParams(has_side_effects=True),
    )(*blocks)


_HBM = pl.BlockSpec(memory_space=pltpu.HBM)
_SEM = pl.BlockSpec(memory_space=pltpu.SEMAPHORE)
_EFFECT = pltpu.SideEffectType.DATAFLOW_SIDE_EFFECTING


def split_start(name, bufs, n_copies, build):
    n = len(bufs)

    def body(*refs):
        for cp in build(refs[:n], refs[n], refs[n + 1]):
            cp.start()
        refs[-1][...] = jnp.zeros_like(refs[-1])

    outs = pl.pallas_call(
        body, name=name, in_specs=[_HBM] * n,
        out_shape=(pltpu.SemaphoreType.DMA((n_copies,)), pltpu.SemaphoreType.DMA((n_copies,)),
                   *[pltpu.HBM(b.shape, b.dtype) for b in bufs], _sds((8, LANE), F32)),
        out_specs=(_SEM, _SEM, *[_HBM] * n, pl.BlockSpec(memory_space=pltpu.VMEM)),
        input_output_aliases={i: 2 + i for i in range(n)},
        compiler_params=pltpu.CompilerParams(has_side_effects=_EFFECT),
    )(*[pltpu.with_memory_space_constraint(b, pltpu.HBM) for b in bufs])
    return (outs[0], outs[1]), list(outs[2:2 + n]), outs[-1]


def split_wait(name, sems, bufs, build, after):
    n = len(bufs)

    def body(*refs):
        for cp in build(refs[:n], refs[n], refs[n + 1]):
            cp.wait()

    return list(pl.pallas_call(
        body, name=name, in_specs=[_HBM] * n + [_SEM, _SEM, pl.BlockSpec(memory_space=pl.ANY)],
        out_shape=[pltpu.HBM(b.shape, b.dtype) for b in bufs], out_specs=[_HBM] * n,
        input_output_aliases={i: i for i in range(n)},
        compiler_params=pltpu.CompilerParams(has_side_effects=_EFFECT),
    )(*bufs, sems[0], sems[1], after))


def _remote(src, dst, send_sems, recv_sems, k, to):
    return pltpu.make_async_remote_copy(src_ref=src, dst_ref=dst, send_sem=send_sems.at[k], recv_sem=recv_sems.at[k],
                                        device_id=to, device_id_type=MESH)


def _other_chips(x, y):
    return [(1 - x, y), (x, 1 - y), (1 - x, 1 - y)]


def _gather_cross(n):
    def build(refs, ss, rs):
        x, y, c = _position()
        me = _slot((x, y, c))
        targets = [(x, y, 1 - c)] + [(*chip, c) for chip in _other_chips(x, y)]
        return [_remote(refs[t], refs[n + t].at[me], ss, rs, 4 * t + k, to) for t in range(n) for k, to in enumerate(targets)]
    return build


def _gather_pass(n):
    def build(refs, ss, rs):
        x, y, c = _position()
        cps = []
        for t in range(n):
            for j, chip in enumerate(_other_chips(x, y)):
                blk = refs[t].at[_slot((*chip, c))]
                cps.append(_remote(blk, blk, ss, rs, 3 * t + j, (x, y, 1 - c)))
        return cps
    return build


def _reduce_pair(n):
    def build(refs, ss, rs):
        x, y, c = _position()
        return [_remote(refs[t].at[j, 1 - c], refs[n + t].at[j], ss, rs, 4 * t + j, (x, y, 1 - c))
                for t in range(n) for j in range(4)]
    return build


def _reduce_cross(n):
    def build(refs, ss, rs):
        x, y, c = _position()
        return [_remote(refs[t].at[2 * chip[0] + chip[1]], refs[n + t].at[r], ss, rs, 3 * t + r, (*chip, c))
                for t in range(n) for r, chip in enumerate(_other_chips(x, y))]
    return build


def pair_add(name, part, landed, tr=256):
    _, _, R, C = part.shape
    tr = _row_tile(R, tr)

    def body(_, p_ref, l_ref, o_ref):
        o_ref[...] = (p_ref[...].astype(F32) + l_ref[...].astype(F32)).astype(o_ref.dtype)

    blk = pl.BlockSpec((None, tr, C), lambda j, i, core: (j, i, 0))
    core = lax.axis_index("c").astype(jnp.int32).reshape(1)
    return pl.pallas_call(
        body, name=name, out_shape=_sds((4, R, C), part.dtype), compiler_params=_cp(("parallel", "parallel")),
        grid_spec=pltpu.PrefetchScalarGridSpec(
            num_scalar_prefetch=1, grid=(4, R // tr),
            in_specs=[pl.BlockSpec((None, None, tr, C), lambda j, i, core: (j, core[0], i, 0)), blk], out_specs=blk),
    )(core, part, landed)


def _row_tile(R, pref):
    if R <= pref:
        return R
    t = (pref // 16) * 16
    while R % t:
        t -= 16
    return t


def all_reduce_small(name, vec):
    R = vec.shape[0]

    def body(v_ref, sum_ref, land_ref, send_sems, recv_sems):
        x, y, c = _position()
        me = (x, y, c)
        flip = lambda v, bit: 1 - v if bit else v
        peers = [(flip(x, k & 4), flip(y, k & 2), flip(c, k & 1)) for k in range(1, N_DEV)]
        land_ref[_slot(me)] = v_ref[...]
        sends = [pltpu.make_async_remote_copy(src_ref=v_ref, dst_ref=land_ref.at[_slot(me)], send_sem=send_sems.at[k],
                                              recv_sem=recv_sems.at[k], device_id=peer, device_id_type=MESH)
                 for k, peer in enumerate(peers)]
        for cp in sends:
            cp.start()
        for k, peer in enumerate(peers):
            pltpu.make_async_remote_copy(src_ref=v_ref, dst_ref=land_ref.at[_slot(peer)], send_sem=send_sems.at[k],
                                         recv_sem=recv_sems.at[k], device_id=peer, device_id_type=MESH).wait_recv()
        acc = land_ref[0]
        for s in range(1, N_DEV):
            acc = acc + land_ref[s]
        sum_ref[...] = acc
        for cp in sends:
            cp.wait_send()

    vm = pl.BlockSpec(memory_space=pltpu.VMEM)
    return pl.pallas_call(
        body, name=name, in_specs=[vm], out_specs=[vm, vm],
        out_shape=[_sds(vec.shape, F32), _sds((N_DEV,) + vec.shape, F32)],
        scratch_shapes=[pltpu.SemaphoreType.DMA((7,)), pltpu.SemaphoreType.DMA((7,))],
        compiler_params=pltpu.CompilerParams(has_side_effects=True),
    )(vec)[0]


def _adamw_math(g, w, m, v):
    m2 = ADAM_B1 * m + (1.0 - ADAM_B1) * g
    v2 = ADAM_B2 * v + (1.0 - ADAM_B2) * (g * g)
    m_hat = m2 / (1.0 - ADAM_B1 ** ADAM_STEP)
    v_hat = v2 / (1.0 - ADAM_B2 ** ADAM_STEP)
    return -ADAM_LR * (m_hat / (jnp.sqrt(v_hat) + ADAM_EPS) + ADAM_WD * w), m2, v2


def adamw_shard(name, chip_parts, landed, w, m, v, layer, prev=None, tr=128):
    L, R, C = w.shape
    tr = _row_tile(R, tr)
    n_in = 5 + (4 if prev is not None else 0)

    def body(*refs):
        own_ref, land_ref, w_ref, m_ref, v_ref = refs[1:6]
        g_ref, d_ref, nm_ref, nv_ref = refs[1 + n_in:]
        g = own_ref[...].astype(F32)
        for s in range(landed.shape[0]):
            g = g + land_ref[s].astype(F32)
        g_ref[...] = g
        d_ref[...], nm_ref[...], nv_ref[...] = _adamw_math(g, w_ref[...], m_ref[...], v_ref[...])

    lay = pl.BlockSpec((None, tr, C), lambda i, chip: (layer, i, 0))
    own = pl.BlockSpec((None, tr, C), lambda i, chip: (chip[0], i, 0))
    chip = (2 * lax.axis_index("x") + lax.axis_index("y")).astype(jnp.int32).reshape(1)
    ins = [chip_parts, landed, w, m, v] + (list(prev) if prev is not None else [])
    return pl.pallas_call(
        body, name=name, out_shape=[_sds((L, R, C), F32)] * 4,
        grid_spec=pltpu.PrefetchScalarGridSpec(
            num_scalar_prefetch=1, grid=(R // tr,),
            in_specs=[own, pl.BlockSpec((landed.shape[0], tr, C), lambda i, chip: (0, i, 0)), lay, lay, lay]
            + [pl.BlockSpec(memory_space=pl.ANY)] * (n_in - 5),
            out_specs=[lay] * 4),
        input_output_aliases={6 + i: i for i in range(n_in - 5)}, compiler_params=_cp(("parallel",)),
    )(chip, *ins)


def adamw_small(name, g, w, m, v):
    def body(g_ref, w_ref, m_ref, v_ref, d_ref, nm_ref, nv_ref):
        d_ref[...], nm_ref[...], nv_ref[...] = _adamw_math(g_ref[...], w_ref[...], m_ref[...], v_ref[...])

    return pl.pallas_call(body, name=name, out_shape=[_sds(w.shape, F32)] * 3)(g, w, m, v)


def _pack(vs):
    flat = jnp.concatenate([v.reshape(-1) for v in vs])
    return flat.reshape(-1, LANE)


def _unpack(packed, like):
    flat, out, off = packed.reshape(-1), [], 0
    for v in like:
        out.append(flat[off:off + v.size].reshape(v.shape))
        off += v.size
    return out


class MeshWeights:
    def __init__(self, w_kv, w_out, pool_w_in, pool_w_grp, hgrn_w_in, hgrn_norm_g):
        self.n_grp, self.pg_loc, self.pg = pool_w_grp.shape[1:]
        wpin, hnorm, token = all_gather_blocks("gather_first", [pool_w_in[0].astype(COMM_DTYPE), hgrn_norm_g])
        self.first = (wpin, hnorm.reshape(1, -1))
        zero = token[0, 0]
        cast = lambda a: (a + zero).astype(COMM_DTYPE)
        self.gather, self.tokens = {}, []
        for layer, blocks in ((0, [cast(pool_w_grp[0]), cast(w_kv[0]), cast(w_out[0])]),
                              (1, [cast(w_kv[1]), cast(w_out[1]), cast(hgrn_w_in[0])])):
            me = _slot(_position())
            lands = [lax.dynamic_update_index_in_dim(lax.empty((N_DEV,) + b.shape, b.dtype), b, me, 0) for b in blocks]
            n = len(blocks)
            sems, bufs, token = split_start(f"gather{layer}_cross", blocks + lands, 4 * n, _gather_cross(n))
            self.gather[layer] = (sems, bufs)
            self.tokens.append(token[0, 0])
        self.reduce = {}

    def first_weights(self):
        return self.first[0], self.first[1], self.tokens[0] + self.tokens[1]

    def _pass_on(self, layer, after):
        sems, bufs = self.gather[layer]
        n = len(bufs) // 2
        gathered = split_wait(f"gather{layer}_cross_wait", sems, bufs, _gather_cross(n), after)[n:]
        sems, bufs, _ = split_start(f"gather{layer}_pass", gathered, 3 * n, _gather_pass(n))
        self.gather[layer] = (sems, bufs)

    def _gathered(self, layer, after):
        sems, bufs = self.gather[layer]
        return split_wait(f"gather{layer}_pass_wait", sems, bufs, _gather_pass(len(bufs)), after)

    def after_proj0(self, after):
        self._pass_on(0, after)

    def layer0_weights(self, after):
        wgrp, wkv, wout = self._gathered(0, after)
        rows = lambda g: g.reshape(-1, g.shape[-1])
        return wgrp.transpose(1, 0, 2, 3).reshape(self.n_grp, self.pg, self.pg), rows(wkv), rows(wout)

    def after_attn0(self, after):
        self._pass_on(1, after)

    def layer1_weights(self, after):
        wkv, wout, whin = self._gathered(1, after)
        rows = lambda g: g.reshape(-1, g.shape[-1])
        return rows(wkv), rows(wout), whin

    def grads_ready(self, layer, grads):
        parts = {}
        for nm, g in grads.items():
            if nm == "pool_w_grp":
                g = g.reshape(self.n_grp, N_DEV, self.pg_loc, self.pg).transpose(1, 0, 2, 3)
            parts[nm] = g.reshape(4, 2, -1, g.shape[-1])
        names, srcs = list(parts), list(parts.values())
        lands = [lax.empty((4,) + p.shape[2:], p.dtype) for p in srcs]
        sems, bufs, token = split_start(f"reduce{layer}_pair", srcs + lands, 4 * len(srcs), _reduce_pair(len(srcs)))
        self.reduce[layer] = (names, sems, bufs)
        return token[0, 0]

    def grads_reduce(self, layer, after):
        names, sems, bufs = self.reduce[layer]
        n = len(names)
        bufs = split_wait(f"reduce{layer}_pair_wait", sems, bufs, _reduce_pair(n), after)
        chip_parts = [pair_add(f"pair_add_{nm}{layer}", bufs[t], bufs[n + t]) for t, nm in enumerate(names)]
        lands = [lax.empty((3,) + p.shape[1:], p.dtype) for p in chip_parts]
        sems, bufs, token = split_start(f"reduce{layer}_cross", chip_parts + lands, 3 * n, _reduce_cross(n))
        self.reduce[layer] = (names, sems, bufs)
        return token[0, 0]

    def grads_done(self, layer, after):
        names, sems, bufs = self.reduce[layer]
        n = len(names)
        bufs = split_wait(f"reduce{layer}_cross_wait", sems, bufs, _reduce_cross(n), after)
        return {nm: (bufs[t], bufs[n + t]) for t, nm in enumerate(names)}


def kernel(x, mem, norm_g, mem_norm_g, w_kv, w_out, pool_w_in, pool_w_grp, pool_scale, hgrn_w_in, hgrn_lb, hgrn_norm_g, final_g, loss_target, m_norm_g, m_mem_norm_g, m_w_kv, m_w_out, m_pool_w_in, m_pool_w_grp, m_pool_scale, m_hgrn_w_in, m_hgrn_lb, m_hgrn_norm_g, m_final_g, v_norm_g, v_mem_norm_g, v_w_kv, v_w_out, v_pool_w_in, v_pool_w_grp, v_pool_scale, v_hgrn_w_in, v_hgrn_lb, v_hgrn_norm_g, v_final_g):
    comm = MeshWeights(w_kv, w_out, pool_w_in, pool_w_grp, hgrn_w_in, hgrn_norm_g)
    loss, grad_x, small = run_step(x[0], mem[0], loss_target[0], norm_g, mem_norm_g, pool_scale, hgrn_lb, final_g, comm)
    loss = lax.psum(loss, AXES)
    names = ("norm_g", "mem_norm_g", "pool_scale", "hgrn_lb", "hgrn_norm_g", "final_g")
    like = (norm_g, mem_norm_g, pool_scale, hgrn_lb, _sds((1, N_DEV * hgrn_norm_g.shape[1]), F32), final_g)
    reduced = _unpack(all_reduce_small("reduce_small", _pack([small[nm].reshape(lk.shape) for nm, lk in zip(names, like)])), like)
    n_loc = hgrn_norm_g.shape[1]
    reduced[4] = lax.dynamic_slice(reduced[4], (0, _slot(_position()) * n_loc), (1, n_loc))
    ws = (norm_g, mem_norm_g, pool_scale, hgrn_lb, hgrn_norm_g, final_g)
    ms = (m_norm_g, m_mem_norm_g, m_pool_scale, m_hgrn_lb, m_hgrn_norm_g, m_final_g)
    vs = (v_norm_g, v_mem_norm_g, v_pool_scale, v_hgrn_lb, v_hgrn_norm_g, v_final_g)
    res = adamw_small("adamw_small", _pack(reduced), _pack(ws), _pack(ms), _pack(vs))
    outs = {nm: [reduced[i]] + [_unpack(r, ws)[i] for r in res] for i, nm in enumerate(names)}
    rows = lambda a: a.reshape(a.shape[0], -1, a.shape[-1])
    state = dict(w_kv=(w_kv, m_w_kv, v_w_kv), w_out=(w_out, m_w_out, v_w_out), pool_w_in=(pool_w_in, m_pool_w_in, v_pool_w_in),
                 pool_w_grp=(pool_w_grp, m_pool_w_grp, v_pool_w_grp), hgrn_w_in=(hgrn_w_in, m_hgrn_w_in, v_hgrn_w_in))
    big, after = {}, res[0]
    for layer in (1, 0):
        for nm, (chip_parts, landed) in comm.grads_done(layer, after).items():
            w, m, v = state[nm]
            big[nm] = adamw_shard(f"adamw_{nm}{layer}", chip_parts, landed, rows(w), rows(m), rows(v),
                                  layer if w.shape[0] > 1 else 0, prev=big.get(nm))
            after = big[nm][0]
    for nm, (w, _, _) in state.items():
        outs[nm] = [r.reshape(w.shape) for r in big[nm]]
    order = ("norm_g", "mem_norm_g", "w_kv", "w_out", "pool_w_in", "pool_w_grp", "pool_scale", "hgrn_w_in", "hgrn_lb",
             "hgrn_norm_g", "final_g")
    return (loss, grad_x[None], *[outs[nm][0] for nm in order], *[outs[nm][1] for nm in order],
            *[outs[nm][2] for nm in order], *[outs[nm][3] for nm in order])
```

```python
import functools

import jax
import jax.numpy as jnp
from jax import lax
from jax.experimental import pallas as pl
from jax.experimental.pallas import tpu as pltpu

F32, BF16 = jnp.float32, jnp.bfloat16
MM_DTYPE = BF16
COMM_DTYPE = BF16
EPS = 1e-6
POOL_WINDOWS = (2, 4, 8, 16)
POOL_HALO = 16
HG_HEAD_DIM = 128
HG_CHUNK = 64
CA_HEADS = 4
N_DEV = 8
ADAM_LR, ADAM_B1, ADAM_B2, ADAM_EPS, ADAM_WD, ADAM_STEP = 0.001, 0.9, 0.999, 1e-08, 0.01, 10
VMEM_LIMIT = 48 << 20
LANE = 128
MESH = pl.DeviceIdType.MESH
AXES = ("x", "y", "c")

NN = (((1,), (0,)), ((), ()))
NT = (((1,), (1,)), ((), ()))
TN = (((0,), (0,)), ((), ()))


def _cp(sem=None):
    return pltpu.CompilerParams(dimension_semantics=sem, vmem_limit_bytes=VMEM_LIMIT)


def _dot(a, b, dims):
    return lax.dot_general(a.astype(MM_DTYPE), b.astype(MM_DTYPE), dims, preferred_element_type=F32)


def _sig(v):
    return 1.0 / (1.0 + jnp.exp(-v))


def _tile(n, pref):
    if n <= pref:
        return n
    t = (pref // LANE) * LANE
    while n % t:
        t -= LANE
    return t


def _sds(shape, dtype):
    return jax.ShapeDtypeStruct(tuple(shape), dtype)


def _mm(name, grid, sem, ins, in_specs, out_shapes, out_specs, dims, epi=None, nk=1, acc_shape=None, aliases=None):
    n_in, n_out = len(ins), len(out_shapes)

    def body(*refs):
        a_ref, b_ref = refs[0], refs[1]
        extra = refs[2:n_in]
        outs = refs[n_in:n_in + n_out]
        p = _dot(a_ref[...], b_ref[...], dims)

        def finish(acc):
            res = epi(acc, *extra) if epi is not None else (acc,)
            for r, o_ref in zip(res, outs):
                o_ref[...] = r.astype(o_ref.dtype)

        if nk == 1:
            finish(p)
        else:
            acc_ref = refs[n_in + n_out]
            k = pl.program_id(len(grid) - 1)

            @pl.when(k == 0)
            def _():
                acc_ref[...] = p

            @pl.when(k > 0)
            def _():
                acc_ref[...] += p

            @pl.when(k == nk - 1)
            def _():
                finish(acc_ref[...])

    scratch = [pltpu.VMEM(acc_shape, F32)] if nk > 1 else []
    return pl.pallas_call(
        body, name=name, grid=grid, in_specs=in_specs, out_specs=out_specs, out_shape=out_shapes,
        scratch_shapes=scratch, compiler_params=_cp(sem), input_output_aliases=aliases or {},
    )(*ins)


def proj_fwd(name, h, wblk, tm=512):
    T, D = h.shape
    nblk, _, nb = wblk.shape
    tm = _tile(T, tm)
    return _mm(name, (nblk, T // tm), ("parallel", "parallel"), [h, wblk],
               [pl.BlockSpec((tm, D), lambda j, i: (i, 0)), pl.BlockSpec((None, D, nb), lambda j, i: (j, 0, 0))],
               [_sds((T, nblk * nb), F32)], [pl.BlockSpec((tm, nb), lambda j, i: (i, j))], NN)[0]


def proj_bwd_w(name, h, dproj, nblk, tm=512, tn=1024):
    T, D = h.shape
    N = dproj.shape[1]
    nb = N // nblk
    tm, tn = _tile(D, tm), _tile(nb, tn)
    per = nb // tn
    return _mm(name, (D // tm, N // tn), ("parallel", "parallel"), [h, dproj],
               [pl.BlockSpec((T, tm), lambda i, n: (0, i)), pl.BlockSpec((T, tn), lambda i, n: (0, n))],
               [_sds((nblk, D, nb), COMM_DTYPE)], [pl.BlockSpec((None, tm, tn), lambda i, n: (n // per, i, n % per))], TN)[0]


def proj_bwd_x(name, dproj, wblk, tm=512):
    T, N = dproj.shape
    nblk, D, nb = wblk.shape
    tm = _tile(T, tm)
    return _mm(name, (T // tm, nblk), ("parallel", "arbitrary"), [dproj, wblk],
               [pl.BlockSpec((tm, nb), lambda i, k: (i, k)), pl.BlockSpec((None, D, nb), lambda i, k: (k, 0, 0))],
               [_sds((T, D), F32)], [pl.BlockSpec((tm, D), lambda i, k: (i, 0))], NT, nk=nblk, acc_shape=(tm, D))[0]


def out_fwd(name, branch, wout, x, tm=512, tn=512):
    T, E = branch.shape
    D = wout.shape[1]
    tm, tn = _tile(T, tm), _tile(D, tn)
    return _mm(name, (T // tm, D // tn), ("parallel", "parallel"), [branch, wout, x],
               [pl.BlockSpec((tm, E), lambda i, j: (i, 0)), pl.BlockSpec((E, tn), lambda i, j: (0, j)),
                pl.BlockSpec((tm, tn), lambda i, j: (i, j))],
               [_sds((T, D), F32)], [pl.BlockSpec((tm, tn), lambda i, j: (i, j))], NN,
               epi=lambda acc, x_ref: (acc + x_ref[...],))[0]


def mm_nt(name, a, b, out_dtype, tm=512, tn=512, res=None):
    M, K = a.shape
    N = b.shape[0]
    tm, tn = _tile(M, tm), _tile(N, tn)
    ins = [a, b] + ([res] if res is not None else [])
    specs = [pl.BlockSpec((tm, K), lambda i, j: (i, 0)), pl.BlockSpec((tn, K), lambda i, j: (j, 0))]
    if res is not None:
        specs.append(pl.BlockSpec((tm, tn), lambda i, j: (i, j)))
    epi = (lambda acc, r_ref: (acc + r_ref[...],)) if res is not None else None
    return _mm(name, (M // tm, N // tn), ("parallel", "parallel"), ins, specs,
               [_sds((M, N), out_dtype)], [pl.BlockSpec((tm, tn), lambda i, j: (i, j))], NT, epi=epi)[0]


def mm_nn(name, a, b, out_dtype, tm=512, tn=512):
    M, K = a.shape
    N = b.shape[1]
    tm, tn = _tile(M, tm), _tile(N, tn)
    return _mm(name, (M // tm, N // tn), ("parallel", "parallel"), [a, b],
               [pl.BlockSpec((tm, K), lambda i, j: (i, 0)), pl.BlockSpec((K, tn), lambda i, j: (0, j))],
               [_sds((M, N), out_dtype)], [pl.BlockSpec((tm, tn), lambda i, j: (i, j))], NN)[0]


def mm_tn(name, a, b, out_dtype, tm=512, tn=512):
    K, M = a.shape
    N = b.shape[1]
    tm, tn = _tile(M, tm), _tile(N, tn)
    return _mm(name, (M // tm, N // tn), ("parallel", "parallel"), [a, b],
               [pl.BlockSpec((K, tm), lambda i, j: (0, i)), pl.BlockSpec((K, tn), lambda i, j: (0, j))],
               [_sds((M, N), out_dtype)], [pl.BlockSpec((tm, tn), lambda i, j: (i, j))], TN)[0]


def rmsnorm_fwd(name, x, g, tm=512):
    T, D = x.shape
    tm = _tile(T, tm)

    def body(x_ref, g_ref, o_ref):
        xf = x_ref[...]
        r = lax.rsqrt(jnp.mean(xf * xf, axis=-1, keepdims=True) + EPS)
        o_ref[...] = ((xf * r) * g_ref[...]).astype(o_ref.dtype)

    return pl.pallas_call(
        body, name=name, grid=(T // tm,),
        in_specs=[pl.BlockSpec((tm, D), lambda i: (i, 0)), pl.BlockSpec((1, D), lambda i: (0, 0))],
        out_specs=pl.BlockSpec((tm, D), lambda i: (i, 0)), out_shape=_sds((T, D), MM_DTYPE),
        compiler_params=_cp(("parallel",)),
    )(x, g.reshape(1, D))


def _rms_bwd_math(xf, g, dh):
    r = lax.rsqrt(jnp.mean(xf * xf, axis=-1, keepdims=True) + EPS)
    gd = dh * g
    dx = r * gd - xf * ((r * r * r) * jnp.mean(gd * xf, axis=-1, keepdims=True))
    dg = jnp.sum(dh * (xf * r), axis=0, keepdims=True)
    return dx, dg


def rmsnorm_bwd(name, x, g, dh, res=None, tm=256):
    T, D = x.shape
    tm = _tile(T, tm)
    has_res = res is not None

    def body(*refs):
        x_ref, g_ref, dh_ref = refs[:3]
        dx_ref, dxb_ref, dg_ref = refs[3 + has_res:]
        dx, dg = _rms_bwd_math(x_ref[...], g_ref[...], dh_ref[...])
        if has_res:
            dx = dx + refs[3][...]
        dx_ref[...] = dx
        dxb_ref[...] = dx.astype(dxb_ref.dtype)

        @pl.when(pl.program_id(0) == 0)
        def _():
            dg_ref[...] = jnp.zeros_like(dg_ref)

        dg_ref[...] += dg

    row = pl.BlockSpec((tm, D), lambda i: (i, 0))
    vec = pl.BlockSpec((1, D), lambda i: (0, 0))
    ins = [x, g.reshape(1, D), dh] + ([res] if has_res else [])
    return pl.pallas_call(
        body, name=name, grid=(T // tm,), in_specs=[row, vec, row] + ([row] if has_res else []),
        out_specs=[row, row, vec], out_shape=[_sds((T, D), F32), _sds((T, D), MM_DTYPE), _sds((1, D), F32)],
        compiler_params=_cp(("arbitrary",)),
    )(*ins)


def final_loss_bwd(name, x, g, target, tm=256):
    T, D = x.shape
    tm = _tile(T, tm)

    def body(x_ref, g_ref, t_ref, loss_ref, dx_ref, dxb_ref, dg_ref):
        xf, gv = x_ref[...], g_ref[...]
        r = lax.rsqrt(jnp.mean(xf * xf, axis=-1, keepdims=True) + EPS)
        err = (xf * r) * gv - t_ref[...]
        part = 0.5 * jnp.sum(jnp.mean(err * err, axis=-1, keepdims=True), axis=0, keepdims=True)
        dx, dg = _rms_bwd_math(xf, gv, err / D)
        dx_ref[...] = dx
        dxb_ref[...] = dx.astype(dxb_ref.dtype)

        @pl.when(pl.program_id(0) == 0)
        def _():
            dg_ref[...] = jnp.zeros_like(dg_ref)
            loss_ref[...] = jnp.zeros_like(loss_ref)

        dg_ref[...] += dg
        loss_ref[...] += jnp.broadcast_to(part, loss_ref.shape)

    row = pl.BlockSpec((tm, D), lambda i: (i, 0))
    vec = pl.BlockSpec((1, D), lambda i: (0, 0))
    return pl.pallas_call(
        body, name=name, grid=(T // tm,), in_specs=[row, vec, row],
        out_specs=[pl.BlockSpec((8, LANE), lambda i: (0, 0)), row, row, vec],
        out_shape=[_sds((8, LANE), F32), _sds((T, D), F32), _sds((T, D), MM_DTYPE), _sds((1, D), F32)],
        compiler_params=_cp(("arbitrary",)),
    )(x, g.reshape(1, D), target)


def _pool_tiles(T, emix):
    pg = emix // len(POOL_WINDOWS)
    tc = 256 if pg % 256 == 0 else LANE
    return pg, tc, _tile(T, 512)


def pool_fwd(name, proj, T, emix):
    pg, tc, R = _pool_tiles(T, emix)
    per_g = pg // tc

    def body(u_ref, o_ref):
        g = pl.program_id(0) // per_g
        for gi, w in enumerate(POOL_WINDOWS):
            @pl.when(g == gi)
            def _():
                for ci in range(T // R):
                    r0 = ci * R
                    cur = u_ref[r0:r0 + R, :]
                    halo = jnp.zeros((POOL_HALO, tc), F32) if ci == 0 else u_ref[r0 - POOL_HALO:r0, :]
                    s = jnp.concatenate([halo, cur], axis=0)
                    for st in range(w.bit_length() - 1):
                        s = s + pltpu.roll(s, 1 << st, axis=0)
                    t = r0 + lax.broadcasted_iota(jnp.int32, (R, 1), 0)
                    cnt = jnp.minimum(t + 1, w).astype(F32)
                    o_ref[r0:r0 + R, :] = (s[POOL_HALO:] / cnt - cur).astype(o_ref.dtype)

    return pl.pallas_call(
        body, name=name, grid=(emix // tc,), in_specs=[pl.BlockSpec((T, tc), lambda j: (0, j))],
        out_specs=pl.BlockSpec((T, tc), lambda j: (0, j)), out_shape=_sds((T, emix), MM_DTYPE),
        compiler_params=_cp(("parallel",)),
    )(proj)


def pool_bwd(name, dpooled, dproj, T, emix):
    pg, tc, R = _pool_tiles(T, emix)
    per_g = pg // tc

    def body(d_ref, _, o_ref):
        g = pl.program_id(0) // per_g
        for gi, w in enumerate(POOL_WINDOWS):
            @pl.when(g == gi)
            def _():
                n = R + POOL_HALO
                for ci in range(T // R):
                    r0 = ci * R
                    cur = d_ref[r0:r0 + R, :]
                    halo = jnp.zeros((POOL_HALO, tc), F32) if ci == T // R - 1 else d_ref[r0 + R:r0 + n, :]
                    t = r0 + lax.broadcasted_iota(jnp.int32, (n, 1), 0)
                    cnt = jnp.minimum(t + 1, w).astype(F32)
                    s = jnp.concatenate([cur, halo], axis=0) / cnt
                    for st in range(w.bit_length() - 1):
                        s = s + pltpu.roll(s, n - (1 << st), axis=0)
                    o_ref[r0:r0 + R, :] = (s[:R] - cur).astype(o_ref.dtype)

    return pl.pallas_call(
        body, name=name, grid=(emix // tc,),
        in_specs=[pl.BlockSpec((T, tc), lambda j: (0, j)), pl.BlockSpec(memory_space=pl.ANY)],
        out_specs=pl.BlockSpec((T, tc), lambda j: (0, j)), out_shape=_sds(dproj.shape, dproj.dtype),
        input_output_aliases={1: 0}, compiler_params=_cp(("parallel",)),
    )(dpooled, dproj)


def grp_fwd(name, pooled, wgrp, scale, proj, gate_off, E):
    T, emix = pooled.shape
    pg = emix // len(POOL_WINDOWS)
    tm = _tile(T, 512)
    tc = 256 if pg % 256 == 0 and gate_off % 256 == 0 else LANE
    per = pg // tc

    def epi(acc, s_ref, g_ref):
        gt = g_ref[...]
        return acc, (acc * s_ref[...]) * (gt * _sig(gt))

    col = pl.BlockSpec((tm, tc), lambda n, i: (i, n))
    return _mm(name, (emix // tc, T // tm), ("parallel", "parallel"), [pooled, wgrp, scale, proj],
               [pl.BlockSpec((tm, pg), lambda n, i: (i, n // per)), pl.BlockSpec((None, pg, tc), lambda n, i: (n // per, 0, n % per)),
                pl.BlockSpec((1, tc), lambda n, i: (0, n)), pl.BlockSpec((tm, tc), lambda n, i: (i, gate_off // tc + n))],
               [_sds((T, emix), F32), _sds((T, E), MM_DTYPE)], [col, col], NN, epi=epi)


def grp_bwd_x(name, dy, wgrp):
    T, emix = dy.shape
    pg = emix // len(POOL_WINDOWS)
    tm = _tile(T, 512)
    return _mm(name, (len(POOL_WINDOWS), T // tm), ("parallel", "parallel"), [dy, wgrp],
               [pl.BlockSpec((tm, pg), lambda g, i: (i, g)), pl.BlockSpec((None, pg, pg), lambda g, i: (g, 0, 0))],
               [_sds((T, emix), F32)], [pl.BlockSpec((tm, pg), lambda g, i: (i, g))], NT)[0]


def grp_bwd_w(name, pooled, dy):
    T, emix = dy.shape
    ng = len(POOL_WINDOWS)
    pg = emix // ng
    col = pl.BlockSpec((T, pg), lambda g: (0, g))
    return _mm(name, (ng,), ("parallel",), [pooled, dy], [col, col],
               [_sds((ng, pg, pg), COMM_DTYPE)], [pl.BlockSpec((None, pg, pg), lambda g: (g, 0, 0))], TN)[0]


def gate_bwd_pool(name, dbranch, y, scale, proj, gate_off, n_proj):
    T, emix = y.shape
    tm, tc = _tile(T, 256), _tile(emix, 512)
    assert gate_off % tc == 0

    def body(db_ref, y_ref, s_ref, g_ref, dy_ref, dg_ref, ds_ref):
        db, yv, sc, gt = db_ref[...], y_ref[...], s_ref[...], g_ref[...]
        sg = _sig(gt)
        dmix = db * (gt * sg)
        dy_ref[...] = (dmix * sc).astype(dy_ref.dtype)
        dg_ref[...] = (db * (yv * sc) * (sg * (1.0 + gt * (1.0 - sg)))).astype(dg_ref.dtype)

        @pl.when(pl.program_id(1) == 0)
        def _():
            ds_ref[...] = jnp.zeros_like(ds_ref)

        ds_ref[...] += jnp.sum(dmix * yv, axis=0, keepdims=True)

    blk = pl.BlockSpec((tm, tc), lambda j, i: (i, j))
    vec = pl.BlockSpec((1, tc), lambda j, i: (0, j))
    gate = pl.BlockSpec((tm, tc), lambda j, i: (i, gate_off // tc + j))
    return pl.pallas_call(
        body, name=name, grid=(emix // tc, T // tm), in_specs=[blk, blk, vec, gate],
        out_specs=[blk, gate, vec],
        out_shape=[_sds((T, emix), MM_DTYPE), _sds((T, n_proj), MM_DTYPE), _sds((1, emix), F32)],
        compiler_params=_cp(("parallel", "arbitrary")),
    )(dbranch, y, scale, proj)


def attn_fwd(name, proj, kv, branch, q_off, gate_off, tm=512):
    T = proj.shape[0]
    M, two_eca = kv.shape
    eca = two_eca // 2
    hd = eca // CA_HEADS
    E = branch.shape[1]
    tm = _tile(T, tm)
    scale = hd ** -0.5

    def body(q_ref, g_ref, k_ref, v_ref, _, ca_ref, br_ref):
        s = _dot(q_ref[...], k_ref[...], NT) * scale
        s = s - jnp.max(s, axis=-1, keepdims=True)
        e = jnp.exp(s)
        p = e / jnp.sum(e, axis=-1, keepdims=True)
        o = _dot(p, v_ref[...], NN)
        gt = g_ref[...]
        ca_ref[...] = o
        br_ref[...] = (o * (gt * _sig(gt))).astype(br_ref.dtype)

    return pl.pallas_call(
        body, name=name, grid=(CA_HEADS, T // tm),
        in_specs=[pl.BlockSpec((tm, hd), lambda h, i: (i, q_off // hd + h)),
                  pl.BlockSpec((tm, hd), lambda h, i: (i, (gate_off + E - eca) // hd + h)),
                  pl.BlockSpec((M, hd), lambda h, i: (0, h)), pl.BlockSpec((M, hd), lambda h, i: (0, CA_HEADS + h)),
                  pl.BlockSpec(memory_space=pl.ANY)],
        out_specs=[pl.BlockSpec((tm, hd), lambda h, i: (i, h)), pl.BlockSpec((tm, hd), lambda h, i: (i, (E - eca) // hd + h))],
        out_shape=[_sds((T, eca), F32), _sds(branch.shape, branch.dtype)],
        input_output_aliases={4: 1}, compiler_params=_cp(("parallel", "parallel")),
    )(proj, proj, kv, kv, branch)


def attn_bwd(name, proj, kv, ca, dbranch, dproj, q_off, gate_off, tm=512):
    T = proj.shape[0]
    M, two_eca = kv.shape
    eca = two_eca // 2
    hd = eca // CA_HEADS
    E = dbranch.shape[1]
    tm = _tile(T, tm)
    scale = hd ** -0.5
    q_blk, g_blk = q_off // hd, (gate_off + E - eca) // hd

    def body(q_ref, g_ref, k_ref, v_ref, ca_ref, db_ref, _, dp_ref, dk_ref, dv_ref, dgate_buf):
        first_rows, part = pl.program_id(1) == 0, pl.program_id(2)

        @pl.when(part == 0)
        def _():
            q, k, v = q_ref[...], k_ref[...], v_ref[...]
            s = _dot(q, k, NT) * scale
            s = s - jnp.max(s, axis=-1, keepdims=True)
            e = jnp.exp(s)
            p = e / jnp.sum(e, axis=-1, keepdims=True)
            gt, db = g_ref[...], db_ref[...]
            sg = _sig(gt)
            do = db * (gt * sg)
            dgate_buf[...] = (db * ca_ref[...] * (sg * (1.0 + gt * (1.0 - sg)))).astype(dgate_buf.dtype)
            dpr = _dot(do, v, NT)
            ds = p * (dpr - jnp.sum(dpr * p, axis=-1, keepdims=True)) * scale
            dp_ref[...] = _dot(ds, k, NN).astype(dp_ref.dtype)

            @pl.when(first_rows)
            def _():
                dk_ref[...] = jnp.zeros_like(dk_ref)
                dv_ref[...] = jnp.zeros_like(dv_ref)

            dk_ref[...] += _dot(ds, q, TN)
            dv_ref[...] += _dot(p, do, TN)

        @pl.when(part == 1)
        def _():
            dp_ref[...] = dgate_buf[...]

    row = lambda blk: pl.BlockSpec((tm, hd), lambda h, i, p: (i, blk + h))
    out_dp = pl.BlockSpec((tm, hd), lambda h, i, p: (i, jnp.where(p == 0, q_blk, g_blk) + h))
    outs = pl.pallas_call(
        body, name=name, grid=(CA_HEADS, T // tm, 2),
        in_specs=[row(q_blk), row(g_blk), pl.BlockSpec((M, hd), lambda h, i, p: (0, h)),
                  pl.BlockSpec((M, hd), lambda h, i, p: (0, CA_HEADS + h)), row(0), row((E - eca) // hd),
                  pl.BlockSpec(memory_space=pl.ANY)],
        out_specs=[out_dp, pl.BlockSpec((M, hd), lambda h, i, p: (0, h)), pl.BlockSpec((M, hd), lambda h, i, p: (0, h))],
        out_shape=[_sds(dproj.shape, dproj.dtype), _sds((M, eca), F32), _sds((M, eca), F32)],
        scratch_shapes=[pltpu.VMEM((tm, hd), dproj.dtype)],
        input_output_aliases={6: 0}, compiler_params=_cp(("parallel", "arbitrary", "arbitrary")),
    )(proj, proj, kv, kv, ca, dbranch, dproj)
    return outs[0], jnp.concatenate([outs[1], outs[2]], axis=1)


def _split3(v):
    hi = v.astype(BF16)
    r1 = v - hi.astype(F32)
    mid = r1.astype(BF16)
    lo = (r1 - mid.astype(F32)).astype(BF16)
    return hi, mid, lo


def _tri_dot(tri, v):
    hi, mid, lo = _split3(v)
    d = functools.partial(lax.dot_general, dimension_numbers=NN, preferred_element_type=F32)
    return d(tri, hi) + d(tri, mid) + d(tri, lo)


def _tri_masks():
    C = HG_CHUNK
    ri = lax.broadcasted_iota(jnp.int32, (C, C), 0)
    ci = lax.broadcasted_iota(jnp.int32, (C, C), 1)
    return ri >= ci, jnp.where(ri >= ci, 1.0, 0.0).astype(BF16), jnp.where(ri <= ci, 1.0, 0.0).astype(BF16)


def _lower_bound(lb2):
    l0, l1 = lb2[0:1, :], lb2[1:2, :]
    m = jnp.maximum(l0, l1)
    e0, e1 = jnp.exp(l0 - m), jnp.exp(l1 - m)
    sm0, sm1 = e0 / (e0 + e1), e1 / (e0 + e1)
    return (sm0 + sm1) - sm0, sm0, sm1


def _hg_chunk_fwd(qi, fi, lb, causal, tril):
    sig = _sig(fi)
    f = lb + (1.0 - lb) * sig
    k = 1.0 - f
    b = _tri_dot(tril, jnp.log(f))
    last = lax.broadcasted_iota(jnp.int32, b.shape, 0) == HG_CHUNK - 1
    bl = jnp.sum(jnp.where(last, b, 0.0), axis=0, keepdims=True)
    eb, enb, ete = jnp.exp(b), jnp.exp(-b), jnp.exp(bl - b)
    sq = _sig(qi)
    q = (qi * sq) * (HG_HEAD_DIM ** -0.5)
    q_dec, k_inv, k_te = q * eb, k * enb, k * ete
    att = jnp.where(causal, _dot(q_dec, k_inv, NT), 0.0)
    return dict(sig=sig, f=f, k=k, eb=eb, enb=enb, ete=ete, dcy=jnp.exp(bl), sq=sq, q_dec=q_dec, k_inv=k_inv, k_te=k_te,
                att=att, last=last)


def _hg_blocking(H, T):
    hb = 4 if H % 4 == 0 else (2 if H % 2 == 0 else 1)
    return hb, _tile(T, 1024)


def hgrn_fwd(name, proj, lb2, norm_g, emix, gate_off, E):
    T = proj.shape[0]
    hd, C = HG_HEAD_DIM, HG_CHUNK
    H, N = emix // hd, T // C
    hb, ts = _hg_blocking(H, T)
    W, nc = hb * hd, ts // C

    def body(q_ref, f_ref, i_ref, g_ref, lb_ref, ng_ref, o_ref, st_ref, br_ref, state):
        @pl.when(pl.program_id(1) == 0)
        def _():
            state[...] = jnp.zeros_like(state)

        causal, tril, _ = _tri_masks()
        lb, _, _ = _lower_bound(lb_ref[...])
        ng = ng_ref[...]

        def chunk(n, carry):
            rows = pl.ds(pl.multiple_of(n * C, C), C)
            for h in range(hb):
                cols = slice(h * hd, (h + 1) * hd)
                v = i_ref[rows, cols]
                c = _hg_chunk_fwd(q_ref[rows, cols], f_ref[rows, cols], lb[:, cols], causal, tril)
                st = state[h]
                o = _dot(c["att"], v, NN) + _dot(c["q_dec"], st, NT)
                st_ref[h, n] = st
                o_ref[rows, cols] = o
                r = lax.rsqrt(jnp.mean(o * o, axis=-1, keepdims=True) + EPS)
                gt = g_ref[rows, cols]
                br_ref[rows, cols] = (((o * r) * ng[:, cols]) * (gt * _sig(gt))).astype(br_ref.dtype)
                state[h] = st * c["dcy"] + _dot(v, c["k_te"], TN)
            return carry

        lax.fori_loop(0, nc, chunk, 0, unroll=4)

    col = lambda off: pl.BlockSpec((ts, W), lambda h, s: (s, off // W + h))
    vec = lambda r: pl.BlockSpec((r, W), lambda h, s: (0, h))
    return pl.pallas_call(
        body, name=name, grid=(H // hb, T // ts),
        in_specs=[col(0), col(emix), col(2 * emix), col(gate_off), vec(2), vec(1)],
        out_specs=[col(0), pl.BlockSpec((hb, nc, hd, hd), lambda h, s: (h, s, 0, 0)), col(0)],
        out_shape=[_sds((T, emix), F32), _sds((H, N, hd, hd), F32), _sds((T, E), MM_DTYPE)],
        scratch_shapes=[pltpu.VMEM((hb, hd, hd), F32)],
        compiler_params=_cp(("parallel", "arbitrary")),
    )(proj, proj, proj, proj, lb2, norm_g)


def hgrn_bwd(name, proj, lb2, norm_g, o_pre, states, dbranch, emix, gate_off):
    T, n_proj = proj.shape
    hd, C = HG_HEAD_DIM, HG_CHUNK
    H, N = emix // hd, T // C
    hb, ts = _hg_blocking(H, T)
    W, nc, ns = hb * hd, ts // C, T // ts
    qscale = HG_HEAD_DIM ** -0.5

    def body(q_ref, f_ref, i_ref, g_ref, lb_ref, ng_ref, o_ref, st_ref, db_ref, dp_ref, dlb_ref, dng_ref, stash, dst_ref, acc):
        first_block, part = pl.program_id(1) == 0, pl.program_id(2)

        @pl.when(part == 0)
        def _():
            @pl.when(first_block)
            def _():
                dst_ref[...] = jnp.zeros_like(dst_ref)
                acc[...] = jnp.zeros_like(acc)

            causal, tril, triu = _tri_masks()
            lb, sm0, sm1 = _lower_bound(lb_ref[...])
            ng = ng_ref[...]

            def chunk(j, carry):
                n = nc - 1 - j
                rows = pl.ds(pl.multiple_of(n * C, C), C)
                for h in range(hb):
                    cols = slice(h * hd, (h + 1) * hd)
                    dst = dst_ref[h]
                    qi, v, gt = q_ref[rows, cols], i_ref[rows, cols], g_ref[rows, cols]
                    c = _hg_chunk_fwd(qi, f_ref[rows, cols], lb[:, cols], causal, tril)
                    st = st_ref[h, n]
                    o, db, ngh = o_ref[rows, cols], db_ref[rows, cols], ng[:, cols]
                    r = lax.rsqrt(jnp.mean(o * o, axis=-1, keepdims=True) + EPS)
                    sg = _sig(gt)
                    dmix = db * (gt * sg)
                    dgate = db * ((o * r) * ngh) * (sg * (1.0 + gt * (1.0 - sg)))
                    dyn = dmix * ngh
                    do = r * dyn - o * ((r * r * r) * jnp.mean(dyn * o, axis=-1, keepdims=True))
                    acc[1:2, cols] += jnp.sum(dmix * (o * r), axis=0, keepdims=True)
                    datt = jnp.where(causal, _dot(do, v, NT), 0.0)
                    dq_dec = _dot(datt, c["k_inv"], NN) + _dot(do, st, NN)
                    dk_inv = _dot(datt, c["q_dec"], TN)
                    dv = _dot(c["att"], do, TN) + _dot(c["k_te"], dst, NT)
                    dk_te = _dot(v, dst, NN)
                    ddcy = jnp.sum(dst * st, axis=0, keepdims=True)
                    dst_ref[h] = _dot(do, c["q_dec"], TN) + dst * c["dcy"]
                    t_te = dk_te * c["k_te"]
                    dbv = dq_dec * c["q_dec"] - dk_inv * c["k_inv"] - t_te
                    dbl = jnp.sum(t_te, axis=0, keepdims=True) + ddcy * c["dcy"]
                    dbv = dbv + jnp.where(c["last"], dbl, 0.0)
                    dlf = _tri_dot(triu, dbv)
                    dk = dk_inv * c["enb"] + dk_te * c["ete"]
                    df = dlf / c["f"] - dk
                    sig, sq = c["sig"], c["sq"]
                    acc[0:1, cols] += jnp.sum(df * (1.0 - sig), axis=0, keepdims=True)
                    dqi = (dq_dec * c["eb"]) * qscale * (sq * (1.0 + qi * (1.0 - sq)))
                    dp_ref[rows, cols] = dqi.astype(dp_ref.dtype)
                    stash[0, rows, cols] = (df * (1.0 - lb[:, cols]) * (sig * (1.0 - sig))).astype(stash.dtype)
                    stash[1, rows, cols] = dv.astype(stash.dtype)
                    stash[2, rows, cols] = dgate.astype(stash.dtype)
                return carry

            lax.fori_loop(0, nc, chunk, 0, unroll=4)
            t = acc[0:1, :] * (sm0 * sm1)
            dlb_ref[...] = jnp.concatenate([-t, t], axis=0)
            dng_ref[...] = acc[1:2, :]

        for pi in range(3):
            @pl.when(part == pi + 1)
            def _():
                dp_ref[...] = stash[pi]

    col = lambda off: pl.BlockSpec((ts, W), lambda h, s, p: (ns - 1 - s, off // W + h))
    vec = lambda r: pl.BlockSpec((r, W), lambda h, s, p: (0, h))
    n_mix = emix // W
    out_col = pl.BlockSpec((ts, W), lambda h, s, p: (ns - 1 - s, jnp.where(p < 3, p * n_mix, gate_off // W) + h))
    return pl.pallas_call(
        body, name=name, grid=(H // hb, ns, 4),
        in_specs=[col(0), col(emix), col(2 * emix), col(gate_off), vec(2), vec(1), col(0),
                  pl.BlockSpec((hb, nc, hd, hd), lambda h, s, p: (h, ns - 1 - s, 0, 0)), col(0)],
        out_specs=[out_col, vec(2), vec(1)],
        out_shape=[_sds((T, n_proj), MM_DTYPE), _sds((2, emix), F32), _sds((1, emix), F32)],
        scratch_shapes=[pltpu.VMEM((3, ts, W), MM_DTYPE), pltpu.VMEM((hb, hd, hd), F32), pltpu.VMEM((8, W), F32)],
        compiler_params=_cp(("parallel", "arbitrary", "arbitrary")),
    )(proj, proj, proj, proj, lb2, norm_g, o_pre, states, dbranch)


class WholeWeights:
    def __init__(self, wkv, wout, wpin, wgrp, whin, hgrn_norm_g):
        self.w = dict(wkv=wkv, wout=wout, wpin=wpin, wgrp=wgrp, whin=whin, hnorm=hgrn_norm_g)
        self.grads = {}

    def first_weights(self):
        return self.w["wpin"], self.w["hnorm"], 0.0

    def after_proj0(self, after):
        pass

    def layer0_weights(self, after):
        return self.w["wgrp"], self.w["wkv"][0], self.w["wout"][0]

    def after_attn0(self, after):
        pass

    def layer1_weights(self, after):
        return self.w["whin"]

    def after_proj1(self, after):
        pass

    def layer1_rest(self, after):
        return self.w["wkv"][1], self.w["wout"][1]

    def grads_ready(self, layer, grads):
        self.grads[layer] = grads
        return 0.0

    def grads_reduce(self, layer, after):
        return 0.0


def run_step(x, mem, target, norm_g, mem_norm_g, pool_scale, hgrn_lb, final_g, comm):
    T, D = x.shape
    mem_n = rmsnorm_fwd("mem_norm", mem, mem_norm_g)
    wpin, hgrn_norm_g, zero = comm.first_weights()
    h0 = rmsnorm_fwd("norm0", x, norm_g[0] + zero)
    proj0 = proj_fwd("proj0", h0, wpin)
    comm.after_proj0(proj0)
    n0 = proj0.shape[1]
    E = n0 // 2
    eca = E // 4
    emix = E - eca
    q0, g0 = emix, emix + eca
    q1, g1 = 3 * emix, 3 * emix + eca
    pooled = pool_fwd("pool_fwd", proj0, T, emix)
    wgrp, wkv0, wout0 = comm.layer0_weights(pooled)
    y0, branch0 = grp_fwd("grp_fwd", pooled, wgrp, pool_scale, proj0, g0, E)
    kv0 = mm_nn("kv0", mem_n, wkv0, MM_DTYPE)
    ca0, branch0 = attn_fwd("attn_fwd0", proj0, kv0, branch0, q0, g0)
    comm.after_attn0(ca0)
    x1 = out_fwd("out0", branch0, wout0, x)
    h1 = rmsnorm_fwd("norm1", x1, norm_g[1])
    whin = comm.layer1_weights(h1)
    proj1 = proj_fwd("proj1", h1, whin)
    comm.after_proj1(proj1)
    o1, states, branch1 = hgrn_fwd("hgrn_fwd", proj1, hgrn_lb, hgrn_norm_g, emix, g1, E)
    wkv1, wout1 = comm.layer1_rest(o1)
    kv1 = mm_nn("kv1", mem_n, wkv1, MM_DTYPE)
    ca1, branch1 = attn_fwd("attn_fwd1", proj1, kv1, branch1, q1, g1)
    x2 = out_fwd("out1", branch1, wout1, x1)
    loss, dx2, dx2b, g_final = final_loss_bwd("final", x2, final_g, target)
    g_wout1 = mm_tn("gwout1", branch1, dx2b, COMM_DTYPE)
    dbranch1 = mm_nt("dbranch1", dx2b, wout1, F32)
    dproj1, g_lb, g_hnorm = hgrn_bwd("hgrn_bwd", proj1, hgrn_lb, hgrn_norm_g, o1, states, dbranch1, emix, g1)
    dproj1, dkv1 = attn_bwd("attn_bwd1", proj1, kv1, ca1, dbranch1, dproj1, q1, g1)
    g_wkv1 = mm_tn("gwkv1", mem_n, dkv1, COMM_DTYPE)
    dmem_n = mm_nt("dmem1", dkv1, wkv1, F32)
    g_whin = proj_bwd_w("gwhin", h1, dproj1, whin.shape[0])
    zero = comm.grads_ready(1, dict(w_kv=g_wkv1, w_out=g_wout1, hgrn_w_in=g_whin))
    dh1 = proj_bwd_x("dh1", dproj1, whin)
    zero = zero + comm.grads_reduce(1, dh1)
    dx1, dx1b, g_norm1 = rmsnorm_bwd("norm1_bwd", x1, norm_g[1] + zero, dh1, res=dx2)
    g_wout0 = mm_tn("gwout0", branch0, dx1b, COMM_DTYPE)
    comm.grads_ready("0a", dict(w_out=g_wout0))
    dbranch0 = mm_nt("dbranch0", dx1b, wout0, F32)
    zero = comm.grads_reduce("0a", dbranch0)
    dy0, dproj0, g_scale = gate_bwd_pool("gate_bwd0", dbranch0, y0, pool_scale + zero, proj0, g0, n0)
    g_wgrp = grp_bwd_w("gwgrp", pooled, dy0)
    dpooled = grp_bwd_x("dpooled", dy0, wgrp)
    dproj0 = pool_bwd("pool_bwd", dpooled, dproj0, T, emix)
    dproj0, dkv0 = attn_bwd("attn_bwd0", proj0, kv0, ca0, dbranch0, dproj0, q0, g0)
    g_wkv0 = mm_tn("gwkv0", mem_n, dkv0, COMM_DTYPE)
    dmem_n = mm_nt("dmem0", dkv0, wkv0, F32, res=dmem_n)
    g_wpin = proj_bwd_w("gwpin", h0, dproj0, wpin.shape[0])
    zero = comm.grads_ready("0b", dict(w_kv=g_wkv0, pool_w_in=g_wpin, pool_w_grp=g_wgrp))
    dh0 = proj_bwd_x("dh0", dproj0, wpin)
    grad_x, _, g_norm0 = rmsnorm_bwd("norm0_bwd", x, norm_g[0] + zero, dh0, res=dx1)
    _, _, g_mem = rmsnorm_bwd("mem_norm_bwd", mem, mem_norm_g, dmem_n)

    small = dict(norm_g=jnp.concatenate([g_norm0, g_norm1], axis=0), mem_norm_g=g_mem[0], pool_scale=g_scale,
                 hgrn_lb=g_lb, hgrn_norm_g=g_hnorm, final_g=g_final[0])
    return loss[0, 0], grad_x, small


def _position():
    return lax.axis_index("x"), lax.axis_index("y"), lax.axis_index("c")


def _slot(p):
    return 4 * p[0] + 2 * p[1] + p[2]


def all_gather_blocks(name, blocks):
    n = len(blocks)

    def body(*refs):
        ins, outs, token = refs[:n], refs[n:2 * n], refs[2 * n]
        send_sems, recv_sems, local_sems = refs[2 * n + 1:]
        token[...] = jnp.zeros_like(token)
        x, y, c = _position()
        me, sibling = (x, y, c), (x, y, 1 - c)
        chips = _other_chips(x, y)

        def copy(t, k, block, to, src=None):
            dst = outs[t].at[_slot(block)]
            return pltpu.make_async_remote_copy(src_ref=dst if src is None else src, dst_ref=dst, send_sem=send_sems.at[t, k],
                                                recv_sem=recv_sems.at[t, k], device_id=to, device_id_type=MESH)

        mine = [pltpu.make_async_copy(ins[t], outs[t].at[_slot(me)], local_sems.at[t]) for t in range(n)]
        for cp in mine:
            cp.start()
        sends = []
        for t in range(n):
            sends.append(copy(t, 0, me, sibling, src=ins[t]))
            sends += [copy(t, 1 + j, me, (*chip, c), src=ins[t]) for j, chip in enumerate(chips)]
        for cp in sends:
            cp.start()
        for t in range(n):
            for j, chip in enumerate(chips):
                copy(t, 1 + j, (*chip, c), me).wait_recv()
                passed = copy(t, 4 + j, (*chip, c), sibling)
                passed.start()
                sends.append(passed)
        for t in range(n):
            copy(t, 0, sibling, me).wait_recv()
            for j, chip in enumerate(chips):
                copy(t, 4 + j, (*chip, 1 - c), me).wait_recv()
        for cp in sends:
            cp.wait_send()
        for cp in mine:
            cp.wait()

    any_spec = pl.BlockSpec(memory_space=pl.ANY)
    return pl.pallas_call(
        body, name=name, in_specs=[any_spec] * n, out_specs=[any_spec] * n + [pl.BlockSpec(memory_space=pltpu.VMEM)],
        out_shape=[_sds((N_DEV,) + b.shape, b.dtype) for b in blocks] + [_sds((8, LANE), F32)],
        scratch_shapes=[pltpu.SemaphoreType.DMA((n, 7)), pltpu.SemaphoreType.DMA((n, 7)), pltpu.SemaphoreType.DMA((n,))],
        compiler_params=pltpu.CompilerParams(has_side_effects=True),
    )(*blocks)


_HBM = pl.BlockSpec(memory_space=pltpu.HBM)
_SEM = pl.BlockSpec(memory_space=pltpu.SEMAPHORE)
_EFFECT = pltpu.SideEffectType.DATAFLOW_SIDE_EFFECTING


def split_start(name, bufs, n_copies, build):
    n = len(bufs)

    def body(*refs):
        for cp in build(refs[:n], refs[n], refs[n + 1]):
            cp.start()
        refs[-1][...] = jnp.zeros_like(refs[-1])

    outs = pl.pallas_call(
        body, name=name, in_specs=[_HBM] * n,
        out_shape=(pltpu.SemaphoreType.DMA((n_copies,)), pltpu.SemaphoreType.DMA((n_copies,)),
                   *[pltpu.HBM(b.shape, b.dtype) for b in bufs], _sds((8, LANE), F32)),
        out_specs=(_SEM, _SEM, *[_HBM] * n, pl.BlockSpec(memory_space=pltpu.VMEM)),
        input_output_aliases={i: 2 + i for i in range(n)},
        compiler_params=pltpu.CompilerParams(has_side_effects=_EFFECT),
    )(*[pltpu.with_memory_space_constraint(b, pltpu.HBM) for b in bufs])
    return (outs[0], outs[1]), list(outs[2:2 + n]), outs[-1]


def split_wait(name, sems, bufs, build, after):
    n = len(bufs)

    def body(*refs):
        for cp in build(refs[:n], refs[n], refs[n + 1]):
            cp.wait()

    return list(pl.pallas_call(
        body, name=name, in_specs=[_HBM] * n + [_SEM, _SEM, pl.BlockSpec(memory_space=pl.ANY)],
        out_shape=[pltpu.HBM(b.shape, b.dtype) for b in bufs], out_specs=[_HBM] * n,
        input_output_aliases={i: i for i in range(n)},
        compiler_params=pltpu.CompilerParams(has_side_effects=_EFFECT),
    )(*bufs, sems[0], sems[1], after))


def _remote(src, dst, send_sems, recv_sems, k, to):
    return pltpu.make_async_remote_copy(src_ref=src, dst_ref=dst, send_sem=send_sems.at[k], recv_sem=recv_sems.at[k],
                                        device_id=to, device_id_type=MESH)


def _other_chips(x, y):
    return [(1 - x, y), (x, 1 - y), (1 - x, 1 - y)]


def _gather_cross(n):
    def build(refs, ss, rs):
        x, y, c = _position()
        me = _slot((x, y, c))
        targets = [(x, y, 1 - c)] + [(*chip, c) for chip in _other_chips(x, y)]
        return [_remote(refs[t], refs[n + t].at[me], ss, rs, 4 * t + k, to) for t in range(n) for k, to in enumerate(targets)]
    return build


def _gather_pass(n):
    def build(refs, ss, rs):
        x, y, c = _position()
        cps = []
        for t in range(n):
            for j, chip in enumerate(_other_chips(x, y)):
                blk = refs[t].at[_slot((*chip, c))]
                cps.append(_remote(blk, blk, ss, rs, 3 * t + j, (x, y, 1 - c)))
        return cps
    return build


def _reduce_pair(n):
    def build(refs, ss, rs):
        x, y, c = _position()
        return [_remote(refs[t].at[j, 1 - c], refs[n + t].at[j], ss, rs, 4 * t + j, (x, y, 1 - c))
                for t in range(n) for j in range(4)]
    return build


def _reduce_cross(n):
    def build(refs, ss, rs):
        x, y, c = _position()
        return [_remote(refs[t].at[2 * chip[0] + chip[1]], refs[n + t].at[r], ss, rs, 3 * t + r, (*chip, c))
                for t in range(n) for r, chip in enumerate(_other_chips(x, y))]
    return build


def pair_add(name, part, landed, tr=256):
    _, _, R, C = part.shape
    tr = _row_tile(R, tr)

    def body(_, p_ref, l_ref, o_ref):
        o_ref[...] = (p_ref[...].astype(F32) + l_ref[...].astype(F32)).astype(o_ref.dtype)

    blk = pl.BlockSpec((None, tr, C), lambda j, i, core: (j, i, 0))
    core = lax.axis_index("c").astype(jnp.int32).reshape(1)
    return pl.pallas_call(
        body, name=name, out_shape=_sds((4, R, C), part.dtype), compiler_params=_cp(("parallel", "parallel")),
        grid_spec=pltpu.PrefetchScalarGridSpec(
            num_scalar_prefetch=1, grid=(4, R // tr),
            in_specs=[pl.BlockSpec((None, None, tr, C), lambda j, i, core: (j, core[0], i, 0)), blk], out_specs=blk),
    )(core, part, landed)


def _row_tile(R, pref):
    if R <= pref:
        return R
    t = (pref // 16) * 16
    while R % t:
        t -= 16
    return t


def all_reduce_small(name, vec):
    R = vec.shape[0]

    def body(v_ref, sum_ref, land_ref, send_sems, recv_sems):
        x, y, c = _position()
        me = (x, y, c)
        flip = lambda v, bit: 1 - v if bit else v
        peers = [(flip(x, k & 4), flip(y, k & 2), flip(c, k & 1)) for k in range(1, N_DEV)]
        land_ref[_slot(me)] = v_ref[...]
        sends = [pltpu.make_async_remote_copy(src_ref=v_ref, dst_ref=land_ref.at[_slot(me)], send_sem=send_sems.at[k],
                                              recv_sem=recv_sems.at[k], device_id=peer, device_id_type=MESH)
                 for k, peer in enumerate(peers)]
        for cp in sends:
            cp.start()
        for k, peer in enumerate(peers):
            pltpu.make_async_remote_copy(src_ref=v_ref, dst_ref=land_ref.at[_slot(peer)], send_sem=send_sems.at[k],
                                         recv_sem=recv_sems.at[k], device_id=peer, device_id_type=MESH).wait_recv()
        acc = land_ref[0]
        for s in range(1, N_DEV):
            acc = acc + land_ref[s]
        sum_ref[...] = acc
        for cp in sends:
            cp.wait_send()

    vm = pl.BlockSpec(memory_space=pltpu.VMEM)
    return pl.pallas_call(
        body, name=name, in_specs=[vm], out_specs=[vm, vm],
        out_shape=[_sds(vec.shape, F32), _sds((N_DEV,) + vec.shape, F32)],
        scratch_shapes=[pltpu.SemaphoreType.DMA((7,)), pltpu.SemaphoreType.DMA((7,))],
        compiler_params=pltpu.CompilerParams(has_side_effects=True),
    )(vec)[0]


def _adamw_math(g, w, m, v):
    m2 = ADAM_B1 * m + (1.0 - ADAM_B1) * g
    v2 = ADAM_B2 * v + (1.0 - ADAM_B2) * (g * g)
    m_hat = m2 / (1.0 - ADAM_B1 ** ADAM_STEP)
    v_hat = v2 / (1.0 - ADAM_B2 ** ADAM_STEP)
    return -ADAM_LR * (m_hat / (jnp.sqrt(v_hat) + ADAM_EPS) + ADAM_WD * w), m2, v2


def adamw_shard(name, chip_parts, landed, w, m, v, layer, prev=None, tr=128):
    L, R, C = w.shape
    tr = _row_tile(R, tr)
    n_in = 5 + (4 if prev is not None else 0)

    def body(*refs):
        own_ref, land_ref, w_ref, m_ref, v_ref = refs[1:6]
        g_ref, d_ref, nm_ref, nv_ref = refs[1 + n_in:]
        g = own_ref[...].astype(F32)
        for s in range(landed.shape[0]):
            g = g + land_ref[s].astype(F32)
        g_ref[...] = g
        d_ref[...], nm_ref[...], nv_ref[...] = _adamw_math(g, w_ref[...], m_ref[...], v_ref[...])

    lay = pl.BlockSpec((None, tr, C), lambda i, chip: (layer, i, 0))
    own = pl.BlockSpec((None, tr, C), lambda i, chip: (chip[0], i, 0))
    chip = (2 * lax.axis_index("x") + lax.axis_index("y")).astype(jnp.int32).reshape(1)
    ins = [chip_parts, landed, w, m, v] + (list(prev) if prev is not None else [])
    return pl.pallas_call(
        body, name=name, out_shape=[_sds((L, R, C), F32)] * 4,
        grid_spec=pltpu.PrefetchScalarGridSpec(
            num_scalar_prefetch=1, grid=(R // tr,),
            in_specs=[own, pl.BlockSpec((landed.shape[0], tr, C), lambda i, chip: (0, i, 0)), lay, lay, lay]
            + [pl.BlockSpec(memory_space=pl.ANY)] * (n_in - 5),
            out_specs=[lay] * 4),
        input_output_aliases={6 + i: i for i in range(n_in - 5)}, compiler_params=_cp(("parallel",)),
    )(chip, *ins)


def adamw_small(name, g, w, m, v):
    def body(g_ref, w_ref, m_ref, v_ref, d_ref, nm_ref, nv_ref):
        d_ref[...], nm_ref[...], nv_ref[...] = _adamw_math(g_ref[...], w_ref[...], m_ref[...], v_ref[...])

    return pl.pallas_call(body, name=name, out_shape=[_sds(w.shape, F32)] * 3)(g, w, m, v)


def _pack(vs):
    flat = jnp.concatenate([v.reshape(-1) for v in vs])
    return flat.reshape(-1, LANE)


def _unpack(packed, like):
    flat, out, off = packed.reshape(-1), [], 0
    for v in like:
        out.append(flat[off:off + v.size].reshape(v.shape))
        off += v.size
    return out


class MeshWeights:
    def __init__(self, w_kv, w_out, pool_w_in, pool_w_grp, hgrn_w_in, hgrn_norm_g):
        self.n_grp, self.pg_loc, self.pg = pool_w_grp.shape[1:]
        wpin, hnorm, token = all_gather_blocks("gather_first", [pool_w_in[0].astype(COMM_DTYPE), hgrn_norm_g])
        self.first = (wpin, hnorm.reshape(1, -1))
        zero = token[0, 0]
        cast = lambda a: (a + zero).astype(COMM_DTYPE)
        self.gather, self.tokens = {}, []
        for layer, blocks in ((0, [cast(pool_w_grp[0]), cast(w_kv[0]), cast(w_out[0])]),
                              (1, [cast(hgrn_w_in[0])]), (2, [cast(w_kv[1]), cast(w_out[1])])):
            me = _slot(_position())
            lands = [lax.dynamic_update_index_in_dim(lax.empty((N_DEV,) + b.shape, b.dtype), b, me, 0) for b in blocks]
            n = len(blocks)
            sems, bufs, token = split_start(f"gather{layer}_cross", blocks + lands, 4 * n, _gather_cross(n))
            self.gather[layer] = (sems, bufs)
            self.tokens.append(token[0, 0])
        self.reduce = {}

    def first_weights(self):
        return self.first[0], self.first[1], self.tokens[0] + self.tokens[1] + self.tokens[2]

    def _pass_on(self, layer, after):
        sems, bufs = self.gather[layer]
        n = len(bufs) // 2
        gathered = split_wait(f"gather{layer}_cross_wait", sems, bufs, _gather_cross(n), after)[n:]
        sems, bufs, _ = split_start(f"gather{layer}_pass", gathered, 3 * n, _gather_pass(n))
        self.gather[layer] = (sems, bufs)

    def _gathered(self, layer, after):
        sems, bufs = self.gather[layer]
        return split_wait(f"gather{layer}_pass_wait", sems, bufs, _gather_pass(len(bufs)), after)

    def after_proj0(self, after):
        self._pass_on(0, after)

    def layer0_weights(self, after):
        wgrp, wkv, wout = self._gathered(0, after)
        rows = lambda g: g.reshape(-1, g.shape[-1])
        return wgrp.transpose(1, 0, 2, 3).reshape(self.n_grp, self.pg, self.pg), rows(wkv), rows(wout)

    def after_attn0(self, after):
        self._pass_on(1, after)

    def layer1_weights(self, after):
        return self._gathered(1, after)[0]

    def after_proj1(self, after):
        self._pass_on(2, after)

    def layer1_rest(self, after):
        wkv, wout = self._gathered(2, after)
        rows = lambda g: g.reshape(-1, g.shape[-1])
        return rows(wkv), rows(wout)

    def grads_ready(self, layer, grads):
        parts = {}
        for nm, g in grads.items():
            if nm == "pool_w_grp":
                g = g.reshape(self.n_grp, N_DEV, self.pg_loc, self.pg).transpose(1, 0, 2, 3)
            parts[nm] = g.reshape(4, 2, -1, g.shape[-1])
        names, srcs = list(parts), list(parts.values())
        lands = [lax.empty((4,) + p.shape[2:], p.dtype) for p in srcs]
        sems, bufs, token = split_start(f"reduce{layer}_pair", srcs + lands, 4 * len(srcs), _reduce_pair(len(srcs)))
        self.reduce[layer] = (names, sems, bufs)
        return token[0, 0]

    def grads_reduce(self, layer, after):
        names, sems, bufs = self.reduce[layer]
        n = len(names)
        bufs = split_wait(f"reduce{layer}_pair_wait", sems, bufs, _reduce_pair(n), after)
        chip_parts = [pair_add(f"pair_add_{nm}{layer}", bufs[t], bufs[n + t]) for t, nm in enumerate(names)]
        lands = [lax.empty((3,) + p.shape[1:], p.dtype) for p in chip_parts]
        sems, bufs, token = split_start(f"reduce{layer}_cross", chip_parts + lands, 3 * n, _reduce_cross(n))
        self.reduce[layer] = (names, sems, bufs)
        return token[0, 0]

    def grads_done(self, layer, after):
        names, sems, bufs = self.reduce[layer]
        n = len(names)
        bufs = split_wait(f"reduce{layer}_cross_wait", sems, bufs, _reduce_cross(n), after)
        return {nm: (bufs[t], bufs[n + t]) for t, nm in enumerate(names)}


def kernel(x, mem, norm_g, mem_norm_g, w_kv, w_out, pool_w_in, pool_w_grp, pool_scale, hgrn_w_in, hgrn_lb, hgrn_norm_g, final_g, loss_target, m_norm_g, m_mem_norm_g, m_w_kv, m_w_out, m_pool_w_in, m_pool_w_grp, m_pool_scale, m_hgrn_w_in, m_hgrn_lb, m_hgrn_norm_g, m_final_g, v_norm_g, v_mem_norm_g, v_w_kv, v_w_out, v_pool_w_in, v_pool_w_grp, v_pool_scale, v_hgrn_w_in, v_hgrn_lb, v_hgrn_norm_g, v_final_g):
    comm = MeshWeights(w_kv, w_out, pool_w_in, pool_w_grp, hgrn_w_in, hgrn_norm_g)
    loss, grad_x, small = run_step(x[0], mem[0], loss_target[0], norm_g, mem_norm_g, pool_scale, hgrn_lb, final_g, comm)
    loss = lax.psum(loss, AXES)
    names = ("norm_g", "mem_norm_g", "pool_scale", "hgrn_lb", "hgrn_norm_g", "final_g")
    like = (norm_g, mem_norm_g, pool_scale, hgrn_lb, _sds((1, N_DEV * hgrn_norm_g.shape[1]), F32), final_g)
    reduced = _unpack(all_reduce_small("reduce_small", _pack([small[nm].reshape(lk.shape) for nm, lk in zip(names, like)])), like)
    n_loc = hgrn_norm_g.shape[1]
    comm.grads_reduce("0b", reduced[0])
    reduced[4] = lax.dynamic_slice(reduced[4], (0, _slot(_position()) * n_loc), (1, n_loc))
    ws = (norm_g, mem_norm_g, pool_scale, hgrn_lb, hgrn_norm_g, final_g)
    ms = (m_norm_g, m_mem_norm_g, m_pool_scale, m_hgrn_lb, m_hgrn_norm_g, m_final_g)
    vs = (v_norm_g, v_mem_norm_g, v_pool_scale, v_hgrn_lb, v_hgrn_norm_g, v_final_g)
    res = adamw_small("adamw_small", _pack(reduced), _pack(ws), _pack(ms), _pack(vs))
    outs = {nm: [reduced[i]] + [_unpack(r, ws)[i] for r in res] for i, nm in enumerate(names)}
    rows = lambda a: a.reshape(a.shape[0], -1, a.shape[-1])
    state = dict(w_kv=(w_kv, m_w_kv, v_w_kv), w_out=(w_out, m_w_out, v_w_out), pool_w_in=(pool_w_in, m_pool_w_in, v_pool_w_in),
                 pool_w_grp=(pool_w_grp, m_pool_w_grp, v_pool_w_grp), hgrn_w_in=(hgrn_w_in, m_hgrn_w_in, v_hgrn_w_in))
    big, after = {}, res[0]
    for group, layer in ((1, 1), ("0a", 0), ("0b", 0)):
        for nm, (chip_parts, landed) in comm.grads_done(group, after).items():
            w, m, v = state[nm]
            big[nm] = adamw_shard(f"adamw_{nm}{layer}", chip_parts, landed, rows(w), rows(m), rows(v),
                                  layer if w.shape[0] > 1 else 0, prev=big.get(nm))
            after = big[nm][0]
    for nm, (w, _, _) in state.items():
        outs[nm] = [r.reshape(w.shape) for r in big[nm]]
    order = ("norm_g", "mem_norm_g", "w_kv", "w_out", "pool_w_in", "pool_w_grp", "pool_scale", "hgrn_w_in", "hgrn_lb",
             "hgrn_norm_g", "final_g")
    return (loss, grad_x[None], *[outs[nm][0] for nm in order], *[outs[nm][1] for nm in order],
            *[outs[nm][2] for nm in order], *[outs[nm][3] for nm in order])
```

```python
import functools

import jax
import jax.numpy as jnp
from jax import lax
from jax.experimental import pallas as pl
from jax.experimental.pallas import tpu as pltpu

F32, BF16 = jnp.float32, jnp.bfloat16
MM_DTYPE = BF16
COMM_DTYPE = BF16
EPS = 1e-6
POOL_WINDOWS = (2, 4, 8, 16)
POOL_HALO = 16
HG_HEAD_DIM = 128
HG_CHUNK = 64
CA_HEADS = 4
N_DEV = 8
ADAM_LR, ADAM_B1, ADAM_B2, ADAM_EPS, ADAM_WD, ADAM_STEP = 0.001, 0.9, 0.999, 1e-08, 0.01, 10
VMEM_LIMIT = 48 << 20
VMEM_LIMIT_WIDE = 56 << 20
LANE = 128
MESH = pl.DeviceIdType.MESH
AXES = ("x", "y", "c")

NN = (((1,), (0,)), ((), ()))
NT = (((1,), (1,)), ((), ()))
TN = (((0,), (0,)), ((), ()))


def _cp(sem=None, vmem=VMEM_LIMIT):
    return pltpu.CompilerParams(dimension_semantics=sem, vmem_limit_bytes=vmem)


def _dot(a, b, dims):
    return lax.dot_general(a.astype(MM_DTYPE), b.astype(MM_DTYPE), dims, preferred_element_type=F32)


def _sig(v):
    return 1.0 / (1.0 + jnp.exp(-v))


def _tile(n, pref):
    if n <= pref:
        return n
    t = (pref // LANE) * LANE
    while n % t:
        t -= LANE
    return t


def _sds(shape, dtype):
    return jax.ShapeDtypeStruct(tuple(shape), dtype)


def _mm(name, grid, sem, ins, in_specs, out_shapes, out_specs, dims, epi=None, nk=1, acc_shape=None, aliases=None):
    n_in, n_out = len(ins), len(out_shapes)

    def body(*refs):
        a_ref, b_ref = refs[0], refs[1]
        extra = refs[2:n_in]
        outs = refs[n_in:n_in + n_out]
        p = _dot(a_ref[...], b_ref[...], dims)

        def finish(acc):
            res = epi(acc, *extra) if epi is not None else (acc,)
            for r, o_ref in zip(res, outs):
                o_ref[...] = r.astype(o_ref.dtype)

        if nk == 1:
            finish(p)
        else:
            acc_ref = refs[n_in + n_out]
            k = pl.program_id(len(grid) - 1)

            @pl.when(k == 0)
            def _():
                acc_ref[...] = p

            @pl.when(k > 0)
            def _():
                acc_ref[...] += p

            @pl.when(k == nk - 1)
            def _():
                finish(acc_ref[...])

    scratch = [pltpu.VMEM(acc_shape, F32)] if nk > 1 else []
    return pl.pallas_call(
        body, name=name, grid=grid, in_specs=in_specs, out_specs=out_specs, out_shape=out_shapes,
        scratch_shapes=scratch, compiler_params=_cp(sem), input_output_aliases=aliases or {},
    )(*ins)


def proj_fwd(name, h, wblk, tm=512):
    T, D = h.shape
    nblk, _, nb = wblk.shape
    tm = _tile(T, tm)
    return _mm(name, (nblk, T // tm), ("parallel", "parallel"), [h, wblk],
               [pl.BlockSpec((tm, D), lambda j, i: (i, 0)), pl.BlockSpec((None, D, nb), lambda j, i: (j, 0, 0))],
               [_sds((T, nblk * nb), F32)], [pl.BlockSpec((tm, nb), lambda j, i: (i, j))], NN)[0]


def proj_bwd_w(name, h, dproj, nblk, tm=512, tn=1024):
    T, D = h.shape
    N = dproj.shape[1]
    nb = N // nblk
    tm, tn = _tile(D, tm), _tile(nb, tn)
    per = nb // tn
    return _mm(name, (D // tm, N // tn), ("parallel", "parallel"), [h, dproj],
               [pl.BlockSpec((T, tm), lambda i, n: (0, i)), pl.BlockSpec((T, tn), lambda i, n: (0, n))],
               [_sds((nblk, D, nb), COMM_DTYPE)], [pl.BlockSpec((None, tm, tn), lambda i, n: (n // per, i, n % per))], TN)[0]


def proj_bwd_x_norm(name, dproj, wblk, x, g, res, tm=512):
    T, N = dproj.shape
    nblk, D, nb = wblk.shape
    tm = _tile(T, tm)
    halves = 2 if nb % (2 * LANE) == 0 else 1
    tk, nk = nb // halves, nblk * halves
    sub = _row_tile(tm, 128)

    def body(a_ref, b_ref, x_ref, g_ref, r_ref, dx_ref, dxb_ref, dg_ref, acc):
        first_rows, k = pl.program_id(0) == 0, pl.program_id(1)
        p = _dot(a_ref[...], b_ref[...], NT)

        @pl.when(k == 0)
        def _():
            acc[...] = p

        @pl.when(k > 0)
        def _():
            acc[...] += p

        @pl.when(k == nk - 1)
        def _():
            @pl.when(first_rows)
            def _():
                dg_ref[...] = jnp.zeros_like(dg_ref)

            for r0 in range(0, tm, sub):
                rows = slice(r0, r0 + sub)
                dx, dg = _rms_bwd_math(x_ref[rows, :], g_ref[...], acc[rows, :])
                dx = dx + r_ref[rows, :]
                dx_ref[rows, :] = dx
                dxb_ref[rows, :] = dx.astype(dxb_ref.dtype)
                dg_ref[...] += dg

    row = pl.BlockSpec((tm, D), lambda i, k: (i, 0))
    vec = pl.BlockSpec((1, D), lambda i, k: (0, 0))
    return pl.pallas_call(
        body, name=name, grid=(T // tm, nk),
        in_specs=[pl.BlockSpec((tm, tk), lambda i, k: (i, k)),
                  pl.BlockSpec((None, D, tk), lambda i, k: (k // halves, 0, k % halves)), row, vec, row],
        out_specs=[row, row, vec], out_shape=[_sds((T, D), F32), _sds((T, D), MM_DTYPE), _sds((1, D), F32)],
        scratch_shapes=[pltpu.VMEM((tm, D), F32)], compiler_params=_cp(("arbitrary", "arbitrary"), VMEM_LIMIT_WIDE),
    )(dproj, wblk, x, g.reshape(1, D), res)


def out_fwd(name, branch, wout, x, tm=512, tn=512):
    T, E = branch.shape
    D = wout.shape[1]
    tm, tn = _tile(T, tm), _tile(D, tn)
    return _mm(name, (T // tm, D // tn), ("parallel", "parallel"), [branch, wout, x],
               [pl.BlockSpec((tm, E), lambda i, j: (i, 0)), pl.BlockSpec((E, tn), lambda i, j: (0, j)),
                pl.BlockSpec((tm, tn), lambda i, j: (i, j))],
               [_sds((T, D), F32)], [pl.BlockSpec((tm, tn), lambda i, j: (i, j))], NN,
               epi=lambda acc, x_ref: (acc + x_ref[...],))[0]


def mm_nt(name, a, b, out_dtype, tm=512, tn=512, res=None):
    M, K = a.shape
    N = b.shape[0]
    tm, tn = _tile(M, tm), _tile(N, tn)
    ins = [a, b] + ([res] if res is not None else [])
    specs = [pl.BlockSpec((tm, K), lambda i, j: (i, 0)), pl.BlockSpec((tn, K), lambda i, j: (j, 0))]
    if res is not None:
        specs.append(pl.BlockSpec((tm, tn), lambda i, j: (i, j)))
    epi = (lambda acc, r_ref: (acc + r_ref[...],)) if res is not None else None
    return _mm(name, (M // tm, N // tn), ("parallel", "parallel"), ins, specs,
               [_sds((M, N), out_dtype)], [pl.BlockSpec((tm, tn), lambda i, j: (i, j))], NT, epi=epi)[0]


def mm_nn(name, a, b, out_dtype, tm=512, tn=512):
    M, K = a.shape
    N = b.shape[1]
    tm, tn = _tile(M, tm), _tile(N, tn)
    return _mm(name, (M // tm, N // tn), ("parallel", "parallel"), [a, b],
               [pl.BlockSpec((tm, K), lambda i, j: (i, 0)), pl.BlockSpec((K, tn), lambda i, j: (0, j))],
               [_sds((M, N), out_dtype)], [pl.BlockSpec((tm, tn), lambda i, j: (i, j))], NN)[0]


def mm_tn(name, a, b, out_dtype, tm=512, tn=512):
    K, M = a.shape
    N = b.shape[1]
    tm, tn = _tile(M, tm), _tile(N, tn)
    return _mm(name, (M // tm, N // tn), ("parallel", "parallel"), [a, b],
               [pl.BlockSpec((K, tm), lambda i, j: (0, i)), pl.BlockSpec((K, tn), lambda i, j: (0, j))],
               [_sds((M, N), out_dtype)], [pl.BlockSpec((tm, tn), lambda i, j: (i, j))], TN)[0]


def rmsnorm_fwd(name, x, g, tm=512):
    T, D = x.shape
    tm = _tile(T, tm)

    def body(x_ref, g_ref, o_ref):
        xf = x_ref[...]
        r = lax.rsqrt(jnp.mean(xf * xf, axis=-1, keepdims=True) + EPS)
        o_ref[...] = ((xf * r) * g_ref[...]).astype(o_ref.dtype)

    return pl.pallas_call(
        body, name=name, grid=(T // tm,),
        in_specs=[pl.BlockSpec((tm, D), lambda i: (i, 0)), pl.BlockSpec((1, D), lambda i: (0, 0))],
        out_specs=pl.BlockSpec((tm, D), lambda i: (i, 0)), out_shape=_sds((T, D), MM_DTYPE),
        compiler_params=_cp(("parallel",)),
    )(x, g.reshape(1, D))


def _rms_bwd_math(xf, g, dh):
    r = lax.rsqrt(jnp.mean(xf * xf, axis=-1, keepdims=True) + EPS)
    gd = dh * g
    dx = r * gd - xf * ((r * r * r) * jnp.mean(gd * xf, axis=-1, keepdims=True))
    dg = jnp.sum(dh * (xf * r), axis=0, keepdims=True)
    return dx, dg


def rmsnorm_bwd(name, x, g, dh, res=None, tm=256):
    T, D = x.shape
    tm = _tile(T, tm)
    has_res = res is not None

    def body(*refs):
        x_ref, g_ref, dh_ref = refs[:3]
        dx_ref, dxb_ref, dg_ref = refs[3 + has_res:]
        dx, dg = _rms_bwd_math(x_ref[...], g_ref[...], dh_ref[...])
        if has_res:
            dx = dx + refs[3][...]
        dx_ref[...] = dx
        dxb_ref[...] = dx.astype(dxb_ref.dtype)

        @pl.when(pl.program_id(0) == 0)
        def _():
            dg_ref[...] = jnp.zeros_like(dg_ref)

        dg_ref[...] += dg

    row = pl.BlockSpec((tm, D), lambda i: (i, 0))
    vec = pl.BlockSpec((1, D), lambda i: (0, 0))
    ins = [x, g.reshape(1, D), dh] + ([res] if has_res else [])
    return pl.pallas_call(
        body, name=name, grid=(T // tm,), in_specs=[row, vec, row] + ([row] if has_res else []),
        out_specs=[row, row, vec], out_shape=[_sds((T, D), F32), _sds((T, D), MM_DTYPE), _sds((1, D), F32)],
        compiler_params=_cp(("arbitrary",)),
    )(*ins)


def final_loss_bwd(name, x, g, target, tm=256):
    T, D = x.shape
    tm = _tile(T, tm)

    def body(x_ref, g_ref, t_ref, loss_ref, dx_ref, dxb_ref, dg_ref):
        xf, gv = x_ref[...], g_ref[...]
        r = lax.rsqrt(jnp.mean(xf * xf, axis=-1, keepdims=True) + EPS)
        err = (xf * r) * gv - t_ref[...]
        part = 0.5 * jnp.sum(jnp.mean(err * err, axis=-1, keepdims=True), axis=0, keepdims=True)
        dx, dg = _rms_bwd_math(xf, gv, err / D)
        dx_ref[...] = dx
        dxb_ref[...] = dx.astype(dxb_ref.dtype)

        @pl.when(pl.program_id(0) == 0)
        def _():
            dg_ref[...] = jnp.zeros_like(dg_ref)
            loss_ref[...] = jnp.zeros_like(loss_ref)

        dg_ref[...] += dg
        loss_ref[...] += jnp.broadcast_to(part, loss_ref.shape)

    row = pl.BlockSpec((tm, D), lambda i: (i, 0))
    vec = pl.BlockSpec((1, D), lambda i: (0, 0))
    return pl.pallas_call(
        body, name=name, grid=(T // tm,), in_specs=[row, vec, row],
        out_specs=[pl.BlockSpec((8, LANE), lambda i: (0, 0)), row, row, vec],
        out_shape=[_sds((8, LANE), F32), _sds((T, D), F32), _sds((T, D), MM_DTYPE), _sds((1, D), F32)],
        compiler_params=_cp(("arbitrary",)),
    )(x, g.reshape(1, D), target)


def _pool_tiles(T, emix):
    pg = emix // len(POOL_WINDOWS)
    tc = 256 if pg % 256 == 0 else LANE
    return pg, tc, _tile(T, 512)


def pool_fwd(name, proj, T, emix):
    pg, tc, R = _pool_tiles(T, emix)
    per_g = pg // tc

    def body(u_ref, o_ref):
        g = pl.program_id(0) // per_g
        for gi, w in enumerate(POOL_WINDOWS):
            @pl.when(g == gi)
            def _():
                for ci in range(T // R):
                    r0 = ci * R
                    cur = u_ref[r0:r0 + R, :]
                    halo = jnp.zeros((POOL_HALO, tc), F32) if ci == 0 else u_ref[r0 - POOL_HALO:r0, :]
                    s = jnp.concatenate([halo, cur], axis=0)
                    for st in range(w.bit_length() - 1):
                        s = s + pltpu.roll(s, 1 << st, axis=0)
                    t = r0 + lax.broadcasted_iota(jnp.int32, (R, 1), 0)
                    cnt = jnp.minimum(t + 1, w).astype(F32)
                    o_ref[r0:r0 + R, :] = (s[POOL_HALO:] / cnt - cur).astype(o_ref.dtype)

    return pl.pallas_call(
        body, name=name, grid=(emix // tc,), in_specs=[pl.BlockSpec((T, tc), lambda j: (0, j))],
        out_specs=pl.BlockSpec((T, tc), lambda j: (0, j)), out_shape=_sds((T, emix), MM_DTYPE),
        compiler_params=_cp(("parallel",)),
    )(proj)


def pool_bwd(name, dpooled, dproj, T, emix):
    pg, tc, R = _pool_tiles(T, emix)
    per_g = pg // tc

    def body(d_ref, _, o_ref):
        g = pl.program_id(0) // per_g
        for gi, w in enumerate(POOL_WINDOWS):
            @pl.when(g == gi)
            def _():
                n = R + POOL_HALO
                for ci in range(T // R):
                    r0 = ci * R
                    cur = d_ref[r0:r0 + R, :]
                    halo = jnp.zeros((POOL_HALO, tc), F32) if ci == T // R - 1 else d_ref[r0 + R:r0 + n, :]
                    t = r0 + lax.broadcasted_iota(jnp.int32, (n, 1), 0)
                    cnt = jnp.minimum(t + 1, w).astype(F32)
                    s = jnp.concatenate([cur, halo], axis=0) / cnt
                    for st in range(w.bit_length() - 1):
                        s = s + pltpu.roll(s, n - (1 << st), axis=0)
                    o_ref[r0:r0 + R, :] = (s[:R] - cur).astype(o_ref.dtype)

    return pl.pallas_call(
        body, name=name, grid=(emix // tc,),
        in_specs=[pl.BlockSpec((T, tc), lambda j: (0, j)), pl.BlockSpec(memory_space=pl.ANY)],
        out_specs=pl.BlockSpec((T, tc), lambda j: (0, j)), out_shape=_sds(dproj.shape, dproj.dtype),
        input_output_aliases={1: 0}, compiler_params=_cp(("parallel",)),
    )(dpooled, dproj)


def grp_fwd(name, pooled, wgrp, scale, proj, gate_off, E):
    T, emix = pooled.shape
    pg = emix // len(POOL_WINDOWS)
    tm = _tile(T, 512)
    tc = 256 if pg % 256 == 0 and gate_off % 256 == 0 else LANE
    per = pg // tc

    def epi(acc, s_ref, g_ref):
        gt = g_ref[...]
        return acc, (acc * s_ref[...]) * (gt * _sig(gt))

    col = pl.BlockSpec((tm, tc), lambda n, i: (i, n))
    return _mm(name, (emix // tc, T // tm), ("parallel", "parallel"), [pooled, wgrp, scale, proj],
               [pl.BlockSpec((tm, pg), lambda n, i: (i, n // per)), pl.BlockSpec((None, pg, tc), lambda n, i: (n // per, 0, n % per)),
                pl.BlockSpec((1, tc), lambda n, i: (0, n)), pl.BlockSpec((tm, tc), lambda n, i: (i, gate_off // tc + n))],
               [_sds((T, emix), F32), _sds((T, E), MM_DTYPE)], [col, col], NN, epi=epi)


def grp_bwd_x(name, dy, wgrp):
    T, emix = dy.shape
    pg = emix // len(POOL_WINDOWS)
    tm = _tile(T, 512)
    return _mm(name, (len(POOL_WINDOWS), T // tm), ("parallel", "parallel"), [dy, wgrp],
               [pl.BlockSpec((tm, pg), lambda g, i: (i, g)), pl.BlockSpec((None, pg, pg), lambda g, i: (g, 0, 0))],
               [_sds((T, emix), F32)], [pl.BlockSpec((tm, pg), lambda g, i: (i, g))], NT)[0]


def grp_bwd_w(name, pooled, dy):
    T, emix = dy.shape
    ng = len(POOL_WINDOWS)
    pg = emix // ng
    col = pl.BlockSpec((T, pg), lambda g: (0, g))
    return _mm(name, (ng,), ("parallel",), [pooled, dy], [col, col],
               [_sds((ng, pg, pg), COMM_DTYPE)], [pl.BlockSpec((None, pg, pg), lambda g: (g, 0, 0))], TN)[0]


def gate_bwd_pool(name, dbranch, y, scale, proj, gate_off, n_proj):
    T, emix = y.shape
    tm, tc = _tile(T, 256), _tile(emix, 512)
    assert gate_off % tc == 0

    def body(db_ref, y_ref, s_ref, g_ref, dy_ref, dg_ref, ds_ref):
        db, yv, sc, gt = db_ref[...], y_ref[...], s_ref[...], g_ref[...]
        sg = _sig(gt)
        dmix = db * (gt * sg)
        dy_ref[...] = (dmix * sc).astype(dy_ref.dtype)
        dg_ref[...] = (db * (yv * sc) * (sg * (1.0 + gt * (1.0 - sg)))).astype(dg_ref.dtype)

        @pl.when(pl.program_id(1) == 0)
        def _():
            ds_ref[...] = jnp.zeros_like(ds_ref)

        ds_ref[...] += jnp.sum(dmix * yv, axis=0, keepdims=True)

    blk = pl.BlockSpec((tm, tc), lambda j, i: (i, j))
    vec = pl.BlockSpec((1, tc), lambda j, i: (0, j))
    gate = pl.BlockSpec((tm, tc), lambda j, i: (i, gate_off // tc + j))
    return pl.pallas_call(
        body, name=name, grid=(emix // tc, T // tm), in_specs=[blk, blk, vec, gate],
        out_specs=[blk, gate, vec],
        out_shape=[_sds((T, emix), MM_DTYPE), _sds((T, n_proj), MM_DTYPE), _sds((1, emix), F32)],
        compiler_params=_cp(("parallel", "arbitrary")),
    )(dbranch, y, scale, proj)


def attn_fwd(name, proj, kv, branch, q_off, gate_off, tm=512):
    T = proj.shape[0]
    M, two_eca = kv.shape
    eca = two_eca // 2
    hd = eca // CA_HEADS
    E = branch.shape[1]
    tm = _tile(T, tm)
    scale = hd ** -0.5

    def body(q_ref, g_ref, k_ref, v_ref, _, ca_ref, br_ref):
        s = _dot(q_ref[...], k_ref[...], NT) * scale
        s = s - jnp.max(s, axis=-1, keepdims=True)
        e = jnp.exp(s)
        p = e / jnp.sum(e, axis=-1, keepdims=True)
        o = _dot(p, v_ref[...], NN)
        gt = g_ref[...]
        ca_ref[...] = o
        br_ref[...] = (o * (gt * _sig(gt))).astype(br_ref.dtype)

    return pl.pallas_call(
        body, name=name, grid=(CA_HEADS, T // tm),
        in_specs=[pl.BlockSpec((tm, hd), lambda h, i: (i, q_off // hd + h)),
                  pl.BlockSpec((tm, hd), lambda h, i: (i, (gate_off + E - eca) // hd + h)),
                  pl.BlockSpec((M, hd), lambda h, i: (0, h)), pl.BlockSpec((M, hd), lambda h, i: (0, CA_HEADS + h)),
                  pl.BlockSpec(memory_space=pl.ANY)],
        out_specs=[pl.BlockSpec((tm, hd), lambda h, i: (i, h)), pl.BlockSpec((tm, hd), lambda h, i: (i, (E - eca) // hd + h))],
        out_shape=[_sds((T, eca), F32), _sds(branch.shape, branch.dtype)],
        input_output_aliases={4: 1}, compiler_params=_cp(("parallel", "parallel")),
    )(proj, proj, kv, kv, branch)


def attn_bwd(name, proj, kv, ca, dbranch, dproj, q_off, gate_off, tm=512):
    T = proj.shape[0]
    M, two_eca = kv.shape
    eca = two_eca // 2
    hd = eca // CA_HEADS
    E = dbranch.shape[1]
    tm = _tile(T, tm)
    scale = hd ** -0.5
    q_blk, g_blk = q_off // hd, (gate_off + E - eca) // hd

    def body(q_ref, g_ref, k_ref, v_ref, ca_ref, db_ref, _, dp_ref, dk_ref, dv_ref, dgate_buf):
        first_rows, part = pl.program_id(1) == 0, pl.program_id(2)

        @pl.when(part == 0)
        def _():
            q, k, v = q_ref[...], k_ref[...], v_ref[...]
            s = _dot(q, k, NT) * scale
            s = s - jnp.max(s, axis=-1, keepdims=True)
            e = jnp.exp(s)
            p = e / jnp.sum(e, axis=-1, keepdims=True)
            gt, db = g_ref[...], db_ref[...]
            sg = _sig(gt)
            do = db * (gt * sg)
            dgate_buf[...] = (db * ca_ref[...] * (sg * (1.0 + gt * (1.0 - sg)))).astype(dgate_buf.dtype)
            dpr = _dot(do, v, NT)
            ds = p * (dpr - jnp.sum(dpr * p, axis=-1, keepdims=True)) * scale
            dp_ref[...] = _dot(ds, k, NN).astype(dp_ref.dtype)

            @pl.when(first_rows)
            def _():
                dk_ref[...] = jnp.zeros_like(dk_ref)
                dv_ref[...] = jnp.zeros_like(dv_ref)

            dk_ref[...] += _dot(ds, q, TN)
            dv_ref[...] += _dot(p, do, TN)

        @pl.when(part == 1)
        def _():
            dp_ref[...] = dgate_buf[...]

    row = lambda blk: pl.BlockSpec((tm, hd), lambda h, i, p: (i, blk + h))
    out_dp = pl.BlockSpec((tm, hd), lambda h, i, p: (i, jnp.where(p == 0, q_blk, g_blk) + h))
    outs = pl.pallas_call(
        body, name=name, grid=(CA_HEADS, T // tm, 2),
        in_specs=[row(q_blk), row(g_blk), pl.BlockSpec((M, hd), lambda h, i, p: (0, h)),
                  pl.BlockSpec((M, hd), lambda h, i, p: (0, CA_HEADS + h)), row(0), row((E - eca) // hd),
                  pl.BlockSpec(memory_space=pl.ANY)],
        out_specs=[out_dp, pl.BlockSpec((M, hd), lambda h, i, p: (0, h)), pl.BlockSpec((M, hd), lambda h, i, p: (0, h))],
        out_shape=[_sds(dproj.shape, dproj.dtype), _sds((M, eca), F32), _sds((M, eca), F32)],
        scratch_shapes=[pltpu.VMEM((tm, hd), dproj.dtype)],
        input_output_aliases={6: 0}, compiler_params=_cp(("parallel", "arbitrary", "arbitrary")),
    )(proj, proj, kv, kv, ca, dbranch, dproj)
    return outs[0], jnp.concatenate([outs[1], outs[2]], axis=1)


def _split3(v):
    hi = v.astype(BF16)
    r1 = v - hi.astype(F32)
    mid = r1.astype(BF16)
    lo = (r1 - mid.astype(F32)).astype(BF16)
    return hi, mid, lo


def _tri_dot(tri, v):
    hi, mid, lo = _split3(v)
    d = functools.partial(lax.dot_general, dimension_numbers=NN, preferred_element_type=F32)
    return d(tri, hi) + d(tri, mid) + d(tri, lo)


def _causal_mask():
    C = HG_CHUNK
    return lax.broadcasted_iota(jnp.int32, (C, C), 0) >= lax.broadcasted_iota(jnp.int32, (C, C), 1)


def _cumsum_matrices(G):
    ri = lax.broadcasted_iota(jnp.int32, (G, G), 0)
    ci = lax.broadcasted_iota(jnp.int32, (G, G), 1)
    same = (ri // HG_CHUNK) == (ci // HG_CHUNK)
    return (jnp.where(same & (ri >= ci), 1.0, 0.0).astype(BF16), jnp.where(same & (ri <= ci), 1.0, 0.0).astype(BF16))


def _forget_gate(fi, lb):
    sig = _sig(fi)
    return sig, lb + (1.0 - lb) * sig


def _log_decay_cumsum(f_ref, b_s, lb, prefix, G):
    def group(i, carry):
        rows = pl.ds(pl.multiple_of(i * G, G), G)
        _, f = _forget_gate(f_ref[rows, :], lb)
        b_s[rows, :] = _tri_dot(prefix, jnp.log(f))
        return carry

    lax.fori_loop(0, b_s.shape[0] // G, group, 0)


def _lower_bound(lb2):
    l0, l1 = lb2[0:1, :], lb2[1:2, :]
    m = jnp.maximum(l0, l1)
    e0, e1 = jnp.exp(l0 - m), jnp.exp(l1 - m)
    sm0, sm1 = e0 / (e0 + e1), e1 / (e0 + e1)
    return (sm0 + sm1) - sm0, sm0, sm1


def _hg_chunk_fwd(qi, fi, lb, causal, b=None, prefix=None):
    sig, f = _forget_gate(fi, lb)
    k = 1.0 - f
    if b is None:
        b = _tri_dot(prefix, jnp.log(f))
    last = lax.broadcasted_iota(jnp.int32, b.shape, 0) == HG_CHUNK - 1
    bl = jnp.sum(jnp.where(last, b, 0.0), axis=0, keepdims=True)
    eb, enb, ete = jnp.exp(b), jnp.exp(-b), jnp.exp(bl - b)
    sq = _sig(qi)
    q = (qi * sq) * (HG_HEAD_DIM ** -0.5)
    q_dec, k_inv, k_te = q * eb, k * enb, k * ete
    att = jnp.where(causal, _dot(q_dec, k_inv, NT), 0.0)
    return dict(sig=sig, f=f, k=k, eb=eb, enb=enb, ete=ete, dcy=jnp.exp(bl), sq=sq, q_dec=q_dec, k_inv=k_inv, k_te=k_te,
                att=att, last=last)


def _hg_blocking(H, T):
    hb = 4 if H % 4 == 0 else (2 if H % 2 == 0 else 1)
    return hb, _tile(T, 1024)


def hgrn_fwd(name, proj, lb2, norm_g, emix, gate_off, E):
    T = proj.shape[0]
    hd, C = HG_HEAD_DIM, HG_CHUNK
    H, N = emix // hd, T // C
    hb, ts = _hg_blocking(H, T)
    W, nc, G = hb * hd, ts // C, _tile(ts, 256)

    def body(q_ref, f_ref, i_ref, g_ref, lb_ref, ng_ref, o_ref, st_ref, br_ref, state, b_s):
        @pl.when(pl.program_id(1) == 0)
        def _():
            state[...] = jnp.zeros_like(state)

        causal = _causal_mask()
        lb, _, _ = _lower_bound(lb_ref[...])
        ng = ng_ref[...]
        _log_decay_cumsum(f_ref, b_s, lb, _cumsum_matrices(G)[0], G)

        def chunk(n, carry):
            rows = pl.ds(pl.multiple_of(n * C, C), C)
            for h in range(hb):
                cols = slice(h * hd, (h + 1) * hd)
                v = i_ref[rows, cols]
                c = _hg_chunk_fwd(q_ref[rows, cols], f_ref[rows, cols], lb[:, cols], causal, b=b_s[rows, cols])
                st = state[h]
                o = _dot(c["att"], v, NN) + _dot(c["q_dec"], st, NT)
                st_ref[h, n] = st
                o_ref[rows, cols] = o
                r = lax.rsqrt(jnp.mean(o * o, axis=-1, keepdims=True) + EPS)
                gt = g_ref[rows, cols]
                br_ref[rows, cols] = (((o * r) * ng[:, cols]) * (gt * _sig(gt))).astype(br_ref.dtype)
                state[h] = st * c["dcy"] + _dot(v, c["k_te"], TN)
            return carry

        lax.fori_loop(0, nc, chunk, 0, unroll=4)

    col = lambda off: pl.BlockSpec((ts, W), lambda h, s: (s, off // W + h))
    vec = lambda r: pl.BlockSpec((r, W), lambda h, s: (0, h))
    return pl.pallas_call(
        body, name=name, grid=(H // hb, T // ts),
        in_specs=[col(0), col(emix), col(2 * emix), col(gate_off), vec(2), vec(1)],
        out_specs=[col(0), pl.BlockSpec((hb, nc, hd, hd), lambda h, s: (h, s, 0, 0)), col(0)],
        out_shape=[_sds((T, emix), F32), _sds((H, N, hd, hd), F32), _sds((T, E), MM_DTYPE)],
        scratch_shapes=[pltpu.VMEM((hb, hd, hd), F32), pltpu.VMEM((ts, W), F32)],
        compiler_params=_cp(("parallel", "arbitrary")),
    )(proj, proj, proj, proj, lb2, norm_g)


def hgrn_bwd(name, proj, lb2, norm_g, o_pre, states, dbranch, emix, gate_off):
    T, n_proj = proj.shape
    hd, C = HG_HEAD_DIM, HG_CHUNK
    H, N = emix // hd, T // C
    hb, ts = _hg_blocking(H, T)
    W, nc, ns = hb * hd, ts // C, T // ts
    qscale = HG_HEAD_DIM ** -0.5

    def body(q_ref, f_ref, i_ref, g_ref, lb_ref, ng_ref, o_ref, st_ref, db_ref, dp_ref, dlb_ref, dng_ref, stash, dst_ref, acc):
        first_block, part = pl.program_id(1) == 0, pl.program_id(2)

        @pl.when(part == 0)
        def _():
            @pl.when(first_block)
            def _():
                dst_ref[...] = jnp.zeros_like(dst_ref)
                acc[...] = jnp.zeros_like(acc)

            causal = _causal_mask()
            prefix, suffix = _cumsum_matrices(C)
            lb, sm0, sm1 = _lower_bound(lb_ref[...])
            ng = ng_ref[...]

            def chunk(j, carry):
                n = nc - 1 - j
                rows = pl.ds(pl.multiple_of(n * C, C), C)
                for h in range(hb):
                    cols = slice(h * hd, (h + 1) * hd)
                    dst = dst_ref[h]
                    qi, v, gt = q_ref[rows, cols], i_ref[rows, cols], g_ref[rows, cols]
                    c = _hg_chunk_fwd(qi, f_ref[rows, cols], lb[:, cols], causal, prefix=prefix)
                    st = st_ref[h, n]
                    o, db, ngh = o_ref[rows, cols], db_ref[rows, cols], ng[:, cols]
                    r = lax.rsqrt(jnp.mean(o * o, axis=-1, keepdims=True) + EPS)
                    sg = _sig(gt)
                    dmix = db * (gt * sg)
                    dgate = db * ((o * r) * ngh) * (sg * (1.0 + gt * (1.0 - sg)))
                    dyn = dmix * ngh
                    do = r * dyn - o * ((r * r * r) * jnp.mean(dyn * o, axis=-1, keepdims=True))
                    acc[1:2, cols] += jnp.sum(dmix * (o * r), axis=0, keepdims=True)
                    datt = jnp.where(causal, _dot(do, v, NT), 0.0)
                    dq_dec = _dot(datt, c["k_inv"], NN) + _dot(do, st, NN)
                    dk_inv = _dot(datt, c["q_dec"], TN)
                    dv = _dot(c["att"], do, TN) + _dot(c["k_te"], dst, NT)
                    dk_te = _dot(v, dst, NN)
                    ddcy = jnp.sum(dst * st, axis=0, keepdims=True)
                    dst_ref[h] = _dot(do, c["q_dec"], TN) + dst * c["dcy"]
                    t_te = dk_te * c["k_te"]
                    dbv = dq_dec * c["q_dec"] - dk_inv * c["k_inv"] - t_te
                    dbl = jnp.sum(t_te, axis=0, keepdims=True) + ddcy * c["dcy"]
                    dbv = dbv + jnp.where(c["last"], dbl, 0.0)
                    dlf = _tri_dot(suffix, dbv)
                    dk = dk_inv * c["enb"] + dk_te * c["ete"]
                    df = dlf / c["f"] - dk
                    sig, sq = c["sig"], c["sq"]
                    acc[0:1, cols] += jnp.sum(df * (1.0 - sig), axis=0, keepdims=True)
                    dqi = (dq_dec * c["eb"]) * qscale * (sq * (1.0 + qi * (1.0 - sq)))
                    dp_ref[rows, cols] = dqi.astype(dp_ref.dtype)
                    stash[0, rows, cols] = (df * (1.0 - lb[:, cols]) * (sig * (1.0 - sig))).astype(stash.dtype)
                    stash[1, rows, cols] = dv.astype(stash.dtype)
                    stash[2, rows, cols] = dgate.astype(stash.dtype)
                return carry

            lax.fori_loop(0, nc, chunk, 0, unroll=4)
            t = acc[0:1, :] * (sm0 * sm1)
            dlb_ref[...] = jnp.concatenate([-t, t], axis=0)
            dng_ref[...] = acc[1:2, :]

        for pi in range(3):
            @pl.when(part == pi + 1)
            def _():
                dp_ref[...] = stash[pi]

    col = lambda off: pl.BlockSpec((ts, W), lambda h, s, p: (ns - 1 - s, off // W + h))
    vec = lambda r: pl.BlockSpec((r, W), lambda h, s, p: (0, h))
    n_mix = emix // W
    out_col = pl.BlockSpec((ts, W), lambda h, s, p: (ns - 1 - s, jnp.where(p < 3, p * n_mix, gate_off // W) + h))
    return pl.pallas_call(
        body, name=name, grid=(H // hb, ns, 4),
        in_specs=[col(0), col(emix), col(2 * emix), col(gate_off), vec(2), vec(1), col(0),
                  pl.BlockSpec((hb, nc, hd, hd), lambda h, s, p: (h, ns - 1 - s, 0, 0)), col(0)],
        out_specs=[out_col, vec(2), vec(1)],
        out_shape=[_sds((T, n_proj), MM_DTYPE), _sds((2, emix), F32), _sds((1, emix), F32)],
        scratch_shapes=[pltpu.VMEM((3, ts, W), MM_DTYPE), pltpu.VMEM((hb, hd, hd), F32), pltpu.VMEM((8, W), F32)],
        compiler_params=_cp(("parallel", "arbitrary", "arbitrary")),
    )(proj, proj, proj, proj, lb2, norm_g, o_pre, states, dbranch)


class WholeWeights:
    def __init__(self, wkv, wout, wpin, wgrp, whin, hgrn_norm_g):
        self.w = dict(wkv=wkv, wout=wout, wpin=wpin, wgrp=wgrp, whin=whin, hnorm=hgrn_norm_g)
        self.grads = {}

    def first_weights(self):
        return self.w["wpin"], self.w["hnorm"], 0.0

    def after_proj0(self, after):
        pass

    def layer0_weights(self, after):
        return self.w["wgrp"], self.w["wkv"][0], self.w["wout"][0]

    def after_attn0(self, after):
        pass

    def layer1_weights(self, after):
        return self.w["whin"]

    def after_proj1(self, after):
        pass

    def layer1_rest(self, after):
        return self.w["wkv"][1], self.w["wout"][1]

    def grads_ready(self, layer, grads):
        self.grads[layer] = grads
        return 0.0

    def grads_reduce(self, layer, after):
        return 0.0


def run_step(x, mem, target, norm_g, mem_norm_g, pool_scale, hgrn_lb, final_g, comm):
    T, D = x.shape
    mem_n = rmsnorm_fwd("mem_norm", mem, mem_norm_g)
    wpin, hgrn_norm_g, zero = comm.first_weights()
    h0 = rmsnorm_fwd("norm0", x, norm_g[0] + zero)
    proj0 = proj_fwd("proj0", h0, wpin)
    comm.after_proj0(proj0)
    n0 = proj0.shape[1]
    E = n0 // 2
    eca = E // 4
    emix = E - eca
    q0, g0 = emix, emix + eca
    q1, g1 = 3 * emix, 3 * emix + eca
    pooled = pool_fwd("pool_fwd", proj0, T, emix)
    wgrp, wkv0, wout0 = comm.layer0_weights(pooled)
    y0, branch0 = grp_fwd("grp_fwd", pooled, wgrp, pool_scale, proj0, g0, E)
    kv0 = mm_nn("kv0", mem_n, wkv0, MM_DTYPE)
    ca0, branch0 = attn_fwd("attn_fwd0", proj0, kv0, branch0, q0, g0)
    comm.after_attn0(ca0)
    x1 = out_fwd("out0", branch0, wout0, x)
    h1 = rmsnorm_fwd("norm1", x1, norm_g[1])
    whin = comm.layer1_weights(h1)
    proj1 = proj_fwd("proj1", h1, whin)
    comm.after_proj1(proj1)
    o1, states, branch1 = hgrn_fwd("hgrn_fwd", proj1, hgrn_lb, hgrn_norm_g, emix, g1, E)
    wkv1, wout1 = comm.layer1_rest(o1)
    kv1 = mm_nn("kv1", mem_n, wkv1, MM_DTYPE)
    ca1, branch1 = attn_fwd("attn_fwd1", proj1, kv1, branch1, q1, g1)
    x2 = out_fwd("out1", branch1, wout1, x1)
    loss, dx2, dx2b, g_final = final_loss_bwd("final", x2, final_g, target)
    g_wout1 = mm_tn("gwout1", branch1, dx2b, COMM_DTYPE)
    dbranch1 = mm_nt("dbranch1", dx2b, wout1, F32)
    dproj1, g_lb, g_hnorm = hgrn_bwd("hgrn_bwd", proj1, hgrn_lb, hgrn_norm_g, o1, states, dbranch1, emix, g1)
    dproj1, dkv1 = attn_bwd("attn_bwd1", proj1, kv1, ca1, dbranch1, dproj1, q1, g1)
    g_wkv1 = mm_tn("gwkv1", mem_n, dkv1, COMM_DTYPE)
    dmem_n = mm_nt("dmem1", dkv1, wkv1, F32)
    g_whin = proj_bwd_w("gwhin", h1, dproj1, whin.shape[0])
    zero = comm.grads_ready(1, dict(w_kv=g_wkv1, w_out=g_wout1, hgrn_w_in=g_whin))
    dx1, dx1b, g_norm1 = proj_bwd_x_norm("dh1", dproj1, whin, x1, norm_g[1] + zero, dx2)
    comm.grads_reduce(1, dx1)
    g_wout0 = mm_tn("gwout0", branch0, dx1b, COMM_DTYPE)
    comm.grads_ready("0a", dict(w_out=g_wout0))
    dbranch0 = mm_nt("dbranch0", dx1b, wout0, F32)
    zero = comm.grads_reduce("0a", dbranch0)
    dy0, dproj0, g_scale = gate_bwd_pool("gate_bwd0", dbranch0, y0, pool_scale + zero, proj0, g0, n0)
    g_wgrp = grp_bwd_w("gwgrp", pooled, dy0)
    dpooled = grp_bwd_x("dpooled", dy0, wgrp)
    dproj0 = pool_bwd("pool_bwd", dpooled, dproj0, T, emix)
    dproj0, dkv0 = attn_bwd("attn_bwd0", proj0, kv0, ca0, dbranch0, dproj0, q0, g0)
    g_wkv0 = mm_tn("gwkv0", mem_n, dkv0, COMM_DTYPE)
    dmem_n = mm_nt("dmem0", dkv0, wkv0, F32, res=dmem_n)
    g_wpin = proj_bwd_w("gwpin", h0, dproj0, wpin.shape[0])
    zero = comm.grads_ready("0b", dict(w_kv=g_wkv0, pool_w_in=g_wpin, pool_w_grp=g_wgrp))
    grad_x, _, g_norm0 = proj_bwd_x_norm("dh0", dproj0, wpin, x, norm_g[0] + zero, dx1)
    _, _, g_mem = rmsnorm_bwd("mem_norm_bwd", mem, mem_norm_g, dmem_n)

    small = dict(norm_g=jnp.concatenate([g_norm0, g_norm1], axis=0), mem_norm_g=g_mem[0], pool_scale=g_scale,
                 hgrn_lb=g_lb, hgrn_norm_g=g_hnorm, final_g=g_final[0])
    return loss[0, 0], grad_x, small


def _position():
    return lax.axis_index("x"), lax.axis_index("y"), lax.axis_index("c")


def _slot(p):
    return 4 * p[0] + 2 * p[1] + p[2]


def all_gather_blocks(name, blocks):
    n = len(blocks)

    def body(*refs):
        ins, outs, token = refs[:n], refs[n:2 * n], refs[2 * n]
        send_sems, recv_sems, local_sems = refs[2 * n + 1:]
        token[...] = jnp.zeros_like(token)
        x, y, c = _position()
        me, sibling = (x, y, c), (x, y, 1 - c)
        chips = _other_chips(x, y)

        def copy(t, k, block, to, src=None):
            dst = outs[t].at[_slot(block)]
            return pltpu.make_async_remote_copy(src_ref=dst if src is None else src, dst_ref=dst, send_sem=send_sems.at[t, k],
                                                recv_sem=recv_sems.at[t, k], device_id=to, device_id_type=MESH)

        mine = [pltpu.make_async_copy(ins[t], outs[t].at[_slot(me)], local_sems.at[t]) for t in range(n)]
        for cp in mine:
            cp.start()
        sends = []
        for t in range(n):
            sends.append(copy(t, 0, me, sibling, src=ins[t]))
            sends += [copy(t, 1 + j, me, (*chip, c), src=ins[t]) for j, chip in enumerate(chips)]
        for cp in sends:
            cp.start()
        for t in range(n):
            for j, chip in enumerate(chips):
                copy(t, 1 + j, (*chip, c), me).wait_recv()
                passed = copy(t, 4 + j, (*chip, c), sibling)
                passed.start()
                sends.append(passed)
        for t in range(n):
            copy(t, 0, sibling, me).wait_recv()
            for j, chip in enumerate(chips):
                copy(t, 4 + j, (*chip, 1 - c), me).wait_recv()
        for cp in sends:
            cp.wait_send()
        for cp in mine:
            cp.wait()

    any_spec = pl.BlockSpec(memory_space=pl.ANY)
    return pl.pallas_call(
        body, name=name, in_specs=[any_spec] * n, out_specs=[any_spec] * n + [pl.BlockSpec(memory_space=pltpu.VMEM)],
        out_shape=[_sds((N_DEV,) + b.shape, b.dtype) for b in blocks] + [_sds((8, LANE), F32)],
        scratch_shapes=[pltpu.SemaphoreType.DMA((n, 7)), pltpu.SemaphoreType.DMA((n, 7)), pltpu.SemaphoreType.DMA((n,))],
        compiler_params=pltpu.CompilerParams(has_side_effects=True),
    )(*blocks)


_HBM = pl.BlockSpec(memory_space=pltpu.HBM)
_SEM = pl.BlockSpec(memory_space=pltpu.SEMAPHORE)
_EFFECT = pltpu.SideEffectType.DATAFLOW_SIDE_EFFECTING


def split_start(name, bufs, n_copies, build):
    n = len(bufs)

    def body(*refs):
        for cp in build(refs[:n], refs[n], refs[n + 1]):
            cp.start()
        refs[-1][...] = jnp.zeros_like(refs[-1])

    outs = pl.pallas_call(
        body, name=name, in_specs=[_HBM] * n,
        out_shape=(pltpu.SemaphoreType.DMA((n_copies,)), pltpu.SemaphoreType.DMA((n_copies,)),
                   *[pltpu.HBM(b.shape, b.dtype) for b in bufs], _sds((8, LANE), F32)),
        out_specs=(_SEM, _SEM, *[_HBM] * n, pl.BlockSpec(memory_space=pltpu.VMEM)),
        input_output_aliases={i: 2 + i for i in range(n)},
        compiler_params=pltpu.CompilerParams(has_side_effects=_EFFECT),
    )(*[pltpu.with_memory_space_constraint(b, pltpu.HBM) for b in bufs])
    return (outs[0], outs[1]), list(outs[2:2 + n]), outs[-1]


def split_wait(name, sems, bufs, build, after):
    n = len(bufs)

    def body(*refs):
        for cp in build(refs[:n], refs[n], refs[n + 1]):
            cp.wait()

    return list(pl.pallas_call(
        body, name=name, in_specs=[_HBM] * n + [_SEM, _SEM, pl.BlockSpec(memory_space=pl.ANY)],
        out_shape=[pltpu.HBM(b.shape, b.dtype) for b in bufs], out_specs=[_HBM] * n,
        input_output_aliases={i: i for i in range(n)},
        compiler_params=pltpu.CompilerParams(has_side_effects=_EFFECT),
    )(*bufs, sems[0], sems[1], after))


def _remote(src, dst, send_sems, recv_sems, k, to):
    return pltpu.make_async_remote_copy(src_ref=src, dst_ref=dst, send_sem=send_sems.at[k], recv_sem=recv_sems.at[k],
                                        device_id=to, device_id_type=MESH)


def _other_chips(x, y):
    return [(1 - x, y), (x, 1 - y), (1 - x, 1 - y)]


def _gather_cross(n):
    def build(refs, ss, rs):
        x, y, c = _position()
        me = _slot((x, y, c))
        targets = [(x, y, 1 - c)] + [(*chip, c) for chip in _other_chips(x, y)]
        return [_remote(refs[t], refs[n + t].at[me], ss, rs, 4 * t + k, to) for t in range(n) for k, to in enumerate(targets)]
    return build


def _gather_pass(n):
    def build(refs, ss, rs):
        x, y, c = _position()
        cps = []
        for t in range(n):
            for j, chip in enumerate(_other_chips(x, y)):
                blk = refs[t].at[_slot((*chip, c))]
                cps.append(_remote(blk, blk, ss, rs, 3 * t + j, (x, y, 1 - c)))
        return cps
    return build


def _reduce_pair(n):
    def build(refs, ss, rs):
        x, y, c = _position()
        return [_remote(refs[t].at[j, 1 - c], refs[n + t].at[j], ss, rs, 4 * t + j, (x, y, 1 - c))
                for t in range(n) for j in range(4)]
    return build


def _reduce_cross(n):
    def build(refs, ss, rs):
        x, y, c = _position()
        return [_remote(refs[t].at[2 * chip[0] + chip[1]], refs[n + t].at[r], ss, rs, 3 * t + r, (*chip, c))
                for t in range(n) for r, chip in enumerate(_other_chips(x, y))]
    return build


def pair_add(name, part, landed, tr=256):
    _, _, R, C = part.shape
    tr = _row_tile(R, tr)

    def body(_, p_ref, l_ref, o_ref):
        o_ref[...] = (p_ref[...].astype(F32) + l_ref[...].astype(F32)).astype(o_ref.dtype)

    blk = pl.BlockSpec((None, tr, C), lambda j, i, core: (j, i, 0))
    core = lax.axis_index("c").astype(jnp.int32).reshape(1)
    return pl.pallas_call(
        body, name=name, out_shape=_sds((4, R, C), part.dtype), compiler_params=_cp(("parallel", "parallel")),
        grid_spec=pltpu.PrefetchScalarGridSpec(
            num_scalar_prefetch=1, grid=(4, R // tr),
            in_specs=[pl.BlockSpec((None, None, tr, C), lambda j, i, core: (j, core[0], i, 0)), blk], out_specs=blk),
    )(core, part, landed)


def _row_tile(R, pref):
    if R <= pref:
        return R
    t = (pref // 16) * 16
    while R % t:
        t -= 16
    return t


def all_reduce_small(name, vec):
    R = vec.shape[0]

    def body(v_ref, sum_ref, land_ref, send_sems, recv_sems):
        x, y, c = _position()
        me = (x, y, c)
        flip = lambda v, bit: 1 - v if bit else v
        peers = [(flip(x, k & 4), flip(y, k & 2), flip(c, k & 1)) for k in range(1, N_DEV)]
        land_ref[_slot(me)] = v_ref[...]
        sends = [pltpu.make_async_remote_copy(src_ref=v_ref, dst_ref=land_ref.at[_slot(me)], send_sem=send_sems.at[k],
                                              recv_sem=recv_sems.at[k], device_id=peer, device_id_type=MESH)
                 for k, peer in enumerate(peers)]
        for cp in sends:
            cp.start()
        for k, peer in enumerate(peers):
            pltpu.make_async_remote_copy(src_ref=v_ref, dst_ref=land_ref.at[_slot(peer)], send_sem=send_sems.at[k],
                                         recv_sem=recv_sems.at[k], device_id=peer, device_id_type=MESH).wait_recv()
        acc = land_ref[0]
        for s in range(1, N_DEV):
            acc = acc + land_ref[s]
        sum_ref[...] = acc
        for cp in sends:
            cp.wait_send()

    vm = pl.BlockSpec(memory_space=pltpu.VMEM)
    return pl.pallas_call(
        body, name=name, in_specs=[vm], out_specs=[vm, vm],
        out_shape=[_sds(vec.shape, F32), _sds((N_DEV,) + vec.shape, F32)],
        scratch_shapes=[pltpu.SemaphoreType.DMA((7,)), pltpu.SemaphoreType.DMA((7,))],
        compiler_params=pltpu.CompilerParams(has_side_effects=True),
    )(vec)[0]


def _adamw_math(g, w, m, v):
    m2 = ADAM_B1 * m + (1.0 - ADAM_B1) * g
    v2 = ADAM_B2 * v + (1.0 - ADAM_B2) * (g * g)
    m_hat = m2 / (1.0 - ADAM_B1 ** ADAM_STEP)
    v_hat = v2 / (1.0 - ADAM_B2 ** ADAM_STEP)
    return -ADAM_LR * (m_hat / (jnp.sqrt(v_hat) + ADAM_EPS) + ADAM_WD * w), m2, v2


def adamw_shard(name, chip_parts, landed, w, m, v, layer, prev=None, tr=128):
    L, R, C = w.shape
    tr = _row_tile(R, tr)
    n_in = 5 + (4 if prev is not None else 0)

    def body(*refs):
        own_ref, land_ref, w_ref, m_ref, v_ref = refs[1:6]
        g_ref, d_ref, nm_ref, nv_ref = refs[1 + n_in:]
        g = own_ref[...].astype(F32)
        for s in range(landed.shape[0]):
            g = g + land_ref[s].astype(F32)
        g_ref[...] = g
        d_ref[...], nm_ref[...], nv_ref[...] = _adamw_math(g, w_ref[...], m_ref[...], v_ref[...])

    lay = pl.BlockSpec((None, tr, C), lambda i, chip: (layer, i, 0))
    own = pl.BlockSpec((None, tr, C), lambda i, chip: (chip[0], i, 0))
    chip = (2 * lax.axis_index("x") + lax.axis_index("y")).astype(jnp.int32).reshape(1)
    ins = [chip_parts, landed, w, m, v] + (list(prev) if prev is not None else [])
    return pl.pallas_call(
        body, name=name, out_shape=[_sds((L, R, C), F32)] * 4,
        grid_spec=pltpu.PrefetchScalarGridSpec(
            num_scalar_prefetch=1, grid=(R // tr,),
            in_specs=[own, pl.BlockSpec((landed.shape[0], tr, C), lambda i, chip: (0, i, 0)), lay, lay, lay]
            + [pl.BlockSpec(memory_space=pl.ANY)] * (n_in - 5),
            out_specs=[lay] * 4),
        input_output_aliases={6 + i: i for i in range(n_in - 5)}, compiler_params=_cp(("parallel",)),
    )(chip, *ins)


def adamw_small(name, g, w, m, v):
    def body(g_ref, w_ref, m_ref, v_ref, d_ref, nm_ref, nv_ref):
        d_ref[...], nm_ref[...], nv_ref[...] = _adamw_math(g_ref[...], w_ref[...], m_ref[...], v_ref[...])

    return pl.pallas_call(body, name=name, out_shape=[_sds(w.shape, F32)] * 3)(g, w, m, v)


def _pack(vs):
    flat = jnp.concatenate([v.reshape(-1) for v in vs])
    return flat.reshape(-1, LANE)


def _unpack(packed, like):
    flat, out, off = packed.reshape(-1), [], 0
    for v in like:
        out.append(flat[off:off + v.size].reshape(v.shape))
        off += v.size
    return out


class MeshWeights:
    def __init__(self, w_kv, w_out, pool_w_in, pool_w_grp, hgrn_w_in, hgrn_norm_g):
        self.n_grp, self.pg_loc, self.pg = pool_w_grp.shape[1:]
        wpin, hnorm, token = all_gather_blocks("gather_first", [pool_w_in[0].astype(COMM_DTYPE), hgrn_norm_g])
        self.first = (wpin, hnorm.reshape(1, -1))
        zero = token[0, 0]
        cast = lambda a: (a + zero).astype(COMM_DTYPE)
        self.gather, self.tokens = {}, []
        for layer, blocks in ((0, [cast(pool_w_grp[0]), cast(w_kv[0]), cast(w_out[0])]),
                              (1, [cast(hgrn_w_in[0])]), (2, [cast(w_kv[1]), cast(w_out[1])])):
            me = _slot(_position())
            lands = [lax.dynamic_update_index_in_dim(lax.empty((N_DEV,) + b.shape, b.dtype), b, me, 0) for b in blocks]
            n = len(blocks)
            sems, bufs, token = split_start(f"gather{layer}_cross", blocks + lands, 4 * n, _gather_cross(n))
            self.gather[layer] = (sems, bufs)
            self.tokens.append(token[0, 0])
        self.reduce = {}

    def first_weights(self):
        return self.first[0], self.first[1], self.tokens[0] + self.tokens[1] + self.tokens[2]

    def _pass_on(self, layer, after):
        sems, bufs = self.gather[layer]
        n = len(bufs) // 2
        gathered = split_wait(f"gather{layer}_cross_wait", sems, bufs, _gather_cross(n), after)[n:]
        sems, bufs, _ = split_start(f"gather{layer}_pass", gathered, 3 * n, _gather_pass(n))
        self.gather[layer] = (sems, bufs)

    def _gathered(self, layer, after):
        sems, bufs = self.gather[layer]
        return split_wait(f"gather{layer}_pass_wait", sems, bufs, _gather_pass(len(bufs)), after)

    def after_proj0(self, after):
        self._pass_on(0, after)

    def layer0_weights(self, after):
        wgrp, wkv, wout = self._gathered(0, after)
        rows = lambda g: g.reshape(-1, g.shape[-1])
        return wgrp.transpose(1, 0, 2, 3).reshape(self.n_grp, self.pg, self.pg), rows(wkv), rows(wout)

    def after_attn0(self, after):
        self._pass_on(1, after)

    def layer1_weights(self, after):
        return self._gathered(1, after)[0]

    def after_proj1(self, after):
        self._pass_on(2, after)

    def layer1_rest(self, after):
        wkv, wout = self._gathered(2, after)
        rows = lambda g: g.reshape(-1, g.shape[-1])
        return rows(wkv), rows(wout)

    def grads_ready(self, layer, grads):
        parts = {}
        for nm, g in grads.items():
            if nm == "pool_w_grp":
                g = g.reshape(self.n_grp, N_DEV, self.pg_loc, self.pg).transpose(1, 0, 2, 3)
            parts[nm] = g.reshape(4, 2, -1, g.shape[-1])
        names, srcs = list(parts), list(parts.values())
        lands = [lax.empty((4,) + p.shape[2:], p.dtype) for p in srcs]
        sems, bufs, token = split_start(f"reduce{layer}_pair", srcs + lands, 4 * len(srcs), _reduce_pair(len(srcs)))
        self.reduce[layer] = (names, sems, bufs)
        return token[0, 0]

    def grads_reduce(self, layer, after):
        names, sems, bufs = self.reduce[layer]
        n = len(names)
        bufs = split_wait(f"reduce{layer}_pair_wait", sems, bufs, _reduce_pair(n), after)
        chip_parts = [pair_add(f"pair_add_{nm}{layer}", bufs[t], bufs[n + t]) for t, nm in enumerate(names)]
        lands = [lax.empty((3,) + p.shape[1:], p.dtype) for p in chip_parts]
        sems, bufs, token = split_start(f"reduce{layer}_cross", chip_parts + lands, 3 * n, _reduce_cross(n))
        self.reduce[layer] = (names, sems, bufs)
        return token[0, 0]

    def grads_done(self, layer, after):
        names, sems, bufs = self.reduce[layer]
        n = len(names)
        bufs = split_wait(f"reduce{layer}_cross_wait", sems, bufs, _reduce_cross(n), after)
        return {nm: (bufs[t], bufs[n + t]) for t, nm in enumerate(names)}


def kernel(x, mem, norm_g, mem_norm_g, w_kv, w_out, pool_w_in, pool_w_grp, pool_scale, hgrn_w_in, hgrn_lb, hgrn_norm_g, final_g, loss_target, m_norm_g, m_mem_norm_g, m_w_kv, m_w_out, m_pool_w_in, m_pool_w_grp, m_pool_scale, m_hgrn_w_in, m_hgrn_lb, m_hgrn_norm_g, m_final_g, v_norm_g, v_mem_norm_g, v_w_kv, v_w_out, v_pool_w_in, v_pool_w_grp, v_pool_scale, v_hgrn_w_in, v_hgrn_lb, v_hgrn_norm_g, v_final_g):
    comm = MeshWeights(w_kv, w_out, pool_w_in, pool_w_grp, hgrn_w_in, hgrn_norm_g)
    loss, grad_x, small = run_step(x[0], mem[0], loss_target[0], norm_g, mem_norm_g, pool_scale, hgrn_lb, final_g, comm)
    loss = lax.psum(loss, AXES)
    names = ("norm_g", "mem_norm_g", "pool_scale", "hgrn_lb", "hgrn_norm_g", "final_g")
    like = (norm_g, mem_norm_g, pool_scale, hgrn_lb, _sds((1, N_DEV * hgrn_norm_g.shape[1]), F32), final_g)
    reduced = _unpack(all_reduce_small("reduce_small", _pack([small[nm].reshape(lk.shape) for nm, lk in zip(names, like)])), like)
    n_loc = hgrn_norm_g.shape[1]
    comm.grads_reduce("0b", reduced[0])
    reduced[4] = lax.dynamic_slice(reduced[4], (0, _slot(_position()) * n_loc), (1, n_loc))
    ws = (norm_g, mem_norm_g, pool_scale, hgrn_lb, hgrn_norm_g, final_g)
    ms = (m_norm_g, m_mem_norm_g, m_pool_scale, m_hgrn_lb, m_hgrn_norm_g, m_final_g)
    vs = (v_norm_g, v_mem_norm_g, v_pool_scale, v_hgrn_lb, v_hgrn_norm_g, v_final_g)
    res = adamw_small("adamw_small", _pack(reduced), _pack(ws), _pack(ms), _pack(vs))
    outs = {nm: [reduced[i]] + [_unpack(r, ws)[i] for r in res] for i, nm in enumerate(names)}
    rows = lambda a: a.reshape(a.shape[0], -1, a.shape[-1])
    state = dict(w_kv=(w_kv, m_w_kv, v_w_kv), w_out=(w_out, m_w_out, v_w_out), pool_w_in=(pool_w_in, m_pool_w_in, v_pool_w_in),
                 pool_w_grp=(pool_w_grp, m_pool_w_grp, v_pool_w_grp), hgrn_w_in=(hgrn_w_in, m_hgrn_w_in, v_hgrn_w_in))
    big, after = {}, res[0]
    for group, layer in ((1, 1), ("0a", 0), ("0b", 0)):
        for nm, (chip_parts, landed) in comm.grads_done(group, after).items():
            w, m, v = state[nm]
            big[nm] = adamw_shard(f"adamw_{nm}{layer}", chip_parts, landed, rows(w), rows(m), rows(v),
                                  layer if w.shape[0] > 1 else 0, prev=big.get(nm))
            after = big[nm][0]
    for nm, (w, _, _) in state.items():
        outs[nm] = [r.reshape(w.shape) for r in big[nm]]
    order = ("norm_g", "mem_norm_g", "w_kv", "w_out", "pool_w_in", "pool_w_grp", "pool_scale", "hgrn_w_in", "hgrn_lb",
             "hgrn_norm_g", "final_g")
    return (loss, grad_x[None], *[outs[nm][0] for nm in order], *[outs[nm][1] for nm in order],
            *[outs[nm][2] for nm in order], *[outs[nm][3] for nm in order])
```

```python
import functools

import jax
import jax.numpy as jnp
from jax import lax
from jax.experimental import pallas as pl
from jax.experimental.pallas import tpu as pltpu

F32, BF16 = jnp.float32, jnp.bfloat16
MM_DTYPE = BF16
COMM_DTYPE = BF16
EPS = 1e-6
POOL_WINDOWS = (2, 4, 8, 16)
POOL_HALO = 16
HG_HEAD_DIM = 128
HG_CHUNK = 64
CA_HEADS = 4
N_DEV = 8
ADAM_LR, ADAM_B1, ADAM_B2, ADAM_EPS, ADAM_WD, ADAM_STEP = 0.001, 0.9, 0.999, 1e-08, 0.01, 10
VMEM_LIMIT = 48 << 20
LANE = 128
MESH = pl.DeviceIdType.MESH
AXES = ("x", "y", "c")

NN = (((1,), (0,)), ((), ()))
NT = (((1,), (1,)), ((), ()))
TN = (((0,), (0,)), ((), ()))


def _cp(sem=None, vmem=VMEM_LIMIT):
    return pltpu.CompilerParams(dimension_semantics=sem, vmem_limit_bytes=vmem)


def _dot(a, b, dims):
    return lax.dot_general(a.astype(MM_DTYPE), b.astype(MM_DTYPE), dims, preferred_element_type=F32)


def _sig(v):
    return 1.0 / (1.0 + jnp.exp(-v))


def _tile(n, pref):
    if n <= pref:
        return n
    t = (pref // LANE) * LANE
    while n % t:
        t -= LANE
    return t


def _sds(shape, dtype):
    return jax.ShapeDtypeStruct(tuple(shape), dtype)


def _mm(name, grid, sem, ins, in_specs, out_shapes, out_specs, dims, epi=None, nk=1, acc_shape=None, aliases=None):
    n_in, n_out = len(ins), len(out_shapes)

    def body(*refs):
        a_ref, b_ref = refs[0], refs[1]
        extra = refs[2:n_in]
        outs = refs[n_in:n_in + n_out]
        p = _dot(a_ref[...], b_ref[...], dims)

        def finish(acc):
            res = epi(acc, *extra) if epi is not None else (acc,)
            for r, o_ref in zip(res, outs):
                o_ref[...] = r.astype(o_ref.dtype)

        if nk == 1:
            finish(p)
        else:
            acc_ref = refs[n_in + n_out]
            k = pl.program_id(len(grid) - 1)

            @pl.when(k == 0)
            def _():
                acc_ref[...] = p

            @pl.when(k > 0)
            def _():
                acc_ref[...] += p

            @pl.when(k == nk - 1)
            def _():
                finish(acc_ref[...])

    scratch = [pltpu.VMEM(acc_shape, F32)] if nk > 1 else []
    return pl.pallas_call(
        body, name=name, grid=grid, in_specs=in_specs, out_specs=out_specs, out_shape=out_shapes,
        scratch_shapes=scratch, compiler_params=_cp(sem), input_output_aliases=aliases or {},
    )(*ins)


def proj_fwd(name, h, wblk, tm=1024):
    T, D = h.shape
    nblk, _, nb = wblk.shape
    tm = _tile(T, tm)
    return _mm(name, (nblk, T // tm), ("parallel", "parallel"), [h, wblk],
               [pl.BlockSpec((tm, D), lambda j, i: (i, 0)), pl.BlockSpec((None, D, nb), lambda j, i: (j, 0, 0))],
               [_sds((T, nblk * nb), F32)], [pl.BlockSpec((tm, nb), lambda j, i: (i, j))], NN)[0]


def proj_bwd_w(name, h, dproj, nblk, tm=1024, tn=1024):
    T, D = h.shape
    N = dproj.shape[1]
    nb = N // nblk
    tm, tn = _tile(D, tm), _tile(nb, tn)
    per = nb // tn
    return _mm(name, (D // tm, N // tn), ("parallel", "parallel"), [h, dproj],
               [pl.BlockSpec((T, tm), lambda i, n: (0, i)), pl.BlockSpec((T, tn), lambda i, n: (0, n))],
               [_sds((nblk, D, nb), COMM_DTYPE)], [pl.BlockSpec((None, tm, tn), lambda i, n: (n // per, i, n % per))], TN)[0]


def proj_bwd_x(name, dproj, wblk, tm=512):
    T, N = dproj.shape
    nblk, D, nb = wblk.shape
    tm = _tile(T, tm)
    return _mm(name, (T // tm, nblk), ("parallel", "arbitrary"), [dproj, wblk],
               [pl.BlockSpec((tm, nb), lambda i, k: (i, k)), pl.BlockSpec((None, D, nb), lambda i, k: (k, 0, 0))],
               [_sds((T, D), F32)], [pl.BlockSpec((tm, D), lambda i, k: (i, 0))], NT, nk=nblk, acc_shape=(tm, D))[0]


def out_fwd(name, branch, wout, x, tm=512, tn=1024):
    T, E = branch.shape
    D = wout.shape[1]
    tm, tn = _tile(T, tm), _tile(D, tn)
    return _mm(name, (T // tm, D // tn), ("parallel", "parallel"), [branch, wout, x],
               [pl.BlockSpec((tm, E), lambda i, j: (i, 0)), pl.BlockSpec((E, tn), lambda i, j: (0, j)),
                pl.BlockSpec((tm, tn), lambda i, j: (i, j))],
               [_sds((T, D), F32)], [pl.BlockSpec((tm, tn), lambda i, j: (i, j))], NN,
               epi=lambda acc, x_ref: (acc + x_ref[...],))[0]


def mm_nt(name, a, b, out_dtype, tm=512, tn=1024, res=None):
    M, K = a.shape
    N = b.shape[0]
    tm, tn = _tile(M, tm), _tile(N, tn)
    ins = [a, b] + ([res] if res is not None else [])
    specs = [pl.BlockSpec((tm, K), lambda i, j: (i, 0)), pl.BlockSpec((tn, K), lambda i, j: (j, 0))]
    if res is not None:
        specs.append(pl.BlockSpec((tm, tn), lambda i, j: (i, j)))
    epi = (lambda acc, r_ref: (acc + r_ref[...],)) if res is not None else None
    return _mm(name, (M // tm, N // tn), ("parallel", "parallel"), ins, specs,
               [_sds((M, N), out_dtype)], [pl.BlockSpec((tm, tn), lambda i, j: (i, j))], NT, epi=epi)[0]


def mm_nn(name, a, b, out_dtype, tm=512, tn=512):
    M, K = a.shape
    N = b.shape[1]
    tm, tn = _tile(M, tm), _tile(N, tn)
    return _mm(name, (M // tm, N // tn), ("parallel", "parallel"), [a, b],
               [pl.BlockSpec((tm, K), lambda i, j: (i, 0)), pl.BlockSpec((K, tn), lambda i, j: (0, j))],
               [_sds((M, N), out_dtype)], [pl.BlockSpec((tm, tn), lambda i, j: (i, j))], NN)[0]


def mm_tn(name, a, b, out_dtype, tm=512, tn=1024):
    K, M = a.shape
    N = b.shape[1]
    tm, tn = _tile(M, tm), _tile(N, tn)
    return _mm(name, (M // tm, N // tn), ("parallel", "parallel"), [a, b],
               [pl.BlockSpec((K, tm), lambda i, j: (0, i)), pl.BlockSpec((K, tn), lambda i, j: (0, j))],
               [_sds((M, N), out_dtype)], [pl.BlockSpec((tm, tn), lambda i, j: (i, j))], TN)[0]


def rmsnorm_fwd(name, x, g, tm=512):
    T, D = x.shape
    tm = _tile(T, tm)

    def body(x_ref, g_ref, o_ref):
        xf = x_ref[...]
        r = lax.rsqrt(jnp.mean(xf * xf, axis=-1, keepdims=True) + EPS)
        o_ref[...] = ((xf * r) * g_ref[...]).astype(o_ref.dtype)

    return pl.pallas_call(
        body, name=name, grid=(T // tm,),
        in_specs=[pl.BlockSpec((tm, D), lambda i: (i, 0)), pl.BlockSpec((1, D), lambda i: (0, 0))],
        out_specs=pl.BlockSpec((tm, D), lambda i: (i, 0)), out_shape=_sds((T, D), MM_DTYPE),
        compiler_params=_cp(("parallel",)),
    )(x, g.reshape(1, D))


def _rms_bwd_math(xf, g, dh):
    r = lax.rsqrt(jnp.mean(xf * xf, axis=-1, keepdims=True) + EPS)
    gd = dh * g
    dx = r * gd - xf * ((r * r * r) * jnp.mean(gd * xf, axis=-1, keepdims=True))
    dg = jnp.sum(dh * (xf * r), axis=0, keepdims=True)
    return dx, dg


def rmsnorm_bwd(name, x, g, dh, res=None, tm=256):
    T, D = x.shape
    tm = _tile(T, tm)
    has_res = res is not None

    def body(*refs):
        x_ref, g_ref, dh_ref = refs[:3]
        dx_ref, dxb_ref, dg_ref = refs[3 + has_res:]
        dx, dg = _rms_bwd_math(x_ref[...], g_ref[...], dh_ref[...])
        if has_res:
            dx = dx + refs[3][...]
        dx_ref[...] = dx
        dxb_ref[...] = dx.astype(dxb_ref.dtype)

        @pl.when(pl.program_id(0) == 0)
        def _():
            dg_ref[...] = jnp.zeros_like(dg_ref)

        dg_ref[...] += dg

    row = pl.BlockSpec((tm, D), lambda i: (i, 0))
    vec = pl.BlockSpec((1, D), lambda i: (0, 0))
    ins = [x, g.reshape(1, D), dh] + ([res] if has_res else [])
    return pl.pallas_call(
        body, name=name, grid=(T // tm,), in_specs=[row, vec, row] + ([row] if has_res else []),
        out_specs=[row, row, vec], out_shape=[_sds((T, D), F32), _sds((T, D), MM_DTYPE), _sds((1, D), F32)],
        compiler_params=_cp(("arbitrary",)),
    )(*ins)


def final_loss_bwd(name, x, g, target, tm=256):
    T, D = x.shape
    tm = _tile(T, tm)

    def body(x_ref, g_ref, t_ref, loss_ref, dx_ref, dxb_ref, dg_ref):
        xf, gv = x_ref[...], g_ref[...]
        r = lax.rsqrt(jnp.mean(xf * xf, axis=-1, keepdims=True) + EPS)
        err = (xf * r) * gv - t_ref[...]
        part = 0.5 * jnp.sum(jnp.mean(err * err, axis=-1, keepdims=True), axis=0, keepdims=True)
        dx, dg = _rms_bwd_math(xf, gv, err / D)
        dx_ref[...] = dx
        dxb_ref[...] = dx.astype(dxb_ref.dtype)

        @pl.when(pl.program_id(0) == 0)
        def _():
            dg_ref[...] = jnp.zeros_like(dg_ref)
            loss_ref[...] = jnp.zeros_like(loss_ref)

        dg_ref[...] += dg
        loss_ref[...] += jnp.broadcast_to(part, loss_ref.shape)

    row = pl.BlockSpec((tm, D), lambda i: (i, 0))
    vec = pl.BlockSpec((1, D), lambda i: (0, 0))
    return pl.pallas_call(
        body, name=name, grid=(T // tm,), in_specs=[row, vec, row],
        out_specs=[pl.BlockSpec((8, LANE), lambda i: (0, 0)), row, row, vec],
        out_shape=[_sds((8, LANE), F32), _sds((T, D), F32), _sds((T, D), MM_DTYPE), _sds((1, D), F32)],
        compiler_params=_cp(("arbitrary",)),
    )(x, g.reshape(1, D), target)


def _pool_tiles(T, emix):
    pg = emix // len(POOL_WINDOWS)
    tc = 256 if pg % 256 == 0 else LANE
    return pg, tc, _tile(T, 512)


def pool_fwd(name, proj, T, emix):
    pg, tc, R = _pool_tiles(T, emix)
    per_g = pg // tc

    def body(u_ref, o_ref):
        g = pl.program_id(0) // per_g
        for gi, w in enumerate(POOL_WINDOWS):
            @pl.when(g == gi)
            def _():
                for ci in range(T // R):
                    r0 = ci * R
                    cur = u_ref[r0:r0 + R, :]
                    halo = jnp.zeros((POOL_HALO, tc), F32) if ci == 0 else u_ref[r0 - POOL_HALO:r0, :]
                    s = jnp.concatenate([halo, cur], axis=0)
                    for st in range(w.bit_length() - 1):
                        s = s + pltpu.roll(s, 1 << st, axis=0)
                    t = r0 + lax.broadcasted_iota(jnp.int32, (R, 1), 0)
                    cnt = jnp.minimum(t + 1, w).astype(F32)
                    o_ref[r0:r0 + R, :] = (s[POOL_HALO:] / cnt - cur).astype(o_ref.dtype)

    return pl.pallas_call(
        body, name=name, grid=(emix // tc,), in_specs=[pl.BlockSpec((T, tc), lambda j: (0, j))],
        out_specs=pl.BlockSpec((T, tc), lambda j: (0, j)), out_shape=_sds((T, emix), MM_DTYPE),
        compiler_params=_cp(("parallel",)),
    )(proj)


def pool_bwd(name, dpooled, dproj, T, emix):
    pg, tc, R = _pool_tiles(T, emix)
    per_g = pg // tc

    def body(d_ref, _, o_ref):
        g = pl.program_id(0) // per_g
        for gi, w in enumerate(POOL_WINDOWS):
            @pl.when(g == gi)
            def _():
                n = R + POOL_HALO
                for ci in range(T // R):
                    r0 = ci * R
                    cur = d_ref[r0:r0 + R, :]
                    halo = jnp.zeros((POOL_HALO, tc), F32) if ci == T // R - 1 else d_ref[r0 + R:r0 + n, :]
                    t = r0 + lax.broadcasted_iota(jnp.int32, (n, 1), 0)
                    cnt = jnp.minimum(t + 1, w).astype(F32)
                    s = jnp.concatenate([cur, halo], axis=0) / cnt
                    for st in range(w.bit_length() - 1):
                        s = s + pltpu.roll(s, n - (1 << st), axis=0)
                    o_ref[r0:r0 + R, :] = (s[:R] - cur).astype(o_ref.dtype)

    return pl.pallas_call(
        body, name=name, grid=(emix // tc,),
        in_specs=[pl.BlockSpec((T, tc), lambda j: (0, j)), pl.BlockSpec(memory_space=pl.ANY)],
        out_specs=pl.BlockSpec((T, tc), lambda j: (0, j)), out_shape=_sds(dproj.shape, dproj.dtype),
        input_output_aliases={1: 0}, compiler_params=_cp(("parallel",)),
    )(dpooled, dproj)


def grp_fwd(name, pooled, wgrp, scale, proj, gate_off, E):
    T, emix = pooled.shape
    pg = emix // len(POOL_WINDOWS)
    tm = _tile(T, 512)
    tc = 256 if pg % 256 == 0 and gate_off % 256 == 0 else LANE
    per = pg // tc

    def epi(acc, s_ref, g_ref):
        gt = g_ref[...]
        return acc, (acc * s_ref[...]) * (gt * _sig(gt))

    col = pl.BlockSpec((tm, tc), lambda n, i: (i, n))
    return _mm(name, (emix // tc, T // tm), ("parallel", "parallel"), [pooled, wgrp, scale, proj],
               [pl.BlockSpec((tm, pg), lambda n, i: (i, n // per)), pl.BlockSpec((None, pg, tc), lambda n, i: (n // per, 0, n % per)),
                pl.BlockSpec((1, tc), lambda n, i: (0, n)), pl.BlockSpec((tm, tc), lambda n, i: (i, gate_off // tc + n))],
               [_sds((T, emix), F32), _sds((T, E), MM_DTYPE)], [col, col], NN, epi=epi)


def grp_bwd_x(name, dy, wgrp):
    T, emix = dy.shape
    pg = emix // len(POOL_WINDOWS)
    tm = _tile(T, 512)
    return _mm(name, (len(POOL_WINDOWS), T // tm), ("parallel", "parallel"), [dy, wgrp],
               [pl.BlockSpec((tm, pg), lambda g, i: (i, g)), pl.BlockSpec((None, pg, pg), lambda g, i: (g, 0, 0))],
               [_sds((T, emix), F32)], [pl.BlockSpec((tm, pg), lambda g, i: (i, g))], NT)[0]


def grp_bwd_w(name, pooled, dy):
    T, emix = dy.shape
    ng = len(POOL_WINDOWS)
    pg = emix // ng
    col = pl.BlockSpec((T, pg), lambda g: (0, g))
    return _mm(name, (ng,), ("parallel",), [pooled, dy], [col, col],
               [_sds((ng, pg, pg), COMM_DTYPE)], [pl.BlockSpec((None, pg, pg), lambda g: (g, 0, 0))], TN)[0]


def gate_bwd_pool(name, dbranch, y, scale, proj, gate_off, n_proj):
    T, emix = y.shape
    tm, tc = _tile(T, 256), _tile(emix, 512)
    assert gate_off % tc == 0

    def body(db_ref, y_ref, s_ref, g_ref, dy_ref, dg_ref, ds_ref):
        db, yv, sc, gt = db_ref[...], y_ref[...], s_ref[...], g_ref[...]
        sg = _sig(gt)
        dmix = db * (gt * sg)
        dy_ref[...] = (dmix * sc).astype(dy_ref.dtype)
        dg_ref[...] = (db * (yv * sc) * (sg * (1.0 + gt * (1.0 - sg)))).astype(dg_ref.dtype)

        @pl.when(pl.program_id(1) == 0)
        def _():
            ds_ref[...] = jnp.zeros_like(ds_ref)

        ds_ref[...] += jnp.sum(dmix * yv, axis=0, keepdims=True)

    blk = pl.BlockSpec((tm, tc), lambda j, i: (i, j))
    vec = pl.BlockSpec((1, tc), lambda j, i: (0, j))
    gate = pl.BlockSpec((tm, tc), lambda j, i: (i, gate_off // tc + j))
    return pl.pallas_call(
        body, name=name, grid=(emix // tc, T // tm), in_specs=[blk, blk, vec, gate],
        out_specs=[blk, gate, vec],
        out_shape=[_sds((T, emix), MM_DTYPE), _sds((T, n_proj), MM_DTYPE), _sds((1, emix), F32)],
        compiler_params=_cp(("parallel", "arbitrary")),
    )(dbranch, y, scale, proj)


def attn_fwd(name, proj, kv, branch, q_off, gate_off, tm=512):
    T = proj.shape[0]
    M, two_eca = kv.shape
    eca = two_eca // 2
    hd = eca // CA_HEADS
    E = branch.shape[1]
    tm = _tile(T, tm)
    scale = hd ** -0.5

    def body(q_ref, g_ref, k_ref, v_ref, _, ca_ref, br_ref):
        s = _dot(q_ref[...], k_ref[...], NT) * scale
        s = s - jnp.max(s, axis=-1, keepdims=True)
        e = jnp.exp(s)
        p = e / jnp.sum(e, axis=-1, keepdims=True)
        o = _dot(p, v_ref[...], NN)
        gt = g_ref[...]
        ca_ref[...] = o
        br_ref[...] = (o * (gt * _sig(gt))).astype(br_ref.dtype)

    return pl.pallas_call(
        body, name=name, grid=(CA_HEADS, T // tm),
        in_specs=[pl.BlockSpec((tm, hd), lambda h, i: (i, q_off // hd + h)),
                  pl.BlockSpec((tm, hd), lambda h, i: (i, (gate_off + E - eca) // hd + h)),
                  pl.BlockSpec((M, hd), lambda h, i: (0, h)), pl.BlockSpec((M, hd), lambda h, i: (0, CA_HEADS + h)),
                  pl.BlockSpec(memory_space=pl.ANY)],
        out_specs=[pl.BlockSpec((tm, hd), lambda h, i: (i, h)), pl.BlockSpec((tm, hd), lambda h, i: (i, (E - eca) // hd + h))],
        out_shape=[_sds((T, eca), F32), _sds(branch.shape, branch.dtype)],
        input_output_aliases={4: 1}, compiler_params=_cp(("parallel", "parallel")),
    )(proj, proj, kv, kv, branch)


def attn_bwd(name, proj, kv, ca, dbranch, dproj, q_off, gate_off, tm=512):
    T = proj.shape[0]
    M, two_eca = kv.shape
    eca = two_eca // 2
    hd = eca // CA_HEADS
    E = dbranch.shape[1]
    tm = _tile(T, tm)
    scale = hd ** -0.5
    q_blk, g_blk = q_off // hd, (gate_off + E - eca) // hd

    def body(q_ref, g_ref, k_ref, v_ref, ca_ref, db_ref, _, dp_ref, dk_ref, dv_ref, dgate_buf):
        first_rows, part = pl.program_id(1) == 0, pl.program_id(2)

        @pl.when(part == 0)
        def _():
            q, k, v = q_ref[...], k_ref[...], v_ref[...]
            s = _dot(q, k, NT) * scale
            s = s - jnp.max(s, axis=-1, keepdims=True)
            e = jnp.exp(s)
            p = e / jnp.sum(e, axis=-1, keepdims=True)
            gt, db = g_ref[...], db_ref[...]
            sg = _sig(gt)
            do = db * (gt * sg)
            dgate_buf[...] = (db * ca_ref[...] * (sg * (1.0 + gt * (1.0 - sg)))).astype(dgate_buf.dtype)
            dpr = _dot(do, v, NT)
            ds = p * (dpr - jnp.sum(dpr * p, axis=-1, keepdims=True)) * scale
            dp_ref[...] = _dot(ds, k, NN).astype(dp_ref.dtype)

            @pl.when(first_rows)
            def _():
                dk_ref[...] = jnp.zeros_like(dk_ref)
                dv_ref[...] = jnp.zeros_like(dv_ref)

            dk_ref[...] += _dot(ds, q, TN)
            dv_ref[...] += _dot(p, do, TN)

        @pl.when(part == 1)
        def _():
            dp_ref[...] = dgate_buf[...]

    row = lambda blk: pl.BlockSpec((tm, hd), lambda h, i, p: (i, blk + h))
    out_dp = pl.BlockSpec((tm, hd), lambda h, i, p: (i, jnp.where(p == 0, q_blk, g_blk) + h))
    outs = pl.pallas_call(
        body, name=name, grid=(CA_HEADS, T // tm, 2),
        in_specs=[row(q_blk), row(g_blk), pl.BlockSpec((M, hd), lambda h, i, p: (0, h)),
                  pl.BlockSpec((M, hd), lambda h, i, p: (0, CA_HEADS + h)), row(0), row((E - eca) // hd),
                  pl.BlockSpec(memory_space=pl.ANY)],
        out_specs=[out_dp, pl.BlockSpec((M, hd), lambda h, i, p: (0, h)), pl.BlockSpec((M, hd), lambda h, i, p: (0, h))],
        out_shape=[_sds(dproj.shape, dproj.dtype), _sds((M, eca), F32), _sds((M, eca), F32)],
        scratch_shapes=[pltpu.VMEM((tm, hd), dproj.dtype)],
        input_output_aliases={6: 0}, compiler_params=_cp(("parallel", "arbitrary", "arbitrary")),
    )(proj, proj, kv, kv, ca, dbranch, dproj)
    return outs[0], jnp.concatenate([outs[1], outs[2]], axis=1)


def _split3(v):
    hi = v.astype(BF16)
    r1 = v - hi.astype(F32)
    mid = r1.astype(BF16)
    lo = (r1 - mid.astype(F32)).astype(BF16)
    return hi, mid, lo


def _tri_dot(tri, v):
    hi, mid, lo = _split3(v)
    d = functools.partial(lax.dot_general, dimension_numbers=NN, preferred_element_type=F32)
    return d(tri, hi) + d(tri, mid) + d(tri, lo)


def _causal_mask():
    C = HG_CHUNK
    return lax.broadcasted_iota(jnp.int32, (C, C), 0) >= lax.broadcasted_iota(jnp.int32, (C, C), 1)


def _cumsum_matrices(G):
    ri = lax.broadcasted_iota(jnp.int32, (G, G), 0)
    ci = lax.broadcasted_iota(jnp.int32, (G, G), 1)
    same = (ri // HG_CHUNK) == (ci // HG_CHUNK)
    return (jnp.where(same & (ri >= ci), 1.0, 0.0).astype(BF16), jnp.where(same & (ri <= ci), 1.0, 0.0).astype(BF16))


def _forget_gate(fi, lb):
    sig = _sig(fi)
    return sig, lb + (1.0 - lb) * sig


def _log_decay_cumsum(f_ref, b_s, lb, prefix, G):
    def group(i, carry):
        rows = pl.ds(pl.multiple_of(i * G, G), G)
        _, f = _forget_gate(f_ref[rows, :], lb)
        b_s[rows, :] = _tri_dot(prefix, jnp.log(f))
        return carry

    lax.fori_loop(0, b_s.shape[0] // G, group, 0)


def _lower_bound(lb2):
    l0, l1 = lb2[0:1, :], lb2[1:2, :]
    m = jnp.maximum(l0, l1)
    e0, e1 = jnp.exp(l0 - m), jnp.exp(l1 - m)
    sm0, sm1 = e0 / (e0 + e1), e1 / (e0 + e1)
    return (sm0 + sm1) - sm0, sm0, sm1


def _hg_chunk_fwd(qi, fi, lb, causal, b=None, prefix=None):
    sig, f = _forget_gate(fi, lb)
    k = 1.0 - f
    if b is None:
        b = _tri_dot(prefix, jnp.log(f))
    last = lax.broadcasted_iota(jnp.int32, b.shape, 0) == HG_CHUNK - 1
    bl = jnp.sum(jnp.where(last, b, 0.0), axis=0, keepdims=True)
    eb, enb, ete = jnp.exp(b), jnp.exp(-b), jnp.exp(bl - b)
    sq = _sig(qi)
    q = (qi * sq) * (HG_HEAD_DIM ** -0.5)
    q_dec, k_inv, k_te = q * eb, k * enb, k * ete
    att = jnp.where(causal, _dot(q_dec, k_inv, NT), 0.0)
    return dict(sig=sig, f=f, k=k, eb=eb, enb=enb, ete=ete, dcy=jnp.exp(bl), sq=sq, q_dec=q_dec, k_inv=k_inv, k_te=k_te,
                att=att, last=last)


def _hg_blocking(H, T):
    hb = 4 if H % 4 == 0 else (2 if H % 2 == 0 else 1)
    return hb, _tile(T, 1024)


def hgrn_fwd(name, proj, lb2, norm_g, emix, gate_off, E):
    T = proj.shape[0]
    hd, C = HG_HEAD_DIM, HG_CHUNK
    H, N = emix // hd, T // C
    hb, ts = _hg_blocking(H, T)
    W, nc, G = hb * hd, ts // C, _tile(ts, 256)

    def body(q_ref, f_ref, i_ref, g_ref, lb_ref, ng_ref, o_ref, st_ref, br_ref, state, b_s):
        @pl.when(pl.program_id(1) == 0)
        def _():
            state[...] = jnp.zeros_like(state)

        causal = _causal_mask()
        lb, _, _ = _lower_bound(lb_ref[...])
        ng = ng_ref[...]
        _log_decay_cumsum(f_ref, b_s, lb, _cumsum_matrices(G)[0], G)

        def chunk(n, carry):
            rows = pl.ds(pl.multiple_of(n * C, C), C)
            for h in range(hb):
                cols = slice(h * hd, (h + 1) * hd)
                v = i_ref[rows, cols]
                c = _hg_chunk_fwd(q_ref[rows, cols], f_ref[rows, cols], lb[:, cols], causal, b=b_s[rows, cols])
                st = state[h]
                o = _dot(c["att"], v, NN) + _dot(c["q_dec"], st, NT)
                st_ref[h, n] = st
                o_ref[rows, cols] = o
                r = lax.rsqrt(jnp.mean(o * o, axis=-1, keepdims=True) + EPS)
                gt = g_ref[rows, cols]
                br_ref[rows, cols] = (((o * r) * ng[:, cols]) * (gt * _sig(gt))).astype(br_ref.dtype)
                state[h] = st * c["dcy"] + _dot(v, c["k_te"], TN)
            return carry

        lax.fori_loop(0, nc, chunk, 0, unroll=4)

    col = lambda off: pl.BlockSpec((ts, W), lambda h, s: (s, off // W + h))
    vec = lambda r: pl.BlockSpec((r, W), lambda h, s: (0, h))
    return pl.pallas_call(
        body, name=name, grid=(H // hb, T // ts),
        in_specs=[col(0), col(emix), col(2 * emix), col(gate_off), vec(2), vec(1)],
        out_specs=[col(0), pl.BlockSpec((hb, nc, hd, hd), lambda h, s: (h, s, 0, 0)), col(0)],
        out_shape=[_sds((T, emix), F32), _sds((H, N, hd, hd), F32), _sds((T, E), MM_DTYPE)],
        scratch_shapes=[pltpu.VMEM((hb, hd, hd), F32), pltpu.VMEM((ts, W), F32)],
        compiler_params=_cp(("parallel", "arbitrary")),
    )(proj, proj, proj, proj, lb2, norm_g)


def hgrn_bwd(name, proj, lb2, norm_g, o_pre, states, dbranch, emix, gate_off):
    T, n_proj = proj.shape
    hd, C = HG_HEAD_DIM, HG_CHUNK
    H, N = emix // hd, T // C
    hb, ts = _hg_blocking(H, T)
    W, nc, ns = hb * hd, ts // C, T // ts
    qscale = HG_HEAD_DIM ** -0.5

    def body(q_ref, f_ref, i_ref, g_ref, lb_ref, ng_ref, o_ref, st_ref, db_ref, dp_ref, dlb_ref, dng_ref, stash, dst_ref, acc):
        first_block, part = pl.program_id(1) == 0, pl.program_id(2)

        @pl.when(part == 0)
        def _():
            @pl.when(first_block)
            def _():
                dst_ref[...] = jnp.zeros_like(dst_ref)
                acc[...] = jnp.zeros_like(acc)

            causal = _causal_mask()
            prefix, suffix = _cumsum_matrices(C)
            lb, sm0, sm1 = _lower_bound(lb_ref[...])
            ng = ng_ref[...]

            def chunk(j, carry):
                n = nc - 1 - j
                rows = pl.ds(pl.multiple_of(n * C, C), C)
                for h in range(hb):
                    cols = slice(h * hd, (h + 1) * hd)
                    dst = dst_ref[h]
                    qi, v, gt = q_ref[rows, cols], i_ref[rows, cols], g_ref[rows, cols]
                    c = _hg_chunk_fwd(qi, f_ref[rows, cols], lb[:, cols], causal, prefix=prefix)
                    st = st_ref[h, n]
                    o, db, ngh = o_ref[rows, cols], db_ref[rows, cols], ng[:, cols]
                    r = lax.rsqrt(jnp.mean(o * o, axis=-1, keepdims=True) + EPS)
                    sg = _sig(gt)
                    dmix = db * (gt * sg)
                    dgate = db * ((o * r) * ngh) * (sg * (1.0 + gt * (1.0 - sg)))
                    dyn = dmix * ngh
                    do = r * dyn - o * ((r * r * r) * jnp.mean(dyn * o, axis=-1, keepdims=True))
                    acc[1:2, cols] += jnp.sum(dmix * (o * r), axis=0, keepdims=True)
                    datt = jnp.where(causal, _dot(do, v, NT), 0.0)
                    dq_dec = _dot(datt, c["k_inv"], NN) + _dot(do, st, NN)
                    dk_inv = _dot(datt, c["q_dec"], TN)
                    dv = _dot(c["att"], do, TN) + _dot(c["k_te"], dst, NT)
                    dk_te = _dot(v, dst, NN)
                    ddcy = jnp.sum(dst * st, axis=0, keepdims=True)
                    dst_ref[h] = _dot(do, c["q_dec"], TN) + dst * c["dcy"]
                    t_te = dk_te * c["k_te"]
                    dbv = dq_dec * c["q_dec"] - dk_inv * c["k_inv"] - t_te
                    dbl = jnp.sum(t_te, axis=0, keepdims=True) + ddcy * c["dcy"]
                    dbv = dbv + jnp.where(c["last"], dbl, 0.0)
                    dlf = _tri_dot(suffix, dbv)
                    dk = dk_inv * c["enb"] + dk_te * c["ete"]
                    df = dlf / c["f"] - dk
                    sig, sq = c["sig"], c["sq"]
                    acc[0:1, cols] += jnp.sum(df * (1.0 - sig), axis=0, keepdims=True)
                    dqi = (dq_dec * c["eb"]) * qscale * (sq * (1.0 + qi * (1.0 - sq)))
                    dp_ref[rows, cols] = dqi.astype(dp_ref.dtype)
                    stash[0, rows, cols] = (df * (1.0 - lb[:, cols]) * (sig * (1.0 - sig))).astype(stash.dtype)
                    stash[1, rows, cols] = dv.astype(stash.dtype)
                    stash[2, rows, cols] = dgate.astype(stash.dtype)
                return carry

            lax.fori_loop(0, nc, chunk, 0, unroll=4)
            t = acc[0:1, :] * (sm0 * sm1)
            dlb_ref[...] = jnp.concatenate([-t, t], axis=0)
            dng_ref[...] = acc[1:2, :]

        for pi in range(3):
            @pl.when(part == pi + 1)
            def _():
                dp_ref[...] = stash[pi]

    col = lambda off: pl.BlockSpec((ts, W), lambda h, s, p: (ns - 1 - s, off // W + h))
    vec = lambda r: pl.BlockSpec((r, W), lambda h, s, p: (0, h))
    n_mix = emix // W
    out_col = pl.BlockSpec((ts, W), lambda h, s, p: (ns - 1 - s, jnp.where(p < 3, p * n_mix, gate_off // W) + h))
    return pl.pallas_call(
        body, name=name, grid=(H // hb, ns, 4),
        in_specs=[col(0), col(emix), col(2 * emix), col(gate_off), vec(2), vec(1), col(0),
                  pl.BlockSpec((hb, nc, hd, hd), lambda h, s, p: (h, ns - 1 - s, 0, 0)), col(0)],
        out_specs=[out_col, vec(2), vec(1)],
        out_shape=[_sds((T, n_proj), MM_DTYPE), _sds((2, emix), F32), _sds((1, emix), F32)],
        scratch_shapes=[pltpu.VMEM((3, ts, W), MM_DTYPE), pltpu.VMEM((hb, hd, hd), F32), pltpu.VMEM((8, W), F32)],
        compiler_params=_cp(("parallel", "arbitrary", "arbitrary")),
    )(proj, proj, proj, proj, lb2, norm_g, o_pre, states, dbranch)


class WholeWeights:
    def __init__(self, wkv, wout, wpin, wgrp, whin, hgrn_norm_g):
        self.w = dict(wkv=wkv, wout=wout, wpin=wpin, wgrp=wgrp, whin=whin, hnorm=hgrn_norm_g)
        self.grads = {}

    def first_weights(self):
        return self.w["wpin"], self.w["hnorm"], 0.0

    def after_proj0(self, after):
        pass

    def layer0_weights(self, after):
        return self.w["wgrp"], self.w["wkv"][0], self.w["wout"][0]

    def after_attn0(self, after):
        pass

    def layer1_weights(self, after):
        return self.w["whin"]

    def after_proj1(self, after):
        pass

    def layer1_rest(self, after):
        return self.w["wkv"][1], self.w["wout"][1]

    def grads_ready(self, layer, grads):
        self.grads[layer] = grads
        return 0.0

    def grads_reduce(self, layer, after):
        return 0.0


def run_step(x, mem, target, norm_g, mem_norm_g, pool_scale, hgrn_lb, final_g, comm):
    T, D = x.shape
    mem_n = rmsnorm_fwd("mem_norm", mem, mem_norm_g)
    wpin, hgrn_norm_g, zero = comm.first_weights()
    h0 = rmsnorm_fwd("norm0", x, norm_g[0] + zero)
    proj0 = proj_fwd("proj0", h0, wpin)
    comm.after_proj0(proj0)
    n0 = proj0.shape[1]
    E = n0 // 2
    eca = E // 4
    emix = E - eca
    q0, g0 = emix, emix + eca
    q1, g1 = 3 * emix, 3 * emix + eca
    pooled = pool_fwd("pool_fwd", proj0, T, emix)
    wgrp, wkv0, wout0 = comm.layer0_weights(pooled)
    y0, branch0 = grp_fwd("grp_fwd", pooled, wgrp, pool_scale, proj0, g0, E)
    kv0 = mm_nn("kv0", mem_n, wkv0, MM_DTYPE)
    ca0, branch0 = attn_fwd("attn_fwd0", proj0, kv0, branch0, q0, g0)
    comm.after_attn0(ca0)
    x1 = out_fwd("out0", branch0, wout0, x)
    h1 = rmsnorm_fwd("norm1", x1, norm_g[1])
    whin = comm.layer1_weights(h1)
    proj1 = proj_fwd("proj1", h1, whin)
    comm.after_proj1(proj1)
    o1, states, branch1 = hgrn_fwd("hgrn_fwd", proj1, hgrn_lb, hgrn_norm_g, emix, g1, E)
    wkv1, wout1 = comm.layer1_rest(o1)
    kv1 = mm_nn("kv1", mem_n, wkv1, MM_DTYPE)
    ca1, branch1 = attn_fwd("attn_fwd1", proj1, kv1, branch1, q1, g1)
    x2 = out_fwd("out1", branch1, wout1, x1)
    loss, dx2, dx2b, g_final = final_loss_bwd("final", x2, final_g, target)
    g_wout1 = mm_tn("gwout1", branch1, dx2b, COMM_DTYPE)
    dbranch1 = mm_nt("dbranch1", dx2b, wout1, F32)
    dproj1, g_lb, g_hnorm = hgrn_bwd("hgrn_bwd", proj1, hgrn_lb, hgrn_norm_g, o1, states, dbranch1, emix, g1)
    dproj1, dkv1 = attn_bwd("attn_bwd1", proj1, kv1, ca1, dbranch1, dproj1, q1, g1)
    g_wkv1 = mm_tn("gwkv1", mem_n, dkv1, COMM_DTYPE)
    dmem_n = mm_nt("dmem1", dkv1, wkv1, F32)
    g_whin = proj_bwd_w("gwhin", h1, dproj1, whin.shape[0])
    zero = comm.grads_ready(1, dict(w_kv=g_wkv1, w_out=g_wout1, hgrn_w_in=g_whin))
    dh1 = proj_bwd_x("dh1", dproj1, whin)
    zero = zero + comm.grads_reduce(1, dh1)
    dx1, dx1b, g_norm1 = rmsnorm_bwd("norm1_bwd", x1, norm_g[1] + zero, dh1, res=dx2)
    g_wout0 = mm_tn("gwout0", branch0, dx1b, COMM_DTYPE)
    comm.grads_ready("0a", dict(w_out=g_wout0))
    dbranch0 = mm_nt("dbranch0", dx1b, wout0, F32)
    zero = comm.grads_reduce("0a", dbranch0)
    dy0, dproj0, g_scale = gate_bwd_pool("gate_bwd0", dbranch0, y0, pool_scale + zero, proj0, g0, n0)
    g_wgrp = grp_bwd_w("gwgrp", pooled, dy0)
    dpooled = grp_bwd_x("dpooled", dy0, wgrp)
    dproj0 = pool_bwd("pool_bwd", dpooled, dproj0, T, emix)
    dproj0, dkv0 = attn_bwd("attn_bwd0", proj0, kv0, ca0, dbranch0, dproj0, q0, g0)
    g_wkv0 = mm_tn("gwkv0", mem_n, dkv0, COMM_DTYPE)
    dmem_n = mm_nt("dmem0", dkv0, wkv0, F32, res=dmem_n)
    g_wpin = proj_bwd_w("gwpin", h0, dproj0, wpin.shape[0])
    zero = comm.grads_ready("0b", dict(w_kv=g_wkv0, pool_w_in=g_wpin, pool_w_grp=g_wgrp))
    dh0 = proj_bwd_x("dh0", dproj0, wpin)
    zero = zero + comm.grads_reduce("0b", dh0)
    grad_x, _, g_norm0 = rmsnorm_bwd("norm0_bwd", x, norm_g[0] + zero, dh0, res=dx1)
    _, _, g_mem = rmsnorm_bwd("mem_norm_bwd", mem, mem_norm_g, dmem_n)

    small = dict(norm_g=jnp.concatenate([g_norm0, g_norm1], axis=0), mem_norm_g=g_mem[0], pool_scale=g_scale,
                 hgrn_lb=g_lb, hgrn_norm_g=g_hnorm, final_g=g_final[0])
    return loss[0, 0], grad_x, small


def _position():
    return lax.axis_index("x"), lax.axis_index("y"), lax.axis_index("c")


def _slot(p):
    return 4 * p[0] + 2 * p[1] + p[2]


def all_gather_blocks(name, blocks):
    n = len(blocks)

    def body(*refs):
        ins, outs, token = refs[:n], refs[n:2 * n], refs[2 * n]
        send_sems, recv_sems, local_sems = refs[2 * n + 1:]
        token[...] = jnp.zeros_like(token)
        x, y, c = _position()
        me, sibling = (x, y, c), (x, y, 1 - c)
        chips = _other_chips(x, y)

        def copy(t, k, block, to, src=None):
            dst = outs[t].at[_slot(block)]
            return pltpu.make_async_remote_copy(src_ref=dst if src is None else src, dst_ref=dst, send_sem=send_sems.at[t, k],
                                                recv_sem=recv_sems.at[t, k], device_id=to, device_id_type=MESH)

        mine = [pltpu.make_async_copy(ins[t], outs[t].at[_slot(me)], local_sems.at[t]) for t in range(n)]
        for cp in mine:
            cp.start()
        sends = []
        for t in range(n):
            sends.append(copy(t, 0, me, sibling, src=ins[t]))
            sends += [copy(t, 1 + j, me, (*chip, c), src=ins[t]) for j, chip in enumerate(chips)]
        for cp in sends:
            cp.start()
        for t in range(n):
            for j, chip in enumerate(chips):
                copy(t, 1 + j, (*chip, c), me).wait_recv()
                passed = copy(t, 4 + j, (*chip, c), sibling)
                passed.start()
                sends.append(passed)
        for t in range(n):
            copy(t, 0, sibling, me).wait_recv()
            for j, chip in enumerate(chips):
                copy(t, 4 + j, (*chip, 1 - c), me).wait_recv()
        for cp in sends:
            cp.wait_send()
        for cp in mine:
            cp.wait()

    any_spec = pl.BlockSpec(memory_space=pl.ANY)
    return pl.pallas_call(
        body, name=name, in_specs=[any_spec] * n, out_specs=[any_spec] * n + [pl.BlockSpec(memory_space=pltpu.VMEM)],
        out_shape=[_sds((N_DEV,) + b.shape, b.dtype) for b in blocks] + [_sds((8, LANE), F32)],
        scratch_shapes=[pltpu.SemaphoreType.DMA((n, 7)), pltpu.SemaphoreType.DMA((n, 7)), pltpu.SemaphoreType.DMA((n,))],
        compiler_params=pltpu.CompilerParams(has_side_effects=True),
    )(*blocks)


_HBM = pl.BlockSpec(memory_space=pltpu.HBM)
_SEM = pl.BlockSpec(memory_space=pltpu.SEMAPHORE)
_EFFECT = pltpu.SideEffectType.DATAFLOW_SIDE_EFFECTING


def split_start(name, bufs, n_copies, build):
    n = len(bufs)

    def body(*refs):
        for cp in build(refs[:n], refs[n], refs[n + 1]):
            cp.start()
        refs[-1][...] = jnp.zeros_like(refs[-1])

    outs = pl.pallas_call(
        body, name=name, in_specs=[_HBM] * n,
        out_shape=(pltpu.SemaphoreType.DMA((n_copies,)), pltpu.SemaphoreType.DMA((n_copies,)),
                   *[pltpu.HBM(b.shape, b.dtype) for b in bufs], _sds((8, LANE), F32)),
        out_specs=(_SEM, _SEM, *[_HBM] * n, pl.BlockSpec(memory_space=pltpu.VMEM)),
        input_output_aliases={i: 2 + i for i in range(n)},
        compiler_params=pltpu.CompilerParams(has_side_effects=_EFFECT),
    )(*[pltpu.with_memory_space_constraint(b, pltpu.HBM) for b in bufs])
    return (outs[0], outs[1]), list(outs[2:2 + n]), outs[-1]


def split_wait(name, sems, bufs, build, after):
    n = len(bufs)

    def body(*refs):
        for cp in build(refs[:n], refs[n], refs[n + 1]):
            cp.wait()

    return list(pl.pallas_call(
        body, name=name, in_specs=[_HBM] * n + [_SEM, _SEM, pl.BlockSpec(memory_space=pl.ANY)],
        out_shape=[pltpu.HBM(b.shape, b.dtype) for b in bufs], out_specs=[_HBM] * n,
        input_output_aliases={i: i for i in range(n)},
        compiler_params=pltpu.CompilerParams(has_side_effects=_EFFECT),
    )(*bufs, sems[0], sems[1], after))


def _remote(src, dst, send_sems, recv_sems, k, to):
    return pltpu.make_async_remote_copy(src_ref=src, dst_ref=dst, send_sem=send_sems.at[k], recv_sem=recv_sems.at[k],
                                        device_id=to, device_id_type=MESH)


def _other_chips(x, y):
    return [(1 - x, y), (x, 1 - y), (1 - x, 1 - y)]


def _gather_cross(n):
    def build(refs, ss, rs):
        x, y, c = _position()
        me = _slot((x, y, c))
        targets = [(x, y, 1 - c)] + [(*chip, c) for chip in _other_chips(x, y)]
        return [_remote(refs[t], refs[n + t].at[me], ss, rs, 4 * t + k, to) for t in range(n) for k, to in enumerate(targets)]
    return build


def _gather_pass(n):
    def build(refs, ss, rs):
        x, y, c = _position()
        cps = []
        for t in range(n):
            for j, chip in enumerate(_other_chips(x, y)):
                blk = refs[t].at[_slot((*chip, c))]
                cps.append(_remote(blk, blk, ss, rs, 3 * t + j, (x, y, 1 - c)))
        return cps
    return build


def _reduce_pair(n):
    def build(refs, ss, rs):
        x, y, c = _position()
        return [_remote(refs[t].at[j, 1 - c], refs[n + t].at[j], ss, rs, 4 * t + j, (x, y, 1 - c))
                for t in range(n) for j in range(4)]
    return build


def _reduce_cross(n):
    def build(refs, ss, rs):
        x, y, c = _position()
        return [_remote(refs[t].at[2 * chip[0] + chip[1]], refs[n + t].at[r], ss, rs, 3 * t + r, (*chip, c))
                for t in range(n) for r, chip in enumerate(_other_chips(x, y))]
    return build


def pair_add(name, part, landed, tr=256):
    _, _, R, C = part.shape
    tr = _row_tile(R, tr)

    def body(_, p_ref, l_ref, o_ref):
        o_ref[...] = (p_ref[...].astype(F32) + l_ref[...].astype(F32)).astype(o_ref.dtype)

    blk = pl.BlockSpec((None, tr, C), lambda j, i, core: (j, i, 0))
    core = lax.axis_index("c").astype(jnp.int32).reshape(1)
    return pl.pallas_call(
        body, name=name, out_shape=_sds((4, R, C), part.dtype), compiler_params=_cp(("parallel", "parallel")),
        grid_spec=pltpu.PrefetchScalarGridSpec(
            num_scalar_prefetch=1, grid=(4, R // tr),
            in_specs=[pl.BlockSpec((None, None, tr, C), lambda j, i, core: (j, core[0], i, 0)), blk], out_specs=blk),
    )(core, part, landed)


def _row_tile(R, pref):
    if R <= pref:
        return R
    t = (pref // 16) * 16
    while R % t:
        t -= 16
    return t


def all_reduce_small(name, vec, after):
    def body(v_ref, _, sum_ref, land_ref, send_sems, recv_sems):
        x, y, c = _position()
        me = (x, y, c)
        flip = lambda v, bit: 1 - v if bit else v
        peers = [(flip(x, k & 4), flip(y, k & 2), flip(c, k & 1)) for k in range(1, N_DEV)]
        land_ref[_slot(me)] = v_ref[...]
        sends = [pltpu.make_async_remote_copy(src_ref=v_ref, dst_ref=land_ref.at[_slot(me)], send_sem=send_sems.at[k],
                                              recv_sem=recv_sems.at[k], device_id=peer, device_id_type=MESH)
                 for k, peer in enumerate(peers)]
        for cp in sends:
            cp.start()
        for k, peer in enumerate(peers):
            pltpu.make_async_remote_copy(src_ref=v_ref, dst_ref=land_ref.at[_slot(peer)], send_sem=send_sems.at[k],
                                         recv_sem=recv_sems.at[k], device_id=peer, device_id_type=MESH).wait_recv()
        acc = land_ref[0]
        for s in range(1, N_DEV):
            acc = acc + land_ref[s]
        sum_ref[...] = acc
        for cp in sends:
            cp.wait_send()

    vm = pl.BlockSpec(memory_space=pltpu.VMEM)
    return pl.pallas_call(
        body, name=name, in_specs=[vm, pl.BlockSpec(memory_space=pl.ANY)], out_specs=[vm, vm],
        out_shape=[_sds(vec.shape, F32), _sds((N_DEV,) + vec.shape, F32)],
        scratch_shapes=[pltpu.SemaphoreType.DMA((7,)), pltpu.SemaphoreType.DMA((7,))],
        compiler_params=pltpu.CompilerParams(has_side_effects=True),
    )(vec, after)[0]


def _adamw_math(g, w, m, v):
    m2 = ADAM_B1 * m + (1.0 - ADAM_B1) * g
    v2 = ADAM_B2 * v + (1.0 - ADAM_B2) * (g * g)
    m_hat = m2 / (1.0 - ADAM_B1 ** ADAM_STEP)
    v_hat = v2 / (1.0 - ADAM_B2 ** ADAM_STEP)
    return -ADAM_LR * (m_hat / (jnp.sqrt(v_hat) + ADAM_EPS) + ADAM_WD * w), m2, v2


def adamw_shard(name, chip_parts, landed, w, m, v, layer, prev=None, tr=128):
    L, R, C = w.shape
    tr = _row_tile(R, tr)
    n_in = 5 + (4 if prev is not None else 0)

    def body(*refs):
        own_ref, land_ref, w_ref, m_ref, v_ref = refs[1:6]
        g_ref, d_ref, nm_ref, nv_ref = refs[1 + n_in:]
        g = own_ref[...].astype(F32)
        for s in range(landed.shape[0]):
            g = g + land_ref[s].astype(F32)
        g_ref[...] = g
        d_ref[...], nm_ref[...], nv_ref[...] = _adamw_math(g, w_ref[...], m_ref[...], v_ref[...])

    lay = pl.BlockSpec((None, tr, C), lambda i, chip: (layer, i, 0))
    own = pl.BlockSpec((None, tr, C), lambda i, chip: (chip[0], i, 0))
    chip = (2 * lax.axis_index("x") + lax.axis_index("y")).astype(jnp.int32).reshape(1)
    ins = [chip_parts, landed, w, m, v] + (list(prev) if prev is not None else [])
    return pl.pallas_call(
        body, name=name, out_shape=[_sds((L, R, C), F32)] * 4,
        grid_spec=pltpu.PrefetchScalarGridSpec(
            num_scalar_prefetch=1, grid=(R // tr,),
            in_specs=[own, pl.BlockSpec((landed.shape[0], tr, C), lambda i, chip: (0, i, 0)), lay, lay, lay]
            + [pl.BlockSpec(memory_space=pl.ANY)] * (n_in - 5),
            out_specs=[lay] * 4),
        input_output_aliases={6 + i: i for i in range(n_in - 5)}, compiler_params=_cp(("parallel",)),
    )(chip, *ins)


def adamw_small(name, g, w, m, v):
    def body(g_ref, w_ref, m_ref, v_ref, d_ref, nm_ref, nv_ref):
        d_ref[...], nm_ref[...], nv_ref[...] = _adamw_math(g_ref[...], w_ref[...], m_ref[...], v_ref[...])

    return pl.pallas_call(body, name=name, out_shape=[_sds(w.shape, F32)] * 3)(g, w, m, v)


def _pack(vs):
    flat = jnp.concatenate([v.reshape(-1) for v in vs])
    return flat.reshape(-1, LANE)


def _unpack(packed, like):
    flat, out, off = packed.reshape(-1), [], 0
    for v in like:
        out.append(flat[off:off + v.size].reshape(v.shape))
        off += v.size
    return out


class MeshWeights:
    def __init__(self, w_kv, w_out, pool_w_in, pool_w_grp, hgrn_w_in, hgrn_norm_g):
        self.n_grp, self.pg_loc, self.pg = pool_w_grp.shape[1:]
        wpin, hnorm, token = all_gather_blocks("gather_first", [pool_w_in[0].astype(COMM_DTYPE), hgrn_norm_g])
        self.first = (wpin, hnorm.reshape(1, -1))
        zero = token[0, 0]
        cast = lambda a: (a + zero).astype(COMM_DTYPE)
        self.gather, self.tokens = {}, []
        for layer, blocks in ((0, [cast(pool_w_grp[0]), cast(w_kv[0]), cast(w_out[0])]),
                              (1, [cast(hgrn_w_in[0])]), (2, [cast(w_kv[1]), cast(w_out[1])])):
            me = _slot(_position())
            lands = [lax.dynamic_update_index_in_dim(lax.empty((N_DEV,) + b.shape, b.dtype), b, me, 0) for b in blocks]
            n = len(blocks)
            sems, bufs, token = split_start(f"gather{layer}_cross", blocks + lands, 4 * n, _gather_cross(n))
            self.gather[layer] = (sems, bufs)
            self.tokens.append(token[0, 0])
        self.reduce = {}

    def first_weights(self):
        return self.first[0], self.first[1], self.tokens[0] + self.tokens[1] + self.tokens[2]

    def _pass_on(self, layer, after):
        sems, bufs = self.gather[layer]
        n = len(bufs) // 2
        gathered = split_wait(f"gather{layer}_cross_wait", sems, bufs, _gather_cross(n), after)[n:]
        sems, bufs, _ = split_start(f"gather{layer}_pass", gathered, 3 * n, _gather_pass(n))
        self.gather[layer] = (sems, bufs)

    def _gathered(self, layer, after):
        sems, bufs = self.gather[layer]
        return split_wait(f"gather{layer}_pass_wait", sems, bufs, _gather_pass(len(bufs)), after)

    def after_proj0(self, after):
        self._pass_on(0, after)

    def layer0_weights(self, after):
        wgrp, wkv, wout = self._gathered(0, after)
        rows = lambda g: g.reshape(-1, g.shape[-1])
        return wgrp.transpose(1, 0, 2, 3).reshape(self.n_grp, self.pg, self.pg), rows(wkv), rows(wout)

    def after_attn0(self, after):
        self._pass_on(1, after)

    def layer1_weights(self, after):
        return self._gathered(1, after)[0]

    def after_proj1(self, after):
        self._pass_on(2, after)

    def layer1_rest(self, after):
        wkv, wout = self._gathered(2, after)
        rows = lambda g: g.reshape(-1, g.shape[-1])
        return rows(wkv), rows(wout)

    def grads_ready(self, layer, grads):
        parts = {}
        for nm, g in grads.items():
            if nm == "pool_w_grp":
                g = g.reshape(self.n_grp, N_DEV, self.pg_loc, self.pg).transpose(1, 0, 2, 3)
            parts[nm] = g.reshape(4, 2, -1, g.shape[-1])
        names, srcs = list(parts), list(parts.values())
        lands = [lax.empty((4,) + p.shape[2:], p.dtype) for p in srcs]
        sems, bufs, token = split_start(f"reduce{layer}_pair", srcs + lands, 4 * len(srcs), _reduce_pair(len(srcs)))
        self.reduce[layer] = (names, sems, bufs)
        return token[0, 0]

    def grads_reduce(self, layer, after):
        names, sems, bufs = self.reduce[layer]
        n = len(names)
        bufs = split_wait(f"reduce{layer}_pair_wait", sems, bufs, _reduce_pair(n), after)
        chip_parts = [pair_add(f"pair_add_{nm}{layer}", bufs[t], bufs[n + t]) for t, nm in enumerate(names)]
        lands = [lax.empty((3,) + p.shape[1:], p.dtype) for p in chip_parts]
        sems, bufs, token = split_start(f"reduce{layer}_cross", chip_parts + lands, 3 * n, _reduce_cross(n))
        self.reduce[layer] = (names, sems, bufs)
        return token[0, 0]

    def grads_done(self, layer, after):
        names, sems, bufs = self.reduce[layer]
        n = len(names)
        bufs = split_wait(f"reduce{layer}_cross_wait", sems, bufs, _reduce_cross(n), after)
        return {nm: (bufs[t], bufs[n + t]) for t, nm in enumerate(names)}


def kernel(x, mem, norm_g, mem_norm_g, w_kv, w_out, pool_w_in, pool_w_grp, pool_scale, hgrn_w_in, hgrn_lb, hgrn_norm_g, final_g, loss_target, m_norm_g, m_mem_norm_g, m_w_kv, m_w_out, m_pool_w_in, m_pool_w_grp, m_pool_scale, m_hgrn_w_in, m_hgrn_lb, m_hgrn_norm_g, m_final_g, v_norm_g, v_mem_norm_g, v_w_kv, v_w_out, v_pool_w_in, v_pool_w_grp, v_pool_scale, v_hgrn_w_in, v_hgrn_lb, v_hgrn_norm_g, v_final_g):
    comm = MeshWeights(w_kv, w_out, pool_w_in, pool_w_grp, hgrn_w_in, hgrn_norm_g)
    loss, grad_x, small = run_step(x[0], mem[0], loss_target[0], norm_g, mem_norm_g, pool_scale, hgrn_lb, final_g, comm)
    loss = lax.psum(loss, AXES)
    rows = lambda a: a.reshape(a.shape[0], -1, a.shape[-1])
    state = dict(w_kv=(w_kv, m_w_kv, v_w_kv), w_out=(w_out, m_w_out, v_w_out), pool_w_in=(pool_w_in, m_pool_w_in, v_pool_w_in),
                 pool_w_grp=(pool_w_grp, m_pool_w_grp, v_pool_w_grp), hgrn_w_in=(hgrn_w_in, m_hgrn_w_in, v_hgrn_w_in))
    big = {}

    def update(group, layer, after):
        for nm, (chip_parts, landed) in comm.grads_done(group, after).items():
            w, m, v = state[nm]
            big[nm] = adamw_shard(f"adamw_{nm}{layer}", chip_parts, landed, rows(w), rows(m), rows(v),
                                  layer if w.shape[0] > 1 else 0, prev=big.get(nm))
            after = big[nm][0]
        return after

    after = update("0a", 0, update(1, 1, grad_x))
    names = ("norm_g", "mem_norm_g", "pool_scale", "hgrn_lb", "hgrn_norm_g", "final_g")
    like = (norm_g, mem_norm_g, pool_scale, hgrn_lb, _sds((1, N_DEV * hgrn_norm_g.shape[1]), F32), final_g)
    packed = _pack([small[nm].reshape(lk.shape) for nm, lk in zip(names, like)])
    reduced = _unpack(all_reduce_small("reduce_small", packed, after), like)
    n_loc = hgrn_norm_g.shape[1]
    reduced[4] = lax.dynamic_slice(reduced[4], (0, _slot(_position()) * n_loc), (1, n_loc))
    ws = (norm_g, mem_norm_g, pool_scale, hgrn_lb, hgrn_norm_g, final_g)
    ms = (m_norm_g, m_mem_norm_g, m_pool_scale, m_hgrn_lb, m_hgrn_norm_g, m_final_g)
    vs = (v_norm_g, v_mem_norm_g, v_pool_scale, v_hgrn_lb, v_hgrn_norm_g, v_final_g)
    res = adamw_small("adamw_small", _pack(reduced), _pack(ws), _pack(ms), _pack(vs))
    outs = {nm: [reduced[i]] + [_unpack(r, ws)[i] for r in res] for i, nm in enumerate(names)}
    update("0b", 0, res[0])
    for nm, (w, _, _) in state.items():
        outs[nm] = [r.reshape(w.shape) for r in big[nm]]
    order = ("norm_g", "mem_norm_g", "w_kv", "w_out", "pool_w_in", "pool_w_grp", "pool_scale", "hgrn_w_in", "hgrn_lb",
             "hgrn_norm_g", "final_g")
    return (loss, grad_x[None], *[outs[nm][0] for nm in order], *[outs[nm][1] for nm in order],
            *[outs[nm][2] for nm in order], *[outs[nm][3] for nm in order])
```

```python
import functools

import jax
import jax.numpy as jnp
from jax import lax
from jax.experimental import pallas as pl
from jax.experimental.pallas import tpu as pltpu

F32, BF16 = jnp.float32, jnp.bfloat16
MM_DTYPE = BF16
COMM_DTYPE = BF16
EPS = 1e-6
POOL_WINDOWS = (2, 4, 8, 16)
POOL_HALO = 16
HG_HEAD_DIM = 128
HG_CHUNK = 64
CA_HEADS = 4
N_DEV = 8
ADAM_LR, ADAM_B1, ADAM_B2, ADAM_EPS, ADAM_WD, ADAM_STEP = 0.001, 0.9, 0.999, 1e-08, 0.01, 10
VMEM_LIMIT = 48 << 20
LANE = 128
MESH = pl.DeviceIdType.MESH
AXES = ("x", "y", "c")

NN = (((1,), (0,)), ((), ()))
NT = (((1,), (1,)), ((), ()))
TN = (((0,), (0,)), ((), ()))


def _cp(sem=None, vmem=VMEM_LIMIT):
    return pltpu.CompilerParams(dimension_semantics=sem, vmem_limit_bytes=vmem)


def _dot(a, b, dims):
    return lax.dot_general(a.astype(MM_DTYPE), b.astype(MM_DTYPE), dims, preferred_element_type=F32)


def _sig(v):
    return 1.0 / (1.0 + jnp.exp(-v))


def _tile(n, pref):
    if n <= pref:
        return n
    t = (pref // LANE) * LANE
    while n % t:
        t -= LANE
    return t


def _sds(shape, dtype):
    return jax.ShapeDtypeStruct(tuple(shape), dtype)


def _mm(name, grid, sem, ins, in_specs, out_shapes, out_specs, dims, epi=None, nk=1, acc_shape=None, aliases=None):
    n_in, n_out = len(ins), len(out_shapes)

    def body(*refs):
        a_ref, b_ref = refs[0], refs[1]
        extra = refs[2:n_in]
        outs = refs[n_in:n_in + n_out]
        p = _dot(a_ref[...], b_ref[...], dims)

        def finish(acc):
            res = epi(acc, *extra) if epi is not None else (acc,)
            for r, o_ref in zip(res, outs):
                o_ref[...] = r.astype(o_ref.dtype)

        if nk == 1:
            finish(p)
        else:
            acc_ref = refs[n_in + n_out]
            k = pl.program_id(len(grid) - 1)

            @pl.when(k == 0)
            def _():
                acc_ref[...] = p

            @pl.when(k > 0)
            def _():
                acc_ref[...] += p

            @pl.when(k == nk - 1)
            def _():
                finish(acc_ref[...])

    scratch = [pltpu.VMEM(acc_shape, F32)] if nk > 1 else []
    return pl.pallas_call(
        body, name=name, grid=grid, in_specs=in_specs, out_specs=out_specs, out_shape=out_shapes,
        scratch_shapes=scratch, compiler_params=_cp(sem), input_output_aliases=aliases or {},
    )(*ins)


def proj_fwd(name, h, wblk, tm=1024):
    T, D = h.shape
    nblk, _, nb = wblk.shape
    tm = _tile(T, tm)
    return _mm(name, (nblk, T // tm), ("parallel", "parallel"), [h, wblk],
               [pl.BlockSpec((tm, D), lambda j, i: (i, 0)), pl.BlockSpec((None, D, nb), lambda j, i: (j, 0, 0))],
               [_sds((T, nblk * nb), F32)], [pl.BlockSpec((tm, nb), lambda j, i: (i, j))], NN)[0]


def proj_bwd_w(name, h, dproj, nblk, tm=1024, tn=1024):
    T, D = h.shape
    N = dproj.shape[1]
    nb = N // nblk
    tm, tn = _tile(D, tm), _tile(nb, tn)
    per = nb // tn
    return _mm(name, (D // tm, N // tn), ("parallel", "parallel"), [h, dproj],
               [pl.BlockSpec((T, tm), lambda i, n: (0, i)), pl.BlockSpec((T, tn), lambda i, n: (0, n))],
               [_sds((nblk, D, nb), COMM_DTYPE)], [pl.BlockSpec((None, tm, tn), lambda i, n: (n // per, i, n % per))], TN)[0]


def proj_bwd_x(name, dproj, wblk, tm=1024):
    T, N = dproj.shape
    nblk, D, nb = wblk.shape
    tm = _tile(T, tm)
    halves = 2 if nb > 1024 and nb % (2 * LANE) == 0 else 1
    tk, nk = nb // halves, nblk * halves
    return _mm(name, (T // tm, nk), ("parallel", "arbitrary"), [dproj, wblk],
               [pl.BlockSpec((tm, tk), lambda i, k: (i, k)),
                pl.BlockSpec((None, D, tk), lambda i, k: (k // halves, 0, k % halves))],
               [_sds((T, D), F32)], [pl.BlockSpec((tm, D), lambda i, k: (i, 0))], NT, nk=nk, acc_shape=(tm, D))[0]


def out_fwd(name, branch, wout, x, tm=512, tn=1024):
    T, E = branch.shape
    D = wout.shape[1]
    tm, tn = _tile(T, tm), _tile(D, tn)
    return _mm(name, (T // tm, D // tn), ("parallel", "parallel"), [branch, wout, x],
               [pl.BlockSpec((tm, E), lambda i, j: (i, 0)), pl.BlockSpec((E, tn), lambda i, j: (0, j)),
                pl.BlockSpec((tm, tn), lambda i, j: (i, j))],
               [_sds((T, D), F32)], [pl.BlockSpec((tm, tn), lambda i, j: (i, j))], NN,
               epi=lambda acc, x_ref: (acc + x_ref[...],))[0]


def mm_nt(name, a, b, out_dtype, tm=1024, tn=1024, res=None):
    M, K = a.shape
    N = b.shape[0]
    tm, tn = _tile(M, tm), _tile(N, tn)
    ins = [a, b] + ([res] if res is not None else [])
    specs = [pl.BlockSpec((tm, K), lambda i, j: (i, 0)), pl.BlockSpec((tn, K), lambda i, j: (j, 0))]
    if res is not None:
        specs.append(pl.BlockSpec((tm, tn), lambda i, j: (i, j)))
    epi = (lambda acc, r_ref: (acc + r_ref[...],)) if res is not None else None
    return _mm(name, (M // tm, N // tn), ("parallel", "parallel"), ins, specs,
               [_sds((M, N), out_dtype)], [pl.BlockSpec((tm, tn), lambda i, j: (i, j))], NT, epi=epi)[0]


def mm_nn(name, a, b, out_dtype, tm=512, tn=512):
    M, K = a.shape
    N = b.shape[1]
    tm, tn = _tile(M, tm), _tile(N, tn)
    return _mm(name, (M // tm, N // tn), ("parallel", "parallel"), [a, b],
               [pl.BlockSpec((tm, K), lambda i, j: (i, 0)), pl.BlockSpec((K, tn), lambda i, j: (0, j))],
               [_sds((M, N), out_dtype)], [pl.BlockSpec((tm, tn), lambda i, j: (i, j))], NN)[0]


def mm_tn(name, a, b, out_dtype, tm=512, tn=1024):
    K, M = a.shape
    N = b.shape[1]
    tm, tn = _tile(M, tm), _tile(N, tn)
    return _mm(name, (M // tm, N // tn), ("parallel", "parallel"), [a, b],
               [pl.BlockSpec((K, tm), lambda i, j: (0, i)), pl.BlockSpec((K, tn), lambda i, j: (0, j))],
               [_sds((M, N), out_dtype)], [pl.BlockSpec((tm, tn), lambda i, j: (i, j))], TN)[0]


def rmsnorm_fwd(name, x, g, tm=512):
    T, D = x.shape
    tm = _tile(T, tm)

    def body(x_ref, g_ref, o_ref):
        xf = x_ref[...]
        r = lax.rsqrt(jnp.mean(xf * xf, axis=-1, keepdims=True) + EPS)
        o_ref[...] = ((xf * r) * g_ref[...]).astype(o_ref.dtype)

    return pl.pallas_call(
        body, name=name, grid=(T // tm,),
        in_specs=[pl.BlockSpec((tm, D), lambda i: (i, 0)), pl.BlockSpec((1, D), lambda i: (0, 0))],
        out_specs=pl.BlockSpec((tm, D), lambda i: (i, 0)), out_shape=_sds((T, D), MM_DTYPE),
        compiler_params=_cp(("parallel",)),
    )(x, g.reshape(1, D))


def _rms_bwd_math(xf, g, dh):
    r = lax.rsqrt(jnp.mean(xf * xf, axis=-1, keepdims=True) + EPS)
    gd = dh * g
    dx = r * gd - xf * ((r * r * r) * jnp.mean(gd * xf, axis=-1, keepdims=True))
    dg = jnp.sum(dh * (xf * r), axis=0, keepdims=True)
    return dx, dg


def rmsnorm_bwd(name, x, g, dh, res=None, tm=256):
    T, D = x.shape
    tm = _tile(T, tm)
    has_res = res is not None

    def body(*refs):
        x_ref, g_ref, dh_ref = refs[:3]
        dx_ref, dxb_ref, dg_ref = refs[3 + has_res:]
        dx, dg = _rms_bwd_math(x_ref[...], g_ref[...], dh_ref[...])
        if has_res:
            dx = dx + refs[3][...]
        dx_ref[...] = dx
        dxb_ref[...] = dx.astype(dxb_ref.dtype)

        @pl.when(pl.program_id(0) == 0)
        def _():
            dg_ref[...] = jnp.zeros_like(dg_ref)

        dg_ref[...] += dg

    row = pl.BlockSpec((tm, D), lambda i: (i, 0))
    vec = pl.BlockSpec((1, D), lambda i: (0, 0))
    ins = [x, g.reshape(1, D), dh] + ([res] if has_res else [])
    return pl.pallas_call(
        body, name=name, grid=(T // tm,), in_specs=[row, vec, row] + ([row] if has_res else []),
        out_specs=[row, row, vec], out_shape=[_sds((T, D), F32), _sds((T, D), MM_DTYPE), _sds((1, D), F32)],
        compiler_params=_cp(("arbitrary",)),
    )(*ins)


def final_loss_bwd(name, x, g, target, tm=256):
    T, D = x.shape
    tm = _tile(T, tm)

    def body(x_ref, g_ref, t_ref, loss_ref, dx_ref, dxb_ref, dg_ref):
        xf, gv = x_ref[...], g_ref[...]
        r = lax.rsqrt(jnp.mean(xf * xf, axis=-1, keepdims=True) + EPS)
        err = (xf * r) * gv - t_ref[...]
        part = 0.5 * jnp.sum(jnp.mean(err * err, axis=-1, keepdims=True), axis=0, keepdims=True)
        dx, dg = _rms_bwd_math(xf, gv, err / D)
        dx_ref[...] = dx
        dxb_ref[...] = dx.astype(dxb_ref.dtype)

        @pl.when(pl.program_id(0) == 0)
        def _():
            dg_ref[...] = jnp.zeros_like(dg_ref)
            loss_ref[...] = jnp.zeros_like(loss_ref)

        dg_ref[...] += dg
        loss_ref[...] += jnp.broadcast_to(part, loss_ref.shape)

    row = pl.BlockSpec((tm, D), lambda i: (i, 0))
    vec = pl.BlockSpec((1, D), lambda i: (0, 0))
    return pl.pallas_call(
        body, name=name, grid=(T // tm,), in_specs=[row, vec, row],
        out_specs=[pl.BlockSpec((8, LANE), lambda i: (0, 0)), row, row, vec],
        out_shape=[_sds((8, LANE), F32), _sds((T, D), F32), _sds((T, D), MM_DTYPE), _sds((1, D), F32)],
        compiler_params=_cp(("arbitrary",)),
    )(x, g.reshape(1, D), target)


def _pool_tiles(T, emix):
    pg = emix // len(POOL_WINDOWS)
    tc = 256 if pg % 256 == 0 else LANE
    return pg, tc, _tile(T, 512)


def pool_fwd(name, proj, T, emix):
    pg, tc, R = _pool_tiles(T, emix)
    per_g = pg // tc

    def body(u_ref, o_ref):
        g = pl.program_id(0) // per_g
        for gi, w in enumerate(POOL_WINDOWS):
            @pl.when(g == gi)
            def _():
                for ci in range(T // R):
                    r0 = ci * R
                    cur = u_ref[r0:r0 + R, :]
                    halo = jnp.zeros((POOL_HALO, tc), F32) if ci == 0 else u_ref[r0 - POOL_HALO:r0, :]
                    s = jnp.concatenate([halo, cur], axis=0)
                    for st in range(w.bit_length() - 1):
                        s = s + pltpu.roll(s, 1 << st, axis=0)
                    t = r0 + lax.broadcasted_iota(jnp.int32, (R, 1), 0)
                    cnt = jnp.minimum(t + 1, w).astype(F32)
                    o_ref[r0:r0 + R, :] = (s[POOL_HALO:] / cnt - cur).astype(o_ref.dtype)

    return pl.pallas_call(
        body, name=name, grid=(emix // tc,), in_specs=[pl.BlockSpec((T, tc), lambda j: (0, j))],
        out_specs=pl.BlockSpec((T, tc), lambda j: (0, j)), out_shape=_sds((T, emix), MM_DTYPE),
        compiler_params=_cp(("parallel",)),
    )(proj)


def pool_bwd(name, dpooled, dproj, T, emix):
    pg, tc, R = _pool_tiles(T, emix)
    per_g = pg // tc

    def body(d_ref, _, o_ref):
        g = pl.program_id(0) // per_g
        for gi, w in enumerate(POOL_WINDOWS):
            @pl.when(g == gi)
            def _():
                n = R + POOL_HALO
                for ci in range(T // R):
                    r0 = ci * R
                    cur = d_ref[r0:r0 + R, :]
                    halo = jnp.zeros((POOL_HALO, tc), F32) if ci == T // R - 1 else d_ref[r0 + R:r0 + n, :]
                    t = r0 + lax.broadcasted_iota(jnp.int32, (n, 1), 0)
                    cnt = jnp.minimum(t + 1, w).astype(F32)
                    s = jnp.concatenate([cur, halo], axis=0) / cnt
                    for st in range(w.bit_length() - 1):
                        s = s + pltpu.roll(s, n - (1 << st), axis=0)
                    o_ref[r0:r0 + R, :] = (s[:R] - cur).astype(o_ref.dtype)

    return pl.pallas_call(
        body, name=name, grid=(emix // tc,),
        in_specs=[pl.BlockSpec((T, tc), lambda j: (0, j)), pl.BlockSpec(memory_space=pl.ANY)],
        out_specs=pl.BlockSpec((T, tc), lambda j: (0, j)), out_shape=_sds(dproj.shape, dproj.dtype),
        input_output_aliases={1: 0}, compiler_params=_cp(("parallel",)),
    )(dpooled, dproj)


def grp_fwd(name, pooled, wgrp, scale, proj, gate_off, E):
    T, emix = pooled.shape
    pg = emix // len(POOL_WINDOWS)
    tm = _tile(T, 2048)
    tc = 256 if pg % 256 == 0 and gate_off % 256 == 0 else LANE
    per = pg // tc

    def epi(acc, s_ref, g_ref):
        gt = g_ref[...]
        return acc, (acc * s_ref[...]) * (gt * _sig(gt))

    col = pl.BlockSpec((tm, tc), lambda n, i: (i, n))
    return _mm(name, (emix // tc, T // tm), ("parallel", "parallel"), [pooled, wgrp, scale, proj],
               [pl.BlockSpec((tm, pg), lambda n, i: (i, n // per)), pl.BlockSpec((None, pg, tc), lambda n, i: (n // per, 0, n % per)),
                pl.BlockSpec((1, tc), lambda n, i: (0, n)), pl.BlockSpec((tm, tc), lambda n, i: (i, gate_off // tc + n))],
               [_sds((T, emix), F32), _sds((T, E), MM_DTYPE)], [col, col], NN, epi=epi)


def grp_bwd_x(name, dy, wgrp):
    T, emix = dy.shape
    pg = emix // len(POOL_WINDOWS)
    tm = _tile(T, 1024)
    return _mm(name, (len(POOL_WINDOWS), T // tm), ("parallel", "parallel"), [dy, wgrp],
               [pl.BlockSpec((tm, pg), lambda g, i: (i, g)), pl.BlockSpec((None, pg, pg), lambda g, i: (g, 0, 0))],
               [_sds((T, emix), F32)], [pl.BlockSpec((tm, pg), lambda g, i: (i, g))], NT)[0]


def grp_bwd_w(name, pooled, dy):
    T, emix = dy.shape
    ng = len(POOL_WINDOWS)
    pg = emix // ng
    col = pl.BlockSpec((T, pg), lambda g: (0, g))
    return _mm(name, (ng,), ("parallel",), [pooled, dy], [col, col],
               [_sds((ng, pg, pg), COMM_DTYPE)], [pl.BlockSpec((None, pg, pg), lambda g: (g, 0, 0))], TN)[0]


def gate_bwd_pool(name, dbranch, y, scale, proj, gate_off, n_proj):
    T, emix = y.shape
    tm, tc = _tile(T, 1024), _tile(emix, 512)
    assert gate_off % tc == 0

    def body(db_ref, y_ref, s_ref, g_ref, dy_ref, dg_ref, ds_ref):
        db, yv, sc, gt = db_ref[...], y_ref[...], s_ref[...], g_ref[...]
        sg = _sig(gt)
        dmix = db * (gt * sg)
        dy_ref[...] = (dmix * sc).astype(dy_ref.dtype)
        dg_ref[...] = (db * (yv * sc) * (sg * (1.0 + gt * (1.0 - sg)))).astype(dg_ref.dtype)

        @pl.when(pl.program_id(1) == 0)
        def _():
            ds_ref[...] = jnp.zeros_like(ds_ref)

        ds_ref[...] += jnp.sum(dmix * yv, axis=0, keepdims=True)

    blk = pl.BlockSpec((tm, tc), lambda j, i: (i, j))
    vec = pl.BlockSpec((1, tc), lambda j, i: (0, j))
    gate = pl.BlockSpec((tm, tc), lambda j, i: (i, gate_off // tc + j))
    return pl.pallas_call(
        body, name=name, grid=(emix // tc, T // tm), in_specs=[blk, blk, vec, gate],
        out_specs=[blk, gate, vec],
        out_shape=[_sds((T, emix), MM_DTYPE), _sds((T, n_proj), MM_DTYPE), _sds((1, emix), F32)],
        compiler_params=_cp(("parallel", "arbitrary")),
    )(dbranch, y, scale, proj)


def attn_fwd(name, proj, kv, branch, q_off, gate_off, tm=1024):
    T = proj.shape[0]
    M, two_eca = kv.shape
    eca = two_eca // 2
    hd = eca // CA_HEADS
    E = branch.shape[1]
    tm = _tile(T, tm)
    scale = hd ** -0.5

    def body(q_ref, g_ref, k_ref, v_ref, _, ca_ref, br_ref):
        s = _dot(q_ref[...], k_ref[...], NT) * scale
        s = s - jnp.max(s, axis=-1, keepdims=True)
        e = jnp.exp(s)
        p = e / jnp.sum(e, axis=-1, keepdims=True)
        o = _dot(p, v_ref[...], NN)
        gt = g_ref[...]
        ca_ref[...] = o
        br_ref[...] = (o * (gt * _sig(gt))).astype(br_ref.dtype)

    return pl.pallas_call(
        body, name=name, grid=(CA_HEADS, T // tm),
        in_specs=[pl.BlockSpec((tm, hd), lambda h, i: (i, q_off // hd + h)),
                  pl.BlockSpec((tm, hd), lambda h, i: (i, (gate_off + E - eca) // hd + h)),
                  pl.BlockSpec((M, hd), lambda h, i: (0, h)), pl.BlockSpec((M, hd), lambda h, i: (0, CA_HEADS + h)),
                  pl.BlockSpec(memory_space=pl.ANY)],
        out_specs=[pl.BlockSpec((tm, hd), lambda h, i: (i, h)), pl.BlockSpec((tm, hd), lambda h, i: (i, (E - eca) // hd + h))],
        out_shape=[_sds((T, eca), F32), _sds(branch.shape, branch.dtype)],
        input_output_aliases={4: 1}, compiler_params=_cp(("parallel", "parallel")),
    )(proj, proj, kv, kv, branch)


def attn_bwd(name, proj, kv, ca, dbranch, dproj, q_off, gate_off, tm=1024):
    T = proj.shape[0]
    M, two_eca = kv.shape
    eca = two_eca // 2
    hd = eca // CA_HEADS
    E = dbranch.shape[1]
    tm = _tile(T, tm)
    scale = hd ** -0.5
    q_blk, g_blk = q_off // hd, (gate_off + E - eca) // hd

    def body(q_ref, g_ref, k_ref, v_ref, ca_ref, db_ref, _, dp_ref, dk_ref, dv_ref, dgate_buf):
        first_rows, part = pl.program_id(1) == 0, pl.program_id(2)

        @pl.when(part == 0)
        def _():
            q, k, v = q_ref[...], k_ref[...], v_ref[...]
            s = _dot(q, k, NT) * scale
            s = s - jnp.max(s, axis=-1, keepdims=True)
            e = jnp.exp(s)
            p = e / jnp.sum(e, axis=-1, keepdims=True)
            gt, db = g_ref[...], db_ref[...]
            sg = _sig(gt)
            do = db * (gt * sg)
            dgate_buf[...] = (db * ca_ref[...] * (sg * (1.0 + gt * (1.0 - sg)))).astype(dgate_buf.dtype)
            dpr = _dot(do, v, NT)
            ds = p * (dpr - jnp.sum(dpr * p, axis=-1, keepdims=True)) * scale
            dp_ref[...] = _dot(ds, k, NN).astype(dp_ref.dtype)

            @pl.when(first_rows)
            def _():
                dk_ref[...] = jnp.zeros_like(dk_ref)
                dv_ref[...] = jnp.zeros_like(dv_ref)

            dk_ref[...] += _dot(ds, q, TN)
            dv_ref[...] += _dot(p, do, TN)

        @pl.when(part == 1)
        def _():
            dp_ref[...] = dgate_buf[...]

    row = lambda blk: pl.BlockSpec((tm, hd), lambda h, i, p: (i, blk + h))
    out_dp = pl.BlockSpec((tm, hd), lambda h, i, p: (i, jnp.where(p == 0, q_blk, g_blk) + h))
    outs = pl.pallas_call(
        body, name=name, grid=(CA_HEADS, T // tm, 2),
        in_specs=[row(q_blk), row(g_blk), pl.BlockSpec((M, hd), lambda h, i, p: (0, h)),
                  pl.BlockSpec((M, hd), lambda h, i, p: (0, CA_HEADS + h)), row(0), row((E - eca) // hd),
                  pl.BlockSpec(memory_space=pl.ANY)],
        out_specs=[out_dp, pl.BlockSpec((M, hd), lambda h, i, p: (0, h)), pl.BlockSpec((M, hd), lambda h, i, p: (0, h))],
        out_shape=[_sds(dproj.shape, dproj.dtype), _sds((M, eca), F32), _sds((M, eca), F32)],
        scratch_shapes=[pltpu.VMEM((tm, hd), dproj.dtype)],
        input_output_aliases={6: 0}, compiler_params=_cp(("parallel", "arbitrary", "arbitrary")),
    )(proj, proj, kv, kv, ca, dbranch, dproj)
    return outs[0], jnp.concatenate([outs[1], outs[2]], axis=1)


def _split3(v):
    hi = v.astype(BF16)
    r1 = v - hi.astype(F32)
    mid = r1.astype(BF16)
    lo = (r1 - mid.astype(F32)).astype(BF16)
    return hi, mid, lo


def _tri_dot(tri, v):
    hi, mid, lo = _split3(v)
    d = functools.partial(lax.dot_general, dimension_numbers=NN, preferred_element_type=F32)
    return d(tri, hi) + d(tri, mid) + d(tri, lo)


def _causal_mask():
    C = HG_CHUNK
    return lax.broadcasted_iota(jnp.int32, (C, C), 0) >= lax.broadcasted_iota(jnp.int32, (C, C), 1)


def _cumsum_matrices(G):
    ri = lax.broadcasted_iota(jnp.int32, (G, G), 0)
    ci = lax.broadcasted_iota(jnp.int32, (G, G), 1)
    same = (ri // HG_CHUNK) == (ci // HG_CHUNK)
    return (jnp.where(same & (ri >= ci), 1.0, 0.0).astype(BF16), jnp.where(same & (ri <= ci), 1.0, 0.0).astype(BF16))


def _forget_gate(fi, lb):
    sig = _sig(fi)
    return sig, lb + (1.0 - lb) * sig


def _log_decay_cumsum(f_ref, b_s, lb, prefix, G):
    def group(i, carry):
        rows = pl.ds(pl.multiple_of(i * G, G), G)
        _, f = _forget_gate(f_ref[rows, :], lb)
        b_s[rows, :] = _tri_dot(prefix, jnp.log(f))
        return carry

    lax.fori_loop(0, b_s.shape[0] // G, group, 0)


def _lower_bound(lb2):
    l0, l1 = lb2[0:1, :], lb2[1:2, :]
    m = jnp.maximum(l0, l1)
    e0, e1 = jnp.exp(l0 - m), jnp.exp(l1 - m)
    sm0, sm1 = e0 / (e0 + e1), e1 / (e0 + e1)
    return (sm0 + sm1) - sm0, sm0, sm1


def _hg_chunk_fwd(qi, fi, lb, causal, b=None, prefix=None):
    sig, f = _forget_gate(fi, lb)
    k = 1.0 - f
    if b is None:
        b = _tri_dot(prefix, jnp.log(f))
    last = lax.broadcasted_iota(jnp.int32, b.shape, 0) == HG_CHUNK - 1
    bl = jnp.sum(jnp.where(last, b, 0.0), axis=0, keepdims=True)
    eb, enb, ete = jnp.exp(b), jnp.exp(-b), jnp.exp(bl - b)
    sq = _sig(qi)
    q = (qi * sq) * (HG_HEAD_DIM ** -0.5)
    q_dec, k_inv, k_te = q * eb, k * enb, k * ete
    att = jnp.where(causal, _dot(q_dec, k_inv, NT), 0.0)
    return dict(sig=sig, f=f, k=k, eb=eb, enb=enb, ete=ete, dcy=jnp.exp(bl), sq=sq, q_dec=q_dec, k_inv=k_inv, k_te=k_te,
                att=att, last=last)


def _hg_blocking(H, T):
    hb = 4 if H % 4 == 0 else (2 if H % 2 == 0 else 1)
    return hb, _tile(T, 1024)


def hgrn_fwd(name, proj, lb2, norm_g, emix, gate_off, E):
    T = proj.shape[0]
    hd, C = HG_HEAD_DIM, HG_CHUNK
    H, N = emix // hd, T // C
    hb, ts = _hg_blocking(H, T)
    W, nc, G = hb * hd, ts // C, _tile(ts, 256)

    def body(q_ref, f_ref, i_ref, g_ref, lb_ref, ng_ref, o_ref, st_ref, br_ref, state, b_s):
        @pl.when(pl.program_id(1) == 0)
        def _():
            state[...] = jnp.zeros_like(state)

        causal = _causal_mask()
        lb, _, _ = _lower_bound(lb_ref[...])
        ng = ng_ref[...]
        _log_decay_cumsum(f_ref, b_s, lb, _cumsum_matrices(G)[0], G)

        def chunk(n, carry):
            rows = pl.ds(pl.multiple_of(n * C, C), C)
            for h in range(hb):
                cols = slice(h * hd, (h + 1) * hd)
                v = i_ref[rows, cols]
                c = _hg_chunk_fwd(q_ref[rows, cols], f_ref[rows, cols], lb[:, cols], causal, b=b_s[rows, cols])
                st = state[h]
                o = _dot(c["att"], v, NN) + _dot(c["q_dec"], st, NT)
                st_ref[h, n] = st
                o_ref[rows, cols] = o
                r = lax.rsqrt(jnp.mean(o * o, axis=-1, keepdims=True) + EPS)
                gt = g_ref[rows, cols]
                br_ref[rows, cols] = (((o * r) * ng[:, cols]) * (gt * _sig(gt))).astype(br_ref.dtype)
                state[h] = st * c["dcy"] + _dot(v, c["k_te"], TN)
            return carry

        lax.fori_loop(0, nc, chunk, 0, unroll=4)

    col = lambda off: pl.BlockSpec((ts, W), lambda h, s: (s, off // W + h))
    vec = lambda r: pl.BlockSpec((r, W), lambda h, s: (0, h))
    return pl.pallas_call(
        body, name=name, grid=(H // hb, T // ts),
        in_specs=[col(0), col(emix), col(2 * emix), col(gate_off), vec(2), vec(1)],
        out_specs=[col(0), pl.BlockSpec((hb, nc, hd, hd), lambda h, s: (h, s, 0, 0)), col(0)],
        out_shape=[_sds((T, emix), F32), _sds((H, N, hd, hd), F32), _sds((T, E), MM_DTYPE)],
        scratch_shapes=[pltpu.VMEM((hb, hd, hd), F32), pltpu.VMEM((ts, W), F32)],
        compiler_params=_cp(("parallel", "arbitrary")),
    )(proj, proj, proj, proj, lb2, norm_g)


def hgrn_bwd(name, proj, lb2, norm_g, o_pre, states, dbranch, emix, gate_off):
    T, n_proj = proj.shape
    hd, C = HG_HEAD_DIM, HG_CHUNK
    H, N = emix // hd, T // C
    hb, ts = _hg_blocking(H, T)
    W, nc, ns = hb * hd, ts // C, T // ts
    qscale = HG_HEAD_DIM ** -0.5

    def body(q_ref, f_ref, i_ref, g_ref, lb_ref, ng_ref, o_ref, st_ref, db_ref, dp_ref, dlb_ref, dng_ref, stash, dst_ref, acc):
        first_block, part = pl.program_id(1) == 0, pl.program_id(2)

        @pl.when(part == 0)
        def _():
            @pl.when(first_block)
            def _():
                dst_ref[...] = jnp.zeros_like(dst_ref)
                acc[...] = jnp.zeros_like(acc)

            causal = _causal_mask()
            prefix, suffix = _cumsum_matrices(C)
            lb, sm0, sm1 = _lower_bound(lb_ref[...])
            ng = ng_ref[...]

            def chunk(j, carry):
                n = nc - 1 - j
                rows = pl.ds(pl.multiple_of(n * C, C), C)
                for h in range(hb):
                    cols = slice(h * hd, (h + 1) * hd)
                    dst = dst_ref[h]
                    qi, v, gt = q_ref[rows, cols], i_ref[rows, cols], g_ref[rows, cols]
                    c = _hg_chunk_fwd(qi, f_ref[rows, cols], lb[:, cols], causal, prefix=prefix)
                    st = st_ref[h, n]
                    o, db, ngh = o_ref[rows, cols], db_ref[rows, cols], ng[:, cols]
                    r = lax.rsqrt(jnp.mean(o * o, axis=-1, keepdims=True) + EPS)
                    sg = _sig(gt)
                    dmix = db * (gt * sg)
                    dgate = db * ((o * r) * ngh) * (sg * (1.0 + gt * (1.0 - sg)))
                    dyn = dmix * ngh
                    do = r * dyn - o * ((r * r * r) * jnp.mean(dyn * o, axis=-1, keepdims=True))
                    acc[1:2, cols] += jnp.sum(dmix * (o * r), axis=0, keepdims=True)
                    datt = jnp.where(causal, _dot(do, v, NT), 0.0)
                    dq_dec = _dot(datt, c["k_inv"], NN) + _dot(do, st, NN)
                    dk_inv = _dot(datt, c["q_dec"], TN)
                    dv = _dot(c["att"], do, TN) + _dot(c["k_te"], dst, NT)
                    dk_te = _dot(v, dst, NN)
                    ddcy = jnp.sum(dst * st, axis=0, keepdims=True)
                    dst_ref[h] = _dot(do, c["q_dec"], TN) + dst * c["dcy"]
                    t_te = dk_te * c["k_te"]
                    dbv = dq_dec * c["q_dec"] - dk_inv * c["k_inv"] - t_te
                    dbl = jnp.sum(t_te, axis=0, keepdims=True) + ddcy * c["dcy"]
                    dbv = dbv + jnp.where(c["last"], dbl, 0.0)
                    dlf = _tri_dot(suffix, dbv)
                    dk = dk_inv * c["enb"] + dk_te * c["ete"]
                    df = dlf / c["f"] - dk
                    sig, sq = c["sig"], c["sq"]
                    acc[0:1, cols] += jnp.sum(df * (1.0 - sig), axis=0, keepdims=True)
                    dqi = (dq_dec * c["eb"]) * qscale * (sq * (1.0 + qi * (1.0 - sq)))
                    dp_ref[rows, cols] = dqi.astype(dp_ref.dtype)
                    stash[0, rows, cols] = (df * (1.0 - lb[:, cols]) * (sig * (1.0 - sig))).astype(stash.dtype)
                    stash[1, rows, cols] = dv.astype(stash.dtype)
                    stash[2, rows, cols] = dgate.astype(stash.dtype)
                return carry

            lax.fori_loop(0, nc, chunk, 0, unroll=4)
            t = acc[0:1, :] * (sm0 * sm1)
            dlb_ref[...] = jnp.concatenate([-t, t], axis=0)
            dng_ref[...] = acc[1:2, :]

        for pi in range(3):
            @pl.when(part == pi + 1)
            def _():
                dp_ref[...] = stash[pi]

    col = lambda off: pl.BlockSpec((ts, W), lambda h, s, p: (ns - 1 - s, off // W + h))
    vec = lambda r: pl.BlockSpec((r, W), lambda h, s, p: (0, h))
    n_mix = emix // W
    out_col = pl.BlockSpec((ts, W), lambda h, s, p: (ns - 1 - s, jnp.where(p < 3, p * n_mix, gate_off // W) + h))
    return pl.pallas_call(
        body, name=name, grid=(H // hb, ns, 4),
        in_specs=[col(0), col(emix), col(2 * emix), col(gate_off), vec(2), vec(1), col(0),
                  pl.BlockSpec((hb, nc, hd, hd), lambda h, s, p: (h, ns - 1 - s, 0, 0)), col(0)],
        out_specs=[out_col, vec(2), vec(1)],
        out_shape=[_sds((T, n_proj), MM_DTYPE), _sds((2, emix), F32), _sds((1, emix), F32)],
        scratch_shapes=[pltpu.VMEM((3, ts, W), MM_DTYPE), pltpu.VMEM((hb, hd, hd), F32), pltpu.VMEM((8, W), F32)],
        compiler_params=_cp(("parallel", "arbitrary", "arbitrary")),
    )(proj, proj, proj, proj, lb2, norm_g, o_pre, states, dbranch)


class WholeWeights:
    def __init__(self, wkv, wout, wpin, wgrp, whin, hgrn_norm_g):
        self.w = dict(wkv=wkv, wout=wout, wpin=wpin, wgrp=wgrp, whin=whin, hnorm=hgrn_norm_g)
        self.grads = {}

    def first_weights(self):
        return self.w["wpin"], self.w["hnorm"], 0.0

    def after_proj0(self, after):
        pass

    def layer0_weights(self, after):
        return self.w["wgrp"], self.w["wkv"][0], self.w["wout"][0]

    def after_attn0(self, after):
        pass

    def layer1_weights(self, after):
        return self.w["whin"]

    def after_proj1(self, after):
        pass

    def layer1_rest(self, after):
        return self.w["wkv"][1], self.w["wout"][1]

    def grads_ready(self, layer, grads):
        self.grads[layer] = grads
        return 0.0

    def grads_reduce(self, layer, after):
        return 0.0


def run_step(x, mem, target, norm_g, mem_norm_g, pool_scale, hgrn_lb, final_g, comm):
    T, D = x.shape
    mem_n = rmsnorm_fwd("mem_norm", mem, mem_norm_g)
    wpin, hgrn_norm_g, zero = comm.first_weights()
    h0 = rmsnorm_fwd("norm0", x, norm_g[0] + zero)
    proj0 = proj_fwd("proj0", h0, wpin)
    comm.after_proj0(proj0)
    n0 = proj0.shape[1]
    E = n0 // 2
    eca = E // 4
    emix = E - eca
    q0, g0 = emix, emix + eca
    q1, g1 = 3 * emix, 3 * emix + eca
    pooled = pool_fwd("pool_fwd", proj0, T, emix)
    wgrp, wkv0, wout0 = comm.layer0_weights(pooled)
    y0, branch0 = grp_fwd("grp_fwd", pooled, wgrp, pool_scale, proj0, g0, E)
    kv0 = mm_nn("kv0", mem_n, wkv0, MM_DTYPE)
    ca0, branch0 = attn_fwd("attn_fwd0", proj0, kv0, branch0, q0, g0)
    comm.after_attn0(ca0)
    x1 = out_fwd("out0", branch0, wout0, x)
    h1 = rmsnorm_fwd("norm1", x1, norm_g[1])
    whin = comm.layer1_weights(h1)
    proj1 = proj_fwd("proj1", h1, whin)
    comm.after_proj1(proj1)
    o1, states, branch1 = hgrn_fwd("hgrn_fwd", proj1, hgrn_lb, hgrn_norm_g, emix, g1, E)
    wkv1, wout1 = comm.layer1_rest(o1)
    kv1 = mm_nn("kv1", mem_n, wkv1, MM_DTYPE)
    ca1, branch1 = attn_fwd("attn_fwd1", proj1, kv1, branch1, q1, g1)
    x2 = out_fwd("out1", branch1, wout1, x1)
    loss, dx2, dx2b, g_final = final_loss_bwd("final", x2, final_g, target)
    g_wout1 = mm_tn("gwout1", branch1, dx2b, COMM_DTYPE)
    dbranch1 = mm_nt("dbranch1", dx2b, wout1, F32)
    dproj1, g_lb, g_hnorm = hgrn_bwd("hgrn_bwd", proj1, hgrn_lb, hgrn_norm_g, o1, states, dbranch1, emix, g1)
    dproj1, dkv1 = attn_bwd("attn_bwd1", proj1, kv1, ca1, dbranch1, dproj1, q1, g1)
    g_wkv1 = mm_tn("gwkv1", mem_n, dkv1, COMM_DTYPE)
    dmem_n = mm_nt("dmem1", dkv1, wkv1, F32)
    g_whin = proj_bwd_w("gwhin", h1, dproj1, whin.shape[0])
    zero = comm.grads_ready(1, dict(w_kv=g_wkv1, w_out=g_wout1, hgrn_w_in=g_whin))
    dh1 = proj_bwd_x("dh1", dproj1, whin)
    zero = zero + comm.grads_reduce(1, dh1)
    dx1, dx1b, g_norm1 = rmsnorm_bwd("norm1_bwd", x1, norm_g[1] + zero, dh1, res=dx2)
    g_wout0 = mm_tn("gwout0", branch0, dx1b, COMM_DTYPE)
    comm.grads_ready("0a", dict(w_out=g_wout0))
    dbranch0 = mm_nt("dbranch0", dx1b, wout0, F32)
    zero = comm.grads_reduce("0a", dbranch0)
    dy0, dproj0, g_scale = gate_bwd_pool("gate_bwd0", dbranch0, y0, pool_scale + zero, proj0, g0, n0)
    g_wgrp = grp_bwd_w("gwgrp", pooled, dy0)
    dpooled = grp_bwd_x("dpooled", dy0, wgrp)
    dproj0 = pool_bwd("pool_bwd", dpooled, dproj0, T, emix)
    dproj0, dkv0 = attn_bwd("attn_bwd0", proj0, kv0, ca0, dbranch0, dproj0, q0, g0)
    g_wkv0 = mm_tn("gwkv0", mem_n, dkv0, COMM_DTYPE)
    dmem_n = mm_nt("dmem0", dkv0, wkv0, F32, res=dmem_n)
    g_wpin = proj_bwd_w("gwpin", h0, dproj0, wpin.shape[0])
    zero = comm.grads_ready("0b", dict(w_kv=g_wkv0, pool_w_in=g_wpin, pool_w_grp=g_wgrp))
    dh0 = proj_bwd_x("dh0", dproj0, wpin)
    zero = zero + comm.grads_reduce("0b", dh0)
    grad_x, _, g_norm0 = rmsnorm_bwd("norm0_bwd", x, norm_g[0] + zero, dh0, res=dx1)
    _, _, g_mem = rmsnorm_bwd("mem_norm_bwd", mem, mem_norm_g, dmem_n)

    small = dict(norm_g=jnp.concatenate([g_norm0, g_norm1], axis=0), mem_norm_g=g_mem[0], pool_scale=g_scale,
                 hgrn_lb=g_lb, hgrn_norm_g=g_hnorm, final_g=g_final[0])
    return loss[0, 0], grad_x, small


def _position():
    return lax.axis_index("x"), lax.axis_index("y"), lax.axis_index("c")


def _slot(p):
    return 4 * p[0] + 2 * p[1] + p[2]


def all_gather_blocks(name, blocks):
    n = len(blocks)

    def body(*refs):
        ins, outs, token = refs[:n], refs[n:2 * n], refs[2 * n]
        send_sems, recv_sems, local_sems = refs[2 * n + 1:]
        token[...] = jnp.zeros_like(token)
        x, y, c = _position()
        me, sibling = (x, y, c), (x, y, 1 - c)
        chips = _other_chips(x, y)

        def copy(t, k, block, to, src=None):
            dst = outs[t].at[_slot(block)]
            return pltpu.make_async_remote_copy(src_ref=dst if src is None else src, dst_ref=dst, send_sem=send_sems.at[t, k],
                                                recv_sem=recv_sems.at[t, k], device_id=to, device_id_type=MESH)

        mine = [pltpu.make_async_copy(ins[t], outs[t].at[_slot(me)], local_sems.at[t]) for t in range(n)]
        for cp in mine:
            cp.start()
        sends = []
        for t in range(n):
            sends.append(copy(t, 0, me, sibling, src=ins[t]))
            sends += [copy(t, 1 + j, me, (*chip, c), src=ins[t]) for j, chip in enumerate(chips)]
        for cp in sends:
            cp.start()
        for t in range(n):
            for j, chip in enumerate(chips):
                copy(t, 1 + j, (*chip, c), me).wait_recv()
                passed = copy(t, 4 + j, (*chip, c), sibling)
                passed.start()
                sends.append(passed)
        for t in range(n):
            copy(t, 0, sibling, me).wait_recv()
            for j, chip in enumerate(chips):
                copy(t, 4 + j, (*chip, 1 - c), me).wait_recv()
        for cp in sends:
            cp.wait_send()
        for cp in mine:
            cp.wait()

    any_spec = pl.BlockSpec(memory_space=pl.ANY)
    return pl.pallas_call(
        body, name=name, in_specs=[any_spec] * n, out_specs=[any_spec] * n + [pl.BlockSpec(memory_space=pltpu.VMEM)],
        out_shape=[_sds((N_DEV,) + b.shape, b.dtype) for b in blocks] + [_sds((8, LANE), F32)],
        scratch_shapes=[pltpu.SemaphoreType.DMA((n, 7)), pltpu.SemaphoreType.DMA((n, 7)), pltpu.SemaphoreType.DMA((n,))],
        compiler_params=pltpu.CompilerParams(has_side_effects=True),
    )(*blocks)


_HBM = pl.BlockSpec(memory_space=pltpu.HBM)
_SEM = pl.BlockSpec(memory_space=pltpu.SEMAPHORE)
_EFFECT = pltpu.SideEffectType.DATAFLOW_SIDE_EFFECTING


def split_start(name, bufs, n_copies, build):
    n = len(bufs)

    def body(*refs):
        for cp in build(refs[:n], refs[n], refs[n + 1]):
            cp.start()
        refs[-1][...] = jnp.zeros_like(refs[-1])

    outs = pl.pallas_call(
        body, name=name, in_specs=[_HBM] * n,
        out_shape=(pltpu.SemaphoreType.DMA((n_copies,)), pltpu.SemaphoreType.DMA((n_copies,)),
                   *[pltpu.HBM(b.shape, b.dtype) for b in bufs], _sds((8, LANE), F32)),
        out_specs=(_SEM, _SEM, *[_HBM] * n, pl.BlockSpec(memory_space=pltpu.VMEM)),
        input_output_aliases={i: 2 + i for i in range(n)},
        compiler_params=pltpu.CompilerParams(has_side_effects=_EFFECT),
    )(*[pltpu.with_memory_space_constraint(b, pltpu.HBM) for b in bufs])
    return (outs[0], outs[1]), list(outs[2:2 + n]), outs[-1]


def split_wait(name, sems, bufs, build, after):
    n = len(bufs)

    def body(*refs):
        for cp in build(refs[:n], refs[n], refs[n + 1]):
            cp.wait()

    return list(pl.pallas_call(
        body, name=name, in_specs=[_HBM] * n + [_SEM, _SEM, pl.BlockSpec(memory_space=pl.ANY)],
        out_shape=[pltpu.HBM(b.shape, b.dtype) for b in bufs], out_specs=[_HBM] * n,
        input_output_aliases={i: i for i in range(n)},
        compiler_params=pltpu.CompilerParams(has_side_effects=_EFFECT),
    )(*bufs, sems[0], sems[1], after))


def _remote(src, dst, send_sems, recv_sems, k, to):
    return pltpu.make_async_remote_copy(src_ref=src, dst_ref=dst, send_sem=send_sems.at[k], recv_sem=recv_sems.at[k],
                                        device_id=to, device_id_type=MESH)


def _other_chips(x, y):
    return [(1 - x, y), (x, 1 - y), (1 - x, 1 - y)]


def _gather_cross(n):
    def build(refs, ss, rs):
        x, y, c = _position()
        me = _slot((x, y, c))
        targets = [(x, y, 1 - c)] + [(*chip, c) for chip in _other_chips(x, y)]
        return [_remote(refs[t], refs[n + t].at[me], ss, rs, 4 * t + k, to) for t in range(n) for k, to in enumerate(targets)]
    return build


def _gather_pass(n):
    def build(refs, ss, rs):
        x, y, c = _position()
        cps = []
        for t in range(n):
            for j, chip in enumerate(_other_chips(x, y)):
                blk = refs[t].at[_slot((*chip, c))]
                cps.append(_remote(blk, blk, ss, rs, 3 * t + j, (x, y, 1 - c)))
        return cps
    return build


def _reduce_pair(n):
    def build(refs, ss, rs):
        x, y, c = _position()
        return [_remote(refs[t].at[j, 1 - c], refs[n + t].at[j], ss, rs, 4 * t + j, (x, y, 1 - c))
                for t in range(n) for j in range(4)]
    return build


def _reduce_cross(n):
    def build(refs, ss, rs):
        x, y, c = _position()
        return [_remote(refs[t].at[2 * chip[0] + chip[1]], refs[n + t].at[r], ss, rs, 3 * t + r, (*chip, c))
                for t in range(n) for r, chip in enumerate(_other_chips(x, y))]
    return build


def pair_add(name, part, landed, tr=256):
    _, _, R, C = part.shape
    tr = _row_tile(R, tr)

    def body(_, p_ref, l_ref, o_ref):
        o_ref[...] = (p_ref[...].astype(F32) + l_ref[...].astype(F32)).astype(o_ref.dtype)

    blk = pl.BlockSpec((None, tr, C), lambda j, i, core: (j, i, 0))
    core = lax.axis_index("c").astype(jnp.int32).reshape(1)
    return pl.pallas_call(
        body, name=name, out_shape=_sds((4, R, C), part.dtype), compiler_params=_cp(("parallel", "parallel")),
        grid_spec=pltpu.PrefetchScalarGridSpec(
            num_scalar_prefetch=1, grid=(4, R // tr),
            in_specs=[pl.BlockSpec((None, None, tr, C), lambda j, i, core: (j, core[0], i, 0)), blk], out_specs=blk),
    )(core, part, landed)


def _row_tile(R, pref):
    if R <= pref:
        return R
    t = (pref // 16) * 16
    while R % t:
        t -= 16
    return t


def all_reduce_small(name, vec, after):
    def body(v_ref, _, sum_ref, land_ref, send_sems, recv_sems):
        x, y, c = _position()
        me = (x, y, c)
        flip = lambda v, bit: 1 - v if bit else v
        peers = [(flip(x, k & 4), flip(y, k & 2), flip(c, k & 1)) for k in range(1, N_DEV)]
        land_ref[_slot(me)] = v_ref[...]
        sends = [pltpu.make_async_remote_copy(src_ref=v_ref, dst_ref=land_ref.at[_slot(me)], send_sem=send_sems.at[k],
                                              recv_sem=recv_sems.at[k], device_id=peer, device_id_type=MESH)
                 for k, peer in enumerate(peers)]
        for cp in sends:
            cp.start()
        for k, peer in enumerate(peers):
            pltpu.make_async_remote_copy(src_ref=v_ref, dst_ref=land_ref.at[_slot(peer)], send_sem=send_sems.at[k],
                                         recv_sem=recv_sems.at[k], device_id=peer, device_id_type=MESH).wait_recv()
        acc = land_ref[0]
        for s in range(1, N_DEV):
            acc = acc + land_ref[s]
        sum_ref[...] = acc
        for cp in sends:
            cp.wait_send()

    vm = pl.BlockSpec(memory_space=pltpu.VMEM)
    return pl.pallas_call(
        body, name=name, in_specs=[vm, pl.BlockSpec(memory_space=pl.ANY)], out_specs=[vm, vm],
        out_shape=[_sds(vec.shape, F32), _sds((N_DEV,) + vec.shape, F32)],
        scratch_shapes=[pltpu.SemaphoreType.DMA((7,)), pltpu.SemaphoreType.DMA((7,))],
        compiler_params=pltpu.CompilerParams(has_side_effects=True),
    )(vec, after)[0]


def _adamw_math(g, w, m, v):
    m2 = ADAM_B1 * m + (1.0 - ADAM_B1) * g
    v2 = ADAM_B2 * v + (1.0 - ADAM_B2) * (g * g)
    m_hat = m2 / (1.0 - ADAM_B1 ** ADAM_STEP)
    v_hat = v2 / (1.0 - ADAM_B2 ** ADAM_STEP)
    return -ADAM_LR * (m_hat / (jnp.sqrt(v_hat) + ADAM_EPS) + ADAM_WD * w), m2, v2


def adamw_shard(name, chip_parts, landed, w, m, v, layer, prev=None, tr=128):
    L, R, C = w.shape
    tr = _row_tile(R, tr)
    n_in = 5 + (4 if prev is not None else 0)

    def body(*refs):
        own_ref, land_ref, w_ref, m_ref, v_ref = refs[1:6]
        g_ref, d_ref, nm_ref, nv_ref = refs[1 + n_in:]
        g = own_ref[...].astype(F32)
        for s in range(landed.shape[0]):
            g = g + land_ref[s].astype(F32)
        g_ref[...] = g
        d_ref[...], nm_ref[...], nv_ref[...] = _adamw_math(g, w_ref[...], m_ref[...], v_ref[...])

    lay = pl.BlockSpec((None, tr, C), lambda i, chip: (layer, i, 0))
    own = pl.BlockSpec((None, tr, C), lambda i, chip: (chip[0], i, 0))
    chip = (2 * lax.axis_index("x") + lax.axis_index("y")).astype(jnp.int32).reshape(1)
    ins = [chip_parts, landed, w, m, v] + (list(prev) if prev is not None else [])
    return pl.pallas_call(
        body, name=name, out_shape=[_sds((L, R, C), F32)] * 4,
        grid_spec=pltpu.PrefetchScalarGridSpec(
            num_scalar_prefetch=1, grid=(R // tr,),
            in_specs=[own, pl.BlockSpec((landed.shape[0], tr, C), lambda i, chip: (0, i, 0)), lay, lay, lay]
            + [pl.BlockSpec(memory_space=pl.ANY)] * (n_in - 5),
            out_specs=[lay] * 4),
        input_output_aliases={6 + i: i for i in range(n_in - 5)}, compiler_params=_cp(("parallel",)),
    )(chip, *ins)


def adamw_small(name, g, w, m, v):
    def body(g_ref, w_ref, m_ref, v_ref, d_ref, nm_ref, nv_ref):
        d_ref[...], nm_ref[...], nv_ref[...] = _adamw_math(g_ref[...], w_ref[...], m_ref[...], v_ref[...])

    return pl.pallas_call(body, name=name, out_shape=[_sds(w.shape, F32)] * 3)(g, w, m, v)


def _pack(vs):
    flat = jnp.concatenate([v.reshape(-1) for v in vs])
    return flat.reshape(-1, LANE)


def _unpack(packed, like):
    flat, out, off = packed.reshape(-1), [], 0
    for v in like:
        out.append(flat[off:off + v.size].reshape(v.shape))
        off += v.size
    return out


class MeshWeights:
    def __init__(self, w_kv, w_out, pool_w_in, pool_w_grp, hgrn_w_in, hgrn_norm_g):
        self.n_grp, self.pg_loc, self.pg = pool_w_grp.shape[1:]
        wpin, hnorm, token = all_gather_blocks("gather_first", [pool_w_in[0].astype(COMM_DTYPE), hgrn_norm_g])
        self.first = (wpin, hnorm.reshape(1, -1))
        zero = token[0, 0]
        cast = lambda a: (a + zero).astype(COMM_DTYPE)
        self.gather, self.tokens = {}, []
        for layer, blocks in ((0, [cast(pool_w_grp[0]), cast(w_kv[0]), cast(w_out[0])]),
                              (1, [cast(hgrn_w_in[0])]), (2, [cast(w_kv[1]), cast(w_out[1])])):
            me = _slot(_position())
            lands = [lax.dynamic_update_index_in_dim(lax.empty((N_DEV,) + b.shape, b.dtype), b, me, 0) for b in blocks]
            n = len(blocks)
            sems, bufs, token = split_start(f"gather{layer}_cross", blocks + lands, 4 * n, _gather_cross(n))
            self.gather[layer] = (sems, bufs)
            self.tokens.append(token[0, 0])
        self.reduce = {}

    def first_weights(self):
        return self.first[0], self.first[1], self.tokens[0] + self.tokens[1] + self.tokens[2]

    def _pass_on(self, layer, after):
        sems, bufs = self.gather[layer]
        n = len(bufs) // 2
        gathered = split_wait(f"gather{layer}_cross_wait", sems, bufs, _gather_cross(n), after)[n:]
        sems, bufs, _ = split_start(f"gather{layer}_pass", gathered, 3 * n, _gather_pass(n))
        self.gather[layer] = (sems, bufs)

    def _gathered(self, layer, after):
        sems, bufs = self.gather[layer]
        return split_wait(f"gather{layer}_pass_wait", sems, bufs, _gather_pass(len(bufs)), after)

    def after_proj0(self, after):
        self._pass_on(0, after)

    def layer0_weights(self, after):
        wgrp, wkv, wout = self._gathered(0, after)
        rows = lambda g: g.reshape(-1, g.shape[-1])
        return wgrp.transpose(1, 0, 2, 3).reshape(self.n_grp, self.pg, self.pg), rows(wkv), rows(wout)

    def after_attn0(self, after):
        self._pass_on(1, after)

    def layer1_weights(self, after):
        return self._gathered(1, after)[0]

    def after_proj1(self, after):
        self._pass_on(2, after)

    def layer1_rest(self, after):
        wkv, wout = self._gathered(2, after)
        rows = lambda g: g.reshape(-1, g.shape[-1])
        return rows(wkv), rows(wout)

    def grads_ready(self, layer, grads):
        parts = {}
        for nm, g in grads.items():
            if nm == "pool_w_grp":
                g = g.reshape(self.n_grp, N_DEV, self.pg_loc, self.pg).transpose(1, 0, 2, 3)
            parts[nm] = g.reshape(4, 2, -1, g.shape[-1])
        names, srcs = list(parts), list(parts.values())
        lands = [lax.empty((4,) + p.shape[2:], p.dtype) for p in srcs]
        sems, bufs, token = split_start(f"reduce{layer}_pair", srcs + lands, 4 * len(srcs), _reduce_pair(len(srcs)))
        self.reduce[layer] = (names, sems, bufs)
        return token[0, 0]

    def grads_reduce(self, layer, after):
        names, sems, bufs = self.reduce[layer]
        n = len(names)
        bufs = split_wait(f"reduce{layer}_pair_wait", sems, bufs, _reduce_pair(n), after)
        chip_parts = [pair_add(f"pair_add_{nm}{layer}", bufs[t], bufs[n + t]) for t, nm in enumerate(names)]
        lands = [lax.empty((3,) + p.shape[1:], p.dtype) for p in chip_parts]
        sems, bufs, token = split_start(f"reduce{layer}_cross", chip_parts + lands, 3 * n, _reduce_cross(n))
        self.reduce[layer] = (names, sems, bufs)
        return token[0, 0]

    def grads_done(self, layer, after):
        names, sems, bufs = self.reduce[layer]
        n = len(names)
        bufs = split_wait(f"reduce{layer}_cross_wait", sems, bufs, _reduce_cross(n), after)
        return {nm: (bufs[t], bufs[n + t]) for t, nm in enumerate(names)}


def kernel(x, mem, norm_g, mem_norm_g, w_kv, w_out, pool_w_in, pool_w_grp, pool_scale, hgrn_w_in, hgrn_lb, hgrn_norm_g, final_g, loss_target, m_norm_g, m_mem_norm_g, m_w_kv, m_w_out, m_pool_w_in, m_pool_w_grp, m_pool_scale, m_hgrn_w_in, m_hgrn_lb, m_hgrn_norm_g, m_final_g, v_norm_g, v_mem_norm_g, v_w_kv, v_w_out, v_pool_w_in, v_pool_w_grp, v_pool_scale, v_hgrn_w_in, v_hgrn_lb, v_hgrn_norm_g, v_final_g):
    comm = MeshWeights(w_kv, w_out, pool_w_in, pool_w_grp, hgrn_w_in, hgrn_norm_g)
    loss, grad_x, small = run_step(x[0], mem[0], loss_target[0], norm_g, mem_norm_g, pool_scale, hgrn_lb, final_g, comm)
    loss = lax.psum(loss, AXES)
    rows = lambda a: a.reshape(a.shape[0], -1, a.shape[-1])
    state = dict(w_kv=(w_kv, m_w_kv, v_w_kv), w_out=(w_out, m_w_out, v_w_out), pool_w_in=(pool_w_in, m_pool_w_in, v_pool_w_in),
                 pool_w_grp=(pool_w_grp, m_pool_w_grp, v_pool_w_grp), hgrn_w_in=(hgrn_w_in, m_hgrn_w_in, v_hgrn_w_in))
    big = {}

    def update(group, layer, after):
        for nm, (chip_parts, landed) in comm.grads_done(group, after).items():
            w, m, v = state[nm]
            big[nm] = adamw_shard(f"adamw_{nm}{layer}", chip_parts, landed, rows(w), rows(m), rows(v),
                                  layer if w.shape[0] > 1 else 0, prev=big.get(nm))
            after = big[nm][0]
        return after

    after = update("0a", 0, update(1, 1, grad_x))
    names = ("norm_g", "mem_norm_g", "pool_scale", "hgrn_lb", "hgrn_norm_g", "final_g")
    like = (norm_g, mem_norm_g, pool_scale, hgrn_lb, _sds((1, N_DEV * hgrn_norm_g.shape[1]), F32), final_g)
    packed = _pack([small[nm].reshape(lk.shape) for nm, lk in zip(names, like)])
    reduced = _unpack(all_reduce_small("reduce_small", packed, after), like)
    n_loc = hgrn_norm_g.shape[1]
    reduced[4] = lax.dynamic_slice(reduced[4], (0, _slot(_position()) * n_loc), (1, n_loc))
    ws = (norm_g, mem_norm_g, pool_scale, hgrn_lb, hgrn_norm_g, final_g)
    ms = (m_norm_g, m_mem_norm_g, m_pool_scale, m_hgrn_lb, m_hgrn_norm_g, m_final_g)
    vs = (v_norm_g, v_mem_norm_g, v_pool_scale, v_hgrn_lb, v_hgrn_norm_g, v_final_g)
    res = adamw_small("adamw_small", _pack(reduced), _pack(ws), _pack(ms), _pack(vs))
    outs = {nm: [reduced[i]] + [_unpack(r, ws)[i] for r in res] for i, nm in enumerate(names)}
    update("0b", 0, res[0])
    for nm, (w, _, _) in state.items():
        outs[nm] = [r.reshape(w.shape) for r in big[nm]]
    order = ("norm_g", "mem_norm_g", "w_kv", "w_out", "pool_w_in", "pool_w_grp", "pool_scale", "hgrn_w_in", "hgrn_lb",
             "hgrn_norm_g", "final_g")
    return (loss, grad_x[None], *[outs[nm][0] for nm in order], *[outs[nm][1] for nm in order],
            *[outs[nm][2] for nm in order], *[outs[nm][3] for nm in order])
```

```python
import functools

import jax
import jax.numpy as jnp
from jax import lax
from jax.experimental import pallas as pl
from jax.experimental.pallas import tpu as pltpu

F32, BF16 = jnp.float32, jnp.bfloat16
MM_DTYPE = BF16
COMM_DTYPE = BF16
EPS = 1e-6
POOL_WINDOWS = (2, 4, 8, 16)
POOL_HALO = 16
HG_HEAD_DIM = 128
HG_CHUNK = 64
CA_HEADS = 4
N_DEV = 8
ADAM_LR, ADAM_B1, ADAM_B2, ADAM_EPS, ADAM_WD, ADAM_STEP = 0.001, 0.9, 0.999, 1e-08, 0.01, 10
VMEM_LIMIT = 48 << 20
VMEM_LIMIT_TALL = 56 << 20
LANE = 128
MESH = pl.DeviceIdType.MESH
AXES = ("x", "y", "c")

NN = (((1,), (0,)), ((), ()))
NT = (((1,), (1,)), ((), ()))
TN = (((0,), (0,)), ((), ()))


def _cp(sem=None, vmem=VMEM_LIMIT):
    return pltpu.CompilerParams(dimension_semantics=sem, vmem_limit_bytes=vmem)


def _dot(a, b, dims):
    return lax.dot_general(a.astype(MM_DTYPE), b.astype(MM_DTYPE), dims, preferred_element_type=F32)


def _sig(v):
    return 1.0 / (1.0 + jnp.exp(-v))


def _tile(n, pref):
    if n <= pref:
        return n
    t = (pref // LANE) * LANE
    while n % t:
        t -= LANE
    return t


def _sds(shape, dtype):
    return jax.ShapeDtypeStruct(tuple(shape), dtype)


def _mm(name, grid, sem, ins, in_specs, out_shapes, out_specs, dims, epi=None, nk=1, acc_shape=None, aliases=None,
        vmem=VMEM_LIMIT):
    n_in, n_out = len(ins), len(out_shapes)

    def body(*refs):
        a_ref, b_ref = refs[0], refs[1]
        extra = refs[2:n_in]
        outs = refs[n_in:n_in + n_out]
        p = _dot(a_ref[...], b_ref[...], dims)

        def finish(acc):
            res = epi(acc, *extra) if epi is not None else (acc,)
            for r, o_ref in zip(res, outs):
                o_ref[...] = r.astype(o_ref.dtype)

        if nk == 1:
            finish(p)
        else:
            acc_ref = refs[n_in + n_out]
            k = pl.program_id(len(grid) - 1)

            @pl.when(k == 0)
            def _():
                acc_ref[...] = p

            @pl.when(k > 0)
            def _():
                acc_ref[...] += p

            @pl.when(k == nk - 1)
            def _():
                finish(acc_ref[...])

    scratch = [pltpu.VMEM(acc_shape, F32)] if nk > 1 else []
    return pl.pallas_call(
        body, name=name, grid=grid, in_specs=in_specs, out_specs=out_specs, out_shape=out_shapes,
        scratch_shapes=scratch, compiler_params=_cp(sem, vmem), input_output_aliases=aliases or {},
    )(*ins)


def proj_fwd(name, h, wblk, tm=1024):
    T, D = h.shape
    nblk, _, nb = wblk.shape
    tm = _tile(T, tm)
    return _mm(name, (nblk, T // tm), ("parallel", "parallel"), [h, wblk],
               [pl.BlockSpec((tm, D), lambda j, i: (i, 0)), pl.BlockSpec((None, D, nb), lambda j, i: (j, 0, 0))],
               [_sds((T, nblk * nb), F32)], [pl.BlockSpec((tm, nb), lambda j, i: (i, j))], NN)[0]


def proj_bwd_w(name, h, dproj, nblk, tm=1024, tn=1024):
    T, D = h.shape
    N = dproj.shape[1]
    nb = N // nblk
    tm, tn = _tile(D, tm), _tile(nb, tn)
    per = nb // tn
    return _mm(name, (D // tm, N // tn), ("parallel", "parallel"), [h, dproj],
               [pl.BlockSpec((T, tm), lambda i, n: (0, i)), pl.BlockSpec((T, tn), lambda i, n: (0, n))],
               [_sds((nblk, D, nb), COMM_DTYPE)], [pl.BlockSpec((None, tm, tn), lambda i, n: (n // per, i, n % per))], TN)[0]


def proj_bwd_x(name, dproj, wblk, after, tm=1024):
    T, N = dproj.shape
    nblk, D, nb = wblk.shape
    tm = _tile(T, tm)
    return _mm(name, (T // tm, nblk), ("parallel", "arbitrary"), [dproj, wblk, jnp.reshape(after, (1, 1))],
               [pl.BlockSpec((tm, nb), lambda i, k: (i, k)), pl.BlockSpec((None, D, nb), lambda i, k: (k, 0, 0)),
                pl.BlockSpec(memory_space=pl.ANY)],
               [_sds((T, D), F32)], [pl.BlockSpec((tm, D), lambda i, k: (i, 0))], NT, nk=nblk, acc_shape=(tm, D),
               vmem=VMEM_LIMIT_TALL)[0]


def out_fwd(name, branch, wout, x, tm=1024, tn=512):
    T, E = branch.shape
    D = wout.shape[1]
    tm, tn = _tile(T, tm), _tile(D, tn)
    return _mm(name, (T // tm, D // tn), ("parallel", "parallel"), [branch, wout, x],
               [pl.BlockSpec((tm, E), lambda i, j: (i, 0)), pl.BlockSpec((E, tn), lambda i, j: (0, j)),
                pl.BlockSpec((tm, tn), lambda i, j: (i, j))],
               [_sds((T, D), F32)], [pl.BlockSpec((tm, tn), lambda i, j: (i, j))], NN,
               epi=lambda acc, x_ref: (acc + x_ref[...],))[0]


def mm_nt(name, a, b, out_dtype, tm=1024, tn=1024, res=None):
    M, K = a.shape
    N = b.shape[0]
    tm, tn = _tile(M, tm), _tile(N, tn)
    ins = [a, b] + ([res] if res is not None else [])
    specs = [pl.BlockSpec((tm, K), lambda i, j: (i, 0)), pl.BlockSpec((tn, K), lambda i, j: (j, 0))]
    if res is not None:
        specs.append(pl.BlockSpec((tm, tn), lambda i, j: (i, j)))
    epi = (lambda acc, r_ref: (acc + r_ref[...],)) if res is not None else None
    return _mm(name, (M // tm, N // tn), ("parallel", "parallel"), ins, specs,
               [_sds((M, N), out_dtype)], [pl.BlockSpec((tm, tn), lambda i, j: (i, j))], NT, epi=epi)[0]


def mm_nn(name, a, b, out_dtype, tm=512, tn=512):
    M, K = a.shape
    N = b.shape[1]
    tm, tn = _tile(M, tm), _tile(N, tn)
    return _mm(name, (M // tm, N // tn), ("parallel", "parallel"), [a, b],
               [pl.BlockSpec((tm, K), lambda i, j: (i, 0)), pl.BlockSpec((K, tn), lambda i, j: (0, j))],
               [_sds((M, N), out_dtype)], [pl.BlockSpec((tm, tn), lambda i, j: (i, j))], NN)[0]


def mm_tn(name, a, b, out_dtype, tm=512, tn=1024):
    K, M = a.shape
    N = b.shape[1]
    tm, tn = _tile(M, tm), _tile(N, tn)
    return _mm(name, (M // tm, N // tn), ("parallel", "parallel"), [a, b],
               [pl.BlockSpec((K, tm), lambda i, j: (0, i)), pl.BlockSpec((K, tn), lambda i, j: (0, j))],
               [_sds((M, N), out_dtype)], [pl.BlockSpec((tm, tn), lambda i, j: (i, j))], TN)[0]


def rmsnorm_fwd(name, x, g, tm=512):
    T, D = x.shape
    tm = _tile(T, tm)

    def body(x_ref, g_ref, o_ref):
        xf = x_ref[...]
        r = lax.rsqrt(jnp.mean(xf * xf, axis=-1, keepdims=True) + EPS)
        o_ref[...] = ((xf * r) * g_ref[...]).astype(o_ref.dtype)

    return pl.pallas_call(
        body, name=name, grid=(T // tm,),
        in_specs=[pl.BlockSpec((tm, D), lambda i: (i, 0)), pl.BlockSpec((1, D), lambda i: (0, 0))],
        out_specs=pl.BlockSpec((tm, D), lambda i: (i, 0)), out_shape=_sds((T, D), MM_DTYPE),
        compiler_params=_cp(("parallel",)),
    )(x, g.reshape(1, D))


def _rms_bwd_math(xf, g, dh):
    r = lax.rsqrt(jnp.mean(xf * xf, axis=-1, keepdims=True) + EPS)
    gd = dh * g
    dx = r * gd - xf * ((r * r * r) * jnp.mean(gd * xf, axis=-1, keepdims=True))
    dg = jnp.sum(dh * (xf * r), axis=0, keepdims=True)
    return dx, dg


def rmsnorm_bwd(name, x, g, dh, res=None, tm=256):
    T, D = x.shape
    tm = _tile(T, tm)
    has_res = res is not None

    def body(*refs):
        x_ref, g_ref, dh_ref = refs[:3]
        dx_ref, dxb_ref, dg_ref = refs[3 + has_res:]
        dx, dg = _rms_bwd_math(x_ref[...], g_ref[...], dh_ref[...])
        if has_res:
            dx = dx + refs[3][...]
        dx_ref[...] = dx
        dxb_ref[...] = dx.astype(dxb_ref.dtype)

        @pl.when(pl.program_id(0) == 0)
        def _():
            dg_ref[...] = jnp.zeros_like(dg_ref)

        dg_ref[...] += dg

    row = pl.BlockSpec((tm, D), lambda i: (i, 0))
    vec = pl.BlockSpec((1, D), lambda i: (0, 0))
    ins = [x, g.reshape(1, D), dh] + ([res] if has_res else [])
    return pl.pallas_call(
        body, name=name, grid=(T // tm,), in_specs=[row, vec, row] + ([row] if has_res else []),
        out_specs=[row, row, vec], out_shape=[_sds((T, D), F32), _sds((T, D), MM_DTYPE), _sds((1, D), F32)],
        compiler_params=_cp(("arbitrary",)),
    )(*ins)


def final_loss_bwd(name, x, g, target, tm=256):
    T, D = x.shape
    tm = _tile(T, tm)

    def body(x_ref, g_ref, t_ref, loss_ref, dx_ref, dxb_ref, dg_ref):
        xf, gv = x_ref[...], g_ref[...]
        r = lax.rsqrt(jnp.mean(xf * xf, axis=-1, keepdims=True) + EPS)
        err = (xf * r) * gv - t_ref[...]
        part = 0.5 * jnp.sum(jnp.mean(err * err, axis=-1, keepdims=True), axis=0, keepdims=True)
        dx, dg = _rms_bwd_math(xf, gv, err / D)
        dx_ref[...] = dx
        dxb_ref[...] = dx.astype(dxb_ref.dtype)

        @pl.when(pl.program_id(0) == 0)
        def _():
            dg_ref[...] = jnp.zeros_like(dg_ref)
            loss_ref[...] = jnp.zeros_like(loss_ref)

        dg_ref[...] += dg
        loss_ref[...] += jnp.broadcast_to(part, loss_ref.shape)

    row = pl.BlockSpec((tm, D), lambda i: (i, 0))
    vec = pl.BlockSpec((1, D), lambda i: (0, 0))
    return pl.pallas_call(
        body, name=name, grid=(T // tm,), in_specs=[row, vec, row],
        out_specs=[pl.BlockSpec((8, LANE), lambda i: (0, 0)), row, row, vec],
        out_shape=[_sds((8, LANE), F32), _sds((T, D), F32), _sds((T, D), MM_DTYPE), _sds((1, D), F32)],
        compiler_params=_cp(("arbitrary",)),
    )(x, g.reshape(1, D), target)


def _pool_tiles(T, emix):
    pg = emix // len(POOL_WINDOWS)
    tc = 256 if pg % 256 == 0 else LANE
    return pg, tc, _tile(T, 512)


def pool_fwd(name, proj, T, emix):
    pg, tc, R = _pool_tiles(T, emix)
    per_g = pg // tc

    def body(u_ref, o_ref):
        g = pl.program_id(0) // per_g
        for gi, w in enumerate(POOL_WINDOWS):
            @pl.when(g == gi)
            def _():
                for ci in range(T // R):
                    r0 = ci * R
                    cur = u_ref[r0:r0 + R, :]
                    halo = jnp.zeros((POOL_HALO, tc), F32) if ci == 0 else u_ref[r0 - POOL_HALO:r0, :]
                    s = jnp.concatenate([halo, cur], axis=0)
                    for st in range(w.bit_length() - 1):
                        s = s + pltpu.roll(s, 1 << st, axis=0)
                    t = r0 + lax.broadcasted_iota(jnp.int32, (R, 1), 0)
                    cnt = jnp.minimum(t + 1, w).astype(F32)
                    o_ref[r0:r0 + R, :] = (s[POOL_HALO:] / cnt - cur).astype(o_ref.dtype)

    return pl.pallas_call(
        body, name=name, grid=(emix // tc,), in_specs=[pl.BlockSpec((T, tc), lambda j: (0, j))],
        out_specs=pl.BlockSpec((T, tc), lambda j: (0, j)), out_shape=_sds((T, emix), MM_DTYPE),
        compiler_params=_cp(("parallel",)),
    )(proj)


def pool_bwd(name, dpooled, dproj, T, emix):
    pg, tc, R = _pool_tiles(T, emix)
    per_g = pg // tc

    def body(d_ref, _, o_ref):
        g = pl.program_id(0) // per_g
        for gi, w in enumerate(POOL_WINDOWS):
            @pl.when(g == gi)
            def _():
                n = R + POOL_HALO
                for ci in range(T // R):
                    r0 = ci * R
                    cur = d_ref[r0:r0 + R, :]
                    halo = jnp.zeros((POOL_HALO, tc), F32) if ci == T // R - 1 else d_ref[r0 + R:r0 + n, :]
                    t = r0 + lax.broadcasted_iota(jnp.int32, (n, 1), 0)
                    cnt = jnp.minimum(t + 1, w).astype(F32)
                    s = jnp.concatenate([cur, halo], axis=0) / cnt
                    for st in range(w.bit_length() - 1):
                        s = s + pltpu.roll(s, n - (1 << st), axis=0)
                    o_ref[r0:r0 + R, :] = (s[:R] - cur).astype(o_ref.dtype)

    return pl.pallas_call(
        body, name=name, grid=(emix // tc,),
        in_specs=[pl.BlockSpec((T, tc), lambda j: (0, j)), pl.BlockSpec(memory_space=pl.ANY)],
        out_specs=pl.BlockSpec((T, tc), lambda j: (0, j)), out_shape=_sds(dproj.shape, dproj.dtype),
        input_output_aliases={1: 0}, compiler_params=_cp(("parallel",)),
    )(dpooled, dproj)


def grp_fwd(name, pooled, wgrp, scale, proj, gate_off, E):
    T, emix = pooled.shape
    pg = emix // len(POOL_WINDOWS)
    tm = _tile(T, 2048)
    tc = 256 if pg % 256 == 0 and gate_off % 256 == 0 else LANE
    per = pg // tc

    def epi(acc, s_ref, g_ref):
        gt = g_ref[...]
        return acc, (acc * s_ref[...]) * (gt * _sig(gt))

    col = pl.BlockSpec((tm, tc), lambda n, i: (i, n))
    return _mm(name, (emix // tc, T // tm), ("parallel", "parallel"), [pooled, wgrp, scale, proj],
               [pl.BlockSpec((tm, pg), lambda n, i: (i, n // per)), pl.BlockSpec((None, pg, tc), lambda n, i: (n // per, 0, n % per)),
                pl.BlockSpec((1, tc), lambda n, i: (0, n)), pl.BlockSpec((tm, tc), lambda n, i: (i, gate_off // tc + n))],
               [_sds((T, emix), F32), _sds((T, E), MM_DTYPE)], [col, col], NN, epi=epi)


def grp_bwd_x(name, dy, wgrp):
    T, emix = dy.shape
    pg = emix // len(POOL_WINDOWS)
    tm = _tile(T, 1024)
    return _mm(name, (len(POOL_WINDOWS), T // tm), ("parallel", "parallel"), [dy, wgrp],
               [pl.BlockSpec((tm, pg), lambda g, i: (i, g)), pl.BlockSpec((None, pg, pg), lambda g, i: (g, 0, 0))],
               [_sds((T, emix), F32)], [pl.BlockSpec((tm, pg), lambda g, i: (i, g))], NT)[0]


def grp_bwd_w(name, pooled, dy):
    T, emix = dy.shape
    ng = len(POOL_WINDOWS)
    pg = emix // ng
    col = pl.BlockSpec((T, pg), lambda g: (0, g))
    return _mm(name, (ng,), ("parallel",), [pooled, dy], [col, col],
               [_sds((ng, pg, pg), COMM_DTYPE)], [pl.BlockSpec((None, pg, pg), lambda g: (g, 0, 0))], TN)[0]


def gate_bwd_pool(name, dbranch, y, scale, proj, gate_off, n_proj):
    T, emix = y.shape
    tm, tc = _tile(T, 1024), _tile(emix, 512)
    assert gate_off % tc == 0

    def body(db_ref, y_ref, s_ref, g_ref, dy_ref, dg_ref, ds_ref):
        db, yv, sc, gt = db_ref[...], y_ref[...], s_ref[...], g_ref[...]
        sg = _sig(gt)
        dmix = db * (gt * sg)
        dy_ref[...] = (dmix * sc).astype(dy_ref.dtype)
        dg_ref[...] = (db * (yv * sc) * (sg * (1.0 + gt * (1.0 - sg)))).astype(dg_ref.dtype)

        @pl.when(pl.program_id(1) == 0)
        def _():
            ds_ref[...] = jnp.zeros_like(ds_ref)

        ds_ref[...] += jnp.sum(dmix * yv, axis=0, keepdims=True)

    blk = pl.BlockSpec((tm, tc), lambda j, i: (i, j))
    vec = pl.BlockSpec((1, tc), lambda j, i: (0, j))
    gate = pl.BlockSpec((tm, tc), lambda j, i: (i, gate_off // tc + j))
    return pl.pallas_call(
        body, name=name, grid=(emix // tc, T // tm), in_specs=[blk, blk, vec, gate],
        out_specs=[blk, gate, vec],
        out_shape=[_sds((T, emix), MM_DTYPE), _sds((T, n_proj), MM_DTYPE), _sds((1, emix), F32)],
        compiler_params=_cp(("parallel", "arbitrary")),
    )(dbranch, y, scale, proj)


def attn_fwd(name, proj, kv, branch, q_off, gate_off, tm=1024):
    T = proj.shape[0]
    M, two_eca = kv.shape
    eca = two_eca // 2
    hd = eca // CA_HEADS
    E = branch.shape[1]
    tm = _tile(T, tm)
    scale = hd ** -0.5

    def body(q_ref, g_ref, k_ref, v_ref, _, ca_ref, br_ref):
        s = _dot(q_ref[...], k_ref[...], NT) * scale
        s = s - jnp.max(s, axis=-1, keepdims=True)
        e = jnp.exp(s)
        p = e / jnp.sum(e, axis=-1, keepdims=True)
        o = _dot(p, v_ref[...], NN)
        gt = g_ref[...]
        ca_ref[...] = o
        br_ref[...] = (o * (gt * _sig(gt))).astype(br_ref.dtype)

    return pl.pallas_call(
        body, name=name, grid=(CA_HEADS, T // tm),
        in_specs=[pl.BlockSpec((tm, hd), lambda h, i: (i, q_off // hd + h)),
                  pl.BlockSpec((tm, hd), lambda h, i: (i, (gate_off + E - eca) // hd + h)),
                  pl.BlockSpec((M, hd), lambda h, i: (0, h)), pl.BlockSpec((M, hd), lambda h, i: (0, CA_HEADS + h)),
                  pl.BlockSpec(memory_space=pl.ANY)],
        out_specs=[pl.BlockSpec((tm, hd), lambda h, i: (i, h)), pl.BlockSpec((tm, hd), lambda h, i: (i, (E - eca) // hd + h))],
        out_shape=[_sds((T, eca), F32), _sds(branch.shape, branch.dtype)],
        input_output_aliases={4: 1}, compiler_params=_cp(("parallel", "parallel")),
    )(proj, proj, kv, kv, branch)


def attn_bwd(name, proj, kv, ca, dbranch, dproj, q_off, gate_off, tm=1024):
    T = proj.shape[0]
    M, two_eca = kv.shape
    eca = two_eca // 2
    hd = eca // CA_HEADS
    E = dbranch.shape[1]
    tm = _tile(T, tm)
    scale = hd ** -0.5
    q_blk, g_blk = q_off // hd, (gate_off + E - eca) // hd

    def body(q_ref, g_ref, k_ref, v_ref, ca_ref, db_ref, _, dp_ref, dk_ref, dv_ref, dgate_buf):
        first_rows, part = pl.program_id(1) == 0, pl.program_id(2)

        @pl.when(part == 0)
        def _():
            q, k, v = q_ref[...], k_ref[...], v_ref[...]
            s = _dot(q, k, NT) * scale
            s = s - jnp.max(s, axis=-1, keepdims=True)
            e = jnp.exp(s)
            p = e / jnp.sum(e, axis=-1, keepdims=True)
            gt, db = g_ref[...], db_ref[...]
            sg = _sig(gt)
            do = db * (gt * sg)
            dgate_buf[...] = (db * ca_ref[...] * (sg * (1.0 + gt * (1.0 - sg)))).astype(dgate_buf.dtype)
            dpr = _dot(do, v, NT)
            ds = p * (dpr - jnp.sum(dpr * p, axis=-1, keepdims=True)) * scale
            dp_ref[...] = _dot(ds, k, NN).astype(dp_ref.dtype)

            @pl.when(first_rows)
            def _():
                dk_ref[...] = jnp.zeros_like(dk_ref)
                dv_ref[...] = jnp.zeros_like(dv_ref)

            dk_ref[...] += _dot(ds, q, TN)
            dv_ref[...] += _dot(p, do, TN)

        @pl.when(part == 1)
        def _():
            dp_ref[...] = dgate_buf[...]

    row = lambda blk: pl.BlockSpec((tm, hd), lambda h, i, p: (i, blk + h))
    out_dp = pl.BlockSpec((tm, hd), lambda h, i, p: (i, jnp.where(p == 0, q_blk, g_blk) + h))
    outs = pl.pallas_call(
        body, name=name, grid=(CA_HEADS, T // tm, 2),
        in_specs=[row(q_blk), row(g_blk), pl.BlockSpec((M, hd), lambda h, i, p: (0, h)),
                  pl.BlockSpec((M, hd), lambda h, i, p: (0, CA_HEADS + h)), row(0), row((E - eca) // hd),
                  pl.BlockSpec(memory_space=pl.ANY)],
        out_specs=[out_dp, pl.BlockSpec((M, hd), lambda h, i, p: (0, h)), pl.BlockSpec((M, hd), lambda h, i, p: (0, h))],
        out_shape=[_sds(dproj.shape, dproj.dtype), _sds((M, eca), F32), _sds((M, eca), F32)],
        scratch_shapes=[pltpu.VMEM((tm, hd), dproj.dtype)],
        input_output_aliases={6: 0}, compiler_params=_cp(("parallel", "arbitrary", "arbitrary")),
    )(proj, proj, kv, kv, ca, dbranch, dproj)
    return outs[0], jnp.concatenate([outs[1], outs[2]], axis=1)


def _split3(v):
    hi = v.astype(BF16)
    r1 = v - hi.astype(F32)
    mid = r1.astype(BF16)
    lo = (r1 - mid.astype(F32)).astype(BF16)
    return hi, mid, lo


def _tri_dot(tri, v):
    hi, mid, lo = _split3(v)
    d = functools.partial(lax.dot_general, dimension_numbers=NN, preferred_element_type=F32)
    return d(tri, hi) + d(tri, mid) + d(tri, lo)


def _causal_mask():
    C = HG_CHUNK
    return lax.broadcasted_iota(jnp.int32, (C, C), 0) >= lax.broadcasted_iota(jnp.int32, (C, C), 1)


def _cumsum_matrices(G):
    ri = lax.broadcasted_iota(jnp.int32, (G, G), 0)
    ci = lax.broadcasted_iota(jnp.int32, (G, G), 1)
    same = (ri // HG_CHUNK) == (ci // HG_CHUNK)
    return (jnp.where(same & (ri >= ci), 1.0, 0.0).astype(BF16), jnp.where(same & (ri <= ci), 1.0, 0.0).astype(BF16))


def _forget_gate(fi, lb):
    sig = _sig(fi)
    return sig, lb + (1.0 - lb) * sig


def _log_decay_cumsum(f_ref, b_s, lb, prefix, G):
    def group(i, carry):
        rows = pl.ds(pl.multiple_of(i * G, G), G)
        _, f = _forget_gate(f_ref[rows, :], lb)
        b_s[rows, :] = _tri_dot(prefix, jnp.log(f))
        return carry

    lax.fori_loop(0, b_s.shape[0] // G, group, 0)


def _lower_bound(lb2):
    l0, l1 = lb2[0:1, :], lb2[1:2, :]
    m = jnp.maximum(l0, l1)
    e0, e1 = jnp.exp(l0 - m), jnp.exp(l1 - m)
    sm0, sm1 = e0 / (e0 + e1), e1 / (e0 + e1)
    return (sm0 + sm1) - sm0, sm0, sm1


def _hg_chunk_fwd(qi, fi, lb, causal, b=None, prefix=None):
    sig, f = _forget_gate(fi, lb)
    k = 1.0 - f
    if b is None:
        b = _tri_dot(prefix, jnp.log(f))
    last = lax.broadcasted_iota(jnp.int32, b.shape, 0) == HG_CHUNK - 1
    bl = jnp.sum(jnp.where(last, b, 0.0), axis=0, keepdims=True)
    eb, enb, ete = jnp.exp(b), jnp.exp(-b), jnp.exp(bl - b)
    sq = _sig(qi)
    q = (qi * sq) * (HG_HEAD_DIM ** -0.5)
    q_dec, k_inv, k_te = q * eb, k * enb, k * ete
    att = jnp.where(causal, _dot(q_dec, k_inv, NT), 0.0)
    return dict(sig=sig, f=f, k=k, eb=eb, enb=enb, ete=ete, dcy=jnp.exp(bl), sq=sq, q_dec=q_dec, k_inv=k_inv, k_te=k_te,
                att=att, last=last)


def _hg_blocking(H, T):
    hb = 4 if H % 4 == 0 else (2 if H % 2 == 0 else 1)
    return hb, _tile(T, 1024)


def hgrn_fwd(name, proj, lb2, norm_g, emix, gate_off, E):
    T = proj.shape[0]
    hd, C = HG_HEAD_DIM, HG_CHUNK
    H, N = emix // hd, T // C
    hb, ts = _hg_blocking(H, T)
    W, nc, G = hb * hd, ts // C, _tile(ts, 256)

    def body(q_ref, f_ref, i_ref, g_ref, lb_ref, ng_ref, o_ref, st_ref, br_ref, state, b_s):
        @pl.when(pl.program_id(1) == 0)
        def _():
            state[...] = jnp.zeros_like(state)

        causal = _causal_mask()
        lb, _, _ = _lower_bound(lb_ref[...])
        ng = ng_ref[...]
        _log_decay_cumsum(f_ref, b_s, lb, _cumsum_matrices(G)[0], G)

        def chunk(n, carry):
            rows = pl.ds(pl.multiple_of(n * C, C), C)
            for h in range(hb):
                cols = slice(h * hd, (h + 1) * hd)
                v = i_ref[rows, cols]
                c = _hg_chunk_fwd(q_ref[rows, cols], f_ref[rows, cols], lb[:, cols], causal, b=b_s[rows, cols])
                st = state[h]
                o = _dot(c["att"], v, NN) + _dot(c["q_dec"], st, NT)
                st_ref[h, n] = st
                o_ref[rows, cols] = o
                r = lax.rsqrt(jnp.mean(o * o, axis=-1, keepdims=True) + EPS)
                gt = g_ref[rows, cols]
                br_ref[rows, cols] = (((o * r) * ng[:, cols]) * (gt * _sig(gt))).astype(br_ref.dtype)
                state[h] = st * c["dcy"] + _dot(v, c["k_te"], TN)
            return carry

        lax.fori_loop(0, nc, chunk, 0, unroll=4)

    col = lambda off: pl.BlockSpec((ts, W), lambda h, s: (s, off // W + h))
    vec = lambda r: pl.BlockSpec((r, W), lambda h, s: (0, h))
    return pl.pallas_call(
        body, name=name, grid=(H // hb, T // ts),
        in_specs=[col(0), col(emix), col(2 * emix), col(gate_off), vec(2), vec(1)],
        out_specs=[col(0), pl.BlockSpec((hb, nc, hd, hd), lambda h, s: (h, s, 0, 0)), col(0)],
        out_shape=[_sds((T, emix), F32), _sds((H, N, hd, hd), F32), _sds((T, E), MM_DTYPE)],
        scratch_shapes=[pltpu.VMEM((hb, hd, hd), F32), pltpu.VMEM((ts, W), F32)],
        compiler_params=_cp(("parallel", "arbitrary")),
    )(proj, proj, proj, proj, lb2, norm_g)


def hgrn_bwd(name, proj, lb2, norm_g, o_pre, states, dbranch, emix, gate_off):
    T, n_proj = proj.shape
    hd, C = HG_HEAD_DIM, HG_CHUNK
    H, N = emix // hd, T // C
    hb, ts = _hg_blocking(H, T)
    W, nc, ns = hb * hd, ts // C, T // ts
    qscale = HG_HEAD_DIM ** -0.5

    def body(q_ref, f_ref, i_ref, g_ref, lb_ref, ng_ref, o_ref, st_ref, db_ref, dp_ref, dlb_ref, dng_ref, stash, dst_ref, acc):
        first_block, part = pl.program_id(1) == 0, pl.program_id(2)

        @pl.when(part == 0)
        def _():
            @pl.when(first_block)
            def _():
                dst_ref[...] = jnp.zeros_like(dst_ref)
                acc[...] = jnp.zeros_like(acc)

            causal = _causal_mask()
            prefix, suffix = _cumsum_matrices(C)
            lb, sm0, sm1 = _lower_bound(lb_ref[...])
            ng = ng_ref[...]

            def chunk(j, carry):
                n = nc - 1 - j
                rows = pl.ds(pl.multiple_of(n * C, C), C)
                for h in range(hb):
                    cols = slice(h * hd, (h + 1) * hd)
                    dst = dst_ref[h]
                    qi, v, gt = q_ref[rows, cols], i_ref[rows, cols], g_ref[rows, cols]
                    c = _hg_chunk_fwd(qi, f_ref[rows, cols], lb[:, cols], causal, prefix=prefix)
                    st = st_ref[h, n]
                    o, db, ngh = o_ref[rows, cols], db_ref[rows, cols], ng[:, cols]
                    r = lax.rsqrt(jnp.mean(o * o, axis=-1, keepdims=True) + EPS)
                    sg = _sig(gt)
                    dmix = db * (gt * sg)
                    dgate = db * ((o * r) * ngh) * (sg * (1.0 + gt * (1.0 - sg)))
                    dyn = dmix * ngh
                    do = r * dyn - o * ((r * r * r) * jnp.mean(dyn * o, axis=-1, keepdims=True))
                    acc[1:2, cols] += jnp.sum(dmix * (o * r), axis=0, keepdims=True)
                    datt = jnp.where(causal, _dot(do, v, NT), 0.0)
                    dq_dec = _dot(datt, c["k_inv"], NN) + _dot(do, st, NN)
                    dk_inv = _dot(datt, c["q_dec"], TN)
                    dv = _dot(c["att"], do, TN) + _dot(c["k_te"], dst, NT)
                    dk_te = _dot(v, dst, NN)
                    ddcy = jnp.sum(dst * st, axis=0, keepdims=True)
                    dst_ref[h] = _dot(do, c["q_dec"], TN) + dst * c["dcy"]
                    t_te = dk_te * c["k_te"]
                    dbv = dq_dec * c["q_dec"] - dk_inv * c["k_inv"] - t_te
                    dbl = jnp.sum(t_te, axis=0, keepdims=True) + ddcy * c["dcy"]
                    dbv = dbv + jnp.where(c["last"], dbl, 0.0)
                    dlf = _tri_dot(suffix, dbv)
                    dk = dk_inv * c["enb"] + dk_te * c["ete"]
                    df = dlf / c["f"] - dk
                    sig, sq = c["sig"], c["sq"]
                    acc[0:1, cols] += jnp.sum(df * (1.0 - sig), axis=0, keepdims=True)
                    dqi = (dq_dec * c["eb"]) * qscale * (sq * (1.0 + qi * (1.0 - sq)))
                    dp_ref[rows, cols] = dqi.astype(dp_ref.dtype)
                    stash[0, rows, cols] = (df * (1.0 - lb[:, cols]) * (sig * (1.0 - sig))).astype(stash.dtype)
                    stash[1, rows, cols] = dv.astype(stash.dtype)
                    stash[2, rows, cols] = dgate.astype(stash.dtype)
                return carry

            lax.fori_loop(0, nc, chunk, 0, unroll=4)
            t = acc[0:1, :] * (sm0 * sm1)
            dlb_ref[...] = jnp.concatenate([-t, t], axis=0)
            dng_ref[...] = acc[1:2, :]

        for pi in range(3):
            @pl.when(part == pi + 1)
            def _():
                dp_ref[...] = stash[pi]

    col = lambda off: pl.BlockSpec((ts, W), lambda h, s, p: (ns - 1 - s, off // W + h))
    vec = lambda r: pl.BlockSpec((r, W), lambda h, s, p: (0, h))
    n_mix = emix // W
    out_col = pl.BlockSpec((ts, W), lambda h, s, p: (ns - 1 - s, jnp.where(p < 3, p * n_mix, gate_off // W) + h))
    return pl.pallas_call(
        body, name=name, grid=(H // hb, ns, 4),
        in_specs=[col(0), col(emix), col(2 * emix), col(gate_off), vec(2), vec(1), col(0),
                  pl.BlockSpec((hb, nc, hd, hd), lambda h, s, p: (h, ns - 1 - s, 0, 0)), col(0)],
        out_specs=[out_col, vec(2), vec(1)],
        out_shape=[_sds((T, n_proj), MM_DTYPE), _sds((2, emix), F32), _sds((1, emix), F32)],
        scratch_shapes=[pltpu.VMEM((3, ts, W), MM_DTYPE), pltpu.VMEM((hb, hd, hd), F32), pltpu.VMEM((8, W), F32)],
        compiler_params=_cp(("parallel", "arbitrary", "arbitrary")),
    )(proj, proj, proj, proj, lb2, norm_g, o_pre, states, dbranch)


class WholeWeights:
    def __init__(self, wkv, wout, wpin, wgrp, whin, hgrn_norm_g):
        self.w = dict(wkv=wkv, wout=wout, wpin=wpin, wgrp=wgrp, whin=whin, hnorm=hgrn_norm_g)
        self.grads = {}

    def first_weights(self):
        return self.w["wpin"], self.w["hnorm"], 0.0

    def after_proj0(self, after):
        pass

    def layer0_weights(self, after):
        return self.w["wgrp"], self.w["wkv"][0], self.w["wout"][0]

    def after_attn0(self, after):
        pass

    def layer1_weights(self, after):
        return self.w["whin"]

    def after_proj1(self, after):
        pass

    def layer1_rest(self, after):
        return self.w["wkv"][1], self.w["wout"][1]

    def grads_ready(self, layer, grads):
        self.grads[layer] = grads
        return 0.0

    def grads_reduce(self, layer, after):
        return 0.0


def run_step(x, mem, target, norm_g, mem_norm_g, pool_scale, hgrn_lb, final_g, comm):
    T, D = x.shape
    mem_n = rmsnorm_fwd("mem_norm", mem, mem_norm_g)
    wpin, hgrn_norm_g, zero = comm.first_weights()
    h0 = rmsnorm_fwd("norm0", x, norm_g[0] + zero)
    proj0 = proj_fwd("proj0", h0, wpin)
    comm.after_proj0(proj0)
    n0 = proj0.shape[1]
    E = n0 // 2
    eca = E // 4
    emix = E - eca
    q0, g0 = emix, emix + eca
    q1, g1 = 3 * emix, 3 * emix + eca
    pooled = pool_fwd("pool_fwd", proj0, T, emix)
    wgrp, wkv0, wout0 = comm.layer0_weights(pooled)
    y0, branch0 = grp_fwd("grp_fwd", pooled, wgrp, pool_scale, proj0, g0, E)
    kv0 = mm_nn("kv0", mem_n, wkv0, MM_DTYPE)
    ca0, branch0 = attn_fwd("attn_fwd0", proj0, kv0, branch0, q0, g0)
    comm.after_attn0(ca0)
    x1 = out_fwd("out0", branch0, wout0, x)
    h1 = rmsnorm_fwd("norm1", x1, norm_g[1])
    whin = comm.layer1_weights(h1)
    proj1 = proj_fwd("proj1", h1, whin)
    comm.after_proj1(proj1)
    o1, states, branch1 = hgrn_fwd("hgrn_fwd", proj1, hgrn_lb, hgrn_norm_g, emix, g1, E)
    wkv1, wout1 = comm.layer1_rest(o1)
    kv1 = mm_nn("kv1", mem_n, wkv1, MM_DTYPE)
    ca1, branch1 = attn_fwd("attn_fwd1", proj1, kv1, branch1, q1, g1)
    x2 = out_fwd("out1", branch1, wout1, x1)
    loss, dx2, dx2b, g_final = final_loss_bwd("final", x2, final_g, target)
    g_wout1 = mm_tn("gwout1", branch1, dx2b, COMM_DTYPE)
    dbranch1 = mm_nt("dbranch1", dx2b, wout1, F32)
    dproj1, g_lb, g_hnorm = hgrn_bwd("hgrn_bwd", proj1, hgrn_lb, hgrn_norm_g, o1, states, dbranch1, emix, g1)
    dproj1, dkv1 = attn_bwd("attn_bwd1", proj1, kv1, ca1, dbranch1, dproj1, q1, g1)
    g_wkv1 = mm_tn("gwkv1", mem_n, dkv1, COMM_DTYPE)
    dmem_n = mm_nt("dmem1", dkv1, wkv1, F32)
    g_whin = proj_bwd_w("gwhin", h1, dproj1, whin.shape[0])
    zero = comm.grads_ready(1, dict(w_kv=g_wkv1, w_out=g_wout1, hgrn_w_in=g_whin))
    dh1 = proj_bwd_x("dh1", dproj1, whin, zero)
    zero = zero + comm.grads_reduce(1, dh1)
    dx1, dx1b, g_norm1 = rmsnorm_bwd("norm1_bwd", x1, norm_g[1] + zero, dh1, res=dx2)
    dbranch0 = mm_nt("dbranch0", dx1b, wout0, F32)
    dy0, dproj0, g_scale = gate_bwd_pool("gate_bwd0", dbranch0, y0, pool_scale, proj0, g0, n0)
    dpooled = grp_bwd_x("dpooled", dy0, wgrp)
    dproj0 = pool_bwd("pool_bwd", dpooled, dproj0, T, emix)
    dproj0, dkv0 = attn_bwd("attn_bwd0", proj0, kv0, ca0, dbranch0, dproj0, q0, g0)
    g_wpin = proj_bwd_w("gwpin", h0, dproj0, wpin.shape[0])
    dkv0 = dkv0 + comm.grads_ready("0b", dict(pool_w_in=g_wpin))
    g_wkv0 = mm_tn("gwkv0", mem_n, dkv0, COMM_DTYPE)
    dmem_n = mm_nt("dmem0", dkv0, wkv0, F32, res=dmem_n)
    g_wout0 = mm_tn("gwout0", branch0, dx1b, COMM_DTYPE)
    g_wgrp = grp_bwd_w("gwgrp", pooled, dy0)
    zero = comm.grads_reduce("0b", g_wgrp)
    zero = zero + comm.grads_ready("0a", dict(w_kv=g_wkv0, w_out=g_wout0, pool_w_grp=g_wgrp))
    dh0 = proj_bwd_x("dh0", dproj0, wpin, zero)
    zero = zero + comm.grads_reduce("0a", dh0)
    grad_x, _, g_norm0 = rmsnorm_bwd("norm0_bwd", x, norm_g[0] + zero, dh0, res=dx1)
    _, _, g_mem = rmsnorm_bwd("mem_norm_bwd", mem, mem_norm_g, dmem_n)

    small = dict(norm_g=jnp.concatenate([g_norm0, g_norm1], axis=0), mem_norm_g=g_mem[0], pool_scale=g_scale,
                 hgrn_lb=g_lb, hgrn_norm_g=g_hnorm, final_g=g_final[0])
    return loss[0, 0], grad_x, small


def _position():
    return lax.axis_index("x"), lax.axis_index("y"), lax.axis_index("c")


def _slot(p):
    return 4 * p[0] + 2 * p[1] + p[2]


def all_gather_blocks(name, blocks):
    n = len(blocks)

    def body(*refs):
        ins, outs, token = refs[:n], refs[n:2 * n], refs[2 * n]
        send_sems, recv_sems, local_sems = refs[2 * n + 1:]
        token[...] = jnp.zeros_like(token)
        x, y, c = _position()
        me, sibling = (x, y, c), (x, y, 1 - c)
        chips = _other_chips(x, y)

        def copy(t, k, block, to, src=None):
            dst = outs[t].at[_slot(block)]
            return pltpu.make_async_remote_copy(src_ref=dst if src is None else src, dst_ref=dst, send_sem=send_sems.at[t, k],
                                                recv_sem=recv_sems.at[t, k], device_id=to, device_id_type=MESH)

        mine = [pltpu.make_async_copy(ins[t], outs[t].at[_slot(me)], local_sems.at[t]) for t in range(n)]
        for cp in mine:
            cp.start()
        sends = []
        for t in range(n):
            sends.append(copy(t, 0, me, sibling, src=ins[t]))
            sends += [copy(t, 1 + j, me, (*chip, c), src=ins[t]) for j, chip in enumerate(chips)]
        for cp in sends:
            cp.start()
        for t in range(n):
            for j, chip in enumerate(chips):
                copy(t, 1 + j, (*chip, c), me).wait_recv()
                passed = copy(t, 4 + j, (*chip, c), sibling)
                passed.start()
                sends.append(passed)
        for t in range(n):
            copy(t, 0, sibling, me).wait_recv()
            for j, chip in enumerate(chips):
                copy(t, 4 + j, (*chip, 1 - c), me).wait_recv()
        for cp in sends:
            cp.wait_send()
        for cp in mine:
            cp.wait()

    any_spec = pl.BlockSpec(memory_space=pl.ANY)
    return pl.pallas_call(
        body, name=name, in_specs=[any_spec] * n, out_specs=[any_spec] * n + [pl.BlockSpec(memory_space=pltpu.VMEM)],
        out_shape=[_sds((N_DEV,) + b.shape, b.dtype) for b in blocks] + [_sds((8, LANE), F32)],
        scratch_shapes=[pltpu.SemaphoreType.DMA((n, 7)), pltpu.SemaphoreType.DMA((n, 7)), pltpu.SemaphoreType.DMA((n,))],
        compiler_params=pltpu.CompilerParams(has_side_effects=True),
    )(*blocks)


_HBM = pl.BlockSpec(memory_space=pltpu.HBM)
_SEM = pl.BlockSpec(memory_space=pltpu.SEMAPHORE)
_EFFECT = pltpu.SideEffectType.DATAFLOW_SIDE_EFFECTING


def split_start(name, bufs, n_copies, build):
    n = len(bufs)

    def body(*refs):
        for cp in build(refs[:n], refs[n], refs[n + 1]):
            cp.start()
        refs[-1][...] = jnp.zeros_like(refs[-1])

    outs = pl.pallas_call(
        body, name=name, in_specs=[_HBM] * n,
        out_shape=(pltpu.SemaphoreType.DMA((n_copies,)), pltpu.SemaphoreType.DMA((n_copies,)),
                   *[pltpu.HBM(b.shape, b.dtype) for b in bufs], _sds((8, LANE), F32)),
        out_specs=(_SEM, _SEM, *[_HBM] * n, pl.BlockSpec(memory_space=pltpu.VMEM)),
        input_output_aliases={i: 2 + i for i in range(n)},
        compiler_params=pltpu.CompilerParams(has_side_effects=_EFFECT),
    )(*[pltpu.with_memory_space_constraint(b, pltpu.HBM) for b in bufs])
    return (outs[0], outs[1]), list(outs[2:2 + n]), outs[-1]


def split_wait(name, sems, bufs, build, after):
    n = len(bufs)

    def body(*refs):
        for cp in build(refs[:n], refs[n], refs[n + 1]):
            cp.wait()

    return list(pl.pallas_call(
        body, name=name, in_specs=[_HBM] * n + [_SEM, _SEM, pl.BlockSpec(memory_space=pl.ANY)],
        out_shape=[pltpu.HBM(b.shape, b.dtype) for b in bufs], out_specs=[_HBM] * n,
        input_output_aliases={i: i for i in range(n)},
        compiler_params=pltpu.CompilerParams(has_side_effects=_EFFECT),
    )(*bufs, sems[0], sems[1], after))


def _remote(src, dst, send_sems, recv_sems, k, to):
    return pltpu.make_async_remote_copy(src_ref=src, dst_ref=dst, send_sem=send_sems.at[k], recv_sem=recv_sems.at[k],
                                        device_id=to, device_id_type=MESH)


def _other_chips(x, y):
    return [(1 - x, y), (x, 1 - y), (1 - x, 1 - y)]


def _gather_cross(n):
    def build(refs, ss, rs):
        x, y, c = _position()
        me = _slot((x, y, c))
        targets = [(x, y, 1 - c)] + [(*chip, c) for chip in _other_chips(x, y)]
        return [_remote(refs[t], refs[n + t].at[me], ss, rs, 4 * t + k, to) for t in range(n) for k, to in enumerate(targets)]
    return build


def _gather_pass(n):
    def build(refs, ss, rs):
        x, y, c = _position()
        cps = []
        for t in range(n):
            for j, chip in enumerate(_other_chips(x, y)):
                blk = refs[t].at[_slot((*chip, c))]
                cps.append(_remote(blk, blk, ss, rs, 3 * t + j, (x, y, 1 - c)))
        return cps
    return build


def _reduce_pair(n):
    def build(refs, ss, rs):
        x, y, c = _position()
        return [_remote(refs[t].at[j, 1 - c], refs[n + t].at[j], ss, rs, 4 * t + j, (x, y, 1 - c))
                for t in range(n) for j in range(4)]
    return build


def _reduce_cross(n):
    def build(refs, ss, rs):
        x, y, c = _position()
        return [_remote(refs[t].at[2 * chip[0] + chip[1]], refs[n + t].at[r], ss, rs, 3 * t + r, (*chip, c))
                for t in range(n) for r, chip in enumerate(_other_chips(x, y))]
    return build


def pair_add(name, part, landed, tr=256):
    _, _, R, C = part.shape
    tr = _row_tile(R, tr)

    def body(_, p_ref, l_ref, o_ref):
        o_ref[...] = (p_ref[...].astype(F32) + l_ref[...].astype(F32)).astype(o_ref.dtype)

    blk = pl.BlockSpec((None, tr, C), lambda j, i, core: (j, i, 0))
    core = lax.axis_index("c").astype(jnp.int32).reshape(1)
    return pl.pallas_call(
        body, name=name, out_shape=_sds((4, R, C), part.dtype), compiler_params=_cp(("parallel", "parallel")),
        grid_spec=pltpu.PrefetchScalarGridSpec(
            num_scalar_prefetch=1, grid=(4, R // tr),
            in_specs=[pl.BlockSpec((None, None, tr, C), lambda j, i, core: (j, core[0], i, 0)), blk], out_specs=blk),
    )(core, part, landed)


def _row_tile(R, pref):
    if R <= pref:
        return R
    t = (pref // 16) * 16
    while R % t:
        t -= 16
    return t


def all_reduce_small(name, vec, after):
    def body(v_ref, _, sum_ref, land_ref, send_sems, recv_sems):
        x, y, c = _position()
        me = (x, y, c)
        flip = lambda v, bit: 1 - v if bit else v
        peers = [(flip(x, k & 4), flip(y, k & 2), flip(c, k & 1)) for k in range(1, N_DEV)]
        land_ref[_slot(me)] = v_ref[...]
        sends = [pltpu.make_async_remote_copy(src_ref=v_ref, dst_ref=land_ref.at[_slot(me)], send_sem=send_sems.at[k],
                                              recv_sem=recv_sems.at[k], device_id=peer, device_id_type=MESH)
                 for k, peer in enumerate(peers)]
        for cp in sends:
            cp.start()
        for k, peer in enumerate(peers):
            pltpu.make_async_remote_copy(src_ref=v_ref, dst_ref=land_ref.at[_slot(peer)], send_sem=send_sems.at[k],
                                         recv_sem=recv_sems.at[k], device_id=peer, device_id_type=MESH).wait_recv()
        acc = land_ref[0]
        for s in range(1, N_DEV):
            acc = acc + land_ref[s]
        sum_ref[...] = acc
        for cp in sends:
            cp.wait_send()

    vm = pl.BlockSpec(memory_space=pltpu.VMEM)
    return pl.pallas_call(
        body, name=name, in_specs=[vm, pl.BlockSpec(memory_space=pl.ANY)], out_specs=[vm, vm],
        out_shape=[_sds(vec.shape, F32), _sds((N_DEV,) + vec.shape, F32)],
        scratch_shapes=[pltpu.SemaphoreType.DMA((7,)), pltpu.SemaphoreType.DMA((7,))],
        compiler_params=pltpu.CompilerParams(has_side_effects=True),
    )(vec, after)[0]


def _adamw_math(g, w, m, v):
    m2 = ADAM_B1 * m + (1.0 - ADAM_B1) * g
    v2 = ADAM_B2 * v + (1.0 - ADAM_B2) * (g * g)
    m_hat = m2 / (1.0 - ADAM_B1 ** ADAM_STEP)
    v_hat = v2 / (1.0 - ADAM_B2 ** ADAM_STEP)
    return -ADAM_LR * (m_hat / (jnp.sqrt(v_hat) + ADAM_EPS) + ADAM_WD * w), m2, v2


def adamw_shard(name, chip_parts, landed, w, m, v, layer, prev=None, tr=128):
    L, R, C = w.shape
    tr = _row_tile(R, tr)
    n_in = 5 + (4 if prev is not None else 0)

    def body(*refs):
        own_ref, land_ref, w_ref, m_ref, v_ref = refs[1:6]
        g_ref, d_ref, nm_ref, nv_ref = refs[1 + n_in:]
        g = own_ref[...].astype(F32)
        for s in range(landed.shape[0]):
            g = g + land_ref[s].astype(F32)
        g_ref[...] = g
        d_ref[...], nm_ref[...], nv_ref[...] = _adamw_math(g, w_ref[...], m_ref[...], v_ref[...])

    lay = pl.BlockSpec((None, tr, C), lambda i, chip: (layer, i, 0))
    own = pl.BlockSpec((None, tr, C), lambda i, chip: (chip[0], i, 0))
    chip = (2 * lax.axis_index("x") + lax.axis_index("y")).astype(jnp.int32).reshape(1)
    ins = [chip_parts, landed, w, m, v] + (list(prev) if prev is not None else [])
    return pl.pallas_call(
        body, name=name, out_shape=[_sds((L, R, C), F32)] * 4,
        grid_spec=pltpu.PrefetchScalarGridSpec(
            num_scalar_prefetch=1, grid=(R // tr,),
            in_specs=[own, pl.BlockSpec((landed.shape[0], tr, C), lambda i, chip: (0, i, 0)), lay, lay, lay]
            + [pl.BlockSpec(memory_space=pl.ANY)] * (n_in - 5),
            out_specs=[lay] * 4),
        input_output_aliases={6 + i: i for i in range(n_in - 5)}, compiler_params=_cp(("parallel",)),
    )(chip, *ins)


def adamw_small(name, g, w, m, v):
    def body(g_ref, w_ref, m_ref, v_ref, d_ref, nm_ref, nv_ref):
        d_ref[...], nm_ref[...], nv_ref[...] = _adamw_math(g_ref[...], w_ref[...], m_ref[...], v_ref[...])

    return pl.pallas_call(body, name=name, out_shape=[_sds(w.shape, F32)] * 3)(g, w, m, v)


def _pack(vs):
    flat = jnp.concatenate([v.reshape(-1) for v in vs])
    return flat.reshape(-1, LANE)


def _unpack(packed, like):
    flat, out, off = packed.reshape(-1), [], 0
    for v in like:
        out.append(flat[off:off + v.size].reshape(v.shape))
        off += v.size
    return out


class MeshWeights:
    def __init__(self, w_kv, w_out, pool_w_in, pool_w_grp, hgrn_w_in, hgrn_norm_g):
        self.n_grp, self.pg_loc, self.pg = pool_w_grp.shape[1:]
        wpin, hnorm, token = all_gather_blocks("gather_first", [pool_w_in[0].astype(COMM_DTYPE), hgrn_norm_g])
        self.first = (wpin, hnorm.reshape(1, -1))
        zero = token[0, 0]
        cast = lambda a: (a + zero).astype(COMM_DTYPE)
        self.gather, self.tokens = {}, []
        for layer, blocks in ((0, [cast(pool_w_grp[0]), cast(w_kv[0]), cast(w_out[0])]),
                              (1, [cast(hgrn_w_in[0])]), (2, [cast(w_kv[1]), cast(w_out[1])])):
            me = _slot(_position())
            lands = [lax.dynamic_update_index_in_dim(lax.empty((N_DEV,) + b.shape, b.dtype), b, me, 0) for b in blocks]
            n = len(blocks)
            sems, bufs, token = split_start(f"gather{layer}_cross", blocks + lands, 4 * n, _gather_cross(n))
            self.gather[layer] = (sems, bufs)
            self.tokens.append(token[0, 0])
        self.reduce = {}

    def first_weights(self):
        return self.first[0], self.first[1], self.tokens[0] + self.tokens[1] + self.tokens[2]

    def _pass_on(self, layer, after):
        sems, bufs = self.gather[layer]
        n = len(bufs) // 2
        gathered = split_wait(f"gather{layer}_cross_wait", sems, bufs, _gather_cross(n), after)[n:]
        sems, bufs, _ = split_start(f"gather{layer}_pass", gathered, 3 * n, _gather_pass(n))
        self.gather[layer] = (sems, bufs)

    def _gathered(self, layer, after):
        sems, bufs = self.gather[layer]
        return split_wait(f"gather{layer}_pass_wait", sems, bufs, _gather_pass(len(bufs)), after)

    def after_proj0(self, after):
        self._pass_on(0, after)

    def layer0_weights(self, after):
        wgrp, wkv, wout = self._gathered(0, after)
        rows = lambda g: g.reshape(-1, g.shape[-1])
        return wgrp.transpose(1, 0, 2, 3).reshape(self.n_grp, self.pg, self.pg), rows(wkv), rows(wout)

    def after_attn0(self, after):
        self._pass_on(1, after)

    def layer1_weights(self, after):
        return self._gathered(1, after)[0]

    def after_proj1(self, after):
        self._pass_on(2, after)

    def layer1_rest(self, after):
        wkv, wout = self._gathered(2, after)
        rows = lambda g: g.reshape(-1, g.shape[-1])
        return rows(wkv), rows(wout)

    def grads_ready(self, layer, grads):
        parts = {}
        for nm, g in grads.items():
            if nm == "pool_w_grp":
                g = g.reshape(self.n_grp, N_DEV, self.pg_loc, self.pg).transpose(1, 0, 2, 3)
            parts[nm] = g.reshape(4, 2, -1, g.shape[-1])
        names, srcs = list(parts), list(parts.values())
        lands = [lax.empty((4,) + p.shape[2:], p.dtype) for p in srcs]
        sems, bufs, token = split_start(f"reduce{layer}_pair", srcs + lands, 4 * len(srcs), _reduce_pair(len(srcs)))
        self.reduce[layer] = (names, sems, bufs)
        return token[0, 0]

    def grads_reduce(self, layer, after):
        names, sems, bufs = self.reduce[layer]
        n = len(names)
        bufs = split_wait(f"reduce{layer}_pair_wait", sems, bufs, _reduce_pair(n), after)
        chip_parts = [pair_add(f"pair_add_{nm}{layer}", bufs[t], bufs[n + t]) for t, nm in enumerate(names)]
        lands = [lax.empty((3,) + p.shape[1:], p.dtype) for p in chip_parts]
        sems, bufs, token = split_start(f"reduce{layer}_cross", chip_parts + lands, 3 * n, _reduce_cross(n))
        self.reduce[layer] = (names, sems, bufs)
        return token[0, 0]

    def grads_done(self, layer, after):
        names, sems, bufs = self.reduce[layer]
        n = len(names)
        bufs = split_wait(f"reduce{layer}_cross_wait", sems, bufs, _reduce_cross(n), after)
        return {nm: (bufs[t], bufs[n + t]) for t, nm in enumerate(names)}


def kernel(x, mem, norm_g, mem_norm_g, w_kv, w_out, pool_w_in, pool_w_grp, pool_scale, hgrn_w_in, hgrn_lb, hgrn_norm_g, final_g, loss_target, m_norm_g, m_mem_norm_g, m_w_kv, m_w_out, m_pool_w_in, m_pool_w_grp, m_pool_scale, m_hgrn_w_in, m_hgrn_lb, m_hgrn_norm_g, m_final_g, v_norm_g, v_mem_norm_g, v_w_kv, v_w_out, v_pool_w_in, v_pool_w_grp, v_pool_scale, v_hgrn_w_in, v_hgrn_lb, v_hgrn_norm_g, v_final_g):
    comm = MeshWeights(w_kv, w_out, pool_w_in, pool_w_grp, hgrn_w_in, hgrn_norm_g)
    loss, grad_x, small = run_step(x[0], mem[0], loss_target[0], norm_g, mem_norm_g, pool_scale, hgrn_lb, final_g, comm)
    loss = lax.psum(loss, AXES)
    rows = lambda a: a.reshape(a.shape[0], -1, a.shape[-1])
    state = dict(w_kv=(w_kv, m_w_kv, v_w_kv), w_out=(w_out, m_w_out, v_w_out), pool_w_in=(pool_w_in, m_pool_w_in, v_pool_w_in),
                 pool_w_grp=(pool_w_grp, m_pool_w_grp, v_pool_w_grp), hgrn_w_in=(hgrn_w_in, m_hgrn_w_in, v_hgrn_w_in))
    big = {}

    def update(group, layer, after):
        for nm, (chip_parts, landed) in comm.grads_done(group, after).items():
            w, m, v = state[nm]
            big[nm] = adamw_shard(f"adamw_{nm}{layer}", chip_parts, landed, rows(w), rows(m), rows(v),
                                  layer if w.shape[0] > 1 else 0, prev=big.get(nm))
            after = big[nm][0]
        return after

    after = update("0b", 0, update(1, 1, grad_x))
    names = ("norm_g", "mem_norm_g", "pool_scale", "hgrn_lb", "hgrn_norm_g", "final_g")
    like = (norm_g, mem_norm_g, pool_scale, hgrn_lb, _sds((1, N_DEV * hgrn_norm_g.shape[1]), F32), final_g)
    packed = _pack([small[nm].reshape(lk.shape) for nm, lk in zip(names, like)])
    reduced = _unpack(all_reduce_small("reduce_small", packed, after), like)
    n_loc = hgrn_norm_g.shape[1]
    reduced[4] = lax.dynamic_slice(reduced[4], (0, _slot(_position()) * n_loc), (1, n_loc))
    ws = (norm_g, mem_norm_g, pool_scale, hgrn_lb, hgrn_norm_g, final_g)
    ms = (m_norm_g, m_mem_norm_g, m_pool_scale, m_hgrn_lb, m_hgrn_norm_g, m_final_g)
    vs = (v_norm_g, v_mem_norm_g, v_pool_scale, v_hgrn_lb, v_hgrn_norm_g, v_final_g)
    res = adamw_small("adamw_small", _pack(reduced), _pack(ws), _pack(ms), _pack(vs))
    outs = {nm: [reduced[i]] + [_unpack(r, ws)[i] for r in res] for i, nm in enumerate(names)}
    update("0a", 0, res[0])
    for nm, (w, _, _) in state.items():
        outs[nm] = [r.reshape(w.shape) for r in big[nm]]
    order = ("norm_g", "mem_norm_g", "w_kv", "w_out", "pool_w_in", "pool_w_grp", "pool_scale", "hgrn_w_in", "hgrn_lb",
             "hgrn_norm_g", "final_g")
    return (loss, grad_x[None], *[outs[nm][0] for nm in order], *[outs[nm][1] for nm in order],
            *[outs[nm][2] for nm in order], *[outs[nm][3] for nm in order])
```

```python
import functools

import jax
import jax.numpy as jnp
from jax import lax
from jax.experimental import pallas as pl
from jax.experimental.pallas import tpu as pltpu

F32, BF16 = jnp.float32, jnp.bfloat16
MM_DTYPE = BF16
COMM_DTYPE = BF16
EPS = 1e-6
POOL_WINDOWS = (2, 4, 8, 16)
POOL_HALO = 16
HG_HEAD_DIM = 128
HG_CHUNK = 64
CA_HEADS = 4
N_DEV = 8
ADAM_LR, ADAM_B1, ADAM_B2, ADAM_EPS, ADAM_WD, ADAM_STEP = 0.001, 0.9, 0.999, 1e-08, 0.01, 10
VMEM_LIMIT = 48 << 20
VMEM_LIMIT_TALL = 56 << 20
LANE = 128
MESH = pl.DeviceIdType.MESH
AXES = ("x", "y", "c")

NN = (((1,), (0,)), ((), ()))
NT = (((1,), (1,)), ((), ()))
TN = (((0,), (0,)), ((), ()))


def _cp(sem=None, vmem=VMEM_LIMIT):
    return pltpu.CompilerParams(dimension_semantics=sem, vmem_limit_bytes=vmem)


def _dot(a, b, dims):
    return lax.dot_general(a.astype(MM_DTYPE), b.astype(MM_DTYPE), dims, preferred_element_type=F32)


def _sig(v):
    return 1.0 / (1.0 + jnp.exp(-v))


def _tile(n, pref):
    if n <= pref:
        return n
    t = (pref // LANE) * LANE
    while n % t:
        t -= LANE
    return t


def _sds(shape, dtype):
    return jax.ShapeDtypeStruct(tuple(shape), dtype)


def _mm(name, grid, sem, ins, in_specs, out_shapes, out_specs, dims, epi=None, nk=1, acc_shape=None, aliases=None,
        vmem=VMEM_LIMIT):
    n_in, n_out = len(ins), len(out_shapes)

    def body(*refs):
        a_ref, b_ref = refs[0], refs[1]
        extra = refs[2:n_in]
        outs = refs[n_in:n_in + n_out]
        p = _dot(a_ref[...], b_ref[...], dims)

        def finish(acc):
            res = epi(acc, *extra) if epi is not None else (acc,)
            for r, o_ref in zip(res, outs):
                o_ref[...] = r.astype(o_ref.dtype)

        if nk == 1:
            finish(p)
        else:
            acc_ref = refs[n_in + n_out]
            k = pl.program_id(len(grid) - 1)

            @pl.when(k == 0)
            def _():
                acc_ref[...] = p

            @pl.when(k > 0)
            def _():
                acc_ref[...] += p

            @pl.when(k == nk - 1)
            def _():
                finish(acc_ref[...])

    scratch = [pltpu.VMEM(acc_shape, F32)] if nk > 1 else []
    return pl.pallas_call(
        body, name=name, grid=grid, in_specs=in_specs, out_specs=out_specs, out_shape=out_shapes,
        scratch_shapes=scratch, compiler_params=_cp(sem, vmem), input_output_aliases=aliases or {},
    )(*ins)


def proj_fwd(name, h, wblk, tm=1024):
    T, D = h.shape
    nblk, _, nb = wblk.shape
    tm = _tile(T, tm)
    return _mm(name, (nblk, T // tm), ("parallel", "parallel"), [h, wblk],
               [pl.BlockSpec((tm, D), lambda j, i: (i, 0)), pl.BlockSpec((None, D, nb), lambda j, i: (j, 0, 0))],
               [_sds((T, nblk * nb), F32)], [pl.BlockSpec((tm, nb), lambda j, i: (i, j))], NN)[0]


def proj_bwd_w(name, h, dproj, nblk, tm=1024, tn=1024):
    T, D = h.shape
    N = dproj.shape[1]
    nb = N // nblk
    tm, tn = _tile(D, tm), _tile(nb, tn)
    per = nb // tn
    return _mm(name, (D // tm, N // tn), ("parallel", "parallel"), [h, dproj],
               [pl.BlockSpec((T, tm), lambda i, n: (0, i)), pl.BlockSpec((T, tn), lambda i, n: (0, n))],
               [_sds((nblk, D, nb), COMM_DTYPE)], [pl.BlockSpec((None, tm, tn), lambda i, n: (n // per, i, n % per))], TN)[0]


def proj_bwd_x(name, dproj, wblk, after, tm=1024):
    T, N = dproj.shape
    nblk, D, nb = wblk.shape
    tm = _tile(T, tm)
    return _mm(name, (T // tm, nblk), ("parallel", "arbitrary"), [dproj, wblk, jnp.reshape(after, (1, 1))],
               [pl.BlockSpec((tm, nb), lambda i, k: (i, k)), pl.BlockSpec((None, D, nb), lambda i, k: (k, 0, 0)),
                pl.BlockSpec(memory_space=pl.ANY)],
               [_sds((T, D), F32)], [pl.BlockSpec((tm, D), lambda i, k: (i, 0))], NT, nk=nblk, acc_shape=(tm, D),
               vmem=VMEM_LIMIT_TALL)[0]


def out_fwd(name, branch, wout, x, after, tm=1024, tn=512):
    T, E = branch.shape
    D = wout.shape[1]
    tm, tn = _tile(T, tm), _tile(D, tn)
    return _mm(name, (T // tm, D // tn), ("parallel", "parallel"), [branch, wout, x, jnp.reshape(after, (1, 1))],
               [pl.BlockSpec((tm, E), lambda i, j: (i, 0)), pl.BlockSpec((E, tn), lambda i, j: (0, j)),
                pl.BlockSpec((tm, tn), lambda i, j: (i, j)), pl.BlockSpec(memory_space=pl.ANY)],
               [_sds((T, D), F32)], [pl.BlockSpec((tm, tn), lambda i, j: (i, j))], NN,
               epi=lambda acc, x_ref, _: (acc + x_ref[...],))[0]


def mm_nt(name, a, b, out_dtype, tm=1024, tn=1024, res=None):
    M, K = a.shape
    N = b.shape[0]
    tm, tn = _tile(M, tm), _tile(N, tn)
    ins = [a, b] + ([res] if res is not None else [])
    specs = [pl.BlockSpec((tm, K), lambda i, j: (i, 0)), pl.BlockSpec((tn, K), lambda i, j: (j, 0))]
    if res is not None:
        specs.append(pl.BlockSpec((tm, tn), lambda i, j: (i, j)))
    epi = (lambda acc, r_ref: (acc + r_ref[...],)) if res is not None else None
    return _mm(name, (M // tm, N // tn), ("parallel", "parallel"), ins, specs,
               [_sds((M, N), out_dtype)], [pl.BlockSpec((tm, tn), lambda i, j: (i, j))], NT, epi=epi)[0]


def mm_nn(name, a, b, out_dtype, tm=512, tn=512):
    M, K = a.shape
    N = b.shape[1]
    tm, tn = _tile(M, tm), _tile(N, tn)
    return _mm(name, (M // tm, N // tn), ("parallel", "parallel"), [a, b],
               [pl.BlockSpec((tm, K), lambda i, j: (i, 0)), pl.BlockSpec((K, tn), lambda i, j: (0, j))],
               [_sds((M, N), out_dtype)], [pl.BlockSpec((tm, tn), lambda i, j: (i, j))], NN)[0]


def mm_tn(name, a, b, out_dtype, tm=512, tn=1024):
    K, M = a.shape
    N = b.shape[1]
    tm, tn = _tile(M, tm), _tile(N, tn)
    return _mm(name, (M // tm, N // tn), ("parallel", "parallel"), [a, b],
               [pl.BlockSpec((K, tm), lambda i, j: (0, i)), pl.BlockSpec((K, tn), lambda i, j: (0, j))],
               [_sds((M, N), out_dtype)], [pl.BlockSpec((tm, tn), lambda i, j: (i, j))], TN)[0]


def rmsnorm_fwd(name, x, g, tm=512):
    T, D = x.shape
    tm = _tile(T, tm)

    def body(x_ref, g_ref, o_ref):
        xf = x_ref[...]
        r = lax.rsqrt(jnp.mean(xf * xf, axis=-1, keepdims=True) + EPS)
        o_ref[...] = ((xf * r) * g_ref[...]).astype(o_ref.dtype)

    return pl.pallas_call(
        body, name=name, grid=(T // tm,),
        in_specs=[pl.BlockSpec((tm, D), lambda i: (i, 0)), pl.BlockSpec((1, D), lambda i: (0, 0))],
        out_specs=pl.BlockSpec((tm, D), lambda i: (i, 0)), out_shape=_sds((T, D), MM_DTYPE),
        compiler_params=_cp(("parallel",)),
    )(x, g.reshape(1, D))


def _rms_bwd_math(xf, g, dh):
    r = lax.rsqrt(jnp.mean(xf * xf, axis=-1, keepdims=True) + EPS)
    gd = dh * g
    dx = r * gd - xf * ((r * r * r) * jnp.mean(gd * xf, axis=-1, keepdims=True))
    dg = jnp.sum(dh * (xf * r), axis=0, keepdims=True)
    return dx, dg


def rmsnorm_bwd(name, x, g, dh, res=None, tm=256):
    T, D = x.shape
    tm = _tile(T, tm)
    has_res = res is not None

    def body(*refs):
        x_ref, g_ref, dh_ref = refs[:3]
        dx_ref, dxb_ref, dg_ref = refs[3 + has_res:]
        dx, dg = _rms_bwd_math(x_ref[...], g_ref[...], dh_ref[...])
        if has_res:
            dx = dx + refs[3][...]
        dx_ref[...] = dx
        dxb_ref[...] = dx.astype(dxb_ref.dtype)

        @pl.when(pl.program_id(0) == 0)
        def _():
            dg_ref[...] = jnp.zeros_like(dg_ref)

        dg_ref[...] += dg

    row = pl.BlockSpec((tm, D), lambda i: (i, 0))
    vec = pl.BlockSpec((1, D), lambda i: (0, 0))
    ins = [x, g.reshape(1, D), dh] + ([res] if has_res else [])
    return pl.pallas_call(
        body, name=name, grid=(T // tm,), in_specs=[row, vec, row] + ([row] if has_res else []),
        out_specs=[row, row, vec], out_shape=[_sds((T, D), F32), _sds((T, D), MM_DTYPE), _sds((1, D), F32)],
        compiler_params=_cp(("arbitrary",)),
    )(*ins)


def final_loss_bwd(name, x, g, target, tm=256):
    T, D = x.shape
    tm = _tile(T, tm)

    def body(x_ref, g_ref, t_ref, loss_ref, dx_ref, dxb_ref, dg_ref):
        xf, gv = x_ref[...], g_ref[...]
        r = lax.rsqrt(jnp.mean(xf * xf, axis=-1, keepdims=True) + EPS)
        err = (xf * r) * gv - t_ref[...]
        part = 0.5 * jnp.sum(jnp.mean(err * err, axis=-1, keepdims=True), axis=0, keepdims=True)
        dx, dg = _rms_bwd_math(xf, gv, err / D)
        dx_ref[...] = dx
        dxb_ref[...] = dx.astype(dxb_ref.dtype)

        @pl.when(pl.program_id(0) == 0)
        def _():
            dg_ref[...] = jnp.zeros_like(dg_ref)
            loss_ref[...] = jnp.zeros_like(loss_ref)

        dg_ref[...] += dg
        loss_ref[...] += jnp.broadcast_to(part, loss_ref.shape)

    row = pl.BlockSpec((tm, D), lambda i: (i, 0))
    vec = pl.BlockSpec((1, D), lambda i: (0, 0))
    return pl.pallas_call(
        body, name=name, grid=(T // tm,), in_specs=[row, vec, row],
        out_specs=[pl.BlockSpec((8, LANE), lambda i: (0, 0)), row, row, vec],
        out_shape=[_sds((8, LANE), F32), _sds((T, D), F32), _sds((T, D), MM_DTYPE), _sds((1, D), F32)],
        compiler_params=_cp(("arbitrary",)),
    )(x, g.reshape(1, D), target)


def _pool_tiles(T, emix):
    pg = emix // len(POOL_WINDOWS)
    tc = 256 if pg % 256 == 0 else LANE
    return pg, tc, _tile(T, 512)


def pool_fwd(name, proj, T, emix, after):
    pg, tc, R = _pool_tiles(T, emix)
    per_g = pg // tc

    def body(u_ref, _, o_ref):
        g = pl.program_id(0) // per_g
        for gi, w in enumerate(POOL_WINDOWS):
            @pl.when(g == gi)
            def _():
                for ci in range(T // R):
                    r0 = ci * R
                    cur = u_ref[r0:r0 + R, :]
                    halo = jnp.zeros((POOL_HALO, tc), F32) if ci == 0 else u_ref[r0 - POOL_HALO:r0, :]
                    s = jnp.concatenate([halo, cur], axis=0)
                    for st in range(w.bit_length() - 1):
                        s = s + pltpu.roll(s, 1 << st, axis=0)
                    t = r0 + lax.broadcasted_iota(jnp.int32, (R, 1), 0)
                    cnt = jnp.minimum(t + 1, w).astype(F32)
                    o_ref[r0:r0 + R, :] = (s[POOL_HALO:] / cnt - cur).astype(o_ref.dtype)

    return pl.pallas_call(
        body, name=name, grid=(emix // tc,),
        in_specs=[pl.BlockSpec((T, tc), lambda j: (0, j)), pl.BlockSpec(memory_space=pl.ANY)],
        out_specs=pl.BlockSpec((T, tc), lambda j: (0, j)), out_shape=_sds((T, emix), MM_DTYPE),
        compiler_params=_cp(("parallel",)),
    )(proj, jnp.reshape(after, (1, 1)))


def pool_bwd(name, dpooled, dproj, T, emix):
    pg, tc, R = _pool_tiles(T, emix)
    per_g = pg // tc

    def body(d_ref, _, o_ref):
        g = pl.program_id(0) // per_g
        for gi, w in enumerate(POOL_WINDOWS):
            @pl.when(g == gi)
            def _():
                n = R + POOL_HALO
                for ci in range(T // R):
                    r0 = ci * R
                    cur = d_ref[r0:r0 + R, :]
                    halo = jnp.zeros((POOL_HALO, tc), F32) if ci == T // R - 1 else d_ref[r0 + R:r0 + n, :]
                    t = r0 + lax.broadcasted_iota(jnp.int32, (n, 1), 0)
                    cnt = jnp.minimum(t + 1, w).astype(F32)
                    s = jnp.concatenate([cur, halo], axis=0) / cnt
                    for st in range(w.bit_length() - 1):
                        s = s + pltpu.roll(s, n - (1 << st), axis=0)
                    o_ref[r0:r0 + R, :] = (s[:R] - cur).astype(o_ref.dtype)

    return pl.pallas_call(
        body, name=name, grid=(emix // tc,),
        in_specs=[pl.BlockSpec((T, tc), lambda j: (0, j)), pl.BlockSpec(memory_space=pl.ANY)],
        out_specs=pl.BlockSpec((T, tc), lambda j: (0, j)), out_shape=_sds(dproj.shape, dproj.dtype),
        input_output_aliases={1: 0}, compiler_params=_cp(("parallel",)),
    )(dpooled, dproj)


def grp_fwd(name, pooled, wgrp, scale, proj, gate_off, E):
    T, emix = pooled.shape
    pg = emix // len(POOL_WINDOWS)
    tm = _tile(T, 2048)
    tc = 256 if pg % 256 == 0 and gate_off % 256 == 0 else LANE
    per = pg // tc

    def epi(acc, s_ref, g_ref):
        gt = g_ref[...]
        return acc, (acc * s_ref[...]) * (gt * _sig(gt))

    col = pl.BlockSpec((tm, tc), lambda n, i: (i, n))
    return _mm(name, (emix // tc, T // tm), ("parallel", "parallel"), [pooled, wgrp, scale, proj],
               [pl.BlockSpec((tm, pg), lambda n, i: (i, n // per)), pl.BlockSpec((None, pg, tc), lambda n, i: (n // per, 0, n % per)),
                pl.BlockSpec((1, tc), lambda n, i: (0, n)), pl.BlockSpec((tm, tc), lambda n, i: (i, gate_off // tc + n))],
               [_sds((T, emix), F32), _sds((T, E), MM_DTYPE)], [col, col], NN, epi=epi)


def grp_bwd_x(name, dy, wgrp):
    T, emix = dy.shape
    pg = emix // len(POOL_WINDOWS)
    tm = _tile(T, 1024)
    return _mm(name, (len(POOL_WINDOWS), T // tm), ("parallel", "parallel"), [dy, wgrp],
               [pl.BlockSpec((tm, pg), lambda g, i: (i, g)), pl.BlockSpec((None, pg, pg), lambda g, i: (g, 0, 0))],
               [_sds((T, emix), F32)], [pl.BlockSpec((tm, pg), lambda g, i: (i, g))], NT)[0]


def grp_bwd_w(name, pooled, dy):
    T, emix = dy.shape
    ng = len(POOL_WINDOWS)
    pg = emix // ng
    col = pl.BlockSpec((T, pg), lambda g: (0, g))
    return _mm(name, (ng,), ("parallel",), [pooled, dy], [col, col],
               [_sds((ng, pg, pg), COMM_DTYPE)], [pl.BlockSpec((None, pg, pg), lambda g: (g, 0, 0))], TN)[0]


def gate_bwd_pool(name, dbranch, y, scale, proj, gate_off, n_proj):
    T, emix = y.shape
    tm, tc = _tile(T, 1024), _tile(emix, 512)
    assert gate_off % tc == 0

    def body(db_ref, y_ref, s_ref, g_ref, dy_ref, dg_ref, ds_ref):
        db, yv, sc, gt = db_ref[...], y_ref[...], s_ref[...], g_ref[...]
        sg = _sig(gt)
        dmix = db * (gt * sg)
        dy_ref[...] = (dmix * sc).astype(dy_ref.dtype)
        dg_ref[...] = (db * (yv * sc) * (sg * (1.0 + gt * (1.0 - sg)))).astype(dg_ref.dtype)

        @pl.when(pl.program_id(1) == 0)
        def _():
            ds_ref[...] = jnp.zeros_like(ds_ref)

        ds_ref[...] += jnp.sum(dmix * yv, axis=0, keepdims=True)

    blk = pl.BlockSpec((tm, tc), lambda j, i: (i, j))
    vec = pl.BlockSpec((1, tc), lambda j, i: (0, j))
    gate = pl.BlockSpec((tm, tc), lambda j, i: (i, gate_off // tc + j))
    return pl.pallas_call(
        body, name=name, grid=(emix // tc, T // tm), in_specs=[blk, blk, vec, gate],
        out_specs=[blk, gate, vec],
        out_shape=[_sds((T, emix), MM_DTYPE), _sds((T, n_proj), MM_DTYPE), _sds((1, emix), F32)],
        compiler_params=_cp(("parallel", "arbitrary")),
    )(dbranch, y, scale, proj)


def attn_fwd(name, proj, kv, branch, q_off, gate_off, tm=1024):
    T = proj.shape[0]
    M, two_eca = kv.shape
    eca = two_eca // 2
    hd = eca // CA_HEADS
    E = branch.shape[1]
    tm = _tile(T, tm)
    scale = hd ** -0.5

    def body(q_ref, g_ref, k_ref, v_ref, _, ca_ref, br_ref):
        s = _dot(q_ref[...], k_ref[...], NT) * scale
        s = s - jnp.max(s, axis=-1, keepdims=True)
        e = jnp.exp(s)
        p = e / jnp.sum(e, axis=-1, keepdims=True)
        o = _dot(p, v_ref[...], NN)
        gt = g_ref[...]
        ca_ref[...] = o
        br_ref[...] = (o * (gt * _sig(gt))).astype(br_ref.dtype)

    return pl.pallas_call(
        body, name=name, grid=(CA_HEADS, T // tm),
        in_specs=[pl.BlockSpec((tm, hd), lambda h, i: (i, q_off // hd + h)),
                  pl.BlockSpec((tm, hd), lambda h, i: (i, (gate_off + E - eca) // hd + h)),
                  pl.BlockSpec((M, hd), lambda h, i: (0, h)), pl.BlockSpec((M, hd), lambda h, i: (0, CA_HEADS + h)),
                  pl.BlockSpec(memory_space=pl.ANY)],
        out_specs=[pl.BlockSpec((tm, hd), lambda h, i: (i, h)), pl.BlockSpec((tm, hd), lambda h, i: (i, (E - eca) // hd + h))],
        out_shape=[_sds((T, eca), F32), _sds(branch.shape, branch.dtype)],
        input_output_aliases={4: 1}, compiler_params=_cp(("parallel", "parallel")),
    )(proj, proj, kv, kv, branch)


def attn_bwd(name, proj, kv, ca, dbranch, dproj, q_off, gate_off, tm=1024):
    T = proj.shape[0]
    M, two_eca = kv.shape
    eca = two_eca // 2
    hd = eca // CA_HEADS
    E = dbranch.shape[1]
    tm = _tile(T, tm)
    scale = hd ** -0.5
    q_blk, g_blk = q_off // hd, (gate_off + E - eca) // hd

    def body(q_ref, g_ref, k_ref, v_ref, ca_ref, db_ref, _, dp_ref, dk_ref, dv_ref, dgate_buf):
        first_rows, part = pl.program_id(1) == 0, pl.program_id(2)

        @pl.when(part == 0)
        def _():
            q, k, v = q_ref[...], k_ref[...], v_ref[...]
            s = _dot(q, k, NT) * scale
            s = s - jnp.max(s, axis=-1, keepdims=True)
            e = jnp.exp(s)
            p = e / jnp.sum(e, axis=-1, keepdims=True)
            gt, db = g_ref[...], db_ref[...]
            sg = _sig(gt)
            do = db * (gt * sg)
            dgate_buf[...] = (db * ca_ref[...] * (sg * (1.0 + gt * (1.0 - sg)))).astype(dgate_buf.dtype)
            dpr = _dot(do, v, NT)
            ds = p * (dpr - jnp.sum(dpr * p, axis=-1, keepdims=True)) * scale
            dp_ref[...] = _dot(ds, k, NN).astype(dp_ref.dtype)

            @pl.when(first_rows)
            def _():
                dk_ref[...] = jnp.zeros_like(dk_ref)
                dv_ref[...] = jnp.zeros_like(dv_ref)

            dk_ref[...] += _dot(ds, q, TN)
            dv_ref[...] += _dot(p, do, TN)

        @pl.when(part == 1)
        def _():
            dp_ref[...] = dgate_buf[...]

    row = lambda blk: pl.BlockSpec((tm, hd), lambda h, i, p: (i, blk + h))
    out_dp = pl.BlockSpec((tm, hd), lambda h, i, p: (i, jnp.where(p == 0, q_blk, g_blk) + h))
    outs = pl.pallas_call(
        body, name=name, grid=(CA_HEADS, T // tm, 2),
        in_specs=[row(q_blk), row(g_blk), pl.BlockSpec((M, hd), lambda h, i, p: (0, h)),
                  pl.BlockSpec((M, hd), lambda h, i, p: (0, CA_HEADS + h)), row(0), row((E - eca) // hd),
                  pl.BlockSpec(memory_space=pl.ANY)],
        out_specs=[out_dp, pl.BlockSpec((M, hd), lambda h, i, p: (0, h)), pl.BlockSpec((M, hd), lambda h, i, p: (0, h))],
        out_shape=[_sds(dproj.shape, dproj.dtype), _sds((M, eca), F32), _sds((M, eca), F32)],
        scratch_shapes=[pltpu.VMEM((tm, hd), dproj.dtype)],
        input_output_aliases={6: 0}, compiler_params=_cp(("parallel", "arbitrary", "arbitrary")),
    )(proj, proj, kv, kv, ca, dbranch, dproj)
    return outs[0], jnp.concatenate([outs[1], outs[2]], axis=1)


def _split3(v):
    hi = v.astype(BF16)
    r1 = v - hi.astype(F32)
    mid = r1.astype(BF16)
    lo = (r1 - mid.astype(F32)).astype(BF16)
    return hi, mid, lo


def _tri_dot(tri, v):
    hi, mid, lo = _split3(v)
    d = functools.partial(lax.dot_general, dimension_numbers=NN, preferred_element_type=F32)
    return d(tri, hi) + d(tri, mid) + d(tri, lo)


def _causal_mask():
    C = HG_CHUNK
    return lax.broadcasted_iota(jnp.int32, (C, C), 0) >= lax.broadcasted_iota(jnp.int32, (C, C), 1)


def _cumsum_matrices(G):
    ri = lax.broadcasted_iota(jnp.int32, (G, G), 0)
    ci = lax.broadcasted_iota(jnp.int32, (G, G), 1)
    same = (ri // HG_CHUNK) == (ci // HG_CHUNK)
    return (jnp.where(same & (ri >= ci), 1.0, 0.0).astype(BF16), jnp.where(same & (ri <= ci), 1.0, 0.0).astype(BF16))


def _forget_gate(fi, lb):
    sig = _sig(fi)
    return sig, lb + (1.0 - lb) * sig


def _log_decay_cumsum(f_ref, b_s, lb, prefix, G):
    def group(i, carry):
        rows = pl.ds(pl.multiple_of(i * G, G), G)
        _, f = _forget_gate(f_ref[rows, :], lb)
        b_s[rows, :] = _tri_dot(prefix, jnp.log(f))
        return carry

    lax.fori_loop(0, b_s.shape[0] // G, group, 0)


def _lower_bound(lb2):
    l0, l1 = lb2[0:1, :], lb2[1:2, :]
    m = jnp.maximum(l0, l1)
    e0, e1 = jnp.exp(l0 - m), jnp.exp(l1 - m)
    sm0, sm1 = e0 / (e0 + e1), e1 / (e0 + e1)
    return (sm0 + sm1) - sm0, sm0, sm1


def _hg_chunk_fwd(qi, fi, lb, causal, b=None, prefix=None):
    sig, f = _forget_gate(fi, lb)
    k = 1.0 - f
    if b is None:
        b = _tri_dot(prefix, jnp.log(f))
    last = lax.broadcasted_iota(jnp.int32, b.shape, 0) == HG_CHUNK - 1
    bl = jnp.sum(jnp.where(last, b, 0.0), axis=0, keepdims=True)
    eb, enb, ete = jnp.exp(b), jnp.exp(-b), jnp.exp(bl - b)
    sq = _sig(qi)
    q = (qi * sq) * (HG_HEAD_DIM ** -0.5)
    q_dec, k_inv, k_te = q * eb, k * enb, k * ete
    att = jnp.where(causal, _dot(q_dec, k_inv, NT), 0.0)
    return dict(sig=sig, f=f, k=k, eb=eb, enb=enb, ete=ete, dcy=jnp.exp(bl), sq=sq, q_dec=q_dec, k_inv=k_inv, k_te=k_te,
                att=att, last=last)


def _hg_blocking(H, T):
    hb = 4 if H % 4 == 0 else (2 if H % 2 == 0 else 1)
    return hb, _tile(T, 1024)


def hgrn_fwd(name, proj, lb2, norm_g, emix, gate_off, E):
    T = proj.shape[0]
    hd, C = HG_HEAD_DIM, HG_CHUNK
    H, N = emix // hd, T // C
    hb, ts = _hg_blocking(H, T)
    W, nc, G = hb * hd, ts // C, _tile(ts, 256)

    def body(q_ref, f_ref, i_ref, g_ref, lb_ref, ng_ref, o_ref, st_ref, br_ref, state, b_s):
        @pl.when(pl.program_id(1) == 0)
        def _():
            state[...] = jnp.zeros_like(state)

        causal = _causal_mask()
        lb, _, _ = _lower_bound(lb_ref[...])
        ng = ng_ref[...]
        _log_decay_cumsum(f_ref, b_s, lb, _cumsum_matrices(G)[0], G)

        def chunk(n, carry):
            rows = pl.ds(pl.multiple_of(n * C, C), C)
            for h in range(hb):
                cols = slice(h * hd, (h + 1) * hd)
                v = i_ref[rows, cols]
                c = _hg_chunk_fwd(q_ref[rows, cols], f_ref[rows, cols], lb[:, cols], causal, b=b_s[rows, cols])
                st = state[h]
                o = _dot(c["att"], v, NN) + _dot(c["q_dec"], st, NT)
                st_ref[h, n] = st
                o_ref[rows, cols] = o
                r = lax.rsqrt(jnp.mean(o * o, axis=-1, keepdims=True) + EPS)
                gt = g_ref[rows, cols]
                br_ref[rows, cols] = (((o * r) * ng[:, cols]) * (gt * _sig(gt))).astype(br_ref.dtype)
                state[h] = st * c["dcy"] + _dot(v, c["k_te"], TN)
            return carry

        lax.fori_loop(0, nc, chunk, 0, unroll=8)

    col = lambda off: pl.BlockSpec((ts, W), lambda h, s: (s, off // W + h))
    vec = lambda r: pl.BlockSpec((r, W), lambda h, s: (0, h))
    return pl.pallas_call(
        body, name=name, grid=(H // hb, T // ts),
        in_specs=[col(0), col(emix), col(2 * emix), col(gate_off), vec(2), vec(1)],
        out_specs=[col(0), pl.BlockSpec((hb, nc, hd, hd), lambda h, s: (h, s, 0, 0)), col(0)],
        out_shape=[_sds((T, emix), F32), _sds((H, N, hd, hd), F32), _sds((T, E), MM_DTYPE)],
        scratch_shapes=[pltpu.VMEM((hb, hd, hd), F32), pltpu.VMEM((ts, W), F32)],
        compiler_params=_cp(("parallel", "arbitrary")),
    )(proj, proj, proj, proj, lb2, norm_g)


def hgrn_bwd(name, proj, lb2, norm_g, o_pre, states, dbranch, emix, gate_off):
    T, n_proj = proj.shape
    hd, C = HG_HEAD_DIM, HG_CHUNK
    H, N = emix // hd, T // C
    hb, ts = _hg_blocking(H, T)
    W, nc, ns = hb * hd, ts // C, T // ts
    qscale = HG_HEAD_DIM ** -0.5

    def body(q_ref, f_ref, i_ref, g_ref, lb_ref, ng_ref, o_ref, st_ref, db_ref, dp_ref, dlb_ref, dng_ref, stash, dst_ref, acc):
        first_block, part = pl.program_id(1) == 0, pl.program_id(2)

        @pl.when(part == 0)
        def _():
            @pl.when(first_block)
            def _():
                dst_ref[...] = jnp.zeros_like(dst_ref)
                acc[...] = jnp.zeros_like(acc)

            causal = _causal_mask()
            prefix, suffix = _cumsum_matrices(C)
            lb, sm0, sm1 = _lower_bound(lb_ref[...])
            ng = ng_ref[...]

            def chunk(j, carry):
                n = nc - 1 - j
                rows = pl.ds(pl.multiple_of(n * C, C), C)
                for h in range(hb):
                    cols = slice(h * hd, (h + 1) * hd)
                    dst = dst_ref[h]
                    qi, v, gt = q_ref[rows, cols], i_ref[rows, cols], g_ref[rows, cols]
                    c = _hg_chunk_fwd(qi, f_ref[rows, cols], lb[:, cols], causal, prefix=prefix)
                    st = st_ref[h, n]
                    o, db, ngh = o_ref[rows, cols], db_ref[rows, cols], ng[:, cols]
                    r = lax.rsqrt(jnp.mean(o * o, axis=-1, keepdims=True) + EPS)
                    sg = _sig(gt)
                    dmix = db * (gt * sg)
                    dgate = db * ((o * r) * ngh) * (sg * (1.0 + gt * (1.0 - sg)))
                    dyn = dmix * ngh
                    do = r * dyn - o * ((r * r * r) * jnp.mean(dyn * o, axis=-1, keepdims=True))
                    acc[1:2, cols] += jnp.sum(dmix * (o * r), axis=0, keepdims=True)
                    datt = jnp.where(causal, _dot(do, v, NT), 0.0)
                    dq_dec = _dot(datt, c["k_inv"], NN) + _dot(do, st, NN)
                    dk_inv = _dot(datt, c["q_dec"], TN)
                    dv = _dot(c["att"], do, TN) + _dot(c["k_te"], dst, NT)
                    dk_te = _dot(v, dst, NN)
                    ddcy = jnp.sum(dst * st, axis=0, keepdims=True)
                    dst_ref[h] = _dot(do, c["q_dec"], TN) + dst * c["dcy"]
                    t_te = dk_te * c["k_te"]
                    dbv = dq_dec * c["q_dec"] - dk_inv * c["k_inv"] - t_te
                    dbl = jnp.sum(t_te, axis=0, keepdims=True) + ddcy * c["dcy"]
                    dbv = dbv + jnp.where(c["last"], dbl, 0.0)
                    dlf = _tri_dot(suffix, dbv)
                    dk = dk_inv * c["enb"] + dk_te * c["ete"]
                    df = dlf / c["f"] - dk
                    sig, sq = c["sig"], c["sq"]
                    acc[0:1, cols] += jnp.sum(df * (1.0 - sig), axis=0, keepdims=True)
                    dqi = (dq_dec * c["eb"]) * qscale * (sq * (1.0 + qi * (1.0 - sq)))
                    dp_ref[rows, cols] = dqi.astype(dp_ref.dtype)
                    stash[0, rows, cols] = (df * (1.0 - lb[:, cols]) * (sig * (1.0 - sig))).astype(stash.dtype)
                    stash[1, rows, cols] = dv.astype(stash.dtype)
                    stash[2, rows, cols] = dgate.astype(stash.dtype)
                return carry

            lax.fori_loop(0, nc, chunk, 0, unroll=4)
            t = acc[0:1, :] * (sm0 * sm1)
            dlb_ref[...] = jnp.concatenate([-t, t], axis=0)
            dng_ref[...] = acc[1:2, :]

        for pi in range(3):
            @pl.when(part == pi + 1)
            def _():
                dp_ref[...] = stash[pi]

    col = lambda off: pl.BlockSpec((ts, W), lambda h, s, p: (ns - 1 - s, off // W + h))
    vec = lambda r: pl.BlockSpec((r, W), lambda h, s, p: (0, h))
    n_mix = emix // W
    out_col = pl.BlockSpec((ts, W), lambda h, s, p: (ns - 1 - s, jnp.where(p < 3, p * n_mix, gate_off // W) + h))
    return pl.pallas_call(
        body, name=name, grid=(H // hb, ns, 4),
        in_specs=[col(0), col(emix), col(2 * emix), col(gate_off), vec(2), vec(1), col(0),
                  pl.BlockSpec((hb, nc, hd, hd), lambda h, s, p: (h, ns - 1 - s, 0, 0)), col(0)],
        out_specs=[out_col, vec(2), vec(1)],
        out_shape=[_sds((T, n_proj), MM_DTYPE), _sds((2, emix), F32), _sds((1, emix), F32)],
        scratch_shapes=[pltpu.VMEM((3, ts, W), MM_DTYPE), pltpu.VMEM((hb, hd, hd), F32), pltpu.VMEM((8, W), F32)],
        compiler_params=_cp(("parallel", "arbitrary", "arbitrary")),
    )(proj, proj, proj, proj, lb2, norm_g, o_pre, states, dbranch)


class WholeWeights:
    def __init__(self, wkv, wout, wpin, wgrp, whin, hgrn_norm_g):
        self.w = dict(wkv=wkv, wout=wout, wpin=wpin, wgrp=wgrp, whin=whin, hnorm=hgrn_norm_g)
        self.grads = {}

    def first_weights(self):
        return self.w["wpin"], self.w["hnorm"], 0.0

    def after_proj0(self, after):
        return 0.0

    def layer0_weights(self, after):
        return self.w["wgrp"], self.w["wkv"][0], self.w["wout"][0]

    def after_attn0(self, after):
        return 0.0

    def layer1_weights(self, after):
        return self.w["whin"]

    def after_proj1(self, after):
        return 0.0

    def layer1_rest(self, after):
        return self.w["wkv"][1], self.w["wout"][1]

    def grads_ready(self, layer, grads):
        self.grads[layer] = grads
        return 0.0

    def grads_reduce(self, layer, after):
        return 0.0


def run_step(x, mem, target, norm_g, mem_norm_g, pool_scale, hgrn_lb, final_g, comm):
    T, D = x.shape
    mem_n = rmsnorm_fwd("mem_norm", mem, mem_norm_g)
    wpin, hgrn_norm_g, zero = comm.first_weights()
    h0 = rmsnorm_fwd("norm0", x, norm_g[0] + zero)
    proj0 = proj_fwd("proj0", h0, wpin)
    zero = comm.after_proj0(proj0)
    n0 = proj0.shape[1]
    E = n0 // 2
    eca = E // 4
    emix = E - eca
    q0, g0 = emix, emix + eca
    q1, g1 = 3 * emix, 3 * emix + eca
    pooled = pool_fwd("pool_fwd", proj0, T, emix, zero)
    wgrp, wkv0, wout0 = comm.layer0_weights(pooled)
    y0, branch0 = grp_fwd("grp_fwd", pooled, wgrp, pool_scale, proj0, g0, E)
    kv0 = mm_nn("kv0", mem_n, wkv0, MM_DTYPE)
    ca0, branch0 = attn_fwd("attn_fwd0", proj0, kv0, branch0, q0, g0)
    x1 = out_fwd("out0", branch0, wout0, x, comm.after_attn0(ca0))
    h1 = rmsnorm_fwd("norm1", x1, norm_g[1])
    whin = comm.layer1_weights(h1)
    proj1 = proj_fwd("proj1", h1, whin)
    zero = comm.after_proj1(proj1)
    o1, states, branch1 = hgrn_fwd("hgrn_fwd", proj1, hgrn_lb + zero, hgrn_norm_g, emix, g1, E)
    wkv1, wout1 = comm.layer1_rest(o1)
    kv1 = mm_nn("kv1", mem_n, wkv1, MM_DTYPE)
    ca1, branch1 = attn_fwd("attn_fwd1", proj1, kv1, branch1, q1, g1)
    x2 = out_fwd("out1", branch1, wout1, x1, ca1[:1, :1])
    loss, dx2, dx2b, g_final = final_loss_bwd("final", x2, final_g, target)
    g_wout1 = mm_tn("gwout1", branch1, dx2b, COMM_DTYPE)
    dbranch1 = mm_nt("dbranch1", dx2b, wout1, F32)
    dproj1, g_lb, g_hnorm = hgrn_bwd("hgrn_bwd", proj1, hgrn_lb, hgrn_norm_g, o1, states, dbranch1, emix, g1)
    dproj1, dkv1 = attn_bwd("attn_bwd1", proj1, kv1, ca1, dbranch1, dproj1, q1, g1)
    g_wkv1 = mm_tn("gwkv1", mem_n, dkv1, COMM_DTYPE)
    dmem_n = mm_nt("dmem1", dkv1, wkv1, F32)
    g_whin = proj_bwd_w("gwhin", h1, dproj1, whin.shape[0])
    zero = comm.grads_ready(1, dict(w_kv=g_wkv1, w_out=g_wout1, hgrn_w_in=g_whin))
    dh1 = proj_bwd_x("dh1", dproj1, whin, zero)
    zero = zero + comm.grads_reduce(1, dh1)
    dx1, dx1b, g_norm1 = rmsnorm_bwd("norm1_bwd", x1, norm_g[1] + zero, dh1, res=dx2)
    dbranch0 = mm_nt("dbranch0", dx1b, wout0, F32)
    dy0, dproj0, g_scale = gate_bwd_pool("gate_bwd0", dbranch0, y0, pool_scale, proj0, g0, n0)
    dpooled = grp_bwd_x("dpooled", dy0, wgrp)
    dproj0 = pool_bwd("pool_bwd", dpooled, dproj0, T, emix)
    dproj0, dkv0 = attn_bwd("attn_bwd0", proj0, kv0, ca0, dbranch0, dproj0, q0, g0)
    g_wpin = proj_bwd_w("gwpin", h0, dproj0, wpin.shape[0])
    dkv0 = dkv0 + comm.grads_ready("0b", dict(pool_w_in=g_wpin))
    g_wkv0 = mm_tn("gwkv0", mem_n, dkv0, COMM_DTYPE)
    dmem_n = mm_nt("dmem0", dkv0, wkv0, F32, res=dmem_n)
    g_wout0 = mm_tn("gwout0", branch0, dx1b, COMM_DTYPE)
    g_wgrp = grp_bwd_w("gwgrp", pooled, dy0)
    zero = comm.grads_reduce("0b", g_wgrp)
    zero = zero + comm.grads_ready("0a", dict(w_kv=g_wkv0, w_out=g_wout0, pool_w_grp=g_wgrp))
    dh0 = proj_bwd_x("dh0", dproj0, wpin, zero)
    zero = zero + comm.grads_reduce("0a", dh0)
    grad_x, _, g_norm0 = rmsnorm_bwd("norm0_bwd", x, norm_g[0] + zero, dh0, res=dx1)
    _, _, g_mem = rmsnorm_bwd("mem_norm_bwd", mem, mem_norm_g, dmem_n)

    small = dict(norm_g=jnp.concatenate([g_norm0, g_norm1], axis=0), mem_norm_g=g_mem[0], pool_scale=g_scale,
                 hgrn_lb=g_lb, hgrn_norm_g=g_hnorm, final_g=g_final[0])
    return loss[0, 0], grad_x, small


def _position():
    return lax.axis_index("x"), lax.axis_index("y"), lax.axis_index("c")


def _slot(p):
    return 4 * p[0] + 2 * p[1] + p[2]


def all_gather_blocks(name, blocks):
    n = len(blocks)

    def body(*refs):
        ins, outs, token = refs[:n], refs[n:2 * n], refs[2 * n]
        send_sems, recv_sems, local_sems = refs[2 * n + 1:]
        token[...] = jnp.zeros_like(token)
        x, y, c = _position()
        me, sibling = (x, y, c), (x, y, 1 - c)
        chips = _other_chips(x, y)

        def copy(t, k, block, to, src=None):
            dst = outs[t].at[_slot(block)]
            return pltpu.make_async_remote_copy(src_ref=dst if src is None else src, dst_ref=dst, send_sem=send_sems.at[t, k],
                                                recv_sem=recv_sems.at[t, k], device_id=to, device_id_type=MESH)

        mine = [pltpu.make_async_copy(ins[t], outs[t].at[_slot(me)], local_sems.at[t]) for t in range(n)]
        for cp in mine:
            cp.start()
        sends = []
        for t in range(n):
            sends.append(copy(t, 0, me, sibling, src=ins[t]))
            sends += [copy(t, 1 + j, me, (*chip, c), src=ins[t]) for j, chip in enumerate(chips)]
        for cp in sends:
            cp.start()
        for t in range(n):
            for j, chip in enumerate(chips):
                copy(t, 1 + j, (*chip, c), me).wait_recv()
                passed = copy(t, 4 + j, (*chip, c), sibling)
                passed.start()
                sends.append(passed)
        for t in range(n):
            copy(t, 0, sibling, me).wait_recv()
            for j, chip in enumerate(chips):
                copy(t, 4 + j, (*chip, 1 - c), me).wait_recv()
        for cp in sends:
            cp.wait_send()
        for cp in mine:
            cp.wait()

    any_spec = pl.BlockSpec(memory_space=pl.ANY)
    return pl.pallas_call(
        body, name=name, in_specs=[any_spec] * n, out_specs=[any_spec] * n + [pl.BlockSpec(memory_space=pltpu.VMEM)],
        out_shape=[_sds((N_DEV,) + b.shape, b.dtype) for b in blocks] + [_sds((8, LANE), F32)],
        scratch_shapes=[pltpu.SemaphoreType.DMA((n, 7)), pltpu.SemaphoreType.DMA((n, 7)), pltpu.SemaphoreType.DMA((n,))],
        compiler_params=pltpu.CompilerParams(has_side_effects=True),
    )(*blocks)


_HBM = pl.BlockSpec(memory_space=pltpu.HBM)
_SEM = pl.BlockSpec(memory_space=pltpu.SEMAPHORE)
_EFFECT = pltpu.SideEffectType.DATAFLOW_SIDE_EFFECTING


def split_start(name, bufs, n_copies, build):
    n = len(bufs)

    def body(*refs):
        for cp in build(refs[:n], refs[n], refs[n + 1]):
            cp.start()
        refs[-1][...] = jnp.zeros_like(refs[-1])

    outs = pl.pallas_call(
        body, name=name, in_specs=[_HBM] * n,
        out_shape=(pltpu.SemaphoreType.DMA((n_copies,)), pltpu.SemaphoreType.DMA((n_copies,)),
                   *[pltpu.HBM(b.shape, b.dtype) for b in bufs], _sds((8, LANE), F32)),
        out_specs=(_SEM, _SEM, *[_HBM] * n, pl.BlockSpec(memory_space=pltpu.VMEM)),
        input_output_aliases={i: 2 + i for i in range(n)},
        compiler_params=pltpu.CompilerParams(has_side_effects=_EFFECT),
    )(*[pltpu.with_memory_space_constraint(b, pltpu.HBM) for b in bufs])
    return (outs[0], outs[1]), list(outs[2:2 + n]), outs[-1]


def split_wait(name, sems, bufs, build, after):
    n = len(bufs)

    def body(*refs):
        for cp in build(refs[:n], refs[n], refs[n + 1]):
            cp.wait()

    return list(pl.pallas_call(
        body, name=name, in_specs=[_HBM] * n + [_SEM, _SEM, pl.BlockSpec(memory_space=pl.ANY)],
        out_shape=[pltpu.HBM(b.shape, b.dtype) for b in bufs], out_specs=[_HBM] * n,
        input_output_aliases={i: i for i in range(n)},
        compiler_params=pltpu.CompilerParams(has_side_effects=_EFFECT),
    )(*bufs, sems[0], sems[1], after))


def _remote(src, dst, send_sems, recv_sems, k, to):
    return pltpu.make_async_remote_copy(src_ref=src, dst_ref=dst, send_sem=send_sems.at[k], recv_sem=recv_sems.at[k],
                                        device_id=to, device_id_type=MESH)


def _other_chips(x, y):
    return [(1 - x, y), (x, 1 - y), (1 - x, 1 - y)]


def _gather_cross(n):
    def build(refs, ss, rs):
        x, y, c = _position()
        me = _slot((x, y, c))
        targets = [(x, y, 1 - c)] + [(*chip, c) for chip in _other_chips(x, y)]
        return [_remote(refs[t], refs[n + t].at[me], ss, rs, 4 * t + k, to) for t in range(n) for k, to in enumerate(targets)]
    return build


def _gather_pass(n):
    def build(refs, ss, rs):
        x, y, c = _position()
        cps = []
        for t in range(n):
            for j, chip in enumerate(_other_chips(x, y)):
                blk = refs[t].at[_slot((*chip, c))]
                cps.append(_remote(blk, blk, ss, rs, 3 * t + j, (x, y, 1 - c)))
        return cps
    return build


def _reduce_pair(n):
    def build(refs, ss, rs):
        x, y, c = _position()
        return [_remote(refs[t].at[j, 1 - c], refs[n + t].at[j], ss, rs, 4 * t + j, (x, y, 1 - c))
                for t in range(n) for j in range(4)]
    return build


def _reduce_cross(n):
    def build(refs, ss, rs):
        x, y, c = _position()
        return [_remote(refs[t].at[2 * chip[0] + chip[1]], refs[n + t].at[r], ss, rs, 3 * t + r, (*chip, c))
                for t in range(n) for r, chip in enumerate(_other_chips(x, y))]
    return build


def pair_add(name, part, landed, tr=1024):
    _, _, R, C = part.shape
    tr = _row_tile(R, tr)

    def body(_, p_ref, l_ref, o_ref):
        o_ref[...] = (p_ref[...].astype(F32) + l_ref[...].astype(F32)).astype(o_ref.dtype)

    blk = pl.BlockSpec((None, tr, C), lambda j, i, core: (j, i, 0))
    core = lax.axis_index("c").astype(jnp.int32).reshape(1)
    return pl.pallas_call(
        body, name=name, out_shape=_sds((4, R, C), part.dtype), compiler_params=_cp(("parallel", "parallel")),
        grid_spec=pltpu.PrefetchScalarGridSpec(
            num_scalar_prefetch=1, grid=(4, R // tr),
            in_specs=[pl.BlockSpec((None, None, tr, C), lambda j, i, core: (j, core[0], i, 0)), blk], out_specs=blk),
    )(core, part, landed)


def _row_tile(R, pref):
    if R <= pref:
        return R
    t = (pref // 16) * 16
    while R % t:
        t -= 16
    return t


def all_reduce_small(name, vec, after):
    def body(v_ref, _, sum_ref, land_ref, send_sems, recv_sems):
        x, y, c = _position()
        me = (x, y, c)
        flip = lambda v, bit: 1 - v if bit else v
        peers = [(flip(x, k & 4), flip(y, k & 2), flip(c, k & 1)) for k in range(1, N_DEV)]
        land_ref[_slot(me)] = v_ref[...]
        sends = [pltpu.make_async_remote_copy(src_ref=v_ref, dst_ref=land_ref.at[_slot(me)], send_sem=send_sems.at[k],
                                              recv_sem=recv_sems.at[k], device_id=peer, device_id_type=MESH)
                 for k, peer in enumerate(peers)]
        for cp in sends:
            cp.start()
        for k, peer in enumerate(peers):
            pltpu.make_async_remote_copy(src_ref=v_ref, dst_ref=land_ref.at[_slot(peer)], send_sem=send_sems.at[k],
                                         recv_sem=recv_sems.at[k], device_id=peer, device_id_type=MESH).wait_recv()
        acc = land_ref[0]
        for s in range(1, N_DEV):
            acc = acc + land_ref[s]
        sum_ref[...] = acc
        for cp in sends:
            cp.wait_send()

    vm = pl.BlockSpec(memory_space=pltpu.VMEM)
    return pl.pallas_call(
        body, name=name, in_specs=[vm, pl.BlockSpec(memory_space=pl.ANY)], out_specs=[vm, vm],
        out_shape=[_sds(vec.shape, F32), _sds((N_DEV,) + vec.shape, F32)],
        scratch_shapes=[pltpu.SemaphoreType.DMA((7,)), pltpu.SemaphoreType.DMA((7,))],
        compiler_params=pltpu.CompilerParams(has_side_effects=True),
    )(vec, after)[0]


def _adamw_math(g, w, m, v):
    m2 = ADAM_B1 * m + (1.0 - ADAM_B1) * g
    v2 = ADAM_B2 * v + (1.0 - ADAM_B2) * (g * g)
    m_hat = m2 / (1.0 - ADAM_B1 ** ADAM_STEP)
    v_hat = v2 / (1.0 - ADAM_B2 ** ADAM_STEP)
    return -ADAM_LR * (m_hat / (jnp.sqrt(v_hat) + ADAM_EPS) + ADAM_WD * w), m2, v2


def adamw_shard(name, chip_parts, landed, w, m, v, layer, prev=None, tr=128):
    L, R, C = w.shape
    tr = _row_tile(R, tr)
    n_in = 5 + (4 if prev is not None else 0)

    def body(*refs):
        own_ref, land_ref, w_ref, m_ref, v_ref = refs[1:6]
        g_ref, d_ref, nm_ref, nv_ref = refs[1 + n_in:]
        g = own_ref[...].astype(F32)
        for s in range(landed.shape[0]):
            g = g + land_ref[s].astype(F32)
        g_ref[...] = g
        d_ref[...], nm_ref[...], nv_ref[...] = _adamw_math(g, w_ref[...], m_ref[...], v_ref[...])

    lay = pl.BlockSpec((None, tr, C), lambda i, chip: (layer, i, 0))
    own = pl.BlockSpec((None, tr, C), lambda i, chip: (chip[0], i, 0))
    chip = (2 * lax.axis_index("x") + lax.axis_index("y")).astype(jnp.int32).reshape(1)
    ins = [chip_parts, landed, w, m, v] + (list(prev) if prev is not None else [])
    return pl.pallas_call(
        body, name=name, out_shape=[_sds((L, R, C), F32)] * 4,
        grid_spec=pltpu.PrefetchScalarGridSpec(
            num_scalar_prefetch=1, grid=(R // tr,),
            in_specs=[own, pl.BlockSpec((landed.shape[0], tr, C), lambda i, chip: (0, i, 0)), lay, lay, lay]
            + [pl.BlockSpec(memory_space=pl.ANY)] * (n_in - 5),
            out_specs=[lay] * 4),
        input_output_aliases={6 + i: i for i in range(n_in - 5)}, compiler_params=_cp(("parallel",)),
    )(chip, *ins)


def adamw_small(name, g, w, m, v):
    def body(g_ref, w_ref, m_ref, v_ref, d_ref, nm_ref, nv_ref):
        d_ref[...], nm_ref[...], nv_ref[...] = _adamw_math(g_ref[...], w_ref[...], m_ref[...], v_ref[...])

    return pl.pallas_call(body, name=name, out_shape=[_sds(w.shape, F32)] * 3)(g, w, m, v)


def _pack(vs):
    flat = jnp.concatenate([v.reshape(-1) for v in vs])
    return flat.reshape(-1, LANE)


def _unpack(packed, like):
    flat, out, off = packed.reshape(-1), [], 0
    for v in like:
        out.append(flat[off:off + v.size].reshape(v.shape))
        off += v.size
    return out


class MeshWeights:
    def __init__(self, w_kv, w_out, pool_w_in, pool_w_grp, hgrn_w_in, hgrn_norm_g):
        self.n_grp, self.pg_loc, self.pg = pool_w_grp.shape[1:]
        wpin, hnorm, token = all_gather_blocks("gather_first", [pool_w_in[0].astype(COMM_DTYPE), hgrn_norm_g])
        self.first = (wpin, hnorm.reshape(1, -1))
        zero = token[0, 0]
        cast = lambda a: (a + zero).astype(COMM_DTYPE)
        self.gather, self.tokens = {}, []
        for layer, blocks in ((0, [cast(pool_w_grp[0]), cast(w_kv[0]), cast(w_out[0])]),
                              (1, [cast(hgrn_w_in[0])]), (2, [cast(w_kv[1]), cast(w_out[1])])):
            me = _slot(_position())
            lands = [lax.dynamic_update_index_in_dim(lax.empty((N_DEV,) + b.shape, b.dtype), b, me, 0) for b in blocks]
            n = len(blocks)
            sems, bufs, token = split_start(f"gather{layer}_cross", blocks + lands, 4 * n, _gather_cross(n))
            self.gather[layer] = (sems, bufs)
            self.tokens.append(token[0, 0])
        self.reduce = {}

    def first_weights(self):
        return self.first[0], self.first[1], self.tokens[0] + self.tokens[1] + self.tokens[2]

    def _pass_on(self, layer, after):
        sems, bufs = self.gather[layer]
        n = len(bufs) // 2
        gathered = split_wait(f"gather{layer}_cross_wait", sems, bufs, _gather_cross(n), after)[n:]
        sems, bufs, token = split_start(f"gather{layer}_pass", gathered, 3 * n, _gather_pass(n))
        self.gather[layer] = (sems, bufs)
        return token[0, 0]

    def _gathered(self, layer, after):
        sems, bufs = self.gather[layer]
        return split_wait(f"gather{layer}_pass_wait", sems, bufs, _gather_pass(len(bufs)), after)

    def after_proj0(self, after):
        return self._pass_on(0, after)

    def layer0_weights(self, after):
        wgrp, wkv, wout = self._gathered(0, after)
        rows = lambda g: g.reshape(-1, g.shape[-1])
        return wgrp.transpose(1, 0, 2, 3).reshape(self.n_grp, self.pg, self.pg), rows(wkv), rows(wout)

    def after_attn0(self, after):
        return self._pass_on(1, after)

    def layer1_weights(self, after):
        return self._gathered(1, after)[0]

    def after_proj1(self, after):
        return self._pass_on(2, after)

    def layer1_rest(self, after):
        wkv, wout = self._gathered(2, after)
        rows = lambda g: g.reshape(-1, g.shape[-1])
        return rows(wkv), rows(wout)

    def grads_ready(self, layer, grads):
        parts = {}
        for nm, g in grads.items():
            if nm == "pool_w_grp":
                g = g.reshape(self.n_grp, N_DEV, self.pg_loc, self.pg).transpose(1, 0, 2, 3)
            parts[nm] = g.reshape(4, 2, -1, g.shape[-1])
        names, srcs = list(parts), list(parts.values())
        lands = [lax.empty((4,) + p.shape[2:], p.dtype) for p in srcs]
        sems, bufs, token = split_start(f"reduce{layer}_pair", srcs + lands, 4 * len(srcs), _reduce_pair(len(srcs)))
        self.reduce[layer] = (names, sems, bufs)
        return token[0, 0]

    def grads_reduce(self, layer, after):
        names, sems, bufs = self.reduce[layer]
        n = len(names)
        bufs = split_wait(f"reduce{layer}_pair_wait", sems, bufs, _reduce_pair(n), after)
        chip_parts = [pair_add(f"pair_add_{nm}{layer}", bufs[t], bufs[n + t]) for t, nm in enumerate(names)]
        lands = [lax.empty((3,) + p.shape[1:], p.dtype) for p in chip_parts]
        sems, bufs, token = split_start(f"reduce{layer}_cross", chip_parts + lands, 3 * n, _reduce_cross(n))
        self.reduce[layer] = (names, sems, bufs)
        return token[0, 0]

    def grads_done(self, layer, after):
        names, sems, bufs = self.reduce[layer]
        n = len(names)
        bufs = split_wait(f"reduce{layer}_cross_wait", sems, bufs, _reduce_cross(n), after)
        return {nm: (bufs[t], bufs[n + t]) for t, nm in enumerate(names)}


def kernel(x, mem, norm_g, mem_norm_g, w_kv, w_out, pool_w_in, pool_w_grp, pool_scale, hgrn_w_in, hgrn_lb, hgrn_norm_g, final_g, loss_target, m_norm_g, m_mem_norm_g, m_w_kv, m_w_out, m_pool_w_in, m_pool_w_grp, m_pool_scale, m_hgrn_w_in, m_hgrn_lb, m_hgrn_norm_g, m_final_g, v_norm_g, v_mem_norm_g, v_w_kv, v_w_out, v_pool_w_in, v_pool_w_grp, v_pool_scale, v_hgrn_w_in, v_hgrn_lb, v_hgrn_norm_g, v_final_g):
    comm = MeshWeights(w_kv, w_out, pool_w_in, pool_w_grp, hgrn_w_in, hgrn_norm_g)
    loss, grad_x, small = run_step(x[0], mem[0], loss_target[0], norm_g, mem_norm_g, pool_scale, hgrn_lb, final_g, comm)
    loss = lax.psum(loss, AXES)
    rows = lambda a: a.reshape(a.shape[0], -1, a.shape[-1])
    state = dict(w_kv=(w_kv, m_w_kv, v_w_kv), w_out=(w_out, m_w_out, v_w_out), pool_w_in=(pool_w_in, m_pool_w_in, v_pool_w_in),
                 pool_w_grp=(pool_w_grp, m_pool_w_grp, v_pool_w_grp), hgrn_w_in=(hgrn_w_in, m_hgrn_w_in, v_hgrn_w_in))
    big = {}

    def update(group, layer, after):
        for nm, (chip_parts, landed) in comm.grads_done(group, after).items():
            w, m, v = state[nm]
            big[nm] = adamw_shard(f"adamw_{nm}{layer}", chip_parts, landed, rows(w), rows(m), rows(v),
                                  layer if w.shape[0] > 1 else 0, prev=big.get(nm))
            after = big[nm][0]
        return after

    after = update("0b", 0, update(1, 1, grad_x))
    names = ("norm_g", "mem_norm_g", "pool_scale", "hgrn_lb", "hgrn_norm_g", "final_g")
    like = (norm_g, mem_norm_g, pool_scale, hgrn_lb, _sds((1, N_DEV * hgrn_norm_g.shape[1]), F32), final_g)
    packed = _pack([small[nm].reshape(lk.shape) for nm, lk in zip(names, like)])
    reduced = _unpack(all_reduce_small("reduce_small", packed, after), like)
    n_loc = hgrn_norm_g.shape[1]
    reduced[4] = lax.dynamic_slice(reduced[4], (0, _slot(_position()) * n_loc), (1, n_loc))
    ws = (norm_g, mem_norm_g, pool_scale, hgrn_lb, hgrn_norm_g, final_g)
    ms = (m_norm_g, m_mem_norm_g, m_pool_scale, m_hgrn_lb, m_hgrn_norm_g, m_final_g)
    vs = (v_norm_g, v_mem_norm_g, v_pool_scale, v_hgrn_lb, v_hgrn_norm_g, v_final_g)
    res = adamw_small("adamw_small", _pack(reduced), _pack(ws), _pack(ms), _pack(vs))
    outs = {nm: [reduced[i]] + [_unpack(r, ws)[i] for r in res] for i, nm in enumerate(names)}
    update("0a", 0, res[0])
    for nm, (w, _, _) in state.items():
        outs[nm] = [r.reshape(w.shape) for r in big[nm]]
    order = ("norm_g", "mem_norm_g", "w_kv", "w_out", "pool_w_in", "pool_w_grp", "pool_scale", "hgrn_w_in", "hgrn_lb",
             "hgrn_norm_g", "final_g")
    return (loss, grad_x[None], *[outs[nm][0] for nm in order], *[outs[nm][1] for nm in order],
            *[outs[nm][2] for nm in order], *[outs[nm][3] for nm in order])
```

```python
import functools

import jax
import jax.numpy as jnp
from jax import lax
from jax.experimental import pallas as pl
from jax.experimental.pallas import tpu as pltpu

F32, BF16 = jnp.float32, jnp.bfloat16
MM_DTYPE = BF16
COMM_DTYPE = BF16
EPS = 1e-6
POOL_WINDOWS = (2, 4, 8, 16)
POOL_HALO = 16
HG_HEAD_DIM = 128
HG_CHUNK = 64
CA_HEADS = 4
N_DEV = 8
ADAM_LR, ADAM_B1, ADAM_B2, ADAM_EPS, ADAM_WD, ADAM_STEP = 0.001, 0.9, 0.999, 1e-08, 0.01, 10
VMEM_LIMIT = 48 << 20
VMEM_LIMIT_TALL = 56 << 20
LANE = 128
MESH = pl.DeviceIdType.MESH
AXES = ("x", "y", "c")

NN = (((1,), (0,)), ((), ()))
NT = (((1,), (1,)), ((), ()))
TN = (((0,), (0,)), ((), ()))


def _cp(sem=None, vmem=VMEM_LIMIT):
    return pltpu.CompilerParams(dimension_semantics=sem, vmem_limit_bytes=vmem)


def _dot(a, b, dims):
    return lax.dot_general(a.astype(MM_DTYPE), b.astype(MM_DTYPE), dims, preferred_element_type=F32)


def _sig(v):
    return 1.0 / (1.0 + jnp.exp(-v))


def _tile(n, pref):
    if n <= pref:
        return n
    t = (pref // LANE) * LANE
    while n % t:
        t -= LANE
    return t


def _sds(shape, dtype):
    return jax.ShapeDtypeStruct(tuple(shape), dtype)


def _mm(name, grid, sem, ins, in_specs, out_shapes, out_specs, dims, epi=None, nk=1, acc_shape=None, aliases=None,
        vmem=VMEM_LIMIT):
    n_in, n_out = len(ins), len(out_shapes)

    def body(*refs):
        a_ref, b_ref = refs[0], refs[1]
        extra = refs[2:n_in]
        outs = refs[n_in:n_in + n_out]
        p = _dot(a_ref[...], b_ref[...], dims)

        def finish(acc):
            res = epi(acc, *extra) if epi is not None else (acc,)
            for r, o_ref in zip(res, outs):
                o_ref[...] = r.astype(o_ref.dtype)

        if nk == 1:
            finish(p)
        else:
            acc_ref = refs[n_in + n_out]
            k = pl.program_id(len(grid) - 1)

            @pl.when(k == 0)
            def _():
                acc_ref[...] = p

            @pl.when(k > 0)
            def _():
                acc_ref[...] += p

            @pl.when(k == nk - 1)
            def _():
                finish(acc_ref[...])

    scratch = [pltpu.VMEM(acc_shape, F32)] if nk > 1 else []
    return pl.pallas_call(
        body, name=name, grid=grid, in_specs=in_specs, out_specs=out_specs, out_shape=out_shapes,
        scratch_shapes=scratch, compiler_params=_cp(sem, vmem), input_output_aliases=aliases or {},
    )(*ins)


def proj_fwd(name, h, wblk, tm=1024):
    T, D = h.shape
    nblk, _, nb = wblk.shape
    tm = _tile(T, tm)
    return _mm(name, (nblk, T // tm), ("parallel", "parallel"), [h, wblk],
               [pl.BlockSpec((tm, D), lambda j, i: (i, 0)), pl.BlockSpec((None, D, nb), lambda j, i: (j, 0, 0))],
               [_sds((T, nblk * nb), F32)], [pl.BlockSpec((tm, nb), lambda j, i: (i, j))], NN)[0]


def proj_bwd_w(name, h, dproj, nblk, tm=1024, tn=1024):
    T, D = h.shape
    N = dproj.shape[1]
    nb = N // nblk
    tm, tn = _tile(D, tm), _tile(nb, tn)
    per = nb // tn
    return _mm(name, (D // tm, N // tn), ("parallel", "parallel"), [h, dproj],
               [pl.BlockSpec((T, tm), lambda i, n: (0, i)), pl.BlockSpec((T, tn), lambda i, n: (0, n))],
               [_sds((nblk, D, nb), COMM_DTYPE)], [pl.BlockSpec((None, tm, tn), lambda i, n: (n // per, i, n % per))], TN)[0]


def proj_bwd_x(name, dproj, wblk, after, tm=1024):
    T, N = dproj.shape
    nblk, D, nb = wblk.shape
    tm = _tile(T, tm)
    return _mm(name, (T // tm, nblk), ("parallel", "arbitrary"), [dproj, wblk, jnp.reshape(after, (1, 1))],
               [pl.BlockSpec((tm, nb), lambda i, k: (i, k)), pl.BlockSpec((None, D, nb), lambda i, k: (k, 0, 0)),
                pl.BlockSpec(memory_space=pl.ANY)],
               [_sds((T, D), F32)], [pl.BlockSpec((tm, D), lambda i, k: (i, 0))], NT, nk=nblk, acc_shape=(tm, D),
               vmem=VMEM_LIMIT_TALL)[0]


def out_fwd(name, branch, wout, x, after, tm=1024, tn=512):
    T, E = branch.shape
    D = wout.shape[1]
    tm, tn = _tile(T, tm), _tile(D, tn)
    return _mm(name, (T // tm, D // tn), ("parallel", "parallel"), [branch, wout, x, jnp.reshape(after, (1, 1))],
               [pl.BlockSpec((tm, E), lambda i, j: (i, 0)), pl.BlockSpec((E, tn), lambda i, j: (0, j)),
                pl.BlockSpec((tm, tn), lambda i, j: (i, j)), pl.BlockSpec(memory_space=pl.ANY)],
               [_sds((T, D), F32)], [pl.BlockSpec((tm, tn), lambda i, j: (i, j))], NN,
               epi=lambda acc, x_ref, _: (acc + x_ref[...],))[0]


def mm_nt(name, a, b, out_dtype, tm=1024, tn=1024, res=None):
    M, K = a.shape
    N = b.shape[0]
    tm, tn = _tile(M, tm), _tile(N, tn)
    ins = [a, b] + ([res] if res is not None else [])
    specs = [pl.BlockSpec((tm, K), lambda i, j: (i, 0)), pl.BlockSpec((tn, K), lambda i, j: (j, 0))]
    if res is not None:
        specs.append(pl.BlockSpec((tm, tn), lambda i, j: (i, j)))
    epi = (lambda acc, r_ref: (acc + r_ref[...],)) if res is not None else None
    return _mm(name, (M // tm, N // tn), ("parallel", "parallel"), ins, specs,
               [_sds((M, N), out_dtype)], [pl.BlockSpec((tm, tn), lambda i, j: (i, j))], NT, epi=epi)[0]


def mm_nn(name, a, b, out_dtype, tm=512, tn=512):
    M, K = a.shape
    N = b.shape[1]
    tm, tn = _tile(M, tm), _tile(N, tn)
    return _mm(name, (M // tm, N // tn), ("parallel", "parallel"), [a, b],
               [pl.BlockSpec((tm, K), lambda i, j: (i, 0)), pl.BlockSpec((K, tn), lambda i, j: (0, j))],
               [_sds((M, N), out_dtype)], [pl.BlockSpec((tm, tn), lambda i, j: (i, j))], NN)[0]


def mm_tn(name, a, b, out_dtype, tm=512, tn=1024):
    K, M = a.shape
    N = b.shape[1]
    tm, tn = _tile(M, tm), _tile(N, tn)
    return _mm(name, (M // tm, N // tn), ("parallel", "parallel"), [a, b],
               [pl.BlockSpec((K, tm), lambda i, j: (0, i)), pl.BlockSpec((K, tn), lambda i, j: (0, j))],
               [_sds((M, N), out_dtype)], [pl.BlockSpec((tm, tn), lambda i, j: (i, j))], TN)[0]


def rmsnorm_fwd(name, x, g, tm=512):
    T, D = x.shape
    tm = _tile(T, tm)

    def body(x_ref, g_ref, o_ref):
        xf = x_ref[...]
        r = lax.rsqrt(jnp.mean(xf * xf, axis=-1, keepdims=True) + EPS)
        o_ref[...] = ((xf * r) * g_ref[...]).astype(o_ref.dtype)

    return pl.pallas_call(
        body, name=name, grid=(T // tm,),
        in_specs=[pl.BlockSpec((tm, D), lambda i: (i, 0)), pl.BlockSpec((1, D), lambda i: (0, 0))],
        out_specs=pl.BlockSpec((tm, D), lambda i: (i, 0)), out_shape=_sds((T, D), MM_DTYPE),
        compiler_params=_cp(("parallel",)),
    )(x, g.reshape(1, D))


def _rms_bwd_math(xf, g, dh):
    r = lax.rsqrt(jnp.mean(xf * xf, axis=-1, keepdims=True) + EPS)
    gd = dh * g
    dx = r * gd - xf * ((r * r * r) * jnp.mean(gd * xf, axis=-1, keepdims=True))
    dg = jnp.sum(dh * (xf * r), axis=0, keepdims=True)
    return dx, dg


def rmsnorm_bwd(name, x, g, dh, res=None, tm=256):
    T, D = x.shape
    tm = _tile(T, tm)
    has_res = res is not None

    def body(*refs):
        x_ref, g_ref, dh_ref = refs[:3]
        dx_ref, dxb_ref, dg_ref = refs[3 + has_res:]
        dx, dg = _rms_bwd_math(x_ref[...], g_ref[...], dh_ref[...])
        if has_res:
            dx = dx + refs[3][...]
        dx_ref[...] = dx
        dxb_ref[...] = dx.astype(dxb_ref.dtype)

        @pl.when(pl.program_id(0) == 0)
        def _():
            dg_ref[...] = jnp.zeros_like(dg_ref)

        dg_ref[...] += dg

    row = pl.BlockSpec((tm, D), lambda i: (i, 0))
    vec = pl.BlockSpec((1, D), lambda i: (0, 0))
    ins = [x, g.reshape(1, D), dh] + ([res] if has_res else [])
    return pl.pallas_call(
        body, name=name, grid=(T // tm,), in_specs=[row, vec, row] + ([row] if has_res else []),
        out_specs=[row, row, vec], out_shape=[_sds((T, D), F32), _sds((T, D), MM_DTYPE), _sds((1, D), F32)],
        compiler_params=_cp(("arbitrary",)),
    )(*ins)


def final_loss_bwd(name, x, g, target, tm=256):
    T, D = x.shape
    tm = _tile(T, tm)

    def body(x_ref, g_ref, t_ref, loss_ref, dx_ref, dxb_ref, dg_ref):
        xf, gv = x_ref[...], g_ref[...]
        r = lax.rsqrt(jnp.mean(xf * xf, axis=-1, keepdims=True) + EPS)
        err = (xf * r) * gv - t_ref[...]
        part = 0.5 * jnp.sum(jnp.mean(err * err, axis=-1, keepdims=True), axis=0, keepdims=True)
        dx, dg = _rms_bwd_math(xf, gv, err / D)
        dx_ref[...] = dx
        dxb_ref[...] = dx.astype(dxb_ref.dtype)

        @pl.when(pl.program_id(0) == 0)
        def _():
            dg_ref[...] = jnp.zeros_like(dg_ref)
            loss_ref[...] = jnp.zeros_like(loss_ref)

        dg_ref[...] += dg
        loss_ref[...] += jnp.broadcast_to(part, loss_ref.shape)

    row = pl.BlockSpec((tm, D), lambda i: (i, 0))
    vec = pl.BlockSpec((1, D), lambda i: (0, 0))
    return pl.pallas_call(
        body, name=name, grid=(T // tm,), in_specs=[row, vec, row],
        out_specs=[pl.BlockSpec((8, LANE), lambda i: (0, 0)), row, row, vec],
        out_shape=[_sds((8, LANE), F32), _sds((T, D), F32), _sds((T, D), MM_DTYPE), _sds((1, D), F32)],
        compiler_params=_cp(("arbitrary",)),
    )(x, g.reshape(1, D), target)


def _pool_tiles(T, emix):
    pg = emix // len(POOL_WINDOWS)
    tc = 256 if pg % 256 == 0 else LANE
    return pg, tc, _tile(T, 512)


def pool_fwd(name, proj, T, emix, after):
    pg, tc, R = _pool_tiles(T, emix)
    per_g = pg // tc

    def body(u_ref, _, o_ref):
        g = pl.program_id(0) // per_g
        for gi, w in enumerate(POOL_WINDOWS):
            @pl.when(g == gi)
            def _():
                for ci in range(T // R):
                    r0 = ci * R
                    cur = u_ref[r0:r0 + R, :]
                    halo = jnp.zeros((POOL_HALO, tc), F32) if ci == 0 else u_ref[r0 - POOL_HALO:r0, :]
                    s = jnp.concatenate([halo, cur], axis=0)
                    for st in range(w.bit_length() - 1):
                        s = s + pltpu.roll(s, 1 << st, axis=0)
                    t = r0 + lax.broadcasted_iota(jnp.int32, (R, 1), 0)
                    cnt = jnp.minimum(t + 1, w).astype(F32)
                    o_ref[r0:r0 + R, :] = (s[POOL_HALO:] / cnt - cur).astype(o_ref.dtype)

    return pl.pallas_call(
        body, name=name, grid=(emix // tc,),
        in_specs=[pl.BlockSpec((T, tc), lambda j: (0, j)), pl.BlockSpec(memory_space=pl.ANY)],
        out_specs=pl.BlockSpec((T, tc), lambda j: (0, j)), out_shape=_sds((T, emix), MM_DTYPE),
        compiler_params=_cp(("parallel",)),
    )(proj, jnp.reshape(after, (1, 1)))


def pool_bwd(name, dpooled, dproj, T, emix):
    pg, tc, R = _pool_tiles(T, emix)
    per_g = pg // tc

    def body(d_ref, _, o_ref):
        g = pl.program_id(0) // per_g
        for gi, w in enumerate(POOL_WINDOWS):
            @pl.when(g == gi)
            def _():
                n = R + POOL_HALO
                for ci in range(T // R):
                    r0 = ci * R
                    cur = d_ref[r0:r0 + R, :]
                    halo = jnp.zeros((POOL_HALO, tc), F32) if ci == T // R - 1 else d_ref[r0 + R:r0 + n, :]
                    t = r0 + lax.broadcasted_iota(jnp.int32, (n, 1), 0)
                    cnt = jnp.minimum(t + 1, w).astype(F32)
                    s = jnp.concatenate([cur, halo], axis=0) / cnt
                    for st in range(w.bit_length() - 1):
                        s = s + pltpu.roll(s, n - (1 << st), axis=0)
                    o_ref[r0:r0 + R, :] = (s[:R] - cur).astype(o_ref.dtype)

    return pl.pallas_call(
        body, name=name, grid=(emix // tc,),
        in_specs=[pl.BlockSpec((T, tc), lambda j: (0, j)), pl.BlockSpec(memory_space=pl.ANY)],
        out_specs=pl.BlockSpec((T, tc), lambda j: (0, j)), out_shape=_sds(dproj.shape, dproj.dtype),
        input_output_aliases={1: 0}, compiler_params=_cp(("parallel",)),
    )(dpooled, dproj)


def grp_fwd(name, pooled, wgrp, scale, proj, gate_off, E):
    T, emix = pooled.shape
    pg = emix // len(POOL_WINDOWS)
    tm = _tile(T, 2048)
    tc = 256 if pg % 256 == 0 and gate_off % 256 == 0 else LANE
    per = pg // tc

    def epi(acc, s_ref, g_ref):
        gt = g_ref[...]
        return acc, (acc * s_ref[...]) * (gt * _sig(gt))

    col = pl.BlockSpec((tm, tc), lambda n, i: (i, n))
    return _mm(name, (emix // tc, T // tm), ("parallel", "parallel"), [pooled, wgrp, scale, proj],
               [pl.BlockSpec((tm, pg), lambda n, i: (i, n // per)), pl.BlockSpec((None, pg, tc), lambda n, i: (n // per, 0, n % per)),
                pl.BlockSpec((1, tc), lambda n, i: (0, n)), pl.BlockSpec((tm, tc), lambda n, i: (i, gate_off // tc + n))],
               [_sds((T, emix), F32), _sds((T, E), MM_DTYPE)], [col, col], NN, epi=epi)


def grp_bwd_x(name, dy, wgrp):
    T, emix = dy.shape
    pg = emix // len(POOL_WINDOWS)
    tm = _tile(T, 1024)
    return _mm(name, (len(POOL_WINDOWS), T // tm), ("parallel", "parallel"), [dy, wgrp],
               [pl.BlockSpec((tm, pg), lambda g, i: (i, g)), pl.BlockSpec((None, pg, pg), lambda g, i: (g, 0, 0))],
               [_sds((T, emix), F32)], [pl.BlockSpec((tm, pg), lambda g, i: (i, g))], NT)[0]


def grp_bwd_w(name, pooled, dy):
    T, emix = dy.shape
    ng = len(POOL_WINDOWS)
    pg = emix // ng
    col = pl.BlockSpec((T, pg), lambda g: (0, g))
    return _mm(name, (ng,), ("parallel",), [pooled, dy], [col, col],
               [_sds((ng, pg, pg), COMM_DTYPE)], [pl.BlockSpec((None, pg, pg), lambda g: (g, 0, 0))], TN)[0]


def gate_bwd_pool(name, dbranch, y, scale, proj, gate_off, n_proj):
    T, emix = y.shape
    tm, tc = _tile(T, 1024), _tile(emix, 512)
    assert gate_off % tc == 0

    def body(db_ref, y_ref, s_ref, g_ref, dy_ref, dg_ref, ds_ref):
        db, yv, sc, gt = db_ref[...], y_ref[...], s_ref[...], g_ref[...]
        sg = _sig(gt)
        dmix = db * (gt * sg)
        dy_ref[...] = (dmix * sc).astype(dy_ref.dtype)
        dg_ref[...] = (db * (yv * sc) * (sg * (1.0 + gt * (1.0 - sg)))).astype(dg_ref.dtype)

        @pl.when(pl.program_id(1) == 0)
        def _():
            ds_ref[...] = jnp.zeros_like(ds_ref)

        ds_ref[...] += jnp.sum(dmix * yv, axis=0, keepdims=True)

    blk = pl.BlockSpec((tm, tc), lambda j, i: (i, j))
    vec = pl.BlockSpec((1, tc), lambda j, i: (0, j))
    gate = pl.BlockSpec((tm, tc), lambda j, i: (i, gate_off // tc + j))
    return pl.pallas_call(
        body, name=name, grid=(emix // tc, T // tm), in_specs=[blk, blk, vec, gate],
        out_specs=[blk, gate, vec],
        out_shape=[_sds((T, emix), MM_DTYPE), _sds((T, n_proj), MM_DTYPE), _sds((1, emix), F32)],
        compiler_params=_cp(("parallel", "arbitrary")),
    )(dbranch, y, scale, proj)


def attn_fwd(name, proj, kv, branch, q_off, gate_off, tm=1024):
    T = proj.shape[0]
    M, two_eca = kv.shape
    eca = two_eca // 2
    hd = eca // CA_HEADS
    E = branch.shape[1]
    tm = _tile(T, tm)
    scale = hd ** -0.5

    def body(q_ref, g_ref, k_ref, v_ref, _, ca_ref, br_ref):
        s = _dot(q_ref[...], k_ref[...], NT) * scale
        s = s - jnp.max(s, axis=-1, keepdims=True)
        e = jnp.exp(s)
        p = e / jnp.sum(e, axis=-1, keepdims=True)
        o = _dot(p, v_ref[...], NN)
        gt = g_ref[...]
        ca_ref[...] = o
        br_ref[...] = (o * (gt * _sig(gt))).astype(br_ref.dtype)

    return pl.pallas_call(
        body, name=name, grid=(CA_HEADS, T // tm),
        in_specs=[pl.BlockSpec((tm, hd), lambda h, i: (i, q_off // hd + h)),
                  pl.BlockSpec((tm, hd), lambda h, i: (i, (gate_off + E - eca) // hd + h)),
                  pl.BlockSpec((M, hd), lambda h, i: (0, h)), pl.BlockSpec((M, hd), lambda h, i: (0, CA_HEADS + h)),
                  pl.BlockSpec(memory_space=pl.ANY)],
        out_specs=[pl.BlockSpec((tm, hd), lambda h, i: (i, h)), pl.BlockSpec((tm, hd), lambda h, i: (i, (E - eca) // hd + h))],
        out_shape=[_sds((T, eca), F32), _sds(branch.shape, branch.dtype)],
        input_output_aliases={4: 1}, compiler_params=_cp(("parallel", "parallel")),
    )(proj, proj, kv, kv, branch)


def attn_bwd(name, proj, kv, ca, dbranch, dproj, q_off, gate_off, tm=1024):
    T = proj.shape[0]
    M, two_eca = kv.shape
    eca = two_eca // 2
    hd = eca // CA_HEADS
    E = dbranch.shape[1]
    tm = _tile(T, tm)
    scale = hd ** -0.5
    q_blk, g_blk = q_off // hd, (gate_off + E - eca) // hd

    def body(q_ref, g_ref, k_ref, v_ref, ca_ref, db_ref, _, dp_ref, dk_ref, dv_ref, dgate_buf):
        first_rows, part = pl.program_id(1) == 0, pl.program_id(2)

        @pl.when(part == 0)
        def _():
            q, k, v = q_ref[...], k_ref[...], v_ref[...]
            s = _dot(q, k, NT) * scale
            s = s - jnp.max(s, axis=-1, keepdims=True)
            e = jnp.exp(s)
            p = e / jnp.sum(e, axis=-1, keepdims=True)
            gt, db = g_ref[...], db_ref[...]
            sg = _sig(gt)
            do = db * (gt * sg)
            dgate_buf[...] = (db * ca_ref[...] * (sg * (1.0 + gt * (1.0 - sg)))).astype(dgate_buf.dtype)
            dpr = _dot(do, v, NT)
            ds = p * (dpr - jnp.sum(dpr * p, axis=-1, keepdims=True)) * scale
            dp_ref[...] = _dot(ds, k, NN).astype(dp_ref.dtype)

            @pl.when(first_rows)
            def _():
                dk_ref[...] = jnp.zeros_like(dk_ref)
                dv_ref[...] = jnp.zeros_like(dv_ref)

            dk_ref[...] += _dot(ds, q, TN)
            dv_ref[...] += _dot(p, do, TN)

        @pl.when(part == 1)
        def _():
            dp_ref[...] = dgate_buf[...]

    row = lambda blk: pl.BlockSpec((tm, hd), lambda h, i, p: (i, blk + h))
    out_dp = pl.BlockSpec((tm, hd), lambda h, i, p: (i, jnp.where(p == 0, q_blk, g_blk) + h))
    outs = pl.pallas_call(
        body, name=name, grid=(CA_HEADS, T // tm, 2),
        in_specs=[row(q_blk), row(g_blk), pl.BlockSpec((M, hd), lambda h, i, p: (0, h)),
                  pl.BlockSpec((M, hd), lambda h, i, p: (0, CA_HEADS + h)), row(0), row((E - eca) // hd),
                  pl.BlockSpec(memory_space=pl.ANY)],
        out_specs=[out_dp, pl.BlockSpec((M, hd), lambda h, i, p: (0, h)), pl.BlockSpec((M, hd), lambda h, i, p: (0, h))],
        out_shape=[_sds(dproj.shape, dproj.dtype), _sds((M, eca), F32), _sds((M, eca), F32)],
        scratch_shapes=[pltpu.VMEM((tm, hd), dproj.dtype)],
        input_output_aliases={6: 0}, compiler_params=_cp(("parallel", "arbitrary", "arbitrary")),
    )(proj, proj, kv, kv, ca, dbranch, dproj)
    return outs[0], jnp.concatenate([outs[1], outs[2]], axis=1)


def _split3(v):
    hi = v.astype(BF16)
    r1 = v - hi.astype(F32)
    mid = r1.astype(BF16)
    lo = (r1 - mid.astype(F32)).astype(BF16)
    return hi, mid, lo


def _tri_dot(tri, v):
    hi, mid, lo = _split3(v)
    d = functools.partial(lax.dot_general, dimension_numbers=NN, preferred_element_type=F32)
    return d(tri, hi) + d(tri, mid) + d(tri, lo)


def _causal_mask():
    C = HG_CHUNK
    return lax.broadcasted_iota(jnp.int32, (C, C), 0) >= lax.broadcasted_iota(jnp.int32, (C, C), 1)


def _cumsum_matrices(G):
    ri = lax.broadcasted_iota(jnp.int32, (G, G), 0)
    ci = lax.broadcasted_iota(jnp.int32, (G, G), 1)
    same = (ri // HG_CHUNK) == (ci // HG_CHUNK)
    return (jnp.where(same & (ri >= ci), 1.0, 0.0).astype(BF16), jnp.where(same & (ri <= ci), 1.0, 0.0).astype(BF16))


def _forget_gate(fi, lb):
    sig = _sig(fi)
    return sig, lb + (1.0 - lb) * sig


def _log_decay_cumsum(f_ref, b_s, lb, prefix, G):
    def group(i, carry):
        rows = pl.ds(pl.multiple_of(i * G, G), G)
        _, f = _forget_gate(f_ref[rows, :], lb)
        b_s[rows, :] = _tri_dot(prefix, jnp.log(f))
        return carry

    lax.fori_loop(0, b_s.shape[0] // G, group, 0)


def _lower_bound(lb2):
    l0, l1 = lb2[0:1, :], lb2[1:2, :]
    m = jnp.maximum(l0, l1)
    e0, e1 = jnp.exp(l0 - m), jnp.exp(l1 - m)
    sm0, sm1 = e0 / (e0 + e1), e1 / (e0 + e1)
    return (sm0 + sm1) - sm0, sm0, sm1


def _hg_chunk_fwd(qi, fi, lb, causal, b=None, prefix=None):
    sig, f = _forget_gate(fi, lb)
    k = 1.0 - f
    if b is None:
        b = _tri_dot(prefix, jnp.log(f))
    last = lax.broadcasted_iota(jnp.int32, b.shape, 0) == HG_CHUNK - 1
    bl = jnp.sum(jnp.where(last, b, 0.0), axis=0, keepdims=True)
    eb, enb, ete = jnp.exp(b), jnp.exp(-b), jnp.exp(bl - b)
    sq = _sig(qi)
    q = (qi * sq) * (HG_HEAD_DIM ** -0.5)
    q_dec, k_inv, k_te = q * eb, k * enb, k * ete
    att = jnp.where(causal, _dot(q_dec, k_inv, NT), 0.0)
    return dict(sig=sig, f=f, k=k, eb=eb, enb=enb, ete=ete, dcy=jnp.exp(bl), sq=sq, q_dec=q_dec, k_inv=k_inv, k_te=k_te,
                att=att, last=last)


def _hg_blocking(H, T):
    hb = 4 if H % 4 == 0 else (2 if H % 2 == 0 else 1)
    return hb, _tile(T, 1024)


def hgrn_fwd(name, proj, lb2, norm_g, emix, gate_off, E):
    T = proj.shape[0]
    hd, C = HG_HEAD_DIM, HG_CHUNK
    H, N = emix // hd, T // C
    hb, ts = _hg_blocking(H, T)
    W, nc, G = hb * hd, ts // C, _tile(ts, 256)

    def body(q_ref, f_ref, i_ref, g_ref, lb_ref, ng_ref, o_ref, st_ref, br_ref, state, b_s):
        @pl.when(pl.program_id(1) == 0)
        def _():
            state[...] = jnp.zeros_like(state)

        causal = _causal_mask()
        lb, _, _ = _lower_bound(lb_ref[...])
        ng = ng_ref[...]
        _log_decay_cumsum(f_ref, b_s, lb, _cumsum_matrices(G)[0], G)

        def chunk(n, carry):
            rows = pl.ds(pl.multiple_of(n * C, C), C)
            for h in range(hb):
                cols = slice(h * hd, (h + 1) * hd)
                v = i_ref[rows, cols]
                c = _hg_chunk_fwd(q_ref[rows, cols], f_ref[rows, cols], lb[:, cols], causal, b=b_s[rows, cols])
                st = state[h]
                o = _dot(c["att"], v, NN) + _dot(c["q_dec"], st, NT)
                st_ref[h, n] = st
                o_ref[rows, cols] = o
                r = lax.rsqrt(jnp.mean(o * o, axis=-1, keepdims=True) + EPS)
                gt = g_ref[rows, cols]
                br_ref[rows, cols] = (((o * r) * ng[:, cols]) * (gt * _sig(gt))).astype(br_ref.dtype)
                state[h] = st * c["dcy"] + _dot(v, c["k_te"], TN)
            return carry

        lax.fori_loop(0, nc, chunk, 0, unroll=8)

    col = lambda off: pl.BlockSpec((ts, W), lambda h, s: (s, off // W + h))
    vec = lambda r: pl.BlockSpec((r, W), lambda h, s: (0, h))
    return pl.pallas_call(
        body, name=name, grid=(H // hb, T // ts),
        in_specs=[col(0), col(emix), col(2 * emix), col(gate_off), vec(2), vec(1)],
        out_specs=[col(0), pl.BlockSpec((hb, nc, hd, hd), lambda h, s: (h, s, 0, 0)), col(0)],
        out_shape=[_sds((T, emix), F32), _sds((H, N, hd, hd), F32), _sds((T, E), MM_DTYPE)],
        scratch_shapes=[pltpu.VMEM((hb, hd, hd), F32), pltpu.VMEM((ts, W), F32)],
        compiler_params=_cp(("parallel", "arbitrary")),
    )(proj, proj, proj, proj, lb2, norm_g)


def hgrn_bwd(name, proj, lb2, norm_g, o_pre, states, dbranch, emix, gate_off):
    T, n_proj = proj.shape
    hd, C = HG_HEAD_DIM, HG_CHUNK
    H, N = emix // hd, T // C
    hb, ts = _hg_blocking(H, T)
    W, nc, ns = hb * hd, ts // C, T // ts
    qscale = HG_HEAD_DIM ** -0.5

    def body(q_ref, f_ref, i_ref, g_ref, lb_ref, ng_ref, o_ref, st_ref, db_ref, dp_ref, dlb_ref, dng_ref, stash, dst_ref, acc):
        first_block, part = pl.program_id(1) == 0, pl.program_id(2)

        @pl.when(part == 0)
        def _():
            @pl.when(first_block)
            def _():
                dst_ref[...] = jnp.zeros_like(dst_ref)
                acc[...] = jnp.zeros_like(acc)

            causal = _causal_mask()
            prefix, suffix = _cumsum_matrices(C)
            lb, sm0, sm1 = _lower_bound(lb_ref[...])
            ng = ng_ref[...]

            def chunk(j, carry):
                n = nc - 1 - j
                rows = pl.ds(pl.multiple_of(n * C, C), C)
                for h in range(hb):
                    cols = slice(h * hd, (h + 1) * hd)
                    dst = dst_ref[h]
                    qi, v, gt = q_ref[rows, cols], i_ref[rows, cols], g_ref[rows, cols]
                    c = _hg_chunk_fwd(qi, f_ref[rows, cols], lb[:, cols], causal, prefix=prefix)
                    st = st_ref[h, n]
                    o, db, ngh = o_ref[rows, cols], db_ref[rows, cols], ng[:, cols]
                    r = lax.rsqrt(jnp.mean(o * o, axis=-1, keepdims=True) + EPS)
                    sg = _sig(gt)
                    dmix = db * (gt * sg)
                    dgate = db * ((o * r) * ngh) * (sg * (1.0 + gt * (1.0 - sg)))
                    dyn = dmix * ngh
                    do = r * dyn - o * ((r * r * r) * jnp.mean(dyn * o, axis=-1, keepdims=True))
                    acc[1:2, cols] += jnp.sum(dmix * (o * r), axis=0, keepdims=True)
                    datt = jnp.where(causal, _dot(do, v, NT), 0.0)
                    dq_dec = _dot(datt, c["k_inv"], NN) + _dot(do, st, NN)
                    dk_inv = _dot(datt, c["q_dec"], TN)
                    dv = _dot(c["att"], do, TN) + _dot(c["k_te"], dst, NT)
                    dk_te = _dot(v, dst, NN)
                    ddcy = jnp.sum(dst * st, axis=0, keepdims=True)
                    dst_ref[h] = _dot(do, c["q_dec"], TN) + dst * c["dcy"]
                    t_te = dk_te * c["k_te"]
                    dbv = dq_dec * c["q_dec"] - dk_inv * c["k_inv"] - t_te
                    dbl = jnp.sum(t_te, axis=0, keepdims=True) + ddcy * c["dcy"]
                    dbv = dbv + jnp.where(c["last"], dbl, 0.0)
                    dlf = _tri_dot(suffix, dbv)
                    dk = dk_inv * c["enb"] + dk_te * c["ete"]
                    df = dlf / c["f"] - dk
                    sig, sq = c["sig"], c["sq"]
                    acc[0:1, cols] += jnp.sum(df * (1.0 - sig), axis=0, keepdims=True)
                    dqi = (dq_dec * c["eb"]) * qscale * (sq * (1.0 + qi * (1.0 - sq)))
                    dp_ref[rows, cols] = dqi.astype(dp_ref.dtype)
                    stash[0, rows, cols] = (df * (1.0 - lb[:, cols]) * (sig * (1.0 - sig))).astype(stash.dtype)
                    stash[1, rows, cols] = dv.astype(stash.dtype)
                    stash[2, rows, cols] = dgate.astype(stash.dtype)
                return carry

            lax.fori_loop(0, nc, chunk, 0, unroll=8)
            t = acc[0:1, :] * (sm0 * sm1)
            dlb_ref[...] = jnp.concatenate([-t, t], axis=0)
            dng_ref[...] = acc[1:2, :]

        for pi in range(3):
            @pl.when(part == pi + 1)
            def _():
                dp_ref[...] = stash[pi]

    col = lambda off: pl.BlockSpec((ts, W), lambda h, s, p: (ns - 1 - s, off // W + h))
    vec = lambda r: pl.BlockSpec((r, W), lambda h, s, p: (0, h))
    n_mix = emix // W
    out_col = pl.BlockSpec((ts, W), lambda h, s, p: (ns - 1 - s, jnp.where(p < 3, p * n_mix, gate_off // W) + h))
    return pl.pallas_call(
        body, name=name, grid=(H // hb, ns, 4),
        in_specs=[col(0), col(emix), col(2 * emix), col(gate_off), vec(2), vec(1), col(0),
                  pl.BlockSpec((hb, nc, hd, hd), lambda h, s, p: (h, ns - 1 - s, 0, 0)), col(0)],
        out_specs=[out_col, vec(2), vec(1)],
        out_shape=[_sds((T, n_proj), MM_DTYPE), _sds((2, emix), F32), _sds((1, emix), F32)],
        scratch_shapes=[pltpu.VMEM((3, ts, W), MM_DTYPE), pltpu.VMEM((hb, hd, hd), F32), pltpu.VMEM((8, W), F32)],
        compiler_params=_cp(("parallel", "arbitrary", "arbitrary")),
    )(proj, proj, proj, proj, lb2, norm_g, o_pre, states, dbranch)


class WholeWeights:
    def __init__(self, wkv, wout, wpin, wgrp, whin, hgrn_norm_g):
        self.w = dict(wkv=wkv, wout=wout, wpin=wpin, wgrp=wgrp, whin=whin, hnorm=hgrn_norm_g)
        self.grads = {}

    def first_weights(self):
        return self.w["wpin"], self.w["hnorm"], 0.0

    def after_proj0(self, after):
        return 0.0

    def layer0_weights(self, after):
        return self.w["wgrp"], self.w["wkv"][0], self.w["wout"][0]

    def after_out0(self, after):
        return 0.0

    def layer1_weights(self, after):
        return self.w["whin"]

    def after_proj1(self, after):
        return 0.0

    def layer1_rest(self, after):
        return self.w["wkv"][1], self.w["wout"][1]

    def grads_ready(self, layer, grads):
        self.grads[layer] = grads
        return 0.0

    def grads_reduce(self, layer, after):
        return 0.0


def run_step(x, mem, target, norm_g, mem_norm_g, pool_scale, hgrn_lb, final_g, comm):
    T, D = x.shape
    mem_n = rmsnorm_fwd("mem_norm", mem, mem_norm_g)
    wpin, hgrn_norm_g, zero = comm.first_weights()
    h0 = rmsnorm_fwd("norm0", x, norm_g[0] + zero)
    proj0 = proj_fwd("proj0", h0, wpin)
    zero = comm.after_proj0(proj0)
    n0 = proj0.shape[1]
    E = n0 // 2
    eca = E // 4
    emix = E - eca
    q0, g0 = emix, emix + eca
    q1, g1 = 3 * emix, 3 * emix + eca
    pooled = pool_fwd("pool_fwd", proj0, T, emix, zero)
    wgrp, wkv0, wout0 = comm.layer0_weights(pooled)
    y0, branch0 = grp_fwd("grp_fwd", pooled, wgrp, pool_scale, proj0, g0, E)
    kv0 = mm_nn("kv0", mem_n, wkv0, MM_DTYPE)
    ca0, branch0 = attn_fwd("attn_fwd0", proj0, kv0, branch0, q0, g0)
    x1 = out_fwd("out0", branch0, wout0, x, ca0[:1, :1])
    h1 = rmsnorm_fwd("norm1", x1, norm_g[1] + comm.after_out0(x1))
    whin = comm.layer1_weights(h1)
    proj1 = proj_fwd("proj1", h1, whin)
    zero = comm.after_proj1(proj1)
    o1, states, branch1 = hgrn_fwd("hgrn_fwd", proj1, hgrn_lb + zero, hgrn_norm_g, emix, g1, E)
    wkv1, wout1 = comm.layer1_rest(o1)
    kv1 = mm_nn("kv1", mem_n, wkv1, MM_DTYPE)
    ca1, branch1 = attn_fwd("attn_fwd1", proj1, kv1, branch1, q1, g1)
    x2 = out_fwd("out1", branch1, wout1, x1, ca1[:1, :1])
    loss, dx2, dx2b, g_final = final_loss_bwd("final", x2, final_g, target)
    g_wout1 = mm_tn("gwout1", branch1, dx2b, COMM_DTYPE)
    dbranch1 = mm_nt("dbranch1", dx2b, wout1, F32)
    dproj1, g_lb, g_hnorm = hgrn_bwd("hgrn_bwd", proj1, hgrn_lb, hgrn_norm_g, o1, states, dbranch1, emix, g1)
    dproj1, dkv1 = attn_bwd("attn_bwd1", proj1, kv1, ca1, dbranch1, dproj1, q1, g1)
    g_wkv1 = mm_tn("gwkv1", mem_n, dkv1, COMM_DTYPE)
    dmem_n = mm_nt("dmem1", dkv1, wkv1, F32)
    g_whin = proj_bwd_w("gwhin", h1, dproj1, whin.shape[0])
    zero = comm.grads_ready(1, dict(w_kv=g_wkv1, w_out=g_wout1, hgrn_w_in=g_whin))
    dh1 = proj_bwd_x("dh1", dproj1, whin, zero)
    zero = zero + comm.grads_reduce(1, dh1)
    dx1, dx1b, g_norm1 = rmsnorm_bwd("norm1_bwd", x1, norm_g[1] + zero, dh1, res=dx2)
    dbranch0 = mm_nt("dbranch0", dx1b, wout0, F32)
    dy0, dproj0, g_scale = gate_bwd_pool("gate_bwd0", dbranch0, y0, pool_scale, proj0, g0, n0)
    dpooled = grp_bwd_x("dpooled", dy0, wgrp)
    dproj0 = pool_bwd("pool_bwd", dpooled, dproj0, T, emix)
    dproj0, dkv0 = attn_bwd("attn_bwd0", proj0, kv0, ca0, dbranch0, dproj0, q0, g0)
    g_wpin = proj_bwd_w("gwpin", h0, dproj0, wpin.shape[0])
    dkv0 = dkv0 + comm.grads_ready("0b", dict(pool_w_in=g_wpin))
    g_wkv0 = mm_tn("gwkv0", mem_n, dkv0, COMM_DTYPE)
    dmem_n = mm_nt("dmem0", dkv0, wkv0, F32, res=dmem_n)
    g_wout0 = mm_tn("gwout0", branch0, dx1b, COMM_DTYPE)
    g_wgrp = grp_bwd_w("gwgrp", pooled, dy0)
    zero = comm.grads_reduce("0b", g_wgrp)
    zero = zero + comm.grads_ready("0a", dict(w_kv=g_wkv0, w_out=g_wout0, pool_w_grp=g_wgrp))
    dh0 = proj_bwd_x("dh0", dproj0, wpin, zero)
    zero = zero + comm.grads_reduce("0a", dh0)
    grad_x, _, g_norm0 = rmsnorm_bwd("norm0_bwd", x, norm_g[0] + zero, dh0, res=dx1)
    _, _, g_mem = rmsnorm_bwd("mem_norm_bwd", mem, mem_norm_g, dmem_n)

    small = dict(norm_g=jnp.concatenate([g_norm0, g_norm1], axis=0), mem_norm_g=g_mem[0], pool_scale=g_scale,
                 hgrn_lb=g_lb, hgrn_norm_g=g_hnorm, final_g=g_final[0])
    return loss[0, 0], grad_x, small


def _position():
    return lax.axis_index("x"), lax.axis_index("y"), lax.axis_index("c")


def _slot(p):
    return 4 * p[0] + 2 * p[1] + p[2]


def all_gather_blocks(name, blocks):
    n = len(blocks)
    halved = [b.shape[0] % 32 == 0 for b in blocks]

    def body(*refs):
        ins, outs, token = refs[:n], refs[n:2 * n], refs[2 * n]
        send_sems, recv_sems, local_sems = refs[2 * n + 1:]
        token[...] = jnp.zeros_like(token)
        x, y, c = _position()
        me, sibling = (x, y, c), (x, y, 1 - c)
        x_nbr, y_nbr, diag = _other_chips(x, y)

        def copy(t, k, block, to, src=None, rows=None):
            dst = outs[t].at[_slot(block)]
            if rows is not None:
                dst = dst.at[rows]
            return pltpu.make_async_remote_copy(src_ref=dst if src is None else src, dst_ref=dst, send_sem=send_sems.at[t, k],
                                                recv_sem=recv_sems.at[t, k], device_id=to, device_id_type=MESH)

        mine = [pltpu.make_async_copy(ins[t], outs[t].at[_slot(me)], local_sems.at[t]) for t in range(n)]
        for cp in mine:
            cp.start()
        sends = []
        for t in range(n):
            sends += [copy(t, 0, me, sibling, src=ins[t]), copy(t, 1, me, (*x_nbr, c), src=ins[t]),
                      copy(t, 2, me, (*y_nbr, c), src=ins[t])]
            if not halved[t]:
                sends.append(copy(t, 4, me, (*diag, c), src=ins[t]))
        for cp in sends:
            cp.start()

        def start(cp):
            cp.start()
            sends.append(cp)

        for t in range(n):
            half = blocks[t].shape[0] // 2
            top, bottom = pl.ds(0, half), pl.ds(half, half)
            copy(t, 1, (*x_nbr, c), me).wait_recv()
            if halved[t]:
                start(copy(t, 4, (*x_nbr, c), (*y_nbr, c), rows=top))
            start(copy(t, 3, (*x_nbr, c), sibling))
            copy(t, 2, (*y_nbr, c), me).wait_recv()
            if halved[t]:
                start(copy(t, 6, (*y_nbr, c), (*x_nbr, c), rows=bottom))
            start(copy(t, 5, (*y_nbr, c), sibling))
            if halved[t]:
                copy(t, 4, (*diag, c), me, rows=top).wait_recv()
                copy(t, 6, (*diag, c), me, rows=bottom).wait_recv()
            else:
                copy(t, 4, (*diag, c), me).wait_recv()
            start(copy(t, 7, (*diag, c), sibling))
        for t in range(n):
            copy(t, 0, sibling, me).wait_recv()
            for k, chip in ((3, x_nbr), (5, y_nbr), (7, diag)):
                copy(t, k, (*chip, 1 - c), me).wait_recv()
        for cp in sends:
            cp.wait_send()
        for cp in mine:
            cp.wait()

    any_spec = pl.BlockSpec(memory_space=pl.ANY)
    return pl.pallas_call(
        body, name=name, in_specs=[any_spec] * n, out_specs=[any_spec] * n + [pl.BlockSpec(memory_space=pltpu.VMEM)],
        out_shape=[_sds((N_DEV,) + b.shape, b.dtype) for b in blocks] + [_sds((8, LANE), F32)],
        scratch_shapes=[pltpu.SemaphoreType.DMA((n, 8)), pltpu.SemaphoreType.DMA((n, 8)), pltpu.SemaphoreType.DMA((n,))],
        compiler_params=pltpu.CompilerParams(has_side_effects=True),
    )(*blocks)


_HBM = pl.BlockSpec(memory_space=pltpu.HBM)
_SEM = pl.BlockSpec(memory_space=pltpu.SEMAPHORE)
_EFFECT = pltpu.SideEffectType.DATAFLOW_SIDE_EFFECTING


def split_start(name, bufs, n_copies, build):
    n = len(bufs)

    def body(*refs):
        for cp in build(refs[:n], refs[n], refs[n + 1]):
            cp.start()
        refs[-1][...] = jnp.zeros_like(refs[-1])

    outs = pl.pallas_call(
        body, name=name, in_specs=[_HBM] * n,
        out_shape=(pltpu.SemaphoreType.DMA((n_copies,)), pltpu.SemaphoreType.DMA((n_copies,)),
                   *[pltpu.HBM(b.shape, b.dtype) for b in bufs], _sds((8, LANE), F32)),
        out_specs=(_SEM, _SEM, *[_HBM] * n, pl.BlockSpec(memory_space=pltpu.VMEM)),
        input_output_aliases={i: 2 + i for i in range(n)},
        compiler_params=pltpu.CompilerParams(has_side_effects=_EFFECT),
    )(*[pltpu.with_memory_space_constraint(b, pltpu.HBM) for b in bufs])
    return (outs[0], outs[1]), list(outs[2:2 + n]), outs[-1]


def split_wait(name, sems, bufs, build, after):
    n = len(bufs)

    def body(*refs):
        for cp in build(refs[:n], refs[n], refs[n + 1]):
            cp.wait()

    return list(pl.pallas_call(
        body, name=name, in_specs=[_HBM] * n + [_SEM, _SEM, pl.BlockSpec(memory_space=pl.ANY)],
        out_shape=[pltpu.HBM(b.shape, b.dtype) for b in bufs], out_specs=[_HBM] * n,
        input_output_aliases={i: i for i in range(n)},
        compiler_params=pltpu.CompilerParams(has_side_effects=_EFFECT),
    )(*bufs, sems[0], sems[1], after))


def _remote(src, dst, send_sems, recv_sems, k, to):
    return pltpu.make_async_remote_copy(src_ref=src, dst_ref=dst, send_sem=send_sems.at[k], recv_sem=recv_sems.at[k],
                                        device_id=to, device_id_type=MESH)


def _other_chips(x, y):
    return [(1 - x, y), (x, 1 - y), (1 - x, 1 - y)]


def _gather_cross(n):
    def build(refs, ss, rs):
        x, y, c = _position()
        me = _slot((x, y, c))
        targets = [(x, y, 1 - c)] + [(*chip, c) for chip in _other_chips(x, y)]
        return [_remote(refs[t], refs[n + t].at[me], ss, rs, 4 * t + k, to) for t in range(n) for k, to in enumerate(targets)]
    return build


def _gather_pass(n):
    def build(refs, ss, rs):
        x, y, c = _position()
        cps = []
        for t in range(n):
            for j, chip in enumerate(_other_chips(x, y)):
                blk = refs[t].at[_slot((*chip, c))]
                cps.append(_remote(blk, blk, ss, rs, 3 * t + j, (x, y, 1 - c)))
        return cps
    return build


def _reduce_pair(n):
    def build(refs, ss, rs):
        x, y, c = _position()
        return [_remote(refs[t].at[j, 1 - c], refs[n + t].at[j], ss, rs, 4 * t + j, (x, y, 1 - c))
                for t in range(n) for j in range(4)]
    return build


def _reduce_cross(n):
    def build(refs, ss, rs):
        x, y, c = _position()
        return [_remote(refs[t].at[2 * chip[0] + chip[1]], refs[n + t].at[r], ss, rs, 3 * t + r, (*chip, c))
                for t in range(n) for r, chip in enumerate(_other_chips(x, y))]
    return build


def pair_add(name, part, landed, tr=1024):
    _, _, R, C = part.shape
    tr = _row_tile(R, tr)

    def body(_, p_ref, l_ref, o_ref):
        o_ref[...] = (p_ref[...].astype(F32) + l_ref[...].astype(F32)).astype(o_ref.dtype)

    blk = pl.BlockSpec((None, tr, C), lambda j, i, core: (j, i, 0))
    core = lax.axis_index("c").astype(jnp.int32).reshape(1)
    return pl.pallas_call(
        body, name=name, out_shape=_sds((4, R, C), part.dtype), compiler_params=_cp(("parallel", "parallel")),
        grid_spec=pltpu.PrefetchScalarGridSpec(
            num_scalar_prefetch=1, grid=(4, R // tr),
            in_specs=[pl.BlockSpec((None, None, tr, C), lambda j, i, core: (j, core[0], i, 0)), blk], out_specs=blk),
    )(core, part, landed)


def _row_tile(R, pref):
    if R <= pref:
        return R
    t = (pref // 16) * 16
    while R % t:
        t -= 16
    return t


def all_reduce_small(name, vec, after):
    def body(v_ref, _, sum_ref, land_ref, send_sems, recv_sems):
        x, y, c = _position()
        me = (x, y, c)
        flip = lambda v, bit: 1 - v if bit else v
        peers = [(flip(x, k & 4), flip(y, k & 2), flip(c, k & 1)) for k in range(1, N_DEV)]
        land_ref[_slot(me)] = v_ref[...]
        sends = [pltpu.make_async_remote_copy(src_ref=v_ref, dst_ref=land_ref.at[_slot(me)], send_sem=send_sems.at[k],
                                              recv_sem=recv_sems.at[k], device_id=peer, device_id_type=MESH)
                 for k, peer in enumerate(peers)]
        for cp in sends:
            cp.start()
        for k, peer in enumerate(peers):
            pltpu.make_async_remote_copy(src_ref=v_ref, dst_ref=land_ref.at[_slot(peer)], send_sem=send_sems.at[k],
                                         recv_sem=recv_sems.at[k], device_id=peer, device_id_type=MESH).wait_recv()
        acc = land_ref[0]
        for s in range(1, N_DEV):
            acc = acc + land_ref[s]
        sum_ref[...] = acc
        for cp in sends:
            cp.wait_send()

    vm = pl.BlockSpec(memory_space=pltpu.VMEM)
    return pl.pallas_call(
        body, name=name, in_specs=[vm, pl.BlockSpec(memory_space=pl.ANY)], out_specs=[vm, vm],
        out_shape=[_sds(vec.shape, F32), _sds((N_DEV,) + vec.shape, F32)],
        scratch_shapes=[pltpu.SemaphoreType.DMA((7,)), pltpu.SemaphoreType.DMA((7,))],
        compiler_params=pltpu.CompilerParams(has_side_effects=True),
    )(vec, after)[0]


def _adamw_math(g, w, m, v):
    m2 = ADAM_B1 * m + (1.0 - ADAM_B1) * g
    v2 = ADAM_B2 * v + (1.0 - ADAM_B2) * (g * g)
    m_hat = m2 / (1.0 - ADAM_B1 ** ADAM_STEP)
    v_hat = v2 / (1.0 - ADAM_B2 ** ADAM_STEP)
    return -ADAM_LR * (m_hat / (jnp.sqrt(v_hat) + ADAM_EPS) + ADAM_WD * w), m2, v2


def adamw_shard(name, chip_parts, landed, w, m, v, layer, prev=None, tr=128):
    L, R, C = w.shape
    tr = _row_tile(R, tr)
    n_in = 5 + (4 if prev is not None else 0)

    def body(*refs):
        own_ref, land_ref, w_ref, m_ref, v_ref = refs[1:6]
        g_ref, d_ref, nm_ref, nv_ref = refs[1 + n_in:]
        g = own_ref[...].astype(F32)
        for s in range(landed.shape[0]):
            g = g + land_ref[s].astype(F32)
        g_ref[...] = g
        d_ref[...], nm_ref[...], nv_ref[...] = _adamw_math(g, w_ref[...], m_ref[...], v_ref[...])

    lay = pl.BlockSpec((None, tr, C), lambda i, chip: (layer, i, 0))
    own = pl.BlockSpec((None, tr, C), lambda i, chip: (chip[0], i, 0))
    chip = (2 * lax.axis_index("x") + lax.axis_index("y")).astype(jnp.int32).reshape(1)
    ins = [chip_parts, landed, w, m, v] + (list(prev) if prev is not None else [])
    return pl.pallas_call(
        body, name=name, out_shape=[_sds((L, R, C), F32)] * 4,
        grid_spec=pltpu.PrefetchScalarGridSpec(
            num_scalar_prefetch=1, grid=(R // tr,),
            in_specs=[own, pl.BlockSpec((landed.shape[0], tr, C), lambda i, chip: (0, i, 0)), lay, lay, lay]
            + [pl.BlockSpec(memory_space=pl.ANY)] * (n_in - 5),
            out_specs=[lay] * 4),
        input_output_aliases={6 + i: i for i in range(n_in - 5)}, compiler_params=_cp(("parallel",)),
    )(chip, *ins)


def adamw_small(name, g, w, m, v):
    def body(g_ref, w_ref, m_ref, v_ref, d_ref, nm_ref, nv_ref):
        d_ref[...], nm_ref[...], nv_ref[...] = _adamw_math(g_ref[...], w_ref[...], m_ref[...], v_ref[...])

    return pl.pallas_call(body, name=name, out_shape=[_sds(w.shape, F32)] * 3)(g, w, m, v)


def _pack(vs):
    flat = jnp.concatenate([v.reshape(-1) for v in vs])
    return flat.reshape(-1, LANE)


def _unpack(packed, like):
    flat, out, off = packed.reshape(-1), [], 0
    for v in like:
        out.append(flat[off:off + v.size].reshape(v.shape))
        off += v.size
    return out


class MeshWeights:
    def __init__(self, w_kv, w_out, pool_w_in, pool_w_grp, hgrn_w_in, hgrn_norm_g):
        self.n_grp, self.pg_loc, self.pg = pool_w_grp.shape[1:]
        wpin, hnorm, token = all_gather_blocks("gather_first", [pool_w_in[0].astype(COMM_DTYPE), hgrn_norm_g])
        self.first = (wpin, hnorm.reshape(1, -1))
        zero = token[0, 0]
        cast = lambda a: (a + zero).astype(COMM_DTYPE)
        self.gather, self.tokens = {}, []
        for layer, blocks in ((0, [cast(pool_w_grp[0]), cast(w_kv[0]), cast(w_out[0])]),
                              (1, [cast(hgrn_w_in[0])]), (2, [cast(w_kv[1]), cast(w_out[1])])):
            me = _slot(_position())
            lands = [lax.dynamic_update_index_in_dim(lax.empty((N_DEV,) + b.shape, b.dtype), b, me, 0) for b in blocks]
            n = len(blocks)
            sems, bufs, token = split_start(f"gather{layer}_cross", blocks + lands, 4 * n, _gather_cross(n))
            self.gather[layer] = (sems, bufs)
            self.tokens.append(token[0, 0])
        self.reduce = {}

    def first_weights(self):
        return self.first[0], self.first[1], self.tokens[0] + self.tokens[1] + self.tokens[2]

    def _pass_on(self, layer, after):
        sems, bufs = self.gather[layer]
        n = len(bufs) // 2
        gathered = split_wait(f"gather{layer}_cross_wait", sems, bufs, _gather_cross(n), after)[n:]
        sems, bufs, token = split_start(f"gather{layer}_pass", gathered, 3 * n, _gather_pass(n))
        self.gather[layer] = (sems, bufs)
        return token[0, 0]

    def _gathered(self, layer, after):
        sems, bufs = self.gather[layer]
        return split_wait(f"gather{layer}_pass_wait", sems, bufs, _gather_pass(len(bufs)), after)

    def after_proj0(self, after):
        return self._pass_on(0, after)

    def layer0_weights(self, after):
        wgrp, wkv, wout = self._gathered(0, after)
        rows = lambda g: g.reshape(-1, g.shape[-1])
        return wgrp.transpose(1, 0, 2, 3).reshape(self.n_grp, self.pg, self.pg), rows(wkv), rows(wout)

    def after_out0(self, after):
        return self._pass_on(1, after)

    def layer1_weights(self, after):
        return self._gathered(1, after)[0]

    def after_proj1(self, after):
        return self._pass_on(2, after)

    def layer1_rest(self, after):
        wkv, wout = self._gathered(2, after)
        rows = lambda g: g.reshape(-1, g.shape[-1])
        return rows(wkv), rows(wout)

    def grads_ready(self, layer, grads):
        parts = {}
        for nm, g in grads.items():
            if nm == "pool_w_grp":
                g = g.reshape(self.n_grp, N_DEV, self.pg_loc, self.pg).transpose(1, 0, 2, 3)
            parts[nm] = g.reshape(4, 2, -1, g.shape[-1])
        names, srcs = list(parts), list(parts.values())
        lands = [lax.empty((4,) + p.shape[2:], p.dtype) for p in srcs]
        sems, bufs, token = split_start(f"reduce{layer}_pair", srcs + lands, 4 * len(srcs), _reduce_pair(len(srcs)))
        self.reduce[layer] = (names, sems, bufs)
        return token[0, 0]

    def grads_reduce(self, layer, after):
        names, sems, bufs = self.reduce[layer]
        n = len(names)
        bufs = split_wait(f"reduce{layer}_pair_wait", sems, bufs, _reduce_pair(n), after)
        chip_parts = [pair_add(f"pair_add_{nm}{layer}", bufs[t], bufs[n + t]) for t, nm in enumerate(names)]
        lands = [lax.empty((3,) + p.shape[1:], p.dtype) for p in chip_parts]
        sems, bufs, token = split_start(f"reduce{layer}_cross", chip_parts + lands, 3 * n, _reduce_cross(n))
        self.reduce[layer] = (names, sems, bufs)
        return token[0, 0]

    def grads_done(self, layer, after):
        names, sems, bufs = self.reduce[layer]
        n = len(names)
        bufs = split_wait(f"reduce{layer}_cross_wait", sems, bufs, _reduce_cross(n), after)
        return {nm: (bufs[t], bufs[n + t]) for t, nm in enumerate(names)}


def kernel(x, mem, norm_g, mem_norm_g, w_kv, w_out, pool_w_in, pool_w_grp, pool_scale, hgrn_w_in, hgrn_lb, hgrn_norm_g, final_g, loss_target, m_norm_g, m_mem_norm_g, m_w_kv, m_w_out, m_pool_w_in, m_pool_w_grp, m_pool_scale, m_hgrn_w_in, m_hgrn_lb, m_hgrn_norm_g, m_final_g, v_norm_g, v_mem_norm_g, v_w_kv, v_w_out, v_pool_w_in, v_pool_w_grp, v_pool_scale, v_hgrn_w_in, v_hgrn_lb, v_hgrn_norm_g, v_final_g):
    comm = MeshWeights(w_kv, w_out, pool_w_in, pool_w_grp, hgrn_w_in, hgrn_norm_g)
    loss, grad_x, small = run_step(x[0], mem[0], loss_target[0], norm_g, mem_norm_g, pool_scale, hgrn_lb, final_g, comm)
    loss = lax.psum(loss, AXES)
    rows = lambda a: a.reshape(a.shape[0], -1, a.shape[-1])
    state = dict(w_kv=(w_kv, m_w_kv, v_w_kv), w_out=(w_out, m_w_out, v_w_out), pool_w_in=(pool_w_in, m_pool_w_in, v_pool_w_in),
                 pool_w_grp=(pool_w_grp, m_pool_w_grp, v_pool_w_grp), hgrn_w_in=(hgrn_w_in, m_hgrn_w_in, v_hgrn_w_in))
    big = {}

    def update(group, layer, after):
        for nm, (chip_parts, landed) in comm.grads_done(group, after).items():
            w, m, v = state[nm]
            big[nm] = adamw_shard(f"adamw_{nm}{layer}", chip_parts, landed, rows(w), rows(m), rows(v),
                                  layer if w.shape[0] > 1 else 0, prev=big.get(nm))
            after = big[nm][0]
        return after

    after = update("0b", 0, update(1, 1, grad_x))
    names = ("norm_g", "mem_norm_g", "pool_scale", "hgrn_lb", "hgrn_norm_g", "final_g")
    like = (norm_g, mem_norm_g, pool_scale, hgrn_lb, _sds((1, N_DEV * hgrn_norm_g.shape[1]), F32), final_g)
    packed = _pack([small[nm].reshape(lk.shape) for nm, lk in zip(names, like)])
    reduced = _unpack(all_reduce_small("reduce_small", packed, after), like)
    n_loc = hgrn_norm_g.shape[1]
    reduced[4] = lax.dynamic_slice(reduced[4], (0, _slot(_position()) * n_loc), (1, n_loc))
    ws = (norm_g, mem_norm_g, pool_scale, hgrn_lb, hgrn_norm_g, final_g)
    ms = (m_norm_g, m_mem_norm_g, m_pool_scale, m_hgrn_lb, m_hgrn_norm_g, m_final_g)
    vs = (v_norm_g, v_mem_norm_g, v_pool_scale, v_hgrn_lb, v_hgrn_norm_g, v_final_g)
    res = adamw_small("adamw_small", _pack(reduced), _pack(ws), _pack(ms), _pack(vs))
    outs = {nm: [reduced[i]] + [_unpack(r, ws)[i] for r in res] for i, nm in enumerate(names)}
    update("0a", 0, res[0])
    for nm, (w, _, _) in state.items():
        outs[nm] = [r.reshape(w.shape) for r in big[nm]]
    order = ("norm_g", "mem_norm_g", "w_kv", "w_out", "pool_w_in", "pool_w_grp", "pool_scale", "hgrn_w_in", "hgrn_lb",
             "hgrn_norm_g", "final_g")
    return (loss, grad_x[None], *[outs[nm][0] for nm in order], *[outs[nm][1] for nm in order],
            *[outs[nm][2] for nm in order], *[outs[nm][3] for nm in order])
```

```python
import functools

import jax
import jax.numpy as jnp
from jax import lax
from jax.experimental import pallas as pl
from jax.experimental.pallas import tpu as pltpu

F32, BF16 = jnp.float32, jnp.bfloat16
MM_DTYPE = BF16
COMM_DTYPE = BF16
EPS = 1e-6
POOL_WINDOWS = (2, 4, 8, 16)
POOL_HALO = 16
HG_HEAD_DIM = 128
HG_CHUNK = 64
CA_HEADS = 4
N_DEV = 8
ADAM_LR, ADAM_B1, ADAM_B2, ADAM_EPS, ADAM_WD, ADAM_STEP = 0.001, 0.9, 0.999, 1e-08, 0.01, 10
VMEM_LIMIT = 48 << 20
VMEM_LIMIT_TALL = 56 << 20
LANE = 128
MESH = pl.DeviceIdType.MESH
AXES = ("x", "y", "c")

NN = (((1,), (0,)), ((), ()))
NT = (((1,), (1,)), ((), ()))
TN = (((0,), (0,)), ((), ()))


def _cp(sem=None, vmem=VMEM_LIMIT):
    return pltpu.CompilerParams(dimension_semantics=sem, vmem_limit_bytes=vmem)


def _dot(a, b, dims):
    return lax.dot_general(a.astype(MM_DTYPE), b.astype(MM_DTYPE), dims, preferred_element_type=F32)


def _sig(v):
    return 1.0 / (1.0 + jnp.exp(-v))


def _tile(n, pref):
    if n <= pref:
        return n
    t = (pref // LANE) * LANE
    while n % t:
        t -= LANE
    return t


def _sds(shape, dtype):
    return jax.ShapeDtypeStruct(tuple(shape), dtype)


def _mm(name, grid, sem, ins, in_specs, out_shapes, out_specs, dims, epi=None, nk=1, acc_shape=None, aliases=None,
        vmem=VMEM_LIMIT):
    n_in, n_out = len(ins), len(out_shapes)

    def body(*refs):
        a_ref, b_ref = refs[0], refs[1]
        extra = refs[2:n_in]
        outs = refs[n_in:n_in + n_out]
        p = _dot(a_ref[...], b_ref[...], dims)

        def finish(acc):
            res = epi(acc, *extra) if epi is not None else (acc,)
            for r, o_ref in zip(res, outs):
                o_ref[...] = r.astype(o_ref.dtype)

        if nk == 1:
            finish(p)
        else:
            acc_ref = refs[n_in + n_out]
            k = pl.program_id(len(grid) - 1)

            @pl.when(k == 0)
            def _():
                acc_ref[...] = p

            @pl.when(k > 0)
            def _():
                acc_ref[...] += p

            @pl.when(k == nk - 1)
            def _():
                finish(acc_ref[...])

    scratch = [pltpu.VMEM(acc_shape, F32)] if nk > 1 else []
    return pl.pallas_call(
        body, name=name, grid=grid, in_specs=in_specs, out_specs=out_specs, out_shape=out_shapes,
        scratch_shapes=scratch, compiler_params=_cp(sem, vmem), input_output_aliases=aliases or {},
    )(*ins)


def proj_fwd(name, h, wblk, tm=1024):
    T, D = h.shape
    nblk, _, nb = wblk.shape
    tm = _tile(T, tm)
    return _mm(name, (nblk, T // tm), ("parallel", "parallel"), [h, wblk],
               [pl.BlockSpec((tm, D), lambda j, i: (i, 0)), pl.BlockSpec((None, D, nb), lambda j, i: (j, 0, 0))],
               [_sds((T, nblk * nb), F32)], [pl.BlockSpec((tm, nb), lambda j, i: (i, j))], NN)[0]


def proj_bwd_w(name, h, dproj, nblk, tm=1024, tn=1024):
    T, D = h.shape
    N = dproj.shape[1]
    nb = N // nblk
    tm, tn = _tile(D, tm), _tile(nb, tn)
    per = nb // tn
    return _mm(name, (D // tm, N // tn), ("parallel", "parallel"), [h, dproj],
               [pl.BlockSpec((T, tm), lambda i, n: (0, i)), pl.BlockSpec((T, tn), lambda i, n: (0, n))],
               [_sds((nblk, D, nb), COMM_DTYPE)], [pl.BlockSpec((None, tm, tn), lambda i, n: (n // per, i, n % per))], TN)[0]


def proj_bwd_x(name, dproj, wblk, after, tm=1024):
    T, N = dproj.shape
    nblk, D, nb = wblk.shape
    tm = _tile(T, tm)
    return _mm(name, (T // tm, nblk), ("parallel", "arbitrary"), [dproj, wblk, jnp.reshape(after, (1, 1))],
               [pl.BlockSpec((tm, nb), lambda i, k: (i, k)), pl.BlockSpec((None, D, nb), lambda i, k: (k, 0, 0)),
                pl.BlockSpec(memory_space=pl.ANY)],
               [_sds((T, D), F32)], [pl.BlockSpec((tm, D), lambda i, k: (i, 0))], NT, nk=nblk, acc_shape=(tm, D),
               vmem=VMEM_LIMIT_TALL)[0]


def out_fwd(name, branch, wout, x, after, tm=1024, tn=512):
    T, E = branch.shape
    D = wout.shape[1]
    tm, tn = _tile(T, tm), _tile(D, tn)
    return _mm(name, (T // tm, D // tn), ("parallel", "parallel"), [branch, wout, x, jnp.reshape(after, (1, 1))],
               [pl.BlockSpec((tm, E), lambda i, j: (i, 0)), pl.BlockSpec((E, tn), lambda i, j: (0, j)),
                pl.BlockSpec((tm, tn), lambda i, j: (i, j)), pl.BlockSpec(memory_space=pl.ANY)],
               [_sds((T, D), F32)], [pl.BlockSpec((tm, tn), lambda i, j: (i, j))], NN,
               epi=lambda acc, x_ref, _: (acc + x_ref[...],))[0]


def mm_nt(name, a, b, out_dtype, tm=1024, tn=1024, res=None):
    M, K = a.shape
    N = b.shape[0]
    tm, tn = _tile(M, tm), _tile(N, tn)
    ins = [a, b] + ([res] if res is not None else [])
    specs = [pl.BlockSpec((tm, K), lambda i, j: (i, 0)), pl.BlockSpec((tn, K), lambda i, j: (j, 0))]
    if res is not None:
        specs.append(pl.BlockSpec((tm, tn), lambda i, j: (i, j)))
    epi = (lambda acc, r_ref: (acc + r_ref[...],)) if res is not None else None
    return _mm(name, (M // tm, N // tn), ("parallel", "parallel"), ins, specs,
               [_sds((M, N), out_dtype)], [pl.BlockSpec((tm, tn), lambda i, j: (i, j))], NT, epi=epi)[0]


def mm_nn(name, a, b, out_dtype, tm=512, tn=512):
    M, K = a.shape
    N = b.shape[1]
    tm, tn = _tile(M, tm), _tile(N, tn)
    return _mm(name, (M // tm, N // tn), ("parallel", "parallel"), [a, b],
               [pl.BlockSpec((tm, K), lambda i, j: (i, 0)), pl.BlockSpec((K, tn), lambda i, j: (0, j))],
               [_sds((M, N), out_dtype)], [pl.BlockSpec((tm, tn), lambda i, j: (i, j))], NN)[0]


def mm_tn(name, a, b, out_dtype, tm=512, tn=1024):
    K, M = a.shape
    N = b.shape[1]
    tm, tn = _tile(M, tm), _tile(N, tn)
    return _mm(name, (M // tm, N // tn), ("parallel", "parallel"), [a, b],
               [pl.BlockSpec((K, tm), lambda i, j: (0, i)), pl.BlockSpec((K, tn), lambda i, j: (0, j))],
               [_sds((M, N), out_dtype)], [pl.BlockSpec((tm, tn), lambda i, j: (i, j))], TN)[0]


def rmsnorm_fwd(name, x, g, tm=512):
    T, D = x.shape
    tm = _tile(T, tm)

    def body(x_ref, g_ref, o_ref):
        xf = x_ref[...]
        r = lax.rsqrt(jnp.mean(xf * xf, axis=-1, keepdims=True) + EPS)
        o_ref[...] = ((xf * r) * g_ref[...]).astype(o_ref.dtype)

    return pl.pallas_call(
        body, name=name, grid=(T // tm,),
        in_specs=[pl.BlockSpec((tm, D), lambda i: (i, 0)), pl.BlockSpec((1, D), lambda i: (0, 0))],
        out_specs=pl.BlockSpec((tm, D), lambda i: (i, 0)), out_shape=_sds((T, D), MM_DTYPE),
        compiler_params=_cp(("parallel",)),
    )(x, g.reshape(1, D))


def _rms_bwd_math(xf, g, dh):
    r = lax.rsqrt(jnp.mean(xf * xf, axis=-1, keepdims=True) + EPS)
    gd = dh * g
    dx = r * gd - xf * ((r * r * r) * jnp.mean(gd * xf, axis=-1, keepdims=True))
    dg = jnp.sum(dh * (xf * r), axis=0, keepdims=True)
    return dx, dg


def rmsnorm_bwd(name, x, g, dh, res=None, tm=256):
    T, D = x.shape
    tm = _tile(T, tm)
    has_res = res is not None

    def body(*refs):
        x_ref, g_ref, dh_ref = refs[:3]
        dx_ref, dxb_ref, dg_ref = refs[3 + has_res:]
        dx, dg = _rms_bwd_math(x_ref[...], g_ref[...], dh_ref[...])
        if has_res:
            dx = dx + refs[3][...]
        dx_ref[...] = dx
        dxb_ref[...] = dx.astype(dxb_ref.dtype)

        @pl.when(pl.program_id(0) == 0)
        def _():
            dg_ref[...] = jnp.zeros_like(dg_ref)

        dg_ref[...] += dg

    row = pl.BlockSpec((tm, D), lambda i: (i, 0))
    vec = pl.BlockSpec((1, D), lambda i: (0, 0))
    ins = [x, g.reshape(1, D), dh] + ([res] if has_res else [])
    return pl.pallas_call(
        body, name=name, grid=(T // tm,), in_specs=[row, vec, row] + ([row] if has_res else []),
        out_specs=[row, row, vec], out_shape=[_sds((T, D), F32), _sds((T, D), MM_DTYPE), _sds((1, D), F32)],
        compiler_params=_cp(("arbitrary",)),
    )(*ins)


def final_loss_bwd(name, x, g, target, tm=256):
    T, D = x.shape
    tm = _tile(T, tm)

    def body(x_ref, g_ref, t_ref, loss_ref, dx_ref, dxb_ref, dg_ref):
        xf, gv = x_ref[...], g_ref[...]
        r = lax.rsqrt(jnp.mean(xf * xf, axis=-1, keepdims=True) + EPS)
        err = (xf * r) * gv - t_ref[...]
        part = 0.5 * jnp.sum(jnp.mean(err * err, axis=-1, keepdims=True), axis=0, keepdims=True)
        dx, dg = _rms_bwd_math(xf, gv, err / D)
        dx_ref[...] = dx
        dxb_ref[...] = dx.astype(dxb_ref.dtype)

        @pl.when(pl.program_id(0) == 0)
        def _():
            dg_ref[...] = jnp.zeros_like(dg_ref)
            loss_ref[...] = jnp.zeros_like(loss_ref)

        dg_ref[...] += dg
        loss_ref[...] += jnp.broadcast_to(part, loss_ref.shape)

    row = pl.BlockSpec((tm, D), lambda i: (i, 0))
    vec = pl.BlockSpec((1, D), lambda i: (0, 0))
    return pl.pallas_call(
        body, name=name, grid=(T // tm,), in_specs=[row, vec, row],
        out_specs=[pl.BlockSpec((8, LANE), lambda i: (0, 0)), row, row, vec],
        out_shape=[_sds((8, LANE), F32), _sds((T, D), F32), _sds((T, D), MM_DTYPE), _sds((1, D), F32)],
        compiler_params=_cp(("arbitrary",)),
    )(x, g.reshape(1, D), target)


def _pool_tiles(T, emix):
    pg = emix // len(POOL_WINDOWS)
    tc = 256 if pg % 256 == 0 else LANE
    return pg, tc, _tile(T, 512)


def pool_fwd(name, proj, T, emix, after):
    pg, tc, R = _pool_tiles(T, emix)
    per_g = pg // tc

    def body(u_ref, _, o_ref):
        g = pl.program_id(0) // per_g
        for gi, w in enumerate(POOL_WINDOWS):
            @pl.when(g == gi)
            def _():
                for ci in range(T // R):
                    r0 = ci * R
                    cur = u_ref[r0:r0 + R, :]
                    halo = jnp.zeros((POOL_HALO, tc), F32) if ci == 0 else u_ref[r0 - POOL_HALO:r0, :]
                    s = jnp.concatenate([halo, cur], axis=0)
                    for st in range(w.bit_length() - 1):
                        s = s + pltpu.roll(s, 1 << st, axis=0)
                    t = r0 + lax.broadcasted_iota(jnp.int32, (R, 1), 0)
                    cnt = jnp.minimum(t + 1, w).astype(F32)
                    o_ref[r0:r0 + R, :] = (s[POOL_HALO:] / cnt - cur).astype(o_ref.dtype)

    return pl.pallas_call(
        body, name=name, grid=(emix // tc,),
        in_specs=[pl.BlockSpec((T, tc), lambda j: (0, j)), pl.BlockSpec(memory_space=pl.ANY)],
        out_specs=pl.BlockSpec((T, tc), lambda j: (0, j)), out_shape=_sds((T, emix), MM_DTYPE),
        compiler_params=_cp(("parallel",)),
    )(proj, jnp.reshape(after, (1, 1)))


def pool_bwd(name, dpooled, dproj, T, emix):
    pg, tc, R = _pool_tiles(T, emix)
    per_g = pg // tc

    def body(d_ref, _, o_ref):
        g = pl.program_id(0) // per_g
        for gi, w in enumerate(POOL_WINDOWS):
            @pl.when(g == gi)
            def _():
                n = R + POOL_HALO
                for ci in range(T // R):
                    r0 = ci * R
                    cur = d_ref[r0:r0 + R, :]
                    halo = jnp.zeros((POOL_HALO, tc), F32) if ci == T // R - 1 else d_ref[r0 + R:r0 + n, :]
                    t = r0 + lax.broadcasted_iota(jnp.int32, (n, 1), 0)
                    cnt = jnp.minimum(t + 1, w).astype(F32)
                    s = jnp.concatenate([cur, halo], axis=0) / cnt
                    for st in range(w.bit_length() - 1):
                        s = s + pltpu.roll(s, n - (1 << st), axis=0)
                    o_ref[r0:r0 + R, :] = (s[:R] - cur).astype(o_ref.dtype)

    return pl.pallas_call(
        body, name=name, grid=(emix // tc,),
        in_specs=[pl.BlockSpec((T, tc), lambda j: (0, j)), pl.BlockSpec(memory_space=pl.ANY)],
        out_specs=pl.BlockSpec((T, tc), lambda j: (0, j)), out_shape=_sds(dproj.shape, dproj.dtype),
        input_output_aliases={1: 0}, compiler_params=_cp(("parallel",)),
    )(dpooled, dproj)


def grp_fwd(name, pooled, wgrp, scale, proj, gate_off, E):
    T, emix = pooled.shape
    pg = emix // len(POOL_WINDOWS)
    tm = _tile(T, 2048)
    tc = 256 if pg % 256 == 0 and gate_off % 256 == 0 else LANE
    per = pg // tc

    def epi(acc, s_ref, g_ref):
        gt = g_ref[...]
        return acc, (acc * s_ref[...]) * (gt * _sig(gt))

    col = pl.BlockSpec((tm, tc), lambda n, i: (i, n))
    return _mm(name, (emix // tc, T // tm), ("parallel", "parallel"), [pooled, wgrp, scale, proj],
               [pl.BlockSpec((tm, pg), lambda n, i: (i, n // per)), pl.BlockSpec((None, pg, tc), lambda n, i: (n // per, 0, n % per)),
                pl.BlockSpec((1, tc), lambda n, i: (0, n)), pl.BlockSpec((tm, tc), lambda n, i: (i, gate_off // tc + n))],
               [_sds((T, emix), F32), _sds((T, E), MM_DTYPE)], [col, col], NN, epi=epi)


def grp_bwd_x(name, dy, wgrp):
    T, emix = dy.shape
    pg = emix // len(POOL_WINDOWS)
    tm = _tile(T, 1024)
    return _mm(name, (len(POOL_WINDOWS), T // tm), ("parallel", "parallel"), [dy, wgrp],
               [pl.BlockSpec((tm, pg), lambda g, i: (i, g)), pl.BlockSpec((None, pg, pg), lambda g, i: (g, 0, 0))],
               [_sds((T, emix), F32)], [pl.BlockSpec((tm, pg), lambda g, i: (i, g))], NT)[0]


def grp_bwd_w(name, pooled, dy):
    T, emix = dy.shape
    ng = len(POOL_WINDOWS)
    pg = emix // ng
    col = pl.BlockSpec((T, pg), lambda g: (0, g))
    return _mm(name, (ng,), ("parallel",), [pooled, dy], [col, col],
               [_sds((ng, pg, pg), COMM_DTYPE)], [pl.BlockSpec((None, pg, pg), lambda g: (g, 0, 0))], TN)[0]


def gate_bwd_pool(name, dbranch, y, scale, proj, gate_off, n_proj):
    T, emix = y.shape
    tm, tc = _tile(T, 1024), _tile(emix, 512)
    assert gate_off % tc == 0

    def body(db_ref, y_ref, s_ref, g_ref, dy_ref, dg_ref, ds_ref):
        db, yv, sc, gt = db_ref[...], y_ref[...], s_ref[...], g_ref[...]
        sg = _sig(gt)
        dmix = db * (gt * sg)
        dy_ref[...] = (dmix * sc).astype(dy_ref.dtype)
        dg_ref[...] = (db * (yv * sc) * (sg * (1.0 + gt * (1.0 - sg)))).astype(dg_ref.dtype)

        @pl.when(pl.program_id(1) == 0)
        def _():
            ds_ref[...] = jnp.zeros_like(ds_ref)

        ds_ref[...] += jnp.sum(dmix * yv, axis=0, keepdims=True)

    blk = pl.BlockSpec((tm, tc), lambda j, i: (i, j))
    vec = pl.BlockSpec((1, tc), lambda j, i: (0, j))
    gate = pl.BlockSpec((tm, tc), lambda j, i: (i, gate_off // tc + j))
    return pl.pallas_call(
        body, name=name, grid=(emix // tc, T // tm), in_specs=[blk, blk, vec, gate],
        out_specs=[blk, gate, vec],
        out_shape=[_sds((T, emix), MM_DTYPE), _sds((T, n_proj), MM_DTYPE), _sds((1, emix), F32)],
        compiler_params=_cp(("parallel", "arbitrary")),
    )(dbranch, y, scale, proj)


def attn_fwd(name, proj, kv, branch, q_off, gate_off, tm=1024):
    T = proj.shape[0]
    M, two_eca = kv.shape
    eca = two_eca // 2
    hd = eca // CA_HEADS
    E = branch.shape[1]
    tm = _tile(T, tm)
    scale = hd ** -0.5

    def body(q_ref, g_ref, k_ref, v_ref, _, ca_ref, br_ref):
        s = _dot(q_ref[...], k_ref[...], NT) * scale
        s = s - jnp.max(s, axis=-1, keepdims=True)
        e = jnp.exp(s)
        p = e / jnp.sum(e, axis=-1, keepdims=True)
        o = _dot(p, v_ref[...], NN)
        gt = g_ref[...]
        ca_ref[...] = o
        br_ref[...] = (o * (gt * _sig(gt))).astype(br_ref.dtype)

    return pl.pallas_call(
        body, name=name, grid=(CA_HEADS, T // tm),
        in_specs=[pl.BlockSpec((tm, hd), lambda h, i: (i, q_off // hd + h)),
                  pl.BlockSpec((tm, hd), lambda h, i: (i, (gate_off + E - eca) // hd + h)),
                  pl.BlockSpec((M, hd), lambda h, i: (0, h)), pl.BlockSpec((M, hd), lambda h, i: (0, CA_HEADS + h)),
                  pl.BlockSpec(memory_space=pl.ANY)],
        out_specs=[pl.BlockSpec((tm, hd), lambda h, i: (i, h)), pl.BlockSpec((tm, hd), lambda h, i: (i, (E - eca) // hd + h))],
        out_shape=[_sds((T, eca), F32), _sds(branch.shape, branch.dtype)],
        input_output_aliases={4: 1}, compiler_params=_cp(("parallel", "parallel")),
    )(proj, proj, kv, kv, branch)


def attn_bwd(name, proj, kv, ca, dbranch, dproj, q_off, gate_off, tm=1024):
    T = proj.shape[0]
    M, two_eca = kv.shape
    eca = two_eca // 2
    hd = eca // CA_HEADS
    E = dbranch.shape[1]
    tm = _tile(T, tm)
    scale = hd ** -0.5
    q_blk, g_blk = q_off // hd, (gate_off + E - eca) // hd

    n_rows = T // tm
    n_steps = CA_HEADS * n_rows

    def body(q_ref, g_ref, k_ref, v_ref, ca_ref, db_ref, _, dp_hbm, dk_ref, dv_ref, stash, sems):
        h, i = pl.program_id(0), pl.program_id(1)
        step = h * n_rows + i
        slot = step % 2

        def writes(slot, h, i):
            rows = pl.ds(pl.multiple_of(i * tm, tm), tm)
            return [pltpu.make_async_copy(stash.at[slot, r], dp_hbm.at[rows, pl.ds(pl.multiple_of((blk + h) * hd, hd), hd)],
                                          sems.at[slot, r]) for r, blk in enumerate((q_blk, g_blk))]

        @pl.when(step >= 2)
        def _():
            for cp in writes(slot, h, i):
                cp.wait()

        q, k, v = q_ref[...], k_ref[...], v_ref[...]
        s = _dot(q, k, NT) * scale
        s = s - jnp.max(s, axis=-1, keepdims=True)
        e = jnp.exp(s)
        p = e / jnp.sum(e, axis=-1, keepdims=True)
        gt, db = g_ref[...], db_ref[...]
        sg = _sig(gt)
        do = db * (gt * sg)
        stash[slot, 1] = (db * ca_ref[...] * (sg * (1.0 + gt * (1.0 - sg)))).astype(stash.dtype)
        dpr = _dot(do, v, NT)
        ds = p * (dpr - jnp.sum(dpr * p, axis=-1, keepdims=True)) * scale
        stash[slot, 0] = _dot(ds, k, NN).astype(stash.dtype)

        @pl.when(i == 0)
        def _():
            dk_ref[...] = jnp.zeros_like(dk_ref)
            dv_ref[...] = jnp.zeros_like(dv_ref)

        dk_ref[...] += _dot(ds, q, TN)
        dv_ref[...] += _dot(p, do, TN)
        for cp in writes(slot, h, i):
            cp.start()

        @pl.when(step == n_steps - 1)
        def _():
            for cp in writes(slot, h, i):
                cp.wait()
            if n_steps > 1:
                for cp in writes(1 - slot, h, i):
                    cp.wait()

    row = lambda blk: pl.BlockSpec((tm, hd), lambda h, i: (i, blk + h))
    outs = pl.pallas_call(
        body, name=name, grid=(CA_HEADS, n_rows),
        in_specs=[row(q_blk), row(g_blk), pl.BlockSpec((M, hd), lambda h, i: (0, h)),
                  pl.BlockSpec((M, hd), lambda h, i: (0, CA_HEADS + h)), row(0), row((E - eca) // hd),
                  pl.BlockSpec(memory_space=pl.ANY)],
        out_specs=[pl.BlockSpec(memory_space=pl.ANY), pl.BlockSpec((M, hd), lambda h, i: (0, h)),
                   pl.BlockSpec((M, hd), lambda h, i: (0, h))],
        out_shape=[_sds(dproj.shape, dproj.dtype), _sds((M, eca), F32), _sds((M, eca), F32)],
        scratch_shapes=[pltpu.VMEM((2, 2, tm, hd), dproj.dtype), pltpu.SemaphoreType.DMA((2, 2))],
        input_output_aliases={6: 0}, compiler_params=_cp(("arbitrary", "arbitrary")),
    )(proj, proj, kv, kv, ca, dbranch, dproj)
    return outs[0], jnp.concatenate([outs[1], outs[2]], axis=1)


def _split3(v):
    hi = v.astype(BF16)
    r1 = v - hi.astype(F32)
    mid = r1.astype(BF16)
    lo = (r1 - mid.astype(F32)).astype(BF16)
    return hi, mid, lo


def _tri_dot(tri, v):
    hi, mid, lo = _split3(v)
    d = functools.partial(lax.dot_general, dimension_numbers=NN, preferred_element_type=F32)
    return d(tri, hi) + d(tri, mid) + d(tri, lo)


def _causal_mask():
    C = HG_CHUNK
    return lax.broadcasted_iota(jnp.int32, (C, C), 0) >= lax.broadcasted_iota(jnp.int32, (C, C), 1)


def _cumsum_matrices(G):
    ri = lax.broadcasted_iota(jnp.int32, (G, G), 0)
    ci = lax.broadcasted_iota(jnp.int32, (G, G), 1)
    same = (ri // HG_CHUNK) == (ci // HG_CHUNK)
    return (jnp.where(same & (ri >= ci), 1.0, 0.0).astype(BF16), jnp.where(same & (ri <= ci), 1.0, 0.0).astype(BF16))


def _forget_gate(fi, lb):
    sig = _sig(fi)
    return sig, lb + (1.0 - lb) * sig


def _log_decay_cumsum(f_ref, b_s, lb, prefix, G):
    def group(i, carry):
        rows = pl.ds(pl.multiple_of(i * G, G), G)
        _, f = _forget_gate(f_ref[rows, :], lb)
        b_s[rows, :] = _tri_dot(prefix, jnp.log(f))
        return carry

    lax.fori_loop(0, b_s.shape[0] // G, group, 0)


def _lower_bound(lb2):
    l0, l1 = lb2[0:1, :], lb2[1:2, :]
    m = jnp.maximum(l0, l1)
    e0, e1 = jnp.exp(l0 - m), jnp.exp(l1 - m)
    sm0, sm1 = e0 / (e0 + e1), e1 / (e0 + e1)
    return (sm0 + sm1) - sm0, sm0, sm1


def _hg_chunk_fwd(qi, fi, lb, causal, b=None, prefix=None):
    sig, f = _forget_gate(fi, lb)
    k = 1.0 - f
    if b is None:
        b = _tri_dot(prefix, jnp.log(f))
    last = lax.broadcasted_iota(jnp.int32, b.shape, 0) == HG_CHUNK - 1
    bl = jnp.sum(jnp.where(last, b, 0.0), axis=0, keepdims=True)
    eb, enb, ete = jnp.exp(b), jnp.exp(-b), jnp.exp(bl - b)
    sq = _sig(qi)
    q = (qi * sq) * (HG_HEAD_DIM ** -0.5)
    q_dec, k_inv, k_te = q * eb, k * enb, k * ete
    att = jnp.where(causal, _dot(q_dec, k_inv, NT), 0.0)
    return dict(sig=sig, f=f, k=k, eb=eb, enb=enb, ete=ete, dcy=jnp.exp(bl), sq=sq, q_dec=q_dec, k_inv=k_inv, k_te=k_te,
                att=att, last=last)


def _hg_blocking(H, T):
    hb = 4 if H % 4 == 0 else (2 if H % 2 == 0 else 1)
    return hb, _tile(T, 1024)


def hgrn_fwd(name, proj, lb2, norm_g, emix, gate_off, E):
    T = proj.shape[0]
    hd, C = HG_HEAD_DIM, HG_CHUNK
    H, N = emix // hd, T // C
    hb, ts = _hg_blocking(H, T)
    W, nc, G = hb * hd, ts // C, _tile(ts, 256)

    def body(q_ref, f_ref, i_ref, g_ref, lb_ref, ng_ref, o_ref, st_ref, br_ref, state, b_s):
        @pl.when(pl.program_id(1) == 0)
        def _():
            state[...] = jnp.zeros_like(state)

        causal = _causal_mask()
        lb, _, _ = _lower_bound(lb_ref[...])
        ng = ng_ref[...]
        _log_decay_cumsum(f_ref, b_s, lb, _cumsum_matrices(G)[0], G)

        def chunk(n, carry):
            rows = pl.ds(pl.multiple_of(n * C, C), C)
            for h in range(hb):
                cols = slice(h * hd, (h + 1) * hd)
                v = i_ref[rows, cols]
                c = _hg_chunk_fwd(q_ref[rows, cols], f_ref[rows, cols], lb[:, cols], causal, b=b_s[rows, cols])
                st = state[h]
                o = _dot(c["att"], v, NN) + _dot(c["q_dec"], st, NT)
                st_ref[h, n] = st
                o_ref[rows, cols] = o
                r = lax.rsqrt(jnp.mean(o * o, axis=-1, keepdims=True) + EPS)
                gt = g_ref[rows, cols]
                br_ref[rows, cols] = (((o * r) * ng[:, cols]) * (gt * _sig(gt))).astype(br_ref.dtype)
                state[h] = st * c["dcy"] + _dot(v, c["k_te"], TN)
            return carry

        lax.fori_loop(0, nc, chunk, 0, unroll=8)

    col = lambda off: pl.BlockSpec((ts, W), lambda h, s: (s, off // W + h))
    vec = lambda r: pl.BlockSpec((r, W), lambda h, s: (0, h))
    return pl.pallas_call(
        body, name=name, grid=(H // hb, T // ts),
        in_specs=[col(0), col(emix), col(2 * emix), col(gate_off), vec(2), vec(1)],
        out_specs=[col(0), pl.BlockSpec((hb, nc, hd, hd), lambda h, s: (h, s, 0, 0)), col(0)],
        out_shape=[_sds((T, emix), F32), _sds((H, N, hd, hd), F32), _sds((T, E), MM_DTYPE)],
        scratch_shapes=[pltpu.VMEM((hb, hd, hd), F32), pltpu.VMEM((ts, W), F32)],
        compiler_params=_cp(("parallel", "arbitrary")),
    )(proj, proj, proj, proj, lb2, norm_g)


def hgrn_bwd(name, proj, lb2, norm_g, o_pre, states, dbranch, emix, gate_off):
    T, n_proj = proj.shape
    hd, C = HG_HEAD_DIM, HG_CHUNK
    H, N = emix // hd, T // C
    hb, ts = _hg_blocking(H, T)
    W, nc, ns = hb * hd, ts // C, T // ts
    qscale = HG_HEAD_DIM ** -0.5

    n_steps = (H // hb) * ns
    regions = (0, emix, 2 * emix, gate_off)

    def body(q_ref, f_ref, i_ref, g_ref, lb_ref, ng_ref, o_ref, st_ref, db_ref, dp_hbm, dlb_ref, dng_ref,
             stash, dst_ref, acc, sems):
        hblk, s = pl.program_id(0), pl.program_id(1)
        step = hblk * ns + s
        slot = step % 2

        def writes(slot, hblk, s):
            rows = pl.ds(pl.multiple_of((ns - 1 - s) * ts, ts), ts)
            return [pltpu.make_async_copy(stash.at[slot, r], dp_hbm.at[rows, pl.ds(pl.multiple_of(off + hblk * W, W), W)],
                                          sems.at[slot, r]) for r, off in enumerate(regions)]

        @pl.when(step >= 2)
        def _():
            for cp in writes(slot, hblk, s):
                cp.wait()

        @pl.when(s == 0)
        def _():
            dst_ref[...] = jnp.zeros_like(dst_ref)
            acc[...] = jnp.zeros_like(acc)

        causal = _causal_mask()
        prefix, suffix = _cumsum_matrices(C)
        lb, sm0, sm1 = _lower_bound(lb_ref[...])
        ng = ng_ref[...]

        def chunk(j, carry):
            n = nc - 1 - j
            rows = pl.ds(pl.multiple_of(n * C, C), C)
            for h in range(hb):
                cols = slice(h * hd, (h + 1) * hd)
                dst = dst_ref[h]
                qi, v, gt = q_ref[rows, cols], i_ref[rows, cols], g_ref[rows, cols]
                c = _hg_chunk_fwd(qi, f_ref[rows, cols], lb[:, cols], causal, prefix=prefix)
                st = st_ref[h, n]
                o, db, ngh = o_ref[rows, cols], db_ref[rows, cols], ng[:, cols]
                r = lax.rsqrt(jnp.mean(o * o, axis=-1, keepdims=True) + EPS)
                sg = _sig(gt)
                dmix = db * (gt * sg)
                dgate = db * ((o * r) * ngh) * (sg * (1.0 + gt * (1.0 - sg)))
                dyn = dmix * ngh
                do = r * dyn - o * ((r * r * r) * jnp.mean(dyn * o, axis=-1, keepdims=True))
                acc[1:2, cols] += jnp.sum(dmix * (o * r), axis=0, keepdims=True)
                datt = jnp.where(causal, _dot(do, v, NT), 0.0)
                dq_dec = _dot(datt, c["k_inv"], NN) + _dot(do, st, NN)
                dk_inv = _dot(datt, c["q_dec"], TN)
                dv = _dot(c["att"], do, TN) + _dot(c["k_te"], dst, NT)
                dk_te = _dot(v, dst, NN)
                ddcy = jnp.sum(dst * st, axis=0, keepdims=True)
                dst_ref[h] = _dot(do, c["q_dec"], TN) + dst * c["dcy"]
                t_te = dk_te * c["k_te"]
                dbv = dq_dec * c["q_dec"] - dk_inv * c["k_inv"] - t_te
                dbl = jnp.sum(t_te, axis=0, keepdims=True) + ddcy * c["dcy"]
                dbv = dbv + jnp.where(c["last"], dbl, 0.0)
                dlf = _tri_dot(suffix, dbv)
                dk = dk_inv * c["enb"] + dk_te * c["ete"]
                df = dlf / c["f"] - dk
                sig, sq = c["sig"], c["sq"]
                acc[0:1, cols] += jnp.sum(df * (1.0 - sig), axis=0, keepdims=True)
                dqi = (dq_dec * c["eb"]) * qscale * (sq * (1.0 + qi * (1.0 - sq)))
                stash[slot, 0, rows, cols] = dqi.astype(stash.dtype)
                stash[slot, 1, rows, cols] = (df * (1.0 - lb[:, cols]) * (sig * (1.0 - sig))).astype(stash.dtype)
                stash[slot, 2, rows, cols] = dv.astype(stash.dtype)
                stash[slot, 3, rows, cols] = dgate.astype(stash.dtype)
            return carry

        lax.fori_loop(0, nc, chunk, 0, unroll=8)
        t = acc[0:1, :] * (sm0 * sm1)
        dlb_ref[...] = jnp.concatenate([-t, t], axis=0)
        dng_ref[...] = acc[1:2, :]
        for cp in writes(slot, hblk, s):
            cp.start()

        @pl.when(step == n_steps - 1)
        def _():
            for cp in writes(slot, hblk, s):
                cp.wait()
            if n_steps > 1:
                for cp in writes(1 - slot, hblk, s):
                    cp.wait()

    col = lambda off: pl.BlockSpec((ts, W), lambda h, s: (ns - 1 - s, off // W + h))
    vec = lambda r: pl.BlockSpec((r, W), lambda h, s: (0, h))
    return pl.pallas_call(
        body, name=name, grid=(H // hb, ns),
        in_specs=[col(0), col(emix), col(2 * emix), col(gate_off), vec(2), vec(1), col(0),
                  pl.BlockSpec((hb, nc, hd, hd), lambda h, s: (h, ns - 1 - s, 0, 0)), col(0)],
        out_specs=[pl.BlockSpec(memory_space=pl.ANY), vec(2), vec(1)],
        out_shape=[_sds((T, n_proj), MM_DTYPE), _sds((2, emix), F32), _sds((1, emix), F32)],
        scratch_shapes=[pltpu.VMEM((2, 4, ts, W), MM_DTYPE), pltpu.VMEM((hb, hd, hd), F32), pltpu.VMEM((8, W), F32),
                        pltpu.SemaphoreType.DMA((2, 4))],
        compiler_params=_cp(("arbitrary", "arbitrary")),
    )(proj, proj, proj, proj, lb2, norm_g, o_pre, states, dbranch)


class WholeWeights:
    def __init__(self, wkv, wout, wpin, wgrp, whin, hgrn_norm_g):
        self.w = dict(wkv=wkv, wout=wout, wpin=wpin, wgrp=wgrp, whin=whin, hnorm=hgrn_norm_g)
        self.grads = {}

    def first_weights(self):
        return self.w["wpin"], self.w["hnorm"], 0.0

    def after_proj0(self, after):
        return 0.0

    def layer0_weights(self, after):
        return self.w["wgrp"], self.w["wkv"][0], self.w["wout"][0]

    def after_out0(self, after):
        return 0.0

    def layer1_weights(self, after):
        return self.w["whin"]

    def after_proj1(self, after):
        return 0.0

    def layer1_rest(self, after):
        return self.w["wkv"][1], self.w["wout"][1]

    def grads_ready(self, layer, grads):
        self.grads[layer] = grads
        return 0.0

    def grads_reduce(self, layer, after):
        return 0.0


def run_step(x, mem, target, norm_g, mem_norm_g, pool_scale, hgrn_lb, final_g, comm):
    T, D = x.shape
    mem_n = rmsnorm_fwd("mem_norm", mem, mem_norm_g)
    wpin, hgrn_norm_g, zero = comm.first_weights()
    h0 = rmsnorm_fwd("norm0", x, norm_g[0] + zero)
    proj0 = proj_fwd("proj0", h0, wpin)
    zero = comm.after_proj0(proj0)
    n0 = proj0.shape[1]
    E = n0 // 2
    eca = E // 4
    emix = E - eca
    q0, g0 = emix, emix + eca
    q1, g1 = 3 * emix, 3 * emix + eca
    pooled = pool_fwd("pool_fwd", proj0, T, emix, zero)
    wgrp, wkv0, wout0 = comm.layer0_weights(pooled)
    y0, branch0 = grp_fwd("grp_fwd", pooled, wgrp, pool_scale, proj0, g0, E)
    kv0 = mm_nn("kv0", mem_n, wkv0, MM_DTYPE)
    ca0, branch0 = attn_fwd("attn_fwd0", proj0, kv0, branch0, q0, g0)
    x1 = out_fwd("out0", branch0, wout0, x, ca0[:1, :1])
    h1 = rmsnorm_fwd("norm1", x1, norm_g[1] + comm.after_out0(x1))
    whin = comm.layer1_weights(h1)
    proj1 = proj_fwd("proj1", h1, whin)
    zero = comm.after_proj1(proj1)
    o1, states, branch1 = hgrn_fwd("hgrn_fwd", proj1, hgrn_lb + zero, hgrn_norm_g, emix, g1, E)
    wkv1, wout1 = comm.layer1_rest(o1)
    kv1 = mm_nn("kv1", mem_n, wkv1, MM_DTYPE)
    ca1, branch1 = attn_fwd("attn_fwd1", proj1, kv1, branch1, q1, g1)
    x2 = out_fwd("out1", branch1, wout1, x1, ca1[:1, :1])
    loss, dx2, dx2b, g_final = final_loss_bwd("final", x2, final_g, target)
    g_wout1 = mm_tn("gwout1", branch1, dx2b, COMM_DTYPE)
    dbranch1 = mm_nt("dbranch1", dx2b, wout1, F32)
    dproj1, g_lb, g_hnorm = hgrn_bwd("hgrn_bwd", proj1, hgrn_lb, hgrn_norm_g, o1, states, dbranch1, emix, g1)
    dproj1, dkv1 = attn_bwd("attn_bwd1", proj1, kv1, ca1, dbranch1, dproj1, q1, g1)
    g_wkv1 = mm_tn("gwkv1", mem_n, dkv1, COMM_DTYPE)
    dmem_n = mm_nt("dmem1", dkv1, wkv1, F32)
    g_whin = proj_bwd_w("gwhin", h1, dproj1, whin.shape[0])
    zero = comm.grads_ready(1, dict(w_kv=g_wkv1, w_out=g_wout1, hgrn_w_in=g_whin))
    dh1 = proj_bwd_x("dh1", dproj1, whin, zero)
    zero = zero + comm.grads_reduce(1, dh1)
    dx1, dx1b, g_norm1 = rmsnorm_bwd("norm1_bwd", x1, norm_g[1] + zero, dh1, res=dx2)
    dbranch0 = mm_nt("dbranch0", dx1b, wout0, F32)
    dy0, dproj0, g_scale = gate_bwd_pool("gate_bwd0", dbranch0, y0, pool_scale, proj0, g0, n0)
    dpooled = grp_bwd_x("dpooled", dy0, wgrp)
    dproj0 = pool_bwd("pool_bwd", dpooled, dproj0, T, emix)
    dproj0, dkv0 = attn_bwd("attn_bwd0", proj0, kv0, ca0, dbranch0, dproj0, q0, g0)
    g_wpin = proj_bwd_w("gwpin", h0, dproj0, wpin.shape[0])
    dkv0 = dkv0 + comm.grads_ready("0b", dict(pool_w_in=g_wpin))
    g_wkv0 = mm_tn("gwkv0", mem_n, dkv0, COMM_DTYPE)
    dmem_n = mm_nt("dmem0", dkv0, wkv0, F32, res=dmem_n)
    g_wout0 = mm_tn("gwout0", branch0, dx1b, COMM_DTYPE)
    g_wgrp = grp_bwd_w("gwgrp", pooled, dy0)
    zero = comm.grads_reduce("0b", g_wgrp)
    zero = zero + comm.grads_ready("0a", dict(w_kv=g_wkv0, w_out=g_wout0, pool_w_grp=g_wgrp))
    dh0 = proj_bwd_x("dh0", dproj0, wpin, zero)
    zero = zero + comm.grads_reduce("0a", dh0)
    grad_x, _, g_norm0 = rmsnorm_bwd("norm0_bwd", x, norm_g[0] + zero, dh0, res=dx1)
    _, _, g_mem = rmsnorm_bwd("mem_norm_bwd", mem, mem_norm_g, dmem_n)

    small = dict(norm_g=jnp.concatenate([g_norm0, g_norm1], axis=0), mem_norm_g=g_mem[0], pool_scale=g_scale,
                 hgrn_lb=g_lb, hgrn_norm_g=g_hnorm, final_g=g_final[0])
    return loss[0, 0], grad_x, small


def _position():
    return lax.axis_index("x"), lax.axis_index("y"), lax.axis_index("c")


def _slot(p):
    return 4 * p[0] + 2 * p[1] + p[2]


def all_gather_blocks(name, blocks):
    n = len(blocks)
    halved = [b.shape[0] % 32 == 0 for b in blocks]

    def body(*refs):
        ins, outs, token = refs[:n], refs[n:2 * n], refs[2 * n]
        send_sems, recv_sems, local_sems = refs[2 * n + 1:]
        token[...] = jnp.zeros_like(token)
        x, y, c = _position()
        me, sibling = (x, y, c), (x, y, 1 - c)
        x_nbr, y_nbr, diag = _other_chips(x, y)

        def copy(t, k, block, to, src=None, rows=None):
            dst = outs[t].at[_slot(block)]
            if rows is not None:
                dst = dst.at[rows]
            return pltpu.make_async_remote_copy(src_ref=dst if src is None else src, dst_ref=dst, send_sem=send_sems.at[t, k],
                                                recv_sem=recv_sems.at[t, k], device_id=to, device_id_type=MESH)

        mine = [pltpu.make_async_copy(ins[t], outs[t].at[_slot(me)], local_sems.at[t]) for t in range(n)]
        for cp in mine:
            cp.start()
        sends = []
        for t in range(n):
            sends += [copy(t, 0, me, sibling, src=ins[t]), copy(t, 1, me, (*x_nbr, c), src=ins[t]),
                      copy(t, 2, me, (*y_nbr, c), src=ins[t])]
            if not halved[t]:
                sends.append(copy(t, 4, me, (*diag, c), src=ins[t]))
        for cp in sends:
            cp.start()

        def start(cp):
            cp.start()
            sends.append(cp)

        for t in range(n):
            half = blocks[t].shape[0] // 2
            top, bottom = pl.ds(0, half), pl.ds(half, half)
            copy(t, 1, (*x_nbr, c), me).wait_recv()
            if halved[t]:
                start(copy(t, 4, (*x_nbr, c), (*y_nbr, c), rows=top))
            start(copy(t, 3, (*x_nbr, c), sibling))
            copy(t, 2, (*y_nbr, c), me).wait_recv()
            if halved[t]:
                start(copy(t, 6, (*y_nbr, c), (*x_nbr, c), rows=bottom))
            start(copy(t, 5, (*y_nbr, c), sibling))
            if halved[t]:
                copy(t, 4, (*diag, c), me, rows=top).wait_recv()
                copy(t, 6, (*diag, c), me, rows=bottom).wait_recv()
            else:
                copy(t, 4, (*diag, c), me).wait_recv()
            start(copy(t, 7, (*diag, c), sibling))
        for t in range(n):
            copy(t, 0, sibling, me).wait_recv()
            for k, chip in ((3, x_nbr), (5, y_nbr), (7, diag)):
                copy(t, k, (*chip, 1 - c), me).wait_recv()
        for cp in sends:
            cp.wait_send()
        for cp in mine:
            cp.wait()

    any_spec = pl.BlockSpec(memory_space=pl.ANY)
    return pl.pallas_call(
        body, name=name, in_specs=[any_spec] * n, out_specs=[any_spec] * n + [pl.BlockSpec(memory_space=pltpu.VMEM)],
        out_shape=[_sds((N_DEV,) + b.shape, b.dtype) for b in blocks] + [_sds((8, LANE), F32)],
        scratch_shapes=[pltpu.SemaphoreType.DMA((n, 8)), pltpu.SemaphoreType.DMA((n, 8)), pltpu.SemaphoreType.DMA((n,))],
        compiler_params=pltpu.CompilerParams(has_side_effects=True),
    )(*blocks)


_HBM = pl.BlockSpec(memory_space=pltpu.HBM)
_SEM = pl.BlockSpec(memory_space=pltpu.SEMAPHORE)
_EFFECT = pltpu.SideEffectType.DATAFLOW_SIDE_EFFECTING


def split_start(name, bufs, n_copies, build):
    n = len(bufs)

    def body(*refs):
        for cp in build(refs[:n], refs[n], refs[n + 1]):
            cp.start()
        refs[-1][...] = jnp.zeros_like(refs[-1])

    outs = pl.pallas_call(
        body, name=name, in_specs=[_HBM] * n,
        out_shape=(pltpu.SemaphoreType.DMA((n_copies,)), pltpu.SemaphoreType.DMA((n_copies,)),
                   *[pltpu.HBM(b.shape, b.dtype) for b in bufs], _sds((8, LANE), F32)),
        out_specs=(_SEM, _SEM, *[_HBM] * n, pl.BlockSpec(memory_space=pltpu.VMEM)),
        input_output_aliases={i: 2 + i for i in range(n)},
        compiler_params=pltpu.CompilerParams(has_side_effects=_EFFECT),
    )(*[pltpu.with_memory_space_constraint(b, pltpu.HBM) for b in bufs])
    return (outs[0], outs[1]), list(outs[2:2 + n]), outs[-1]


def split_wait(name, sems, bufs, build, after):
    n = len(bufs)

    def body(*refs):
        for cp in build(refs[:n], refs[n], refs[n + 1]):
            cp.wait()

    return list(pl.pallas_call(
        body, name=name, in_specs=[_HBM] * n + [_SEM, _SEM, pl.BlockSpec(memory_space=pl.ANY)],
        out_shape=[pltpu.HBM(b.shape, b.dtype) for b in bufs], out_specs=[_HBM] * n,
        input_output_aliases={i: i for i in range(n)},
        compiler_params=pltpu.CompilerParams(has_side_effects=_EFFECT),
    )(*bufs, sems[0], sems[1], after))


def _remote(src, dst, send_sems, recv_sems, k, to):
    return pltpu.make_async_remote_copy(src_ref=src, dst_ref=dst, send_sem=send_sems.at[k], recv_sem=recv_sems.at[k],
                                        device_id=to, device_id_type=MESH)


def _other_chips(x, y):
    return [(1 - x, y), (x, 1 - y), (1 - x, 1 - y)]


def _gather_cross(n):
    def build(refs, ss, rs):
        x, y, c = _position()
        me = _slot((x, y, c))
        targets = [(x, y, 1 - c)] + [(*chip, c) for chip in _other_chips(x, y)]
        return [_remote(refs[t], refs[n + t].at[me], ss, rs, 4 * t + k, to) for t in range(n) for k, to in enumerate(targets)]
    return build


def _gather_pass(n):
    def build(refs, ss, rs):
        x, y, c = _position()
        cps = []
        for t in range(n):
            for j, chip in enumerate(_other_chips(x, y)):
                blk = refs[t].at[_slot((*chip, c))]
                cps.append(_remote(blk, blk, ss, rs, 3 * t + j, (x, y, 1 - c)))
        return cps
    return build


def _reduce_pair(n):
    def build(refs, ss, rs):
        x, y, c = _position()
        return [_remote(refs[t].at[j, 1 - c], refs[n + t].at[j], ss, rs, 4 * t + j, (x, y, 1 - c))
                for t in range(n) for j in range(4)]
    return build


def _reduce_cross(n):
    def build(refs, ss, rs):
        x, y, c = _position()
        return [_remote(refs[t].at[2 * chip[0] + chip[1]], refs[n + t].at[r], ss, rs, 3 * t + r, (*chip, c))
                for t in range(n) for r, chip in enumerate(_other_chips(x, y))]
    return build


def pair_add(name, part, landed, tr=1024):
    _, _, R, C = part.shape
    tr = _row_tile(R, tr)

    def body(_, p_ref, l_ref, o_ref):
        o_ref[...] = (p_ref[...].astype(F32) + l_ref[...].astype(F32)).astype(o_ref.dtype)

    blk = pl.BlockSpec((None, tr, C), lambda j, i, core: (j, i, 0))
    core = lax.axis_index("c").astype(jnp.int32).reshape(1)
    return pl.pallas_call(
        body, name=name, out_shape=_sds((4, R, C), part.dtype), compiler_params=_cp(("parallel", "parallel")),
        grid_spec=pltpu.PrefetchScalarGridSpec(
            num_scalar_prefetch=1, grid=(4, R // tr),
            in_specs=[pl.BlockSpec((None, None, tr, C), lambda j, i, core: (j, core[0], i, 0)), blk], out_specs=blk),
    )(core, part, landed)


def _row_tile(R, pref):
    if R <= pref:
        return R
    t = (pref // 16) * 16
    while R % t:
        t -= 16
    return t


def all_reduce_small(name, vec, after):
    def body(v_ref, _, sum_ref, land_ref, send_sems, recv_sems):
        x, y, c = _position()
        me = (x, y, c)
        flip = lambda v, bit: 1 - v if bit else v
        peers = [(flip(x, k & 4), flip(y, k & 2), flip(c, k & 1)) for k in range(1, N_DEV)]
        land_ref[_slot(me)] = v_ref[...]
        sends = [pltpu.make_async_remote_copy(src_ref=v_ref, dst_ref=land_ref.at[_slot(me)], send_sem=send_sems.at[k],
                                              recv_sem=recv_sems.at[k], device_id=peer, device_id_type=MESH)
                 for k, peer in enumerate(peers)]
        for cp in sends:
            cp.start()
        for k, peer in enumerate(peers):
            pltpu.make_async_remote_copy(src_ref=v_ref, dst_ref=land_ref.at[_slot(peer)], send_sem=send_sems.at[k],
                                         recv_sem=recv_sems.at[k], device_id=peer, device_id_type=MESH).wait_recv()
        acc = land_ref[0]
        for s in range(1, N_DEV):
            acc = acc + land_ref[s]
        sum_ref[...] = acc
        for cp in sends:
            cp.wait_send()

    vm = pl.BlockSpec(memory_space=pltpu.VMEM)
    return pl.pallas_call(
        body, name=name, in_specs=[vm, pl.BlockSpec(memory_space=pl.ANY)], out_specs=[vm, vm],
        out_shape=[_sds(vec.shape, F32), _sds((N_DEV,) + vec.shape, F32)],
        scratch_shapes=[pltpu.SemaphoreType.DMA((7,)), pltpu.SemaphoreType.DMA((7,))],
        compiler_params=pltpu.CompilerParams(has_side_effects=True),
    )(vec, after)[0]


def _adamw_math(g, w, m, v):
    m2 = ADAM_B1 * m + (1.0 - ADAM_B1) * g
    v2 = ADAM_B2 * v + (1.0 - ADAM_B2) * (g * g)
    m_hat = m2 / (1.0 - ADAM_B1 ** ADAM_STEP)
    v_hat = v2 / (1.0 - ADAM_B2 ** ADAM_STEP)
    return -ADAM_LR * (m_hat / (jnp.sqrt(v_hat) + ADAM_EPS) + ADAM_WD * w), m2, v2


def adamw_shard(name, chip_parts, landed, w, m, v, layer, prev=None, tr=128):
    L, R, C = w.shape
    tr = _row_tile(R, tr)
    n_in = 5 + (4 if prev is not None else 0)

    def body(*refs):
        own_ref, land_ref, w_ref, m_ref, v_ref = refs[1:6]
        g_ref, d_ref, nm_ref, nv_ref = refs[1 + n_in:]
        g = own_ref[...].astype(F32)
        for s in range(landed.shape[0]):
            g = g + land_ref[s].astype(F32)
        g_ref[...] = g
        d_ref[...], nm_ref[...], nv_ref[...] = _adamw_math(g, w_ref[...], m_ref[...], v_ref[...])

    lay = pl.BlockSpec((None, tr, C), lambda i, chip: (layer, i, 0))
    own = pl.BlockSpec((None, tr, C), lambda i, chip: (chip[0], i, 0))
    chip = (2 * lax.axis_index("x") + lax.axis_index("y")).astype(jnp.int32).reshape(1)
    ins = [chip_parts, landed, w, m, v] + (list(prev) if prev is not None else [])
    return pl.pallas_call(
        body, name=name, out_shape=[_sds((L, R, C), F32)] * 4,
        grid_spec=pltpu.PrefetchScalarGridSpec(
            num_scalar_prefetch=1, grid=(R // tr,),
            in_specs=[own, pl.BlockSpec((landed.shape[0], tr, C), lambda i, chip: (0, i, 0)), lay, lay, lay]
            + [pl.BlockSpec(memory_space=pl.ANY)] * (n_in - 5),
            out_specs=[lay] * 4),
        input_output_aliases={6 + i: i for i in range(n_in - 5)}, compiler_params=_cp(("parallel",)),
    )(chip, *ins)


def adamw_small(name, g, w, m, v):
    def body(g_ref, w_ref, m_ref, v_ref, d_ref, nm_ref, nv_ref):
        d_ref[...], nm_ref[...], nv_ref[...] = _adamw_math(g_ref[...], w_ref[...], m_ref[...], v_ref[...])

    return pl.pallas_call(body, name=name, out_shape=[_sds(w.shape, F32)] * 3)(g, w, m, v)


def _pack(vs):
    flat = jnp.concatenate([v.reshape(-1) for v in vs])
    return flat.reshape(-1, LANE)


def _unpack(packed, like):
    flat, out, off = packed.reshape(-1), [], 0
    for v in like:
        out.append(flat[off:off + v.size].reshape(v.shape))
        off += v.size
    return out


class MeshWeights:
    def __init__(self, w_kv, w_out, pool_w_in, pool_w_grp, hgrn_w_in, hgrn_norm_g):
        self.n_grp, self.pg_loc, self.pg = pool_w_grp.shape[1:]
        wpin, hnorm, token = all_gather_blocks("gather_first", [pool_w_in[0].astype(COMM_DTYPE), hgrn_norm_g])
        self.first = (wpin, hnorm.reshape(1, -1))
        zero = token[0, 0]
        cast = lambda a: (a + zero).astype(COMM_DTYPE)
        self.gather, self.tokens = {}, []
        for layer, blocks in ((0, [cast(pool_w_grp[0]), cast(w_kv[0]), cast(w_out[0])]),
                              (1, [cast(hgrn_w_in[0])]), (2, [cast(w_kv[1]), cast(w_out[1])])):
            me = _slot(_position())
            lands = [lax.dynamic_update_index_in_dim(lax.empty((N_DEV,) + b.shape, b.dtype), b, me, 0) for b in blocks]
            n = len(blocks)
            sems, bufs, token = split_start(f"gather{layer}_cross", blocks + lands, 4 * n, _gather_cross(n))
            self.gather[layer] = (sems, bufs)
            self.tokens.append(token[0, 0])
        self.reduce = {}

    def first_weights(self):
        return self.first[0], self.first[1], self.tokens[0] + self.tokens[1] + self.tokens[2]

    def _pass_on(self, layer, after):
        sems, bufs = self.gather[layer]
        n = len(bufs) // 2
        gathered = split_wait(f"gather{layer}_cross_wait", sems, bufs, _gather_cross(n), after)[n:]
        sems, bufs, token = split_start(f"gather{layer}_pass", gathered, 3 * n, _gather_pass(n))
        self.gather[layer] = (sems, bufs)
        return token[0, 0]

    def _gathered(self, layer, after):
        sems, bufs = self.gather[layer]
        return split_wait(f"gather{layer}_pass_wait", sems, bufs, _gather_pass(len(bufs)), after)

    def after_proj0(self, after):
        return self._pass_on(0, after)

    def layer0_weights(self, after):
        wgrp, wkv, wout = self._gathered(0, after)
        rows = lambda g: g.reshape(-1, g.shape[-1])
        return wgrp.transpose(1, 0, 2, 3).reshape(self.n_grp, self.pg, self.pg), rows(wkv), rows(wout)

    def after_out0(self, after):
        return self._pass_on(1, after)

    def layer1_weights(self, after):
        return self._gathered(1, after)[0]

    def after_proj1(self, after):
        return self._pass_on(2, after)

    def layer1_rest(self, after):
        wkv, wout = self._gathered(2, after)
        rows = lambda g: g.reshape(-1, g.shape[-1])
        return rows(wkv), rows(wout)

    def grads_ready(self, layer, grads):
        parts = {}
        for nm, g in grads.items():
            if nm == "pool_w_grp":
                g = g.reshape(self.n_grp, N_DEV, self.pg_loc, self.pg).transpose(1, 0, 2, 3)
            parts[nm] = g.reshape(4, 2, -1, g.shape[-1])
        names, srcs = list(parts), list(parts.values())
        lands = [lax.empty((4,) + p.shape[2:], p.dtype) for p in srcs]
        sems, bufs, token = split_start(f"reduce{layer}_pair", srcs + lands, 4 * len(srcs), _reduce_pair(len(srcs)))
        self.reduce[layer] = (names, sems, bufs)
        return token[0, 0]

    def grads_reduce(self, layer, after):
        names, sems, bufs = self.reduce[layer]
        n = len(names)
        bufs = split_wait(f"reduce{layer}_pair_wait", sems, bufs, _reduce_pair(n), after)
        chip_parts = [pair_add(f"pair_add_{nm}{layer}", bufs[t], bufs[n + t]) for t, nm in enumerate(names)]
        lands = [lax.empty((3,) + p.shape[1:], p.dtype) for p in chip_parts]
        sems, bufs, token = split_start(f"reduce{layer}_cross", chip_parts + lands, 3 * n, _reduce_cross(n))
        self.reduce[layer] = (names, sems, bufs)
        return token[0, 0]

    def grads_done(self, layer, after):
        names, sems, bufs = self.reduce[layer]
        n = len(names)
        bufs = split_wait(f"reduce{layer}_cross_wait", sems, bufs, _reduce_cross(n), after)
        return {nm: (bufs[t], bufs[n + t]) for t, nm in enumerate(names)}


def kernel(x, mem, norm_g, mem_norm_g, w_kv, w_out, pool_w_in, pool_w_grp, pool_scale, hgrn_w_in, hgrn_lb, hgrn_norm_g, final_g, loss_target, m_norm_g, m_mem_norm_g, m_w_kv, m_w_out, m_pool_w_in, m_pool_w_grp, m_pool_scale, m_hgrn_w_in, m_hgrn_lb, m_hgrn_norm_g, m_final_g, v_norm_g, v_mem_norm_g, v_w_kv, v_w_out, v_pool_w_in, v_pool_w_grp, v_pool_scale, v_hgrn_w_in, v_hgrn_lb, v_hgrn_norm_g, v_final_g):
    comm = MeshWeights(w_kv, w_out, pool_w_in, pool_w_grp, hgrn_w_in, hgrn_norm_g)
    loss, grad_x, small = run_step(x[0], mem[0], loss_target[0], norm_g, mem_norm_g, pool_scale, hgrn_lb, final_g, comm)
    loss = lax.psum(loss, AXES)
    rows = lambda a: a.reshape(a.shape[0], -1, a.shape[-1])
    state = dict(w_kv=(w_kv, m_w_kv, v_w_kv), w_out=(w_out, m_w_out, v_w_out), pool_w_in=(pool_w_in, m_pool_w_in, v_pool_w_in),
                 pool_w_grp=(pool_w_grp, m_pool_w_grp, v_pool_w_grp), hgrn_w_in=(hgrn_w_in, m_hgrn_w_in, v_hgrn_w_in))
    big = {}

    def update(group, layer, after):
        for nm, (chip_parts, landed) in comm.grads_done(group, after).items():
            w, m, v = state[nm]
            big[nm] = adamw_shard(f"adamw_{nm}{layer}", chip_parts, landed, rows(w), rows(m), rows(v),
                                  layer if w.shape[0] > 1 else 0, prev=big.get(nm))
            after = big[nm][0]
        return after

    after = update("0b", 0, update(1, 1, grad_x))
    names = ("norm_g", "mem_norm_g", "pool_scale", "hgrn_lb", "hgrn_norm_g", "final_g")
    like = (norm_g, mem_norm_g, pool_scale, hgrn_lb, _sds((1, N_DEV * hgrn_norm_g.shape[1]), F32), final_g)
    packed = _pack([small[nm].reshape(lk.shape) for nm, lk in zip(names, like)])
    reduced = _unpack(all_reduce_small("reduce_small", packed, after), like)
    n_loc = hgrn_norm_g.shape[1]
    reduced[4] = lax.dynamic_slice(reduced[4], (0, _slot(_position()) * n_loc), (1, n_loc))
    ws = (norm_g, mem_norm_g, pool_scale, hgrn_lb, hgrn_norm_g, final_g)
    ms = (m_norm_g, m_mem_norm_g, m_pool_scale, m_hgrn_lb, m_hgrn_norm_g, m_final_g)
    vs = (v_norm_g, v_mem_norm_g, v_pool_scale, v_hgrn_lb, v_hgrn_norm_g, v_final_g)
    res = adamw_small("adamw_small", _pack(reduced), _pack(ws), _pack(ms), _pack(vs))
    outs = {nm: [reduced[i]] + [_unpack(r, ws)[i] for r in res] for i, nm in enumerate(names)}
    update("0a", 0, res[0])
    for nm, (w, _, _) in state.items():
        outs[nm] = [r.reshape(w.shape) for r in big[nm]]
    order = ("norm_g", "mem_norm_g", "w_kv", "w_out", "pool_w_in", "pool_w_grp", "pool_scale", "hgrn_w_in", "hgrn_lb",
             "hgrn_norm_g", "final_g")
    return (loss, grad_x[None], *[outs[nm][0] for nm in order], *[outs[nm][1] for nm in order],
            *[outs[nm][2] for nm in order], *[outs[nm][3] for nm in order])
```

```python
import functools

import jax
import jax.numpy as jnp
from jax import lax
from jax.experimental import pallas as pl
from jax.experimental.pallas import tpu as pltpu

F32, BF16 = jnp.float32, jnp.bfloat16
MM_DTYPE = BF16
COMM_DTYPE = BF16
EPS = 1e-6
POOL_WINDOWS = (2, 4, 8, 16)
POOL_HALO = 16
HG_HEAD_DIM = 128
HG_CHUNK = 64
CA_HEADS = 4
N_DEV = 8
ADAM_LR, ADAM_B1, ADAM_B2, ADAM_EPS, ADAM_WD, ADAM_STEP = 0.001, 0.9, 0.999, 1e-08, 0.01, 10
VMEM_LIMIT = 48 << 20
VMEM_LIMIT_TALL = 56 << 20
LANE = 128
MESH = pl.DeviceIdType.MESH
AXES = ("x", "y", "c")

NN = (((1,), (0,)), ((), ()))
NT = (((1,), (1,)), ((), ()))
TN = (((0,), (0,)), ((), ()))


def _cp(sem=None, vmem=VMEM_LIMIT):
    return pltpu.CompilerParams(dimension_semantics=sem, vmem_limit_bytes=vmem)


def _dot(a, b, dims):
    return lax.dot_general(a.astype(MM_DTYPE), b.astype(MM_DTYPE), dims, preferred_element_type=F32)


def _sig(v):
    return 1.0 / (1.0 + jnp.exp(-v))


def _tile(n, pref):
    if n <= pref:
        return n
    t = (pref // LANE) * LANE
    while n % t:
        t -= LANE
    return t


def _sds(shape, dtype):
    return jax.ShapeDtypeStruct(tuple(shape), dtype)


def _mm(name, grid, sem, ins, in_specs, out_shapes, out_specs, dims, epi=None, nk=1, acc_shape=None, aliases=None,
        vmem=VMEM_LIMIT):
    n_in, n_out = len(ins), len(out_shapes)

    def body(*refs):
        a_ref, b_ref = refs[0], refs[1]
        extra = refs[2:n_in]
        outs = refs[n_in:n_in + n_out]
        p = _dot(a_ref[...], b_ref[...], dims)

        def finish(acc):
            res = epi(acc, *extra) if epi is not None else (acc,)
            for r, o_ref in zip(res, outs):
                o_ref[...] = r.astype(o_ref.dtype)

        if nk == 1:
            finish(p)
        else:
            acc_ref = refs[n_in + n_out]
            k = pl.program_id(len(grid) - 1)

            @pl.when(k == 0)
            def _():
                acc_ref[...] = p

            @pl.when(k > 0)
            def _():
                acc_ref[...] += p

            @pl.when(k == nk - 1)
            def _():
                finish(acc_ref[...])

    scratch = [pltpu.VMEM(acc_shape, F32)] if nk > 1 else []
    return pl.pallas_call(
        body, name=name, grid=grid, in_specs=in_specs, out_specs=out_specs, out_shape=out_shapes,
        scratch_shapes=scratch, compiler_params=_cp(sem, vmem), input_output_aliases=aliases or {},
    )(*ins)


def proj_fwd(name, h, wblk, tm=1024):
    T, D = h.shape
    nblk, _, nb = wblk.shape
    tm = _tile(T, tm)
    return _mm(name, (nblk, T // tm), ("parallel", "parallel"), [h, wblk],
               [pl.BlockSpec((tm, D), lambda j, i: (i, 0)), pl.BlockSpec((None, D, nb), lambda j, i: (j, 0, 0))],
               [_sds((T, nblk * nb), F32)], [pl.BlockSpec((tm, nb), lambda j, i: (i, j))], NN)[0]


def proj_bwd_w(name, h_t, dproj, nblk, tm=1024, tn=1024):
    D, T = h_t.shape
    N = dproj.shape[1]
    nb = N // nblk
    tm, tn = _tile(D, tm), _tile(nb, tn)
    per = nb // tn
    return _mm(name, (D // tm, N // tn), ("parallel", "parallel"), [h_t, dproj],
               [pl.BlockSpec((tm, T), lambda i, n: (i, 0)), pl.BlockSpec((T, tn), lambda i, n: (0, n))],
               [_sds((nblk, D, nb), COMM_DTYPE)], [pl.BlockSpec((None, tm, tn), lambda i, n: (n // per, i, n % per))], NN)[0]


def proj_bwd_x(name, dproj, wblk, after, tm=1024):
    T, N = dproj.shape
    nblk, D, nb = wblk.shape
    tm = _tile(T, tm)
    return _mm(name, (T // tm, nblk), ("parallel", "arbitrary"), [dproj, wblk, jnp.reshape(after, (1, 1))],
               [pl.BlockSpec((tm, nb), lambda i, k: (i, k)), pl.BlockSpec((None, D, nb), lambda i, k: (k, 0, 0)),
                pl.BlockSpec(memory_space=pl.ANY)],
               [_sds((T, D), F32)], [pl.BlockSpec((tm, D), lambda i, k: (i, 0))], NT, nk=nblk, acc_shape=(tm, D),
               vmem=VMEM_LIMIT_TALL)[0]


def out_fwd(name, branch, wout, x, after, tm=1024, tn=512):
    T, E = branch.shape
    D = wout.shape[1]
    tm, tn = _tile(T, tm), _tile(D, tn)
    return _mm(name, (T // tm, D // tn), ("parallel", "parallel"), [branch, wout, x, jnp.reshape(after, (1, 1))],
               [pl.BlockSpec((tm, E), lambda i, j: (i, 0)), pl.BlockSpec((E, tn), lambda i, j: (0, j)),
                pl.BlockSpec((tm, tn), lambda i, j: (i, j)), pl.BlockSpec(memory_space=pl.ANY)],
               [_sds((T, D), F32)], [pl.BlockSpec((tm, tn), lambda i, j: (i, j))], NN,
               epi=lambda acc, x_ref, _: (acc + x_ref[...],))[0]


def mm_nt(name, a, b, out_dtype, tm=1024, tn=1024, res=None):
    M, K = a.shape
    N = b.shape[0]
    tm, tn = _tile(M, tm), _tile(N, tn)
    ins = [a, b] + ([res] if res is not None else [])
    specs = [pl.BlockSpec((tm, K), lambda i, j: (i, 0)), pl.BlockSpec((tn, K), lambda i, j: (j, 0))]
    if res is not None:
        specs.append(pl.BlockSpec((tm, tn), lambda i, j: (i, j)))
    epi = (lambda acc, r_ref: (acc + r_ref[...],)) if res is not None else None
    return _mm(name, (M // tm, N // tn), ("parallel", "parallel"), ins, specs,
               [_sds((M, N), out_dtype)], [pl.BlockSpec((tm, tn), lambda i, j: (i, j))], NT, epi=epi)[0]


def mm_nn(name, a, b, out_dtype, tm=512, tn=512):
    M, K = a.shape
    N = b.shape[1]
    tm, tn = _tile(M, tm), _tile(N, tn)
    return _mm(name, (M // tm, N // tn), ("parallel", "parallel"), [a, b],
               [pl.BlockSpec((tm, K), lambda i, j: (i, 0)), pl.BlockSpec((K, tn), lambda i, j: (0, j))],
               [_sds((M, N), out_dtype)], [pl.BlockSpec((tm, tn), lambda i, j: (i, j))], NN)[0]


def mm_tn(name, a, b, out_dtype, tm=512, tn=1024):
    K, M = a.shape
    N = b.shape[1]
    tm, tn = _tile(M, tm), _tile(N, tn)
    return _mm(name, (M // tm, N // tn), ("parallel", "parallel"), [a, b],
               [pl.BlockSpec((K, tm), lambda i, j: (0, i)), pl.BlockSpec((K, tn), lambda i, j: (0, j))],
               [_sds((M, N), out_dtype)], [pl.BlockSpec((tm, tn), lambda i, j: (i, j))], TN)[0]


def rmsnorm_fwd(name, x, g, tm=512, with_transpose=False):
    T, D = x.shape
    tm = _tile(T, tm)

    def body(x_ref, g_ref, o_ref, *t_ref):
        xf = x_ref[...]
        r = lax.rsqrt(jnp.mean(xf * xf, axis=-1, keepdims=True) + EPS)
        h = (xf * r) * g_ref[...]
        o_ref[...] = h.astype(o_ref.dtype)
        if with_transpose:
            t_ref[0][...] = h.T.astype(o_ref.dtype)

    row = pl.BlockSpec((tm, D), lambda i: (i, 0))
    outs = pl.pallas_call(
        body, name=name, grid=(T // tm,), in_specs=[row, pl.BlockSpec((1, D), lambda i: (0, 0))],
        out_specs=[row] + ([pl.BlockSpec((D, tm), lambda i: (0, i))] if with_transpose else []),
        out_shape=[_sds((T, D), MM_DTYPE)] + ([_sds((D, T), MM_DTYPE)] if with_transpose else []),
        compiler_params=_cp(("parallel",)),
    )(x, g.reshape(1, D))
    return outs if with_transpose else outs[0]


def _rms_bwd_math(xf, g, dh):
    r = lax.rsqrt(jnp.mean(xf * xf, axis=-1, keepdims=True) + EPS)
    gd = dh * g
    dx = r * gd - xf * ((r * r * r) * jnp.mean(gd * xf, axis=-1, keepdims=True))
    dg = jnp.sum(dh * (xf * r), axis=0, keepdims=True)
    return dx, dg


def rmsnorm_bwd(name, x, g, dh, res=None, tm=256):
    T, D = x.shape
    tm = _tile(T, tm)
    has_res = res is not None

    def body(*refs):
        x_ref, g_ref, dh_ref = refs[:3]
        dx_ref, dxb_ref, dg_ref = refs[3 + has_res:]
        dx, dg = _rms_bwd_math(x_ref[...], g_ref[...], dh_ref[...])
        if has_res:
            dx = dx + refs[3][...]
        dx_ref[...] = dx
        dxb_ref[...] = dx.astype(dxb_ref.dtype)

        @pl.when(pl.program_id(0) == 0)
        def _():
            dg_ref[...] = jnp.zeros_like(dg_ref)

        dg_ref[...] += dg

    row = pl.BlockSpec((tm, D), lambda i: (i, 0))
    vec = pl.BlockSpec((1, D), lambda i: (0, 0))
    ins = [x, g.reshape(1, D), dh] + ([res] if has_res else [])
    return pl.pallas_call(
        body, name=name, grid=(T // tm,), in_specs=[row, vec, row] + ([row] if has_res else []),
        out_specs=[row, row, vec], out_shape=[_sds((T, D), F32), _sds((T, D), MM_DTYPE), _sds((1, D), F32)],
        compiler_params=_cp(("arbitrary",)),
    )(*ins)


def final_loss_bwd(name, x, g, target, tm=256):
    T, D = x.shape
    tm = _tile(T, tm)

    def body(x_ref, g_ref, t_ref, loss_ref, dx_ref, dxb_ref, dg_ref):
        xf, gv = x_ref[...], g_ref[...]
        r = lax.rsqrt(jnp.mean(xf * xf, axis=-1, keepdims=True) + EPS)
        err = (xf * r) * gv - t_ref[...]
        part = 0.5 * jnp.sum(jnp.mean(err * err, axis=-1, keepdims=True), axis=0, keepdims=True)
        dx, dg = _rms_bwd_math(xf, gv, err / D)
        dx_ref[...] = dx
        dxb_ref[...] = dx.astype(dxb_ref.dtype)

        @pl.when(pl.program_id(0) == 0)
        def _():
            dg_ref[...] = jnp.zeros_like(dg_ref)
            loss_ref[...] = jnp.zeros_like(loss_ref)

        dg_ref[...] += dg
        loss_ref[...] += jnp.broadcast_to(part, loss_ref.shape)

    row = pl.BlockSpec((tm, D), lambda i: (i, 0))
    vec = pl.BlockSpec((1, D), lambda i: (0, 0))
    return pl.pallas_call(
        body, name=name, grid=(T // tm,), in_specs=[row, vec, row],
        out_specs=[pl.BlockSpec((8, LANE), lambda i: (0, 0)), row, row, vec],
        out_shape=[_sds((8, LANE), F32), _sds((T, D), F32), _sds((T, D), MM_DTYPE), _sds((1, D), F32)],
        compiler_params=_cp(("arbitrary",)),
    )(x, g.reshape(1, D), target)


def _pool_tiles(T, emix):
    pg = emix // len(POOL_WINDOWS)
    tc = 256 if pg % 256 == 0 else LANE
    return pg, tc, _tile(T, 512)


def pool_fwd(name, proj, T, emix, after):
    pg, tc, R = _pool_tiles(T, emix)
    per_g = pg // tc

    def body(u_ref, _, o_ref):
        g = pl.program_id(0) // per_g
        for gi, w in enumerate(POOL_WINDOWS):
            @pl.when(g == gi)
            def _():
                for ci in range(T // R):
                    r0 = ci * R
                    cur = u_ref[r0:r0 + R, :]
                    halo = jnp.zeros((POOL_HALO, tc), F32) if ci == 0 else u_ref[r0 - POOL_HALO:r0, :]
                    s = jnp.concatenate([halo, cur], axis=0)
                    for st in range(w.bit_length() - 1):
                        s = s + pltpu.roll(s, 1 << st, axis=0)
                    t = r0 + lax.broadcasted_iota(jnp.int32, (R, 1), 0)
                    cnt = jnp.minimum(t + 1, w).astype(F32)
                    o_ref[r0:r0 + R, :] = (s[POOL_HALO:] / cnt - cur).astype(o_ref.dtype)

    return pl.pallas_call(
        body, name=name, grid=(emix // tc,),
        in_specs=[pl.BlockSpec((T, tc), lambda j: (0, j)), pl.BlockSpec(memory_space=pl.ANY)],
        out_specs=pl.BlockSpec((T, tc), lambda j: (0, j)), out_shape=_sds((T, emix), MM_DTYPE),
        compiler_params=_cp(("parallel",)),
    )(proj, jnp.reshape(after, (1, 1)))


def pool_bwd(name, dpooled, dproj, T, emix):
    pg, tc, R = _pool_tiles(T, emix)
    per_g = pg // tc

    def body(d_ref, _, o_ref):
        g = pl.program_id(0) // per_g
        for gi, w in enumerate(POOL_WINDOWS):
            @pl.when(g == gi)
            def _():
                n = R + POOL_HALO
                for ci in range(T // R):
                    r0 = ci * R
                    cur = d_ref[r0:r0 + R, :]
                    halo = jnp.zeros((POOL_HALO, tc), F32) if ci == T // R - 1 else d_ref[r0 + R:r0 + n, :]
                    t = r0 + lax.broadcasted_iota(jnp.int32, (n, 1), 0)
                    cnt = jnp.minimum(t + 1, w).astype(F32)
                    s = jnp.concatenate([cur, halo], axis=0) / cnt
                    for st in range(w.bit_length() - 1):
                        s = s + pltpu.roll(s, n - (1 << st), axis=0)
                    o_ref[r0:r0 + R, :] = (s[:R] - cur).astype(o_ref.dtype)

    return pl.pallas_call(
        body, name=name, grid=(emix // tc,),
        in_specs=[pl.BlockSpec((T, tc), lambda j: (0, j)), pl.BlockSpec(memory_space=pl.ANY)],
        out_specs=pl.BlockSpec((T, tc), lambda j: (0, j)), out_shape=_sds(dproj.shape, dproj.dtype),
        input_output_aliases={1: 0}, compiler_params=_cp(("parallel",)),
    )(dpooled, dproj)


def grp_fwd(name, pooled, wgrp, scale, proj, gate_off, E):
    T, emix = pooled.shape
    pg = emix // len(POOL_WINDOWS)
    tm = _tile(T, 2048)
    tc = 256 if pg % 256 == 0 and gate_off % 256 == 0 else LANE
    per = pg // tc

    def epi(acc, s_ref, g_ref):
        gt = g_ref[...]
        return acc, (acc * s_ref[...]) * (gt * _sig(gt))

    col = pl.BlockSpec((tm, tc), lambda n, i: (i, n))
    return _mm(name, (emix // tc, T // tm), ("parallel", "parallel"), [pooled, wgrp, scale, proj],
               [pl.BlockSpec((tm, pg), lambda n, i: (i, n // per)), pl.BlockSpec((None, pg, tc), lambda n, i: (n // per, 0, n % per)),
                pl.BlockSpec((1, tc), lambda n, i: (0, n)), pl.BlockSpec((tm, tc), lambda n, i: (i, gate_off // tc + n))],
               [_sds((T, emix), F32), _sds((T, E), MM_DTYPE)], [col, col], NN, epi=epi)


def grp_bwd_x(name, dy, wgrp):
    T, emix = dy.shape
    pg = emix // len(POOL_WINDOWS)
    tm = _tile(T, 1024)
    return _mm(name, (len(POOL_WINDOWS), T // tm), ("parallel", "parallel"), [dy, wgrp],
               [pl.BlockSpec((tm, pg), lambda g, i: (i, g)), pl.BlockSpec((None, pg, pg), lambda g, i: (g, 0, 0))],
               [_sds((T, emix), F32)], [pl.BlockSpec((tm, pg), lambda g, i: (i, g))], NT)[0]


def grp_bwd_w(name, pooled, dy):
    T, emix = dy.shape
    ng = len(POOL_WINDOWS)
    pg = emix // ng
    col = pl.BlockSpec((T, pg), lambda g: (0, g))
    return _mm(name, (ng,), ("parallel",), [pooled, dy], [col, col],
               [_sds((ng, pg, pg), COMM_DTYPE)], [pl.BlockSpec((None, pg, pg), lambda g: (g, 0, 0))], TN)[0]


def gate_bwd_pool(name, dbranch, y, scale, proj, gate_off, n_proj):
    T, emix = y.shape
    tm, tc = _tile(T, 1024), _tile(emix, 512)
    assert gate_off % tc == 0

    def body(db_ref, y_ref, s_ref, g_ref, dy_ref, dg_ref, ds_ref):
        db, yv, sc, gt = db_ref[...], y_ref[...], s_ref[...], g_ref[...]
        sg = _sig(gt)
        dmix = db * (gt * sg)
        dy_ref[...] = (dmix * sc).astype(dy_ref.dtype)
        dg_ref[...] = (db * (yv * sc) * (sg * (1.0 + gt * (1.0 - sg)))).astype(dg_ref.dtype)

        @pl.when(pl.program_id(1) == 0)
        def _():
            ds_ref[...] = jnp.zeros_like(ds_ref)

        ds_ref[...] += jnp.sum(dmix * yv, axis=0, keepdims=True)

    blk = pl.BlockSpec((tm, tc), lambda j, i: (i, j))
    vec = pl.BlockSpec((1, tc), lambda j, i: (0, j))
    gate = pl.BlockSpec((tm, tc), lambda j, i: (i, gate_off // tc + j))
    return pl.pallas_call(
        body, name=name, grid=(emix // tc, T // tm), in_specs=[blk, blk, vec, gate],
        out_specs=[blk, gate, vec],
        out_shape=[_sds((T, emix), MM_DTYPE), _sds((T, n_proj), MM_DTYPE), _sds((1, emix), F32)],
        compiler_params=_cp(("parallel", "arbitrary")),
    )(dbranch, y, scale, proj)


def attn_fwd(name, proj, kv, branch, q_off, gate_off, tm=1024):
    T = proj.shape[0]
    M, two_eca = kv.shape
    eca = two_eca // 2
    hd = eca // CA_HEADS
    E = branch.shape[1]
    tm = _tile(T, tm)
    scale = hd ** -0.5

    def body(q_ref, g_ref, k_ref, v_ref, _, ca_ref, br_ref):
        s = _dot(q_ref[...], k_ref[...], NT) * scale
        s = s - jnp.max(s, axis=-1, keepdims=True)
        e = jnp.exp(s)
        p = e / jnp.sum(e, axis=-1, keepdims=True)
        o = _dot(p, v_ref[...], NN)
        gt = g_ref[...]
        ca_ref[...] = o
        br_ref[...] = (o * (gt * _sig(gt))).astype(br_ref.dtype)

    return pl.pallas_call(
        body, name=name, grid=(CA_HEADS, T // tm),
        in_specs=[pl.BlockSpec((tm, hd), lambda h, i: (i, q_off // hd + h)),
                  pl.BlockSpec((tm, hd), lambda h, i: (i, (gate_off + E - eca) // hd + h)),
                  pl.BlockSpec((M, hd), lambda h, i: (0, h)), pl.BlockSpec((M, hd), lambda h, i: (0, CA_HEADS + h)),
                  pl.BlockSpec(memory_space=pl.ANY)],
        out_specs=[pl.BlockSpec((tm, hd), lambda h, i: (i, h)), pl.BlockSpec((tm, hd), lambda h, i: (i, (E - eca) // hd + h))],
        out_shape=[_sds((T, eca), F32), _sds(branch.shape, branch.dtype)],
        input_output_aliases={4: 1}, compiler_params=_cp(("parallel", "parallel")),
    )(proj, proj, kv, kv, branch)


def attn_bwd(name, proj, kv, ca, dbranch, dproj, q_off, gate_off, tm=1024):
    T = proj.shape[0]
    M, two_eca = kv.shape
    eca = two_eca // 2
    hd = eca // CA_HEADS
    E = dbranch.shape[1]
    tm = _tile(T, tm)
    scale = hd ** -0.5
    q_blk, g_blk = q_off // hd, (gate_off + E - eca) // hd

    n_rows = T // tm
    n_steps = CA_HEADS * n_rows

    def body(q_ref, g_ref, k_ref, v_ref, ca_ref, db_ref, _, dp_hbm, dk_ref, dv_ref, stash, sems):
        h, i = pl.program_id(0), pl.program_id(1)
        step = h * n_rows + i
        slot = step % 2

        def writes(slot, h, i):
            rows = pl.ds(pl.multiple_of(i * tm, tm), tm)
            return [pltpu.make_async_copy(stash.at[slot, r], dp_hbm.at[rows, pl.ds(pl.multiple_of((blk + h) * hd, hd), hd)],
                                          sems.at[slot, r]) for r, blk in enumerate((q_blk, g_blk))]

        @pl.when(step >= 2)
        def _():
            for cp in writes(slot, h, i):
                cp.wait()

        q, k, v = q_ref[...], k_ref[...], v_ref[...]
        s = _dot(q, k, NT) * scale
        s = s - jnp.max(s, axis=-1, keepdims=True)
        e = jnp.exp(s)
        p = e / jnp.sum(e, axis=-1, keepdims=True)
        gt, db = g_ref[...], db_ref[...]
        sg = _sig(gt)
        do = db * (gt * sg)
        stash[slot, 1] = (db * ca_ref[...] * (sg * (1.0 + gt * (1.0 - sg)))).astype(stash.dtype)
        dpr = _dot(do, v, NT)
        ds = p * (dpr - jnp.sum(dpr * p, axis=-1, keepdims=True)) * scale
        stash[slot, 0] = _dot(ds, k, NN).astype(stash.dtype)

        @pl.when(i == 0)
        def _():
            dk_ref[...] = jnp.zeros_like(dk_ref)
            dv_ref[...] = jnp.zeros_like(dv_ref)

        dk_ref[...] += _dot(ds, q, TN)
        dv_ref[...] += _dot(p, do, TN)
        for cp in writes(slot, h, i):
            cp.start()

        @pl.when(step == n_steps - 1)
        def _():
            for cp in writes(slot, h, i):
                cp.wait()
            if n_steps > 1:
                for cp in writes(1 - slot, h, i):
                    cp.wait()

    row = lambda blk: pl.BlockSpec((tm, hd), lambda h, i: (i, blk + h))
    outs = pl.pallas_call(
        body, name=name, grid=(CA_HEADS, n_rows),
        in_specs=[row(q_blk), row(g_blk), pl.BlockSpec((M, hd), lambda h, i: (0, h)),
                  pl.BlockSpec((M, hd), lambda h, i: (0, CA_HEADS + h)), row(0), row((E - eca) // hd),
                  pl.BlockSpec(memory_space=pl.ANY)],
        out_specs=[pl.BlockSpec(memory_space=pl.ANY), pl.BlockSpec((M, hd), lambda h, i: (0, h)),
                   pl.BlockSpec((M, hd), lambda h, i: (0, h))],
        out_shape=[_sds(dproj.shape, dproj.dtype), _sds((M, eca), F32), _sds((M, eca), F32)],
        scratch_shapes=[pltpu.VMEM((2, 2, tm, hd), dproj.dtype), pltpu.SemaphoreType.DMA((2, 2))],
        input_output_aliases={6: 0}, compiler_params=_cp(("arbitrary", "arbitrary")),
    )(proj, proj, kv, kv, ca, dbranch, dproj)
    return outs[0], jnp.concatenate([outs[1], outs[2]], axis=1)


def _split3(v):
    hi = v.astype(BF16)
    r1 = v - hi.astype(F32)
    mid = r1.astype(BF16)
    lo = (r1 - mid.astype(F32)).astype(BF16)
    return hi, mid, lo


def _tri_dot(tri, v):
    hi, mid, lo = _split3(v)
    d = functools.partial(lax.dot_general, dimension_numbers=NN, preferred_element_type=F32)
    return d(tri, hi) + d(tri, mid) + d(tri, lo)


def _causal_mask():
    C = HG_CHUNK
    return lax.broadcasted_iota(jnp.int32, (C, C), 0) >= lax.broadcasted_iota(jnp.int32, (C, C), 1)


def _cumsum_matrices(G):
    ri = lax.broadcasted_iota(jnp.int32, (G, G), 0)
    ci = lax.broadcasted_iota(jnp.int32, (G, G), 1)
    same = (ri // HG_CHUNK) == (ci // HG_CHUNK)
    return (jnp.where(same & (ri >= ci), 1.0, 0.0).astype(BF16), jnp.where(same & (ri <= ci), 1.0, 0.0).astype(BF16))


def _forget_gate(fi, lb):
    sig = _sig(fi)
    return sig, lb + (1.0 - lb) * sig


def _log_decay_cumsum(f_ref, b_s, lb, prefix, G):
    def group(i, carry):
        rows = pl.ds(pl.multiple_of(i * G, G), G)
        _, f = _forget_gate(f_ref[rows, :], lb)
        b_s[rows, :] = _tri_dot(prefix, jnp.log(f))
        return carry

    lax.fori_loop(0, b_s.shape[0] // G, group, 0)


def _lower_bound(lb2):
    l0, l1 = lb2[0:1, :], lb2[1:2, :]
    m = jnp.maximum(l0, l1)
    e0, e1 = jnp.exp(l0 - m), jnp.exp(l1 - m)
    sm0, sm1 = e0 / (e0 + e1), e1 / (e0 + e1)
    return (sm0 + sm1) - sm0, sm0, sm1


def _hg_chunk_fwd(qi, fi, lb, causal, b=None, prefix=None):
    sig, f = _forget_gate(fi, lb)
    k = 1.0 - f
    if b is None:
        b = _tri_dot(prefix, jnp.log(f))
    last = lax.broadcasted_iota(jnp.int32, b.shape, 0) == HG_CHUNK - 1
    bl = jnp.sum(jnp.where(last, b, 0.0), axis=0, keepdims=True)
    eb, enb, ete = jnp.exp(b), jnp.exp(-b), jnp.exp(bl - b)
    sq = _sig(qi)
    q = (qi * sq) * (HG_HEAD_DIM ** -0.5)
    q_dec, k_inv, k_te = q * eb, k * enb, k * ete
    att = jnp.where(causal, _dot(q_dec, k_inv, NT), 0.0)
    return dict(sig=sig, f=f, k=k, eb=eb, enb=enb, ete=ete, dcy=jnp.exp(bl), sq=sq, q_dec=q_dec, k_inv=k_inv, k_te=k_te,
                att=att, last=last)


def _hg_blocking(H, T):
    hb = 4 if H % 4 == 0 else (2 if H % 2 == 0 else 1)
    return hb, _tile(T, 1024)


def hgrn_fwd(name, proj, lb2, norm_g, emix, gate_off, E):
    T = proj.shape[0]
    hd, C = HG_HEAD_DIM, HG_CHUNK
    H, N = emix // hd, T // C
    hb, ts = _hg_blocking(H, T)
    W, nc, G = hb * hd, ts // C, _tile(ts, 256)

    def body(q_ref, f_ref, i_ref, g_ref, lb_ref, ng_ref, o_ref, st_ref, br_ref, state, b_s):
        @pl.when(pl.program_id(1) == 0)
        def _():
            state[...] = jnp.zeros_like(state)

        causal = _causal_mask()
        lb, _, _ = _lower_bound(lb_ref[...])
        ng = ng_ref[...]
        _log_decay_cumsum(f_ref, b_s, lb, _cumsum_matrices(G)[0], G)

        def chunk(n, carry):
            rows = pl.ds(pl.multiple_of(n * C, C), C)
            for h in range(hb):
                cols = slice(h * hd, (h + 1) * hd)
                v = i_ref[rows, cols]
                c = _hg_chunk_fwd(q_ref[rows, cols], f_ref[rows, cols], lb[:, cols], causal, b=b_s[rows, cols])
                st = state[h]
                o = _dot(c["att"], v, NN) + _dot(c["q_dec"], st, NT)
                st_ref[h, n] = st
                o_ref[rows, cols] = o
                r = lax.rsqrt(jnp.mean(o * o, axis=-1, keepdims=True) + EPS)
                gt = g_ref[rows, cols]
                br_ref[rows, cols] = (((o * r) * ng[:, cols]) * (gt * _sig(gt))).astype(br_ref.dtype)
                state[h] = st * c["dcy"] + _dot(v, c["k_te"], TN)
            return carry

        lax.fori_loop(0, nc, chunk, 0, unroll=8)

    col = lambda off: pl.BlockSpec((ts, W), lambda h, s: (s, off // W + h))
    vec = lambda r: pl.BlockSpec((r, W), lambda h, s: (0, h))
    return pl.pallas_call(
        body, name=name, grid=(H // hb, T // ts),
        in_specs=[col(0), col(emix), col(2 * emix), col(gate_off), vec(2), vec(1)],
        out_specs=[col(0), pl.BlockSpec((hb, nc, hd, hd), lambda h, s: (h, s, 0, 0)), col(0)],
        out_shape=[_sds((T, emix), F32), _sds((H, N, hd, hd), F32), _sds((T, E), MM_DTYPE)],
        scratch_shapes=[pltpu.VMEM((hb, hd, hd), F32), pltpu.VMEM((ts, W), F32)],
        compiler_params=_cp(("parallel", "arbitrary")),
    )(proj, proj, proj, proj, lb2, norm_g)


def hgrn_bwd(name, proj, lb2, norm_g, o_pre, states, dbranch, emix, gate_off):
    T, n_proj = proj.shape
    hd, C = HG_HEAD_DIM, HG_CHUNK
    H, N = emix // hd, T // C
    hb, ts = _hg_blocking(H, T)
    W, nc, ns = hb * hd, ts // C, T // ts
    qscale = HG_HEAD_DIM ** -0.5

    n_steps = (H // hb) * ns
    regions = (0, emix, 2 * emix, gate_off)

    def body(q_ref, f_ref, i_ref, g_ref, lb_ref, ng_ref, o_ref, st_ref, db_ref, dp_hbm, dlb_ref, dng_ref,
             stash, dst_ref, acc, sems):
        hblk, s = pl.program_id(0), pl.program_id(1)
        step = hblk * ns + s
        slot = step % 2

        def writes(slot, hblk, s):
            rows = pl.ds(pl.multiple_of((ns - 1 - s) * ts, ts), ts)
            return [pltpu.make_async_copy(stash.at[slot, r], dp_hbm.at[rows, pl.ds(pl.multiple_of(off + hblk * W, W), W)],
                                          sems.at[slot, r]) for r, off in enumerate(regions)]

        @pl.when(step >= 2)
        def _():
            for cp in writes(slot, hblk, s):
                cp.wait()

        @pl.when(s == 0)
        def _():
            dst_ref[...] = jnp.zeros_like(dst_ref)
            acc[...] = jnp.zeros_like(acc)

        causal = _causal_mask()
        prefix, suffix = _cumsum_matrices(C)
        lb, sm0, sm1 = _lower_bound(lb_ref[...])
        ng = ng_ref[...]

        def chunk(j, carry):
            n = nc - 1 - j
            rows = pl.ds(pl.multiple_of(n * C, C), C)
            for h in range(hb):
                cols = slice(h * hd, (h + 1) * hd)
                dst = dst_ref[h]
                qi, v, gt = q_ref[rows, cols], i_ref[rows, cols], g_ref[rows, cols]
                c = _hg_chunk_fwd(qi, f_ref[rows, cols], lb[:, cols], causal, prefix=prefix)
                st = st_ref[h, n]
                o, db, ngh = o_ref[rows, cols], db_ref[rows, cols], ng[:, cols]
                r = lax.rsqrt(jnp.mean(o * o, axis=-1, keepdims=True) + EPS)
                sg = _sig(gt)
                dmix = db * (gt * sg)
                dgate = db * ((o * r) * ngh) * (sg * (1.0 + gt * (1.0 - sg)))
                dyn = dmix * ngh
                do = r * dyn - o * ((r * r * r) * jnp.mean(dyn * o, axis=-1, keepdims=True))
                acc[1:2, cols] += jnp.sum(dmix * (o * r), axis=0, keepdims=True)
                datt = jnp.where(causal, _dot(do, v, NT), 0.0)
                dq_dec = _dot(datt, c["k_inv"], NN) + _dot(do, st, NN)
                dk_inv = _dot(datt, c["q_dec"], TN)
                dv = _dot(c["att"], do, TN) + _dot(c["k_te"], dst, NT)
                dk_te = _dot(v, dst, NN)
                ddcy = jnp.sum(dst * st, axis=0, keepdims=True)
                dst_ref[h] = _dot(do, c["q_dec"], TN) + dst * c["dcy"]
                t_te = dk_te * c["k_te"]
                dbv = dq_dec * c["q_dec"] - dk_inv * c["k_inv"] - t_te
                dbl = jnp.sum(t_te, axis=0, keepdims=True) + ddcy * c["dcy"]
                dbv = dbv + jnp.where(c["last"], dbl, 0.0)
                dlf = _tri_dot(suffix, dbv)
                dk = dk_inv * c["enb"] + dk_te * c["ete"]
                df = dlf / c["f"] - dk
                sig, sq = c["sig"], c["sq"]
                acc[0:1, cols] += jnp.sum(df * (1.0 - sig), axis=0, keepdims=True)
                dqi = (dq_dec * c["eb"]) * qscale * (sq * (1.0 + qi * (1.0 - sq)))
                stash[slot, 0, rows, cols] = dqi.astype(stash.dtype)
                stash[slot, 1, rows, cols] = (df * (1.0 - lb[:, cols]) * (sig * (1.0 - sig))).astype(stash.dtype)
                stash[slot, 2, rows, cols] = dv.astype(stash.dtype)
                stash[slot, 3, rows, cols] = dgate.astype(stash.dtype)
            return carry

        lax.fori_loop(0, nc, chunk, 0, unroll=8)
        t = acc[0:1, :] * (sm0 * sm1)
        dlb_ref[...] = jnp.concatenate([-t, t], axis=0)
        dng_ref[...] = acc[1:2, :]
        for cp in writes(slot, hblk, s):
            cp.start()

        @pl.when(step == n_steps - 1)
        def _():
            for cp in writes(slot, hblk, s):
                cp.wait()
            if n_steps > 1:
                for cp in writes(1 - slot, hblk, s):
                    cp.wait()

    col = lambda off: pl.BlockSpec((ts, W), lambda h, s: (ns - 1 - s, off // W + h))
    vec = lambda r: pl.BlockSpec((r, W), lambda h, s: (0, h))
    return pl.pallas_call(
        body, name=name, grid=(H // hb, ns),
        in_specs=[col(0), col(emix), col(2 * emix), col(gate_off), vec(2), vec(1), col(0),
                  pl.BlockSpec((hb, nc, hd, hd), lambda h, s: (h, ns - 1 - s, 0, 0)), col(0)],
        out_specs=[pl.BlockSpec(memory_space=pl.ANY), vec(2), vec(1)],
        out_shape=[_sds((T, n_proj), MM_DTYPE), _sds((2, emix), F32), _sds((1, emix), F32)],
        scratch_shapes=[pltpu.VMEM((2, 4, ts, W), MM_DTYPE), pltpu.VMEM((hb, hd, hd), F32), pltpu.VMEM((8, W), F32),
                        pltpu.SemaphoreType.DMA((2, 4))],
        compiler_params=_cp(("arbitrary", "arbitrary")),
    )(proj, proj, proj, proj, lb2, norm_g, o_pre, states, dbranch)


class WholeWeights:
    def __init__(self, wkv, wout, wpin, wgrp, whin, hgrn_norm_g):
        self.w = dict(wkv=wkv, wout=wout, wpin=wpin, wgrp=wgrp, whin=whin, hnorm=hgrn_norm_g)
        self.grads = {}

    def first_weights(self):
        return self.w["wpin"], self.w["hnorm"], 0.0

    def after_proj0(self, after):
        return 0.0

    def layer0_weights(self, after):
        return self.w["wgrp"], self.w["wkv"][0], self.w["wout"][0]

    def after_out0(self, after):
        return 0.0

    def layer1_weights(self, after):
        return self.w["whin"]

    def after_proj1(self, after):
        return 0.0

    def layer1_rest(self, after):
        return self.w["wkv"][1], self.w["wout"][1]

    def grads_ready(self, layer, grads):
        self.grads[layer] = grads
        return 0.0

    def grads_reduce(self, layer, after):
        return 0.0


def run_step(x, mem, target, norm_g, mem_norm_g, pool_scale, hgrn_lb, final_g, comm):
    T, D = x.shape
    mem_n = rmsnorm_fwd("mem_norm", mem, mem_norm_g)
    wpin, hgrn_norm_g, zero = comm.first_weights()
    h0, h0_t = rmsnorm_fwd("norm0", x, norm_g[0] + zero, with_transpose=True)
    proj0 = proj_fwd("proj0", h0, wpin)
    zero = comm.after_proj0(proj0)
    n0 = proj0.shape[1]
    E = n0 // 2
    eca = E // 4
    emix = E - eca
    q0, g0 = emix, emix + eca
    q1, g1 = 3 * emix, 3 * emix + eca
    pooled = pool_fwd("pool_fwd", proj0, T, emix, zero)
    wgrp, wkv0, wout0 = comm.layer0_weights(pooled)
    y0, branch0 = grp_fwd("grp_fwd", pooled, wgrp, pool_scale, proj0, g0, E)
    kv0 = mm_nn("kv0", mem_n, wkv0, MM_DTYPE)
    ca0, branch0 = attn_fwd("attn_fwd0", proj0, kv0, branch0, q0, g0)
    x1 = out_fwd("out0", branch0, wout0, x, ca0[:1, :1])
    h1, h1_t = rmsnorm_fwd("norm1", x1, norm_g[1] + comm.after_out0(x1), with_transpose=True)
    whin = comm.layer1_weights(h1)
    proj1 = proj_fwd("proj1", h1, whin)
    zero = comm.after_proj1(proj1)
    o1, states, branch1 = hgrn_fwd("hgrn_fwd", proj1, hgrn_lb + zero, hgrn_norm_g, emix, g1, E)
    wkv1, wout1 = comm.layer1_rest(o1)
    kv1 = mm_nn("kv1", mem_n, wkv1, MM_DTYPE)
    ca1, branch1 = attn_fwd("attn_fwd1", proj1, kv1, branch1, q1, g1)
    x2 = out_fwd("out1", branch1, wout1, x1, ca1[:1, :1])
    loss, dx2, dx2b, g_final = final_loss_bwd("final", x2, final_g, target)
    g_wout1 = mm_tn("gwout1", branch1, dx2b, COMM_DTYPE)
    dbranch1 = mm_nt("dbranch1", dx2b, wout1, F32)
    dproj1, g_lb, g_hnorm = hgrn_bwd("hgrn_bwd", proj1, hgrn_lb, hgrn_norm_g, o1, states, dbranch1, emix, g1)
    dproj1, dkv1 = attn_bwd("attn_bwd1", proj1, kv1, ca1, dbranch1, dproj1, q1, g1)
    g_wkv1 = mm_tn("gwkv1", mem_n, dkv1, COMM_DTYPE)
    dmem_n = mm_nt("dmem1", dkv1, wkv1, F32)
    g_whin = proj_bwd_w("gwhin", h1_t, dproj1, whin.shape[0])
    zero = comm.grads_ready(1, dict(w_kv=g_wkv1, w_out=g_wout1, hgrn_w_in=g_whin))
    dh1 = proj_bwd_x("dh1", dproj1, whin, zero)
    zero = zero + comm.grads_reduce(1, dh1)
    dx1, dx1b, g_norm1 = rmsnorm_bwd("norm1_bwd", x1, norm_g[1] + zero, dh1, res=dx2)
    dbranch0 = mm_nt("dbranch0", dx1b, wout0, F32)
    dy0, dproj0, g_scale = gate_bwd_pool("gate_bwd0", dbranch0, y0, pool_scale, proj0, g0, n0)
    dpooled = grp_bwd_x("dpooled", dy0, wgrp)
    dproj0 = pool_bwd("pool_bwd", dpooled, dproj0, T, emix)
    dproj0, dkv0 = attn_bwd("attn_bwd0", proj0, kv0, ca0, dbranch0, dproj0, q0, g0)
    g_wpin = proj_bwd_w("gwpin", h0_t, dproj0, wpin.shape[0])
    dkv0 = dkv0 + comm.grads_ready("0b", dict(pool_w_in=g_wpin))
    g_wkv0 = mm_tn("gwkv0", mem_n, dkv0, COMM_DTYPE)
    dmem_n = mm_nt("dmem0", dkv0, wkv0, F32, res=dmem_n)
    g_wout0 = mm_tn("gwout0", branch0, dx1b, COMM_DTYPE)
    g_wgrp = grp_bwd_w("gwgrp", pooled, dy0)
    zero = comm.grads_reduce("0b", g_wgrp)
    zero = zero + comm.grads_ready("0a", dict(w_kv=g_wkv0, w_out=g_wout0, pool_w_grp=g_wgrp))
    dh0 = proj_bwd_x("dh0", dproj0, wpin, zero)
    zero = zero + comm.grads_reduce("0a", dh0)
    grad_x, _, g_norm0 = rmsnorm_bwd("norm0_bwd", x, norm_g[0] + zero, dh0, res=dx1)
    _, _, g_mem = rmsnorm_bwd("mem_norm_bwd", mem, mem_norm_g, dmem_n)

    small = dict(norm_g=jnp.concatenate([g_norm0, g_norm1], axis=0), mem_norm_g=g_mem[0], pool_scale=g_scale,
                 hgrn_lb=g_lb, hgrn_norm_g=g_hnorm, final_g=g_final[0])
    return loss[0, 0], grad_x, small


def _position():
    return lax.axis_index("x"), lax.axis_index("y"), lax.axis_index("c")


def _slot(p):
    return 4 * p[0] + 2 * p[1] + p[2]


def all_gather_blocks(name, blocks):
    n = len(blocks)
    halved = [b.shape[0] % 32 == 0 for b in blocks]

    def body(*refs):
        ins, outs, token = refs[:n], refs[n:2 * n], refs[2 * n]
        send_sems, recv_sems, local_sems = refs[2 * n + 1:]
        token[...] = jnp.zeros_like(token)
        x, y, c = _position()
        me, sibling = (x, y, c), (x, y, 1 - c)
        x_nbr, y_nbr, diag = _other_chips(x, y)

        def copy(t, k, block, to, src=None, rows=None):
            dst = outs[t].at[_slot(block)]
            if rows is not None:
                dst = dst.at[rows]
            return pltpu.make_async_remote_copy(src_ref=dst if src is None else src, dst_ref=dst, send_sem=send_sems.at[t, k],
                                                recv_sem=recv_sems.at[t, k], device_id=to, device_id_type=MESH)

        mine = [pltpu.make_async_copy(ins[t], outs[t].at[_slot(me)], local_sems.at[t]) for t in range(n)]
        for cp in mine:
            cp.start()
        sends = []
        for t in range(n):
            sends += [copy(t, 0, me, sibling, src=ins[t]), copy(t, 1, me, (*x_nbr, c), src=ins[t]),
                      copy(t, 2, me, (*y_nbr, c), src=ins[t])]
            if not halved[t]:
                sends.append(copy(t, 4, me, (*diag, c), src=ins[t]))
        for cp in sends:
            cp.start()

        def start(cp):
            cp.start()
            sends.append(cp)

        for t in range(n):
            half = blocks[t].shape[0] // 2
            top, bottom = pl.ds(0, half), pl.ds(half, half)
            copy(t, 1, (*x_nbr, c), me).wait_recv()
            if halved[t]:
                start(copy(t, 4, (*x_nbr, c), (*y_nbr, c), rows=top))
            start(copy(t, 3, (*x_nbr, c), sibling))
            copy(t, 2, (*y_nbr, c), me).wait_recv()
            if halved[t]:
                start(copy(t, 6, (*y_nbr, c), (*x_nbr, c), rows=bottom))
            start(copy(t, 5, (*y_nbr, c), sibling))
            if halved[t]:
                copy(t, 4, (*diag, c), me, rows=top).wait_recv()
                copy(t, 6, (*diag, c), me, rows=bottom).wait_recv()
            else:
                copy(t, 4, (*diag, c), me).wait_recv()
            start(copy(t, 7, (*diag, c), sibling))
        for t in range(n):
            copy(t, 0, sibling, me).wait_recv()
            for k, chip in ((3, x_nbr), (5, y_nbr), (7, diag)):
                copy(t, k, (*chip, 1 - c), me).wait_recv()
        for cp in sends:
            cp.wait_send()
        for cp in mine:
            cp.wait()

    any_spec = pl.BlockSpec(memory_space=pl.ANY)
    return pl.pallas_call(
        body, name=name, in_specs=[any_spec] * n, out_specs=[any_spec] * n + [pl.BlockSpec(memory_space=pltpu.VMEM)],
        out_shape=[_sds((N_DEV,) + b.shape, b.dtype) for b in blocks] + [_sds((8, LANE), F32)],
        scratch_shapes=[pltpu.SemaphoreType.DMA((n, 8)), pltpu.SemaphoreType.DMA((n, 8)), pltpu.SemaphoreType.DMA((n,))],
        compiler_params=pltpu.CompilerParams(has_side_effects=True),
    )(*blocks)


_HBM = pl.BlockSpec(memory_space=pltpu.HBM)
_SEM = pl.BlockSpec(memory_space=pltpu.SEMAPHORE)
_EFFECT = pltpu.SideEffectType.DATAFLOW_SIDE_EFFECTING


def split_start(name, bufs, n_copies, build):
    n = len(bufs)

    def body(*refs):
        for cp in build(refs[:n], refs[n], refs[n + 1]):
            cp.start()
        refs[-1][...] = jnp.zeros_like(refs[-1])

    outs = pl.pallas_call(
        body, name=name, in_specs=[_HBM] * n,
        out_shape=(pltpu.SemaphoreType.DMA((n_copies,)), pltpu.SemaphoreType.DMA((n_copies,)),
                   *[pltpu.HBM(b.shape, b.dtype) for b in bufs], _sds((8, LANE), F32)),
        out_specs=(_SEM, _SEM, *[_HBM] * n, pl.BlockSpec(memory_space=pltpu.VMEM)),
        input_output_aliases={i: 2 + i for i in range(n)},
        compiler_params=pltpu.CompilerParams(has_side_effects=_EFFECT),
    )(*[pltpu.with_memory_space_constraint(b, pltpu.HBM) for b in bufs])
    return (outs[0], outs[1]), list(outs[2:2 + n]), outs[-1]


def split_wait(name, sems, bufs, build, after):
    n = len(bufs)

    def body(*refs):
        for cp in build(refs[:n], refs[n], refs[n + 1]):
            cp.wait()

    return list(pl.pallas_call(
        body, name=name, in_specs=[_HBM] * n + [_SEM, _SEM, pl.BlockSpec(memory_space=pl.ANY)],
        out_shape=[pltpu.HBM(b.shape, b.dtype) for b in bufs], out_specs=[_HBM] * n,
        input_output_aliases={i: i for i in range(n)},
        compiler_params=pltpu.CompilerParams(has_side_effects=_EFFECT),
    )(*bufs, sems[0], sems[1], after))


def _remote(src, dst, send_sems, recv_sems, k, to):
    return pltpu.make_async_remote_copy(src_ref=src, dst_ref=dst, send_sem=send_sems.at[k], recv_sem=recv_sems.at[k],
                                        device_id=to, device_id_type=MESH)


def _other_chips(x, y):
    return [(1 - x, y), (x, 1 - y), (1 - x, 1 - y)]


def _gather_cross(n):
    def build(refs, ss, rs):
        x, y, c = _position()
        me = _slot((x, y, c))
        targets = [(x, y, 1 - c)] + [(*chip, c) for chip in _other_chips(x, y)]
        return [_remote(refs[t], refs[n + t].at[me], ss, rs, 4 * t + k, to) for t in range(n) for k, to in enumerate(targets)]
    return build


def _gather_pass(n):
    def build(refs, ss, rs):
        x, y, c = _position()
        cps = []
        for t in range(n):
            for j, chip in enumerate(_other_chips(x, y)):
                blk = refs[t].at[_slot((*chip, c))]
                cps.append(_remote(blk, blk, ss, rs, 3 * t + j, (x, y, 1 - c)))
        return cps
    return build


def _reduce_pair(n):
    def build(refs, ss, rs):
        x, y, c = _position()
        return [_remote(refs[t].at[j, 1 - c], refs[n + t].at[j], ss, rs, 4 * t + j, (x, y, 1 - c))
                for t in range(n) for j in range(4)]
    return build


def _reduce_cross(n):
    def build(refs, ss, rs):
        x, y, c = _position()
        return [_remote(refs[t].at[2 * chip[0] + chip[1]], refs[n + t].at[r], ss, rs, 3 * t + r, (*chip, c))
                for t in range(n) for r, chip in enumerate(_other_chips(x, y))]
    return build


def pair_add(name, part, landed, tr=1024):
    _, _, R, C = part.shape
    tr = _row_tile(R, tr)

    def body(_, p_ref, l_ref, o_ref):
        o_ref[...] = (p_ref[...].astype(F32) + l_ref[...].astype(F32)).astype(o_ref.dtype)

    blk = pl.BlockSpec((None, tr, C), lambda j, i, core: (j, i, 0))
    core = lax.axis_index("c").astype(jnp.int32).reshape(1)
    return pl.pallas_call(
        body, name=name, out_shape=_sds((4, R, C), part.dtype), compiler_params=_cp(("parallel", "parallel")),
        grid_spec=pltpu.PrefetchScalarGridSpec(
            num_scalar_prefetch=1, grid=(4, R // tr),
            in_specs=[pl.BlockSpec((None, None, tr, C), lambda j, i, core: (j, core[0], i, 0)), blk], out_specs=blk),
    )(core, part, landed)


def _row_tile(R, pref):
    if R <= pref:
        return R
    t = (pref // 16) * 16
    while R % t:
        t -= 16
    return t


def all_reduce_small(name, vec, after):
    def body(v_ref, _, sum_ref, land_ref, send_sems, recv_sems):
        x, y, c = _position()
        me = (x, y, c)
        flip = lambda v, bit: 1 - v if bit else v
        peers = [(flip(x, k & 4), flip(y, k & 2), flip(c, k & 1)) for k in range(1, N_DEV)]
        land_ref[_slot(me)] = v_ref[...]
        sends = [pltpu.make_async_remote_copy(src_ref=v_ref, dst_ref=land_ref.at[_slot(me)], send_sem=send_sems.at[k],
                                              recv_sem=recv_sems.at[k], device_id=peer, device_id_type=MESH)
                 for k, peer in enumerate(peers)]
        for cp in sends:
            cp.start()
        for k, peer in enumerate(peers):
            pltpu.make_async_remote_copy(src_ref=v_ref, dst_ref=land_ref.at[_slot(peer)], send_sem=send_sems.at[k],
                                         recv_sem=recv_sems.at[k], device_id=peer, device_id_type=MESH).wait_recv()
        acc = land_ref[0]
        for s in range(1, N_DEV):
            acc = acc + land_ref[s]
        sum_ref[...] = acc
        for cp in sends:
            cp.wait_send()

    vm = pl.BlockSpec(memory_space=pltpu.VMEM)
    return pl.pallas_call(
        body, name=name, in_specs=[vm, pl.BlockSpec(memory_space=pl.ANY)], out_specs=[vm, vm],
        out_shape=[_sds(vec.shape, F32), _sds((N_DEV,) + vec.shape, F32)],
        scratch_shapes=[pltpu.SemaphoreType.DMA((7,)), pltpu.SemaphoreType.DMA((7,))],
        compiler_params=pltpu.CompilerParams(has_side_effects=True),
    )(vec, after)[0]


def _adamw_math(g, w, m, v):
    m2 = ADAM_B1 * m + (1.0 - ADAM_B1) * g
    v2 = ADAM_B2 * v + (1.0 - ADAM_B2) * (g * g)
    m_hat = m2 / (1.0 - ADAM_B1 ** ADAM_STEP)
    v_hat = v2 / (1.0 - ADAM_B2 ** ADAM_STEP)
    return -ADAM_LR * (m_hat / (jnp.sqrt(v_hat) + ADAM_EPS) + ADAM_WD * w), m2, v2


def adamw_shard(name, chip_parts, landed, w, m, v, layer, prev=None, tr=128):
    L, R, C = w.shape
    tr = _row_tile(R, tr)
    n_in = 5 + (4 if prev is not None else 0)

    def body(*refs):
        own_ref, land_ref, w_ref, m_ref, v_ref = refs[1:6]
        g_ref, d_ref, nm_ref, nv_ref = refs[1 + n_in:]
        g = own_ref[...].astype(F32)
        for s in range(landed.shape[0]):
            g = g + land_ref[s].astype(F32)
        g_ref[...] = g
        d_ref[...], nm_ref[...], nv_ref[...] = _adamw_math(g, w_ref[...], m_ref[...], v_ref[...])

    lay = pl.BlockSpec((None, tr, C), lambda i, chip: (layer, i, 0))
    own = pl.BlockSpec((None, tr, C), lambda i, chip: (chip[0], i, 0))
    chip = (2 * lax.axis_index("x") + lax.axis_index("y")).astype(jnp.int32).reshape(1)
    ins = [chip_parts, landed, w, m, v] + (list(prev) if prev is not None else [])
    return pl.pallas_call(
        body, name=name, out_shape=[_sds((L, R, C), F32)] * 4,
        grid_spec=pltpu.PrefetchScalarGridSpec(
            num_scalar_prefetch=1, grid=(R // tr,),
            in_specs=[own, pl.BlockSpec((landed.shape[0], tr, C), lambda i, chip: (0, i, 0)), lay, lay, lay]
            + [pl.BlockSpec(memory_space=pl.ANY)] * (n_in - 5),
            out_specs=[lay] * 4),
        input_output_aliases={6 + i: i for i in range(n_in - 5)}, compiler_params=_cp(("parallel",)),
    )(chip, *ins)


def adamw_small(name, g, w, m, v):
    def body(g_ref, w_ref, m_ref, v_ref, d_ref, nm_ref, nv_ref):
        d_ref[...], nm_ref[...], nv_ref[...] = _adamw_math(g_ref[...], w_ref[...], m_ref[...], v_ref[...])

    return pl.pallas_call(body, name=name, out_shape=[_sds(w.shape, F32)] * 3)(g, w, m, v)


def _pack(vs):
    flat = jnp.concatenate([v.reshape(-1) for v in vs])
    return flat.reshape(-1, LANE)


def _unpack(packed, like):
    flat, out, off = packed.reshape(-1), [], 0
    for v in like:
        out.append(flat[off:off + v.size].reshape(v.shape))
        off += v.size
    return out


class MeshWeights:
    def __init__(self, w_kv, w_out, pool_w_in, pool_w_grp, hgrn_w_in, hgrn_norm_g):
        self.n_grp, self.pg_loc, self.pg = pool_w_grp.shape[1:]
        wpin, hnorm, token = all_gather_blocks("gather_first", [pool_w_in[0].astype(COMM_DTYPE), hgrn_norm_g])
        self.first = (wpin, hnorm.reshape(1, -1))
        zero = token[0, 0]
        cast = lambda a: (a + zero).astype(COMM_DTYPE)
        self.gather, self.tokens = {}, []
        for layer, blocks in ((0, [cast(pool_w_grp[0]), cast(w_kv[0]), cast(w_out[0])]),
                              (1, [cast(hgrn_w_in[0])]), (2, [cast(w_kv[1]), cast(w_out[1])])):
            me = _slot(_position())
            lands = [lax.dynamic_update_index_in_dim(lax.empty((N_DEV,) + b.shape, b.dtype), b, me, 0) for b in blocks]
            n = len(blocks)
            sems, bufs, token = split_start(f"gather{layer}_cross", blocks + lands, 4 * n, _gather_cross(n))
            self.gather[layer] = (sems, bufs)
            self.tokens.append(token[0, 0])
        self.reduce = {}

    def first_weights(self):
        return self.first[0], self.first[1], self.tokens[0] + self.tokens[1] + self.tokens[2]

    def _pass_on(self, layer, after):
        sems, bufs = self.gather[layer]
        n = len(bufs) // 2
        gathered = split_wait(f"gather{layer}_cross_wait", sems, bufs, _gather_cross(n), after)[n:]
        sems, bufs, token = split_start(f"gather{layer}_pass", gathered, 3 * n, _gather_pass(n))
        self.gather[layer] = (sems, bufs)
        return token[0, 0]

    def _gathered(self, layer, after):
        sems, bufs = self.gather[layer]
        return split_wait(f"gather{layer}_pass_wait", sems, bufs, _gather_pass(len(bufs)), after)

    def after_proj0(self, after):
        return self._pass_on(0, after)

    def layer0_weights(self, after):
        wgrp, wkv, wout = self._gathered(0, after)
        rows = lambda g: g.reshape(-1, g.shape[-1])
        return wgrp.transpose(1, 0, 2, 3).reshape(self.n_grp, self.pg, self.pg), rows(wkv), rows(wout)

    def after_out0(self, after):
        return self._pass_on(1, after)

    def layer1_weights(self, after):
        return self._gathered(1, after)[0]

    def after_proj1(self, after):
        return self._pass_on(2, after)

    def layer1_rest(self, after):
        wkv, wout = self._gathered(2, after)
        rows = lambda g: g.reshape(-1, g.shape[-1])
        return rows(wkv), rows(wout)

    def grads_ready(self, layer, grads):
        parts = {}
        for nm, g in grads.items():
            if nm == "pool_w_grp":
                g = g.reshape(self.n_grp, N_DEV, self.pg_loc, self.pg).transpose(1, 0, 2, 3)
            parts[nm] = g.reshape(4, 2, -1, g.shape[-1])
        names, srcs = list(parts), list(parts.values())
        lands = [lax.empty((4,) + p.shape[2:], p.dtype) for p in srcs]
        sems, bufs, token = split_start(f"reduce{layer}_pair", srcs + lands, 4 * len(srcs), _reduce_pair(len(srcs)))
        self.reduce[layer] = (names, sems, bufs)
        return token[0, 0]

    def grads_reduce(self, layer, after):
        names, sems, bufs = self.reduce[layer]
        n = len(names)
        bufs = split_wait(f"reduce{layer}_pair_wait", sems, bufs, _reduce_pair(n), after)
        chip_parts = [pair_add(f"pair_add_{nm}{layer}", bufs[t], bufs[n + t]) for t, nm in enumerate(names)]
        lands = [lax.empty((3,) + p.shape[1:], p.dtype) for p in chip_parts]
        sems, bufs, token = split_start(f"reduce{layer}_cross", chip_parts + lands, 3 * n, _reduce_cross(n))
        self.reduce[layer] = (names, sems, bufs)
        return token[0, 0]

    def grads_done(self, layer, after):
        names, sems, bufs = self.reduce[layer]
        n = len(names)
        bufs = split_wait(f"reduce{layer}_cross_wait", sems, bufs, _reduce_cross(n), after)
        return {nm: (bufs[t], bufs[n + t]) for t, nm in enumerate(names)}


def kernel(x, mem, norm_g, mem_norm_g, w_kv, w_out, pool_w_in, pool_w_grp, pool_scale, hgrn_w_in, hgrn_lb, hgrn_norm_g, final_g, loss_target, m_norm_g, m_mem_norm_g, m_w_kv, m_w_out, m_pool_w_in, m_pool_w_grp, m_pool_scale, m_hgrn_w_in, m_hgrn_lb, m_hgrn_norm_g, m_final_g, v_norm_g, v_mem_norm_g, v_w_kv, v_w_out, v_pool_w_in, v_pool_w_grp, v_pool_scale, v_hgrn_w_in, v_hgrn_lb, v_hgrn_norm_g, v_final_g):
    comm = MeshWeights(w_kv, w_out, pool_w_in, pool_w_grp, hgrn_w_in, hgrn_norm_g)
    loss, grad_x, small = run_step(x[0], mem[0], loss_target[0], norm_g, mem_norm_g, pool_scale, hgrn_lb, final_g, comm)
    loss = lax.psum(loss, AXES)
    rows = lambda a: a.reshape(a.shape[0], -1, a.shape[-1])
    state = dict(w_kv=(w_kv, m_w_kv, v_w_kv), w_out=(w_out, m_w_out, v_w_out), pool_w_in=(pool_w_in, m_pool_w_in, v_pool_w_in),
                 pool_w_grp=(pool_w_grp, m_pool_w_grp, v_pool_w_grp), hgrn_w_in=(hgrn_w_in, m_hgrn_w_in, v_hgrn_w_in))
    big = {}

    def update(group, layer, after):
        for nm, (chip_parts, landed) in comm.grads_done(group, after).items():
            w, m, v = state[nm]
            big[nm] = adamw_shard(f"adamw_{nm}{layer}", chip_parts, landed, rows(w), rows(m), rows(v),
                                  layer if w.shape[0] > 1 else 0, prev=big.get(nm))
            after = big[nm][0]
        return after

    after = update("0b", 0, update(1, 1, grad_x))
    names = ("norm_g", "mem_norm_g", "pool_scale", "hgrn_lb", "hgrn_norm_g", "final_g")
    like = (norm_g, mem_norm_g, pool_scale, hgrn_lb, _sds((1, N_DEV * hgrn_norm_g.shape[1]), F32), final_g)
    packed = _pack([small[nm].reshape(lk.shape) for nm, lk in zip(names, like)])
    reduced = _unpack(all_reduce_small("reduce_small", packed, after), like)
    n_loc = hgrn_norm_g.shape[1]
    reduced[4] = lax.dynamic_slice(reduced[4], (0, _slot(_position()) * n_loc), (1, n_loc))
    ws = (norm_g, mem_norm_g, pool_scale, hgrn_lb, hgrn_norm_g, final_g)
    ms = (m_norm_g, m_mem_norm_g, m_pool_scale, m_hgrn_lb, m_hgrn_norm_g, m_final_g)
    vs = (v_norm_g, v_mem_norm_g, v_pool_scale, v_hgrn_lb, v_hgrn_norm_g, v_final_g)
    res = adamw_small("adamw_small", _pack(reduced), _pack(ws), _pack(ms), _pack(vs))
    outs = {nm: [reduced[i]] + [_unpack(r, ws)[i] for r in res] for i, nm in enumerate(names)}
    update("0a", 0, res[0])
    for nm, (w, _, _) in state.items():
        outs[nm] = [r.reshape(w.shape) for r in big[nm]]
    order = ("norm_g", "mem_norm_g", "w_kv", "w_out", "pool_w_in", "pool_w_grp", "pool_scale", "hgrn_w_in", "hgrn_lb",
             "hgrn_norm_g", "final_g")
    return (loss, grad_x[None], *[outs[nm][0] for nm in order], *[outs[nm][1] for nm in order],
            *[outs[nm][2] for nm in order], *[outs[nm][3] for nm in order])
```

```python
import functools

import jax
import jax.numpy as jnp
from jax import lax
from jax.experimental import pallas as pl
from jax.experimental.pallas import tpu as pltpu

F32, BF16 = jnp.float32, jnp.bfloat16
MM_DTYPE = BF16
COMM_DTYPE = BF16
EPS = 1e-6
POOL_WINDOWS = (2, 4, 8, 16)
POOL_HALO = 16
HG_HEAD_DIM = 128
HG_CHUNK = 64
CA_HEADS = 4
N_DEV = 8
ADAM_LR, ADAM_B1, ADAM_B2, ADAM_EPS, ADAM_WD, ADAM_STEP = 0.001, 0.9, 0.999, 1e-08, 0.01, 10
VMEM_LIMIT = 48 << 20
VMEM_LIMIT_TALL = 56 << 20
LANE = 128
MESH = pl.DeviceIdType.MESH
AXES = ("x", "y", "c")

NN = (((1,), (0,)), ((), ()))
NT = (((1,), (1,)), ((), ()))
TN = (((0,), (0,)), ((), ()))


def _cp(sem=None, vmem=VMEM_LIMIT):
    return pltpu.CompilerParams(dimension_semantics=sem, vmem_limit_bytes=vmem)


def _dot(a, b, dims):
    return lax.dot_general(a.astype(MM_DTYPE), b.astype(MM_DTYPE), dims, preferred_element_type=F32)


def _sig(v):
    return 1.0 / (1.0 + jnp.exp(-v))


def _tile(n, pref):
    if n <= pref:
        return n
    t = (pref // LANE) * LANE
    while n % t:
        t -= LANE
    return t


def _sds(shape, dtype):
    return jax.ShapeDtypeStruct(tuple(shape), dtype)


def _mm(name, grid, sem, ins, in_specs, out_shapes, out_specs, dims, epi=None, nk=1, acc_shape=None, aliases=None,
        vmem=VMEM_LIMIT):
    n_in, n_out = len(ins), len(out_shapes)

    def body(*refs):
        a_ref, b_ref = refs[0], refs[1]
        extra = refs[2:n_in]
        outs = refs[n_in:n_in + n_out]
        p = _dot(a_ref[...], b_ref[...], dims)

        def finish(acc):
            res = epi(acc, *extra) if epi is not None else (acc,)
            for r, o_ref in zip(res, outs):
                o_ref[...] = r.astype(o_ref.dtype)

        if nk == 1:
            finish(p)
        else:
            acc_ref = refs[n_in + n_out]
            k = pl.program_id(len(grid) - 1)

            @pl.when(k == 0)
            def _():
                acc_ref[...] = p

            @pl.when(k > 0)
            def _():
                acc_ref[...] += p

            @pl.when(k == nk - 1)
            def _():
                finish(acc_ref[...])

    scratch = [pltpu.VMEM(acc_shape, F32)] if nk > 1 else []
    return pl.pallas_call(
        body, name=name, grid=grid, in_specs=in_specs, out_specs=out_specs, out_shape=out_shapes,
        scratch_shapes=scratch, compiler_params=_cp(sem, vmem), input_output_aliases=aliases or {},
    )(*ins)


def proj_fwd(name, h, wblk):
    T, D = h.shape
    nblk, _, nb = wblk.shape
    tm = _tile(T, 1024)
    return _mm(name, (nblk, T // tm), ("parallel", "parallel"), [h, wblk],
               [pl.BlockSpec((tm, D), lambda j, i: (i, 0)), pl.BlockSpec((None, D, nb), lambda j, i: (j, 0, 0))],
               [_sds((T, nblk * nb), F32)], [pl.BlockSpec((tm, nb), lambda j, i: (i, j))], NN)[0]


def proj_bwd_w(name, h, dproj, nblk):
    T, D = h.shape
    N = dproj.shape[1]
    nb = N // nblk
    tm, tn = _tile(D, 1024 if nb <= 1024 else 512), _tile(nb, 2048)
    per = nb // tn
    return _mm(name, (D // tm, N // tn), ("parallel", "parallel"), [h, dproj],
               [pl.BlockSpec((T, tm), lambda i, n: (0, i)), pl.BlockSpec((T, tn), lambda i, n: (0, n))],
               [_sds((nblk, D, nb), COMM_DTYPE)], [pl.BlockSpec((None, tm, tn), lambda i, n: (n // per, i, n % per))], TN)[0]


def proj_bwd_x(name, dproj, wblk, after, tm=1024):
    T, N = dproj.shape
    nblk, D, nb = wblk.shape
    tm = _tile(T, tm)
    return _mm(name, (T // tm, nblk), ("parallel", "arbitrary"), [dproj, wblk, jnp.reshape(after, (1, 1))],
               [pl.BlockSpec((tm, nb), lambda i, k: (i, k)), pl.BlockSpec((None, D, nb), lambda i, k: (k, 0, 0)),
                pl.BlockSpec(memory_space=pl.ANY)],
               [_sds((T, D), F32)], [pl.BlockSpec((tm, D), lambda i, k: (i, 0))], NT, nk=nblk, acc_shape=(tm, D),
               vmem=VMEM_LIMIT_TALL)[0]


def out_fwd(name, branch, wout, x, after, tm=1024, tn=512):
    T, E = branch.shape
    D = wout.shape[1]
    tm, tn = _tile(T, tm), _tile(D, tn)
    return _mm(name, (T // tm, D // tn), ("parallel", "parallel"), [branch, wout, x, jnp.reshape(after, (1, 1))],
               [pl.BlockSpec((tm, E), lambda i, j: (i, 0)), pl.BlockSpec((E, tn), lambda i, j: (0, j)),
                pl.BlockSpec((tm, tn), lambda i, j: (i, j)), pl.BlockSpec(memory_space=pl.ANY)],
               [_sds((T, D), F32)], [pl.BlockSpec((tm, tn), lambda i, j: (i, j))], NN,
               epi=lambda acc, x_ref, _: (acc + x_ref[...],))[0]


def mm_nt(name, a, b, out_dtype, tm=1024, tn=2048, res=None):
    M, K = a.shape
    N = b.shape[0]
    tm, tn = _tile(M, tm), _tile(N, tn)
    ins = [a, b] + ([res] if res is not None else [])
    specs = [pl.BlockSpec((tm, K), lambda i, j: (i, 0)), pl.BlockSpec((tn, K), lambda i, j: (j, 0))]
    if res is not None:
        specs.append(pl.BlockSpec((tm, tn), lambda i, j: (i, j)))
    epi = (lambda acc, r_ref: (acc + r_ref[...],)) if res is not None else None
    return _mm(name, (M // tm, N // tn), ("parallel", "parallel"), ins, specs,
               [_sds((M, N), out_dtype)], [pl.BlockSpec((tm, tn), lambda i, j: (i, j))], NT, epi=epi)[0]


def mm_nn(name, a, b, out_dtype, tm=512, tn=512):
    M, K = a.shape
    N = b.shape[1]
    tm, tn = _tile(M, tm), _tile(N, tn)
    return _mm(name, (M // tm, N // tn), ("parallel", "parallel"), [a, b],
               [pl.BlockSpec((tm, K), lambda i, j: (i, 0)), pl.BlockSpec((K, tn), lambda i, j: (0, j))],
               [_sds((M, N), out_dtype)], [pl.BlockSpec((tm, tn), lambda i, j: (i, j))], NN)[0]


def mm_tn(name, a, b, out_dtype, tm=1024, tn=1024):
    K, M = a.shape
    N = b.shape[1]
    tm, tn = _tile(M, tm), _tile(N, tn)
    return _mm(name, (M // tm, N // tn), ("parallel", "parallel"), [a, b],
               [pl.BlockSpec((K, tm), lambda i, j: (0, i)), pl.BlockSpec((K, tn), lambda i, j: (0, j))],
               [_sds((M, N), out_dtype)], [pl.BlockSpec((tm, tn), lambda i, j: (i, j))], TN)[0]


def rmsnorm_fwd(name, x, g, tm=512):
    T, D = x.shape
    tm = _tile(T, tm)

    def body(x_ref, g_ref, o_ref):
        xf = x_ref[...]
        r = lax.rsqrt(jnp.mean(xf * xf, axis=-1, keepdims=True) + EPS)
        o_ref[...] = ((xf * r) * g_ref[...]).astype(o_ref.dtype)

    return pl.pallas_call(
        body, name=name, grid=(T // tm,),
        in_specs=[pl.BlockSpec((tm, D), lambda i: (i, 0)), pl.BlockSpec((1, D), lambda i: (0, 0))],
        out_specs=pl.BlockSpec((tm, D), lambda i: (i, 0)), out_shape=_sds((T, D), MM_DTYPE),
        compiler_params=_cp(("parallel",)),
    )(x, g.reshape(1, D))


def _rms_bwd_math(xf, g, dh):
    r = lax.rsqrt(jnp.mean(xf * xf, axis=-1, keepdims=True) + EPS)
    gd = dh * g
    dx = r * gd - xf * ((r * r * r) * jnp.mean(gd * xf, axis=-1, keepdims=True))
    dg = jnp.sum(dh * (xf * r), axis=0, keepdims=True)
    return dx, dg


def rmsnorm_bwd(name, x, g, dh, res=None, tm=512):
    T, D = x.shape
    tm = _tile(T, tm)
    has_res = res is not None

    def body(*refs):
        x_ref, g_ref, dh_ref = refs[:3]
        dx_ref, dxb_ref, dg_ref = refs[3 + has_res:]
        dx, dg = _rms_bwd_math(x_ref[...], g_ref[...], dh_ref[...])
        if has_res:
            dx = dx + refs[3][...]
        dx_ref[...] = dx
        dxb_ref[...] = dx.astype(dxb_ref.dtype)

        @pl.when(pl.program_id(0) == 0)
        def _():
            dg_ref[...] = jnp.zeros_like(dg_ref)

        dg_ref[...] += dg

    row = pl.BlockSpec((tm, D), lambda i: (i, 0))
    vec = pl.BlockSpec((1, D), lambda i: (0, 0))
    ins = [x, g.reshape(1, D), dh] + ([res] if has_res else [])
    return pl.pallas_call(
        body, name=name, grid=(T // tm,), in_specs=[row, vec, row] + ([row] if has_res else []),
        out_specs=[row, row, vec], out_shape=[_sds((T, D), F32), _sds((T, D), MM_DTYPE), _sds((1, D), F32)],
        compiler_params=_cp(("arbitrary",)),
    )(*ins)


def final_loss_bwd(name, x, g, target, tm=512):
    T, D = x.shape
    tm = _tile(T, tm)

    def body(x_ref, g_ref, t_ref, loss_ref, dx_ref, dxb_ref, dg_ref):
        xf, gv = x_ref[...], g_ref[...]
        r = lax.rsqrt(jnp.mean(xf * xf, axis=-1, keepdims=True) + EPS)
        err = (xf * r) * gv - t_ref[...]
        part = 0.5 * jnp.sum(jnp.mean(err * err, axis=-1, keepdims=True), axis=0, keepdims=True)
        dx, dg = _rms_bwd_math(xf, gv, err / D)
        dx_ref[...] = dx
        dxb_ref[...] = dx.astype(dxb_ref.dtype)

        @pl.when(pl.program_id(0) == 0)
        def _():
            dg_ref[...] = jnp.zeros_like(dg_ref)
            loss_ref[...] = jnp.zeros_like(loss_ref)

        dg_ref[...] += dg
        loss_ref[...] += jnp.broadcast_to(part, loss_ref.shape)

    row = pl.BlockSpec((tm, D), lambda i: (i, 0))
    vec = pl.BlockSpec((1, D), lambda i: (0, 0))
    return pl.pallas_call(
        body, name=name, grid=(T // tm,), in_specs=[row, vec, row],
        out_specs=[pl.BlockSpec((8, LANE), lambda i: (0, 0)), row, row, vec],
        out_shape=[_sds((8, LANE), F32), _sds((T, D), F32), _sds((T, D), MM_DTYPE), _sds((1, D), F32)],
        compiler_params=_cp(("arbitrary",)),
    )(x, g.reshape(1, D), target)


def _pool_tiles(T, emix):
    pg = emix // len(POOL_WINDOWS)
    tc = 256 if pg % 256 == 0 else LANE
    return pg, tc, _tile(T, 512)


def pool_fwd(name, proj, T, emix, after):
    pg, tc, R = _pool_tiles(T, emix)
    per_g = pg // tc

    def body(u_ref, _, o_ref):
        g = pl.program_id(0) // per_g
        for gi, w in enumerate(POOL_WINDOWS):
            @pl.when(g == gi)
            def _():
                for ci in range(T // R):
                    r0 = ci * R
                    cur = u_ref[r0:r0 + R, :]
                    halo = jnp.zeros((POOL_HALO, tc), F32) if ci == 0 else u_ref[r0 - POOL_HALO:r0, :]
                    s = jnp.concatenate([halo, cur], axis=0)
                    for st in range(w.bit_length() - 1):
                        s = s + pltpu.roll(s, 1 << st, axis=0)
                    t = r0 + lax.broadcasted_iota(jnp.int32, (R, 1), 0)
                    cnt = jnp.minimum(t + 1, w).astype(F32)
                    o_ref[r0:r0 + R, :] = (s[POOL_HALO:] / cnt - cur).astype(o_ref.dtype)

    return pl.pallas_call(
        body, name=name, grid=(emix // tc,),
        in_specs=[pl.BlockSpec((T, tc), lambda j: (0, j)), pl.BlockSpec(memory_space=pl.ANY)],
        out_specs=pl.BlockSpec((T, tc), lambda j: (0, j)), out_shape=_sds((T, emix), MM_DTYPE),
        compiler_params=_cp(("parallel",)),
    )(proj, jnp.reshape(after, (1, 1)))


def pool_bwd(name, dpooled, dproj, T, emix):
    pg, tc, R = _pool_tiles(T, emix)
    per_g = pg // tc

    def body(d_ref, _, o_ref):
        g = pl.program_id(0) // per_g
        for gi, w in enumerate(POOL_WINDOWS):
            @pl.when(g == gi)
            def _():
                n = R + POOL_HALO
                for ci in range(T // R):
                    r0 = ci * R
                    cur = d_ref[r0:r0 + R, :]
                    halo = jnp.zeros((POOL_HALO, tc), F32) if ci == T // R - 1 else d_ref[r0 + R:r0 + n, :]
                    t = r0 + lax.broadcasted_iota(jnp.int32, (n, 1), 0)
                    cnt = jnp.minimum(t + 1, w).astype(F32)
                    s = jnp.concatenate([cur, halo], axis=0) / cnt
                    for st in range(w.bit_length() - 1):
                        s = s + pltpu.roll(s, n - (1 << st), axis=0)
                    o_ref[r0:r0 + R, :] = (s[:R] - cur).astype(o_ref.dtype)

    return pl.pallas_call(
        body, name=name, grid=(emix // tc,),
        in_specs=[pl.BlockSpec((T, tc), lambda j: (0, j)), pl.BlockSpec(memory_space=pl.ANY)],
        out_specs=pl.BlockSpec((T, tc), lambda j: (0, j)), out_shape=_sds(dproj.shape, dproj.dtype),
        input_output_aliases={1: 0}, compiler_params=_cp(("parallel",)),
    )(dpooled, dproj)


def grp_fwd(name, pooled, wgrp, scale, proj, gate_off, E):
    T, emix = pooled.shape
    pg = emix // len(POOL_WINDOWS)
    tm = _tile(T, 2048)
    tc = 256 if pg % 256 == 0 and gate_off % 256 == 0 else LANE
    per = pg // tc

    def epi(acc, s_ref, g_ref):
        gt = g_ref[...]
        return acc, (acc * s_ref[...]) * (gt * _sig(gt))

    col = pl.BlockSpec((tm, tc), lambda n, i: (i, n))
    return _mm(name, (emix // tc, T // tm), ("parallel", "parallel"), [pooled, wgrp, scale, proj],
               [pl.BlockSpec((tm, pg), lambda n, i: (i, n // per)), pl.BlockSpec((None, pg, tc), lambda n, i: (n // per, 0, n % per)),
                pl.BlockSpec((1, tc), lambda n, i: (0, n)), pl.BlockSpec((tm, tc), lambda n, i: (i, gate_off // tc + n))],
               [_sds((T, emix), F32), _sds((T, E), MM_DTYPE)], [col, col], NN, epi=epi)


def grp_bwd_x(name, dy, wgrp):
    T, emix = dy.shape
    pg = emix // len(POOL_WINDOWS)
    tm = _tile(T, 1024)
    return _mm(name, (len(POOL_WINDOWS), T // tm), ("parallel", "parallel"), [dy, wgrp],
               [pl.BlockSpec((tm, pg), lambda g, i: (i, g)), pl.BlockSpec((None, pg, pg), lambda g, i: (g, 0, 0))],
               [_sds((T, emix), F32)], [pl.BlockSpec((tm, pg), lambda g, i: (i, g))], NT)[0]


def grp_bwd_w(name, pooled, dy):
    T, emix = dy.shape
    ng = len(POOL_WINDOWS)
    pg = emix // ng
    col = pl.BlockSpec((T, pg), lambda g: (0, g))
    return _mm(name, (ng,), ("parallel",), [pooled, dy], [col, col],
               [_sds((ng, pg, pg), COMM_DTYPE)], [pl.BlockSpec((None, pg, pg), lambda g: (g, 0, 0))], TN)[0]


def gate_bwd_pool(name, dbranch, y, scale, proj, gate_off, n_proj):
    T, emix = y.shape
    tm, tc = _tile(T, 1024), _tile(emix, 512)
    assert gate_off % tc == 0

    def body(db_ref, y_ref, s_ref, g_ref, dy_ref, dg_ref, ds_ref):
        db, yv, sc, gt = db_ref[...], y_ref[...], s_ref[...], g_ref[...]
        sg = _sig(gt)
        dmix = db * (gt * sg)
        dy_ref[...] = (dmix * sc).astype(dy_ref.dtype)
        dg_ref[...] = (db * (yv * sc) * (sg * (1.0 + gt * (1.0 - sg)))).astype(dg_ref.dtype)

        @pl.when(pl.program_id(1) == 0)
        def _():
            ds_ref[...] = jnp.zeros_like(ds_ref)

        ds_ref[...] += jnp.sum(dmix * yv, axis=0, keepdims=True)

    blk = pl.BlockSpec((tm, tc), lambda j, i: (i, j))
    vec = pl.BlockSpec((1, tc), lambda j, i: (0, j))
    gate = pl.BlockSpec((tm, tc), lambda j, i: (i, gate_off // tc + j))
    return pl.pallas_call(
        body, name=name, grid=(emix // tc, T // tm), in_specs=[blk, blk, vec, gate],
        out_specs=[blk, gate, vec],
        out_shape=[_sds((T, emix), MM_DTYPE), _sds((T, n_proj), MM_DTYPE), _sds((1, emix), F32)],
        compiler_params=_cp(("parallel", "arbitrary")),
    )(dbranch, y, scale, proj)


def attn_fwd(name, proj, kv, branch, q_off, gate_off, tm=2048):
    T = proj.shape[0]
    M, two_eca = kv.shape
    eca = two_eca // 2
    hd = eca // CA_HEADS
    E = branch.shape[1]
    tm = _tile(T, tm)
    scale = hd ** -0.5

    def body(q_ref, g_ref, k_ref, v_ref, _, ca_ref, br_ref):
        s = _dot(q_ref[...], k_ref[...], NT) * scale
        s = s - jnp.max(s, axis=-1, keepdims=True)
        e = jnp.exp(s)
        p = e / jnp.sum(e, axis=-1, keepdims=True)
        o = _dot(p, v_ref[...], NN)
        gt = g_ref[...]
        ca_ref[...] = o
        br_ref[...] = (o * (gt * _sig(gt))).astype(br_ref.dtype)

    return pl.pallas_call(
        body, name=name, grid=(CA_HEADS, T // tm),
        in_specs=[pl.BlockSpec((tm, hd), lambda h, i: (i, q_off // hd + h)),
                  pl.BlockSpec((tm, hd), lambda h, i: (i, (gate_off + E - eca) // hd + h)),
                  pl.BlockSpec((M, hd), lambda h, i: (0, h)), pl.BlockSpec((M, hd), lambda h, i: (0, CA_HEADS + h)),
                  pl.BlockSpec(memory_space=pl.ANY)],
        out_specs=[pl.BlockSpec((tm, hd), lambda h, i: (i, h)), pl.BlockSpec((tm, hd), lambda h, i: (i, (E - eca) // hd + h))],
        out_shape=[_sds((T, eca), F32), _sds(branch.shape, branch.dtype)],
        input_output_aliases={4: 1}, compiler_params=_cp(("parallel", "parallel")),
    )(proj, proj, kv, kv, branch)


def attn_bwd(name, proj, kv, ca, dbranch, dproj, q_off, gate_off, tm=2048):
    T = proj.shape[0]
    M, two_eca = kv.shape
    eca = two_eca // 2
    hd = eca // CA_HEADS
    E = dbranch.shape[1]
    tm = _tile(T, tm)
    scale = hd ** -0.5
    q_blk, g_blk = q_off // hd, (gate_off + E - eca) // hd

    n_rows = T // tm
    n_steps = CA_HEADS * n_rows

    def body(q_ref, g_ref, k_ref, v_ref, ca_ref, db_ref, _, dp_hbm, dk_ref, dv_ref, stash, sems):
        h, i = pl.program_id(0), pl.program_id(1)
        step = h * n_rows + i
        slot = step % 2

        def writes(slot, h, i):
            rows = pl.ds(pl.multiple_of(i * tm, tm), tm)
            return [pltpu.make_async_copy(stash.at[slot, r], dp_hbm.at[rows, pl.ds(pl.multiple_of((blk + h) * hd, hd), hd)],
                                          sems.at[slot, r]) for r, blk in enumerate((q_blk, g_blk))]

        @pl.when(step >= 2)
        def _():
            for cp in writes(slot, h, i):
                cp.wait()

        q, k, v = q_ref[...], k_ref[...], v_ref[...]
        s = _dot(q, k, NT) * scale
        s = s - jnp.max(s, axis=-1, keepdims=True)
        e = jnp.exp(s)
        p = e / jnp.sum(e, axis=-1, keepdims=True)
        gt, db = g_ref[...], db_ref[...]
        sg = _sig(gt)
        do = db * (gt * sg)
        stash[slot, 1] = (db * ca_ref[...] * (sg * (1.0 + gt * (1.0 - sg)))).astype(stash.dtype)
        dpr = _dot(do, v, NT)
        ds = p * (dpr - jnp.sum(dpr * p, axis=-1, keepdims=True)) * scale
        stash[slot, 0] = _dot(ds, k, NN).astype(stash.dtype)

        @pl.when(i == 0)
        def _():
            dk_ref[...] = jnp.zeros_like(dk_ref)
            dv_ref[...] = jnp.zeros_like(dv_ref)

        dk_ref[...] += _dot(ds, q, TN)
        dv_ref[...] += _dot(p, do, TN)
        for cp in writes(slot, h, i):
            cp.start()

        @pl.when(step == n_steps - 1)
        def _():
            for cp in writes(slot, h, i):
                cp.wait()
            if n_steps > 1:
                for cp in writes(1 - slot, h, i):
                    cp.wait()

    row = lambda blk: pl.BlockSpec((tm, hd), lambda h, i: (i, blk + h))
    outs = pl.pallas_call(
        body, name=name, grid=(CA_HEADS, n_rows),
        in_specs=[row(q_blk), row(g_blk), pl.BlockSpec((M, hd), lambda h, i: (0, h)),
                  pl.BlockSpec((M, hd), lambda h, i: (0, CA_HEADS + h)), row(0), row((E - eca) // hd),
                  pl.BlockSpec(memory_space=pl.ANY)],
        out_specs=[pl.BlockSpec(memory_space=pl.ANY), pl.BlockSpec((M, hd), lambda h, i: (0, h)),
                   pl.BlockSpec((M, hd), lambda h, i: (0, h))],
        out_shape=[_sds(dproj.shape, dproj.dtype), _sds((M, eca), F32), _sds((M, eca), F32)],
        scratch_shapes=[pltpu.VMEM((2, 2, tm, hd), dproj.dtype), pltpu.SemaphoreType.DMA((2, 2))],
        input_output_aliases={6: 0}, compiler_params=_cp(("arbitrary", "arbitrary")),
    )(proj, proj, kv, kv, ca, dbranch, dproj)
    return outs[0], jnp.concatenate([outs[1], outs[2]], axis=1)


def _split3(v):
    hi = v.astype(BF16)
    r1 = v - hi.astype(F32)
    mid = r1.astype(BF16)
    lo = (r1 - mid.astype(F32)).astype(BF16)
    return hi, mid, lo


def _tri_dot(tri, v):
    hi, mid, lo = _split3(v)
    d = functools.partial(lax.dot_general, dimension_numbers=NN, preferred_element_type=F32)
    return d(tri, hi) + d(tri, mid) + d(tri, lo)


def _causal_mask():
    C = HG_CHUNK
    return lax.broadcasted_iota(jnp.int32, (C, C), 0) >= lax.broadcasted_iota(jnp.int32, (C, C), 1)


def _cumsum_matrices(G):
    ri = lax.broadcasted_iota(jnp.int32, (G, G), 0)
    ci = lax.broadcasted_iota(jnp.int32, (G, G), 1)
    same = (ri // HG_CHUNK) == (ci // HG_CHUNK)
    return (jnp.where(same & (ri >= ci), 1.0, 0.0).astype(BF16), jnp.where(same & (ri <= ci), 1.0, 0.0).astype(BF16))


def _forget_gate(fi, lb):
    sig = _sig(fi)
    return sig, lb + (1.0 - lb) * sig


def _log_decay_cumsum(f_ref, b_s, lb, prefix, G):
    def group(i, carry):
        rows = pl.ds(pl.multiple_of(i * G, G), G)
        _, f = _forget_gate(f_ref[rows, :], lb)
        b_s[rows, :] = _tri_dot(prefix, jnp.log(f))
        return carry

    lax.fori_loop(0, b_s.shape[0] // G, group, 0)


def _lower_bound(lb2):
    l0, l1 = lb2[0:1, :], lb2[1:2, :]
    m = jnp.maximum(l0, l1)
    e0, e1 = jnp.exp(l0 - m), jnp.exp(l1 - m)
    sm0, sm1 = e0 / (e0 + e1), e1 / (e0 + e1)
    return (sm0 + sm1) - sm0, sm0, sm1


def _hg_chunk_fwd(qi, fi, lb, causal, b=None, prefix=None):
    sig, f = _forget_gate(fi, lb)
    k = 1.0 - f
    if b is None:
        b = _tri_dot(prefix, jnp.log(f))
    last = lax.broadcasted_iota(jnp.int32, b.shape, 0) == HG_CHUNK - 1
    bl = jnp.sum(jnp.where(last, b, 0.0), axis=0, keepdims=True)
    eb, enb, ete = jnp.exp(b), jnp.exp(-b), jnp.exp(bl - b)
    sq = _sig(qi)
    q = (qi * sq) * (HG_HEAD_DIM ** -0.5)
    q_dec, k_inv, k_te = q * eb, k * enb, k * ete
    att = jnp.where(causal, _dot(q_dec, k_inv, NT), 0.0)
    return dict(sig=sig, f=f, k=k, eb=eb, enb=enb, ete=ete, dcy=jnp.exp(bl), sq=sq, q_dec=q_dec, k_inv=k_inv, k_te=k_te,
                att=att, last=last)


def _hg_blocking(H, T):
    hb = 4 if H % 4 == 0 else (2 if H % 2 == 0 else 1)
    return hb, _tile(T, 1024)


def hgrn_fwd(name, proj, lb2, norm_g, emix, gate_off, E):
    T = proj.shape[0]
    hd, C = HG_HEAD_DIM, HG_CHUNK
    H, N = emix // hd, T // C
    hb, ts = _hg_blocking(H, T)
    W, nc, G = hb * hd, ts // C, _tile(ts, 256)

    def body(q_ref, f_ref, i_ref, g_ref, lb_ref, ng_ref, o_ref, st_ref, br_ref, state, b_s):
        @pl.when(pl.program_id(1) == 0)
        def _():
            state[...] = jnp.zeros_like(state)

        causal = _causal_mask()
        lb, _, _ = _lower_bound(lb_ref[...])
        ng = ng_ref[...]
        _log_decay_cumsum(f_ref, b_s, lb, _cumsum_matrices(G)[0], G)

        def chunk(n, carry):
            rows = pl.ds(pl.multiple_of(n * C, C), C)
            for h in range(hb):
                cols = slice(h * hd, (h + 1) * hd)
                v = i_ref[rows, cols]
                c = _hg_chunk_fwd(q_ref[rows, cols], f_ref[rows, cols], lb[:, cols], causal, b=b_s[rows, cols])
                st = state[h]
                o = _dot(c["att"], v, NN) + _dot(c["q_dec"], st, NT)
                st_ref[h, n] = st
                o_ref[rows, cols] = o
                r = lax.rsqrt(jnp.mean(o * o, axis=-1, keepdims=True) + EPS)
                gt = g_ref[rows, cols]
                br_ref[rows, cols] = (((o * r) * ng[:, cols]) * (gt * _sig(gt))).astype(br_ref.dtype)
                state[h] = st * c["dcy"] + _dot(v, c["k_te"], TN)
            return carry

        lax.fori_loop(0, nc, chunk, 0, unroll=8)

    col = lambda off: pl.BlockSpec((ts, W), lambda h, s: (s, off // W + h))
    vec = lambda r: pl.BlockSpec((r, W), lambda h, s: (0, h))
    return pl.pallas_call(
        body, name=name, grid=(H // hb, T // ts),
        in_specs=[col(0), col(emix), col(2 * emix), col(gate_off), vec(2), vec(1)],
        out_specs=[col(0), pl.BlockSpec((hb, nc, hd, hd), lambda h, s: (h, s, 0, 0)), col(0)],
        out_shape=[_sds((T, emix), F32), _sds((H, N, hd, hd), F32), _sds((T, E), MM_DTYPE)],
        scratch_shapes=[pltpu.VMEM((hb, hd, hd), F32), pltpu.VMEM((ts, W), F32)],
        compiler_params=_cp(("parallel", "arbitrary")),
    )(proj, proj, proj, proj, lb2, norm_g)


def hgrn_bwd(name, proj, lb2, norm_g, o_pre, states, dbranch, emix, gate_off):
    T, n_proj = proj.shape
    hd, C = HG_HEAD_DIM, HG_CHUNK
    H, N = emix // hd, T // C
    hb, ts = _hg_blocking(H, T)
    W, nc, ns = hb * hd, ts // C, T // ts
    qscale = HG_HEAD_DIM ** -0.5

    n_steps = (H // hb) * ns
    regions = (0, emix, 2 * emix, gate_off)

    def body(q_ref, f_ref, i_ref, g_ref, lb_ref, ng_ref, o_ref, st_ref, db_ref, dp_hbm, dlb_ref, dng_ref,
             stash, dst_ref, acc, sems):
        hblk, s = pl.program_id(0), pl.program_id(1)
        step = hblk * ns + s
        slot = step % 2

        def writes(slot, hblk, s):
            rows = pl.ds(pl.multiple_of((ns - 1 - s) * ts, ts), ts)
            return [pltpu.make_async_copy(stash.at[slot, r], dp_hbm.at[rows, pl.ds(pl.multiple_of(off + hblk * W, W), W)],
                                          sems.at[slot, r]) for r, off in enumerate(regions)]

        @pl.when(step >= 2)
        def _():
            for cp in writes(slot, hblk, s):
                cp.wait()

        @pl.when(s == 0)
        def _():
            dst_ref[...] = jnp.zeros_like(dst_ref)
            acc[...] = jnp.zeros_like(acc)

        causal = _causal_mask()
        prefix, suffix = _cumsum_matrices(C)
        lb, sm0, sm1 = _lower_bound(lb_ref[...])
        ng = ng_ref[...]

        def chunk(j, carry):
            n = nc - 1 - j
            rows = pl.ds(pl.multiple_of(n * C, C), C)
            for h in range(hb):
                cols = slice(h * hd, (h + 1) * hd)
                dst = dst_ref[h]
                qi, v, gt = q_ref[rows, cols], i_ref[rows, cols], g_ref[rows, cols]
                c = _hg_chunk_fwd(qi, f_ref[rows, cols], lb[:, cols], causal, prefix=prefix)
                st = st_ref[h, n]
                o, db, ngh = o_ref[rows, cols], db_ref[rows, cols], ng[:, cols]
                r = lax.rsqrt(jnp.mean(o * o, axis=-1, keepdims=True) + EPS)
                sg = _sig(gt)
                dmix = db * (gt * sg)
                dgate = db * ((o * r) * ngh) * (sg * (1.0 + gt * (1.0 - sg)))
                dyn = dmix * ngh
                do = r * dyn - o * ((r * r * r) * jnp.mean(dyn * o, axis=-1, keepdims=True))
                acc[1:2, cols] += jnp.sum(dmix * (o * r), axis=0, keepdims=True)
                datt = jnp.where(causal, _dot(do, v, NT), 0.0)
                dq_dec = _dot(datt, c["k_inv"], NN) + _dot(do, st, NN)
                dk_inv = _dot(datt, c["q_dec"], TN)
                dv = _dot(c["att"], do, TN) + _dot(c["k_te"], dst, NT)
                dk_te = _dot(v, dst, NN)
                ddcy = jnp.sum(dst * st, axis=0, keepdims=True)
                dst_ref[h] = _dot(do, c["q_dec"], TN) + dst * c["dcy"]
                t_te = dk_te * c["k_te"]
                dbv = dq_dec * c["q_dec"] - dk_inv * c["k_inv"] - t_te
                dbl = jnp.sum(t_te, axis=0, keepdims=True) + ddcy * c["dcy"]
                dbv = dbv + jnp.where(c["last"], dbl, 0.0)
                dlf = _tri_dot(suffix, dbv)
                dk = dk_inv * c["enb"] + dk_te * c["ete"]
                df = dlf / c["f"] - dk
                sig, sq = c["sig"], c["sq"]
                acc[0:1, cols] += jnp.sum(df * (1.0 - sig), axis=0, keepdims=True)
                dqi = (dq_dec * c["eb"]) * qscale * (sq * (1.0 + qi * (1.0 - sq)))
                stash[slot, 0, rows, cols] = dqi.astype(stash.dtype)
                stash[slot, 1, rows, cols] = (df * (1.0 - lb[:, cols]) * (sig * (1.0 - sig))).astype(stash.dtype)
                stash[slot, 2, rows, cols] = dv.astype(stash.dtype)
                stash[slot, 3, rows, cols] = dgate.astype(stash.dtype)
            return carry

        lax.fori_loop(0, nc, chunk, 0, unroll=8)
        t = acc[0:1, :] * (sm0 * sm1)
        dlb_ref[...] = jnp.concatenate([-t, t], axis=0)
        dng_ref[...] = acc[1:2, :]
        for cp in writes(slot, hblk, s):
            cp.start()

        @pl.when(step == n_steps - 1)
        def _():
            for cp in writes(slot, hblk, s):
                cp.wait()
            if n_steps > 1:
                for cp in writes(1 - slot, hblk, s):
                    cp.wait()

    col = lambda off: pl.BlockSpec((ts, W), lambda h, s: (ns - 1 - s, off // W + h))
    vec = lambda r: pl.BlockSpec((r, W), lambda h, s: (0, h))
    return pl.pallas_call(
        body, name=name, grid=(H // hb, ns),
        in_specs=[col(0), col(emix), col(2 * emix), col(gate_off), vec(2), vec(1), col(0),
                  pl.BlockSpec((hb, nc, hd, hd), lambda h, s: (h, ns - 1 - s, 0, 0)), col(0)],
        out_specs=[pl.BlockSpec(memory_space=pl.ANY), vec(2), vec(1)],
        out_shape=[_sds((T, n_proj), MM_DTYPE), _sds((2, emix), F32), _sds((1, emix), F32)],
        scratch_shapes=[pltpu.VMEM((2, 4, ts, W), MM_DTYPE), pltpu.VMEM((hb, hd, hd), F32), pltpu.VMEM((8, W), F32),
                        pltpu.SemaphoreType.DMA((2, 4))],
        compiler_params=_cp(("arbitrary", "arbitrary")),
    )(proj, proj, proj, proj, lb2, norm_g, o_pre, states, dbranch)


class WholeWeights:
    def __init__(self, wkv, wout, wpin, wgrp, whin, hgrn_norm_g):
        self.w = dict(wkv=wkv, wout=wout, wpin=wpin, wgrp=wgrp, whin=whin, hnorm=hgrn_norm_g)
        self.grads = {}

    def first_weights(self):
        return self.w["wpin"], self.w["hnorm"], 0.0

    def after_proj0(self, after):
        return 0.0

    def layer0_weights(self, after):
        return self.w["wgrp"], self.w["wkv"][0], self.w["wout"][0]

    def after_out0(self, after):
        return 0.0

    def layer1_weights(self, after):
        return self.w["whin"]

    def after_proj1(self, after):
        return 0.0

    def layer1_rest(self, after):
        return self.w["wkv"][1], self.w["wout"][1]

    def grads_ready(self, layer, grads):
        self.grads[layer] = grads
        return 0.0

    def grads_reduce(self, layer, after):
        return 0.0


def run_step(x, mem, target, norm_g, mem_norm_g, pool_scale, hgrn_lb, final_g, comm):
    T, D = x.shape
    mem_n = rmsnorm_fwd("mem_norm", mem, mem_norm_g)
    wpin, hgrn_norm_g, zero = comm.first_weights()
    h0 = rmsnorm_fwd("norm0", x, norm_g[0] + zero)
    proj0 = proj_fwd("proj0", h0, wpin)
    zero = comm.after_proj0(proj0)
    n0 = proj0.shape[1]
    E = n0 // 2
    eca = E // 4
    emix = E - eca
    q0, g0 = emix, emix + eca
    q1, g1 = 3 * emix, 3 * emix + eca
    pooled = pool_fwd("pool_fwd", proj0, T, emix, zero)
    wgrp, wkv0, wout0 = comm.layer0_weights(pooled)
    y0, branch0 = grp_fwd("grp_fwd", pooled, wgrp, pool_scale, proj0, g0, E)
    kv0 = mm_nn("kv0", mem_n, wkv0, MM_DTYPE)
    ca0, branch0 = attn_fwd("attn_fwd0", proj0, kv0, branch0, q0, g0)
    x1 = out_fwd("out0", branch0, wout0, x, ca0[:1, :1])
    h1 = rmsnorm_fwd("norm1", x1, norm_g[1] + comm.after_out0(x1))
    whin = comm.layer1_weights(h1)
    proj1 = proj_fwd("proj1", h1, whin)
    zero = comm.after_proj1(proj1)
    o1, states, branch1 = hgrn_fwd("hgrn_fwd", proj1, hgrn_lb + zero, hgrn_norm_g, emix, g1, E)
    wkv1, wout1 = comm.layer1_rest(o1)
    kv1 = mm_nn("kv1", mem_n, wkv1, MM_DTYPE)
    ca1, branch1 = attn_fwd("attn_fwd1", proj1, kv1, branch1, q1, g1)
    x2 = out_fwd("out1", branch1, wout1, x1, ca1[:1, :1])
    loss, dx2, dx2b, g_final = final_loss_bwd("final", x2, final_g, target)
    g_wout1 = mm_tn("gwout1", branch1, dx2b, COMM_DTYPE)
    dbranch1 = mm_nt("dbranch1", dx2b, wout1, F32)
    dproj1, g_lb, g_hnorm = hgrn_bwd("hgrn_bwd", proj1, hgrn_lb, hgrn_norm_g, o1, states, dbranch1, emix, g1)
    dproj1, dkv1 = attn_bwd("attn_bwd1", proj1, kv1, ca1, dbranch1, dproj1, q1, g1)
    g_wkv1 = mm_tn("gwkv1", mem_n, dkv1, COMM_DTYPE)
    dmem_n = mm_nt("dmem1", dkv1, wkv1, F32)
    g_whin = proj_bwd_w("gwhin", h1, dproj1, whin.shape[0])
    zero = comm.grads_ready(1, dict(w_kv=g_wkv1, w_out=g_wout1, hgrn_w_in=g_whin))
    dh1 = proj_bwd_x("dh1", dproj1, whin, zero)
    zero = zero + comm.grads_reduce(1, dh1)
    dx1, dx1b, g_norm1 = rmsnorm_bwd("norm1_bwd", x1, norm_g[1] + zero, dh1, res=dx2)
    dbranch0 = mm_nt("dbranch0", dx1b, wout0, F32)
    dy0, dproj0, g_scale = gate_bwd_pool("gate_bwd0", dbranch0, y0, pool_scale, proj0, g0, n0)
    dpooled = grp_bwd_x("dpooled", dy0, wgrp)
    dproj0 = pool_bwd("pool_bwd", dpooled, dproj0, T, emix)
    dproj0, dkv0 = attn_bwd("attn_bwd0", proj0, kv0, ca0, dbranch0, dproj0, q0, g0)
    g_wpin = proj_bwd_w("gwpin", h0, dproj0, wpin.shape[0])
    dkv0 = dkv0 + comm.grads_ready("0b", dict(pool_w_in=g_wpin))
    g_wkv0 = mm_tn("gwkv0", mem_n, dkv0, COMM_DTYPE)
    dmem_n = mm_nt("dmem0", dkv0, wkv0, F32, res=dmem_n)
    g_wout0 = mm_tn("gwout0", branch0, dx1b, COMM_DTYPE)
    g_wgrp = grp_bwd_w("gwgrp", pooled, dy0)
    zero = comm.grads_reduce("0b", g_wgrp)
    zero = zero + comm.grads_ready("0a", dict(w_kv=g_wkv0, w_out=g_wout0, pool_w_grp=g_wgrp))
    dh0 = proj_bwd_x("dh0", dproj0, wpin, zero)
    zero = zero + comm.grads_reduce("0a", dh0)
    grad_x, _, g_norm0 = rmsnorm_bwd("norm0_bwd", x, norm_g[0] + zero, dh0, res=dx1)
    _, _, g_mem = rmsnorm_bwd("mem_norm_bwd", mem, mem_norm_g, dmem_n)

    small = dict(norm_g=jnp.concatenate([g_norm0, g_norm1], axis=0), mem_norm_g=g_mem[0], pool_scale=g_scale,
                 hgrn_lb=g_lb, hgrn_norm_g=g_hnorm, final_g=g_final[0])
    return loss[0, 0], grad_x, small


def _position():
    return lax.axis_index("x"), lax.axis_index("y"), lax.axis_index("c")


def _slot(p):
    return 4 * p[0] + 2 * p[1] + p[2]


def all_gather_blocks(name, blocks):
    n = len(blocks)
    halved = [b.shape[0] % 32 == 0 for b in blocks]

    def body(*refs):
        ins, outs, token = refs[:n], refs[n:2 * n], refs[2 * n]
        send_sems, recv_sems, local_sems = refs[2 * n + 1:]
        token[...] = jnp.zeros_like(token)
        x, y, c = _position()
        me, sibling = (x, y, c), (x, y, 1 - c)
        x_nbr, y_nbr, diag = _other_chips(x, y)

        def copy(t, k, block, to, src=None, rows=None):
            dst = outs[t].at[_slot(block)]
            if rows is not None:
                dst = dst.at[rows]
            return pltpu.make_async_remote_copy(src_ref=dst if src is None else src, dst_ref=dst, send_sem=send_sems.at[t, k],
                                                recv_sem=recv_sems.at[t, k], device_id=to, device_id_type=MESH)

        mine = [pltpu.make_async_copy(ins[t], outs[t].at[_slot(me)], local_sems.at[t]) for t in range(n)]
        for cp in mine:
            cp.start()
        sends = []
        for t in range(n):
            sends += [copy(t, 0, me, sibling, src=ins[t]), copy(t, 1, me, (*x_nbr, c), src=ins[t]),
                      copy(t, 2, me, (*y_nbr, c), src=ins[t])]
            if not halved[t]:
                sends.append(copy(t, 4, me, (*diag, c), src=ins[t]))
        for cp in sends:
            cp.start()

        def start(cp):
            cp.start()
            sends.append(cp)

        for t in range(n):
            half = blocks[t].shape[0] // 2
            top, bottom = pl.ds(0, half), pl.ds(half, half)
            copy(t, 1, (*x_nbr, c), me).wait_recv()
            if halved[t]:
                start(copy(t, 4, (*x_nbr, c), (*y_nbr, c), rows=top))
            start(copy(t, 3, (*x_nbr, c), sibling))
            copy(t, 2, (*y_nbr, c), me).wait_recv()
            if halved[t]:
                start(copy(t, 6, (*y_nbr, c), (*x_nbr, c), rows=bottom))
            start(copy(t, 5, (*y_nbr, c), sibling))
            if halved[t]:
                copy(t, 4, (*diag, c), me, rows=top).wait_recv()
                copy(t, 6, (*diag, c), me, rows=bottom).wait_recv()
            else:
                copy(t, 4, (*diag, c), me).wait_recv()
            start(copy(t, 7, (*diag, c), sibling))
        for t in range(n):
            copy(t, 0, sibling, me).wait_recv()
            for k, chip in ((3, x_nbr), (5, y_nbr), (7, diag)):
                copy(t, k, (*chip, 1 - c), me).wait_recv()
        for cp in sends:
            cp.wait_send()
        for cp in mine:
            cp.wait()

    any_spec = pl.BlockSpec(memory_space=pl.ANY)
    return pl.pallas_call(
        body, name=name, in_specs=[any_spec] * n, out_specs=[any_spec] * n + [pl.BlockSpec(memory_space=pltpu.VMEM)],
        out_shape=[_sds((N_DEV,) + b.shape, b.dtype) for b in blocks] + [_sds((8, LANE), F32)],
        scratch_shapes=[pltpu.SemaphoreType.DMA((n, 8)), pltpu.SemaphoreType.DMA((n, 8)), pltpu.SemaphoreType.DMA((n,))],
        compiler_params=pltpu.CompilerParams(has_side_effects=True),
    )(*blocks)


_HBM = pl.BlockSpec(memory_space=pltpu.HBM)
_SEM = pl.BlockSpec(memory_space=pltpu.SEMAPHORE)
_EFFECT = pltpu.SideEffectType.DATAFLOW_SIDE_EFFECTING


def split_start(name, bufs, n_copies, build):
    n = len(bufs)

    def body(*refs):
        for cp in build(refs[:n], refs[n], refs[n + 1]):
            cp.start()
        refs[-1][...] = jnp.zeros_like(refs[-1])

    outs = pl.pallas_call(
        body, name=name, in_specs=[_HBM] * n,
        out_shape=(pltpu.SemaphoreType.DMA((n_copies,)), pltpu.SemaphoreType.DMA((n_copies,)),
                   *[pltpu.HBM(b.shape, b.dtype) for b in bufs], _sds((8, LANE), F32)),
        out_specs=(_SEM, _SEM, *[_HBM] * n, pl.BlockSpec(memory_space=pltpu.VMEM)),
        input_output_aliases={i: 2 + i for i in range(n)},
        compiler_params=pltpu.CompilerParams(has_side_effects=_EFFECT),
    )(*[pltpu.with_memory_space_constraint(b, pltpu.HBM) for b in bufs])
    return (outs[0], outs[1]), list(outs[2:2 + n]), outs[-1]


def split_wait(name, sems, bufs, build, after):
    n = len(bufs)

    def body(*refs):
        for cp in build(refs[:n], refs[n], refs[n + 1]):
            cp.wait()

    return list(pl.pallas_call(
        body, name=name, in_specs=[_HBM] * n + [_SEM, _SEM, pl.BlockSpec(memory_space=pl.ANY)],
        out_shape=[pltpu.HBM(b.shape, b.dtype) for b in bufs], out_specs=[_HBM] * n,
        input_output_aliases={i: i for i in range(n)},
        compiler_params=pltpu.CompilerParams(has_side_effects=_EFFECT),
    )(*bufs, sems[0], sems[1], after))


def _remote(src, dst, send_sems, recv_sems, k, to):
    return pltpu.make_async_remote_copy(src_ref=src, dst_ref=dst, send_sem=send_sems.at[k], recv_sem=recv_sems.at[k],
                                        device_id=to, device_id_type=MESH)


def _other_chips(x, y):
    return [(1 - x, y), (x, 1 - y), (1 - x, 1 - y)]


def _gather_cross(n):
    def build(refs, ss, rs):
        x, y, c = _position()
        me = _slot((x, y, c))
        targets = [(x, y, 1 - c)] + [(*chip, c) for chip in _other_chips(x, y)]
        return [_remote(refs[t], refs[n + t].at[me], ss, rs, 4 * t + k, to) for t in range(n) for k, to in enumerate(targets)]
    return build


def _gather_pass(n):
    def build(refs, ss, rs):
        x, y, c = _position()
        cps = []
        for t in range(n):
            for j, chip in enumerate(_other_chips(x, y)):
                blk = refs[t].at[_slot((*chip, c))]
                cps.append(_remote(blk, blk, ss, rs, 3 * t + j, (x, y, 1 - c)))
        return cps
    return build


def _reduce_pair(n):
    def build(refs, ss, rs):
        x, y, c = _position()
        return [_remote(refs[t].at[j, 1 - c], refs[n + t].at[j], ss, rs, 4 * t + j, (x, y, 1 - c))
                for t in range(n) for j in range(4)]
    return build


def _reduce_cross(n):
    def build(refs, ss, rs):
        x, y, c = _position()
        return [_remote(refs[t].at[2 * chip[0] + chip[1]], refs[n + t].at[r], ss, rs, 3 * t + r, (*chip, c))
                for t in range(n) for r, chip in enumerate(_other_chips(x, y))]
    return build


def pair_add(name, part, landed, tr=1024):
    _, _, R, C = part.shape
    tr = _row_tile(R, tr)

    def body(_, p_ref, l_ref, o_ref):
        o_ref[...] = (p_ref[...].astype(F32) + l_ref[...].astype(F32)).astype(o_ref.dtype)

    blk = pl.BlockSpec((None, tr, C), lambda j, i, core: (j, i, 0))
    core = lax.axis_index("c").astype(jnp.int32).reshape(1)
    return pl.pallas_call(
        body, name=name, out_shape=_sds((4, R, C), part.dtype), compiler_params=_cp(("parallel", "parallel")),
        grid_spec=pltpu.PrefetchScalarGridSpec(
            num_scalar_prefetch=1, grid=(4, R // tr),
            in_specs=[pl.BlockSpec((None, None, tr, C), lambda j, i, core: (j, core[0], i, 0)), blk], out_specs=blk),
    )(core, part, landed)


def _row_tile(R, pref):
    if R <= pref:
        return R
    t = (pref // 16) * 16
    while R % t:
        t -= 16
    return t


def all_reduce_small(name, vec, after):
    def body(v_ref, _, sum_ref, land_ref, send_sems, recv_sems):
        x, y, c = _position()
        me = (x, y, c)
        flip = lambda v, bit: 1 - v if bit else v
        peers = [(flip(x, k & 4), flip(y, k & 2), flip(c, k & 1)) for k in range(1, N_DEV)]
        land_ref[_slot(me)] = v_ref[...]
        sends = [pltpu.make_async_remote_copy(src_ref=v_ref, dst_ref=land_ref.at[_slot(me)], send_sem=send_sems.at[k],
                                              recv_sem=recv_sems.at[k], device_id=peer, device_id_type=MESH)
                 for k, peer in enumerate(peers)]
        for cp in sends:
            cp.start()
        for k, peer in enumerate(peers):
            pltpu.make_async_remote_copy(src_ref=v_ref, dst_ref=land_ref.at[_slot(peer)], send_sem=send_sems.at[k],
                                         recv_sem=recv_sems.at[k], device_id=peer, device_id_type=MESH).wait_recv()
        acc = land_ref[0]
        for s in range(1, N_DEV):
            acc = acc + land_ref[s]
        sum_ref[...] = acc
        for cp in sends:
            cp.wait_send()

    vm = pl.BlockSpec(memory_space=pltpu.VMEM)
    return pl.pallas_call(
        body, name=name, in_specs=[vm, pl.BlockSpec(memory_space=pl.ANY)], out_specs=[vm, vm],
        out_shape=[_sds(vec.shape, F32), _sds((N_DEV,) + vec.shape, F32)],
        scratch_shapes=[pltpu.SemaphoreType.DMA((7,)), pltpu.SemaphoreType.DMA((7,))],
        compiler_params=pltpu.CompilerParams(has_side_effects=True),
    )(vec, after)[0]


def _adamw_math(g, w, m, v):
    m2 = ADAM_B1 * m + (1.0 - ADAM_B1) * g
    v2 = ADAM_B2 * v + (1.0 - ADAM_B2) * (g * g)
    m_hat = m2 / (1.0 - ADAM_B1 ** ADAM_STEP)
    v_hat = v2 / (1.0 - ADAM_B2 ** ADAM_STEP)
    return -ADAM_LR * (m_hat / (jnp.sqrt(v_hat) + ADAM_EPS) + ADAM_WD * w), m2, v2


def adamw_shard(name, chip_parts, landed, w, m, v, layer, prev=None, tr=256):
    L, R, C = w.shape
    tr = _row_tile(R, tr)
    n_in = 5 + (4 if prev is not None else 0)

    def body(*refs):
        own_ref, land_ref, w_ref, m_ref, v_ref = refs[1:6]
        g_ref, d_ref, nm_ref, nv_ref = refs[1 + n_in:]
        g = own_ref[...].astype(F32)
        for s in range(landed.shape[0]):
            g = g + land_ref[s].astype(F32)
        g_ref[...] = g
        d_ref[...], nm_ref[...], nv_ref[...] = _adamw_math(g, w_ref[...], m_ref[...], v_ref[...])

    lay = pl.BlockSpec((None, tr, C), lambda i, chip: (layer, i, 0))
    own = pl.BlockSpec((None, tr, C), lambda i, chip: (chip[0], i, 0))
    chip = (2 * lax.axis_index("x") + lax.axis_index("y")).astype(jnp.int32).reshape(1)
    ins = [chip_parts, landed, w, m, v] + (list(prev) if prev is not None else [])
    return pl.pallas_call(
        body, name=name, out_shape=[_sds((L, R, C), F32)] * 4,
        grid_spec=pltpu.PrefetchScalarGridSpec(
            num_scalar_prefetch=1, grid=(R // tr,),
            in_specs=[own, pl.BlockSpec((landed.shape[0], tr, C), lambda i, chip: (0, i, 0)), lay, lay, lay]
            + [pl.BlockSpec(memory_space=pl.ANY)] * (n_in - 5),
            out_specs=[lay] * 4),
        input_output_aliases={6 + i: i for i in range(n_in - 5)}, compiler_params=_cp(("parallel",)),
    )(chip, *ins)


def adamw_small(name, g, w, m, v):
    def body(g_ref, w_ref, m_ref, v_ref, d_ref, nm_ref, nv_ref):
        d_ref[...], nm_ref[...], nv_ref[...] = _adamw_math(g_ref[...], w_ref[...], m_ref[...], v_ref[...])

    return pl.pallas_call(body, name=name, out_shape=[_sds(w.shape, F32)] * 3)(g, w, m, v)


def _pack(vs):
    flat = jnp.concatenate([v.reshape(-1) for v in vs])
    return flat.reshape(-1, LANE)


def _unpack(packed, like):
    flat, out, off = packed.reshape(-1), [], 0
    for v in like:
        out.append(flat[off:off + v.size].reshape(v.shape))
        off += v.size
    return out


class MeshWeights:
    def __init__(self, w_kv, w_out, pool_w_in, pool_w_grp, hgrn_w_in, hgrn_norm_g):
        self.n_grp, self.pg_loc, self.pg = pool_w_grp.shape[1:]
        wpin, hnorm, token = all_gather_blocks("gather_first", [pool_w_in[0].astype(COMM_DTYPE), hgrn_norm_g])
        self.first = (wpin, hnorm.reshape(1, -1))
        zero = token[0, 0]
        cast = lambda a: (a + zero).astype(COMM_DTYPE)
        self.gather, self.tokens = {}, []
        for layer, blocks in ((0, [cast(pool_w_grp[0]), cast(w_kv[0]), cast(w_out[0])]),
                              (1, [cast(hgrn_w_in[0])]), (2, [cast(w_kv[1]), cast(w_out[1])])):
            me = _slot(_position())
            lands = [lax.dynamic_update_index_in_dim(lax.empty((N_DEV,) + b.shape, b.dtype), b, me, 0) for b in blocks]
            n = len(blocks)
            sems, bufs, token = split_start(f"gather{layer}_cross", blocks + lands, 4 * n, _gather_cross(n))
            self.gather[layer] = (sems, bufs)
            self.tokens.append(token[0, 0])
        self.reduce = {}

    def first_weights(self):
        return self.first[0], self.first[1], self.tokens[0] + self.tokens[1] + self.tokens[2]

    def _pass_on(self, layer, after):
        sems, bufs = self.gather[layer]
        n = len(bufs) // 2
        gathered = split_wait(f"gather{layer}_cross_wait", sems, bufs, _gather_cross(n), after)[n:]
        sems, bufs, token = split_start(f"gather{layer}_pass", gathered, 3 * n, _gather_pass(n))
        self.gather[layer] = (sems, bufs)
        return token[0, 0]

    def _gathered(self, layer, after):
        sems, bufs = self.gather[layer]
        return split_wait(f"gather{layer}_pass_wait", sems, bufs, _gather_pass(len(bufs)), after)

    def after_proj0(self, after):
        return self._pass_on(0, after)

    def layer0_weights(self, after):
        wgrp, wkv, wout = self._gathered(0, after)
        rows = lambda g: g.reshape(-1, g.shape[-1])
        return wgrp.transpose(1, 0, 2, 3).reshape(self.n_grp, self.pg, self.pg), rows(wkv), rows(wout)

    def after_out0(self, after):
        return self._pass_on(1, after)

    def layer1_weights(self, after):
        return self._gathered(1, after)[0]

    def after_proj1(self, after):
        return self._pass_on(2, after)

    def layer1_rest(self, after):
        wkv, wout = self._gathered(2, after)
        rows = lambda g: g.reshape(-1, g.shape[-1])
        return rows(wkv), rows(wout)

    def grads_ready(self, layer, grads):
        parts = {}
        for nm, g in grads.items():
            if nm == "pool_w_grp":
                g = g.reshape(self.n_grp, N_DEV, self.pg_loc, self.pg).transpose(1, 0, 2, 3)
            parts[nm] = g.reshape(4, 2, -1, g.shape[-1])
        names, srcs = list(parts), list(parts.values())
        lands = [lax.empty((4,) + p.shape[2:], p.dtype) for p in srcs]
        sems, bufs, token = split_start(f"reduce{layer}_pair", srcs + lands, 4 * len(srcs), _reduce_pair(len(srcs)))
        self.reduce[layer] = (names, sems, bufs)
        return token[0, 0]

    def grads_reduce(self, layer, after):
        names, sems, bufs = self.reduce[layer]
        n = len(names)
        bufs = split_wait(f"reduce{layer}_pair_wait", sems, bufs, _reduce_pair(n), after)
        chip_parts = [pair_add(f"pair_add_{nm}{layer}", bufs[t], bufs[n + t]) for t, nm in enumerate(names)]
        lands = [lax.empty((3,) + p.shape[1:], p.dtype) for p in chip_parts]
        sems, bufs, token = split_start(f"reduce{layer}_cross", chip_parts + lands, 3 * n, _reduce_cross(n))
        self.reduce[layer] = (names, sems, bufs)
        return token[0, 0]

    def grads_done(self, layer, after):
        names, sems, bufs = self.reduce[layer]
        n = len(names)
        bufs = split_wait(f"reduce{layer}_cross_wait", sems, bufs, _reduce_cross(n), after)
        return {nm: (bufs[t], bufs[n + t]) for t, nm in enumerate(names)}


def kernel(x, mem, norm_g, mem_norm_g, w_kv, w_out, pool_w_in, pool_w_grp, pool_scale, hgrn_w_in, hgrn_lb, hgrn_norm_g, final_g, loss_target, m_norm_g, m_mem_norm_g, m_w_kv, m_w_out, m_pool_w_in, m_pool_w_grp, m_pool_scale, m_hgrn_w_in, m_hgrn_lb, m_hgrn_norm_g, m_final_g, v_norm_g, v_mem_norm_g, v_w_kv, v_w_out, v_pool_w_in, v_pool_w_grp, v_pool_scale, v_hgrn_w_in, v_hgrn_lb, v_hgrn_norm_g, v_final_g):
    comm = MeshWeights(w_kv, w_out, pool_w_in, pool_w_grp, hgrn_w_in, hgrn_norm_g)
    loss, grad_x, small = run_step(x[0], mem[0], loss_target[0], norm_g, mem_norm_g, pool_scale, hgrn_lb, final_g, comm)
    loss = lax.psum(loss, AXES)
    rows = lambda a: a.reshape(a.shape[0], -1, a.shape[-1])
    state = dict(w_kv=(w_kv, m_w_kv, v_w_kv), w_out=(w_out, m_w_out, v_w_out), pool_w_in=(pool_w_in, m_pool_w_in, v_pool_w_in),
                 pool_w_grp=(pool_w_grp, m_pool_w_grp, v_pool_w_grp), hgrn_w_in=(hgrn_w_in, m_hgrn_w_in, v_hgrn_w_in))
    big = {}

    def update(group, layer, after):
        for nm, (chip_parts, landed) in comm.grads_done(group, after).items():
            w, m, v = state[nm]
            big[nm] = adamw_shard(f"adamw_{nm}{layer}", chip_parts, landed, rows(w), rows(m), rows(v),
                                  layer if w.shape[0] > 1 else 0, prev=big.get(nm))
            after = big[nm][0]
        return after

    after = update("0b", 0, update(1, 1, grad_x))
    names = ("norm_g", "mem_norm_g", "pool_scale", "hgrn_lb", "hgrn_norm_g", "final_g")
    like = (norm_g, mem_norm_g, pool_scale, hgrn_lb, _sds((1, N_DEV * hgrn_norm_g.shape[1]), F32), final_g)
    packed = _pack([small[nm].reshape(lk.shape) for nm, lk in zip(names, like)])
    reduced = _unpack(all_reduce_small("reduce_small", packed, after), like)
    n_loc = hgrn_norm_g.shape[1]
    reduced[4] = lax.dynamic_slice(reduced[4], (0, _slot(_position()) * n_loc), (1, n_loc))
    ws = (norm_g, mem_norm_g, pool_scale, hgrn_lb, hgrn_norm_g, final_g)
    ms = (m_norm_g, m_mem_norm_g, m_pool_scale, m_hgrn_lb, m_hgrn_norm_g, m_final_g)
    vs = (v_norm_g, v_mem_norm_g, v_pool_scale, v_hgrn_lb, v_hgrn_norm_g, v_final_g)
    res = adamw_small("adamw_small", _pack(reduced), _pack(ws), _pack(ms), _pack(vs))
    outs = {nm: [reduced[i]] + [_unpack(r, ws)[i] for r in res] for i, nm in enumerate(names)}
    update("0a", 0, res[0])
    for nm, (w, _, _) in state.items():
        outs[nm] = [r.reshape(w.shape) for r in big[nm]]
    order = ("norm_g", "mem_norm_g", "w_kv", "w_out", "pool_w_in", "pool_w_grp", "pool_scale", "hgrn_w_in", "hgrn_lb",
             "hgrn_norm_g", "final_g")
    return (loss, grad_x[None], *[outs[nm][0] for nm in order], *[outs[nm][1] for nm in order],
            *[outs[nm][2] for nm in order], *[outs[nm][3] for nm in order])
```

```python
import functools

import jax
import jax.numpy as jnp
from jax import lax
from jax.experimental import pallas as pl
from jax.experimental.pallas import tpu as pltpu

F32, BF16 = jnp.float32, jnp.bfloat16
MM_DTYPE = BF16
COMM_DTYPE = BF16
EPS = 1e-6
POOL_WINDOWS = (2, 4, 8, 16)
POOL_HALO = 16
HG_HEAD_DIM = 128
HG_CHUNK = 64
CA_HEADS = 4
N_DEV = 8
ADAM_LR, ADAM_B1, ADAM_B2, ADAM_EPS, ADAM_WD, ADAM_STEP = 0.001, 0.9, 0.999, 1e-08, 0.01, 10
VMEM_LIMIT = 48 << 20
VMEM_LIMIT_TALL = 56 << 20
LANE = 128
MESH = pl.DeviceIdType.MESH
AXES = ("x", "y", "c")

NN = (((1,), (0,)), ((), ()))
NT = (((1,), (1,)), ((), ()))
TN = (((0,), (0,)), ((), ()))


def _cp(sem=None, vmem=VMEM_LIMIT):
    return pltpu.CompilerParams(dimension_semantics=sem, vmem_limit_bytes=vmem)


def _dot(a, b, dims):
    return lax.dot_general(a.astype(MM_DTYPE), b.astype(MM_DTYPE), dims, preferred_element_type=F32)


def _sig(v):
    return 1.0 / (1.0 + jnp.exp(-v))


def _tile(n, pref):
    if n <= pref:
        return n
    t = (pref // LANE) * LANE
    while n % t:
        t -= LANE
    return t


def _sds(shape, dtype):
    return jax.ShapeDtypeStruct(tuple(shape), dtype)


def _mm(name, grid, sem, ins, in_specs, out_shapes, out_specs, dims, epi=None, nk=1, acc_shape=None, aliases=None,
        vmem=VMEM_LIMIT):
    n_in, n_out = len(ins), len(out_shapes)

    def body(*refs):
        a_ref, b_ref = refs[0], refs[1]
        extra = refs[2:n_in]
        outs = refs[n_in:n_in + n_out]
        p = _dot(a_ref[...], b_ref[...], dims)

        def finish(acc):
            res = epi(acc, *extra) if epi is not None else (acc,)
            for r, o_ref in zip(res, outs):
                o_ref[...] = r.astype(o_ref.dtype)

        if nk == 1:
            finish(p)
        else:
            acc_ref = refs[n_in + n_out]
            k = pl.program_id(len(grid) - 1)

            @pl.when(k == 0)
            def _():
                acc_ref[...] = p

            @pl.when(k > 0)
            def _():
                acc_ref[...] += p

            @pl.when(k == nk - 1)
            def _():
                finish(acc_ref[...])

    scratch = [pltpu.VMEM(acc_shape, F32)] if nk > 1 else []
    return pl.pallas_call(
        body, name=name, grid=grid, in_specs=in_specs, out_specs=out_specs, out_shape=out_shapes,
        scratch_shapes=scratch, compiler_params=_cp(sem, vmem), input_output_aliases=aliases or {},
    )(*ins)


def proj_fwd(name, h, wblk):
    T, D = h.shape
    nblk, _, nb = wblk.shape
    tm = _tile(T, 1024)
    return _mm(name, (nblk, T // tm), ("parallel", "parallel"), [h, wblk],
               [pl.BlockSpec((tm, D), lambda j, i: (i, 0)), pl.BlockSpec((None, D, nb), lambda j, i: (j, 0, 0))],
               [_sds((T, nblk * nb), F32)], [pl.BlockSpec((tm, nb), lambda j, i: (i, j))], NN)[0]


def proj_bwd_w(name, h, dproj, nblk):
    T, D = h.shape
    N = dproj.shape[1]
    nb = N // nblk
    tm, tn = _tile(D, 1024 if nb <= 1024 else 512), _tile(nb, 2048)
    per = nb // tn
    return _mm(name, (D // tm, N // tn), ("parallel", "parallel"), [h, dproj],
               [pl.BlockSpec((T, tm), lambda i, n: (0, i)), pl.BlockSpec((T, tn), lambda i, n: (0, n))],
               [_sds((nblk, D, nb), COMM_DTYPE)], [pl.BlockSpec((None, tm, tn), lambda i, n: (n // per, i, n % per))], TN)[0]


def proj_bwd_x(name, dproj, wblk, after, tm=1024):
    T, N = dproj.shape
    nblk, D, nb = wblk.shape
    tm = _tile(T, tm)
    return _mm(name, (T // tm, nblk), ("parallel", "arbitrary"), [dproj, wblk, jnp.reshape(after, (1, 1))],
               [pl.BlockSpec((tm, nb), lambda i, k: (i, k)), pl.BlockSpec((None, D, nb), lambda i, k: (k, 0, 0)),
                pl.BlockSpec(memory_space=pl.ANY)],
               [_sds((T, D), F32)], [pl.BlockSpec((tm, D), lambda i, k: (i, 0))], NT, nk=nblk, acc_shape=(tm, D),
               vmem=VMEM_LIMIT_TALL)[0]


def out_fwd(name, branch, wout, x, after, tm=1024, tn=512):
    T, E = branch.shape
    D = wout.shape[1]
    tm, tn = _tile(T, tm), _tile(D, tn)
    return _mm(name, (T // tm, D // tn), ("parallel", "parallel"), [branch, wout, x, jnp.reshape(after, (1, 1))],
               [pl.BlockSpec((tm, E), lambda i, j: (i, 0)), pl.BlockSpec((E, tn), lambda i, j: (0, j)),
                pl.BlockSpec((tm, tn), lambda i, j: (i, j)), pl.BlockSpec(memory_space=pl.ANY)],
               [_sds((T, D), F32)], [pl.BlockSpec((tm, tn), lambda i, j: (i, j))], NN,
               epi=lambda acc, x_ref, _: (acc + x_ref[...],))[0]


def mm_nt(name, a, b, out_dtype, tm=1024, tn=2048, res=None):
    M, K = a.shape
    N = b.shape[0]
    tm, tn = _tile(M, tm), _tile(N, tn)
    ins = [a, b] + ([res] if res is not None else [])
    specs = [pl.BlockSpec((tm, K), lambda i, j: (i, 0)), pl.BlockSpec((tn, K), lambda i, j: (j, 0))]
    if res is not None:
        specs.append(pl.BlockSpec((tm, tn), lambda i, j: (i, j)))
    epi = (lambda acc, r_ref: (acc + r_ref[...],)) if res is not None else None
    return _mm(name, (M // tm, N // tn), ("parallel", "parallel"), ins, specs,
               [_sds((M, N), out_dtype)], [pl.BlockSpec((tm, tn), lambda i, j: (i, j))], NT, epi=epi)[0]


def mm_nn(name, a, b, out_dtype, tm=512, tn=512):
    M, K = a.shape
    N = b.shape[1]
    tm, tn = _tile(M, tm), _tile(N, tn)
    return _mm(name, (M // tm, N // tn), ("parallel", "parallel"), [a, b],
               [pl.BlockSpec((tm, K), lambda i, j: (i, 0)), pl.BlockSpec((K, tn), lambda i, j: (0, j))],
               [_sds((M, N), out_dtype)], [pl.BlockSpec((tm, tn), lambda i, j: (i, j))], NN)[0]


def mm_tn(name, a, b, out_dtype, tm=1024, tn=1024):
    K, M = a.shape
    N = b.shape[1]
    tm, tn = _tile(M, tm), _tile(N, tn)
    return _mm(name, (M // tm, N // tn), ("parallel", "parallel"), [a, b],
               [pl.BlockSpec((K, tm), lambda i, j: (0, i)), pl.BlockSpec((K, tn), lambda i, j: (0, j))],
               [_sds((M, N), out_dtype)], [pl.BlockSpec((tm, tn), lambda i, j: (i, j))], TN)[0]


def rmsnorm_fwd(name, x, g, tm=512):
    T, D = x.shape
    tm = _tile(T, tm)

    def body(x_ref, g_ref, o_ref):
        xf = x_ref[...]
        r = lax.rsqrt(jnp.mean(xf * xf, axis=-1, keepdims=True) + EPS)
        o_ref[...] = ((xf * r) * g_ref[...]).astype(o_ref.dtype)

    return pl.pallas_call(
        body, name=name, grid=(T // tm,),
        in_specs=[pl.BlockSpec((tm, D), lambda i: (i, 0)), pl.BlockSpec((1, D), lambda i: (0, 0))],
        out_specs=pl.BlockSpec((tm, D), lambda i: (i, 0)), out_shape=_sds((T, D), MM_DTYPE),
        compiler_params=_cp(("parallel",)),
    )(x, g.reshape(1, D))


def _rms_bwd_math(xf, g, dh):
    r = lax.rsqrt(jnp.mean(xf * xf, axis=-1, keepdims=True) + EPS)
    gd = dh * g
    dx = r * gd - xf * ((r * r * r) * jnp.mean(gd * xf, axis=-1, keepdims=True))
    dg = jnp.sum(dh * (xf * r), axis=0, keepdims=True)
    return dx, dg


def rmsnorm_bwd(name, x, g, dh, res=None, tm=512):
    T, D = x.shape
    tm = _tile(T, tm)
    has_res = res is not None

    def body(*refs):
        x_ref, g_ref, dh_ref = refs[:3]
        dx_ref, dxb_ref, dg_ref = refs[3 + has_res:]
        dx, dg = _rms_bwd_math(x_ref[...], g_ref[...], dh_ref[...])
        if has_res:
            dx = dx + refs[3][...]
        dx_ref[...] = dx
        dxb_ref[...] = dx.astype(dxb_ref.dtype)

        @pl.when(pl.program_id(0) == 0)
        def _():
            dg_ref[...] = jnp.zeros_like(dg_ref)

        dg_ref[...] += dg

    row = pl.BlockSpec((tm, D), lambda i: (i, 0))
    vec = pl.BlockSpec((1, D), lambda i: (0, 0))
    ins = [x, g.reshape(1, D), dh] + ([res] if has_res else [])
    return pl.pallas_call(
        body, name=name, grid=(T // tm,), in_specs=[row, vec, row] + ([row] if has_res else []),
        out_specs=[row, row, vec], out_shape=[_sds((T, D), F32), _sds((T, D), MM_DTYPE), _sds((1, D), F32)],
        compiler_params=_cp(("arbitrary",)),
    )(*ins)


def final_loss_bwd(name, x, g, target, tm=512):
    T, D = x.shape
    tm = _tile(T, tm)

    def body(x_ref, g_ref, t_ref, loss_ref, dx_ref, dxb_ref, dg_ref):
        xf, gv = x_ref[...], g_ref[...]
        r = lax.rsqrt(jnp.mean(xf * xf, axis=-1, keepdims=True) + EPS)
        err = (xf * r) * gv - t_ref[...]
        part = 0.5 * jnp.sum(jnp.mean(err * err, axis=-1, keepdims=True), axis=0, keepdims=True)
        dx, dg = _rms_bwd_math(xf, gv, err / D)
        dx_ref[...] = dx
        dxb_ref[...] = dx.astype(dxb_ref.dtype)

        @pl.when(pl.program_id(0) == 0)
        def _():
            dg_ref[...] = jnp.zeros_like(dg_ref)
            loss_ref[...] = jnp.zeros_like(loss_ref)

        dg_ref[...] += dg
        loss_ref[...] += jnp.broadcast_to(part, loss_ref.shape)

    row = pl.BlockSpec((tm, D), lambda i: (i, 0))
    vec = pl.BlockSpec((1, D), lambda i: (0, 0))
    return pl.pallas_call(
        body, name=name, grid=(T // tm,), in_specs=[row, vec, row],
        out_specs=[pl.BlockSpec((8, LANE), lambda i: (0, 0)), row, row, vec],
        out_shape=[_sds((8, LANE), F32), _sds((T, D), F32), _sds((T, D), MM_DTYPE), _sds((1, D), F32)],
        compiler_params=_cp(("arbitrary",)),
    )(x, g.reshape(1, D), target)


def _pool_tiles(T, emix):
    pg = emix // len(POOL_WINDOWS)
    tc = 256 if pg % 256 == 0 else LANE
    return pg, tc, _tile(T, 512)


def pool_fwd(name, proj, T, emix, after):
    pg, tc, R = _pool_tiles(T, emix)
    per_g = pg // tc

    def body(u_ref, _, o_ref):
        g = pl.program_id(0) // per_g
        for gi, w in enumerate(POOL_WINDOWS):
            @pl.when(g == gi)
            def _():
                for ci in range(T // R):
                    r0 = ci * R
                    cur = u_ref[r0:r0 + R, :]
                    halo = jnp.zeros((POOL_HALO, tc), F32) if ci == 0 else u_ref[r0 - POOL_HALO:r0, :]
                    s = jnp.concatenate([halo, cur], axis=0)
                    for st in range(w.bit_length() - 1):
                        s = s + pltpu.roll(s, 1 << st, axis=0)
                    t = r0 + lax.broadcasted_iota(jnp.int32, (R, 1), 0)
                    cnt = jnp.minimum(t + 1, w).astype(F32)
                    o_ref[r0:r0 + R, :] = (s[POOL_HALO:] / cnt - cur).astype(o_ref.dtype)

    return pl.pallas_call(
        body, name=name, grid=(emix // tc,),
        in_specs=[pl.BlockSpec((T, tc), lambda j: (0, j)), pl.BlockSpec(memory_space=pl.ANY)],
        out_specs=pl.BlockSpec((T, tc), lambda j: (0, j)), out_shape=_sds((T, emix), MM_DTYPE),
        compiler_params=_cp(("parallel",)),
    )(proj, jnp.reshape(after, (1, 1)))


def pool_bwd(name, dpooled, dproj, T, emix):
    pg, tc, R = _pool_tiles(T, emix)
    per_g = pg // tc

    def body(d_ref, _, o_ref):
        g = pl.program_id(0) // per_g
        for gi, w in enumerate(POOL_WINDOWS):
            @pl.when(g == gi)
            def _():
                n = R + POOL_HALO
                for ci in range(T // R):
                    r0 = ci * R
                    cur = d_ref[r0:r0 + R, :]
                    halo = jnp.zeros((POOL_HALO, tc), F32) if ci == T // R - 1 else d_ref[r0 + R:r0 + n, :]
                    t = r0 + lax.broadcasted_iota(jnp.int32, (n, 1), 0)
                    cnt = jnp.minimum(t + 1, w).astype(F32)
                    s = jnp.concatenate([cur, halo], axis=0) / cnt
                    for st in range(w.bit_length() - 1):
                        s = s + pltpu.roll(s, n - (1 << st), axis=0)
                    o_ref[r0:r0 + R, :] = (s[:R] - cur).astype(o_ref.dtype)

    return pl.pallas_call(
        body, name=name, grid=(emix // tc,),
        in_specs=[pl.BlockSpec((T, tc), lambda j: (0, j)), pl.BlockSpec(memory_space=pl.ANY)],
        out_specs=pl.BlockSpec((T, tc), lambda j: (0, j)), out_shape=_sds(dproj.shape, dproj.dtype),
        input_output_aliases={1: 0}, compiler_params=_cp(("parallel",)),
    )(dpooled, dproj)


def grp_fwd(name, pooled, wgrp, scale, proj, gate_off, E):
    T, emix = pooled.shape
    pg = emix // len(POOL_WINDOWS)
    tm = _tile(T, 2048)
    tc = 256 if pg % 256 == 0 and gate_off % 256 == 0 else LANE
    per = pg // tc

    def epi(acc, s_ref, g_ref):
        gt = g_ref[...]
        return acc, (acc * s_ref[...]) * (gt * _sig(gt))

    col = pl.BlockSpec((tm, tc), lambda n, i: (i, n))
    return _mm(name, (emix // tc, T // tm), ("parallel", "parallel"), [pooled, wgrp, scale, proj],
               [pl.BlockSpec((tm, pg), lambda n, i: (i, n // per)), pl.BlockSpec((None, pg, tc), lambda n, i: (n // per, 0, n % per)),
                pl.BlockSpec((1, tc), lambda n, i: (0, n)), pl.BlockSpec((tm, tc), lambda n, i: (i, gate_off // tc + n))],
               [_sds((T, emix), F32), _sds((T, E), MM_DTYPE)], [col, col], NN, epi=epi)


def grp_bwd_x(name, dy, wgrp):
    T, emix = dy.shape
    pg = emix // len(POOL_WINDOWS)
    tm = _tile(T, 1024)
    return _mm(name, (len(POOL_WINDOWS), T // tm), ("parallel", "parallel"), [dy, wgrp],
               [pl.BlockSpec((tm, pg), lambda g, i: (i, g)), pl.BlockSpec((None, pg, pg), lambda g, i: (g, 0, 0))],
               [_sds((T, emix), F32)], [pl.BlockSpec((tm, pg), lambda g, i: (i, g))], NT)[0]


def grp_bwd_w(name, pooled, dy):
    T, emix = dy.shape
    ng = len(POOL_WINDOWS)
    pg = emix // ng
    col = pl.BlockSpec((T, pg), lambda g: (0, g))
    return _mm(name, (ng,), ("parallel",), [pooled, dy], [col, col],
               [_sds((ng, pg, pg), COMM_DTYPE)], [pl.BlockSpec((None, pg, pg), lambda g: (g, 0, 0))], TN)[0]


def gate_bwd_pool(name, dbranch, y, scale, proj, gate_off, n_proj):
    T, emix = y.shape
    tm, tc = _tile(T, 1024), _tile(emix, 512)
    assert gate_off % tc == 0

    def body(db_ref, y_ref, s_ref, g_ref, dy_ref, dg_ref, ds_ref):
        db, yv, sc, gt = db_ref[...], y_ref[...], s_ref[...], g_ref[...]
        sg = _sig(gt)
        dmix = db * (gt * sg)
        dy_ref[...] = (dmix * sc).astype(dy_ref.dtype)
        dg_ref[...] = (db * (yv * sc) * (sg * (1.0 + gt * (1.0 - sg)))).astype(dg_ref.dtype)

        @pl.when(pl.program_id(1) == 0)
        def _():
            ds_ref[...] = jnp.zeros_like(ds_ref)

        ds_ref[...] += jnp.sum(dmix * yv, axis=0, keepdims=True)

    blk = pl.BlockSpec((tm, tc), lambda j, i: (i, j))
    vec = pl.BlockSpec((1, tc), lambda j, i: (0, j))
    gate = pl.BlockSpec((tm, tc), lambda j, i: (i, gate_off // tc + j))
    return pl.pallas_call(
        body, name=name, grid=(emix // tc, T // tm), in_specs=[blk, blk, vec, gate],
        out_specs=[blk, gate, vec],
        out_shape=[_sds((T, emix), MM_DTYPE), _sds((T, n_proj), MM_DTYPE), _sds((1, emix), F32)],
        compiler_params=_cp(("parallel", "arbitrary")),
    )(dbranch, y, scale, proj)


def attn_fwd(name, proj, kv, branch, q_off, gate_off, tm=2048):
    T = proj.shape[0]
    M, two_eca = kv.shape
    eca = two_eca // 2
    hd = eca // CA_HEADS
    E = branch.shape[1]
    tm = _tile(T, tm)
    scale = hd ** -0.5

    def body(q_ref, g_ref, k_ref, v_ref, _, ca_ref, br_ref):
        s = _dot(q_ref[...], k_ref[...], NT) * scale
        s = s - jnp.max(s, axis=-1, keepdims=True)
        e = jnp.exp(s)
        p = e / jnp.sum(e, axis=-1, keepdims=True)
        o = _dot(p, v_ref[...], NN)
        gt = g_ref[...]
        ca_ref[...] = o
        br_ref[...] = (o * (gt * _sig(gt))).astype(br_ref.dtype)

    return pl.pallas_call(
        body, name=name, grid=(CA_HEADS, T // tm),
        in_specs=[pl.BlockSpec((tm, hd), lambda h, i: (i, q_off // hd + h)),
                  pl.BlockSpec((tm, hd), lambda h, i: (i, (gate_off + E - eca) // hd + h)),
                  pl.BlockSpec((M, hd), lambda h, i: (0, h)), pl.BlockSpec((M, hd), lambda h, i: (0, CA_HEADS + h)),
                  pl.BlockSpec(memory_space=pl.ANY)],
        out_specs=[pl.BlockSpec((tm, hd), lambda h, i: (i, h)), pl.BlockSpec((tm, hd), lambda h, i: (i, (E - eca) // hd + h))],
        out_shape=[_sds((T, eca), F32), _sds(branch.shape, branch.dtype)],
        input_output_aliases={4: 1}, compiler_params=_cp(("parallel", "parallel")),
    )(proj, proj, kv, kv, branch)


def attn_bwd(name, proj, kv, ca, dbranch, dproj, q_off, gate_off, tm=2048):
    T = proj.shape[0]
    M, two_eca = kv.shape
    eca = two_eca // 2
    hd = eca // CA_HEADS
    E = dbranch.shape[1]
    tm = _tile(T, tm)
    scale = hd ** -0.5
    q_blk, g_blk = q_off // hd, (gate_off + E - eca) // hd

    n_rows = T // tm
    n_steps = CA_HEADS * n_rows

    def body(q_ref, g_ref, k_ref, v_ref, ca_ref, db_ref, _, dp_hbm, dk_ref, dv_ref, stash, sems):
        h, i = pl.program_id(0), pl.program_id(1)
        step = h * n_rows + i
        slot = step % 2

        def writes(slot, h, i):
            rows = pl.ds(pl.multiple_of(i * tm, tm), tm)
            return [pltpu.make_async_copy(stash.at[slot, r], dp_hbm.at[rows, pl.ds(pl.multiple_of((blk + h) * hd, hd), hd)],
                                          sems.at[slot, r]) for r, blk in enumerate((q_blk, g_blk))]

        @pl.when(step >= 2)
        def _():
            for cp in writes(slot, h, i):
                cp.wait()

        q, k, v = q_ref[...], k_ref[...], v_ref[...]
        s = _dot(q, k, NT) * scale
        s = s - jnp.max(s, axis=-1, keepdims=True)
        e = jnp.exp(s)
        p = e / jnp.sum(e, axis=-1, keepdims=True)
        gt, db = g_ref[...], db_ref[...]
        sg = _sig(gt)
        do = db * (gt * sg)
        stash[slot, 1] = (db * ca_ref[...] * (sg * (1.0 + gt * (1.0 - sg)))).astype(stash.dtype)
        dpr = _dot(do, v, NT)
        ds = p * (dpr - jnp.sum(dpr * p, axis=-1, keepdims=True)) * scale
        stash[slot, 0] = _dot(ds, k, NN).astype(stash.dtype)

        @pl.when(i == 0)
        def _():
            dk_ref[...] = jnp.zeros_like(dk_ref)
            dv_ref[...] = jnp.zeros_like(dv_ref)

        dk_ref[...] += _dot(ds, q, TN)
        dv_ref[...] += _dot(p, do, TN)
        for cp in writes(slot, h, i):
            cp.start()

        @pl.when(step == n_steps - 1)
        def _():
            for cp in writes(slot, h, i):
                cp.wait()
            if n_steps > 1:
                for cp in writes(1 - slot, h, i):
                    cp.wait()

    row = lambda blk: pl.BlockSpec((tm, hd), lambda h, i: (i, blk + h))
    outs = pl.pallas_call(
        body, name=name, grid=(CA_HEADS, n_rows),
        in_specs=[row(q_blk), row(g_blk), pl.BlockSpec((M, hd), lambda h, i: (0, h)),
                  pl.BlockSpec((M, hd), lambda h, i: (0, CA_HEADS + h)), row(0), row((E - eca) // hd),
                  pl.BlockSpec(memory_space=pl.ANY)],
        out_specs=[pl.BlockSpec(memory_space=pl.ANY), pl.BlockSpec((M, hd), lambda h, i: (0, h)),
                   pl.BlockSpec((M, hd), lambda h, i: (0, h))],
        out_shape=[_sds(dproj.shape, dproj.dtype), _sds((M, eca), F32), _sds((M, eca), F32)],
        scratch_shapes=[pltpu.VMEM((2, 2, tm, hd), dproj.dtype), pltpu.SemaphoreType.DMA((2, 2))],
        input_output_aliases={6: 0}, compiler_params=_cp(("arbitrary", "arbitrary")),
    )(proj, proj, kv, kv, ca, dbranch, dproj)
    return outs[0], jnp.concatenate([outs[1], outs[2]], axis=1)


def _split3(v):
    hi = v.astype(BF16)
    r1 = v - hi.astype(F32)
    mid = r1.astype(BF16)
    lo = (r1 - mid.astype(F32)).astype(BF16)
    return hi, mid, lo


def _tri_dot(tri, v):
    hi, mid, lo = _split3(v)
    d = functools.partial(lax.dot_general, dimension_numbers=NN, preferred_element_type=F32)
    return d(tri, hi) + d(tri, mid) + d(tri, lo)


def _causal_mask():
    C = HG_CHUNK
    return lax.broadcasted_iota(jnp.int32, (C, C), 0) >= lax.broadcasted_iota(jnp.int32, (C, C), 1)


def _cumsum_matrices(G):
    ri = lax.broadcasted_iota(jnp.int32, (G, G), 0)
    ci = lax.broadcasted_iota(jnp.int32, (G, G), 1)
    same = (ri // HG_CHUNK) == (ci // HG_CHUNK)
    return (jnp.where(same & (ri >= ci), 1.0, 0.0).astype(BF16), jnp.where(same & (ri <= ci), 1.0, 0.0).astype(BF16))


def _forget_gate(fi, lb):
    sig = _sig(fi)
    return sig, lb + (1.0 - lb) * sig


def _log_decay_cumsum(f_ref, b_s, lb, prefix, G):
    def group(i, carry):
        rows = pl.ds(pl.multiple_of(i * G, G), G)
        _, f = _forget_gate(f_ref[rows, :], lb)
        b_s[rows, :] = _tri_dot(prefix, jnp.log(f))
        return carry

    lax.fori_loop(0, b_s.shape[0] // G, group, 0)


def _lower_bound(lb2):
    l0, l1 = lb2[0:1, :], lb2[1:2, :]
    m = jnp.maximum(l0, l1)
    e0, e1 = jnp.exp(l0 - m), jnp.exp(l1 - m)
    sm0, sm1 = e0 / (e0 + e1), e1 / (e0 + e1)
    return (sm0 + sm1) - sm0, sm0, sm1


def _hg_chunk_fwd(qi, fi, lb, causal, b=None, prefix=None):
    sig, f = _forget_gate(fi, lb)
    k = 1.0 - f
    if b is None:
        b = _tri_dot(prefix, jnp.log(f))
    last = lax.broadcasted_iota(jnp.int32, b.shape, 0) == HG_CHUNK - 1
    bl = jnp.sum(jnp.where(last, b, 0.0), axis=0, keepdims=True)
    eb, enb, ete = jnp.exp(b), jnp.exp(-b), jnp.exp(bl - b)
    sq = _sig(qi)
    q = (qi * sq) * (HG_HEAD_DIM ** -0.5)
    q_dec, k_inv, k_te = q * eb, k * enb, k * ete
    att = jnp.where(causal, _dot(q_dec, k_inv, NT), 0.0) if causal is not None else None
    return dict(sig=sig, f=f, k=k, eb=eb, enb=enb, ete=ete, dcy=jnp.exp(bl), sq=sq, q_dec=q_dec, k_inv=k_inv, k_te=k_te,
                att=att, last=last)


def _hg_blocking(H, T):
    hb = 4 if H % 4 == 0 else (2 if H % 2 == 0 else 1)
    return hb, _tile(T, 1024)


def hgrn_fwd(name, proj, lb2, norm_g, emix, gate_off, E):
    T = proj.shape[0]
    hd, C = HG_HEAD_DIM, HG_CHUNK
    H, N = emix // hd, T // C
    hb, ts = _hg_blocking(H, T)
    W, nc, G = hb * hd, ts // C, _tile(ts, 256)

    def body(q_ref, f_ref, i_ref, g_ref, lb_ref, ng_ref, o_ref, st_ref, br_ref, state, b_s):
        @pl.when(pl.program_id(1) == 0)
        def _():
            state[...] = jnp.zeros_like(state)

        causal2 = (lax.broadcasted_iota(jnp.int32, (C, 2 * C), 0)
                   >= lax.broadcasted_iota(jnp.int32, (C, 2 * C), 1) % C)
        lb, _, _ = _lower_bound(lb_ref[...])
        ng = ng_ref[...]
        _log_decay_cumsum(f_ref, b_s, lb, _cumsum_matrices(G)[0], G)
        first = lax.broadcasted_iota(jnp.int32, (C, 2 * hd), 1) < hd
        same_head = (lax.broadcasted_iota(jnp.int32, (2 * hd, 2 * hd), 0) // hd
                     == lax.broadcasted_iota(jnp.int32, (2 * hd, 2 * hd), 1) // hd)

        def stacked(a):
            return jnp.concatenate([jnp.where(first, a, 0.0), jnp.where(first, 0.0, a)], axis=0)

        def chunk(n, carry):
            rows = pl.ds(pl.multiple_of(n * C, C), C)
            for p in range(hb // 2):
                cols = slice(2 * p * hd, 2 * (p + 1) * hd)
                v = i_ref[rows, cols]
                c = _hg_chunk_fwd(q_ref[rows, cols], f_ref[rows, cols], lb[:, cols], None, b=b_s[rows, cols])
                att = jnp.where(causal2, _dot(c["q_dec"], stacked(c["k_inv"]), NT), 0.0)
                st = state[p]
                o = _dot(att, stacked(v), NN) + _dot(c["q_dec"], st, NT)
                for j in range(2):
                    st_ref[2 * p + j, n] = st[j * hd:(j + 1) * hd, j * hd:(j + 1) * hd]
                o_ref[rows, cols] = o
                ms = jnp.where(first, jnp.mean(jnp.where(first, o * o, 0.0), axis=-1, keepdims=True),
                               jnp.mean(jnp.where(first, 0.0, o * o), axis=-1, keepdims=True)) * 2.0
                r = lax.rsqrt(ms + EPS)
                gt = g_ref[rows, cols]
                br_ref[rows, cols] = (((o * r) * ng[:, cols]) * (gt * _sig(gt))).astype(br_ref.dtype)
                state[p] = st * c["dcy"] + jnp.where(same_head, _dot(v, c["k_te"], TN), 0.0)
            return carry

        lax.fori_loop(0, nc, chunk, 0, unroll=8)

    col = lambda off: pl.BlockSpec((ts, W), lambda h, s: (s, off // W + h))
    vec = lambda r: pl.BlockSpec((r, W), lambda h, s: (0, h))
    return pl.pallas_call(
        body, name=name, grid=(H // hb, T // ts),
        in_specs=[col(0), col(emix), col(2 * emix), col(gate_off), vec(2), vec(1)],
        out_specs=[col(0), pl.BlockSpec((hb, nc, hd, hd), lambda h, s: (h, s, 0, 0)), col(0)],
        out_shape=[_sds((T, emix), F32), _sds((H, N, hd, hd), F32), _sds((T, E), MM_DTYPE)],
        scratch_shapes=[pltpu.VMEM((hb // 2, 2 * hd, 2 * hd), F32), pltpu.VMEM((ts, W), F32)],
        compiler_params=_cp(("parallel", "arbitrary")),
    )(proj, proj, proj, proj, lb2, norm_g)


def hgrn_bwd(name, proj, lb2, norm_g, o_pre, states, dbranch, emix, gate_off):
    T, n_proj = proj.shape
    hd, C = HG_HEAD_DIM, HG_CHUNK
    H, N = emix // hd, T // C
    hb, ts = _hg_blocking(H, T)
    W, nc, ns = hb * hd, ts // C, T // ts
    qscale = HG_HEAD_DIM ** -0.5

    n_steps = (H // hb) * ns
    regions = (0, emix, 2 * emix, gate_off)

    def body(q_ref, f_ref, i_ref, g_ref, lb_ref, ng_ref, o_ref, st_ref, db_ref, dp_hbm, dlb_ref, dng_ref,
             stash, dst_ref, acc, sems):
        hblk, s = pl.program_id(0), pl.program_id(1)
        step = hblk * ns + s
        slot = step % 2

        def writes(slot, hblk, s):
            rows = pl.ds(pl.multiple_of((ns - 1 - s) * ts, ts), ts)
            return [pltpu.make_async_copy(stash.at[slot, r], dp_hbm.at[rows, pl.ds(pl.multiple_of(off + hblk * W, W), W)],
                                          sems.at[slot, r]) for r, off in enumerate(regions)]

        @pl.when(step >= 2)
        def _():
            for cp in writes(slot, hblk, s):
                cp.wait()

        @pl.when(s == 0)
        def _():
            dst_ref[...] = jnp.zeros_like(dst_ref)
            acc[...] = jnp.zeros_like(acc)

        causal = _causal_mask()
        prefix, suffix = _cumsum_matrices(C)
        lb, sm0, sm1 = _lower_bound(lb_ref[...])
        ng = ng_ref[...]

        def chunk(j, carry):
            n = nc - 1 - j
            rows = pl.ds(pl.multiple_of(n * C, C), C)
            for h in range(hb):
                cols = slice(h * hd, (h + 1) * hd)
                dst = dst_ref[h]
                qi, v, gt = q_ref[rows, cols], i_ref[rows, cols], g_ref[rows, cols]
                c = _hg_chunk_fwd(qi, f_ref[rows, cols], lb[:, cols], causal, prefix=prefix)
                st = st_ref[h, n]
                o, db, ngh = o_ref[rows, cols], db_ref[rows, cols], ng[:, cols]
                r = lax.rsqrt(jnp.mean(o * o, axis=-1, keepdims=True) + EPS)
                sg = _sig(gt)
                dmix = db * (gt * sg)
                dgate = db * ((o * r) * ngh) * (sg * (1.0 + gt * (1.0 - sg)))
                dyn = dmix * ngh
                do = r * dyn - o * ((r * r * r) * jnp.mean(dyn * o, axis=-1, keepdims=True))
                acc[1:2, cols] += jnp.sum(dmix * (o * r), axis=0, keepdims=True)
                datt = jnp.where(causal, _dot(do, v, NT), 0.0)
                dq_dec = _dot(datt, c["k_inv"], NN) + _dot(do, st, NN)
                dk_inv = _dot(datt, c["q_dec"], TN)
                dv = _dot(c["att"], do, TN) + _dot(c["k_te"], dst, NT)
                dk_te = _dot(v, dst, NN)
                ddcy = jnp.sum(dst * st, axis=0, keepdims=True)
                dst_ref[h] = _dot(do, c["q_dec"], TN) + dst * c["dcy"]
                t_te = dk_te * c["k_te"]
                dbv = dq_dec * c["q_dec"] - dk_inv * c["k_inv"] - t_te
                dbl = jnp.sum(t_te, axis=0, keepdims=True) + ddcy * c["dcy"]
                dbv = dbv + jnp.where(c["last"], dbl, 0.0)
                dlf = _tri_dot(suffix, dbv)
                dk = dk_inv * c["enb"] + dk_te * c["ete"]
                df = dlf / c["f"] - dk
                sig, sq = c["sig"], c["sq"]
                acc[0:1, cols] += jnp.sum(df * (1.0 - sig), axis=0, keepdims=True)
                dqi = (dq_dec * c["eb"]) * qscale * (sq * (1.0 + qi * (1.0 - sq)))
                stash[slot, 0, rows, cols] = dqi.astype(stash.dtype)
                stash[slot, 1, rows, cols] = (df * (1.0 - lb[:, cols]) * (sig * (1.0 - sig))).astype(stash.dtype)
                stash[slot, 2, rows, cols] = dv.astype(stash.dtype)
                stash[slot, 3, rows, cols] = dgate.astype(stash.dtype)
            return carry

        lax.fori_loop(0, nc, chunk, 0, unroll=8)
        t = acc[0:1, :] * (sm0 * sm1)
        dlb_ref[...] = jnp.concatenate([-t, t], axis=0)
        dng_ref[...] = acc[1:2, :]
        for cp in writes(slot, hblk, s):
            cp.start()

        @pl.when(step == n_steps - 1)
        def _():
            for cp in writes(slot, hblk, s):
                cp.wait()
            if n_steps > 1:
                for cp in writes(1 - slot, hblk, s):
                    cp.wait()

    col = lambda off: pl.BlockSpec((ts, W), lambda h, s: (ns - 1 - s, off // W + h))
    vec = lambda r: pl.BlockSpec((r, W), lambda h, s: (0, h))
    return pl.pallas_call(
        body, name=name, grid=(H // hb, ns),
        in_specs=[col(0), col(emix), col(2 * emix), col(gate_off), vec(2), vec(1), col(0),
                  pl.BlockSpec((hb, nc, hd, hd), lambda h, s: (h, ns - 1 - s, 0, 0)), col(0)],
        out_specs=[pl.BlockSpec(memory_space=pl.ANY), vec(2), vec(1)],
        out_shape=[_sds((T, n_proj), MM_DTYPE), _sds((2, emix), F32), _sds((1, emix), F32)],
        scratch_shapes=[pltpu.VMEM((2, 4, ts, W), MM_DTYPE), pltpu.VMEM((hb, hd, hd), F32), pltpu.VMEM((8, W), F32),
                        pltpu.SemaphoreType.DMA((2, 4))],
        compiler_params=_cp(("arbitrary", "arbitrary")),
    )(proj, proj, proj, proj, lb2, norm_g, o_pre, states, dbranch)


class WholeWeights:
    def __init__(self, wkv, wout, wpin, wgrp, whin, hgrn_norm_g):
        self.w = dict(wkv=wkv, wout=wout, wpin=wpin, wgrp=wgrp, whin=whin, hnorm=hgrn_norm_g)
        self.grads = {}

    def first_weights(self):
        return self.w["wpin"], self.w["hnorm"], 0.0

    def after_proj0(self, after):
        return 0.0

    def layer0_weights(self, after):
        return self.w["wgrp"], self.w["wkv"][0], self.w["wout"][0]

    def after_out0(self, after):
        return 0.0

    def layer1_weights(self, after):
        return self.w["whin"]

    def after_proj1(self, after):
        return 0.0

    def layer1_rest(self, after):
        return self.w["wkv"][1], self.w["wout"][1]

    def grads_ready(self, layer, grads):
        self.grads[layer] = grads
        return 0.0

    def grads_reduce(self, layer, after):
        return 0.0


def run_step(x, mem, target, norm_g, mem_norm_g, pool_scale, hgrn_lb, final_g, comm):
    T, D = x.shape
    mem_n = rmsnorm_fwd("mem_norm", mem, mem_norm_g)
    wpin, hgrn_norm_g, zero = comm.first_weights()
    h0 = rmsnorm_fwd("norm0", x, norm_g[0] + zero)
    proj0 = proj_fwd("proj0", h0, wpin)
    zero = comm.after_proj0(proj0)
    n0 = proj0.shape[1]
    E = n0 // 2
    eca = E // 4
    emix = E - eca
    q0, g0 = emix, emix + eca
    q1, g1 = 3 * emix, 3 * emix + eca
    pooled = pool_fwd("pool_fwd", proj0, T, emix, zero)
    wgrp, wkv0, wout0 = comm.layer0_weights(pooled)
    y0, branch0 = grp_fwd("grp_fwd", pooled, wgrp, pool_scale, proj0, g0, E)
    kv0 = mm_nn("kv0", mem_n, wkv0, MM_DTYPE)
    ca0, branch0 = attn_fwd("attn_fwd0", proj0, kv0, branch0, q0, g0)
    x1 = out_fwd("out0", branch0, wout0, x, ca0[:1, :1])
    h1 = rmsnorm_fwd("norm1", x1, norm_g[1] + comm.after_out0(x1))
    whin = comm.layer1_weights(h1)
    proj1 = proj_fwd("proj1", h1, whin)
    zero = comm.after_proj1(proj1)
    o1, states, branch1 = hgrn_fwd("hgrn_fwd", proj1, hgrn_lb + zero, hgrn_norm_g, emix, g1, E)
    wkv1, wout1 = comm.layer1_rest(o1)
    kv1 = mm_nn("kv1", mem_n, wkv1, MM_DTYPE)
    ca1, branch1 = attn_fwd("attn_fwd1", proj1, kv1, branch1, q1, g1)
    x2 = out_fwd("out1", branch1, wout1, x1, ca1[:1, :1])
    loss, dx2, dx2b, g_final = final_loss_bwd("final", x2, final_g, target)
    g_wout1 = mm_tn("gwout1", branch1, dx2b, COMM_DTYPE)
    dbranch1 = mm_nt("dbranch1", dx2b, wout1, F32)
    dproj1, g_lb, g_hnorm = hgrn_bwd("hgrn_bwd", proj1, hgrn_lb, hgrn_norm_g, o1, states, dbranch1, emix, g1)
    dproj1, dkv1 = attn_bwd("attn_bwd1", proj1, kv1, ca1, dbranch1, dproj1, q1, g1)
    g_wkv1 = mm_tn("gwkv1", mem_n, dkv1, COMM_DTYPE)
    dmem_n = mm_nt("dmem1", dkv1, wkv1, F32)
    g_whin = proj_bwd_w("gwhin", h1, dproj1, whin.shape[0])
    zero = comm.grads_ready(1, dict(w_kv=g_wkv1, w_out=g_wout1, hgrn_w_in=g_whin))
    dh1 = proj_bwd_x("dh1", dproj1, whin, zero)
    zero = zero + comm.grads_reduce(1, dh1)
    dx1, dx1b, g_norm1 = rmsnorm_bwd("norm1_bwd", x1, norm_g[1] + zero, dh1, res=dx2)
    dbranch0 = mm_nt("dbranch0", dx1b, wout0, F32)
    dy0, dproj0, g_scale = gate_bwd_pool("gate_bwd0", dbranch0, y0, pool_scale, proj0, g0, n0)
    dpooled = grp_bwd_x("dpooled", dy0, wgrp)
    dproj0 = pool_bwd("pool_bwd", dpooled, dproj0, T, emix)
    dproj0, dkv0 = attn_bwd("attn_bwd0", proj0, kv0, ca0, dbranch0, dproj0, q0, g0)
    g_wpin = proj_bwd_w("gwpin", h0, dproj0, wpin.shape[0])
    dkv0 = dkv0 + comm.grads_ready("0b", dict(pool_w_in=g_wpin))
    g_wkv0 = mm_tn("gwkv0", mem_n, dkv0, COMM_DTYPE)
    dmem_n = mm_nt("dmem0", dkv0, wkv0, F32, res=dmem_n)
    g_wout0 = mm_tn("gwout0", branch0, dx1b, COMM_DTYPE)
    g_wgrp = grp_bwd_w("gwgrp", pooled, dy0)
    zero = comm.grads_reduce("0b", g_wgrp)
    zero = zero + comm.grads_ready("0a", dict(w_kv=g_wkv0, w_out=g_wout0, pool_w_grp=g_wgrp))
    dh0 = proj_bwd_x("dh0", dproj0, wpin, zero)
    zero = zero + comm.grads_reduce("0a", dh0)
    grad_x, _, g_norm0 = rmsnorm_bwd("norm0_bwd", x, norm_g[0] + zero, dh0, res=dx1)
    _, _, g_mem = rmsnorm_bwd("mem_norm_bwd", mem, mem_norm_g, dmem_n)

    small = dict(norm_g=jnp.concatenate([g_norm0, g_norm1], axis=0), mem_norm_g=g_mem[0], pool_scale=g_scale,
                 hgrn_lb=g_lb, hgrn_norm_g=g_hnorm, final_g=g_final[0])
    return loss[0, 0], grad_x, small


def _position():
    return lax.axis_index("x"), lax.axis_index("y"), lax.axis_index("c")


def _slot(p):
    return 4 * p[0] + 2 * p[1] + p[2]


def all_gather_blocks(name, blocks):
    n = len(blocks)
    halved = [b.shape[0] % 32 == 0 for b in blocks]

    def body(*refs):
        ins, outs, token = refs[:n], refs[n:2 * n], refs[2 * n]
        send_sems, recv_sems, local_sems = refs[2 * n + 1:]
        token[...] = jnp.zeros_like(token)
        x, y, c = _position()
        me, sibling = (x, y, c), (x, y, 1 - c)
        x_nbr, y_nbr, diag = _other_chips(x, y)

        def copy(t, k, block, to, src=None, rows=None):
            dst = outs[t].at[_slot(block)]
            if rows is not None:
                dst = dst.at[rows]
            return pltpu.make_async_remote_copy(src_ref=dst if src is None else src, dst_ref=dst, send_sem=send_sems.at[t, k],
                                                recv_sem=recv_sems.at[t, k], device_id=to, device_id_type=MESH)

        mine = [pltpu.make_async_copy(ins[t], outs[t].at[_slot(me)], local_sems.at[t]) for t in range(n)]
        for cp in mine:
            cp.start()
        sends = []
        for t in range(n):
            sends += [copy(t, 0, me, sibling, src=ins[t]), copy(t, 1, me, (*x_nbr, c), src=ins[t]),
                      copy(t, 2, me, (*y_nbr, c), src=ins[t])]
            if not halved[t]:
                sends.append(copy(t, 4, me, (*diag, c), src=ins[t]))
        for cp in sends:
            cp.start()

        def start(cp):
            cp.start()
            sends.append(cp)

        for t in range(n):
            half = blocks[t].shape[0] // 2
            top, bottom = pl.ds(0, half), pl.ds(half, half)
            copy(t, 1, (*x_nbr, c), me).wait_recv()
            if halved[t]:
                start(copy(t, 4, (*x_nbr, c), (*y_nbr, c), rows=top))
            start(copy(t, 3, (*x_nbr, c), sibling))
            copy(t, 2, (*y_nbr, c), me).wait_recv()
            if halved[t]:
                start(copy(t, 6, (*y_nbr, c), (*x_nbr, c), rows=bottom))
            start(copy(t, 5, (*y_nbr, c), sibling))
            if halved[t]:
                copy(t, 4, (*diag, c), me, rows=top).wait_recv()
                copy(t, 6, (*diag, c), me, rows=bottom).wait_recv()
            else:
                copy(t, 4, (*diag, c), me).wait_recv()
            start(copy(t, 7, (*diag, c), sibling))
        for t in range(n):
            copy(t, 0, sibling, me).wait_recv()
            for k, chip in ((3, x_nbr), (5, y_nbr), (7, diag)):
                copy(t, k, (*chip, 1 - c), me).wait_recv()
        for cp in sends:
            cp.wait_send()
        for cp in mine:
            cp.wait()

    any_spec = pl.BlockSpec(memory_space=pl.ANY)
    return pl.pallas_call(
        body, name=name, in_specs=[any_spec] * n, out_specs=[any_spec] * n + [pl.BlockSpec(memory_space=pltpu.VMEM)],
        out_shape=[_sds((N_DEV,) + b.shape, b.dtype) for b in blocks] + [_sds((8, LANE), F32)],
        scratch_shapes=[pltpu.SemaphoreType.DMA((n, 8)), pltpu.SemaphoreType.DMA((n, 8)), pltpu.SemaphoreType.DMA((n,))],
        compiler_params=pltpu.CompilerParams(has_side_effects=True),
    )(*blocks)


_HBM = pl.BlockSpec(memory_space=pltpu.HBM)
_SEM = pl.BlockSpec(memory_space=pltpu.SEMAPHORE)
_EFFECT = pltpu.SideEffectType.DATAFLOW_SIDE_EFFECTING


def split_start(name, bufs, n_copies, build):
    n = len(bufs)

    def body(*refs):
        for cp in build(refs[:n], refs[n], refs[n + 1]):
            cp.start()
        refs[-1][...] = jnp.zeros_like(refs[-1])

    outs = pl.pallas_call(
        body, name=name, in_specs=[_HBM] * n,
        out_shape=(pltpu.SemaphoreType.DMA((n_copies,)), pltpu.SemaphoreType.DMA((n_copies,)),
                   *[pltpu.HBM(b.shape, b.dtype) for b in bufs], _sds((8, LANE), F32)),
        out_specs=(_SEM, _SEM, *[_HBM] * n, pl.BlockSpec(memory_space=pltpu.VMEM)),
        input_output_aliases={i: 2 + i for i in range(n)},
        compiler_params=pltpu.CompilerParams(has_side_effects=_EFFECT),
    )(*[pltpu.with_memory_space_constraint(b, pltpu.HBM) for b in bufs])
    return (outs[0], outs[1]), list(outs[2:2 + n]), outs[-1]


def split_wait(name, sems, bufs, build, after):
    n = len(bufs)

    def body(*refs):
        for cp in build(refs[:n], refs[n], refs[n + 1]):
            cp.wait()

    return list(pl.pallas_call(
        body, name=name, in_specs=[_HBM] * n + [_SEM, _SEM, pl.BlockSpec(memory_space=pl.ANY)],
        out_shape=[pltpu.HBM(b.shape, b.dtype) for b in bufs], out_specs=[_HBM] * n,
        input_output_aliases={i: i for i in range(n)},
        compiler_params=pltpu.CompilerParams(has_side_effects=_EFFECT),
    )(*bufs, sems[0], sems[1], after))


def _remote(src, dst, send_sems, recv_sems, k, to):
    return pltpu.make_async_remote_copy(src_ref=src, dst_ref=dst, send_sem=send_sems.at[k], recv_sem=recv_sems.at[k],
                                        device_id=to, device_id_type=MESH)


def _other_chips(x, y):
    return [(1 - x, y), (x, 1 - y), (1 - x, 1 - y)]


def _gather_cross(n):
    def build(refs, ss, rs):
        x, y, c = _position()
        me = _slot((x, y, c))
        targets = [(x, y, 1 - c)] + [(*chip, c) for chip in _other_chips(x, y)]
        return [_remote(refs[t], refs[n + t].at[me], ss, rs, 4 * t + k, to) for t in range(n) for k, to in enumerate(targets)]
    return build


def _gather_pass(n):
    def build(refs, ss, rs):
        x, y, c = _position()
        cps = []
        for t in range(n):
            for j, chip in enumerate(_other_chips(x, y)):
                blk = refs[t].at[_slot((*chip, c))]
                cps.append(_remote(blk, blk, ss, rs, 3 * t + j, (x, y, 1 - c)))
        return cps
    return build


def _reduce_pair(n):
    def build(refs, ss, rs):
        x, y, c = _position()
        return [_remote(refs[t].at[j, 1 - c], refs[n + t].at[j], ss, rs, 4 * t + j, (x, y, 1 - c))
                for t in range(n) for j in range(4)]
    return build


def _reduce_cross(n):
    def build(refs, ss, rs):
        x, y, c = _position()
        return [_remote(refs[t].at[2 * chip[0] + chip[1]], refs[n + t].at[r], ss, rs, 3 * t + r, (*chip, c))
                for t in range(n) for r, chip in enumerate(_other_chips(x, y))]
    return build


def pair_add(name, part, landed, tr=1024):
    _, _, R, C = part.shape
    tr = _row_tile(R, tr)

    def body(_, p_ref, l_ref, o_ref):
        o_ref[...] = (p_ref[...].astype(F32) + l_ref[...].astype(F32)).astype(o_ref.dtype)

    blk = pl.BlockSpec((None, tr, C), lambda j, i, core: (j, i, 0))
    core = lax.axis_index("c").astype(jnp.int32).reshape(1)
    return pl.pallas_call(
        body, name=name, out_shape=_sds((4, R, C), part.dtype), compiler_params=_cp(("parallel", "parallel")),
        grid_spec=pltpu.PrefetchScalarGridSpec(
            num_scalar_prefetch=1, grid=(4, R // tr),
            in_specs=[pl.BlockSpec((None, None, tr, C), lambda j, i, core: (j, core[0], i, 0)), blk], out_specs=blk),
    )(core, part, landed)


def _row_tile(R, pref):
    if R <= pref:
        return R
    t = (pref // 16) * 16
    while R % t:
        t -= 16
    return t


def all_reduce_small(name, vec, after):
    def body(v_ref, _, sum_ref, land_ref, send_sems, recv_sems):
        x, y, c = _position()
        me = (x, y, c)
        flip = lambda v, bit: 1 - v if bit else v
        peers = [(flip(x, k & 4), flip(y, k & 2), flip(c, k & 1)) for k in range(1, N_DEV)]
        land_ref[_slot(me)] = v_ref[...]
        sends = [pltpu.make_async_remote_copy(src_ref=v_ref, dst_ref=land_ref.at[_slot(me)], send_sem=send_sems.at[k],
                                              recv_sem=recv_sems.at[k], device_id=peer, device_id_type=MESH)
                 for k, peer in enumerate(peers)]
        for cp in sends:
            cp.start()
        for k, peer in enumerate(peers):
            pltpu.make_async_remote_copy(src_ref=v_ref, dst_ref=land_ref.at[_slot(peer)], send_sem=send_sems.at[k],
                                         recv_sem=recv_sems.at[k], device_id=peer, device_id_type=MESH).wait_recv()
        acc = land_ref[0]
        for s in range(1, N_DEV):
            acc = acc + land_ref[s]
        sum_ref[...] = acc
        for cp in sends:
            cp.wait_send()

    vm = pl.BlockSpec(memory_space=pltpu.VMEM)
    return pl.pallas_call(
        body, name=name, in_specs=[vm, pl.BlockSpec(memory_space=pl.ANY)], out_specs=[vm, vm],
        out_shape=[_sds(vec.shape, F32), _sds((N_DEV,) + vec.shape, F32)],
        scratch_shapes=[pltpu.SemaphoreType.DMA((7,)), pltpu.SemaphoreType.DMA((7,))],
        compiler_params=pltpu.CompilerParams(has_side_effects=True),
    )(vec, after)[0]


def _adamw_math(g, w, m, v):
    m2 = ADAM_B1 * m + (1.0 - ADAM_B1) * g
    v2 = ADAM_B2 * v + (1.0 - ADAM_B2) * (g * g)
    m_hat = m2 / (1.0 - ADAM_B1 ** ADAM_STEP)
    v_hat = v2 / (1.0 - ADAM_B2 ** ADAM_STEP)
    return -ADAM_LR * (m_hat / (jnp.sqrt(v_hat) + ADAM_EPS) + ADAM_WD * w), m2, v2


def adamw_shard(name, chip_parts, landed, w, m, v, layer, prev=None, tr=256):
    L, R, C = w.shape
    tr = _row_tile(R, tr)
    n_in = 5 + (4 if prev is not None else 0)

    def body(*refs):
        own_ref, land_ref, w_ref, m_ref, v_ref = refs[1:6]
        g_ref, d_ref, nm_ref, nv_ref = refs[1 + n_in:]
        g = own_ref[...].astype(F32)
        for s in range(landed.shape[0]):
            g = g + land_ref[s].astype(F32)
        g_ref[...] = g
        d_ref[...], nm_ref[...], nv_ref[...] = _adamw_math(g, w_ref[...], m_ref[...], v_ref[...])

    lay = pl.BlockSpec((None, tr, C), lambda i, chip: (layer, i, 0))
    own = pl.BlockSpec((None, tr, C), lambda i, chip: (chip[0], i, 0))
    chip = (2 * lax.axis_index("x") + lax.axis_index("y")).astype(jnp.int32).reshape(1)
    ins = [chip_parts, landed, w, m, v] + (list(prev) if prev is not None else [])
    return pl.pallas_call(
        body, name=name, out_shape=[_sds((L, R, C), F32)] * 4,
        grid_spec=pltpu.PrefetchScalarGridSpec(
            num_scalar_prefetch=1, grid=(R // tr,),
            in_specs=[own, pl.BlockSpec((landed.shape[0], tr, C), lambda i, chip: (0, i, 0)), lay, lay, lay]
            + [pl.BlockSpec(memory_space=pl.ANY)] * (n_in - 5),
            out_specs=[lay] * 4),
        input_output_aliases={6 + i: i for i in range(n_in - 5)}, compiler_params=_cp(("parallel",)),
    )(chip, *ins)


def adamw_small(name, g, w, m, v):
    def body(g_ref, w_ref, m_ref, v_ref, d_ref, nm_ref, nv_ref):
        d_ref[...], nm_ref[...], nv_ref[...] = _adamw_math(g_ref[...], w_ref[...], m_ref[...], v_ref[...])

    return pl.pallas_call(body, name=name, out_shape=[_sds(w.shape, F32)] * 3)(g, w, m, v)


def _pack(vs):
    flat = jnp.concatenate([v.reshape(-1) for v in vs])
    return flat.reshape(-1, LANE)


def _unpack(packed, like):
    flat, out, off = packed.reshape(-1), [], 0
    for v in like:
        out.append(flat[off:off + v.size].reshape(v.shape))
        off += v.size
    return out


class MeshWeights:
    def __init__(self, w_kv, w_out, pool_w_in, pool_w_grp, hgrn_w_in, hgrn_norm_g):
        self.n_grp, self.pg_loc, self.pg = pool_w_grp.shape[1:]
        wpin, hnorm, token = all_gather_blocks("gather_first", [pool_w_in[0].astype(COMM_DTYPE), hgrn_norm_g])
        self.first = (wpin, hnorm.reshape(1, -1))
        zero = token[0, 0]
        cast = lambda a: (a + zero).astype(COMM_DTYPE)
        self.gather, self.tokens = {}, []
        for layer, blocks in ((0, [cast(pool_w_grp[0]), cast(w_kv[0]), cast(w_out[0])]),
                              (1, [cast(hgrn_w_in[0])]), (2, [cast(w_kv[1]), cast(w_out[1])])):
            me = _slot(_position())
            lands = [lax.dynamic_update_index_in_dim(lax.empty((N_DEV,) + b.shape, b.dtype), b, me, 0) for b in blocks]
            n = len(blocks)
            sems, bufs, token = split_start(f"gather{layer}_cross", blocks + lands, 4 * n, _gather_cross(n))
            self.gather[layer] = (sems, bufs)
            self.tokens.append(token[0, 0])
        self.reduce = {}

    def first_weights(self):
        return self.first[0], self.first[1], self.tokens[0] + self.tokens[1] + self.tokens[2]

    def _pass_on(self, layer, after):
        sems, bufs = self.gather[layer]
        n = len(bufs) // 2
        gathered = split_wait(f"gather{layer}_cross_wait", sems, bufs, _gather_cross(n), after)[n:]
        sems, bufs, token = split_start(f"gather{layer}_pass", gathered, 3 * n, _gather_pass(n))
        self.gather[layer] = (sems, bufs)
        return token[0, 0]

    def _gathered(self, layer, after):
        sems, bufs = self.gather[layer]
        return split_wait(f"gather{layer}_pass_wait", sems, bufs, _gather_pass(len(bufs)), after)

    def after_proj0(self, after):
        return self._pass_on(0, after)

    def layer0_weights(self, after):
        wgrp, wkv, wout = self._gathered(0, after)
        rows = lambda g: g.reshape(-1, g.shape[-1])
        return wgrp.transpose(1, 0, 2, 3).reshape(self.n_grp, self.pg, self.pg), rows(wkv), rows(wout)

    def after_out0(self, after):
        return self._pass_on(1, after)

    def layer1_weights(self, after):
        return self._gathered(1, after)[0]

    def after_proj1(self, after):
        return self._pass_on(2, after)

    def layer1_rest(self, after):
        wkv, wout = self._gathered(2, after)
        rows = lambda g: g.reshape(-1, g.shape[-1])
        return rows(wkv), rows(wout)

    def grads_ready(self, layer, grads):
        parts = {}
        for nm, g in grads.items():
            if nm == "pool_w_grp":
                g = g.reshape(self.n_grp, N_DEV, self.pg_loc, self.pg).transpose(1, 0, 2, 3)
            parts[nm] = g.reshape(4, 2, -1, g.shape[-1])
        names, srcs = list(parts), list(parts.values())
        lands = [lax.empty((4,) + p.shape[2:], p.dtype) for p in srcs]
        sems, bufs, token = split_start(f"reduce{layer}_pair", srcs + lands, 4 * len(srcs), _reduce_pair(len(srcs)))
        self.reduce[layer] = (names, sems, bufs)
        return token[0, 0]

    def grads_reduce(self, layer, after):
        names, sems, bufs = self.reduce[layer]
        n = len(names)
        bufs = split_wait(f"reduce{layer}_pair_wait", sems, bufs, _reduce_pair(n), after)
        chip_parts = [pair_add(f"pair_add_{nm}{layer}", bufs[t], bufs[n + t]) for t, nm in enumerate(names)]
        lands = [lax.empty((3,) + p.shape[1:], p.dtype) for p in chip_parts]
        sems, bufs, token = split_start(f"reduce{layer}_cross", chip_parts + lands, 3 * n, _reduce_cross(n))
        self.reduce[layer] = (names, sems, bufs)
        return token[0, 0]

    def grads_done(self, layer, after):
        names, sems, bufs = self.reduce[layer]
        n = len(names)
        bufs = split_wait(f"reduce{layer}_cross_wait", sems, bufs, _reduce_cross(n), after)
        return {nm: (bufs[t], bufs[n + t]) for t, nm in enumerate(names)}


def kernel(x, mem, norm_g, mem_norm_g, w_kv, w_out, pool_w_in, pool_w_grp, pool_scale, hgrn_w_in, hgrn_lb, hgrn_norm_g, final_g, loss_target, m_norm_g, m_mem_norm_g, m_w_kv, m_w_out, m_pool_w_in, m_pool_w_grp, m_pool_scale, m_hgrn_w_in, m_hgrn_lb, m_hgrn_norm_g, m_final_g, v_norm_g, v_mem_norm_g, v_w_kv, v_w_out, v_pool_w_in, v_pool_w_grp, v_pool_scale, v_hgrn_w_in, v_hgrn_lb, v_hgrn_norm_g, v_final_g):
    comm = MeshWeights(w_kv, w_out, pool_w_in, pool_w_grp, hgrn_w_in, hgrn_norm_g)
    loss, grad_x, small = run_step(x[0], mem[0], loss_target[0], norm_g, mem_norm_g, pool_scale, hgrn_lb, final_g, comm)
    loss = lax.psum(loss, AXES)
    rows = lambda a: a.reshape(a.shape[0], -1, a.shape[-1])
    state = dict(w_kv=(w_kv, m_w_kv, v_w_kv), w_out=(w_out, m_w_out, v_w_out), pool_w_in=(pool_w_in, m_pool_w_in, v_pool_w_in),
                 pool_w_grp=(pool_w_grp, m_pool_w_grp, v_pool_w_grp), hgrn_w_in=(hgrn_w_in, m_hgrn_w_in, v_hgrn_w_in))
    big = {}

    def update(group, layer, after):
        for nm, (chip_parts, landed) in comm.grads_done(group, after).items():
            w, m, v = state[nm]
            big[nm] = adamw_shard(f"adamw_{nm}{layer}", chip_parts, landed, rows(w), rows(m), rows(v),
                                  layer if w.shape[0] > 1 else 0, prev=big.get(nm))
            after = big[nm][0]
        return after

    after = update("0b", 0, update(1, 1, grad_x))
    names = ("norm_g", "mem_norm_g", "pool_scale", "hgrn_lb", "hgrn_norm_g", "final_g")
    like = (norm_g, mem_norm_g, pool_scale, hgrn_lb, _sds((1, N_DEV * hgrn_norm_g.shape[1]), F32), final_g)
    packed = _pack([small[nm].reshape(lk.shape) for nm, lk in zip(names, like)])
    reduced = _unpack(all_reduce_small("reduce_small", packed, after), like)
    n_loc = hgrn_norm_g.shape[1]
    reduced[4] = lax.dynamic_slice(reduced[4], (0, _slot(_position()) * n_loc), (1, n_loc))
    ws = (norm_g, mem_norm_g, pool_scale, hgrn_lb, hgrn_norm_g, final_g)
    ms = (m_norm_g, m_mem_norm_g, m_pool_scale, m_hgrn_lb, m_hgrn_norm_g, m_final_g)
    vs = (v_norm_g, v_mem_norm_g, v_pool_scale, v_hgrn_lb, v_hgrn_norm_g, v_final_g)
    res = adamw_small("adamw_small", _pack(reduced), _pack(ws), _pack(ms), _pack(vs))
    outs = {nm: [reduced[i]] + [_unpack(r, ws)[i] for r in res] for i, nm in enumerate(names)}
    update("0a", 0, res[0])
    for nm, (w, _, _) in state.items():
        outs[nm] = [r.reshape(w.shape) for r in big[nm]]
    order = ("norm_g", "mem_norm_g", "w_kv", "w_out", "pool_w_in", "pool_w_grp", "pool_scale", "hgrn_w_in", "hgrn_lb",
             "hgrn_norm_g", "final_g")
    return (loss, grad_x[None], *[outs[nm][0] for nm in order], *[outs[nm][1] for nm in order],
            *[outs[nm][2] for nm in order], *[outs[nm][3] for nm in order])
```

```python
import functools

import jax
import jax.numpy as jnp
from jax import lax
from jax.experimental import pallas as pl
from jax.experimental.pallas import tpu as pltpu

F32, BF16 = jnp.float32, jnp.bfloat16
MM_DTYPE = BF16
COMM_DTYPE = BF16
EPS = 1e-6
POOL_WINDOWS = (2, 4, 8, 16)
POOL_HALO = 16
HG_HEAD_DIM = 128
HG_CHUNK = 64
CA_HEADS = 4
N_DEV = 8
ADAM_LR, ADAM_B1, ADAM_B2, ADAM_EPS, ADAM_WD, ADAM_STEP = 0.001, 0.9, 0.999, 1e-08, 0.01, 10
VMEM_LIMIT = 48 << 20
VMEM_LIMIT_TALL = 56 << 20
LANE = 128
MESH = pl.DeviceIdType.MESH

NN = (((1,), (0,)), ((), ()))
NT = (((1,), (1,)), ((), ()))
TN = (((0,), (0,)), ((), ()))


def _cp(sem=None, vmem=VMEM_LIMIT):
    return pltpu.CompilerParams(dimension_semantics=sem, vmem_limit_bytes=vmem)


def _dot(a, b, dims):
    return lax.dot_general(a.astype(MM_DTYPE), b.astype(MM_DTYPE), dims, preferred_element_type=F32)


def _sig(v):
    return 1.0 / (1.0 + jnp.exp(-v))


def _tile(n, pref):
    if n <= pref:
        return n
    t = (pref // LANE) * LANE
    while n % t:
        t -= LANE
    return t


def _sds(shape, dtype):
    return jax.ShapeDtypeStruct(tuple(shape), dtype)


def _mm(name, grid, sem, ins, in_specs, out_shapes, out_specs, dims, epi=None, nk=1, acc_shape=None, aliases=None,
        vmem=VMEM_LIMIT):
    n_in, n_out = len(ins), len(out_shapes)

    def body(*refs):
        a_ref, b_ref = refs[0], refs[1]
        extra = refs[2:n_in]
        outs = refs[n_in:n_in + n_out]
        p = _dot(a_ref[...], b_ref[...], dims)

        def finish(acc):
            res = epi(acc, *extra) if epi is not None else (acc,)
            for r, o_ref in zip(res, outs):
                o_ref[...] = r.astype(o_ref.dtype)

        if nk == 1:
            finish(p)
        else:
            acc_ref = refs[n_in + n_out]
            k = pl.program_id(len(grid) - 1)

            @pl.when(k == 0)
            def _():
                acc_ref[...] = p

            @pl.when(k > 0)
            def _():
                acc_ref[...] += p

            @pl.when(k == nk - 1)
            def _():
                finish(acc_ref[...])

    scratch = [pltpu.VMEM(acc_shape, F32)] if nk > 1 else []
    return pl.pallas_call(
        body, name=name, grid=grid, in_specs=in_specs, out_specs=out_specs, out_shape=out_shapes,
        scratch_shapes=scratch, compiler_params=_cp(sem, vmem), input_output_aliases=aliases or {},
    )(*ins)


def proj_fwd(name, h, wblk):
    T, D = h.shape
    nblk, _, nb = wblk.shape
    tm = _tile(T, 1024)
    return _mm(name, (nblk, T // tm), ("parallel", "parallel"), [h, wblk],
               [pl.BlockSpec((tm, D), lambda j, i: (i, 0)), pl.BlockSpec((None, D, nb), lambda j, i: (j, 0, 0))],
               [_sds((T, nblk * nb), F32)], [pl.BlockSpec((tm, nb), lambda j, i: (i, j))], NN)[0]


def proj_bwd_w(name, h, dproj, nblk):
    T, D = h.shape
    N = dproj.shape[1]
    nb = N // nblk
    tm, tn = _tile(D, 1024 if nb <= 1024 else 512), _tile(nb, 2048)
    per = nb // tn
    return _mm(name, (D // tm, N // tn), ("parallel", "parallel"), [h, dproj],
               [pl.BlockSpec((T, tm), lambda i, n: (0, i)), pl.BlockSpec((T, tn), lambda i, n: (0, n))],
               [_sds((nblk, D, nb), COMM_DTYPE)], [pl.BlockSpec((None, tm, tn), lambda i, n: (n // per, i, n % per))], TN)[0]


def proj_bwd_x(name, dproj, wblk, after, tm=1024):
    T, N = dproj.shape
    nblk, D, nb = wblk.shape
    tm = _tile(T, tm)
    return _mm(name, (T // tm, nblk), ("parallel", "arbitrary"), [dproj, wblk, jnp.reshape(after, (1, 1))],
               [pl.BlockSpec((tm, nb), lambda i, k: (i, k)), pl.BlockSpec((None, D, nb), lambda i, k: (k, 0, 0)),
                pl.BlockSpec(memory_space=pl.ANY)],
               [_sds((T, D), F32)], [pl.BlockSpec((tm, D), lambda i, k: (i, 0))], NT, nk=nblk, acc_shape=(tm, D),
               vmem=VMEM_LIMIT_TALL)[0]


def out_fwd(name, branch, wout, x, after, tm=1024, tn=512):
    T, E = branch.shape
    D = wout.shape[1]
    tm, tn = _tile(T, tm), _tile(D, tn)
    return _mm(name, (T // tm, D // tn), ("parallel", "parallel"), [branch, wout, x, jnp.reshape(after, (1, 1))],
               [pl.BlockSpec((tm, E), lambda i, j: (i, 0)), pl.BlockSpec((E, tn), lambda i, j: (0, j)),
                pl.BlockSpec((tm, tn), lambda i, j: (i, j)), pl.BlockSpec(memory_space=pl.ANY)],
               [_sds((T, D), F32)], [pl.BlockSpec((tm, tn), lambda i, j: (i, j))], NN,
               epi=lambda acc, x_ref, _: (acc + x_ref[...],))[0]


def mm_nt(name, a, b, out_dtype, tm=1024, tn=2048, res=None):
    M, K = a.shape
    N = b.shape[0]
    tm, tn = _tile(M, tm), _tile(N, tn)
    ins = [a, b] + ([res] if res is not None else [])
    specs = [pl.BlockSpec((tm, K), lambda i, j: (i, 0)), pl.BlockSpec((tn, K), lambda i, j: (j, 0))]
    if res is not None:
        specs.append(pl.BlockSpec((tm, tn), lambda i, j: (i, j)))
    epi = (lambda acc, r_ref: (acc + r_ref[...],)) if res is not None else None
    return _mm(name, (M // tm, N // tn), ("parallel", "parallel"), ins, specs,
               [_sds((M, N), out_dtype)], [pl.BlockSpec((tm, tn), lambda i, j: (i, j))], NT, epi=epi)[0]


def mm_nn(name, a, b, out_dtype, tm=512, tn=512):
    M, K = a.shape
    N = b.shape[1]
    tm, tn = _tile(M, tm), _tile(N, tn)
    return _mm(name, (M // tm, N // tn), ("parallel", "parallel"), [a, b],
               [pl.BlockSpec((tm, K), lambda i, j: (i, 0)), pl.BlockSpec((K, tn), lambda i, j: (0, j))],
               [_sds((M, N), out_dtype)], [pl.BlockSpec((tm, tn), lambda i, j: (i, j))], NN)[0]


def mm_tn(name, a, b, out_dtype, tm=1024, tn=1024):
    K, M = a.shape
    N = b.shape[1]
    tm, tn = _tile(M, tm), _tile(N, tn)
    return _mm(name, (M // tm, N // tn), ("parallel", "parallel"), [a, b],
               [pl.BlockSpec((K, tm), lambda i, j: (0, i)), pl.BlockSpec((K, tn), lambda i, j: (0, j))],
               [_sds((M, N), out_dtype)], [pl.BlockSpec((tm, tn), lambda i, j: (i, j))], TN)[0]


def rmsnorm_fwd(name, x, g, tm=512):
    T, D = x.shape
    tm = _tile(T, tm)

    def body(x_ref, g_ref, o_ref):
        xf = x_ref[...]
        r = lax.rsqrt(jnp.mean(xf * xf, axis=-1, keepdims=True) + EPS)
        o_ref[...] = ((xf * r) * g_ref[...]).astype(o_ref.dtype)

    return pl.pallas_call(
        body, name=name, grid=(T // tm,),
        in_specs=[pl.BlockSpec((tm, D), lambda i: (i, 0)), pl.BlockSpec((1, D), lambda i: (0, 0))],
        out_specs=pl.BlockSpec((tm, D), lambda i: (i, 0)), out_shape=_sds((T, D), MM_DTYPE),
        compiler_params=_cp(("parallel",)),
    )(x, g.reshape(1, D))


def _rms_bwd_math(xf, g, dh):
    r = lax.rsqrt(jnp.mean(xf * xf, axis=-1, keepdims=True) + EPS)
    gd = dh * g
    dx = r * gd - xf * ((r * r * r) * jnp.mean(gd * xf, axis=-1, keepdims=True))
    dg = jnp.sum(dh * (xf * r), axis=0, keepdims=True)
    return dx, dg


def rmsnorm_bwd(name, x, g, dh, res=None, tm=512):
    T, D = x.shape
    tm = _tile(T, tm)
    has_res = res is not None

    def body(*refs):
        x_ref, g_ref, dh_ref = refs[:3]
        dx_ref, dxb_ref, dg_ref = refs[3 + has_res:]
        dx, dg = _rms_bwd_math(x_ref[...], g_ref[...], dh_ref[...])
        if has_res:
            dx = dx + refs[3][...]
        dx_ref[...] = dx
        dxb_ref[...] = dx.astype(dxb_ref.dtype)

        @pl.when(pl.program_id(0) == 0)
        def _():
            dg_ref[...] = jnp.zeros_like(dg_ref)

        dg_ref[...] += dg

    row = pl.BlockSpec((tm, D), lambda i: (i, 0))
    vec = pl.BlockSpec((1, D), lambda i: (0, 0))
    ins = [x, g.reshape(1, D), dh] + ([res] if has_res else [])
    return pl.pallas_call(
        body, name=name, grid=(T // tm,), in_specs=[row, vec, row] + ([row] if has_res else []),
        out_specs=[row, row, vec], out_shape=[_sds((T, D), F32), _sds((T, D), MM_DTYPE), _sds((1, D), F32)],
        compiler_params=_cp(("arbitrary",)),
    )(*ins)


def final_loss_bwd(name, x, g, target, tm=512):
    T, D = x.shape
    tm = _tile(T, tm)

    def body(x_ref, g_ref, t_ref, loss_ref, dx_ref, dxb_ref, dg_ref):
        xf, gv = x_ref[...], g_ref[...]
        r = lax.rsqrt(jnp.mean(xf * xf, axis=-1, keepdims=True) + EPS)
        err = (xf * r) * gv - t_ref[...]
        part = 0.5 * jnp.sum(jnp.mean(err * err, axis=-1, keepdims=True), axis=0, keepdims=True)
        dx, dg = _rms_bwd_math(xf, gv, err / D)
        dx_ref[...] = dx
        dxb_ref[...] = dx.astype(dxb_ref.dtype)

        @pl.when(pl.program_id(0) == 0)
        def _():
            dg_ref[...] = jnp.zeros_like(dg_ref)
            loss_ref[...] = jnp.zeros_like(loss_ref)

        dg_ref[...] += dg
        loss_ref[...] += jnp.broadcast_to(part, loss_ref.shape)

    row = pl.BlockSpec((tm, D), lambda i: (i, 0))
    vec = pl.BlockSpec((1, D), lambda i: (0, 0))
    return pl.pallas_call(
        body, name=name, grid=(T // tm,), in_specs=[row, vec, row],
        out_specs=[pl.BlockSpec((8, LANE), lambda i: (0, 0)), row, row, vec],
        out_shape=[_sds((8, LANE), F32), _sds((T, D), F32), _sds((T, D), MM_DTYPE), _sds((1, D), F32)],
        compiler_params=_cp(("arbitrary",)),
    )(x, g.reshape(1, D), target)


def _pool_tiles(T, emix):
    pg = emix // len(POOL_WINDOWS)
    tc = 256 if pg % 256 == 0 else LANE
    return pg, tc, _tile(T, 512)


def pool_fwd(name, proj, T, emix, after):
    pg, tc, R = _pool_tiles(T, emix)
    per_g = pg // tc

    def body(u_ref, _, o_ref):
        g = pl.program_id(0) // per_g
        for gi, w in enumerate(POOL_WINDOWS):
            @pl.when(g == gi)
            def _():
                for ci in range(T // R):
                    r0 = ci * R
                    cur = u_ref[r0:r0 + R, :]
                    halo = jnp.zeros((POOL_HALO, tc), F32) if ci == 0 else u_ref[r0 - POOL_HALO:r0, :]
                    s = jnp.concatenate([halo, cur], axis=0)
                    for st in range(w.bit_length() - 1):
                        s = s + pltpu.roll(s, 1 << st, axis=0)
                    t = r0 + lax.broadcasted_iota(jnp.int32, (R, 1), 0)
                    cnt = jnp.minimum(t + 1, w).astype(F32)
                    o_ref[r0:r0 + R, :] = (s[POOL_HALO:] / cnt - cur).astype(o_ref.dtype)

    return pl.pallas_call(
        body, name=name, grid=(emix // tc,),
        in_specs=[pl.BlockSpec((T, tc), lambda j: (0, j)), pl.BlockSpec(memory_space=pl.ANY)],
        out_specs=pl.BlockSpec((T, tc), lambda j: (0, j)), out_shape=_sds((T, emix), MM_DTYPE),
        compiler_params=_cp(("parallel",)),
    )(proj, jnp.reshape(after, (1, 1)))


def pool_bwd(name, dpooled, dproj, T, emix):
    pg, tc, R = _pool_tiles(T, emix)
    per_g = pg // tc

    def body(d_ref, _, o_ref):
        g = pl.program_id(0) // per_g
        for gi, w in enumerate(POOL_WINDOWS):
            @pl.when(g == gi)
            def _():
                n = R + POOL_HALO
                for ci in range(T // R):
                    r0 = ci * R
                    cur = d_ref[r0:r0 + R, :]
                    halo = jnp.zeros((POOL_HALO, tc), F32) if ci == T // R - 1 else d_ref[r0 + R:r0 + n, :]
                    t = r0 + lax.broadcasted_iota(jnp.int32, (n, 1), 0)
                    cnt = jnp.minimum(t + 1, w).astype(F32)
                    s = jnp.concatenate([cur, halo], axis=0) / cnt
                    for st in range(w.bit_length() - 1):
                        s = s + pltpu.roll(s, n - (1 << st), axis=0)
                    o_ref[r0:r0 + R, :] = (s[:R] - cur).astype(o_ref.dtype)

    return pl.pallas_call(
        body, name=name, grid=(emix // tc,),
        in_specs=[pl.BlockSpec((T, tc), lambda j: (0, j)), pl.BlockSpec(memory_space=pl.ANY)],
        out_specs=pl.BlockSpec((T, tc), lambda j: (0, j)), out_shape=_sds(dproj.shape, dproj.dtype),
        input_output_aliases={1: 0}, compiler_params=_cp(("parallel",)),
    )(dpooled, dproj)


def grp_fwd(name, pooled, wgrp, scale, proj, gate_off, E):
    T, emix = pooled.shape
    pg = emix // len(POOL_WINDOWS)
    tm = _tile(T, 2048)
    tc = 256 if pg % 256 == 0 and gate_off % 256 == 0 else LANE
    per = pg // tc

    def epi(acc, s_ref, g_ref):
        gt = g_ref[...]
        return acc, (acc * s_ref[...]) * (gt * _sig(gt))

    col = pl.BlockSpec((tm, tc), lambda n, i: (i, n))
    return _mm(name, (emix // tc, T // tm), ("parallel", "parallel"), [pooled, wgrp, scale, proj],
               [pl.BlockSpec((tm, pg), lambda n, i: (i, n // per)), pl.BlockSpec((None, pg, tc), lambda n, i: (n // per, 0, n % per)),
                pl.BlockSpec((1, tc), lambda n, i: (0, n)), pl.BlockSpec((tm, tc), lambda n, i: (i, gate_off // tc + n))],
               [_sds((T, emix), F32), _sds((T, E), MM_DTYPE)], [col, col], NN, epi=epi)


def grp_bwd_x(name, dy, wgrp):
    T, emix = dy.shape
    pg = emix // len(POOL_WINDOWS)
    tm = _tile(T, 1024)
    return _mm(name, (len(POOL_WINDOWS), T // tm), ("parallel", "parallel"), [dy, wgrp],
               [pl.BlockSpec((tm, pg), lambda g, i: (i, g)), pl.BlockSpec((None, pg, pg), lambda g, i: (g, 0, 0))],
               [_sds((T, emix), F32)], [pl.BlockSpec((tm, pg), lambda g, i: (i, g))], NT)[0]


def grp_bwd_w(name, pooled, dy):
    T, emix = dy.shape
    ng = len(POOL_WINDOWS)
    pg = emix // ng
    col = pl.BlockSpec((T, pg), lambda g: (0, g))
    return _mm(name, (ng,), ("parallel",), [pooled, dy], [col, col],
               [_sds((ng, pg, pg), COMM_DTYPE)], [pl.BlockSpec((None, pg, pg), lambda g: (g, 0, 0))], TN)[0]


def gate_bwd_pool(name, dbranch, y, scale, proj, gate_off, n_proj):
    T, emix = y.shape
    tm, tc = _tile(T, 1024), _tile(emix, 512)
    assert gate_off % tc == 0

    def body(db_ref, y_ref, s_ref, g_ref, dy_ref, dg_ref, ds_ref):
        db, yv, sc, gt = db_ref[...], y_ref[...], s_ref[...], g_ref[...]
        sg = _sig(gt)
        dmix = db * (gt * sg)
        dy_ref[...] = (dmix * sc).astype(dy_ref.dtype)
        dg_ref[...] = (db * (yv * sc) * (sg * (1.0 + gt * (1.0 - sg)))).astype(dg_ref.dtype)

        @pl.when(pl.program_id(1) == 0)
        def _():
            ds_ref[...] = jnp.zeros_like(ds_ref)

        ds_ref[...] += jnp.sum(dmix * yv, axis=0, keepdims=True)

    blk = pl.BlockSpec((tm, tc), lambda j, i: (i, j))
    vec = pl.BlockSpec((1, tc), lambda j, i: (0, j))
    gate = pl.BlockSpec((tm, tc), lambda j, i: (i, gate_off // tc + j))
    return pl.pallas_call(
        body, name=name, grid=(emix // tc, T // tm), in_specs=[blk, blk, vec, gate],
        out_specs=[blk, gate, vec],
        out_shape=[_sds((T, emix), MM_DTYPE), _sds((T, n_proj), MM_DTYPE), _sds((1, emix), F32)],
        compiler_params=_cp(("parallel", "arbitrary")),
    )(dbranch, y, scale, proj)


def attn_fwd(name, proj, kv, branch, q_off, gate_off, tm=2048):
    T = proj.shape[0]
    M, two_eca = kv.shape
    eca = two_eca // 2
    hd = eca // CA_HEADS
    E = branch.shape[1]
    tm = _tile(T, tm)
    scale = hd ** -0.5

    def body(q_ref, g_ref, k_ref, v_ref, _, ca_ref, br_ref):
        s = _dot(q_ref[...], k_ref[...], NT) * scale
        s = s - jnp.max(s, axis=-1, keepdims=True)
        e = jnp.exp(s)
        p = e / jnp.sum(e, axis=-1, keepdims=True)
        o = _dot(p, v_ref[...], NN)
        gt = g_ref[...]
        ca_ref[...] = o
        br_ref[...] = (o * (gt * _sig(gt))).astype(br_ref.dtype)

    return pl.pallas_call(
        body, name=name, grid=(CA_HEADS, T // tm),
        in_specs=[pl.BlockSpec((tm, hd), lambda h, i: (i, q_off // hd + h)),
                  pl.BlockSpec((tm, hd), lambda h, i: (i, (gate_off + E - eca) // hd + h)),
                  pl.BlockSpec((M, hd), lambda h, i: (0, h)), pl.BlockSpec((M, hd), lambda h, i: (0, CA_HEADS + h)),
                  pl.BlockSpec(memory_space=pl.ANY)],
        out_specs=[pl.BlockSpec((tm, hd), lambda h, i: (i, h)), pl.BlockSpec((tm, hd), lambda h, i: (i, (E - eca) // hd + h))],
        out_shape=[_sds((T, eca), F32), _sds(branch.shape, branch.dtype)],
        input_output_aliases={4: 1}, compiler_params=_cp(("parallel", "parallel")),
    )(proj, proj, kv, kv, branch)


def attn_bwd(name, proj, kv, ca, dbranch, dproj, q_off, gate_off, tm=2048):
    T = proj.shape[0]
    M, two_eca = kv.shape
    eca = two_eca // 2
    hd = eca // CA_HEADS
    E = dbranch.shape[1]
    tm = _tile(T, tm)
    scale = hd ** -0.5
    q_blk, g_blk = q_off // hd, (gate_off + E - eca) // hd

    n_rows = T // tm
    n_steps = CA_HEADS * n_rows

    def body(q_ref, g_ref, k_ref, v_ref, ca_ref, db_ref, _, dp_hbm, dk_ref, dv_ref, stash, sems):
        h, i = pl.program_id(0), pl.program_id(1)
        step = h * n_rows + i
        slot = step % 2

        def writes(slot, h, i):
            rows = pl.ds(pl.multiple_of(i * tm, tm), tm)
            return [pltpu.make_async_copy(stash.at[slot, r], dp_hbm.at[rows, pl.ds(pl.multiple_of((blk + h) * hd, hd), hd)],
                                          sems.at[slot, r]) for r, blk in enumerate((q_blk, g_blk))]

        @pl.when(step >= 2)
        def _():
            for cp in writes(slot, h, i):
                cp.wait()

        q, k, v = q_ref[...], k_ref[...], v_ref[...]
        s = _dot(q, k, NT) * scale
        s = s - jnp.max(s, axis=-1, keepdims=True)
        e = jnp.exp(s)
        p = e / jnp.sum(e, axis=-1, keepdims=True)
        gt, db = g_ref[...], db_ref[...]
        sg = _sig(gt)
        do = db * (gt * sg)
        stash[slot, 1] = (db * ca_ref[...] * (sg * (1.0 + gt * (1.0 - sg)))).astype(stash.dtype)
        dpr = _dot(do, v, NT)
        ds = p * (dpr - jnp.sum(dpr * p, axis=-1, keepdims=True)) * scale
        stash[slot, 0] = _dot(ds, k, NN).astype(stash.dtype)

        @pl.when(i == 0)
        def _():
            dk_ref[...] = jnp.zeros_like(dk_ref)
            dv_ref[...] = jnp.zeros_like(dv_ref)

        dk_ref[...] += _dot(ds, q, TN)
        dv_ref[...] += _dot(p, do, TN)
        for cp in writes(slot, h, i):
            cp.start()

        @pl.when(step == n_steps - 1)
        def _():
            for cp in writes(slot, h, i):
                cp.wait()
            if n_steps > 1:
                for cp in writes(1 - slot, h, i):
                    cp.wait()

    row = lambda blk: pl.BlockSpec((tm, hd), lambda h, i: (i, blk + h))
    outs = pl.pallas_call(
        body, name=name, grid=(CA_HEADS, n_rows),
        in_specs=[row(q_blk), row(g_blk), pl.BlockSpec((M, hd), lambda h, i: (0, h)),
                  pl.BlockSpec((M, hd), lambda h, i: (0, CA_HEADS + h)), row(0), row((E - eca) // hd),
                  pl.BlockSpec(memory_space=pl.ANY)],
        out_specs=[pl.BlockSpec(memory_space=pl.ANY), pl.BlockSpec((M, hd), lambda h, i: (0, h)),
                   pl.BlockSpec((M, hd), lambda h, i: (0, h))],
        out_shape=[_sds(dproj.shape, dproj.dtype), _sds((M, eca), F32), _sds((M, eca), F32)],
        scratch_shapes=[pltpu.VMEM((2, 2, tm, hd), dproj.dtype), pltpu.SemaphoreType.DMA((2, 2))],
        input_output_aliases={6: 0}, compiler_params=_cp(("arbitrary", "arbitrary")),
    )(proj, proj, kv, kv, ca, dbranch, dproj)
    return outs[0], jnp.concatenate([outs[1], outs[2]], axis=1)


def _split3(v):
    hi = v.astype(BF16)
    r1 = v - hi.astype(F32)
    mid = r1.astype(BF16)
    lo = (r1 - mid.astype(F32)).astype(BF16)
    return hi, mid, lo


def _tri_dot(tri, v):
    hi, mid, lo = _split3(v)
    d = functools.partial(lax.dot_general, dimension_numbers=NN, preferred_element_type=F32)
    return d(tri, hi) + d(tri, mid) + d(tri, lo)


def _causal_mask():
    C = HG_CHUNK
    return lax.broadcasted_iota(jnp.int32, (C, C), 0) >= lax.broadcasted_iota(jnp.int32, (C, C), 1)


def _cumsum_matrices(G):
    ri = lax.broadcasted_iota(jnp.int32, (G, G), 0)
    ci = lax.broadcasted_iota(jnp.int32, (G, G), 1)
    same = (ri // HG_CHUNK) == (ci // HG_CHUNK)
    return (jnp.where(same & (ri >= ci), 1.0, 0.0).astype(BF16), jnp.where(same & (ri <= ci), 1.0, 0.0).astype(BF16))


def _forget_gate(fi, lb):
    sig = _sig(fi)
    return sig, lb + (1.0 - lb) * sig


def _log_decay_cumsum(f_ref, b_s, lb, prefix, G):
    def group(i, carry):
        rows = pl.ds(pl.multiple_of(i * G, G), G)
        _, f = _forget_gate(f_ref[rows, :], lb)
        b_s[rows, :] = _tri_dot(prefix, jnp.log(f))
        return carry

    lax.fori_loop(0, b_s.shape[0] // G, group, 0)


def _lower_bound(lb2):
    l0, l1 = lb2[0:1, :], lb2[1:2, :]
    m = jnp.maximum(l0, l1)
    e0, e1 = jnp.exp(l0 - m), jnp.exp(l1 - m)
    sm0, sm1 = e0 / (e0 + e1), e1 / (e0 + e1)
    return (sm0 + sm1) - sm0, sm0, sm1


def _hg_chunk_fwd(qi, fi, lb, causal, b=None, prefix=None):
    sig, f = _forget_gate(fi, lb)
    k = 1.0 - f
    if b is None:
        b = _tri_dot(prefix, jnp.log(f))
    last = lax.broadcasted_iota(jnp.int32, b.shape, 0) == HG_CHUNK - 1
    bl = jnp.sum(jnp.where(last, b, 0.0), axis=0, keepdims=True)
    eb, enb, ete = jnp.exp(b), jnp.exp(-b), jnp.exp(bl - b)
    sq = _sig(qi)
    q = (qi * sq) * (HG_HEAD_DIM ** -0.5)
    q_dec, k_inv, k_te = q * eb, k * enb, k * ete
    att = jnp.where(causal, _dot(q_dec, k_inv, NT), 0.0) if causal is not None else None
    return dict(sig=sig, f=f, k=k, eb=eb, enb=enb, ete=ete, dcy=jnp.exp(bl), sq=sq, q_dec=q_dec, k_inv=k_inv, k_te=k_te,
                att=att, last=last)


def _hg_blocking(H, T):
    hb = 4 if H % 4 == 0 else (2 if H % 2 == 0 else 1)
    return hb, _tile(T, 1024)


def hgrn_fwd(name, proj, lb2, norm_g, emix, gate_off, E):
    T = proj.shape[0]
    hd, C = HG_HEAD_DIM, HG_CHUNK
    H, N = emix // hd, T // C
    hb, ts = _hg_blocking(H, T)
    W, nc, G = hb * hd, ts // C, _tile(ts, 256)

    def body(q_ref, f_ref, i_ref, g_ref, lb_ref, ng_ref, o_ref, st_ref, br_ref, state, b_s):
        @pl.when(pl.program_id(1) == 0)
        def _():
            state[...] = jnp.zeros_like(state)

        causal2 = (lax.broadcasted_iota(jnp.int32, (C, 2 * C), 0)
                   >= lax.broadcasted_iota(jnp.int32, (C, 2 * C), 1) % C)
        lb, _, _ = _lower_bound(lb_ref[...])
        ng = ng_ref[...]
        _log_decay_cumsum(f_ref, b_s, lb, _cumsum_matrices(G)[0], G)
        first = lax.broadcasted_iota(jnp.int32, (C, 2 * hd), 1) < hd
        same_head = (lax.broadcasted_iota(jnp.int32, (2 * hd, 2 * hd), 0) // hd
                     == lax.broadcasted_iota(jnp.int32, (2 * hd, 2 * hd), 1) // hd)

        def stacked(a):
            return jnp.concatenate([jnp.where(first, a, 0.0), jnp.where(first, 0.0, a)], axis=0)

        def chunk(n, carry):
            rows = pl.ds(pl.multiple_of(n * C, C), C)
            for p in range(hb // 2):
                cols = slice(2 * p * hd, 2 * (p + 1) * hd)
                v = i_ref[rows, cols]
                c = _hg_chunk_fwd(q_ref[rows, cols], f_ref[rows, cols], lb[:, cols], None, b=b_s[rows, cols])
                att = jnp.where(causal2, _dot(c["q_dec"], stacked(c["k_inv"]), NT), 0.0)
                st = state[p]
                o = _dot(att, stacked(v), NN) + _dot(c["q_dec"], st, NT)
                for j in range(2):
                    st_ref[2 * p + j, n] = st[j * hd:(j + 1) * hd, j * hd:(j + 1) * hd]
                o_ref[rows, cols] = o
                ms = jnp.where(first, jnp.mean(jnp.where(first, o * o, 0.0), axis=-1, keepdims=True),
                               jnp.mean(jnp.where(first, 0.0, o * o), axis=-1, keepdims=True)) * 2.0
                r = lax.rsqrt(ms + EPS)
                gt = g_ref[rows, cols]
                br_ref[rows, cols] = (((o * r) * ng[:, cols]) * (gt * _sig(gt))).astype(br_ref.dtype)
                state[p] = st * c["dcy"] + jnp.where(same_head, _dot(v, c["k_te"], TN), 0.0)
            return carry

        lax.fori_loop(0, nc, chunk, 0, unroll=8)

    col = lambda off: pl.BlockSpec((ts, W), lambda h, s: (s, off // W + h))
    vec = lambda r: pl.BlockSpec((r, W), lambda h, s: (0, h))
    return pl.pallas_call(
        body, name=name, grid=(H // hb, T // ts),
        in_specs=[col(0), col(emix), col(2 * emix), col(gate_off), vec(2), vec(1)],
        out_specs=[col(0), pl.BlockSpec((hb, nc, hd, hd), lambda h, s: (h, s, 0, 0)), col(0)],
        out_shape=[_sds((T, emix), F32), _sds((H, N, hd, hd), F32), _sds((T, E), MM_DTYPE)],
        scratch_shapes=[pltpu.VMEM((hb // 2, 2 * hd, 2 * hd), F32), pltpu.VMEM((ts, W), F32)],
        compiler_params=_cp(("parallel", "arbitrary")),
    )(proj, proj, proj, proj, lb2, norm_g)


def hgrn_bwd(name, proj, lb2, norm_g, o_pre, states, dbranch, emix, gate_off):
    T, n_proj = proj.shape
    hd, C = HG_HEAD_DIM, HG_CHUNK
    H, N = emix // hd, T // C
    hb, ts = _hg_blocking(H, T)
    W, nc, ns = hb * hd, ts // C, T // ts
    qscale = HG_HEAD_DIM ** -0.5

    n_steps = (H // hb) * ns
    regions = (0, emix, 2 * emix, gate_off)

    def body(q_ref, f_ref, i_ref, g_ref, lb_ref, ng_ref, o_ref, st_ref, db_ref, dp_hbm, dlb_ref, dng_ref,
             stash, dst_ref, acc, sems):
        hblk, s = pl.program_id(0), pl.program_id(1)
        step = hblk * ns + s
        slot = step % 2

        def writes(slot, hblk, s):
            rows = pl.ds(pl.multiple_of((ns - 1 - s) * ts, ts), ts)
            return [pltpu.make_async_copy(stash.at[slot, r], dp_hbm.at[rows, pl.ds(pl.multiple_of(off + hblk * W, W), W)],
                                          sems.at[slot, r]) for r, off in enumerate(regions)]

        @pl.when(step >= 2)
        def _():
            for cp in writes(slot, hblk, s):
                cp.wait()

        @pl.when(s == 0)
        def _():
            dst_ref[...] = jnp.zeros_like(dst_ref)
            acc[...] = jnp.zeros_like(acc)

        causal = _causal_mask()
        prefix, suffix = _cumsum_matrices(C)
        lb, sm0, sm1 = _lower_bound(lb_ref[...])
        ng = ng_ref[...]

        def chunk(j, carry):
            n = nc - 1 - j
            rows = pl.ds(pl.multiple_of(n * C, C), C)
            for h in range(hb):
                cols = slice(h * hd, (h + 1) * hd)
                dst = dst_ref[h]
                qi, v, gt = q_ref[rows, cols], i_ref[rows, cols], g_ref[rows, cols]
                c = _hg_chunk_fwd(qi, f_ref[rows, cols], lb[:, cols], causal, prefix=prefix)
                st = st_ref[h, n]
                o, db, ngh = o_ref[rows, cols], db_ref[rows, cols], ng[:, cols]
                r = lax.rsqrt(jnp.mean(o * o, axis=-1, keepdims=True) + EPS)
                sg = _sig(gt)
                dmix = db * (gt * sg)
                dgate = db * ((o * r) * ngh) * (sg * (1.0 + gt * (1.0 - sg)))
                dyn = dmix * ngh
                do = r * dyn - o * ((r * r * r) * jnp.mean(dyn * o, axis=-1, keepdims=True))
                acc[1:2, cols] += jnp.sum(dmix * (o * r), axis=0, keepdims=True)
                datt = jnp.where(causal, _dot(do, v, NT), 0.0)
                dq_dec = _dot(datt, c["k_inv"], NN) + _dot(do, st, NN)
                dk_inv = _dot(datt, c["q_dec"], TN)
                dv = _dot(c["att"], do, TN) + _dot(c["k_te"], dst, NT)
                dk_te = _dot(v, dst, NN)
                ddcy = jnp.sum(dst * st, axis=0, keepdims=True)
                dst_ref[h] = _dot(do, c["q_dec"], TN) + dst * c["dcy"]
                t_te = dk_te * c["k_te"]
                dbv = dq_dec * c["q_dec"] - dk_inv * c["k_inv"] - t_te
                dbl = jnp.sum(t_te, axis=0, keepdims=True) + ddcy * c["dcy"]
                dbv = dbv + jnp.where(c["last"], dbl, 0.0)
                dlf = _tri_dot(suffix, dbv)
                dk = dk_inv * c["enb"] + dk_te * c["ete"]
                df = dlf / c["f"] - dk
                sig, sq = c["sig"], c["sq"]
                acc[0:1, cols] += jnp.sum(df * (1.0 - sig), axis=0, keepdims=True)
                dqi = (dq_dec * c["eb"]) * qscale * (sq * (1.0 + qi * (1.0 - sq)))
                stash[slot, 0, rows, cols] = dqi.astype(stash.dtype)
                stash[slot, 1, rows, cols] = (df * (1.0 - lb[:, cols]) * (sig * (1.0 - sig))).astype(stash.dtype)
                stash[slot, 2, rows, cols] = dv.astype(stash.dtype)
                stash[slot, 3, rows, cols] = dgate.astype(stash.dtype)
            return carry

        lax.fori_loop(0, nc, chunk, 0, unroll=True)
        t = acc[0:1, :] * (sm0 * sm1)
        dlb_ref[...] = jnp.concatenate([-t, t], axis=0)
        dng_ref[...] = acc[1:2, :]
        for cp in writes(slot, hblk, s):
            cp.start()

        @pl.when(step == n_steps - 1)
        def _():
            for cp in writes(slot, hblk, s):
                cp.wait()
            if n_steps > 1:
                for cp in writes(1 - slot, hblk, s):
                    cp.wait()

    col = lambda off: pl.BlockSpec((ts, W), lambda h, s: (ns - 1 - s, off // W + h))
    vec = lambda r: pl.BlockSpec((r, W), lambda h, s: (0, h))
    return pl.pallas_call(
        body, name=name, grid=(H // hb, ns),
        in_specs=[col(0), col(emix), col(2 * emix), col(gate_off), vec(2), vec(1), col(0),
                  pl.BlockSpec((hb, nc, hd, hd), lambda h, s: (h, ns - 1 - s, 0, 0)), col(0)],
        out_specs=[pl.BlockSpec(memory_space=pl.ANY), vec(2), vec(1)],
        out_shape=[_sds((T, n_proj), MM_DTYPE), _sds((2, emix), F32), _sds((1, emix), F32)],
        scratch_shapes=[pltpu.VMEM((2, 4, ts, W), MM_DTYPE), pltpu.VMEM((hb, hd, hd), F32), pltpu.VMEM((8, W), F32),
                        pltpu.SemaphoreType.DMA((2, 4))],
        compiler_params=_cp(("arbitrary", "arbitrary")),
    )(proj, proj, proj, proj, lb2, norm_g, o_pre, states, dbranch)


class WholeWeights:
    def __init__(self, wkv, wout, wpin, wgrp, whin, hgrn_norm_g):
        self.w = dict(wkv=wkv, wout=wout, wpin=wpin, wgrp=wgrp, whin=whin, hnorm=hgrn_norm_g)
        self.grads = {}

    def first_weights(self):
        return self.w["wpin"], self.w["hnorm"], 0.0

    def after_proj0(self, after):
        return 0.0

    def layer0_weights(self, after):
        return self.w["wgrp"], self.w["wkv"][0], self.w["wout"][0]

    def after_out0(self, after):
        return 0.0

    def layer1_weights(self, after):
        return self.w["whin"]

    def after_proj1(self, after):
        return 0.0

    def layer1_rest(self, after):
        return self.w["wkv"][1], self.w["wout"][1]

    def grads_ready(self, layer, grads):
        self.grads[layer] = grads
        return 0.0

    def grads_reduce(self, layer, after):
        return 0.0


def run_step(x, mem, target, norm_g, mem_norm_g, pool_scale, hgrn_lb, final_g, comm):
    T, D = x.shape
    mem_n = rmsnorm_fwd("mem_norm", mem, mem_norm_g)
    wpin, hgrn_norm_g, zero = comm.first_weights()
    h0 = rmsnorm_fwd("norm0", x, norm_g[0] + zero)
    proj0 = proj_fwd("proj0", h0, wpin)
    zero = comm.after_proj0(proj0)
    n0 = proj0.shape[1]
    E = n0 // 2
    eca = E // 4
    emix = E - eca
    q0, g0 = emix, emix + eca
    q1, g1 = 3 * emix, 3 * emix + eca
    pooled = pool_fwd("pool_fwd", proj0, T, emix, zero)
    wgrp, wkv0, wout0 = comm.layer0_weights(pooled)
    y0, branch0 = grp_fwd("grp_fwd", pooled, wgrp, pool_scale, proj0, g0, E)
    kv0 = mm_nn("kv0", mem_n, wkv0, MM_DTYPE)
    ca0, branch0 = attn_fwd("attn_fwd0", proj0, kv0, branch0, q0, g0)
    x1 = out_fwd("out0", branch0, wout0, x, ca0[:1, :1])
    h1 = rmsnorm_fwd("norm1", x1, norm_g[1] + comm.after_out0(x1))
    whin = comm.layer1_weights(h1)
    proj1 = proj_fwd("proj1", h1, whin)
    zero = comm.after_proj1(proj1)
    o1, states, branch1 = hgrn_fwd("hgrn_fwd", proj1, hgrn_lb + zero, hgrn_norm_g, emix, g1, E)
    wkv1, wout1 = comm.layer1_rest(o1)
    kv1 = mm_nn("kv1", mem_n, wkv1, MM_DTYPE)
    ca1, branch1 = attn_fwd("attn_fwd1", proj1, kv1, branch1, q1, g1)
    x2 = out_fwd("out1", branch1, wout1, x1, ca1[:1, :1])
    loss, dx2, dx2b, g_final = final_loss_bwd("final", x2, final_g, target)
    g_wout1 = mm_tn("gwout1", branch1, dx2b, COMM_DTYPE)
    dbranch1 = mm_nt("dbranch1", dx2b, wout1, F32)
    dproj1, g_lb, g_hnorm = hgrn_bwd("hgrn_bwd", proj1, hgrn_lb, hgrn_norm_g, o1, states, dbranch1, emix, g1)
    dproj1, dkv1 = attn_bwd("attn_bwd1", proj1, kv1, ca1, dbranch1, dproj1, q1, g1)
    g_wkv1 = mm_tn("gwkv1", mem_n, dkv1, COMM_DTYPE)
    dmem_n = mm_nt("dmem1", dkv1, wkv1, F32)
    g_whin = proj_bwd_w("gwhin", h1, dproj1, whin.shape[0])
    zero = comm.grads_ready(1, dict(w_kv=g_wkv1, w_out=g_wout1, hgrn_w_in=g_whin))
    dh1 = proj_bwd_x("dh1", dproj1, whin, zero)
    zero = zero + comm.grads_reduce(1, dh1)
    dx1, dx1b, g_norm1 = rmsnorm_bwd("norm1_bwd", x1, norm_g[1] + zero, dh1, res=dx2)
    dbranch0 = mm_nt("dbranch0", dx1b, wout0, F32)
    dy0, dproj0, g_scale = gate_bwd_pool("gate_bwd0", dbranch0, y0, pool_scale, proj0, g0, n0)
    dpooled = grp_bwd_x("dpooled", dy0, wgrp)
    dproj0 = pool_bwd("pool_bwd", dpooled, dproj0, T, emix)
    dproj0, dkv0 = attn_bwd("attn_bwd0", proj0, kv0, ca0, dbranch0, dproj0, q0, g0)
    g_wpin = proj_bwd_w("gwpin", h0, dproj0, wpin.shape[0])
    dkv0 = dkv0 + comm.grads_ready("0b", dict(pool_w_in=g_wpin))
    g_wkv0 = mm_tn("gwkv0", mem_n, dkv0, COMM_DTYPE)
    dmem_n = mm_nt("dmem0", dkv0, wkv0, F32, res=dmem_n)
    g_wout0 = mm_tn("gwout0", branch0, dx1b, COMM_DTYPE)
    g_wgrp = grp_bwd_w("gwgrp", pooled, dy0)
    zero = comm.grads_reduce("0b", g_wgrp)
    zero = zero + comm.grads_ready("0a", dict(w_kv=g_wkv0, w_out=g_wout0, pool_w_grp=g_wgrp))
    dh0 = proj_bwd_x("dh0", dproj0, wpin, zero)
    zero = zero + comm.grads_reduce("0a", dh0)
    grad_x, _, g_norm0 = rmsnorm_bwd("norm0_bwd", x, norm_g[0] + zero, dh0, res=dx1)
    _, _, g_mem = rmsnorm_bwd("mem_norm_bwd", mem, mem_norm_g, dmem_n)

    small = dict(norm_g=jnp.concatenate([g_norm0, g_norm1], axis=0), mem_norm_g=g_mem[0], pool_scale=g_scale,
                 hgrn_lb=g_lb, hgrn_norm_g=g_hnorm, final_g=g_final[0])
    return loss[0, 0], grad_x, small


def _position():
    return lax.axis_index("x"), lax.axis_index("y"), lax.axis_index("c")


def _slot(p):
    return 4 * p[0] + 2 * p[1] + p[2]


def all_gather_blocks(name, blocks):
    n = len(blocks)
    halved = [b.shape[0] % 32 == 0 for b in blocks]

    def body(*refs):
        ins, outs, token = refs[:n], refs[n:2 * n], refs[2 * n]
        send_sems, recv_sems, local_sems = refs[2 * n + 1:]
        token[...] = jnp.zeros_like(token)
        x, y, c = _position()
        me, sibling = (x, y, c), (x, y, 1 - c)
        x_nbr, y_nbr, diag = _other_chips(x, y)

        def copy(t, k, block, to, src=None, rows=None):
            dst = outs[t].at[_slot(block)]
            if rows is not None:
                dst = dst.at[rows]
            return pltpu.make_async_remote_copy(src_ref=dst if src is None else src, dst_ref=dst, send_sem=send_sems.at[t, k],
                                                recv_sem=recv_sems.at[t, k], device_id=to, device_id_type=MESH)

        mine = [pltpu.make_async_copy(ins[t], outs[t].at[_slot(me)], local_sems.at[t]) for t in range(n)]
        for cp in mine:
            cp.start()
        sends = []
        for t in range(n):
            sends += [copy(t, 0, me, sibling, src=ins[t]), copy(t, 1, me, (*x_nbr, c), src=ins[t]),
                      copy(t, 2, me, (*y_nbr, c), src=ins[t])]
            if not halved[t]:
                sends.append(copy(t, 4, me, (*diag, c), src=ins[t]))
        for cp in sends:
            cp.start()

        def start(cp):
            cp.start()
            sends.append(cp)

        for t in range(n):
            half = blocks[t].shape[0] // 2
            top, bottom = pl.ds(0, half), pl.ds(half, half)
            copy(t, 1, (*x_nbr, c), me).wait_recv()
            if halved[t]:
                start(copy(t, 4, (*x_nbr, c), (*y_nbr, c), rows=top))
            start(copy(t, 3, (*x_nbr, c), sibling))
            copy(t, 2, (*y_nbr, c), me).wait_recv()
            if halved[t]:
                start(copy(t, 6, (*y_nbr, c), (*x_nbr, c), rows=bottom))
            start(copy(t, 5, (*y_nbr, c), sibling))
            if halved[t]:
                copy(t, 4, (*diag, c), me, rows=top).wait_recv()
                copy(t, 6, (*diag, c), me, rows=bottom).wait_recv()
            else:
                copy(t, 4, (*diag, c), me).wait_recv()
            start(copy(t, 7, (*diag, c), sibling))
        for t in range(n):
            copy(t, 0, sibling, me).wait_recv()
            for k, chip in ((3, x_nbr), (5, y_nbr), (7, diag)):
                copy(t, k, (*chip, 1 - c), me).wait_recv()
        for cp in sends:
            cp.wait_send()
        for cp in mine:
            cp.wait()

    any_spec = pl.BlockSpec(memory_space=pl.ANY)
    return pl.pallas_call(
        body, name=name, in_specs=[any_spec] * n, out_specs=[any_spec] * n + [pl.BlockSpec(memory_space=pltpu.VMEM)],
        out_shape=[_sds((N_DEV,) + b.shape, b.dtype) for b in blocks] + [_sds((8, LANE), F32)],
        scratch_shapes=[pltpu.SemaphoreType.DMA((n, 8)), pltpu.SemaphoreType.DMA((n, 8)), pltpu.SemaphoreType.DMA((n,))],
        compiler_params=pltpu.CompilerParams(has_side_effects=True),
    )(*blocks)


_HBM = pl.BlockSpec(memory_space=pltpu.HBM)
_SEM = pl.BlockSpec(memory_space=pltpu.SEMAPHORE)
_EFFECT = pltpu.SideEffectType.DATAFLOW_SIDE_EFFECTING


def split_start(name, bufs, n_copies, build):
    n = len(bufs)

    def body(*refs):
        for cp in build(refs[:n], refs[n], refs[n + 1]):
            cp.start()
        refs[-1][...] = jnp.zeros_like(refs[-1])

    outs = pl.pallas_call(
        body, name=name, in_specs=[_HBM] * n,
        out_shape=(pltpu.SemaphoreType.DMA((n_copies,)), pltpu.SemaphoreType.DMA((n_copies,)),
                   *[pltpu.HBM(b.shape, b.dtype) for b in bufs], _sds((8, LANE), F32)),
        out_specs=(_SEM, _SEM, *[_HBM] * n, pl.BlockSpec(memory_space=pltpu.VMEM)),
        input_output_aliases={i: 2 + i for i in range(n)},
        compiler_params=pltpu.CompilerParams(has_side_effects=_EFFECT),
    )(*[pltpu.with_memory_space_constraint(b, pltpu.HBM) for b in bufs])
    return (outs[0], outs[1]), list(outs[2:2 + n]), outs[-1]


def split_wait(name, sems, bufs, build, after):
    n = len(bufs)

    def body(*refs):
        for cp in build(refs[:n], refs[n], refs[n + 1]):
            cp.wait()

    return list(pl.pallas_call(
        body, name=name, in_specs=[_HBM] * n + [_SEM, _SEM, pl.BlockSpec(memory_space=pl.ANY)],
        out_shape=[pltpu.HBM(b.shape, b.dtype) for b in bufs], out_specs=[_HBM] * n,
        input_output_aliases={i: i for i in range(n)},
        compiler_params=pltpu.CompilerParams(has_side_effects=_EFFECT),
    )(*bufs, sems[0], sems[1], after))


def _remote(src, dst, send_sems, recv_sems, k, to):
    return pltpu.make_async_remote_copy(src_ref=src, dst_ref=dst, send_sem=send_sems.at[k], recv_sem=recv_sems.at[k],
                                        device_id=to, device_id_type=MESH)


def _other_chips(x, y):
    return [(1 - x, y), (x, 1 - y), (1 - x, 1 - y)]


def _gather_cross(n):
    def build(refs, ss, rs):
        x, y, c = _position()
        me = _slot((x, y, c))
        targets = [(x, y, 1 - c)] + [(*chip, c) for chip in _other_chips(x, y)]
        return [_remote(refs[t], refs[n + t].at[me], ss, rs, 4 * t + k, to) for t in range(n) for k, to in enumerate(targets)]
    return build


def _gather_pass(n):
    def build(refs, ss, rs):
        x, y, c = _position()
        cps = []
        for t in range(n):
            for j, chip in enumerate(_other_chips(x, y)):
                blk = refs[t].at[_slot((*chip, c))]
                cps.append(_remote(blk, blk, ss, rs, 3 * t + j, (x, y, 1 - c)))
        return cps
    return build


def _reduce_pair(n):
    def build(refs, ss, rs):
        x, y, c = _position()
        return [_remote(refs[t].at[j, 1 - c], refs[n + t].at[j], ss, rs, 4 * t + j, (x, y, 1 - c))
                for t in range(n) for j in range(4)]
    return build


def _reduce_cross(n):
    def build(refs, ss, rs):
        x, y, c = _position()
        return [_remote(refs[t].at[2 * chip[0] + chip[1]], refs[n + t].at[r], ss, rs, 3 * t + r, (*chip, c))
                for t in range(n) for r, chip in enumerate(_other_chips(x, y))]
    return build


def pair_add(name, part, landed, tr=1024):
    _, _, R, C = part.shape
    tr = _row_tile(R, tr)

    def body(_, p_ref, l_ref, o_ref):
        o_ref[...] = (p_ref[...].astype(F32) + l_ref[...].astype(F32)).astype(o_ref.dtype)

    blk = pl.BlockSpec((None, tr, C), lambda j, i, core: (j, i, 0))
    core = lax.axis_index("c").astype(jnp.int32).reshape(1)
    return pl.pallas_call(
        body, name=name, out_shape=_sds((4, R, C), part.dtype), compiler_params=_cp(("parallel", "parallel")),
        grid_spec=pltpu.PrefetchScalarGridSpec(
            num_scalar_prefetch=1, grid=(4, R // tr),
            in_specs=[pl.BlockSpec((None, None, tr, C), lambda j, i, core: (j, core[0], i, 0)), blk], out_specs=blk),
    )(core, part, landed)


def _row_tile(R, pref):
    if R <= pref:
        return R
    t = (pref // 16) * 16
    while R % t:
        t -= 16
    return t


def all_reduce_small(name, vec, after):
    def body(v_ref, _, sum_ref, land_ref, send_sems, recv_sems):
        x, y, c = _position()
        me = (x, y, c)
        flip = lambda v, bit: 1 - v if bit else v
        peers = [(flip(x, k & 4), flip(y, k & 2), flip(c, k & 1)) for k in range(1, N_DEV)]
        land_ref[_slot(me)] = v_ref[...]
        sends = [pltpu.make_async_remote_copy(src_ref=v_ref, dst_ref=land_ref.at[_slot(me)], send_sem=send_sems.at[k],
                                              recv_sem=recv_sems.at[k], device_id=peer, device_id_type=MESH)
                 for k, peer in enumerate(peers)]
        for cp in sends:
            cp.start()
        for k, peer in enumerate(peers):
            pltpu.make_async_remote_copy(src_ref=v_ref, dst_ref=land_ref.at[_slot(peer)], send_sem=send_sems.at[k],
                                         recv_sem=recv_sems.at[k], device_id=peer, device_id_type=MESH).wait_recv()
        acc = land_ref[0]
        for s in range(1, N_DEV):
            acc = acc + land_ref[s]
        sum_ref[...] = acc
        for cp in sends:
            cp.wait_send()

    vm = pl.BlockSpec(memory_space=pltpu.VMEM)
    return pl.pallas_call(
        body, name=name, in_specs=[vm, pl.BlockSpec(memory_space=pl.ANY)], out_specs=[vm, vm],
        out_shape=[_sds(vec.shape, F32), _sds((N_DEV,) + vec.shape, F32)],
        scratch_shapes=[pltpu.SemaphoreType.DMA((7,)), pltpu.SemaphoreType.DMA((7,))],
        compiler_params=pltpu.CompilerParams(has_side_effects=True),
    )(vec, after)[0]


def _adamw_math(g, w, m, v):
    m2 = ADAM_B1 * m + (1.0 - ADAM_B1) * g
    v2 = ADAM_B2 * v + (1.0 - ADAM_B2) * (g * g)
    m_hat = m2 / (1.0 - ADAM_B1 ** ADAM_STEP)
    v_hat = v2 / (1.0 - ADAM_B2 ** ADAM_STEP)
    return -ADAM_LR * (m_hat / (jnp.sqrt(v_hat) + ADAM_EPS) + ADAM_WD * w), m2, v2


def adamw_shard(name, chip_parts, landed, w, m, v, layer, prev=None, tr=256):
    L, R, C = w.shape
    tr = _row_tile(R, tr)
    n_in = 5 + (4 if prev is not None else 0)

    def body(*refs):
        own_ref, land_ref, w_ref, m_ref, v_ref = refs[1:6]
        g_ref, d_ref, nm_ref, nv_ref = refs[1 + n_in:]
        g = own_ref[...].astype(F32)
        for s in range(landed.shape[0]):
            g = g + land_ref[s].astype(F32)
        g_ref[...] = g
        d_ref[...], nm_ref[...], nv_ref[...] = _adamw_math(g, w_ref[...], m_ref[...], v_ref[...])

    lay = pl.BlockSpec((None, tr, C), lambda i, chip: (layer, i, 0))
    own = pl.BlockSpec((None, tr, C), lambda i, chip: (chip[0], i, 0))
    chip = (2 * lax.axis_index("x") + lax.axis_index("y")).astype(jnp.int32).reshape(1)
    ins = [chip_parts, landed, w, m, v] + (list(prev) if prev is not None else [])
    return pl.pallas_call(
        body, name=name, out_shape=[_sds((L, R, C), F32)] * 4,
        grid_spec=pltpu.PrefetchScalarGridSpec(
            num_scalar_prefetch=1, grid=(R // tr,),
            in_specs=[own, pl.BlockSpec((landed.shape[0], tr, C), lambda i, chip: (0, i, 0)), lay, lay, lay]
            + [pl.BlockSpec(memory_space=pl.ANY)] * (n_in - 5),
            out_specs=[lay] * 4),
        input_output_aliases={6 + i: i for i in range(n_in - 5)}, compiler_params=_cp(("parallel",)),
    )(chip, *ins)


def adamw_small(name, g, w, m, v):
    def body(g_ref, w_ref, m_ref, v_ref, d_ref, nm_ref, nv_ref):
        d_ref[...], nm_ref[...], nv_ref[...] = _adamw_math(g_ref[...], w_ref[...], m_ref[...], v_ref[...])

    return pl.pallas_call(body, name=name, out_shape=[_sds(w.shape, F32)] * 3)(g, w, m, v)


def _pack(vs):
    flat = jnp.concatenate([v.reshape(-1) for v in vs])
    return flat.reshape(-1, LANE)


def _unpack(packed, like):
    flat, out, off = packed.reshape(-1), [], 0
    for v in like:
        out.append(flat[off:off + v.size].reshape(v.shape))
        off += v.size
    return out


class MeshWeights:
    def __init__(self, w_kv, w_out, pool_w_in, pool_w_grp, hgrn_w_in, hgrn_norm_g):
        self.n_grp, self.pg_loc, self.pg = pool_w_grp.shape[1:]
        wpin, hnorm, token = all_gather_blocks("gather_first", [pool_w_in[0].astype(COMM_DTYPE), hgrn_norm_g])
        self.first = (wpin, hnorm.reshape(1, -1))
        zero = token[0, 0]
        cast = lambda a: (a + zero).astype(COMM_DTYPE)
        self.gather, self.tokens = {}, []
        for layer, blocks in ((0, [cast(pool_w_grp[0]), cast(w_kv[0]), cast(w_out[0])]),
                              (1, [cast(hgrn_w_in[0])]), (2, [cast(w_kv[1]), cast(w_out[1])])):
            me = _slot(_position())
            lands = [lax.dynamic_update_index_in_dim(lax.empty((N_DEV,) + b.shape, b.dtype), b, me, 0) for b in blocks]
            n = len(blocks)
            sems, bufs, token = split_start(f"gather{layer}_cross", blocks + lands, 4 * n, _gather_cross(n))
            self.gather[layer] = (sems, bufs)
            self.tokens.append(token[0, 0])
        self.reduce = {}

    def first_weights(self):
        return self.first[0], self.first[1], self.tokens[0] + self.tokens[1] + self.tokens[2]

    def _pass_on(self, layer, after):
        sems, bufs = self.gather[layer]
        n = len(bufs) // 2
        gathered = split_wait(f"gather{layer}_cross_wait", sems, bufs, _gather_cross(n), after)[n:]
        sems, bufs, token = split_start(f"gather{layer}_pass", gathered, 3 * n, _gather_pass(n))
        self.gather[layer] = (sems, bufs)
        return token[0, 0]

    def _gathered(self, layer, after):
        sems, bufs = self.gather[layer]
        return split_wait(f"gather{layer}_pass_wait", sems, bufs, _gather_pass(len(bufs)), after)

    def after_proj0(self, after):
        return self._pass_on(0, after)

    def layer0_weights(self, after):
        wgrp, wkv, wout = self._gathered(0, after)
        rows = lambda g: g.reshape(-1, g.shape[-1])
        return wgrp.transpose(1, 0, 2, 3).reshape(self.n_grp, self.pg, self.pg), rows(wkv), rows(wout)

    def after_out0(self, after):
        return self._pass_on(1, after)

    def layer1_weights(self, after):
        return self._gathered(1, after)[0]

    def after_proj1(self, after):
        return self._pass_on(2, after)

    def layer1_rest(self, after):
        wkv, wout = self._gathered(2, after)
        rows = lambda g: g.reshape(-1, g.shape[-1])
        return rows(wkv), rows(wout)

    def grads_ready(self, layer, grads):
        parts = {}
        for nm, g in grads.items():
            if nm == "pool_w_grp":
                g = g.reshape(self.n_grp, N_DEV, self.pg_loc, self.pg).transpose(1, 0, 2, 3)
            parts[nm] = g.reshape(4, 2, -1, g.shape[-1])
        names, srcs = list(parts), list(parts.values())
        lands = [lax.empty((4,) + p.shape[2:], p.dtype) for p in srcs]
        sems, bufs, token = split_start(f"reduce{layer}_pair", srcs + lands, 4 * len(srcs), _reduce_pair(len(srcs)))
        self.reduce[layer] = (names, sems, bufs)
        return token[0, 0]

    def grads_reduce(self, layer, after):
        names, sems, bufs = self.reduce[layer]
        n = len(names)
        bufs = split_wait(f"reduce{layer}_pair_wait", sems, bufs, _reduce_pair(n), after)
        chip_parts = [pair_add(f"pair_add_{nm}{layer}", bufs[t], bufs[n + t]) for t, nm in enumerate(names)]
        lands = [lax.empty((3,) + p.shape[1:], p.dtype) for p in chip_parts]
        sems, bufs, token = split_start(f"reduce{layer}_cross", chip_parts + lands, 3 * n, _reduce_cross(n))
        self.reduce[layer] = (names, sems, bufs)
        return token[0, 0]

    def grads_done(self, layer, after):
        names, sems, bufs = self.reduce[layer]
        n = len(names)
        bufs = split_wait(f"reduce{layer}_cross_wait", sems, bufs, _reduce_cross(n), after)
        return {nm: (bufs[t], bufs[n + t]) for t, nm in enumerate(names)}


def kernel(x, mem, norm_g, mem_norm_g, w_kv, w_out, pool_w_in, pool_w_grp, pool_scale, hgrn_w_in, hgrn_lb, hgrn_norm_g, final_g, loss_target, m_norm_g, m_mem_norm_g, m_w_kv, m_w_out, m_pool_w_in, m_pool_w_grp, m_pool_scale, m_hgrn_w_in, m_hgrn_lb, m_hgrn_norm_g, m_final_g, v_norm_g, v_mem_norm_g, v_w_kv, v_w_out, v_pool_w_in, v_pool_w_grp, v_pool_scale, v_hgrn_w_in, v_hgrn_lb, v_hgrn_norm_g, v_final_g):
    comm = MeshWeights(w_kv, w_out, pool_w_in, pool_w_grp, hgrn_w_in, hgrn_norm_g)
    loss, grad_x, small = run_step(x[0], mem[0], loss_target[0], norm_g, mem_norm_g, pool_scale, hgrn_lb, final_g, comm)
    rows = lambda a: a.reshape(a.shape[0], -1, a.shape[-1])
    state = dict(w_kv=(w_kv, m_w_kv, v_w_kv), w_out=(w_out, m_w_out, v_w_out), pool_w_in=(pool_w_in, m_pool_w_in, v_pool_w_in),
                 pool_w_grp=(pool_w_grp, m_pool_w_grp, v_pool_w_grp), hgrn_w_in=(hgrn_w_in, m_hgrn_w_in, v_hgrn_w_in))
    big = {}

    def update(group, layer, after):
        for nm, (chip_parts, landed) in comm.grads_done(group, after).items():
            w, m, v = state[nm]
            big[nm] = adamw_shard(f"adamw_{nm}{layer}", chip_parts, landed, rows(w), rows(m), rows(v),
                                  layer if w.shape[0] > 1 else 0, prev=big.get(nm))
            after = big[nm][0]
        return after

    after = update("0b", 0, update(1, 1, grad_x))
    names = ("norm_g", "mem_norm_g", "pool_scale", "hgrn_lb", "hgrn_norm_g", "final_g")
    like = (norm_g, mem_norm_g, pool_scale, hgrn_lb, _sds((1, N_DEV * hgrn_norm_g.shape[1]), F32), final_g)
    packed = _pack([small[nm].reshape(lk.shape) for nm, lk in zip(names, like)] + [jnp.full((1, LANE), loss, F32)])
    reduced = _unpack(all_reduce_small("reduce_small", packed, after), like + (_sds((1, LANE), F32),))
    loss = reduced.pop()[0, 0]
    n_loc = hgrn_norm_g.shape[1]
    reduced[4] = lax.dynamic_slice(reduced[4], (0, _slot(_position()) * n_loc), (1, n_loc))
    ws = (norm_g, mem_norm_g, pool_scale, hgrn_lb, hgrn_norm_g, final_g)
    ms = (m_norm_g, m_mem_norm_g, m_pool_scale, m_hgrn_lb, m_hgrn_norm_g, m_final_g)
    vs = (v_norm_g, v_mem_norm_g, v_pool_scale, v_hgrn_lb, v_hgrn_norm_g, v_final_g)
    res = adamw_small("adamw_small", _pack(reduced), _pack(ws), _pack(ms), _pack(vs))
    outs = {nm: [reduced[i]] + [_unpack(r, ws)[i] for r in res] for i, nm in enumerate(names)}
    update("0a", 0, res[0])
    for nm, (w, _, _) in state.items():
        outs[nm] = [r.reshape(w.shape) for r in big[nm]]
    order = ("norm_g", "mem_norm_g", "w_kv", "w_out", "pool_w_in", "pool_w_grp", "pool_scale", "hgrn_w_in", "hgrn_lb",
             "hgrn_norm_g", "final_g")
    return (loss, grad_x[None], *[outs[nm][0] for nm in order], *[outs[nm][1] for nm in order],
            *[outs[nm][2] for nm in order], *[outs[nm][3] for nm in order])
```
